```python
import jax, jax.numpy as jnp
from jax import lax
import numpy as np

D_MODEL = 1024
BATCH = 8
SEQ = 8192
DEPTH = 1

HEAD_DIM = 64
A_Q_HEADS = 8
A_KV_HEADS = 2
A_HALF_WINDOW = 128
B_HEADS = 8
B_PATTERNS = ((128, 1), (512, 4), (2048, 16))
D_FF = 2816
ROPE_THETA = 10000.0
NORM_EPS = 1e-6
FFN_RES_WEIGHT = 0.5

A_Q_W = A_Q_HEADS * HEAD_DIM
A_KV_W = A_KV_HEADS * HEAD_DIM
B_W = B_HEADS * HEAD_DIM
IN_W = A_Q_W + 2 * A_KV_W + 3 * B_W
MIX_W = A_Q_W + B_W

kernel_name = "hybrid_window_gqa_dilated_macaron_encoder"


def rms_norm(x, g):
    xf = x.astype(jnp.float32)
    y = xf * lax.rsqrt(jnp.mean(xf * xf, axis=-1, keepdims=True) + NORM_EPS)
    return (y * g.astype(jnp.float32)).astype(x.dtype)


def swiglu(h, w_gate, w_up, w_down):
    return (jax.nn.silu(h @ w_gate) * (h @ w_up)) @ w_down


def rope_tables(positions):
    inv_freq = 1.0 / (ROPE_THETA ** (jnp.arange(0, HEAD_DIM, 2, dtype=jnp.float32) / HEAD_DIM))
    ang = positions.astype(jnp.float32)[..., None] * inv_freq
    return jnp.cos(ang)[:, :, None, :], jnp.sin(ang)[:, :, None, :]


def apply_rope(t, cos, sin):
    tf = t.astype(jnp.float32)
    t1, t2 = jnp.split(tf, 2, axis=-1)
    return jnp.concatenate([t1 * cos - t2 * sin, t2 * cos + t1 * sin], axis=-1).astype(t.dtype)


def banded_attention(q, k, v, half_window, sink=None):
    blk = half_window
    B, L, Hq, Dh = q.shape
    Hkv = k.shape[2]
    G = Hq // Hkv
    nb = -(-L // blk)
    Lp = nb * blk
    pad = Lp - L
    qb = jnp.pad(q, ((0, 0), (0, pad), (0, 0), (0, 0))).astype(jnp.float32).reshape(B, nb, blk, Hkv, G, Dh)
    kp = jnp.pad(k, ((0, 0), (blk, blk + pad), (0, 0), (0, 0))).astype(jnp.float32)
    vp = jnp.pad(v, ((0, 0), (blk, blk + pad), (0, 0), (0, 0))).astype(jnp.float32)
    kw = jnp.concatenate([kp[:, j * blk:j * blk + Lp].reshape(B, nb, blk, Hkv, Dh) for j in range(3)], axis=2)
    vw = jnp.concatenate([vp[:, j * blk:j * blk + Lp].reshape(B, nb, blk, Hkv, Dh) for j in range(3)], axis=2)
    qpos = jnp.arange(Lp).reshape(nb, blk)
    kpos = jnp.arange(nb)[:, None] * blk + jnp.arange(3 * blk)[None, :] - blk
    valid = (jnp.abs(qpos[:, :, None] - kpos[:, None, :]) <= half_window) & (kpos[:, None, :] >= 0) & (kpos[:, None, :] < L)
    s = jnp.einsum('bnqhgd,bnkhd->bnhgqk', qb, kw) * (Dh ** -0.5)
    s = jnp.where(valid[None, :, None, None], s, -jnp.inf)
    m = jnp.max(s, axis=-1)
    if sink is not None:
        sk = sink.astype(jnp.float32).reshape(Hkv, G)[None, None, :, :, None]
        m = jnp.maximum(m, sk)
    p = jnp.exp(s - m[..., None])
    den = jnp.sum(p, axis=-1)
    if sink is not None:
        den = den + jnp.exp(sk - m)
    o = jnp.einsum('bnhgqk,bnkhd->bnqhgd', p, vw) / jnp.transpose(den, (0, 1, 4, 2, 3))[..., None]
    lse = jnp.transpose(m + jnp.log(den), (0, 1, 4, 2, 3)).reshape(B, Lp, Hq)[:, :L]
    return o.reshape(B, Lp, Hq, Dh)[:, :L], lse


def dilated_window_attention(q, k, v, window, dilation):
    B, S, H, Dh = q.shape
    msub = S // dilation

    def to_sub(t):
        return t.reshape(B, msub, dilation, H, Dh).transpose(0, 2, 1, 3, 4).reshape(B * dilation, msub, H, Dh)

    o, lse = banded_attention(to_sub(q), to_sub(k), to_sub(v), window // (2 * dilation))
    o = o.reshape(B, dilation, msub, H, Dh).transpose(0, 2, 1, 3, 4).reshape(B, S, H, Dh)
    lse = lse.reshape(B, dilation, msub, H).transpose(0, 2, 1, 3).reshape(B, S, H)
    return o, lse


def mixer(h, w_in, a_sink, w_out, cos, sin):
    B, S, _ = h.shape
    proj = h @ w_in
    cuts = np.cumsum([A_Q_W, A_KV_W, A_KV_W, B_W, B_W]).tolist()
    aq, ak, av, bq, bk, bv = jnp.split(proj, cuts, axis=-1)
    aq = apply_rope(aq.reshape(B, S, A_Q_HEADS, HEAD_DIM), cos, sin)
    ak = apply_rope(ak.reshape(B, S, A_KV_HEADS, HEAD_DIM), cos, sin)
    av = av.reshape(B, S, A_KV_HEADS, HEAD_DIM)
    bq = apply_rope(bq.reshape(B, S, B_HEADS, HEAD_DIM), cos, sin)
    bk = apply_rope(bk.reshape(B, S, B_HEADS, HEAD_DIM), cos, sin)
    bv = bv.reshape(B, S, B_HEADS, HEAD_DIM)
    a_out, _ = banded_attention(aq, ak, av, A_HALF_WINDOW, sink=a_sink)
    outs, lses = [], []
    for w, d in B_PATTERNS:
        o, l = dilated_window_attention(bq, bk, bv, w, d)
        outs.append(o)
        lses.append(l)
    wts = jax.nn.softmax(jnp.stack(lses, axis=0), axis=0)
    b_out = jnp.sum(wts[..., None] * jnp.stack(outs, axis=0), axis=0)
    cat = jnp.concatenate([a_out.reshape(B, S, A_Q_W), b_out.reshape(B, S, B_W)], axis=-1).astype(h.dtype)
    return cat @ w_out


def _fwd_setup_inputs(seed: int = 0) -> dict:
    key = jax.random.key(seed)
    ks = jax.random.split(key, 16)
    f32 = jnp.float32
    nrm = lambda k, shape, scale: jax.random.normal(k, shape, f32) * scale
    gain = lambda k: 1.0 + 0.02 * jax.random.normal(k, (DEPTH, D_MODEL), f32)
    x = jax.random.normal(ks[0], (BATCH, SEQ, D_MODEL), f32)
    offsets = jax.random.randint(ks[1], (BATCH, 1), 0, 4096, dtype=jnp.int32)
    positions = (jnp.arange(SEQ, dtype=jnp.int32)[None, :] + offsets).astype(jnp.int32)
    return {
        "x": x,
        "positions": positions,
        "norm_ffn1": gain(ks[2]),
        "w_gate1": nrm(ks[3], (DEPTH, D_MODEL, D_FF), D_MODEL ** -0.5),
        "w_up1": nrm(ks[4], (DEPTH, D_MODEL, D_FF), D_MODEL ** -0.5),
        "w_down1": nrm(ks[5], (DEPTH, D_FF, D_MODEL), D_FF ** -0.5),
        "norm_mix": gain(ks[6]),
        "w_in": nrm(ks[7], (DEPTH, D_MODEL, IN_W), D_MODEL ** -0.5),
        "a_sink": nrm(ks[8], (DEPTH, A_Q_HEADS), 0.5),
        "w_out": nrm(ks[9], (DEPTH, MIX_W, D_MODEL), MIX_W ** -0.5),
        "norm_ffn2": gain(ks[10]),
        "w_gate2": nrm(ks[11], (DEPTH, D_MODEL, D_FF), D_MODEL ** -0.5),
        "w_up2": nrm(ks[12], (DEPTH, D_MODEL, D_FF), D_MODEL ** -0.5),
        "w_down2": nrm(ks[13], (DEPTH, D_FF, D_MODEL), D_FF ** -0.5),
        "norm_final": 1.0 + 0.02 * jax.random.normal(ks[14], (D_MODEL,), f32),
    }


def _fwd_reference(x, positions, norm_ffn1, w_gate1, w_up1, w_down1, norm_mix, w_in, a_sink, w_out,
              norm_ffn2, w_gate2, w_up2, w_down2, norm_final):
    cos, sin = rope_tables(positions)
    for l in range(DEPTH):
        x = x + FFN_RES_WEIGHT * swiglu(rms_norm(x, norm_ffn1[l]), w_gate1[l], w_up1[l], w_down1[l])
        x = x + mixer(rms_norm(x, norm_mix[l]), w_in[l], a_sink[l], w_out[l], cos, sin)
        x = x + FFN_RES_WEIGHT * swiglu(rms_norm(x, norm_ffn2[l]), w_gate2[l], w_up2[l], w_down2[l])
    return rms_norm(x, norm_final)


import jax as _jax
import jax.numpy as _jnp

TWIN_FORMAT = 'train_step'
FWD_PARAMS = ['x', 'positions', 'norm_ffn1', 'w_gate1', 'w_up1', 'w_down1', 'norm_mix', 'w_in', 'a_sink', 'w_out', 'norm_ffn2', 'w_gate2', 'w_up2', 'w_down2', 'norm_final']
TWIN_WEIGHTS = ['norm_ffn1', 'w_gate1', 'w_up1', 'w_down1', 'norm_mix', 'w_in', 'a_sink', 'w_out', 'norm_ffn2', 'w_gate2', 'w_up2', 'w_down2', 'norm_final']
TWIN_DIFF_INPUT = 'x'
TWIN_INPUTS = ['x', 'positions', 'norm_ffn1', 'w_gate1', 'w_up1', 'w_down1', 'norm_mix', 'w_in', 'a_sink', 'w_out', 'norm_ffn2', 'w_gate2', 'w_up2', 'w_down2', 'norm_final', 'loss_target', 'm_norm_ffn1', 'm_w_gate1', 'm_w_up1', 'm_w_down1', 'm_norm_mix', 'm_w_in', 'm_a_sink', 'm_w_out', 'm_norm_ffn2', 'm_w_gate2', 'm_w_up2', 'm_w_down2', 'm_norm_final', 'v_norm_ffn1', 'v_w_gate1', 'v_w_up1', 'v_w_down1', 'v_norm_mix', 'v_w_in', 'v_a_sink', 'v_w_out', 'v_norm_ffn2', 'v_w_gate2', 'v_w_up2', 'v_w_down2', 'v_norm_final']
TWIN_OUTPUTS = ['loss', 'grad_x', 'grad_norm_ffn1', 'grad_w_gate1', 'grad_w_up1', 'grad_w_down1', 'grad_norm_mix', 'grad_w_in', 'grad_a_sink', 'grad_w_out', 'grad_norm_ffn2', 'grad_w_gate2', 'grad_w_up2', 'grad_w_down2', 'grad_norm_final', 'delta_norm_ffn1', 'delta_w_gate1', 'delta_w_up1', 'delta_w_down1', 'delta_norm_mix', 'delta_w_in', 'delta_a_sink', 'delta_w_out', 'delta_norm_ffn2', 'delta_w_gate2', 'delta_w_up2', 'delta_w_down2', 'delta_norm_final', 'new_m_norm_ffn1', 'new_m_w_gate1', 'new_m_w_up1', 'new_m_w_down1', 'new_m_norm_mix', 'new_m_w_in', 'new_m_a_sink', 'new_m_w_out', 'new_m_norm_ffn2', 'new_m_w_gate2', 'new_m_w_up2', 'new_m_w_down2', 'new_m_norm_final', 'new_v_norm_ffn1', 'new_v_w_gate1', 'new_v_w_up1', 'new_v_w_down1', 'new_v_norm_mix', 'new_v_w_in', 'new_v_a_sink', 'new_v_w_out', 'new_v_norm_ffn2', 'new_v_w_gate2', 'new_v_w_up2', 'new_v_w_down2', 'new_v_norm_final']
TWIN_LEAF_KINDS = {'loss': 'loss', 'grad_x': 'grad_x', 'grad_norm_ffn1': 'grad_w', 'grad_w_gate1': 'grad_w', 'grad_w_up1': 'grad_w', 'grad_w_down1': 'grad_w', 'grad_norm_mix': 'grad_w', 'grad_w_in': 'grad_w', 'grad_a_sink': 'grad_w', 'grad_w_out': 'grad_w', 'grad_norm_ffn2': 'grad_w', 'grad_w_gate2': 'grad_w', 'grad_w_up2': 'grad_w', 'grad_w_down2': 'grad_w', 'grad_norm_final': 'grad_w', 'delta_norm_ffn1': 'delta_w', 'delta_w_gate1': 'delta_w', 'delta_w_up1': 'delta_w', 'delta_w_down1': 'delta_w', 'delta_norm_mix': 'delta_w', 'delta_w_in': 'delta_w', 'delta_a_sink': 'delta_w', 'delta_w_out': 'delta_w', 'delta_norm_ffn2': 'delta_w', 'delta_w_gate2': 'delta_w', 'delta_w_up2': 'delta_w', 'delta_w_down2': 'delta_w', 'delta_norm_final': 'delta_w', 'new_m_norm_ffn1': 'new_m', 'new_m_w_gate1': 'new_m', 'new_m_w_up1': 'new_m', 'new_m_w_down1': 'new_m', 'new_m_norm_mix': 'new_m', 'new_m_w_in': 'new_m', 'new_m_a_sink': 'new_m', 'new_m_w_out': 'new_m', 'new_m_norm_ffn2': 'new_m', 'new_m_w_gate2': 'new_m', 'new_m_w_up2': 'new_m', 'new_m_w_down2': 'new_m', 'new_m_norm_final': 'new_m', 'new_v_norm_ffn1': 'new_v', 'new_v_w_gate1': 'new_v', 'new_v_w_up1': 'new_v', 'new_v_w_down1': 'new_v', 'new_v_norm_mix': 'new_v', 'new_v_w_in': 'new_v', 'new_v_a_sink': 'new_v', 'new_v_w_out': 'new_v', 'new_v_norm_ffn2': 'new_v', 'new_v_w_gate2': 'new_v', 'new_v_w_up2': 'new_v', 'new_v_w_down2': 'new_v', 'new_v_norm_final': 'new_v'}


def _forward(args):
    return _fwd_reference(*[args[k] for k in FWD_PARAMS])


def _output_shape():
    def fwd():
        inp = _fwd_setup_inputs(0)
        return _fwd_reference(*[inp[k] for k in FWD_PARAMS])
    out = _jax.eval_shape(fwd)
    return out.shape, out.dtype

N_MICROBATCH = 1
ADAM_LR = 0.001
ADAM_B1 = 0.9
ADAM_B2 = 0.999
ADAM_EPS = 1e-08
ADAM_WD = 0.01
ADAM_STEP = 10
PER_EXAMPLE_BATCH_AXIS = {'x': 0, 'positions': 0, 'loss_target': 0}
SHARED_INPUTS = []
_WEIGHT_DTYPES = {'norm_ffn1': _jnp.float32, 'w_gate1': _jnp.float32, 'w_up1': _jnp.float32, 'w_down1': _jnp.float32, 'norm_mix': _jnp.float32, 'w_in': _jnp.float32, 'a_sink': _jnp.float32, 'w_out': _jnp.float32, 'norm_ffn2': _jnp.float32, 'w_gate2': _jnp.float32, 'w_up2': _jnp.float32, 'w_down2': _jnp.float32, 'norm_final': _jnp.float32}
MOMENT_SCALE = {'norm_ffn1': 1.126808e-01, 'w_gate1': 4.695442e-02, 'w_up1': 4.555027e-02, 'w_down1': 7.546195e-02, 'norm_mix': 5.853595e-02, 'w_in': 3.752140e-02, 'a_sink': 5.495854e-04, 'w_out': 3.338404e-02, 'norm_ffn2': 9.925536e-02, 'w_gate2': 4.278787e-02, 'w_up2': 4.154136e-02, 'w_down2': 6.895557e-02, 'norm_final': 6.395962e+01}


def _to_microbatches(a, axis):
    t = _jnp.moveaxis(a, axis, 0)
    t = t.reshape((N_MICROBATCH, t.shape[0] // N_MICROBATCH) + t.shape[1:])
    return _jnp.moveaxis(t, 1, axis + 1)


def setup_inputs(seed: int = 0) -> dict:
    inp = _fwd_setup_inputs(seed)
    key = _jax.random.fold_in(_jax.random.key(seed), 7919)
    shape, _ = _output_shape()
    out = dict(inp)
    out["loss_target"] = _jax.random.normal(_jax.random.fold_in(key, 0), shape, _jnp.float32)
    for i, name in enumerate(TWIN_WEIGHTS):
        w = inp[name].astype(_jnp.float32)
        if MOMENT_SCALE is None:
            s = _jnp.sqrt(_jnp.mean(_jnp.square(w)) + 1e-30)
        else:
            s = MOMENT_SCALE[name]
        km, kv = _jax.random.split(_jax.random.fold_in(key, i + 1))
        out[name] = w
        out["m_" + name] = s * _jax.random.normal(km, w.shape, _jnp.float32)
        out["v_" + name] = (s * s) * _jax.random.uniform(kv, w.shape, _jnp.float32, 0.5, 1.5)
    if N_MICROBATCH > 1:
        for name, axis in PER_EXAMPLE_BATCH_AXIS.items():
            out[name] = _to_microbatches(out[name], axis)
    return {'x': out['x'], 'positions': out['positions'], 'norm_ffn1': out['norm_ffn1'], 'w_gate1': out['w_gate1'], 'w_up1': out['w_up1'], 'w_down1': out['w_down1'], 'norm_mix': out['norm_mix'], 'w_in': out['w_in'], 'a_sink': out['a_sink'], 'w_out': out['w_out'], 'norm_ffn2': out['norm_ffn2'], 'w_gate2': out['w_gate2'], 'w_up2': out['w_up2'], 'w_down2': out['w_down2'], 'norm_final': out['norm_final'], 'loss_target': out['loss_target'], 'm_norm_ffn1': out['m_norm_ffn1'], 'm_w_gate1': out['m_w_gate1'], 'm_w_up1': out['m_w_up1'], 'm_w_down1': out['m_w_down1'], 'm_norm_mix': out['m_norm_mix'], 'm_w_in': out['m_w_in'], 'm_a_sink': out['m_a_sink'], 'm_w_out': out['m_w_out'], 'm_norm_ffn2': out['m_norm_ffn2'], 'm_w_gate2': out['m_w_gate2'], 'm_w_up2': out['m_w_up2'], 'm_w_down2': out['m_w_down2'], 'm_norm_final': out['m_norm_final'], 'v_norm_ffn1': out['v_norm_ffn1'], 'v_w_gate1': out['v_w_gate1'], 'v_w_up1': out['v_w_up1'], 'v_w_down1': out['v_w_down1'], 'v_norm_mix': out['v_norm_mix'], 'v_w_in': out['v_w_in'], 'v_a_sink': out['v_a_sink'], 'v_w_out': out['v_w_out'], 'v_norm_ffn2': out['v_norm_ffn2'], 'v_w_gate2': out['v_w_gate2'], 'v_w_up2': out['v_w_up2'], 'v_w_down2': out['v_w_down2'], 'v_norm_final': out['v_norm_final']}


def _loss(weights, diff, rest, loss_target):
    with _jax.named_scope("forward"):
        args = {**rest, TWIN_DIFF_INPUT: diff, **{k: w.astype(_WEIGHT_DTYPES[k]) for k, w in weights.items()}}
        y = _forward(args)
    with _jax.named_scope("loss_head"):
        err = _jnp.square(y.astype(_jnp.float32) - loss_target)
        return 0.5 * _jnp.sum(_jnp.mean(err, axis=-1)) if err.ndim else 0.5 * err


def _adamw(w, g, m, v):
    m = ADAM_B1 * m + (1.0 - ADAM_B1) * g
    v = ADAM_B2 * v + (1.0 - ADAM_B2) * _jnp.square(g)
    m_hat = m / (1.0 - ADAM_B1 ** ADAM_STEP)
    v_hat = v / (1.0 - ADAM_B2 ** ADAM_STEP)
    delta = -ADAM_LR * (m_hat / (_jnp.sqrt(v_hat) + ADAM_EPS) + ADAM_WD * w)
    return delta, m, v


def reference(x, positions, norm_ffn1, w_gate1, w_up1, w_down1, norm_mix, w_in, a_sink, w_out, norm_ffn2, w_gate2, w_up2, w_down2, norm_final, loss_target, m_norm_ffn1, m_w_gate1, m_w_up1, m_w_down1, m_norm_mix, m_w_in, m_a_sink, m_w_out, m_norm_ffn2, m_w_gate2, m_w_up2, m_w_down2, m_norm_final, v_norm_ffn1, v_w_gate1, v_w_up1, v_w_down1, v_norm_mix, v_w_in, v_a_sink, v_w_out, v_norm_ffn2, v_w_gate2, v_w_up2, v_w_down2, v_norm_final):
    given = dict(x=x, positions=positions, norm_ffn1=norm_ffn1, w_gate1=w_gate1, w_up1=w_up1, w_down1=w_down1, norm_mix=norm_mix, w_in=w_in, a_sink=a_sink, w_out=w_out, norm_ffn2=norm_ffn2, w_gate2=w_gate2, w_up2=w_up2, w_down2=w_down2, norm_final=norm_final, loss_target=loss_target, m_norm_ffn1=m_norm_ffn1, m_w_gate1=m_w_gate1, m_w_up1=m_w_up1, m_w_down1=m_w_down1, m_norm_mix=m_norm_mix, m_w_in=m_w_in, m_a_sink=m_a_sink, m_w_out=m_w_out, m_norm_ffn2=m_norm_ffn2, m_w_gate2=m_w_gate2, m_w_up2=m_w_up2, m_w_down2=m_w_down2, m_norm_final=m_norm_final, v_norm_ffn1=v_norm_ffn1, v_w_gate1=v_w_gate1, v_w_up1=v_w_up1, v_w_down1=v_w_down1, v_norm_mix=v_norm_mix, v_w_in=v_w_in, v_a_sink=v_a_sink, v_w_out=v_w_out, v_norm_ffn2=v_norm_ffn2, v_w_gate2=v_w_gate2, v_w_up2=v_w_up2, v_w_down2=v_w_down2, v_norm_final=v_norm_final)
    weights = {n: given[n] for n in TWIN_WEIGHTS}
    shared = {n: given[n] for n in SHARED_INPUTS}
    per_example = {n: given[n] for n in ['x', 'positions']}
    grad_fn = _jax.value_and_grad(_loss, argnums=(0, 1))

    def one_microbatch(ex, loss_target):
        ex = dict(ex)
        diff = ex.pop(TWIN_DIFF_INPUT)
        return grad_fn(weights, diff, {**shared, **ex}, loss_target)

    if N_MICROBATCH == 1:
        loss, (grad_w, grad_x) = one_microbatch(per_example, given["loss_target"])
    else:
        def body(carry, xs):
            loss_sum, grad_sum = carry
            l_k, (gw_k, gx_k) = one_microbatch(xs[0], xs[1])
            with _jax.named_scope("update"):
                return (loss_sum + l_k, _jax.tree.map(_jnp.add, grad_sum, gw_k)), gx_k

        init = (_jnp.zeros((), _jnp.float32), _jax.tree.map(_jnp.zeros_like, weights))
        (loss, grad_w), grad_x = _jax.lax.scan(body, init, (per_example, given["loss_target"]))
    with _jax.named_scope("update"):
        delta_w, new_m, new_v = {}, {}, {}
        for n in TWIN_WEIGHTS:
            delta_w[n], new_m[n], new_v[n] = _adamw(weights[n], grad_w[n], given["m_" + n], given["v_" + n])
    return (loss, grad_x, *[grad_w[n] for n in TWIN_WEIGHTS], *[delta_w[n] for n in TWIN_WEIGHTS],
            *[new_m[n] for n in TWIN_WEIGHTS], *[new_v[n] for n in TWIN_WEIGHTS])
```

```python
import jax
import jax.numpy as jnp
from jax import lax
from jax.experimental import pallas as pl
from jax.experimental.pallas import tpu as pltpu

F32 = jnp.float32
BF16 = jnp.bfloat16

HEAD_DIM = 64
LANES = 128
SUBLANES = 8
A_Q_W, A_KV_W, B_W = 512, 128, 512
A_HALF_WINDOW = 128
B_PATTERNS = ((128, 1), (512, 4), (2048, 16))
ROPE_THETA = 10000.0
NORM_EPS = 1e-6
FFN_RES_WEIGHT = 0.5
ADAM_LR, ADAM_B1, ADAM_B2, ADAM_EPS, ADAM_WD, ADAM_STEP = 0.001, 0.9, 0.999, 1e-08, 0.01, 10
N_CHIPS = 4
N_DEV = 8
QB = 128
NEG = -1e30
VMEM_LIMIT = 56 * 1024 * 1024
MESH = pl.DeviceIdType.MESH
ANY = pl.BlockSpec(memory_space=pl.ANY)


def _params(sem=None):
    return pltpu.CompilerParams(dimension_semantics=sem, vmem_limit_bytes=VMEM_LIMIT)


def _sds(shape, dtype):
    return jax.ShapeDtypeStruct(tuple(shape), dtype)


def _dot(a, b):
    return jnp.dot(a, b, preferred_element_type=F32)


def _dot_nt(a, b):
    return lax.dot_general(a, b, (((1,), (1,)), ((), ())), preferred_element_type=F32)


def _dot_tn(a, b):
    return lax.dot_general(a, b, (((0,), (0,)), ((), ())), preferred_element_type=F32)


def _rms_stats(x):
    r = lax.rsqrt(jnp.mean(x * x, axis=-1, keepdims=True) + NORM_EPS)
    return x * r, r


def _rms_bwd(dh, x, g):
    xhat, r = _rms_stats(x)
    dxn = dh * g
    dx = r * (dxn - xhat * jnp.mean(dxn * xhat, axis=-1, keepdims=True))
    tm, d = x.shape
    dg = (dh * xhat).reshape(tm // SUBLANES, SUBLANES, d).sum(axis=0)
    return dx, dg


def _sigmoid(x):
    return 1.0 / (1.0 + jnp.exp(-x))


def _swap32(t):
    n = t.shape[-1]
    lane = lax.broadcasted_iota(jnp.int32, t.shape, t.ndim - 1)
    return jnp.where((lane % HEAD_DIM) < HEAD_DIM // 2, pltpu.roll(t, n - HEAD_DIM // 2, axis=t.ndim - 1),
                     pltpu.roll(t, HEAD_DIM // 2, axis=t.ndim - 1))


def _cast_all(ws):
    n = len(ws)

    def body(*refs):
        for i in range(n):
            refs[n + i][...] = refs[i][...].astype(BF16)

    return pl.pallas_call(body, name="cast_shards", out_shape=[_sds(w.shape, BF16) for w in ws],
                          compiler_params=_params())(*ws)


def _mesh_pos():
    return lax.axis_index("x"), lax.axis_index("y"), lax.axis_index("c")


def _chip_peers(x, y, c):
    return [((1 - x, y, c), 2 * (1 - x) + y), ((x, 1 - y, c), 2 * x + (1 - y)), ((1 - x, 1 - y, c), 2 * (1 - x) + (1 - y))]


def _gather_weights(ws):
    n = len(ws)

    def body(*refs):
        ins, outs = refs[:n], refs[n:2 * n]
        ici_send, ici_recv, d2d_send, d2d_recv, loc = refs[2 * n:]
        x, y, c = _mesh_pos()
        me = 2 * x + y
        sibling = (x, y, 1 - c)
        peers = _chip_peers(x, y, c)

        def half(k, who):
            r2 = ws[k].shape[0] // 2
            return pl.ds(pl.multiple_of(who * r2, 16), r2)

        local = [pltpu.make_async_copy(ins[k], outs[k].at[me], loc.at[k]) for k in range(n)]
        for cp in local:
            cp.start()
        first = []
        for k in range(n):
            for rel, (dev, _) in enumerate(peers):
                cp = pltpu.make_async_remote_copy(src_ref=ins[k].at[half(k, c), :], dst_ref=outs[k].at[me, half(k, c), :],
                                                  send_sem=ici_send.at[k * 3 + rel], recv_sem=ici_recv.at[k * 3 + rel],
                                                  device_id=dev, device_id_type=MESH)
                cp.start()
                first.append(cp)
        passed = []
        for k in range(n):
            for rel, (dev, chip) in enumerate(peers):
                blk = outs[k].at[chip, half(k, c), :]
                pltpu.make_async_remote_copy(src_ref=blk, dst_ref=blk, send_sem=ici_send.at[k * 3 + rel], recv_sem=ici_recv.at[k * 3 + rel],
                                             device_id=dev, device_id_type=MESH).wait_recv()
                cp = pltpu.make_async_remote_copy(src_ref=blk, dst_ref=blk, send_sem=d2d_send.at[k * 3 + rel], recv_sem=d2d_recv.at[k * 3 + rel],
                                                  device_id=sibling, device_id_type=MESH)
                cp.start()
                passed.append(cp)
        for k in range(n):
            for rel, (dev, chip) in enumerate(peers):
                blk = outs[k].at[chip, half(k, 1 - c), :]
                pltpu.make_async_remote_copy(src_ref=blk, dst_ref=blk, send_sem=d2d_send.at[k * 3 + rel], recv_sem=d2d_recv.at[k * 3 + rel],
                                             device_id=sibling, device_id_type=MESH).wait_recv()
        for cp in first + passed:
            cp.wait_send()
        for cp in local:
            cp.wait()

    return pl.pallas_call(
        body, name="gather_weights", out_shape=[_sds((N_CHIPS,) + w.shape, BF16) for w in ws],
        in_specs=[ANY] * n, out_specs=[ANY] * n,
        scratch_shapes=[pltpu.SemaphoreType.DMA((n * 3,)), pltpu.SemaphoreType.DMA((n * 3,)), pltpu.SemaphoreType.DMA((n * 3,)),
                        pltpu.SemaphoreType.DMA((n * 3,)), pltpu.SemaphoreType.DMA((n,))],
        compiler_params=_params())(*ws)


def _ffn_fwd(x, g, wg, wu, wd, name, tm=512):
    T, D = x.shape
    ns, _, fs = wg.shape

    def body(x_ref, g_ref, wg_ref, wu_ref, wd_ref, xo_ref, h_ref, gate_ref, up_ref, act_ref, h_scr, acc):
        j = pl.program_id(1)

        @pl.when(j == 0)
        def _():
            xhat, _ = _rms_stats(x_ref[...])
            hb = (xhat * g_ref[...]).astype(BF16)
            h_scr[...] = hb
            h_ref[...] = hb
            acc[...] = jnp.zeros_like(acc)

        h = h_scr[...]
        gate = _dot(h, wg_ref[...])
        up = _dot(h, wu_ref[...])
        act = ((gate * _sigmoid(gate)) * up).astype(BF16)
        gate_ref[...] = gate.astype(BF16)
        up_ref[...] = up.astype(BF16)
        act_ref[...] = act
        acc[...] += _dot(act, wd_ref[...])

        @pl.when(j == ns - 1)
        def _():
            xo_ref[...] = x_ref[...] + FFN_RES_WEIGHT * acc[...]

    row = pl.BlockSpec((tm, D), lambda i, j: (i, 0))
    col_w = pl.BlockSpec((None, D, fs), lambda i, j: (j, 0, 0))
    saved = pl.BlockSpec((None, tm, fs), lambda i, j: (j, i, 0))
    return pl.pallas_call(
        body, name=name, grid=(T // tm, ns),
        in_specs=[row, pl.BlockSpec((1, D), lambda i, j: (0, 0)), col_w, col_w, pl.BlockSpec((None, fs, D), lambda i, j: (j, 0, 0))],
        out_specs=[row, row, saved, saved, saved],
        out_shape=[_sds((T, D), F32), _sds((T, D), BF16), _sds((ns, T, fs), BF16), _sds((ns, T, fs), BF16), _sds((ns, T, fs), BF16)],
        scratch_shapes=[pltpu.VMEM((tm, D), BF16), pltpu.VMEM((tm, D), F32)],
        compiler_params=_params(("parallel", "arbitrary")))(x, g, wg, wu, wd)


def _ffn_dx(dxo, x, g, gate_s, up_s, wg, wu, wd, name, tm=512):
    T, D = x.shape
    ns, _, fs = wg.shape

    def body(dxo_ref, x_ref, g_ref, gate_ref, up_ref, wg_ref, wu_ref, wd_ref, dx_ref, dff_ref, dgate_ref, dup_ref, dg_ref,
             dff_scr, dh_acc):
        i, j = pl.program_id(0), pl.program_id(1)

        @pl.when((i == 0) & (j == 0))
        def _():
            dg_ref[...] = jnp.zeros_like(dg_ref)

        @pl.when(j == 0)
        def _():
            d = (FFN_RES_WEIGHT * dxo_ref[...]).astype(BF16)
            dff_scr[...] = d
            dff_ref[...] = d
            dh_acc[...] = jnp.zeros_like(dh_acc)

        da = _dot_nt(dff_scr[...], wd_ref[...])
        gate = gate_ref[...].astype(F32)
        up = up_ref[...].astype(F32)
        s = _sigmoid(gate)
        silu = gate * s
        dup = (da * silu).astype(BF16)
        dgate = (da * up * (s * (1.0 + gate * (1.0 - s)))).astype(BF16)
        dgate_ref[...] = dgate
        dup_ref[...] = dup
        dh_acc[...] += _dot_nt(dgate, wg_ref[...]) + _dot_nt(dup, wu_ref[...])

        @pl.when(j == ns - 1)
        def _():
            dxn, dg = _rms_bwd(dh_acc[...], x_ref[...], g_ref[...])
            dg_ref[...] += dg
            dx_ref[...] = dxo_ref[...] + dxn

    row = pl.BlockSpec((tm, D), lambda i, j: (i, 0))
    col_w = pl.BlockSpec((None, D, fs), lambda i, j: (j, 0, 0))
    saved = pl.BlockSpec((None, tm, fs), lambda i, j: (j, i, 0))
    return pl.pallas_call(
        body, name=name, grid=(T // tm, ns),
        in_specs=[row, row, pl.BlockSpec((1, D), lambda i, j: (0, 0)), saved, saved, col_w, col_w,
                  pl.BlockSpec((None, fs, D), lambda i, j: (j, 0, 0))],
        out_specs=[row, row, saved, saved, pl.BlockSpec((SUBLANES, D), lambda i, j: (0, 0))],
        out_shape=[_sds((T, D), F32), _sds((T, D), BF16), _sds((ns, T, fs), BF16), _sds((ns, T, fs), BF16), _sds((SUBLANES, D), F32)],
        scratch_shapes=[pltpu.VMEM((tm, D), BF16), pltpu.VMEM((tm, D), F32)],
        compiler_params=_params(("arbitrary", "arbitrary")))(dxo, x, g, gate_s, up_s, wg, wu, wd)


def _tn(a, bs, mode, name, tk=1024, nb=None):
    nbs = len(bs)
    if mode == "shard_b":
        T, M = a.shape
        G, _, N = bs[0].shape
        a_spec = pl.BlockSpec((tk, M), lambda g, t: (t, 0))
        b_spec = pl.BlockSpec((None, tk, N), lambda g, t: (g, t, 0))
        o_spec, o_shape = pl.BlockSpec((None, M, N), lambda g, t: (g, 0, 0)), (G, M, N)
    elif mode == "shard_a":
        G, T, M = a.shape
        N = bs[0].shape[1]
        a_spec = pl.BlockSpec((None, tk, M), lambda g, t: (g, t, 0))
        b_spec = pl.BlockSpec((tk, N), lambda g, t: (t, 0))
        o_spec, o_shape = pl.BlockSpec((None, M, N), lambda g, t: (g, 0, 0)), (G, M, N)
    else:
        T, M = a.shape
        N = nb
        G = bs[0].shape[1] // nb
        a_spec = pl.BlockSpec((tk, M), lambda g, t: (t, 0))
        b_spec = pl.BlockSpec((tk, N), lambda g, t: (t, g))
        o_spec, o_shape = pl.BlockSpec((M, N), lambda g, t: (0, g)), (M, G * nb)

    def body(a_ref, *refs):
        b_refs, o_refs = refs[:nbs], refs[nbs:]
        av = a_ref[...].astype(BF16)
        for b_ref, o_ref in zip(b_refs, o_refs):
            @pl.when(pl.program_id(1) == 0)
            def _():
                o_ref[...] = jnp.zeros_like(o_ref)

            o_ref[...] += _dot_tn(av, b_ref[...].astype(BF16))

    return pl.pallas_call(
        body, name=name, grid=(G, T // tk), in_specs=[a_spec] + [b_spec] * nbs, out_specs=[o_spec] * nbs,
        out_shape=[_sds(o_shape, F32)] * nbs, compiler_params=_params(("parallel", "arbitrary")))(a, *bs)


def _rope_tables(pos_col, inv_freq):
    T = pos_col.shape[0]

    def body(p_ref, f_ref, c_ref, s_ref):
        ang = p_ref[...].astype(F32) * f_ref[...]
        lane = lax.broadcasted_iota(jnp.int32, ang.shape, 1)
        c_ref[...] = jnp.cos(ang)
        sn = jnp.sin(ang)
        s_ref[...] = jnp.where((lane % HEAD_DIM) < HEAD_DIM // 2, -sn, sn)

    tm = 1024
    return pl.pallas_call(
        body, name="rope_tables", grid=(T // tm,),
        in_specs=[pl.BlockSpec((tm, 1), lambda i: (i, 0)), pl.BlockSpec((1, LANES), lambda i: (0, 0))],
        out_specs=[pl.BlockSpec((tm, LANES), lambda i: (i, 0))] * 2,
        out_shape=[_sds((T, LANES), F32)] * 2, compiler_params=_params(("parallel",)))(pos_col, inv_freq)


def _deinterleave(scr, out_ref, d, tm, nblk):
    for r in range(d):
        for cb in range(nblk):
            out_ref[r, :, cb * LANES:(cb + 1) * LANES] = scr[cb, pl.ds(r, tm // d, stride=d), :].astype(out_ref.dtype)


def _interleave(in_ref, scr, d, tm, nblk):
    for r in range(d):
        for cb in range(nblk):
            scr[cb, pl.ds(r, tm // d, stride=d), :] = in_ref[r, :, cb * LANES:(cb + 1) * LANES].astype(F32)


def _proj_rope(x, g, w_in, cos, sin, tm=512):
    T, D = x.shape
    dils = [d for _, d in B_PATTERNS if d > 1]
    nbb = B_W // LANES
    scale = HEAD_DIM ** -0.5
    cuts = [0, A_Q_W, A_Q_W + A_KV_W, A_Q_W + 2 * A_KV_W, A_Q_W + 2 * A_KV_W + B_W, A_Q_W + 2 * A_KV_W + 2 * B_W,
            A_Q_W + 2 * A_KV_W + 3 * B_W]

    def body(x_ref, g_ref, w_ref, c_ref, s_ref, h_ref, aq_ref, ak_ref, av_ref, *rest):
        b_refs, scr = rest[:-1], rest[-1]
        xhat, _ = _rms_stats(x_ref[...])
        h = (xhat * g_ref[...]).astype(BF16)
        h_ref[...] = h
        cs, sn = c_ref[...], s_ref[...]

        def seg(idx, rope, mult):
            lo, hi = cuts[idx], cuts[idx + 1]
            blocks = []
            for cb in range((hi - lo) // LANES):
                p = _dot(h, w_ref[:, lo + cb * LANES:lo + (cb + 1) * LANES])
                if rope:
                    p = p * cs + _swap32(p) * sn
                if mult != 1.0:
                    p = p * mult
                blocks.append(p)
            return blocks

        for idx, ref, rope, mult in ((0, aq_ref, True, scale), (1, ak_ref, True, 1.0), (2, av_ref, False, 1.0)):
            for cb, p in enumerate(seg(idx, rope, mult)):
                ref[:, cb * LANES:(cb + 1) * LANES] = p.astype(BF16)
        for which, (idx, rope, mult) in enumerate(((3, True, scale), (4, True, 1.0), (5, False, 1.0))):
            for cb, p in enumerate(seg(idx, rope, mult)):
                b_refs[which][:, cb * LANES:(cb + 1) * LANES] = p.astype(BF16)
                scr[cb] = p
            for di, d in enumerate(dils):
                _deinterleave(scr, b_refs[3 * (di + 1) + which], d, tm, nbb)

    row = lambda w: pl.BlockSpec((tm, w), lambda i: (i, 0))
    out_specs = [row(D), row(A_Q_W), row(A_KV_W), row(A_KV_W)] + [row(B_W)] * 3
    out_shape = [_sds((T, D), BF16), _sds((T, A_Q_W), BF16), _sds((T, A_KV_W), BF16), _sds((T, A_KV_W), BF16)] + [_sds((T, B_W), BF16)] * 3
    for d in dils:
        out_specs += [pl.BlockSpec((d, tm // d, B_W), lambda i: (0, i, 0))] * 3
        out_shape += [_sds((d, T // d, B_W), BF16)] * 3
    return pl.pallas_call(
        body, name="proj_rope", grid=(T // tm,),
        in_specs=[row(D), pl.BlockSpec((1, D), lambda i: (0, 0)), pl.BlockSpec(w_in.shape, lambda i: (0, 0)), row(LANES), row(LANES)],
        out_specs=out_specs, out_shape=out_shape, scratch_shapes=[pltpu.VMEM((nbb, tm, LANES), F32)],
        compiler_params=_params(("parallel",)))(x, g, w_in, cos, sin)


def _band_bias(qs, ws, kw, hw):
    ri = lax.broadcasted_iota(jnp.int32, (QB, kw), 0)
    ci = lax.broadcasted_iota(jnp.int32, (QB, kw), 1)
    return jnp.where(jnp.abs(ri + (qs - ws) - ci) <= hw, 0.0, NEG).astype(F32)


def _dup_kv_head(src_ref, dst_ref, head, L):
    step = min(L, 1024)
    for r0 in range(0, L, step):
        xf = src_ref[r0:r0 + step, :].astype(F32)
        lane = lax.broadcasted_iota(jnp.int32, xf.shape, 1)
        keep = jnp.logical_xor(lane < HEAD_DIM, head == 1)
        dst_ref[r0:r0 + step, :] = jnp.where(keep, xf, pltpu.roll(xf, HEAD_DIM, axis=1)).astype(dst_ref.dtype)


def _attn_fwd(q, k, v, sink, hw, gqa, out_dtype, name):
    NB, L, Cq = q.shape
    Ls = min(L, 2048)
    kw = min(QB + 2 * hw, L)

    def body(sink_ref, q_ref, k_ref, v_ref, o_ref, lse_ref, *scr):
        b, s_idx = pl.program_id(1), pl.program_id(2)
        if gqa:
            kd, vd = scr

            @pl.when(s_idx == 0)
            def _():
                _dup_kv_head(k_ref, kd, b // 2, L)
                _dup_kv_head(v_ref, vd, b // 2, L)
        else:
            kd, vd = k_ref, v_ref
        lane = lax.broadcasted_iota(jnp.int32, (QB, LANES), 1)
        lo = lane < HEAD_DIM

        def step(n, carry):
            ql = pl.multiple_of(n * QB, QB)
            qs = s_idx * Ls + ql
            ws = pl.multiple_of(jnp.clip(qs - hw, 0, L - kw), 64)
            qv = q_ref[pl.ds(ql, QB), :]
            kv_, vv = kd[pl.ds(ws, kw), :], vd[pl.ds(ws, kw), :]
            bias = _band_bias(qs, ws, kw, hw)
            res = []
            for half in (0, 1):
                qm = jnp.where(lo if half == 0 else jnp.logical_not(lo), qv, jnp.zeros_like(qv))
                s = _dot_nt(qm, kv_) + bias
                m = jnp.max(s, axis=-1, keepdims=True)
                if gqa:
                    sk = sink_ref[2 * b + half]
                    m = jnp.maximum(m, sk)
                p = jnp.exp(s - m)
                den = jnp.sum(p, axis=-1, keepdims=True)
                if gqa:
                    den = den + jnp.exp(sk - m)
                res.append((_dot(p.astype(BF16), vv) * (1.0 / den), m + jnp.log(den)))
            o_ref[pl.ds(ql, QB), :] = jnp.where(lo, res[0][0], res[1][0]).astype(o_ref.dtype)
            lse_ref[pl.ds(ql, QB), :] = jnp.where(lo, res[0][1], res[1][1])
            return carry

        lax.fori_loop(0, Ls // QB, step, 0)

    kv_map = (lambda r, b, s: (r, 0, 0)) if gqa else (lambda r, b, s: (r, 0, b))
    seg = pl.BlockSpec((None, Ls, LANES), lambda r, b, s: (r, s, b))
    return pl.pallas_call(
        body, name=name, grid=(NB, Cq // LANES, L // Ls),
        in_specs=[pl.BlockSpec(memory_space=pltpu.SMEM), seg, pl.BlockSpec((None, L, LANES), kv_map), pl.BlockSpec((None, L, LANES), kv_map)],
        out_specs=[seg, seg], out_shape=[_sds((NB, L, Cq), out_dtype), _sds((NB, L, Cq), F32)],
        scratch_shapes=[pltpu.VMEM((L, LANES), BF16)] * 2 if gqa else [],
        compiler_params=_params(("parallel", "parallel", "arbitrary")))(sink, q, k, v)


def _attn_bwd(q, k, v, do, lse, delta, sink, hw, gqa, name):
    NB, L, Cq = q.shape
    Ck = k.shape[2]
    Ls = min(L, 2048)
    kw = min(QB + 2 * hw, L)
    reps = kw // LANES
    nseg = L // Ls
    scale = HEAD_DIM ** -0.5

    def body(sink_ref, q_ref, do_ref, lse_ref, dl_ref, k_ref, v_ref, dq_ref, dk_ref, dv_ref, dsk_ref, *scr):
        b, s_idx = pl.program_id(1), pl.program_id(2)
        lane = lax.broadcasted_iota(jnp.int32, (QB, LANES), 1)
        lo = lane < HEAD_DIM
        if gqa:
            kd, vd, dk_acc, dv_acc, dsk_acc = scr

            @pl.when(s_idx == 0)
            def _():
                _dup_kv_head(k_ref, kd, b // 2, L)
                _dup_kv_head(v_ref, vd, b // 2, L)
                dk_acc[...] = jnp.zeros_like(dk_acc)
                dv_acc[...] = jnp.zeros_like(dv_acc)
                dsk_acc[...] = jnp.zeros_like(dsk_acc)

            @pl.when((s_idx == 0) & (b == 0))
            def _():
                dk_ref[...] = jnp.zeros_like(dk_ref)
                dv_ref[...] = jnp.zeros_like(dv_ref)
        else:
            kd, vd, dk_acc, dv_acc = k_ref, v_ref, dk_ref, dv_ref

            @pl.when(s_idx == 0)
            def _():
                dk_ref[...] = jnp.zeros_like(dk_ref)
                dv_ref[...] = jnp.zeros_like(dv_ref)

        def step(n, carry):
            ql = pl.multiple_of(n * QB, QB)
            qs = s_idx * Ls + ql
            ws = pl.multiple_of(jnp.clip(qs - hw, 0, L - kw), 64)
            qv, dov = q_ref[pl.ds(ql, QB), :], do_ref[pl.ds(ql, QB), :]
            lse, dl = lse_ref[pl.ds(ql, QB), :], dl_ref[pl.ds(ql, QB), :]
            kv_, vv = kd[pl.ds(ws, kw), :], vd[pl.ds(ws, kw), :]
            bias = _band_bias(qs, ws, kw, hw)
            lse_sw, dl_sw = pltpu.roll(lse, HEAD_DIM, axis=1), pltpu.roll(dl, HEAD_DIM, axis=1)
            dqs = []
            dk_c = jnp.zeros((kw, LANES), F32)
            dv_c = jnp.zeros((kw, LANES), F32)
            for half in (0, 1):
                msk = lo if half == 0 else jnp.logical_not(lo)
                qm = jnp.where(msk, qv, jnp.zeros_like(qv))
                dom = jnp.where(msk, dov, jnp.zeros_like(dov))
                lse_h = jnp.where(msk, lse, lse_sw)
                dl_h = jnp.where(msk, dl, dl_sw)
                s = _dot_nt(qm, kv_) + bias
                p = jnp.exp(s - jnp.tile(lse_h, (1, reps)))
                dp = _dot_nt(dom, vv)
                ds = (p * (dp - jnp.tile(dl_h, (1, reps)))).astype(BF16)
                dqs.append(_dot(ds, kv_))
                dk_c = dk_c + _dot_tn(ds, qm)
                dv_c = dv_c + _dot_tn(p.astype(BF16), dom)
            dq_ref[pl.ds(ql, QB), :] = (jnp.where(lo, dqs[0], dqs[1]) * scale).astype(dq_ref.dtype)
            dk_acc[pl.ds(ws, kw), :] += dk_c
            dv_acc[pl.ds(ws, kw), :] += dv_c
            if gqa:
                sk = jnp.where(lo, sink_ref[2 * b], sink_ref[2 * b + 1])
                dsk_acc[...] += -jnp.exp(sk - lse) * dl
            return carry

        lax.fori_loop(0, Ls // QB, step, 0)

        if gqa:
            @pl.when(s_idx == nseg - 1)
            def _():
                step_rows = min(L, 1024)
                for r0 in range(0, L, step_rows):
                    lanek = lax.broadcasted_iota(jnp.int32, (step_rows, LANES), 1)
                    mine = jnp.logical_xor(lanek < HEAD_DIM, (b // 2) == 1)
                    for acc, ref in ((dk_acc, dk_ref), (dv_acc, dv_ref)):
                        a = acc[r0:r0 + step_rows, :]
                        ref[r0:r0 + step_rows, :] += jnp.where(mine, a + pltpu.roll(a, HEAD_DIM, axis=1), 0.0)
                dsk_ref[...] = dsk_acc[...].reshape(QB // SUBLANES, SUBLANES, LANES).sum(axis=0)
        else:
            dsk_ref[...] = jnp.zeros_like(dsk_ref)

    kv_map = (lambda r, b, s: (r, 0, 0)) if gqa else (lambda r, b, s: (r, 0, b))
    seg = pl.BlockSpec((None, Ls, LANES), lambda r, b, s: (r, s, b))
    full = pl.BlockSpec((None, L, LANES), kv_map)
    scratch = ([pltpu.VMEM((L, LANES), BF16)] * 2 + [pltpu.VMEM((L, LANES), F32)] * 2 + [pltpu.VMEM((QB, LANES), F32)]) if gqa else []
    return pl.pallas_call(
        body, name=name, grid=(NB, Cq // LANES, nseg),
        in_specs=[pl.BlockSpec(memory_space=pltpu.SMEM), seg, seg, seg, seg, full, full],
        out_specs=[seg, full, full, pl.BlockSpec((None, None, SUBLANES, LANES), lambda r, b, s: (r, b, 0, 0))],
        out_shape=[_sds((NB, L, Cq), BF16), _sds((NB, L, Ck), F32), _sds((NB, L, Ck), F32),
                   _sds((NB, Cq // LANES, SUBLANES, LANES), F32)],
        scratch_shapes=scratch,
        compiler_params=_params(("arbitrary", "arbitrary", "arbitrary")))(sink, q, do, lse, delta, k, v)


def _merge_b(a_out, o1, l1, o4, l4, o16, l16, tm=512):
    T = a_out.shape[0]
    nbb = B_W // LANES

    def body(a_ref, o1_ref, l1_ref, o4_ref, l4_ref, o16_ref, l16_ref, cat_ref, lg1_ref, lg4_ref, lg16_ref, so, sl, slg):
        _interleave(o4_ref, so.at[0], 4, tm, nbb)
        _interleave(l4_ref, sl.at[0], 4, tm, nbb)
        _interleave(o16_ref, so.at[1], 16, tm, nbb)
        _interleave(l16_ref, sl.at[1], 16, tm, nbb)
        cat_ref[:, 0:A_Q_W] = a_ref[...]
        for cb in range(nbb):
            cols = slice(cb * LANES, (cb + 1) * LANES)
            os_ = (o1_ref[:, cols], so[0, cb], so[1, cb])
            ls_ = (l1_ref[:, cols], sl[0, cb], sl[1, cb])
            m = jnp.maximum(jnp.maximum(ls_[0], ls_[1]), ls_[2])
            es = [jnp.exp(l - m) for l in ls_]
            den = es[0] + es[1] + es[2]
            out = (es[0] * os_[0] + es[1] * os_[1] + es[2] * os_[2]) * (1.0 / den)
            lg = m + jnp.log(den)
            cat_ref[:, A_Q_W + cb * LANES:A_Q_W + (cb + 1) * LANES] = out.astype(BF16)
            lg1_ref[:, cols] = lg
            slg[cb] = lg
        _deinterleave(slg, lg4_ref, 4, tm, nbb)
        _deinterleave(slg, lg16_ref, 16, tm, nbb)

    row = lambda w: pl.BlockSpec((tm, w), lambda i: (i, 0))
    perm = lambda d: pl.BlockSpec((d, tm // d, B_W), lambda i: (0, i, 0))
    return pl.pallas_call(
        body, name="merge_patterns", grid=(T // tm,),
        in_specs=[row(A_Q_W), row(B_W), row(B_W), perm(4), perm(4), perm(16), perm(16)],
        out_specs=[row(A_Q_W + B_W), row(B_W), perm(4), perm(16)],
        out_shape=[_sds((T, A_Q_W + B_W), BF16), _sds((T, B_W), F32), _sds((4, T // 4, B_W), F32), _sds((16, T // 16, B_W), F32)],
        scratch_shapes=[pltpu.VMEM((2, nbb, tm, LANES), F32), pltpu.VMEM((2, nbb, tm, LANES), F32), pltpu.VMEM((nbb, tm, LANES), F32)],
        compiler_params=_params(("parallel",)))(a_out, o1, l1, o4, l4, o16, l16)


def _out_proj(x, cat, w_out, tm=512):
    T, D = x.shape

    def body(x_ref, c_ref, w_ref, o_ref):
        o_ref[...] = x_ref[...] + _dot(c_ref[...], w_ref[...])

    row = lambda w: pl.BlockSpec((tm, w), lambda i: (i, 0))
    return pl.pallas_call(
        body, name="out_proj", grid=(T // tm,), in_specs=[row(D), row(cat.shape[1]), pl.BlockSpec(w_out.shape, lambda i: (0, 0))],
        out_specs=row(D), out_shape=_sds((T, D), F32), compiler_params=_params(("parallel",)))(x, cat, w_out)


def _final_loss(x, g, target, tm=512):
    T, D = x.shape

    def body(x_ref, g_ref, t_ref, dx_ref, dg_ref, loss_ref):
        @pl.when(pl.program_id(0) == 0)
        def _():
            dg_ref[...] = jnp.zeros_like(dg_ref)
            loss_ref[...] = jnp.zeros_like(loss_ref)

        xv, gv = x_ref[...], g_ref[...]
        xhat, _ = _rms_stats(xv)
        err = xhat * gv - t_ref[...]
        loss_ref[...] += 0.5 * jnp.sum(jnp.sum(err * err, axis=-1, keepdims=True) * (1.0 / D), axis=0, keepdims=True)
        dx, dg = _rms_bwd(err * (1.0 / D), xv, gv)
        dx_ref[...] = dx
        dg_ref[...] += dg

    row = pl.BlockSpec((tm, D), lambda i: (i, 0))
    return pl.pallas_call(
        body, name="final_loss", grid=(T // tm,), in_specs=[row, pl.BlockSpec((1, D), lambda i: (0, 0)), row],
        out_specs=[row, pl.BlockSpec((SUBLANES, D), lambda i: (0, 0)), pl.BlockSpec((SUBLANES, LANES), lambda i: (0, 0))],
        out_shape=[_sds((T, D), F32), _sds((SUBLANES, D), F32), _sds((SUBLANES, LANES), F32)],
        compiler_params=_params(("arbitrary",)))(x, g, target)


def _dcat(dx, w_out, cat, tm=512):
    T, D = dx.shape
    C = cat.shape[1]
    nba, nbb = A_Q_W // LANES, B_W // LANES

    def body(dx_ref, w_ref, cat_ref, doa_ref, dla_ref, dob1_ref, dlb1_ref, dob4_ref, dlb4_ref, dob16_ref, dlb16_ref, sdo, sdl):
        dc = _dot_nt(dx_ref[...].astype(BF16), w_ref[...])
        ri = lax.broadcasted_iota(jnp.int32, (LANES, LANES), 0)
        ci = lax.broadcasted_iota(jnp.int32, (LANES, LANES), 1)
        same_head = ((ri // HEAD_DIM) == (ci // HEAD_DIM)).astype(BF16)
        for cb in range(C // LANES):
            cols = slice(cb * LANES, (cb + 1) * LANES)
            blk = dc[:, cols]
            prod = blk * cat_ref[:, cols].astype(F32)
            hi = prod.astype(BF16)
            lo_ = (prod - hi.astype(F32)).astype(BF16)
            dl = _dot(hi, same_head) + _dot(lo_, same_head)
            if cb < nba:
                doa_ref[:, cols] = blk.astype(BF16)
                dla_ref[:, cols] = dl
            else:
                bcols = slice((cb - nba) * LANES, (cb - nba + 1) * LANES)
                dob1_ref[:, bcols] = blk.astype(BF16)
                dlb1_ref[:, bcols] = dl
                sdo[cb - nba] = blk
                sdl[cb - nba] = dl
        _deinterleave(sdo, dob4_ref, 4, tm, nbb)
        _deinterleave(sdl, dlb4_ref, 4, tm, nbb)
        _deinterleave(sdo, dob16_ref, 16, tm, nbb)
        _deinterleave(sdl, dlb16_ref, 16, tm, nbb)

    row = lambda w: pl.BlockSpec((tm, w), lambda i: (i, 0))
    perm = lambda d: pl.BlockSpec((d, tm // d, B_W), lambda i: (0, i, 0))
    return pl.pallas_call(
        body, name="dcat", grid=(T // tm,), in_specs=[row(D), pl.BlockSpec(w_out.shape, lambda i: (0, 0)), row(C)],
        out_specs=[row(A_Q_W), row(A_Q_W), row(B_W), row(B_W), perm(4), perm(4), perm(16), perm(16)],
        out_shape=[_sds((T, A_Q_W), BF16), _sds((T, A_Q_W), F32), _sds((T, B_W), BF16), _sds((T, B_W), F32),
                   _sds((4, T // 4, B_W), BF16), _sds((4, T // 4, B_W), F32), _sds((16, T // 16, B_W), BF16), _sds((16, T // 16, B_W), F32)],
        scratch_shapes=[pltpu.VMEM((nbb, tm, LANES), F32)] * 2, compiler_params=_params(("parallel",)))(dx, w_out, cat)


def _rope_bwd_assemble(dqa, dka, dva, b1, b4, b16, cos, sin, tm=512):
    T = dqa.shape[0]
    nbb = B_W // LANES
    width = A_Q_W + 2 * A_KV_W + 3 * B_W

    def body(dqa_ref, dka_ref, dva_ref, q1, k1, v1, q4, k4, v4, q16, k16, v16, c_ref, s_ref, o_ref, scr):
        cs, sn = c_ref[...], s_ref[...]

        def unrope(t):
            return t * cs + _swap32(t * sn)

        col = 0
        for ref, rope in ((dqa_ref, True), (dka_ref, True), (dva_ref, False)):
            for cb in range(ref.shape[1] // LANES):
                t = ref[:, cb * LANES:(cb + 1) * LANES].astype(F32)
                o_ref[:, col:col + LANES] = (unrope(t) if rope else t).astype(BF16)
                col += LANES
        for which, (r1, r4, r16, rope) in enumerate(((q1, q4, q16, True), (k1, k4, k16, True), (v1, v4, v16, False))):
            _interleave(r4, scr.at[0], 4, tm, nbb)
            _interleave(r16, scr.at[1], 16, tm, nbb)
            for cb in range(nbb):
                t = r1[:, cb * LANES:(cb + 1) * LANES].astype(F32) + scr[0, cb] + scr[1, cb]
                o_ref[:, col:col + LANES] = (unrope(t) if rope else t).astype(BF16)
                col += LANES

    row = lambda w: pl.BlockSpec((tm, w), lambda i: (i, 0))
    perm = lambda d: pl.BlockSpec((d, tm // d, B_W), lambda i: (0, i, 0))
    return pl.pallas_call(
        body, name="rope_bwd", grid=(T // tm,),
        in_specs=[row(A_Q_W), row(A_KV_W), row(A_KV_W)] + [row(B_W)] * 3 + [perm(4)] * 3 + [perm(16)] * 3 + [row(LANES), row(LANES)],
        out_specs=row(width), out_shape=_sds((T, width), BF16), scratch_shapes=[pltpu.VMEM((2, nbb, tm, LANES), F32)],
        compiler_params=_params(("parallel",)))(dqa, dka, dva, *b1, *b4, *b16, cos, sin)


def _dh_norm(dproj, w_in, x, g, dres, tm=512):
    T, D = x.shape

    def body(dp_ref, w_ref, x_ref, g_ref, dr_ref, dx_ref, dg_ref):
        @pl.when(pl.program_id(0) == 0)
        def _():
            dg_ref[...] = jnp.zeros_like(dg_ref)

        dxn, dg = _rms_bwd(_dot_nt(dp_ref[...], w_ref[...]), x_ref[...], g_ref[...])
        dg_ref[...] += dg
        dx_ref[...] = dr_ref[...] + dxn

    row = lambda w: pl.BlockSpec((tm, w), lambda i: (i, 0))
    return pl.pallas_call(
        body, name="dh_norm", grid=(T // tm,),
        in_specs=[row(dproj.shape[1]), pl.BlockSpec(w_in.shape, lambda i: (0, 0)), row(D), pl.BlockSpec((1, D), lambda i: (0, 0)), row(D)],
        out_specs=[row(D), pl.BlockSpec((SUBLANES, D), lambda i: (0, 0))],
        out_shape=[_sds((T, D), F32), _sds((SUBLANES, D), F32)], compiler_params=_params(("arbitrary",)))(dproj, w_in, x, g, dres)


def _pair_send_half(gs):
    n = len(gs)

    def body(*refs):
        ins, outs = refs[:n], refs[n:2 * n]
        send, recv = refs[2 * n:]
        x, y, c = _mesh_pos()
        cps = []
        for k in range(n):
            r2 = gs[k].shape[1] // 2
            src = ins[k].at[:, pl.ds(pl.multiple_of((1 - c) * r2, 8), r2), :]
            cp = pltpu.make_async_remote_copy(src_ref=src, dst_ref=outs[k], send_sem=send.at[k], recv_sem=recv.at[k],
                                              device_id=(x, y, 1 - c), device_id_type=MESH)
            cp.start()
            cps.append(cp)
        for cp in cps:
            cp.wait()

    return pl.pallas_call(
        body, name="grad_pair_send", out_shape=[_sds((g.shape[0], g.shape[1] // 2, g.shape[2]), F32) for g in gs],
        in_specs=[ANY] * n, out_specs=[ANY] * n,
        scratch_shapes=[pltpu.SemaphoreType.DMA((n,)), pltpu.SemaphoreType.DMA((n,))], compiler_params=_params())(*gs)


def _chip_sum(c_arr, g, rb, name):
    ns, R, C = g.shape
    r2 = R // 2
    tr = r2 // 2 if (r2 // 2) % 16 == 0 else r2

    def body(c_ref, g_ref, rb_ref, o_ref):
        o_ref[...] = (g_ref[...] + rb_ref[...]).astype(BF16)

    nt = r2 // tr
    grid_spec = pltpu.PrefetchScalarGridSpec(
        num_scalar_prefetch=1, grid=(ns, nt),
        in_specs=[pl.BlockSpec((None, tr, C), lambda j, t, c_ref: (j, c_ref[0] * nt + t, 0)),
                  pl.BlockSpec((None, tr, C), lambda j, t, c_ref: (j, t, 0))],
        out_specs=pl.BlockSpec((None, tr, C), lambda j, t, c_ref: (j, t, 0)))
    return pl.pallas_call(body, name=name, grid_spec=grid_spec, out_shape=_sds((ns, r2, C), BF16),
                          compiler_params=_params(("parallel", "parallel")))(c_arr, g, rb)


def _chip_exchange(cs):
    n = len(cs)

    def body(*refs):
        ins, outs = refs[:n], refs[n:2 * n]
        send, recv, loc = refs[2 * n:]
        x, y, c = _mesh_pos()
        me = 2 * x + y
        peers = _chip_peers(x, y, c)
        local = [pltpu.make_async_copy(ins[k].at[me], outs[k].at[me], loc.at[k]) for k in range(n)]
        for cp in local:
            cp.start()
        cps = []
        for k in range(n):
            for rel, (dev, chip) in enumerate(peers):
                cp = pltpu.make_async_remote_copy(src_ref=ins[k].at[chip], dst_ref=outs[k].at[me], send_sem=send.at[k * 3 + rel],
                                                  recv_sem=recv.at[k * 3 + rel], device_id=dev, device_id_type=MESH)
                cp.start()
                cps.append(cp)
        for k in range(n):
            for rel, (dev, chip) in enumerate(peers):
                pltpu.make_async_remote_copy(src_ref=ins[k].at[chip], dst_ref=outs[k].at[chip], send_sem=send.at[k * 3 + rel],
                                             recv_sem=recv.at[k * 3 + rel], device_id=dev, device_id_type=MESH).wait_recv()
        for cp in cps:
            cp.wait_send()
        for cp in local:
            cp.wait()

    return pl.pallas_call(
        body, name="grad_chip_exchange", out_shape=[_sds(a.shape, BF16) for a in cs], in_specs=[ANY] * n, out_specs=[ANY] * n,
        scratch_shapes=[pltpu.SemaphoreType.DMA((n * 3,)), pltpu.SemaphoreType.DMA((n * 3,)), pltpu.SemaphoreType.DMA((n,))],
        compiler_params=_params())(*cs)


def _sum_chips(xs, name):
    ns, r2, C = xs.shape
    tr = r2 // 2 if (r2 // 2) % 16 == 0 else r2

    def body(x_ref, o_ref):
        acc = x_ref[0].astype(F32)
        for j in range(1, ns):
            acc = acc + x_ref[j].astype(F32)
        o_ref[...] = acc

    return pl.pallas_call(
        body, name=name, grid=(r2 // tr,), in_specs=[pl.BlockSpec((ns, tr, C), lambda t: (0, t, 0))],
        out_specs=pl.BlockSpec((tr, C), lambda t: (t, 0)), out_shape=_sds((r2, C), F32), compiler_params=_params(("parallel",)))(xs)


def _pair_share(hs):
    n = len(hs)

    def body(*refs):
        ins, outs = refs[:n], refs[n:2 * n]
        send, recv, loc = refs[2 * n:]
        x, y, c = _mesh_pos()
        cps, local = [], []
        for k in range(n):
            lc = pltpu.make_async_copy(ins[k], outs[k].at[c], loc.at[k])
            lc.start()
            local.append(lc)
            cp = pltpu.make_async_remote_copy(src_ref=ins[k], dst_ref=outs[k].at[c], send_sem=send.at[k], recv_sem=recv.at[k],
                                              device_id=(x, y, 1 - c), device_id_type=MESH)
            cp.start()
            cps.append(cp)
        for k in range(n):
            pltpu.make_async_remote_copy(src_ref=ins[k], dst_ref=outs[k].at[1 - c], send_sem=send.at[k], recv_sem=recv.at[k],
                                         device_id=(x, y, 1 - c), device_id_type=MESH).wait_recv()
        for cp in cps:
            cp.wait_send()
        for lc in local:
            lc.wait()

    return pl.pallas_call(
        body, name="grad_pair_share", out_shape=[_sds((2,) + h.shape, F32) for h in hs], in_specs=[ANY] * n, out_specs=[ANY] * n,
        scratch_shapes=[pltpu.SemaphoreType.DMA((n,)), pltpu.SemaphoreType.DMA((n,)), pltpu.SemaphoreType.DMA((n,))],
        compiler_params=_params())(*hs)


def _allreduce_small(v):
    rows, W = v.shape

    def body(v_ref, o_ref, buf, send, recv):
        x, y, c = _mesh_pos()
        me = 4 * x + 2 * y + c
        cps = []
        for m in range(1, N_DEV):
            dev = (x ^ (m >> 2), y ^ ((m >> 1) & 1), c ^ (m & 1))
            cp = pltpu.make_async_remote_copy(src_ref=v_ref, dst_ref=buf.at[me], send_sem=send.at[m - 1], recv_sem=recv.at[m - 1],
                                              device_id=dev, device_id_type=MESH)
            cp.start()
            cps.append(cp)
        for m in range(1, N_DEV):
            pltpu.make_async_remote_copy(src_ref=v_ref, dst_ref=buf.at[me ^ m], send_sem=send.at[m - 1], recv_sem=recv.at[m - 1],
                                         device_id=(x, y, c), device_id_type=MESH).wait_recv()
        for cp in cps:
            cp.wait_send()
        buf[me] = v_ref[...]
        acc = buf[0]
        for i in range(1, N_DEV):
            acc = acc + buf[i]
        o_ref[...] = acc

    return pl.pallas_call(
        body, name="allreduce_small", out_shape=_sds((rows, W), F32),
        scratch_shapes=[pltpu.VMEM((N_DEV, rows, W), F32), pltpu.SemaphoreType.DMA((N_DEV - 1,)), pltpu.SemaphoreType.DMA((N_DEV - 1,))],
        compiler_params=_params())(v)


def _adamw(w, g, m, v, name):
    R, C = w.shape
    tr = R // 2 if (R // 2) % SUBLANES == 0 else R
    c1 = 1.0 / (1.0 - ADAM_B1 ** ADAM_STEP)
    c2 = 1.0 / (1.0 - ADAM_B2 ** ADAM_STEP)

    def body(w_ref, g_ref, m_ref, v_ref, d_ref, nm_ref, nv_ref):
        gv = g_ref[...]
        nm = ADAM_B1 * m_ref[...] + (1.0 - ADAM_B1) * gv
        nv = ADAM_B2 * v_ref[...] + (1.0 - ADAM_B2) * (gv * gv)
        d_ref[...] = -ADAM_LR * ((nm * c1) / (jnp.sqrt(nv * c2) + ADAM_EPS) + ADAM_WD * w_ref[...])
        nm_ref[...] = nm
        nv_ref[...] = nv

    blk = pl.BlockSpec((tr, C), lambda t: (t, 0))
    return pl.pallas_call(body, name=name, grid=(R // tr,), in_specs=[blk] * 4, out_specs=[blk] * 3,
                          out_shape=[_sds((R, C), F32)] * 3, compiler_params=_params(("parallel",)))(w, g, m, v)


def _local_step(x, positions, target, norms, a_sink, W):
    T, D = x.shape
    g1, gm, g2, gf = norms
    inv_freq = 1.0 / (ROPE_THETA ** (jnp.arange(0, HEAD_DIM, 2, dtype=F32) / HEAD_DIM))
    inv_freq = jnp.tile(inv_freq, LANES // (HEAD_DIM // 2)).reshape(1, LANES)
    cos, sin = _rope_tables(positions.reshape(T, 1), inv_freq)
    no_sink = jnp.zeros((2 * (B_W // LANES),), F32)

    x1, h1, gate1, up1, act1 = _ffn_fwd(x, g1, W["wg1"], W["wu1"], W["wd1"], "ffn1_fwd")
    (h2, aq, ak, av, bq1, bk1, bv1, bq4, bk4, bv4, bq16, bk16, bv16) = _proj_rope(x1, gm, W["w_in"], cos, sin)
    a_out, a_lse = _attn_fwd(aq[None], ak[None], av[None], a_sink, A_HALF_WINDOW, True, BF16, "attn_a_fwd")
    bqs = {1: (bq1[None], bk1[None], bv1[None]), 4: (bq4, bk4, bv4), 16: (bq16, bk16, bv16)}
    b_o, b_l = {}, {}
    for w, d in B_PATTERNS:
        q_, k_, v_ = bqs[d]
        b_o[d], b_l[d] = _attn_fwd(q_, k_, v_, no_sink, w // (2 * d), False, F32, f"attn_b{d}_fwd")
    cat, lg1, lg4, lg16 = _merge_b(a_out[0], b_o[1][0], b_l[1][0], b_o[4], b_l[4], b_o[16], b_l[16])
    x2 = _out_proj(x1, cat, W["w_out"])
    x3, h3, gate2, up2, act2 = _ffn_fwd(x2, g2, W["wg2"], W["wu2"], W["wd2"], "ffn2_fwd")

    dx3, dgf, loss8 = _final_loss(x3, gf, target)
    dx2, dff2, dgate2, dup2, dg2 = _ffn_dx(dx3, x2, g2, gate2, up2, W["wg2"], W["wu2"], W["wd2"], "ffn2_dx")
    dwg2, dwu2 = _tn(h3, [dgate2, dup2], "shard_b", "ffn2_dw_in")
    (dwd2,) = _tn(act2, [dff2], "shard_a", "ffn2_dw_down")

    doa, dla, dob1, dlb1, dob4, dlb4, dob16, dlb16 = _dcat(dx2, W["w_out"], cat)
    (dw_out,) = _tn(cat, [dx2], "nblock", "w_out_dw", nb=D)
    dqa, dka, dva, dsk = _attn_bwd(aq[None], ak[None], av[None], doa[None], a_lse, dla[None], a_sink, A_HALF_WINDOW, True, "attn_a_bwd")
    bwd_in = {1: (dob1[None], lg1[None], dlb1[None]), 4: (dob4, lg4, dlb4), 16: (dob16, lg16, dlb16)}
    bg = {}
    for w, d in B_PATTERNS:
        q_, k_, v_ = bqs[d]
        do_, l_, dl_ = bwd_in[d]
        bg[d] = _attn_bwd(q_, k_, v_, do_, l_, dl_, no_sink, w // (2 * d), False, f"attn_b{d}_bwd")[:3]
    dproj = _rope_bwd_assemble(dqa[0], dka[0], dva[0], [t[0] for t in bg[1]], bg[4], bg[16], cos, sin)
    (dw_in,) = _tn(h2, [dproj], "nblock", "w_in_dw", nb=dproj.shape[1] // 2)
    dx1, dgm = _dh_norm(dproj, W["w_in"], x1, gm, dx2)

    dx0, dff1, dgate1, dup1, dg1 = _ffn_dx(dx1, x, g1, gate1, up1, W["wg1"], W["wu1"], W["wd1"], "ffn1_dx")
    dwg1, dwu1 = _tn(h1, [dgate1, dup1], "shard_b", "ffn1_dw_in")
    (dwd1,) = _tn(act1, [dff1], "shard_a", "ffn1_dw_down")

    dsink = dsk[0, :, :, ::HEAD_DIM].sum(axis=1).reshape(-1)
    big = dict(wg1=dwg1, wu1=dwu1, wd1=dwd1, w_in=dw_in, w_out=dw_out, wg2=dwg2, wu2=dwu2, wd2=dwd2)
    small = dict(g1=dg1.sum(axis=0), gm=dgm.sum(axis=0), g2=dg2.sum(axis=0), gf=dgf.sum(axis=0), sink=dsink, loss=loss8[0, 0])
    return dx0, big, small


BIG = ("wg1", "wu1", "wd1", "w_in", "w_out", "wg2", "wu2", "wd2")


def kernel(x, positions, norm_ffn1, w_gate1, w_up1, w_down1, norm_mix, w_in, a_sink, w_out, norm_ffn2, w_gate2, w_up2, w_down2, norm_final, loss_target, m_norm_ffn1, m_w_gate1, m_w_up1, m_w_down1, m_norm_mix, m_w_in, m_a_sink, m_w_out, m_norm_ffn2, m_w_gate2, m_w_up2, m_w_down2, m_norm_final, v_norm_ffn1, v_w_gate1, v_w_up1, v_w_down1, v_norm_mix, v_w_in, v_a_sink, v_w_out, v_norm_ffn2, v_w_gate2, v_w_up2, v_w_down2, v_norm_final):
    T, D = x.shape[1], x.shape[2]
    shards = dict(wg1=w_gate1[0], wu1=w_up1[0], wd1=w_down1[0], w_in=w_in[0], w_out=w_out[0], wg2=w_gate2[0], wu2=w_up2[0], wd2=w_down2[0])
    moms = dict(wg1=(m_w_gate1, v_w_gate1), wu1=(m_w_up1, v_w_up1), wd1=(m_w_down1, v_w_down1), w_in=(m_w_in, v_w_in),
                w_out=(m_w_out, v_w_out), wg2=(m_w_gate2, v_w_gate2), wu2=(m_w_up2, v_w_up2), wd2=(m_w_down2, v_w_down2))

    casted = _cast_all([shards[k] for k in BIG])
    full = dict(zip(BIG, _gather_weights(casted)))
    W = dict(full)
    W["w_in"] = jnp.concatenate([full["w_in"][j] for j in range(N_CHIPS)], axis=1)
    W["w_out"] = full["w_out"].reshape(N_CHIPS * w_out.shape[1], D)

    norms = (norm_ffn1, norm_mix, norm_ffn2, norm_final.reshape(1, D))
    grad_x, big, small = _local_step(x[0], positions[0], loss_target[0], norms, a_sink[0], W)
    cols = w_in.shape[2]
    big["w_in"] = jnp.stack([big["w_in"][:, j * cols:(j + 1) * cols] for j in range(N_CHIPS)], axis=0)
    big["w_out"] = big["w_out"].reshape(N_CHIPS, w_out.shape[1], D)

    c_arr = lax.axis_index("c").astype(jnp.int32).reshape(1)
    gs = [big[k] for k in BIG]
    rbs = _pair_send_half(gs)
    chip_sums = [_chip_sum(c_arr, g, rb, f"chip_sum_{k}") for k, g, rb in zip(BIG, gs, rbs)]
    landed = _chip_exchange(chip_sums)
    halves = [_sum_chips(a, f"sum_chips_{k}") for k, a in zip(BIG, landed)]
    grads = {k: g2.reshape(shards[k].shape) for k, g2 in zip(BIG, _pair_share(halves))}

    def pad_row(a):
        a = a.reshape(-1)
        return jnp.pad(a, (0, D - a.shape[0]))

    row4 = pad_row(jnp.concatenate([small["sink"], small["loss"].reshape(1)]))
    vec = jnp.stack([small["g1"], small["gm"], small["g2"], small["gf"], row4] + [jnp.zeros((D,), F32)] * 3, axis=0)
    red = _allreduce_small(vec)
    loss = red[4, 8]
    g_small = jnp.stack([red[0], red[1], red[2], red[3], pad_row(red[4, 0:8])] + [jnp.zeros((D,), F32)] * 3, axis=0)

    def small_stack(a1, am, a2, af, ask):
        return jnp.stack([pad_row(a1), pad_row(am), pad_row(a2), pad_row(af), pad_row(ask)] + [jnp.zeros((D,), F32)] * 3, axis=0)

    w_small = small_stack(norm_ffn1, norm_mix, norm_ffn2, norm_final, a_sink)
    m_small = small_stack(m_norm_ffn1, m_norm_mix, m_norm_ffn2, m_norm_final, m_a_sink)
    v_small = small_stack(v_norm_ffn1, v_norm_mix, v_norm_ffn2, v_norm_final, v_a_sink)
    live = small_stack(jnp.ones_like(norm_ffn1), jnp.ones_like(norm_mix), jnp.ones_like(norm_ffn2), jnp.ones_like(norm_final), jnp.ones_like(a_sink))
    v_small = jnp.where(live > 0, v_small, 1.0)

    upd = {}
    for k in BIG:
        m_, v_ = moms[k]
        shp = m_.shape
        d_, nm_, nv_ = _adamw(shards[k], grads[k], m_[0], v_[0], f"adamw_{k}")
        upd[k] = (grads[k].reshape(shp), d_.reshape(shp), nm_.reshape(shp), nv_.reshape(shp))
    ds_, nms_, nvs_ = _adamw(w_small, g_small, m_small, v_small, "adamw_small")

    def small_out(arr):
        return [arr[0].reshape(1, D), arr[1].reshape(1, D), arr[2].reshape(1, D), arr[3], arr[4, 0:8].reshape(1, 8)]

    gs_, dss, nmss, nvss = small_out(g_small), small_out(ds_), small_out(nms_), small_out(nvs_)

    def ordered(i):
        sm = (gs_, dss, nmss, nvss)[i]
        return [sm[0], upd["wg1"][i], upd["wu1"][i], upd["wd1"][i], sm[1], upd["w_in"][i], sm[4], upd["w_out"][i], sm[2],
                upd["wg2"][i], upd["wu2"][i], upd["wd2"][i], sm[3]]

    return (loss, grad_x[None], *ordered(0), *ordered(1), *ordered(2), *ordered(3))
```

```python
import jax
import jax.numpy as jnp
from jax import lax
from jax.experimental import pallas as pl
from jax.experimental.pallas import tpu as pltpu

F32 = jnp.float32
BF16 = jnp.bfloat16

HEAD_DIM = 64
LANES = 128
SUBLANES = 8
A_Q_W, A_KV_W, B_W = 512, 128, 512
A_HALF_WINDOW = 128
B_PATTERNS = ((128, 1), (512, 4), (2048, 16))
ROPE_THETA = 10000.0
NORM_EPS = 1e-6
FFN_RES_WEIGHT = 0.5
ADAM_LR, ADAM_B1, ADAM_B2, ADAM_EPS, ADAM_WD, ADAM_STEP = 0.001, 0.9, 0.999, 1e-08, 0.01, 10
N_CHIPS = 4
N_DEV = 8
QB = 128
NEG = -1e30
VMEM_LIMIT = 56 * 1024 * 1024
MESH = pl.DeviceIdType.MESH
ANY = pl.BlockSpec(memory_space=pl.ANY)


def _params(sem=None):
    return pltpu.CompilerParams(dimension_semantics=sem, vmem_limit_bytes=VMEM_LIMIT)


def _sds(shape, dtype):
    return jax.ShapeDtypeStruct(tuple(shape), dtype)


def _dot(a, b):
    return jnp.dot(a, b, preferred_element_type=F32)


def _dot_nt(a, b):
    return lax.dot_general(a, b, (((1,), (1,)), ((), ())), preferred_element_type=F32)


def _dot_tn(a, b):
    return lax.dot_general(a, b, (((0,), (0,)), ((), ())), preferred_element_type=F32)


def _rms_stats(x):
    r = lax.rsqrt(jnp.mean(x * x, axis=-1, keepdims=True) + NORM_EPS)
    return x * r, r


def _rms_bwd(dh, x, g):
    xhat, r = _rms_stats(x)
    dxn = dh * g
    dx = r * (dxn - xhat * jnp.mean(dxn * xhat, axis=-1, keepdims=True))
    tm, d = x.shape
    dg = (dh * xhat).reshape(tm // SUBLANES, SUBLANES, d).sum(axis=0)
    return dx, dg


def _sigmoid(x):
    return 1.0 / (1.0 + jnp.exp(-x))


def _swap32(t):
    n = t.shape[-1]
    lane = lax.broadcasted_iota(jnp.int32, t.shape, t.ndim - 1)
    return jnp.where((lane % HEAD_DIM) < HEAD_DIM // 2, pltpu.roll(t, n - HEAD_DIM // 2, axis=t.ndim - 1),
                     pltpu.roll(t, HEAD_DIM // 2, axis=t.ndim - 1))


def _cast_all(ws):
    n = len(ws)

    def body(*refs):
        for i in range(n):
            refs[n + i][...] = refs[i][...].astype(BF16)

    return pl.pallas_call(body, name="cast_shards", out_shape=[_sds(w.shape, BF16) for w in ws],
                          compiler_params=_params())(*ws)


def _mesh_pos():
    return lax.axis_index("x"), lax.axis_index("y"), lax.axis_index("c")


def _chip_peers(x, y, c):
    return [((1 - x, y, c), 2 * (1 - x) + y), ((x, 1 - y, c), 2 * x + (1 - y)), ((1 - x, 1 - y, c), 2 * (1 - x) + (1 - y))]


def _gather_weights(ws):
    n = len(ws)

    def body(*refs):
        ins, outs = refs[:n], refs[n:2 * n]
        ici_send, ici_recv, d2d_send, d2d_recv, loc = refs[2 * n:]
        x, y, c = _mesh_pos()
        me = 2 * x + y
        sibling = (x, y, 1 - c)
        peers = _chip_peers(x, y, c)

        def half(k, who):
            r2 = ws[k].shape[0] // 2
            return pl.ds(pl.multiple_of(who * r2, 16), r2)

        local = [pltpu.make_async_copy(ins[k], outs[k].at[me], loc.at[k]) for k in range(n)]
        for cp in local:
            cp.start()
        first = []
        for k in range(n):
            for rel, (dev, _) in enumerate(peers):
                cp = pltpu.make_async_remote_copy(src_ref=ins[k].at[half(k, c), :], dst_ref=outs[k].at[me, half(k, c), :],
                                                  send_sem=ici_send.at[k * 3 + rel], recv_sem=ici_recv.at[k * 3 + rel],
                                                  device_id=dev, device_id_type=MESH)
                cp.start()
                first.append(cp)
        passed = []
        for k in range(n):
            for rel, (dev, chip) in enumerate(peers):
                blk = outs[k].at[chip, half(k, c), :]
                pltpu.make_async_remote_copy(src_ref=blk, dst_ref=blk, send_sem=ici_send.at[k * 3 + rel], recv_sem=ici_recv.at[k * 3 + rel],
                                             device_id=dev, device_id_type=MESH).wait_recv()
                cp = pltpu.make_async_remote_copy(src_ref=blk, dst_ref=blk, send_sem=d2d_send.at[k * 3 + rel], recv_sem=d2d_recv.at[k * 3 + rel],
                                                  device_id=sibling, device_id_type=MESH)
                cp.start()
                passed.append(cp)
        for k in range(n):
            for rel, (dev, chip) in enumerate(peers):
                blk = outs[k].at[chip, half(k, 1 - c), :]
                pltpu.make_async_remote_copy(src_ref=blk, dst_ref=blk, send_sem=d2d_send.at[k * 3 + rel], recv_sem=d2d_recv.at[k * 3 + rel],
                                             device_id=sibling, device_id_type=MESH).wait_recv()
        for cp in first + passed:
            cp.wait_send()
        for cp in local:
            cp.wait()

    return pl.pallas_call(
        body, name="gather_weights", out_shape=[_sds((N_CHIPS,) + w.shape, BF16) for w in ws],
        in_specs=[ANY] * n, out_specs=[ANY] * n,
        scratch_shapes=[pltpu.SemaphoreType.DMA((n * 3,)), pltpu.SemaphoreType.DMA((n * 3,)), pltpu.SemaphoreType.DMA((n * 3,)),
                        pltpu.SemaphoreType.DMA((n * 3,)), pltpu.SemaphoreType.DMA((n,))],
        compiler_params=_params())(*ws)


def _ffn_fwd(x, g, wg, wu, wd, name, tm=512):
    T, D = x.shape
    ns, _, fs = wg.shape

    def body(x_ref, g_ref, wg_ref, wu_ref, wd_ref, xo_ref, h_ref, gate_ref, up_ref, act_ref, h_scr, acc):
        j = pl.program_id(1)

        @pl.when(j == 0)
        def _():
            xhat, _ = _rms_stats(x_ref[...])
            hb = (xhat * g_ref[...]).astype(BF16)
            h_scr[...] = hb
            h_ref[...] = hb
            acc[...] = jnp.zeros_like(acc)

        h = h_scr[...]
        gate = _dot(h, wg_ref[...])
        up = _dot(h, wu_ref[...])
        act = ((gate * _sigmoid(gate)) * up).astype(BF16)
        gate_ref[...] = gate.astype(BF16)
        up_ref[...] = up.astype(BF16)
        act_ref[...] = act
        acc[...] += _dot(act, wd_ref[...])

        @pl.when(j == ns - 1)
        def _():
            xo_ref[...] = x_ref[...] + FFN_RES_WEIGHT * acc[...]

    row = pl.BlockSpec((tm, D), lambda i, j: (i, 0))
    col_w = pl.BlockSpec((None, D, fs), lambda i, j: (j, 0, 0))
    saved = pl.BlockSpec((None, tm, fs), lambda i, j: (j, i, 0))
    return pl.pallas_call(
        body, name=name, grid=(T // tm, ns),
        in_specs=[row, pl.BlockSpec((1, D), lambda i, j: (0, 0)), col_w, col_w, pl.BlockSpec((None, fs, D), lambda i, j: (j, 0, 0))],
        out_specs=[row, row, saved, saved, saved],
        out_shape=[_sds((T, D), F32), _sds((T, D), BF16), _sds((ns, T, fs), BF16), _sds((ns, T, fs), BF16), _sds((ns, T, fs), BF16)],
        scratch_shapes=[pltpu.VMEM((tm, D), BF16), pltpu.VMEM((tm, D), F32)],
        compiler_params=_params(("parallel", "arbitrary")))(x, g, wg, wu, wd)


def _ffn_dx(dxo, x, g, gate_s, up_s, wg, wu, wd, name, tm=512):
    T, D = x.shape
    ns, _, fs = wg.shape

    def body(dxo_ref, x_ref, g_ref, gate_ref, up_ref, wg_ref, wu_ref, wd_ref, dx_ref, dff_ref, dgate_ref, dup_ref, dg_ref,
             dff_scr, dh_acc):
        i, j = pl.program_id(0), pl.program_id(1)

        @pl.when((i == 0) & (j == 0))
        def _():
            dg_ref[...] = jnp.zeros_like(dg_ref)

        @pl.when(j == 0)
        def _():
            d = (FFN_RES_WEIGHT * dxo_ref[...]).astype(BF16)
            dff_scr[...] = d
            dff_ref[...] = d
            dh_acc[...] = jnp.zeros_like(dh_acc)

        da = _dot_nt(dff_scr[...], wd_ref[...])
        gate = gate_ref[...].astype(F32)
        up = up_ref[...].astype(F32)
        s = _sigmoid(gate)
        silu = gate * s
        dup = (da * silu).astype(BF16)
        dgate = (da * up * (s * (1.0 + gate * (1.0 - s)))).astype(BF16)
        dgate_ref[...] = dgate
        dup_ref[...] = dup
        dh_acc[...] += _dot_nt(dgate, wg_ref[...]) + _dot_nt(dup, wu_ref[...])

        @pl.when(j == ns - 1)
        def _():
            dxn, dg = _rms_bwd(dh_acc[...], x_ref[...], g_ref[...])
            dg_ref[...] += dg
            dx_ref[...] = dxo_ref[...] + dxn

    row = pl.BlockSpec((tm, D), lambda i, j: (i, 0))
    col_w = pl.BlockSpec((None, D, fs), lambda i, j: (j, 0, 0))
    saved = pl.BlockSpec((None, tm, fs), lambda i, j: (j, i, 0))
    return pl.pallas_call(
        body, name=name, grid=(T // tm, ns),
        in_specs=[row, row, pl.BlockSpec((1, D), lambda i, j: (0, 0)), saved, saved, col_w, col_w,
                  pl.BlockSpec((None, fs, D), lambda i, j: (j, 0, 0))],
        out_specs=[row, row, saved, saved, pl.BlockSpec((SUBLANES, D), lambda i, j: (0, 0))],
        out_shape=[_sds((T, D), F32), _sds((T, D), BF16), _sds((ns, T, fs), BF16), _sds((ns, T, fs), BF16), _sds((SUBLANES, D), F32)],
        scratch_shapes=[pltpu.VMEM((tm, D), BF16), pltpu.VMEM((tm, D), F32)],
        compiler_params=_params(("arbitrary", "arbitrary")))(dxo, x, g, gate_s, up_s, wg, wu, wd)


def _tn(a, bs, mode, name, tk=1024, nb=None):
    nbs = len(bs)
    if mode == "shard_b":
        T, M = a.shape
        G, _, N = bs[0].shape
        a_spec = pl.BlockSpec((tk, M), lambda g, t: (t, 0))
        b_spec = pl.BlockSpec((None, tk, N), lambda g, t: (g, t, 0))
        o_spec, o_shape = pl.BlockSpec((None, M, N), lambda g, t: (g, 0, 0)), (G, M, N)
    elif mode == "shard_a":
        G, T, M = a.shape
        N = bs[0].shape[1]
        a_spec = pl.BlockSpec((None, tk, M), lambda g, t: (g, t, 0))
        b_spec = pl.BlockSpec((tk, N), lambda g, t: (t, 0))
        o_spec, o_shape = pl.BlockSpec((None, M, N), lambda g, t: (g, 0, 0)), (G, M, N)
    else:
        T, M = a.shape
        N = nb
        G = bs[0].shape[1] // nb
        a_spec = pl.BlockSpec((tk, M), lambda g, t: (t, 0))
        b_spec = pl.BlockSpec((tk, N), lambda g, t: (t, g))
        o_spec, o_shape = pl.BlockSpec((M, N), lambda g, t: (0, g)), (M, G * nb)

    def body(a_ref, *refs):
        b_refs, o_refs = refs[:nbs], refs[nbs:]
        av = a_ref[...].astype(BF16)
        for b_ref, o_ref in zip(b_refs, o_refs):
            @pl.when(pl.program_id(1) == 0)
            def _():
                o_ref[...] = jnp.zeros_like(o_ref)

            o_ref[...] += _dot_tn(av, b_ref[...].astype(BF16))

    return pl.pallas_call(
        body, name=name, grid=(G, T // tk), in_specs=[a_spec] + [b_spec] * nbs, out_specs=[o_spec] * nbs,
        out_shape=[_sds(o_shape, F32)] * nbs, compiler_params=_params(("parallel", "arbitrary")))(a, *bs)


def _rope_tables(pos_col, inv_freq):
    T = pos_col.shape[0]

    def body(p_ref, f_ref, c_ref, s_ref):
        ang = p_ref[...].astype(F32) * f_ref[...]
        lane = lax.broadcasted_iota(jnp.int32, ang.shape, 1)
        c_ref[...] = jnp.cos(ang)
        sn = jnp.sin(ang)
        s_ref[...] = jnp.where((lane % HEAD_DIM) < HEAD_DIM // 2, -sn, sn)

    tm = 1024
    return pl.pallas_call(
        body, name="rope_tables", grid=(T // tm,),
        in_specs=[pl.BlockSpec((tm, 1), lambda i: (i, 0)), pl.BlockSpec((1, LANES), lambda i: (0, 0))],
        out_specs=[pl.BlockSpec((tm, LANES), lambda i: (i, 0))] * 2,
        out_shape=[_sds((T, LANES), F32)] * 2, compiler_params=_params(("parallel",)))(pos_col, inv_freq)


def _deinterleave(scr, out_ref, d, tm, nblk):
    for r in range(d):
        for cb in range(nblk):
            out_ref[r, :, cb * LANES:(cb + 1) * LANES] = scr[cb, pl.ds(r, tm // d, stride=d), :].astype(out_ref.dtype)


def _interleave(in_ref, scr, d, tm, nblk):
    for r in range(d):
        for cb in range(nblk):
            scr[cb, pl.ds(r, tm // d, stride=d), :] = in_ref[r, :, cb * LANES:(cb + 1) * LANES].astype(F32)


def _proj_rope(x, g, w_in, cos, sin, tm=512):
    T, D = x.shape
    dils = [d for _, d in B_PATTERNS if d > 1]
    nbb = B_W // LANES
    scale = HEAD_DIM ** -0.5
    cuts = [0, A_Q_W, A_Q_W + A_KV_W, A_Q_W + 2 * A_KV_W, A_Q_W + 2 * A_KV_W + B_W, A_Q_W + 2 * A_KV_W + 2 * B_W,
            A_Q_W + 2 * A_KV_W + 3 * B_W]

    def body(x_ref, g_ref, w_ref, c_ref, s_ref, h_ref, aq_ref, ak_ref, av_ref, *rest):
        b_refs, scr = rest[:-1], rest[-1]
        xhat, _ = _rms_stats(x_ref[...])
        h = (xhat * g_ref[...]).astype(BF16)
        h_ref[...] = h
        cs, sn = c_ref[...], s_ref[...]

        def seg(idx, rope, mult):
            lo, hi = cuts[idx], cuts[idx + 1]
            blocks = []
            for cb in range((hi - lo) // LANES):
                p = _dot(h, w_ref[:, lo + cb * LANES:lo + (cb + 1) * LANES])
                if rope:
                    p = p * cs + _swap32(p) * sn
                if mult != 1.0:
                    p = p * mult
                blocks.append(p)
            return blocks

        for idx, ref, rope, mult in ((0, aq_ref, True, scale), (1, ak_ref, True, 1.0), (2, av_ref, False, 1.0)):
            for cb, p in enumerate(seg(idx, rope, mult)):
                ref[:, cb * LANES:(cb + 1) * LANES] = p.astype(BF16)
        for which, (idx, rope, mult) in enumerate(((3, True, scale), (4, True, 1.0), (5, False, 1.0))):
            for cb, p in enumerate(seg(idx, rope, mult)):
                b_refs[which][:, cb * LANES:(cb + 1) * LANES] = p.astype(BF16)
                scr[cb] = p
            for di, d in enumerate(dils):
                _deinterleave(scr, b_refs[3 * (di + 1) + which], d, tm, nbb)

    row = lambda w: pl.BlockSpec((tm, w), lambda i: (i, 0))
    out_specs = [row(D), row(A_Q_W), row(A_KV_W), row(A_KV_W)] + [row(B_W)] * 3
    out_shape = [_sds((T, D), BF16), _sds((T, A_Q_W), BF16), _sds((T, A_KV_W), BF16), _sds((T, A_KV_W), BF16)] + [_sds((T, B_W), BF16)] * 3
    for d in dils:
        out_specs += [pl.BlockSpec((d, tm // d, B_W), lambda i: (0, i, 0))] * 3
        out_shape += [_sds((d, T // d, B_W), BF16)] * 3
    return pl.pallas_call(
        body, name="proj_rope", grid=(T // tm,),
        in_specs=[row(D), pl.BlockSpec((1, D), lambda i: (0, 0)), pl.BlockSpec(w_in.shape, lambda i: (0, 0)), row(LANES), row(LANES)],
        out_specs=out_specs, out_shape=out_shape, scratch_shapes=[pltpu.VMEM((nbb, tm, LANES), F32)],
        compiler_params=_params(("parallel",)))(x, g, w_in, cos, sin)


def _band_bias(rel, kw, hw):
    ri = lax.broadcasted_iota(jnp.int32, (QB, kw), 0)
    ci = lax.broadcasted_iota(jnp.int32, (QB, kw), 1)
    return jnp.where(jnp.abs(ri + rel - ci) <= hw, 0.0, NEG).astype(F32)


def _band_setup(bias_scr, kw, hw):
    if bias_scr is not None:
        for i in range(3):
            bias_scr[i] = _band_bias(i * hw, kw, hw)


def _band_window(bias_scr, qs, L, kw, hw):
    ws = pl.multiple_of(jnp.clip(qs - hw, 0, L - kw), 64)
    if bias_scr is None:
        return ws, _band_bias(qs - ws, kw, hw)
    return ws, bias_scr[lax.shift_right_logical(qs - ws, hw.bit_length() - 1)]


def _dup_kv_head(src_ref, dst_ref, head, L):
    step = min(L, 1024)
    for r0 in range(0, L, step):
        xf = src_ref[r0:r0 + step, :].astype(F32)
        lane = lax.broadcasted_iota(jnp.int32, xf.shape, 1)
        keep = jnp.logical_xor(lane < HEAD_DIM, head == 1)
        dst_ref[r0:r0 + step, :] = jnp.where(keep, xf, pltpu.roll(xf, HEAD_DIM, axis=1)).astype(dst_ref.dtype)


def _attn_fwd(q, k, v, sink, hw, gqa, out_dtype, name):
    NB, L, Cq = q.shape
    Ls = min(L, 2048)
    kw = min(QB + 2 * hw, L)
    tables = L >= QB + 2 * hw
    unroll = min(4, Ls // QB)

    def body(sink_ref, q_ref, k_ref, v_ref, o_ref, lse_ref, *scr):
        b, s_idx = pl.program_id(1), pl.program_id(2)
        bias_scr = scr[0] if tables else None
        _band_setup(bias_scr, kw, hw)
        if gqa:
            kd, vd = scr[-2:]

            @pl.when(s_idx == 0)
            def _():
                _dup_kv_head(k_ref, kd, b // 2, L)
                _dup_kv_head(v_ref, vd, b // 2, L)
        else:
            kd, vd = k_ref, v_ref
        lane = lax.broadcasted_iota(jnp.int32, (QB, LANES), 1)
        lo = lane < HEAD_DIM

        def block(ql):
            qs = s_idx * Ls + ql
            ws, bias = _band_window(bias_scr, qs, L, kw, hw)
            qv = q_ref[pl.ds(ql, QB), :]
            kv_, vv = kd[pl.ds(ws, kw), :], vd[pl.ds(ws, kw), :]
            res = []
            for half in (0, 1):
                qm = jnp.where(lo if half == 0 else jnp.logical_not(lo), qv, jnp.zeros_like(qv))
                s = _dot_nt(qm, kv_) + bias
                m = jnp.max(s, axis=-1, keepdims=True)
                if gqa:
                    sk = sink_ref[2 * b + half]
                    m = jnp.maximum(m, sk)
                p = jnp.exp(s - m)
                den = jnp.sum(p, axis=-1, keepdims=True)
                if gqa:
                    den = den + jnp.exp(sk - m)
                res.append((_dot(p.astype(BF16), vv) * (1.0 / den), m + jnp.log(den)))
            o_ref[pl.ds(ql, QB), :] = jnp.where(lo, res[0][0], res[1][0]).astype(o_ref.dtype)
            lse_ref[pl.ds(ql, QB), :] = jnp.where(lo, res[0][1], res[1][1])

        def step(n, carry):
            for u in range(unroll):
                block(pl.multiple_of((n * unroll + u) * QB, QB))
            return carry

        lax.fori_loop(0, Ls // (QB * unroll), step, 0)

    kv_map = (lambda r, b, s: (r, 0, 0)) if gqa else (lambda r, b, s: (r, 0, b))
    seg = pl.BlockSpec((None, Ls, LANES), lambda r, b, s: (r, s, b))
    return pl.pallas_call(
        body, name=name, grid=(NB, Cq // LANES, L // Ls),
        in_specs=[pl.BlockSpec(memory_space=pltpu.SMEM), seg, pl.BlockSpec((None, L, LANES), kv_map), pl.BlockSpec((None, L, LANES), kv_map)],
        out_specs=[seg, seg], out_shape=[_sds((NB, L, Cq), out_dtype), _sds((NB, L, Cq), F32)],
        scratch_shapes=([pltpu.VMEM((3, QB, kw), F32)] if tables else []) + ([pltpu.VMEM((L, LANES), BF16)] * 2 if gqa else []),
        compiler_params=_params(("parallel", "parallel", "arbitrary")))(sink, q, k, v)


def _attn_bwd(q, k, v, do, lse, delta, sink, hw, gqa, name):
    NB, L, Cq = q.shape
    Ck = k.shape[2]
    Ls = min(L, 2048)
    kw = min(QB + 2 * hw, L)
    reps = kw // LANES
    nseg = L // Ls
    scale = HEAD_DIM ** -0.5
    tables = L >= QB + 2 * hw
    unroll = min(4, Ls // QB)

    def body(sink_ref, q_ref, do_ref, lse_ref, dl_ref, k_ref, v_ref, dq_ref, dk_ref, dv_ref, dsk_ref, *scr):
        b, s_idx = pl.program_id(1), pl.program_id(2)
        lane = lax.broadcasted_iota(jnp.int32, (QB, LANES), 1)
        lo = lane < HEAD_DIM
        bias_scr = scr[0] if tables else None
        _band_setup(bias_scr, kw, hw)
        if gqa:
            kd, vd, dk_acc, dv_acc, dsk_acc = scr[-5:]

            @pl.when(s_idx == 0)
            def _():
                _dup_kv_head(k_ref, kd, b // 2, L)
                _dup_kv_head(v_ref, vd, b // 2, L)
                dk_acc[...] = jnp.zeros_like(dk_acc)
                dv_acc[...] = jnp.zeros_like(dv_acc)
                dsk_acc[...] = jnp.zeros_like(dsk_acc)

            @pl.when((s_idx == 0) & (b == 0))
            def _():
                dk_ref[...] = jnp.zeros_like(dk_ref)
                dv_ref[...] = jnp.zeros_like(dv_ref)
        else:
            kd, vd, dk_acc, dv_acc = k_ref, v_ref, dk_ref, dv_ref

            @pl.when(s_idx == 0)
            def _():
                dk_ref[...] = jnp.zeros_like(dk_ref)
                dv_ref[...] = jnp.zeros_like(dv_ref)

        def block(ql):
            qs = s_idx * Ls + ql
            ws, bias = _band_window(bias_scr, qs, L, kw, hw)
            qv, dov = q_ref[pl.ds(ql, QB), :], do_ref[pl.ds(ql, QB), :]
            lse, dl = lse_ref[pl.ds(ql, QB), :], dl_ref[pl.ds(ql, QB), :]
            kv_, vv = kd[pl.ds(ws, kw), :], vd[pl.ds(ws, kw), :]
            lse_sw, dl_sw = pltpu.roll(lse, HEAD_DIM, axis=1), pltpu.roll(dl, HEAD_DIM, axis=1)
            dqs = []
            dk_c = jnp.zeros((kw, LANES), F32)
            dv_c = jnp.zeros((kw, LANES), F32)
            for half in (0, 1):
                msk = lo if half == 0 else jnp.logical_not(lo)
                qm = jnp.where(msk, qv, jnp.zeros_like(qv))
                dom = jnp.where(msk, dov, jnp.zeros_like(dov))
                lse_h = jnp.where(msk, lse, lse_sw)
                dl_h = jnp.where(msk, dl, dl_sw)
                s = _dot_nt(qm, kv_) + bias
                p = jnp.exp(s - jnp.tile(lse_h, (1, reps)))
                dp = _dot_nt(dom, vv)
                ds = (p * (dp - jnp.tile(dl_h, (1, reps)))).astype(BF16)
                dqs.append(_dot(ds, kv_))
                dk_c = dk_c + _dot_tn(ds, qm)
                dv_c = dv_c + _dot_tn(p.astype(BF16), dom)
            dq_ref[pl.ds(ql, QB), :] = (jnp.where(lo, dqs[0], dqs[1]) * scale).astype(dq_ref.dtype)
            dk_acc[pl.ds(ws, kw), :] += dk_c
            dv_acc[pl.ds(ws, kw), :] += dv_c
            if gqa:
                sk = jnp.where(lo, sink_ref[2 * b], sink_ref[2 * b + 1])
                dsk_acc[...] += -jnp.exp(sk - lse) * dl

        def step(n, carry):
            for u in range(unroll):
                block(pl.multiple_of((n * unroll + u) * QB, QB))
            return carry

        lax.fori_loop(0, Ls // (QB * unroll), step, 0)

        if gqa:
            @pl.when(s_idx == nseg - 1)
            def _():
                step_rows = min(L, 1024)
                for r0 in range(0, L, step_rows):
                    lanek = lax.broadcasted_iota(jnp.int32, (step_rows, LANES), 1)
                    mine = jnp.logical_xor(lanek < HEAD_DIM, (b // 2) == 1)
                    for acc, ref in ((dk_acc, dk_ref), (dv_acc, dv_ref)):
                        a = acc[r0:r0 + step_rows, :]
                        ref[r0:r0 + step_rows, :] += jnp.where(mine, a + pltpu.roll(a, HEAD_DIM, axis=1), 0.0)
                dsk_ref[...] = dsk_acc[...].reshape(QB // SUBLANES, SUBLANES, LANES).sum(axis=0)
        else:
            dsk_ref[...] = jnp.zeros_like(dsk_ref)

    kv_map = (lambda r, b, s: (r, 0, 0)) if gqa else (lambda r, b, s: (r, 0, b))
    seg = pl.BlockSpec((None, Ls, LANES), lambda r, b, s: (r, s, b))
    full = pl.BlockSpec((None, L, LANES), kv_map)
    scratch = [pltpu.VMEM((3, QB, kw), F32)] if tables else []
    if gqa:
        scratch += [pltpu.VMEM((L, LANES), BF16)] * 2 + [pltpu.VMEM((L, LANES), F32)] * 2 + [pltpu.VMEM((QB, LANES), F32)]
    return pl.pallas_call(
        body, name=name, grid=(NB, Cq // LANES, nseg),
        in_specs=[pl.BlockSpec(memory_space=pltpu.SMEM), seg, seg, seg, seg, full, full],
        out_specs=[seg, full, full, pl.BlockSpec((None, None, SUBLANES, LANES), lambda r, b, s: (r, b, 0, 0))],
        out_shape=[_sds((NB, L, Cq), BF16), _sds((NB, L, Ck), F32), _sds((NB, L, Ck), F32),
                   _sds((NB, Cq // LANES, SUBLANES, LANES), F32)],
        scratch_shapes=scratch,
        compiler_params=_params(("arbitrary", "arbitrary", "arbitrary")))(sink, q, do, lse, delta, k, v)


def _merge_b(a_out, o1, l1, o4, l4, o16, l16, tm=512):
    T = a_out.shape[0]
    nbb = B_W // LANES

    def body(a_ref, o1_ref, l1_ref, o4_ref, l4_ref, o16_ref, l16_ref, cat_ref, lg1_ref, lg4_ref, lg16_ref, so, sl, slg):
        _interleave(o4_ref, so.at[0], 4, tm, nbb)
        _interleave(l4_ref, sl.at[0], 4, tm, nbb)
        _interleave(o16_ref, so.at[1], 16, tm, nbb)
        _interleave(l16_ref, sl.at[1], 16, tm, nbb)
        cat_ref[:, 0:A_Q_W] = a_ref[...]
        for cb in range(nbb):
            cols = slice(cb * LANES, (cb + 1) * LANES)
            os_ = (o1_ref[:, cols], so[0, cb], so[1, cb])
            ls_ = (l1_ref[:, cols], sl[0, cb], sl[1, cb])
            m = jnp.maximum(jnp.maximum(ls_[0], ls_[1]), ls_[2])
            es = [jnp.exp(l - m) for l in ls_]
            den = es[0] + es[1] + es[2]
            out = (es[0] * os_[0] + es[1] * os_[1] + es[2] * os_[2]) * (1.0 / den)
            lg = m + jnp.log(den)
            cat_ref[:, A_Q_W + cb * LANES:A_Q_W + (cb + 1) * LANES] = out.astype(BF16)
            lg1_ref[:, cols] = lg
            slg[cb] = lg
        _deinterleave(slg, lg4_ref, 4, tm, nbb)
        _deinterleave(slg, lg16_ref, 16, tm, nbb)

    row = lambda w: pl.BlockSpec((tm, w), lambda i: (i, 0))
    perm = lambda d: pl.BlockSpec((d, tm // d, B_W), lambda i: (0, i, 0))
    return pl.pallas_call(
        body, name="merge_patterns", grid=(T // tm,),
        in_specs=[row(A_Q_W), row(B_W), row(B_W), perm(4), perm(4), perm(16), perm(16)],
        out_specs=[row(A_Q_W + B_W), row(B_W), perm(4), perm(16)],
        out_shape=[_sds((T, A_Q_W + B_W), BF16), _sds((T, B_W), F32), _sds((4, T // 4, B_W), F32), _sds((16, T // 16, B_W), F32)],
        scratch_shapes=[pltpu.VMEM((2, nbb, tm, LANES), F32), pltpu.VMEM((2, nbb, tm, LANES), F32), pltpu.VMEM((nbb, tm, LANES), F32)],
        compiler_params=_params(("parallel",)))(a_out, o1, l1, o4, l4, o16, l16)


def _out_proj(x, cat, w_out, tm=512):
    T, D = x.shape

    def body(x_ref, c_ref, w_ref, o_ref):
        o_ref[...] = x_ref[...] + _dot(c_ref[...], w_ref[...])

    row = lambda w: pl.BlockSpec((tm, w), lambda i: (i, 0))
    return pl.pallas_call(
        body, name="out_proj", grid=(T // tm,), in_specs=[row(D), row(cat.shape[1]), pl.BlockSpec(w_out.shape, lambda i: (0, 0))],
        out_specs=row(D), out_shape=_sds((T, D), F32), compiler_params=_params(("parallel",)))(x, cat, w_out)


def _final_loss(x, g, target, tm=512):
    T, D = x.shape

    def body(x_ref, g_ref, t_ref, dx_ref, dg_ref, loss_ref):
        @pl.when(pl.program_id(0) == 0)
        def _():
            dg_ref[...] = jnp.zeros_like(dg_ref)
            loss_ref[...] = jnp.zeros_like(loss_ref)

        xv, gv = x_ref[...], g_ref[...]
        xhat, _ = _rms_stats(xv)
        err = xhat * gv - t_ref[...]
        loss_ref[...] += 0.5 * jnp.sum(jnp.sum(err * err, axis=-1, keepdims=True) * (1.0 / D), axis=0, keepdims=True)
        dx, dg = _rms_bwd(err * (1.0 / D), xv, gv)
        dx_ref[...] = dx
        dg_ref[...] += dg

    row = pl.BlockSpec((tm, D), lambda i: (i, 0))
    return pl.pallas_call(
        body, name="final_loss", grid=(T // tm,), in_specs=[row, pl.BlockSpec((1, D), lambda i: (0, 0)), row],
        out_specs=[row, pl.BlockSpec((SUBLANES, D), lambda i: (0, 0)), pl.BlockSpec((SUBLANES, LANES), lambda i: (0, 0))],
        out_shape=[_sds((T, D), F32), _sds((SUBLANES, D), F32), _sds((SUBLANES, LANES), F32)],
        compiler_params=_params(("arbitrary",)))(x, g, target)


def _dcat(dx, w_out, cat, tm=512):
    T, D = dx.shape
    C = cat.shape[1]
    nba, nbb = A_Q_W // LANES, B_W // LANES

    def body(dx_ref, w_ref, cat_ref, doa_ref, dla_ref, dob1_ref, dlb1_ref, dob4_ref, dlb4_ref, dob16_ref, dlb16_ref, sdo, sdl):
        dc = _dot_nt(dx_ref[...].astype(BF16), w_ref[...])
        ri = lax.broadcasted_iota(jnp.int32, (LANES, LANES), 0)
        ci = lax.broadcasted_iota(jnp.int32, (LANES, LANES), 1)
        same_head = ((ri // HEAD_DIM) == (ci // HEAD_DIM)).astype(BF16)
        for cb in range(C // LANES):
            cols = slice(cb * LANES, (cb + 1) * LANES)
            blk = dc[:, cols]
            prod = blk * cat_ref[:, cols].astype(F32)
            hi = prod.astype(BF16)
            lo_ = (prod - hi.astype(F32)).astype(BF16)
            dl = _dot(hi, same_head) + _dot(lo_, same_head)
            if cb < nba:
                doa_ref[:, cols] = blk.astype(BF16)
                dla_ref[:, cols] = dl
            else:
                bcols = slice((cb - nba) * LANES, (cb - nba + 1) * LANES)
                dob1_ref[:, bcols] = blk.astype(BF16)
                dlb1_ref[:, bcols] = dl
                sdo[cb - nba] = blk
                sdl[cb - nba] = dl
        _deinterleave(sdo, dob4_ref, 4, tm, nbb)
        _deinterleave(sdl, dlb4_ref, 4, tm, nbb)
        _deinterleave(sdo, dob16_ref, 16, tm, nbb)
        _deinterleave(sdl, dlb16_ref, 16, tm, nbb)

    row = lambda w: pl.BlockSpec((tm, w), lambda i: (i, 0))
    perm = lambda d: pl.BlockSpec((d, tm // d, B_W), lambda i: (0, i, 0))
    return pl.pallas_call(
        body, name="dcat", grid=(T // tm,), in_specs=[row(D), pl.BlockSpec(w_out.shape, lambda i: (0, 0)), row(C)],
        out_specs=[row(A_Q_W), row(A_Q_W), row(B_W), row(B_W), perm(4), perm(4), perm(16), perm(16)],
        out_shape=[_sds((T, A_Q_W), BF16), _sds((T, A_Q_W), F32), _sds((T, B_W), BF16), _sds((T, B_W), F32),
                   _sds((4, T // 4, B_W), BF16), _sds((4, T // 4, B_W), F32), _sds((16, T // 16, B_W), BF16), _sds((16, T // 16, B_W), F32)],
        scratch_shapes=[pltpu.VMEM((nbb, tm, LANES), F32)] * 2, compiler_params=_params(("parallel",)))(dx, w_out, cat)


def _rope_bwd_assemble(dqa, dka, dva, b1, b4, b16, cos, sin, tm=512):
    T = dqa.shape[0]
    nbb = B_W // LANES
    width = A_Q_W + 2 * A_KV_W + 3 * B_W

    def body(dqa_ref, dka_ref, dva_ref, q1, k1, v1, q4, k4, v4, q16, k16, v16, c_ref, s_ref, o_ref, scr):
        cs, sn = c_ref[...], s_ref[...]

        def unrope(t):
            return t * cs + _swap32(t * sn)

        col = 0
        for ref, rope in ((dqa_ref, True), (dka_ref, True), (dva_ref, False)):
            for cb in range(ref.shape[1] // LANES):
                t = ref[:, cb * LANES:(cb + 1) * LANES].astype(F32)
                o_ref[:, col:col + LANES] = (unrope(t) if rope else t).astype(BF16)
                col += LANES
        for which, (r1, r4, r16, rope) in enumerate(((q1, q4, q16, True), (k1, k4, k16, True), (v1, v4, v16, False))):
            _interleave(r4, scr.at[0], 4, tm, nbb)
            _interleave(r16, scr.at[1], 16, tm, nbb)
            for cb in range(nbb):
                t = r1[:, cb * LANES:(cb + 1) * LANES].astype(F32) + scr[0, cb] + scr[1, cb]
                o_ref[:, col:col + LANES] = (unrope(t) if rope else t).astype(BF16)
                col += LANES

    row = lambda w: pl.BlockSpec((tm, w), lambda i: (i, 0))
    perm = lambda d: pl.BlockSpec((d, tm // d, B_W), lambda i: (0, i, 0))
    return pl.pallas_call(
        body, name="rope_bwd", grid=(T // tm,),
        in_specs=[row(A_Q_W), row(A_KV_W), row(A_KV_W)] + [row(B_W)] * 3 + [perm(4)] * 3 + [perm(16)] * 3 + [row(LANES), row(LANES)],
        out_specs=row(width), out_shape=_sds((T, width), BF16), scratch_shapes=[pltpu.VMEM((2, nbb, tm, LANES), F32)],
        compiler_params=_params(("parallel",)))(dqa, dka, dva, *b1, *b4, *b16, cos, sin)


def _dh_norm(dproj, w_in, x, g, dres, tm=512):
    T, D = x.shape

    def body(dp_ref, w_ref, x_ref, g_ref, dr_ref, dx_ref, dg_ref):
        @pl.when(pl.program_id(0) == 0)
        def _():
            dg_ref[...] = jnp.zeros_like(dg_ref)

        dxn, dg = _rms_bwd(_dot_nt(dp_ref[...], w_ref[...]), x_ref[...], g_ref[...])
        dg_ref[...] += dg
        dx_ref[...] = dr_ref[...] + dxn

    row = lambda w: pl.BlockSpec((tm, w), lambda i: (i, 0))
    return pl.pallas_call(
        body, name="dh_norm", grid=(T // tm,),
        in_specs=[row(dproj.shape[1]), pl.BlockSpec(w_in.shape, lambda i: (0, 0)), row(D), pl.BlockSpec((1, D), lambda i: (0, 0)), row(D)],
        out_specs=[row(D), pl.BlockSpec((SUBLANES, D), lambda i: (0, 0))],
        out_shape=[_sds((T, D), F32), _sds((SUBLANES, D), F32)], compiler_params=_params(("arbitrary",)))(dproj, w_in, x, g, dres)


def _pair_send_half(gs):
    n = len(gs)

    def body(*refs):
        ins, outs = refs[:n], refs[n:2 * n]
        send, recv = refs[2 * n:]
        x, y, c = _mesh_pos()
        cps = []
        for k in range(n):
            r2 = gs[k].shape[1] // 2
            src = ins[k].at[:, pl.ds(pl.multiple_of((1 - c) * r2, 8), r2), :]
            cp = pltpu.make_async_remote_copy(src_ref=src, dst_ref=outs[k], send_sem=send.at[k], recv_sem=recv.at[k],
                                              device_id=(x, y, 1 - c), device_id_type=MESH)
            cp.start()
            cps.append(cp)
        for cp in cps:
            cp.wait()

    return pl.pallas_call(
        body, name="grad_pair_send", out_shape=[_sds((g.shape[0], g.shape[1] // 2, g.shape[2]), F32) for g in gs],
        in_specs=[ANY] * n, out_specs=[ANY] * n,
        scratch_shapes=[pltpu.SemaphoreType.DMA((n,)), pltpu.SemaphoreType.DMA((n,))], compiler_params=_params())(*gs)


def _chip_sum(c_arr, g, rb, name):
    ns, R, C = g.shape
    r2 = R // 2
    tr = r2 // 2 if (r2 // 2) % 16 == 0 else r2

    def body(c_ref, g_ref, rb_ref, o_ref):
        o_ref[...] = (g_ref[...] + rb_ref[...]).astype(BF16)

    nt = r2 // tr
    grid_spec = pltpu.PrefetchScalarGridSpec(
        num_scalar_prefetch=1, grid=(ns, nt),
        in_specs=[pl.BlockSpec((None, tr, C), lambda j, t, c_ref: (j, c_ref[0] * nt + t, 0)),
                  pl.BlockSpec((None, tr, C), lambda j, t, c_ref: (j, t, 0))],
        out_specs=pl.BlockSpec((None, tr, C), lambda j, t, c_ref: (j, t, 0)))
    return pl.pallas_call(body, name=name, grid_spec=grid_spec, out_shape=_sds((ns, r2, C), BF16),
                          compiler_params=_params(("parallel", "parallel")))(c_arr, g, rb)


def _chip_exchange(cs):
    n = len(cs)

    def body(*refs):
        ins, outs = refs[:n], refs[n:2 * n]
        send, recv, loc = refs[2 * n:]
        x, y, c = _mesh_pos()
        me = 2 * x + y
        peers = _chip_peers(x, y, c)
        local = [pltpu.make_async_copy(ins[k].at[me], outs[k].at[me], loc.at[k]) for k in range(n)]
        for cp in local:
            cp.start()
        cps = []
        for k in range(n):
            for rel, (dev, chip) in enumerate(peers):
                cp = pltpu.make_async_remote_copy(src_ref=ins[k].at[chip], dst_ref=outs[k].at[me], send_sem=send.at[k * 3 + rel],
                                                  recv_sem=recv.at[k * 3 + rel], device_id=dev, device_id_type=MESH)
                cp.start()
                cps.append(cp)
        for k in range(n):
            for rel, (dev, chip) in enumerate(peers):
                pltpu.make_async_remote_copy(src_ref=ins[k].at[chip], dst_ref=outs[k].at[chip], send_sem=send.at[k * 3 + rel],
                                             recv_sem=recv.at[k * 3 + rel], device_id=dev, device_id_type=MESH).wait_recv()
        for cp in cps:
            cp.wait_send()
        for cp in local:
            cp.wait()

    return pl.pallas_call(
        body, name="grad_chip_exchange", out_shape=[_sds(a.shape, BF16) for a in cs], in_specs=[ANY] * n, out_specs=[ANY] * n,
        scratch_shapes=[pltpu.SemaphoreType.DMA((n * 3,)), pltpu.SemaphoreType.DMA((n * 3,)), pltpu.SemaphoreType.DMA((n,))],
        compiler_params=_params())(*cs)


def _sum_chips(xs, name):
    ns, r2, C = xs.shape
    tr = r2 // 2 if (r2 // 2) % 16 == 0 else r2

    def body(x_ref, o_ref):
        acc = x_ref[0].astype(F32)
        for j in range(1, ns):
            acc = acc + x_ref[j].astype(F32)
        o_ref[...] = acc

    return pl.pallas_call(
        body, name=name, grid=(r2 // tr,), in_specs=[pl.BlockSpec((ns, tr, C), lambda t: (0, t, 0))],
        out_specs=pl.BlockSpec((tr, C), lambda t: (t, 0)), out_shape=_sds((r2, C), F32), compiler_params=_params(("parallel",)))(xs)


def _pair_share(hs):
    n = len(hs)

    def body(*refs):
        ins, outs = refs[:n], refs[n:2 * n]
        send, recv, loc = refs[2 * n:]
        x, y, c = _mesh_pos()
        cps, local = [], []
        for k in range(n):
            lc = pltpu.make_async_copy(ins[k], outs[k].at[c], loc.at[k])
            lc.start()
            local.append(lc)
            cp = pltpu.make_async_remote_copy(src_ref=ins[k], dst_ref=outs[k].at[c], send_sem=send.at[k], recv_sem=recv.at[k],
                                              device_id=(x, y, 1 - c), device_id_type=MESH)
            cp.start()
            cps.append(cp)
        for k in range(n):
            pltpu.make_async_remote_copy(src_ref=ins[k], dst_ref=outs[k].at[1 - c], send_sem=send.at[k], recv_sem=recv.at[k],
                                         device_id=(x, y, 1 - c), device_id_type=MESH).wait_recv()
        for cp in cps:
            cp.wait_send()
        for lc in local:
            lc.wait()

    return pl.pallas_call(
        body, name="grad_pair_share", out_shape=[_sds((2,) + h.shape, F32) for h in hs], in_specs=[ANY] * n, out_specs=[ANY] * n,
        scratch_shapes=[pltpu.SemaphoreType.DMA((n,)), pltpu.SemaphoreType.DMA((n,)), pltpu.SemaphoreType.DMA((n,))],
        compiler_params=_params())(*hs)


def _allreduce_small(v):
    rows, W = v.shape

    def body(v_ref, o_ref, buf, send, recv):
        x, y, c = _mesh_pos()
        me = 4 * x + 2 * y + c
        cps = []
        for m in range(1, N_DEV):
            dev = (x ^ (m >> 2), y ^ ((m >> 1) & 1), c ^ (m & 1))
            cp = pltpu.make_async_remote_copy(src_ref=v_ref, dst_ref=buf.at[me], send_sem=send.at[m - 1], recv_sem=recv.at[m - 1],
                                              device_id=dev, device_id_type=MESH)
            cp.start()
            cps.append(cp)
        for m in range(1, N_DEV):
            pltpu.make_async_remote_copy(src_ref=v_ref, dst_ref=buf.at[me ^ m], send_sem=send.at[m - 1], recv_sem=recv.at[m - 1],
                                         device_id=(x, y, c), device_id_type=MESH).wait_recv()
        for cp in cps:
            cp.wait_send()
        buf[me] = v_ref[...]
        acc = buf[0]
        for i in range(1, N_DEV):
            acc = acc + buf[i]
        o_ref[...] = acc

    return pl.pallas_call(
        body, name="allreduce_small", out_shape=_sds((rows, W), F32),
        scratch_shapes=[pltpu.VMEM((N_DEV, rows, W), F32), pltpu.SemaphoreType.DMA((N_DEV - 1,)), pltpu.SemaphoreType.DMA((N_DEV - 1,))],
        compiler_params=_params())(v)


def _adamw(w, g, m, v, name):
    R, C = w.shape
    tr = R // 2 if (R // 2) % SUBLANES == 0 else R
    c1 = 1.0 / (1.0 - ADAM_B1 ** ADAM_STEP)
    c2 = 1.0 / (1.0 - ADAM_B2 ** ADAM_STEP)

    def body(w_ref, g_ref, m_ref, v_ref, d_ref, nm_ref, nv_ref):
        gv = g_ref[...]
        nm = ADAM_B1 * m_ref[...] + (1.0 - ADAM_B1) * gv
        nv = ADAM_B2 * v_ref[...] + (1.0 - ADAM_B2) * (gv * gv)
        d_ref[...] = -ADAM_LR * ((nm * c1) / (jnp.sqrt(nv * c2) + ADAM_EPS) + ADAM_WD * w_ref[...])
        nm_ref[...] = nm
        nv_ref[...] = nv

    blk = pl.BlockSpec((tr, C), lambda t: (t, 0))
    return pl.pallas_call(body, name=name, grid=(R // tr,), in_specs=[blk] * 4, out_specs=[blk] * 3,
                          out_shape=[_sds((R, C), F32)] * 3, compiler_params=_params(("parallel",)))(w, g, m, v)


def _local_step(x, positions, target, norms, a_sink, W):
    T, D = x.shape
    g1, gm, g2, gf = norms
    inv_freq = 1.0 / (ROPE_THETA ** (jnp.arange(0, HEAD_DIM, 2, dtype=F32) / HEAD_DIM))
    inv_freq = jnp.tile(inv_freq, LANES // (HEAD_DIM // 2)).reshape(1, LANES)
    cos, sin = _rope_tables(positions.reshape(T, 1), inv_freq)
    no_sink = jnp.zeros((2 * (B_W // LANES),), F32)

    x1, h1, gate1, up1, act1 = _ffn_fwd(x, g1, W["wg1"], W["wu1"], W["wd1"], "ffn1_fwd")
    (h2, aq, ak, av, bq1, bk1, bv1, bq4, bk4, bv4, bq16, bk16, bv16) = _proj_rope(x1, gm, W["w_in"], cos, sin)
    a_out, a_lse = _attn_fwd(aq[None], ak[None], av[None], a_sink, A_HALF_WINDOW, True, BF16, "attn_a_fwd")
    bqs = {1: (bq1[None], bk1[None], bv1[None]), 4: (bq4, bk4, bv4), 16: (bq16, bk16, bv16)}
    b_o, b_l = {}, {}
    for w, d in B_PATTERNS:
        q_, k_, v_ = bqs[d]
        b_o[d], b_l[d] = _attn_fwd(q_, k_, v_, no_sink, w // (2 * d), False, F32, f"attn_b{d}_fwd")
    cat, lg1, lg4, lg16 = _merge_b(a_out[0], b_o[1][0], b_l[1][0], b_o[4], b_l[4], b_o[16], b_l[16])
    x2 = _out_proj(x1, cat, W["w_out"])
    x3, h3, gate2, up2, act2 = _ffn_fwd(x2, g2, W["wg2"], W["wu2"], W["wd2"], "ffn2_fwd")

    dx3, dgf, loss8 = _final_loss(x3, gf, target)
    dx2, dff2, dgate2, dup2, dg2 = _ffn_dx(dx3, x2, g2, gate2, up2, W["wg2"], W["wu2"], W["wd2"], "ffn2_dx")
    dwg2, dwu2 = _tn(h3, [dgate2, dup2], "shard_b", "ffn2_dw_in")
    (dwd2,) = _tn(act2, [dff2], "shard_a", "ffn2_dw_down")

    doa, dla, dob1, dlb1, dob4, dlb4, dob16, dlb16 = _dcat(dx2, W["w_out"], cat)
    (dw_out,) = _tn(cat, [dx2], "nblock", "w_out_dw", nb=D)
    dqa, dka, dva, dsk = _attn_bwd(aq[None], ak[None], av[None], doa[None], a_lse, dla[None], a_sink, A_HALF_WINDOW, True, "attn_a_bwd")
    bwd_in = {1: (dob1[None], lg1[None], dlb1[None]), 4: (dob4, lg4, dlb4), 16: (dob16, lg16, dlb16)}
    bg = {}
    for w, d in B_PATTERNS:
        q_, k_, v_ = bqs[d]
        do_, l_, dl_ = bwd_in[d]
        bg[d] = _attn_bwd(q_, k_, v_, do_, l_, dl_, no_sink, w // (2 * d), False, f"attn_b{d}_bwd")[:3]
    dproj = _rope_bwd_assemble(dqa[0], dka[0], dva[0], [t[0] for t in bg[1]], bg[4], bg[16], cos, sin)
    (dw_in,) = _tn(h2, [dproj], "nblock", "w_in_dw", nb=dproj.shape[1] // 2)
    dx1, dgm = _dh_norm(dproj, W["w_in"], x1, gm, dx2)

    dx0, dff1, dgate1, dup1, dg1 = _ffn_dx(dx1, x, g1, gate1, up1, W["wg1"], W["wu1"], W["wd1"], "ffn1_dx")
    dwg1, dwu1 = _tn(h1, [dgate1, dup1], "shard_b", "ffn1_dw_in")
    (dwd1,) = _tn(act1, [dff1], "shard_a", "ffn1_dw_down")

    dsink = dsk[0, :, :, ::HEAD_DIM].sum(axis=1).reshape(-1)
    big = dict(wg1=dwg1, wu1=dwu1, wd1=dwd1, w_in=dw_in, w_out=dw_out, wg2=dwg2, wu2=dwu2, wd2=dwd2)
    small = dict(g1=dg1.sum(axis=0), gm=dgm.sum(axis=0), g2=dg2.sum(axis=0), gf=dgf.sum(axis=0), sink=dsink, loss=loss8[0, 0])
    return dx0, big, small


BIG = ("wg1", "wu1", "wd1", "w_in", "w_out", "wg2", "wu2", "wd2")


def kernel(x, positions, norm_ffn1, w_gate1, w_up1, w_down1, norm_mix, w_in, a_sink, w_out, norm_ffn2, w_gate2, w_up2, w_down2, norm_final, loss_target, m_norm_ffn1, m_w_gate1, m_w_up1, m_w_down1, m_norm_mix, m_w_in, m_a_sink, m_w_out, m_norm_ffn2, m_w_gate2, m_w_up2, m_w_down2, m_norm_final, v_norm_ffn1, v_w_gate1, v_w_up1, v_w_down1, v_norm_mix, v_w_in, v_a_sink, v_w_out, v_norm_ffn2, v_w_gate2, v_w_up2, v_w_down2, v_norm_final):
    T, D = x.shape[1], x.shape[2]
    shards = dict(wg1=w_gate1[0], wu1=w_up1[0], wd1=w_down1[0], w_in=w_in[0], w_out=w_out[0], wg2=w_gate2[0], wu2=w_up2[0], wd2=w_down2[0])
    moms = dict(wg1=(m_w_gate1, v_w_gate1), wu1=(m_w_up1, v_w_up1), wd1=(m_w_down1, v_w_down1), w_in=(m_w_in, v_w_in),
                w_out=(m_w_out, v_w_out), wg2=(m_w_gate2, v_w_gate2), wu2=(m_w_up2, v_w_up2), wd2=(m_w_down2, v_w_down2))

    casted = _cast_all([shards[k] for k in BIG])
    full = dict(zip(BIG, _gather_weights(casted)))
    W = dict(full)
    W["w_in"] = jnp.concatenate([full["w_in"][j] for j in range(N_CHIPS)], axis=1)
    W["w_out"] = full["w_out"].reshape(N_CHIPS * w_out.shape[1], D)

    norms = (norm_ffn1, norm_mix, norm_ffn2, norm_final.reshape(1, D))
    grad_x, big, small = _local_step(x[0], positions[0], loss_target[0], norms, a_sink[0], W)
    cols = w_in.shape[2]
    big["w_in"] = jnp.stack([big["w_in"][:, j * cols:(j + 1) * cols] for j in range(N_CHIPS)], axis=0)
    big["w_out"] = big["w_out"].reshape(N_CHIPS, w_out.shape[1], D)

    c_arr = lax.axis_index("c").astype(jnp.int32).reshape(1)
    gs = [big[k] for k in BIG]
    rbs = _pair_send_half(gs)
    chip_sums = [_chip_sum(c_arr, g, rb, f"chip_sum_{k}") for k, g, rb in zip(BIG, gs, rbs)]
    landed = _chip_exchange(chip_sums)
    halves = [_sum_chips(a, f"sum_chips_{k}") for k, a in zip(BIG, landed)]
    grads = {k: g2.reshape(shards[k].shape) for k, g2 in zip(BIG, _pair_share(halves))}

    def pad_row(a):
        a = a.reshape(-1)
        return jnp.pad(a, (0, D - a.shape[0]))

    row4 = pad_row(jnp.concatenate([small["sink"], small["loss"].reshape(1)]))
    vec = jnp.stack([small["g1"], small["gm"], small["g2"], small["gf"], row4] + [jnp.zeros((D,), F32)] * 3, axis=0)
    red = _allreduce_small(vec)
    loss = red[4, 8]
    g_small = jnp.stack([red[0], red[1], red[2], red[3], pad_row(red[4, 0:8])] + [jnp.zeros((D,), F32)] * 3, axis=0)

    def small_stack(a1, am, a2, af, ask):
        return jnp.stack([pad_row(a1), pad_row(am), pad_row(a2), pad_row(af), pad_row(ask)] + [jnp.zeros((D,), F32)] * 3, axis=0)

    w_small = small_stack(norm_ffn1, norm_mix, norm_ffn2, norm_final, a_sink)
    m_small = small_stack(m_norm_ffn1, m_norm_mix, m_norm_ffn2, m_norm_final, m_a_sink)
    v_small = small_stack(v_norm_ffn1, v_norm_mix, v_norm_ffn2, v_norm_final, v_a_sink)
    live = small_stack(jnp.ones_like(norm_ffn1), jnp.ones_like(norm_mix), jnp.ones_like(norm_ffn2), jnp.ones_like(norm_final), jnp.ones_like(a_sink))
    v_small = jnp.where(live > 0, v_small, 1.0)

    upd = {}
    for k in BIG:
        m_, v_ = moms[k]
        shp = m_.shape
        d_, nm_, nv_ = _adamw(shards[k], grads[k], m_[0], v_[0], f"adamw_{k}")
        upd[k] = (grads[k].reshape(shp), d_.reshape(shp), nm_.reshape(shp), nv_.reshape(shp))
    ds_, nms_, nvs_ = _adamw(w_small, g_small, m_small, v_small, "adamw_small")

    def small_out(arr):
        return [arr[0].reshape(1, D), arr[1].reshape(1, D), arr[2].reshape(1, D), arr[3], arr[4, 0:8].reshape(1, 8)]

    gs_, dss, nmss, nvss = small_out(g_small), small_out(ds_), small_out(nms_), small_out(nvs_)

    def ordered(i):
        sm = (gs_, dss, nmss, nvss)[i]
        return [sm[0], upd["wg1"][i], upd["wu1"][i], upd["wd1"][i], sm[1], upd["w_in"][i], sm[4], upd["w_out"][i], sm[2],
                upd["wg2"][i], upd["wu2"][i], upd["wd2"][i], sm[3]]

    return (loss, grad_x[None], *ordered(0), *ordered(1), *ordered(2), *ordered(3))
```

```python
import jax
import jax.numpy as jnp
from jax import lax
from jax.experimental import pallas as pl
from jax.experimental.pallas import tpu as pltpu

F32 = jnp.float32
BF16 = jnp.bfloat16

HEAD_DIM = 64
LANES = 128
SUBLANES = 8
A_Q_W, A_KV_W, B_W = 512, 128, 512
A_HALF_WINDOW = 128
B_PATTERNS = ((128, 1), (512, 4), (2048, 16))
ROPE_THETA = 10000.0
NORM_EPS = 1e-6
FFN_RES_WEIGHT = 0.5
ADAM_LR, ADAM_B1, ADAM_B2, ADAM_EPS, ADAM_WD, ADAM_STEP = 0.001, 0.9, 0.999, 1e-08, 0.01, 10
N_CHIPS = 4
N_DEV = 8
QB = 128
NEG = -1e30
VMEM_LIMIT = 56 * 1024 * 1024
MESH = pl.DeviceIdType.MESH
ANY = pl.BlockSpec(memory_space=pl.ANY)


def _params(sem=None):
    return pltpu.CompilerParams(dimension_semantics=sem, vmem_limit_bytes=VMEM_LIMIT)


def _sds(shape, dtype):
    return jax.ShapeDtypeStruct(tuple(shape), dtype)


def _dot(a, b):
    return jnp.dot(a, b, preferred_element_type=F32)


def _dot_nt(a, b):
    return lax.dot_general(a, b, (((1,), (1,)), ((), ())), preferred_element_type=F32)


def _dot_tn(a, b):
    return lax.dot_general(a, b, (((0,), (0,)), ((), ())), preferred_element_type=F32)


def _rms_stats(x):
    r = lax.rsqrt(jnp.mean(x * x, axis=-1, keepdims=True) + NORM_EPS)
    return x * r, r


def _rms_bwd(dh, x, g):
    xhat, r = _rms_stats(x)
    dxn = dh * g
    dx = r * (dxn - xhat * jnp.mean(dxn * xhat, axis=-1, keepdims=True))
    tm, d = x.shape
    dg = (dh * xhat).reshape(tm // SUBLANES, SUBLANES, d).sum(axis=0)
    return dx, dg


def _sigmoid(x):
    return 1.0 / (1.0 + jnp.exp(-x))


def _swap32(t):
    n = t.shape[-1]
    lane = lax.broadcasted_iota(jnp.int32, t.shape, t.ndim - 1)
    return jnp.where((lane % HEAD_DIM) < HEAD_DIM // 2, pltpu.roll(t, n - HEAD_DIM // 2, axis=t.ndim - 1),
                     pltpu.roll(t, HEAD_DIM // 2, axis=t.ndim - 1))


def _cast_place(me_arr, w, name):
    R, C = w.shape
    tr = R // 2 if (R // 2) % 16 == 0 else R

    def body(me_ref, w_ref, o_ref):
        o_ref[...] = w_ref[...].astype(BF16)

    grid_spec = pltpu.PrefetchScalarGridSpec(
        num_scalar_prefetch=1, grid=(R // tr,), in_specs=[pl.BlockSpec((tr, C), lambda t, me: (t, 0))],
        out_specs=pl.BlockSpec((None, tr, C), lambda t, me: (me[0], t, 0)))
    return pl.pallas_call(body, name=name, grid_spec=grid_spec, out_shape=_sds((N_CHIPS, R, C), BF16),
                          compiler_params=_params(("parallel",)))(me_arr, w)


HBM = pl.BlockSpec(memory_space=pltpu.HBM)
SEM = pl.BlockSpec(memory_space=pltpu.SEMAPHORE)


def _push_start(name, bufs, ncopies, plan, after):
    nb = len(bufs)

    def body(*refs):
        send, recv, token = refs[nb + 1], refs[nb + 2], refs[-1]
        for i, (src, dst, dev) in enumerate(plan(refs[:nb])):
            pltpu.make_async_remote_copy(src_ref=src, dst_ref=dst, send_sem=send.at[i], recv_sem=recv.at[i],
                                         device_id=dev, device_id_type=MESH).start()
        token[...] = jnp.zeros_like(token)

    outs = pl.pallas_call(
        body, name=name,
        out_shape=(pltpu.SemaphoreType.DMA((ncopies,)), pltpu.SemaphoreType.DMA((ncopies,)), *[pltpu.HBM(b.shape, b.dtype) for b in bufs],
                   _sds((SUBLANES, LANES), F32)),
        in_specs=[HBM] * nb + [ANY], out_specs=(SEM, SEM, *([HBM] * nb), pl.BlockSpec(memory_space=pltpu.VMEM)),
        input_output_aliases={i: 2 + i for i in range(nb)},
        compiler_params=pltpu.CompilerParams(has_side_effects=pltpu.SideEffectType.DATAFLOW_SIDE_EFFECTING),
    )(*[pltpu.with_memory_space_constraint(b, pltpu.HBM) for b in bufs], after)
    return outs[0], outs[1], list(outs[2:2 + nb]), outs[-1]


def _push_wait(name, send, recv, bufs, plan, after):
    nb = len(bufs)

    def body(*refs):
        send_ref, recv_ref = refs[nb], refs[nb + 1]
        for i, (src, dst, dev) in enumerate(plan(refs[:nb])):
            cp = pltpu.make_async_remote_copy(src_ref=src, dst_ref=dst, send_sem=send_ref.at[i], recv_sem=recv_ref.at[i],
                                              device_id=dev, device_id_type=MESH)
            cp.wait_send()
            cp.wait_recv()

    outs = pl.pallas_call(
        body, name=name, out_shape=tuple(pltpu.HBM(b.shape, b.dtype) for b in bufs),
        in_specs=[HBM] * nb + [SEM, SEM, ANY], out_specs=tuple([HBM] * nb), input_output_aliases={i: i for i in range(nb)},
        compiler_params=pltpu.CompilerParams(has_side_effects=pltpu.SideEffectType.DATAFLOW_SIDE_EFFECTING),
    )(*bufs, send, recv, after)
    return list(outs)


def _mesh_pos():
    return lax.axis_index("x"), lax.axis_index("y"), lax.axis_index("c")


def _chip_peers(x, y, c):
    return [((1 - x, y, c), 2 * (1 - x) + y), ((x, 1 - y, c), 2 * x + (1 - y)), ((1 - x, 1 - y, c), 2 * (1 - x) + (1 - y))]


def _gather_plan(n):
    def plan(refs):
        x, y, c = _mesh_pos()
        me = 2 * x + y
        return [(refs[k].at[me], refs[k].at[me], dev) for k in range(n) for dev, _ in _chip_peers(x, y, c)]
    return plan


def _gather_weights(fulls):
    n = len(fulls)

    def body(*refs):
        ins, outs = refs[:n], refs[n:2 * n]
        ici_send, ici_recv, d2d_send, d2d_recv = refs[2 * n:]
        x, y, c = _mesh_pos()
        me = 2 * x + y
        sibling = (x, y, 1 - c)
        peers = _chip_peers(x, y, c)

        def half(k, who):
            r2 = fulls[k].shape[1] // 2
            return pl.ds(pl.multiple_of(who * r2, 16), r2)

        first = []
        for k in range(n):
            for rel, (dev, _) in enumerate(peers):
                cp = pltpu.make_async_remote_copy(src_ref=ins[k].at[me, half(k, c), :], dst_ref=outs[k].at[me, half(k, c), :],
                                                  send_sem=ici_send.at[k * 3 + rel], recv_sem=ici_recv.at[k * 3 + rel],
                                                  device_id=dev, device_id_type=MESH)
                cp.start()
                first.append(cp)
        passed = []
        for k in range(n):
            for rel, (dev, chip) in enumerate(peers):
                blk = outs[k].at[chip, half(k, c), :]
                pltpu.make_async_remote_copy(src_ref=blk, dst_ref=blk, send_sem=ici_send.at[k * 3 + rel], recv_sem=ici_recv.at[k * 3 + rel],
                                             device_id=dev, device_id_type=MESH).wait_recv()
                cp = pltpu.make_async_remote_copy(src_ref=blk, dst_ref=blk, send_sem=d2d_send.at[k * 3 + rel], recv_sem=d2d_recv.at[k * 3 + rel],
                                                  device_id=sibling, device_id_type=MESH)
                cp.start()
                passed.append(cp)
        for k in range(n):
            for rel, (dev, chip) in enumerate(peers):
                blk = outs[k].at[chip, half(k, 1 - c), :]
                pltpu.make_async_remote_copy(src_ref=blk, dst_ref=blk, send_sem=d2d_send.at[k * 3 + rel], recv_sem=d2d_recv.at[k * 3 + rel],
                                             device_id=sibling, device_id_type=MESH).wait_recv()
        for cp in first + passed:
            cp.wait_send()

    return pl.pallas_call(
        body, name="gather_weights", out_shape=[_sds(f.shape, BF16) for f in fulls],
        in_specs=[ANY] * n, out_specs=[ANY] * n, input_output_aliases={k: k for k in range(n)},
        scratch_shapes=[pltpu.SemaphoreType.DMA((n * 3,))] * 4, compiler_params=_params())(*fulls)


def _ffn_fwd(x, g, wg, wu, wd, name, tm=512):
    T, D = x.shape
    ns, _, fs = wg.shape

    def body(x_ref, g_ref, wg_ref, wu_ref, wd_ref, xo_ref, h_ref, gate_ref, up_ref, act_ref, h_scr, acc):
        j = pl.program_id(1)

        @pl.when(j == 0)
        def _():
            xhat, _ = _rms_stats(x_ref[...])
            hb = (xhat * g_ref[...]).astype(BF16)
            h_scr[...] = hb
            h_ref[...] = hb
            acc[...] = jnp.zeros_like(acc)

        h = h_scr[...]
        gate = _dot(h, wg_ref[...])
        up = _dot(h, wu_ref[...])
        act = ((gate * _sigmoid(gate)) * up).astype(BF16)
        gate_ref[...] = gate.astype(BF16)
        up_ref[...] = up.astype(BF16)
        act_ref[...] = act
        acc[...] += _dot(act, wd_ref[...])

        @pl.when(j == ns - 1)
        def _():
            xo_ref[...] = x_ref[...] + FFN_RES_WEIGHT * acc[...]

    row = pl.BlockSpec((tm, D), lambda i, j: (i, 0))
    col_w = pl.BlockSpec((None, D, fs), lambda i, j: (j, 0, 0))
    saved = pl.BlockSpec((None, tm, fs), lambda i, j: (j, i, 0))
    return pl.pallas_call(
        body, name=name, grid=(T // tm, ns),
        in_specs=[row, pl.BlockSpec((1, D), lambda i, j: (0, 0)), col_w, col_w, pl.BlockSpec((None, fs, D), lambda i, j: (j, 0, 0))],
        out_specs=[row, row, saved, saved, saved],
        out_shape=[_sds((T, D), F32), _sds((T, D), BF16), _sds((ns, T, fs), BF16), _sds((ns, T, fs), BF16), _sds((ns, T, fs), BF16)],
        scratch_shapes=[pltpu.VMEM((tm, D), BF16), pltpu.VMEM((tm, D), F32)],
        compiler_params=_params(("parallel", "arbitrary")))(x, g, wg, wu, wd)


def _ffn_dx(dxo, x, g, gate_s, up_s, wg, wu, wd, name, tm=512):
    T, D = x.shape
    ns, _, fs = wg.shape

    def body(dxo_ref, x_ref, g_ref, gate_ref, up_ref, wg_ref, wu_ref, wd_ref, dx_ref, dff_ref, dgate_ref, dup_ref, dg_ref,
             dff_scr, dh_acc):
        i, j = pl.program_id(0), pl.program_id(1)

        @pl.when((i == 0) & (j == 0))
        def _():
            dg_ref[...] = jnp.zeros_like(dg_ref)

        @pl.when(j == 0)
        def _():
            d = (FFN_RES_WEIGHT * dxo_ref[...]).astype(BF16)
            dff_scr[...] = d
            dff_ref[...] = d
            dh_acc[...] = jnp.zeros_like(dh_acc)

        da = _dot_nt(dff_scr[...], wd_ref[...])
        gate = gate_ref[...].astype(F32)
        up = up_ref[...].astype(F32)
        s = _sigmoid(gate)
        silu = gate * s
        dup = (da * silu).astype(BF16)
        dgate = (da * up * (s * (1.0 + gate * (1.0 - s)))).astype(BF16)
        dgate_ref[...] = dgate
        dup_ref[...] = dup
        dh_acc[...] += _dot_nt(dgate, wg_ref[...]) + _dot_nt(dup, wu_ref[...])

        @pl.when(j == ns - 1)
        def _():
            dxn, dg = _rms_bwd(dh_acc[...], x_ref[...], g_ref[...])
            dg_ref[...] += dg
            dx_ref[...] = dxo_ref[...] + dxn

    row = pl.BlockSpec((tm, D), lambda i, j: (i, 0))
    col_w = pl.BlockSpec((None, D, fs), lambda i, j: (j, 0, 0))
    saved = pl.BlockSpec((None, tm, fs), lambda i, j: (j, i, 0))
    return pl.pallas_call(
        body, name=name, grid=(T // tm, ns),
        in_specs=[row, row, pl.BlockSpec((1, D), lambda i, j: (0, 0)), saved, saved, col_w, col_w,
                  pl.BlockSpec((None, fs, D), lambda i, j: (j, 0, 0))],
        out_specs=[row, row, saved, saved, pl.BlockSpec((SUBLANES, D), lambda i, j: (0, 0))],
        out_shape=[_sds((T, D), F32), _sds((T, D), BF16), _sds((ns, T, fs), BF16), _sds((ns, T, fs), BF16), _sds((SUBLANES, D), F32)],
        scratch_shapes=[pltpu.VMEM((tm, D), BF16), pltpu.VMEM((tm, D), F32)],
        compiler_params=_params(("arbitrary", "arbitrary")))(dxo, x, g, gate_s, up_s, wg, wu, wd)


def _tn(a, bs, mode, name, tk=1024, nb=None, dep=None):
    nbs = len(bs)
    if mode == "shard_b":
        T, M = a.shape
        G, _, N = bs[0].shape
        a_spec = pl.BlockSpec((tk, M), lambda g, t: (t, 0))
        b_spec = pl.BlockSpec((None, tk, N), lambda g, t: (g, t, 0))
        o_spec, o_shape = pl.BlockSpec((None, M, N), lambda g, t: (g, 0, 0)), (G, M, N)
    elif mode == "shard_a":
        G, T, M = a.shape
        N = bs[0].shape[1]
        a_spec = pl.BlockSpec((None, tk, M), lambda g, t: (g, t, 0))
        b_spec = pl.BlockSpec((tk, N), lambda g, t: (t, 0))
        o_spec, o_shape = pl.BlockSpec((None, M, N), lambda g, t: (g, 0, 0)), (G, M, N)
    else:
        T, M = a.shape
        N = nb
        G = bs[0].shape[1] // nb
        a_spec = pl.BlockSpec((tk, M), lambda g, t: (t, 0))
        b_spec = pl.BlockSpec((tk, N), lambda g, t: (t, g))
        o_spec, o_shape = pl.BlockSpec((M, N), lambda g, t: (0, g)), (M, G * nb)

    nt = T // tk

    def body(a_ref, *refs):
        b_refs, o_refs, ob_refs = refs[:nbs], refs[-2 * nbs:-nbs], refs[-nbs:]
        av = a_ref[...].astype(BF16)
        for b_ref, o_ref, ob_ref in zip(b_refs, o_refs, ob_refs):
            @pl.when(pl.program_id(1) == 0)
            def _():
                o_ref[...] = jnp.zeros_like(o_ref)

            o_ref[...] += _dot_tn(av, b_ref[...].astype(BF16))

            @pl.when(pl.program_id(1) == nt - 1)
            def _():
                ob_ref[...] = o_ref[...].astype(BF16)

    outs = pl.pallas_call(
        body, name=name, grid=(G, nt), in_specs=[a_spec] + [b_spec] * nbs + ([ANY] if dep is not None else []),
        out_specs=[o_spec] * (2 * nbs), out_shape=[_sds(o_shape, F32)] * nbs + [_sds(o_shape, BF16)] * nbs,
        compiler_params=_params(("parallel", "arbitrary")))(a, *bs, *([dep] if dep is not None else []))
    return list(zip(outs[:nbs], outs[nbs:]))


def _rope_tables(pos_col, inv_freq):
    T = pos_col.shape[0]

    def body(p_ref, f_ref, c_ref, s_ref):
        ang = p_ref[...].astype(F32) * f_ref[...]
        lane = lax.broadcasted_iota(jnp.int32, ang.shape, 1)
        c_ref[...] = jnp.cos(ang)
        sn = jnp.sin(ang)
        s_ref[...] = jnp.where((lane % HEAD_DIM) < HEAD_DIM // 2, -sn, sn)

    tm = 1024
    return pl.pallas_call(
        body, name="rope_tables", grid=(T // tm,),
        in_specs=[pl.BlockSpec((tm, 1), lambda i: (i, 0)), pl.BlockSpec((1, LANES), lambda i: (0, 0))],
        out_specs=[pl.BlockSpec((tm, LANES), lambda i: (i, 0))] * 2,
        out_shape=[_sds((T, LANES), F32)] * 2, compiler_params=_params(("parallel",)))(pos_col, inv_freq)


def _deinterleave(scr, out_ref, d, tm, nblk):
    for r in range(d):
        for cb in range(nblk):
            out_ref[r, :, cb * LANES:(cb + 1) * LANES] = scr[cb, pl.ds(r, tm // d, stride=d), :].astype(out_ref.dtype)


def _interleave(in_ref, scr, d, tm, nblk):
    for r in range(d):
        for cb in range(nblk):
            scr[cb, pl.ds(r, tm // d, stride=d), :] = in_ref[r, :, cb * LANES:(cb + 1) * LANES].astype(F32)


def _proj_rope(x, g, w_in, cos, sin, tm=512):
    T, D = x.shape
    dils = [d for _, d in B_PATTERNS if d > 1]
    nbb = B_W // LANES
    scale = HEAD_DIM ** -0.5
    cuts = [0, A_Q_W, A_Q_W + A_KV_W, A_Q_W + 2 * A_KV_W, A_Q_W + 2 * A_KV_W + B_W, A_Q_W + 2 * A_KV_W + 2 * B_W,
            A_Q_W + 2 * A_KV_W + 3 * B_W]

    def body(x_ref, g_ref, w_ref, c_ref, s_ref, h_ref, aq_ref, ak_ref, av_ref, *rest):
        b_refs, scr = rest[:-1], rest[-1]
        xhat, _ = _rms_stats(x_ref[...])
        h = (xhat * g_ref[...]).astype(BF16)
        h_ref[...] = h
        cs, sn = c_ref[...], s_ref[...]

        def seg(idx, rope, mult):
            lo, hi = cuts[idx], cuts[idx + 1]
            blocks = []
            for cb in range((hi - lo) // LANES):
                p = _dot(h, w_ref[:, lo + cb * LANES:lo + (cb + 1) * LANES])
                if rope:
                    p = p * cs + _swap32(p) * sn
                if mult != 1.0:
                    p = p * mult
                blocks.append(p)
            return blocks

        for idx, ref, rope, mult in ((0, aq_ref, True, scale), (1, ak_ref, True, 1.0), (2, av_ref, False, 1.0)):
            for cb, p in enumerate(seg(idx, rope, mult)):
                ref[:, cb * LANES:(cb + 1) * LANES] = p.astype(BF16)
        for which, (idx, rope, mult) in enumerate(((3, True, scale), (4, True, 1.0), (5, False, 1.0))):
            for cb, p in enumerate(seg(idx, rope, mult)):
                b_refs[which][:, cb * LANES:(cb + 1) * LANES] = p.astype(BF16)
                scr[cb] = p
            for di, d in enumerate(dils):
                _deinterleave(scr, b_refs[3 * (di + 1) + which], d, tm, nbb)

    row = lambda w: pl.BlockSpec((tm, w), lambda i: (i, 0))
    out_specs = [row(D), row(A_Q_W), row(A_KV_W), row(A_KV_W)] + [row(B_W)] * 3
    out_shape = [_sds((T, D), BF16), _sds((T, A_Q_W), BF16), _sds((T, A_KV_W), BF16), _sds((T, A_KV_W), BF16)] + [_sds((T, B_W), BF16)] * 3
    for d in dils:
        out_specs += [pl.BlockSpec((d, tm // d, B_W), lambda i: (0, i, 0))] * 3
        out_shape += [_sds((d, T // d, B_W), BF16)] * 3
    return pl.pallas_call(
        body, name="proj_rope", grid=(T // tm,),
        in_specs=[row(D), pl.BlockSpec((1, D), lambda i: (0, 0)), pl.BlockSpec(w_in.shape, lambda i: (0, 0)), row(LANES), row(LANES)],
        out_specs=out_specs, out_shape=out_shape, scratch_shapes=[pltpu.VMEM((nbb, tm, LANES), F32)],
        compiler_params=_params(("parallel",)))(x, g, w_in, cos, sin)


def _band_bias(rel, kw, hw):
    ri = lax.broadcasted_iota(jnp.int32, (QB, kw), 0)
    ci = lax.broadcasted_iota(jnp.int32, (QB, kw), 1)
    return jnp.where(jnp.abs(ri + rel - ci) <= hw, 0.0, NEG).astype(F32)


def _band_setup(bias_scr, kw, hw):
    if bias_scr is not None:
        for i in range(3):
            bias_scr[i] = _band_bias(i * hw, kw, hw)


def _band_window(bias_scr, qs, L, kw, hw):
    ws = pl.multiple_of(jnp.clip(qs - hw, 0, L - kw), 64)
    if bias_scr is None:
        return ws, _band_bias(qs - ws, kw, hw)
    return ws, bias_scr[lax.shift_right_logical(qs - ws, hw.bit_length() - 1)]


def _dup_kv_head(src_ref, dst_ref, head, L):
    step = min(L, 1024)
    for r0 in range(0, L, step):
        xf = src_ref[r0:r0 + step, :].astype(F32)
        lane = lax.broadcasted_iota(jnp.int32, xf.shape, 1)
        keep = jnp.logical_xor(lane < HEAD_DIM, head == 1)
        dst_ref[r0:r0 + step, :] = jnp.where(keep, xf, pltpu.roll(xf, HEAD_DIM, axis=1)).astype(dst_ref.dtype)


def _attn_fwd(q, k, v, sink, hw, gqa, out_dtype, name):
    NB, L, Cq = q.shape
    Ls = min(L, 2048)
    kw = min(QB + 2 * hw, L)
    tables = L >= QB + 2 * hw
    unroll = min(4, Ls // QB)

    def body(sink_ref, q_ref, k_ref, v_ref, o_ref, lse_ref, *scr):
        b, s_idx = pl.program_id(1), pl.program_id(2)
        bias_scr = scr[0] if tables else None
        _band_setup(bias_scr, kw, hw)
        if gqa:
            kd, vd = scr[-2:]

            @pl.when(s_idx == 0)
            def _():
                _dup_kv_head(k_ref, kd, b // 2, L)
                _dup_kv_head(v_ref, vd, b // 2, L)
        else:
            kd, vd = k_ref, v_ref
        lane = lax.broadcasted_iota(jnp.int32, (QB, LANES), 1)
        lo = lane < HEAD_DIM

        def block(ql):
            qs = s_idx * Ls + ql
            ws, bias = _band_window(bias_scr, qs, L, kw, hw)
            qv = q_ref[pl.ds(ql, QB), :]
            kv_, vv = kd[pl.ds(ws, kw), :], vd[pl.ds(ws, kw), :]
            res = []
            for half in (0, 1):
                qm = jnp.where(lo if half == 0 else jnp.logical_not(lo), qv, jnp.zeros_like(qv))
                s = _dot_nt(qm, kv_) + bias
                m = jnp.max(s, axis=-1, keepdims=True)
                if gqa:
                    sk = sink_ref[2 * b + half]
                    m = jnp.maximum(m, sk)
                p = jnp.exp(s - m)
                den = jnp.sum(p, axis=-1, keepdims=True)
                if gqa:
                    den = den + jnp.exp(sk - m)
                res.append((_dot(p.astype(BF16), vv) * (1.0 / den), m + jnp.log(den)))
            o_ref[pl.ds(ql, QB), :] = jnp.where(lo, res[0][0], res[1][0]).astype(o_ref.dtype)
            lse_ref[pl.ds(ql, QB), :] = jnp.where(lo, res[0][1], res[1][1])

        def step(n, carry):
            for u in range(unroll):
                block(pl.multiple_of((n * unroll + u) * QB, QB))
            return carry

        lax.fori_loop(0, Ls // (QB * unroll), step, 0)

    kv_map = (lambda r, b, s: (r, 0, 0)) if gqa else (lambda r, b, s: (r, 0, b))
    seg = pl.BlockSpec((None, Ls, LANES), lambda r, b, s: (r, s, b))
    return pl.pallas_call(
        body, name=name, grid=(NB, Cq // LANES, L // Ls),
        in_specs=[pl.BlockSpec(memory_space=pltpu.SMEM), seg, pl.BlockSpec((None, L, LANES), kv_map), pl.BlockSpec((None, L, LANES), kv_map)],
        out_specs=[seg, seg], out_shape=[_sds((NB, L, Cq), out_dtype), _sds((NB, L, Cq), F32)],
        scratch_shapes=([pltpu.VMEM((3, QB, kw), F32)] if tables else []) + ([pltpu.VMEM((L, LANES), BF16)] * 2 if gqa else []),
        compiler_params=_params(("parallel", "parallel", "arbitrary")))(sink, q, k, v)


def _attn_bwd(q, k, v, do, lse, delta, sink, hw, gqa, name):
    NB, L, Cq = q.shape
    Ck = k.shape[2]
    Ls = min(L, 2048)
    kw = min(QB + 2 * hw, L)
    reps = kw // LANES
    nseg = L // Ls
    scale = HEAD_DIM ** -0.5
    tables = L >= QB + 2 * hw
    unroll = min(4, Ls // QB)

    def body(sink_ref, q_ref, do_ref, lse_ref, dl_ref, k_ref, v_ref, dq_ref, dk_ref, dv_ref, dsk_ref, *scr):
        b, s_idx = pl.program_id(1), pl.program_id(2)
        lane = lax.broadcasted_iota(jnp.int32, (QB, LANES), 1)
        lo = lane < HEAD_DIM
        bias_scr = scr[0] if tables else None
        _band_setup(bias_scr, kw, hw)
        if gqa:
            kd, vd, dk_acc, dv_acc, dsk_acc = scr[-5:]

            @pl.when(s_idx == 0)
            def _():
                _dup_kv_head(k_ref, kd, b // 2, L)
                _dup_kv_head(v_ref, vd, b // 2, L)
                dk_acc[...] = jnp.zeros_like(dk_acc)
                dv_acc[...] = jnp.zeros_like(dv_acc)
                dsk_acc[...] = jnp.zeros_like(dsk_acc)

            @pl.when((s_idx == 0) & (b == 0))
            def _():
                dk_ref[...] = jnp.zeros_like(dk_ref)
                dv_ref[...] = jnp.zeros_like(dv_ref)
        else:
            kd, vd, dk_acc, dv_acc = k_ref, v_ref, dk_ref, dv_ref

            @pl.when(s_idx == 0)
            def _():
                dk_ref[...] = jnp.zeros_like(dk_ref)
                dv_ref[...] = jnp.zeros_like(dv_ref)

        def block(ql):
            qs = s_idx * Ls + ql
            ws, bias = _band_window(bias_scr, qs, L, kw, hw)
            qv, dov = q_ref[pl.ds(ql, QB), :], do_ref[pl.ds(ql, QB), :]
            lse, dl = lse_ref[pl.ds(ql, QB), :], dl_ref[pl.ds(ql, QB), :]
            kv_, vv = kd[pl.ds(ws, kw), :], vd[pl.ds(ws, kw), :]
            lse_sw, dl_sw = pltpu.roll(lse, HEAD_DIM, axis=1), pltpu.roll(dl, HEAD_DIM, axis=1)
            dqs = []
            dk_c = jnp.zeros((kw, LANES), F32)
            dv_c = jnp.zeros((kw, LANES), F32)
            for half in (0, 1):
                msk = lo if half == 0 else jnp.logical_not(lo)
                qm = jnp.where(msk, qv, jnp.zeros_like(qv))
                dom = jnp.where(msk, dov, jnp.zeros_like(dov))
                lse_h = jnp.where(msk, lse, lse_sw)
                dl_h = jnp.where(msk, dl, dl_sw)
                s = _dot_nt(qm, kv_) + bias
                p = jnp.exp(s - jnp.tile(lse_h, (1, reps)))
                dp = _dot_nt(dom, vv)
                ds = (p * (dp - jnp.tile(dl_h, (1, reps)))).astype(BF16)
                dqs.append(_dot(ds, kv_))
                dk_c = dk_c + _dot_tn(ds, qm)
                dv_c = dv_c + _dot_tn(p.astype(BF16), dom)
            dq_ref[pl.ds(ql, QB), :] = (jnp.where(lo, dqs[0], dqs[1]) * scale).astype(dq_ref.dtype)
            dk_acc[pl.ds(ws, kw), :] += dk_c
            dv_acc[pl.ds(ws, kw), :] += dv_c
            if gqa:
                sk = jnp.where(lo, sink_ref[2 * b], sink_ref[2 * b + 1])
                dsk_acc[...] += -jnp.exp(sk - lse) * dl

        def step(n, carry):
            for u in range(unroll):
                block(pl.multiple_of((n * unroll + u) * QB, QB))
            return carry

        lax.fori_loop(0, Ls // (QB * unroll), step, 0)

        if gqa:
            @pl.when(s_idx == nseg - 1)
            def _():
                step_rows = min(L, 1024)
                for r0 in range(0, L, step_rows):
                    lanek = lax.broadcasted_iota(jnp.int32, (step_rows, LANES), 1)
                    mine = jnp.logical_xor(lanek < HEAD_DIM, (b // 2) == 1)
                    for acc, ref in ((dk_acc, dk_ref), (dv_acc, dv_ref)):
                        a = acc[r0:r0 + step_rows, :]
                        ref[r0:r0 + step_rows, :] += jnp.where(mine, a + pltpu.roll(a, HEAD_DIM, axis=1), 0.0)
                dsk_ref[...] = dsk_acc[...].reshape(QB // SUBLANES, SUBLANES, LANES).sum(axis=0)
        else:
            dsk_ref[...] = jnp.zeros_like(dsk_ref)

    kv_map = (lambda r, b, s: (r, 0, 0)) if gqa else (lambda r, b, s: (r, 0, b))
    seg = pl.BlockSpec((None, Ls, LANES), lambda r, b, s: (r, s, b))
    full = pl.BlockSpec((None, L, LANES), kv_map)
    scratch = [pltpu.VMEM((3, QB, kw), F32)] if tables else []
    if gqa:
        scratch += [pltpu.VMEM((L, LANES), BF16)] * 2 + [pltpu.VMEM((L, LANES), F32)] * 2 + [pltpu.VMEM((QB, LANES), F32)]
    return pl.pallas_call(
        body, name=name, grid=(NB, Cq // LANES, nseg),
        in_specs=[pl.BlockSpec(memory_space=pltpu.SMEM), seg, seg, seg, seg, full, full],
        out_specs=[seg, full, full, pl.BlockSpec((None, None, SUBLANES, LANES), lambda r, b, s: (r, b, 0, 0))],
        out_shape=[_sds((NB, L, Cq), BF16), _sds((NB, L, Ck), F32), _sds((NB, L, Ck), F32),
                   _sds((NB, Cq // LANES, SUBLANES, LANES), F32)],
        scratch_shapes=scratch,
        compiler_params=_params(("arbitrary", "arbitrary", "arbitrary")))(sink, q, do, lse, delta, k, v)


def _merge_b(a_out, o1, l1, o4, l4, o16, l16, tm=512):
    T = a_out.shape[0]
    nbb = B_W // LANES

    def body(a_ref, o1_ref, l1_ref, o4_ref, l4_ref, o16_ref, l16_ref, cat_ref, lg1_ref, lg4_ref, lg16_ref, so, sl, slg):
        _interleave(o4_ref, so.at[0], 4, tm, nbb)
        _interleave(l4_ref, sl.at[0], 4, tm, nbb)
        _interleave(o16_ref, so.at[1], 16, tm, nbb)
        _interleave(l16_ref, sl.at[1], 16, tm, nbb)
        cat_ref[:, 0:A_Q_W] = a_ref[...]
        for cb in range(nbb):
            cols = slice(cb * LANES, (cb + 1) * LANES)
            os_ = (o1_ref[:, cols], so[0, cb], so[1, cb])
            ls_ = (l1_ref[:, cols], sl[0, cb], sl[1, cb])
            m = jnp.maximum(jnp.maximum(ls_[0], ls_[1]), ls_[2])
            es = [jnp.exp(l - m) for l in ls_]
            den = es[0] + es[1] + es[2]
            out = (es[0] * os_[0] + es[1] * os_[1] + es[2] * os_[2]) * (1.0 / den)
            lg = m + jnp.log(den)
            cat_ref[:, A_Q_W + cb * LANES:A_Q_W + (cb + 1) * LANES] = out.astype(BF16)
            lg1_ref[:, cols] = lg
            slg[cb] = lg
        _deinterleave(slg, lg4_ref, 4, tm, nbb)
        _deinterleave(slg, lg16_ref, 16, tm, nbb)

    row = lambda w: pl.BlockSpec((tm, w), lambda i: (i, 0))
    perm = lambda d: pl.BlockSpec((d, tm // d, B_W), lambda i: (0, i, 0))
    return pl.pallas_call(
        body, name="merge_patterns", grid=(T // tm,),
        in_specs=[row(A_Q_W), row(B_W), row(B_W), perm(4), perm(4), perm(16), perm(16)],
        out_specs=[row(A_Q_W + B_W), row(B_W), perm(4), perm(16)],
        out_shape=[_sds((T, A_Q_W + B_W), BF16), _sds((T, B_W), F32), _sds((4, T // 4, B_W), F32), _sds((16, T // 16, B_W), F32)],
        scratch_shapes=[pltpu.VMEM((2, nbb, tm, LANES), F32), pltpu.VMEM((2, nbb, tm, LANES), F32), pltpu.VMEM((nbb, tm, LANES), F32)],
        compiler_params=_params(("parallel",)))(a_out, o1, l1, o4, l4, o16, l16)


def _out_proj(x, cat, w_out, tm=512):
    T, D = x.shape

    def body(x_ref, c_ref, w_ref, o_ref):
        o_ref[...] = x_ref[...] + _dot(c_ref[...], w_ref[...])

    row = lambda w: pl.BlockSpec((tm, w), lambda i: (i, 0))
    return pl.pallas_call(
        body, name="out_proj", grid=(T // tm,), in_specs=[row(D), row(cat.shape[1]), pl.BlockSpec(w_out.shape, lambda i: (0, 0))],
        out_specs=row(D), out_shape=_sds((T, D), F32), compiler_params=_params(("parallel",)))(x, cat, w_out)


def _final_loss(x, g, target, tm=512):
    T, D = x.shape

    def body(x_ref, g_ref, t_ref, dx_ref, dg_ref, loss_ref):
        @pl.when(pl.program_id(0) == 0)
        def _():
            dg_ref[...] = jnp.zeros_like(dg_ref)
            loss_ref[...] = jnp.zeros_like(loss_ref)

        xv, gv = x_ref[...], g_ref[...]
        xhat, _ = _rms_stats(xv)
        err = xhat * gv - t_ref[...]
        loss_ref[...] += 0.5 * jnp.sum(jnp.sum(err * err, axis=-1, keepdims=True) * (1.0 / D), axis=0, keepdims=True)
        dx, dg = _rms_bwd(err * (1.0 / D), xv, gv)
        dx_ref[...] = dx
        dg_ref[...] += dg

    row = pl.BlockSpec((tm, D), lambda i: (i, 0))
    return pl.pallas_call(
        body, name="final_loss", grid=(T // tm,), in_specs=[row, pl.BlockSpec((1, D), lambda i: (0, 0)), row],
        out_specs=[row, pl.BlockSpec((SUBLANES, D), lambda i: (0, 0)), pl.BlockSpec((SUBLANES, LANES), lambda i: (0, 0))],
        out_shape=[_sds((T, D), F32), _sds((SUBLANES, D), F32), _sds((SUBLANES, LANES), F32)],
        compiler_params=_params(("arbitrary",)))(x, g, target)


def _dcat(dx, w_out, cat, tm=512):
    T, D = dx.shape
    C = cat.shape[1]
    nba, nbb = A_Q_W // LANES, B_W // LANES

    def body(dx_ref, w_ref, cat_ref, doa_ref, dla_ref, dob1_ref, dlb1_ref, dob4_ref, dlb4_ref, dob16_ref, dlb16_ref, sdo, sdl):
        dc = _dot_nt(dx_ref[...].astype(BF16), w_ref[...])
        ri = lax.broadcasted_iota(jnp.int32, (LANES, LANES), 0)
        ci = lax.broadcasted_iota(jnp.int32, (LANES, LANES), 1)
        same_head = ((ri // HEAD_DIM) == (ci // HEAD_DIM)).astype(BF16)
        for cb in range(C // LANES):
            cols = slice(cb * LANES, (cb + 1) * LANES)
            blk = dc[:, cols]
            prod = blk * cat_ref[:, cols].astype(F32)
            hi = prod.astype(BF16)
            lo_ = (prod - hi.astype(F32)).astype(BF16)
            dl = _dot(hi, same_head) + _dot(lo_, same_head)
            if cb < nba:
                doa_ref[:, cols] = blk.astype(BF16)
                dla_ref[:, cols] = dl
            else:
                bcols = slice((cb - nba) * LANES, (cb - nba + 1) * LANES)
                dob1_ref[:, bcols] = blk.astype(BF16)
                dlb1_ref[:, bcols] = dl
                sdo[cb - nba] = blk
                sdl[cb - nba] = dl
        _deinterleave(sdo, dob4_ref, 4, tm, nbb)
        _deinterleave(sdl, dlb4_ref, 4, tm, nbb)
        _deinterleave(sdo, dob16_ref, 16, tm, nbb)
        _deinterleave(sdl, dlb16_ref, 16, tm, nbb)

    row = lambda w: pl.BlockSpec((tm, w), lambda i: (i, 0))
    perm = lambda d: pl.BlockSpec((d, tm // d, B_W), lambda i: (0, i, 0))
    return pl.pallas_call(
        body, name="dcat", grid=(T // tm,), in_specs=[row(D), pl.BlockSpec(w_out.shape, lambda i: (0, 0)), row(C)],
        out_specs=[row(A_Q_W), row(A_Q_W), row(B_W), row(B_W), perm(4), perm(4), perm(16), perm(16)],
        out_shape=[_sds((T, A_Q_W), BF16), _sds((T, A_Q_W), F32), _sds((T, B_W), BF16), _sds((T, B_W), F32),
                   _sds((4, T // 4, B_W), BF16), _sds((4, T // 4, B_W), F32), _sds((16, T // 16, B_W), BF16), _sds((16, T // 16, B_W), F32)],
        scratch_shapes=[pltpu.VMEM((nbb, tm, LANES), F32)] * 2, compiler_params=_params(("parallel",)))(dx, w_out, cat)


def _rope_bwd_assemble(dqa, dka, dva, b1, b4, b16, cos, sin, tm=512):
    T = dqa.shape[0]
    nbb = B_W // LANES
    width = A_Q_W + 2 * A_KV_W + 3 * B_W

    def body(dqa_ref, dka_ref, dva_ref, q1, k1, v1, q4, k4, v4, q16, k16, v16, c_ref, s_ref, o_ref, scr):
        cs, sn = c_ref[...], s_ref[...]

        def unrope(t):
            return t * cs + _swap32(t * sn)

        col = 0
        for ref, rope in ((dqa_ref, True), (dka_ref, True), (dva_ref, False)):
            for cb in range(ref.shape[1] // LANES):
                t = ref[:, cb * LANES:(cb + 1) * LANES].astype(F32)
                o_ref[:, col:col + LANES] = (unrope(t) if rope else t).astype(BF16)
                col += LANES
        for which, (r1, r4, r16, rope) in enumerate(((q1, q4, q16, True), (k1, k4, k16, True), (v1, v4, v16, False))):
            _interleave(r4, scr.at[0], 4, tm, nbb)
            _interleave(r16, scr.at[1], 16, tm, nbb)
            for cb in range(nbb):
                t = r1[:, cb * LANES:(cb + 1) * LANES].astype(F32) + scr[0, cb] + scr[1, cb]
                o_ref[:, col:col + LANES] = (unrope(t) if rope else t).astype(BF16)
                col += LANES

    row = lambda w: pl.BlockSpec((tm, w), lambda i: (i, 0))
    perm = lambda d: pl.BlockSpec((d, tm // d, B_W), lambda i: (0, i, 0))
    return pl.pallas_call(
        body, name="rope_bwd", grid=(T // tm,),
        in_specs=[row(A_Q_W), row(A_KV_W), row(A_KV_W)] + [row(B_W)] * 3 + [perm(4)] * 3 + [perm(16)] * 3 + [row(LANES), row(LANES)],
        out_specs=row(width), out_shape=_sds((T, width), BF16), scratch_shapes=[pltpu.VMEM((2, nbb, tm, LANES), F32)],
        compiler_params=_params(("parallel",)))(dqa, dka, dva, *b1, *b4, *b16, cos, sin)


def _dh_norm(dproj, w_in, x, g, dres, tm=512):
    T, D = x.shape

    def body(dp_ref, w_ref, x_ref, g_ref, dr_ref, dx_ref, dg_ref):
        @pl.when(pl.program_id(0) == 0)
        def _():
            dg_ref[...] = jnp.zeros_like(dg_ref)

        dxn, dg = _rms_bwd(_dot_nt(dp_ref[...], w_ref[...]), x_ref[...], g_ref[...])
        dg_ref[...] += dg
        dx_ref[...] = dr_ref[...] + dxn

    row = lambda w: pl.BlockSpec((tm, w), lambda i: (i, 0))
    return pl.pallas_call(
        body, name="dh_norm", grid=(T // tm,),
        in_specs=[row(dproj.shape[1]), pl.BlockSpec(w_in.shape, lambda i: (0, 0)), row(D), pl.BlockSpec((1, D), lambda i: (0, 0)), row(D)],
        out_specs=[row(D), pl.BlockSpec((SUBLANES, D), lambda i: (0, 0))],
        out_shape=[_sds((T, D), F32), _sds((SUBLANES, D), F32)], compiler_params=_params(("arbitrary",)))(dproj, w_in, x, g, dres)


def _grad_push_plan(n):
    def plan(refs):
        x, y, c = _mesh_pos()
        return [(refs[k].at[chip], refs[n + k].at[rel], dev) for k in range(n) for rel, (dev, chip) in enumerate(_chip_peers(x, y, c))]
    return plan


def _sum_own(me_arr, g, landed, name):
    ns, R, C = g.shape
    tr = R // 2 if (R // 2) % 16 == 0 else R

    def body(me_ref, g_ref, x_ref, o_ref):
        acc = g_ref[...]
        for rel in range(ns - 1):
            acc = acc + x_ref[rel].astype(F32)
        o_ref[...] = acc

    grid_spec = pltpu.PrefetchScalarGridSpec(
        num_scalar_prefetch=1, grid=(R // tr,),
        in_specs=[pl.BlockSpec((None, tr, C), lambda t, me: (me[0], t, 0)), pl.BlockSpec((ns - 1, tr, C), lambda t, me: (0, t, 0))],
        out_specs=pl.BlockSpec((tr, C), lambda t, me: (t, 0)))
    return pl.pallas_call(body, name=name, grid_spec=grid_spec, out_shape=_sds((R, C), F32),
                          compiler_params=_params(("parallel",)))(me_arr, g, landed)


def _pair_swap(ps, name):
    n = len(ps)

    def body(*refs):
        ins, outs = refs[:n], refs[n:2 * n]
        send, recv = refs[2 * n:]
        x, y, c = _mesh_pos()
        cps = []
        for k in range(n):
            cp = pltpu.make_async_remote_copy(src_ref=ins[k], dst_ref=outs[k], send_sem=send.at[k], recv_sem=recv.at[k],
                                              device_id=(x, y, 1 - c), device_id_type=MESH)
            cp.start()
            cps.append(cp)
        for cp in cps:
            cp.wait()

    return pl.pallas_call(
        body, name=name, out_shape=[_sds(p.shape, F32) for p in ps], in_specs=[ANY] * n, out_specs=[ANY] * n,
        scratch_shapes=[pltpu.SemaphoreType.DMA((n,)), pltpu.SemaphoreType.DMA((n,))], compiler_params=_params())(*ps)


def _allreduce_small(v):
    rows, W = v.shape

    def body(v_ref, o_ref, buf, send, recv):
        x, y, c = _mesh_pos()
        me = 4 * x + 2 * y + c
        cps = []
        for m in range(1, N_DEV):
            dev = (x ^ (m >> 2), y ^ ((m >> 1) & 1), c ^ (m & 1))
            cp = pltpu.make_async_remote_copy(src_ref=v_ref, dst_ref=buf.at[me], send_sem=send.at[m - 1], recv_sem=recv.at[m - 1],
                                              device_id=dev, device_id_type=MESH)
            cp.start()
            cps.append(cp)
        for m in range(1, N_DEV):
            pltpu.make_async_remote_copy(src_ref=v_ref, dst_ref=buf.at[me ^ m], send_sem=send.at[m - 1], recv_sem=recv.at[m - 1],
                                         device_id=(x, y, c), device_id_type=MESH).wait_recv()
        for cp in cps:
            cp.wait_send()
        buf[me] = v_ref[...]
        acc = buf[0]
        for i in range(1, N_DEV):
            acc = acc + buf[i]
        o_ref[...] = acc

    return pl.pallas_call(
        body, name="allreduce_small", out_shape=_sds((rows, W), F32),
        scratch_shapes=[pltpu.VMEM((N_DEV, rows, W), F32), pltpu.SemaphoreType.DMA((N_DEV - 1,)), pltpu.SemaphoreType.DMA((N_DEV - 1,))],
        compiler_params=_params())(v)


def _adamw(w, gp, gq, m, v, name):
    R, C = w.shape
    tr = R // 2 if (R // 2) % SUBLANES == 0 else R
    c1 = 1.0 / (1.0 - ADAM_B1 ** ADAM_STEP)
    c2 = 1.0 / (1.0 - ADAM_B2 ** ADAM_STEP)

    def body(w_ref, gp_ref, gq_ref, m_ref, v_ref, g_ref, d_ref, nm_ref, nv_ref):
        gv = gp_ref[...] + gq_ref[...]
        nm = ADAM_B1 * m_ref[...] + (1.0 - ADAM_B1) * gv
        nv = ADAM_B2 * v_ref[...] + (1.0 - ADAM_B2) * (gv * gv)
        g_ref[...] = gv
        d_ref[...] = -ADAM_LR * ((nm * c1) / (jnp.sqrt(nv * c2) + ADAM_EPS) + ADAM_WD * w_ref[...])
        nm_ref[...] = nm
        nv_ref[...] = nv

    blk = pl.BlockSpec((tr, C), lambda t: (t, 0))
    return pl.pallas_call(body, name=name, grid=(R // tr,), in_specs=[blk] * 5, out_specs=[blk] * 4,
                          out_shape=[_sds((R, C), F32)] * 4, compiler_params=_params(("parallel",)))(w, gp, gq, m, v)


def _local_step(x, positions, target, norms, a_sink, comm):
    T, D = x.shape
    g1, gm, g2, gf = norms
    inv_freq = 1.0 / (ROPE_THETA ** (jnp.arange(0, HEAD_DIM, 2, dtype=F32) / HEAD_DIM))
    inv_freq = jnp.tile(inv_freq, LANES // (HEAD_DIM // 2)).reshape(1, LANES)
    cos, sin = _rope_tables(positions.reshape(T, 1), inv_freq)
    no_sink = jnp.zeros((2 * (B_W // LANES),), F32)
    W = {k: comm.weight(k, x) for k in ("wg1", "wu1", "wd1")}

    x1, h1, gate1, up1, act1 = _ffn_fwd(x, comm.order(g1), W["wg1"], W["wu1"], W["wd1"], "ffn1_fwd")
    W["w_in"] = comm.weight("w_in", x1)
    (h2, aq, ak, av, bq1, bk1, bv1, bq4, bk4, bv4, bq16, bk16, bv16) = _proj_rope(x1, gm, W["w_in"], cos, sin)
    a_out, a_lse = _attn_fwd(aq[None], ak[None], av[None], a_sink, A_HALF_WINDOW, True, BF16, "attn_a_fwd")
    bqs = {1: (bq1[None], bk1[None], bv1[None]), 4: (bq4, bk4, bv4), 16: (bq16, bk16, bv16)}
    b_o, b_l = {}, {}
    for w, d in B_PATTERNS:
        q_, k_, v_ = bqs[d]
        b_o[d], b_l[d] = _attn_fwd(q_, k_, v_, no_sink, w // (2 * d), False, F32, f"attn_b{d}_fwd")
    cat, lg1, lg4, lg16 = _merge_b(a_out[0], b_o[1][0], b_l[1][0], b_o[4], b_l[4], b_o[16], b_l[16])
    W["w_out"] = comm.weight("w_out", cat)
    x2 = _out_proj(x1, cat, W["w_out"])
    for k in ("wg2", "wu2", "wd2"):
        W[k] = comm.weight(k, x2)
    x3, h3, gate2, up2, act2 = _ffn_fwd(x2, g2, W["wg2"], W["wu2"], W["wd2"], "ffn2_fwd")

    dx3, dgf, loss8 = _final_loss(x3, gf, target)
    dx2, dff2, dgate2, dup2, dg2 = _ffn_dx(dx3, x2, g2, gate2, up2, W["wg2"], W["wu2"], W["wd2"], "ffn2_dx")
    dwg2, dwu2 = _tn(h3, [dgate2, dup2], "shard_b", "ffn2_dw_in")
    (dwd2,) = _tn(act2, [dff2], "shard_a", "ffn2_dw_down")
    comm.ready(dict(wg2=dwg2, wu2=dwu2, wd2=dwd2), dwd2[0])

    doa, dla, dob1, dlb1, dob4, dlb4, dob16, dlb16 = _dcat(dx2, W["w_out"], cat)
    (dw_out,) = _tn(cat, [dx2], "nblock", "w_out_dw", nb=D, dep=comm.dep())
    dqa, dka, dva, dsk = _attn_bwd(aq[None], ak[None], av[None], doa[None], a_lse, dla[None], comm.order(a_sink), A_HALF_WINDOW, True,
                                   "attn_a_bwd")
    bwd_in = {1: (dob1[None], lg1[None], dlb1[None]), 4: (dob4, lg4, dlb4), 16: (dob16, lg16, dlb16)}
    bg = {}
    for w, d in B_PATTERNS:
        q_, k_, v_ = bqs[d]
        do_, l_, dl_ = bwd_in[d]
        bg[d] = _attn_bwd(q_, k_, v_, do_, l_, dl_, no_sink, w // (2 * d), False, f"attn_b{d}_bwd")[:3]
    dproj = _rope_bwd_assemble(dqa[0], dka[0], dva[0], [t[0] for t in bg[1]], bg[4], bg[16], cos, sin)
    (dw_in,) = _tn(h2, [dproj], "nblock", "w_in_dw", nb=dproj.shape[1] // 2)
    comm.ready(dict(w_in=dw_in, w_out=dw_out), dw_in[0])
    dx1, dgm = _dh_norm(dproj, W["w_in"], x1, comm.order(gm), dx2)

    dx0, dff1, dgate1, dup1, dg1 = _ffn_dx(dx1, x, g1, gate1, up1, W["wg1"], W["wu1"], W["wd1"], "ffn1_dx")
    (dwd1,) = _tn(act1, [dff1], "shard_a", "ffn1_dw_down")
    comm.ready(dict(wd1=dwd1), dwd1[0])
    (dwg1,) = _tn(h1, [dgate1], "shard_b", "ffn1_dw_gate", dep=comm.dep())
    comm.ready(dict(wg1=dwg1), dwg1[0])
    (dwu1,) = _tn(h1, [dup1], "shard_b", "ffn1_dw_up", dep=comm.dep())
    comm.ready(dict(wu1=dwu1), dwu1[0])

    dsink = dsk[0, :, :, ::HEAD_DIM].sum(axis=1).reshape(-1)
    small = dict(g1=dg1.sum(axis=0), gm=dgm.sum(axis=0), g2=dg2.sum(axis=0), gf=dgf.sum(axis=0), sink=dsink, loss=loss8[0, 0])
    return dx0, small


BIG = ("wg1", "wu1", "wd1", "w_in", "w_out", "wg2", "wu2", "wd2")
GATHER_GROUPS = (("w_in",), ("w_out",), ("wg2", "wu2", "wd2"))


class _Comm:
    def __init__(self, shards):
        x, y, c = _mesh_pos()
        self.me = (2 * x + y).astype(jnp.int32).reshape(1)
        self.shards = shards
        self.tokens = []
        self.waiting = {}
        self.groups = []
        fulls = {k: _cast_place(self.me, shards[k], f"cast_{k}") for k in BIG}
        first = ("wg1", "wu1", "wd1")
        self.full = dict(zip(first, _gather_weights([fulls[k] for k in first])))
        dep = self.full["wd1"]
        for gi, names in enumerate(GATHER_GROUPS):
            plan = _gather_plan(len(names))
            send, recv, bufs, tok = _push_start(f"gather_start_{gi}", [fulls[k] for k in names], 3 * len(names), plan, dep)
            self.tokens.append(tok)
            dep = tok
            for k in names:
                self.waiting[k] = (gi, names, send, recv, bufs, plan)

    def order(self, a):
        for tok in self.tokens:
            a = a + tok[0, 0]
        self.tokens = []
        return a

    def dep(self):
        return self.tokens[-1] if self.tokens else None

    def weight(self, name, after):
        if name in self.waiting:
            gi, names, send, recv, bufs, plan = self.waiting[name]
            for k, buf in zip(names, _push_wait(f"gather_wait_{gi}", send, recv, bufs, plan, after)):
                self.full[k] = buf
                del self.waiting[k]
        full = self.full[name]
        if name == "w_in":
            return jnp.concatenate([full[j] for j in range(N_CHIPS)], axis=1)
        if name == "w_out":
            return full.reshape(N_CHIPS * full.shape[1], full.shape[2])
        return full

    def ready(self, grads, after):
        names = list(grads)
        f32s, b16s = [], []
        for k in names:
            gf, gb = grads[k]
            if k == "w_in":
                cols = self.shards[k].shape[1]
                gf = jnp.stack([gf[:, j * cols:(j + 1) * cols] for j in range(N_CHIPS)], axis=0)
                gb = jnp.stack([gb[:, j * cols:(j + 1) * cols] for j in range(N_CHIPS)], axis=0)
            if k == "w_out":
                gf, gb = gf.reshape((N_CHIPS,) + self.shards[k].shape), gb.reshape((N_CHIPS,) + self.shards[k].shape)
            f32s.append(gf)
            b16s.append(gb)
        n = len(names)
        lands = [lax.empty((N_CHIPS - 1,) + self.shards[k].shape, BF16) for k in names]
        plan = _grad_push_plan(n)
        gi = len(self.groups)
        send, recv, bufs, tok = _push_start(f"grad_start_{gi}", b16s + lands, 3 * n, plan, after)
        self.tokens.append(tok)
        self.groups.append((names, f32s, send, recv, bufs, plan))

    def finish(self):
        out = {}
        after = self.tokens[-1]
        for gi, (names, f32s, send, recv, bufs, plan) in enumerate(self.groups):
            n = len(names)
            bufs = _push_wait(f"grad_wait_{gi}", send, recv, bufs, plan, after)
            mine = [_sum_own(self.me, f32s[i], bufs[n + i], f"sum_{k}") for i, k in enumerate(names)]
            theirs = _pair_swap(mine, f"grad_pair_swap_{gi}")
            for k, p, q in zip(names, mine, theirs):
                out[k] = (p, q)
            after = theirs[-1]
        return out


def kernel(x, positions, norm_ffn1, w_gate1, w_up1, w_down1, norm_mix, w_in, a_sink, w_out, norm_ffn2, w_gate2, w_up2, w_down2, norm_final, loss_target, m_norm_ffn1, m_w_gate1, m_w_up1, m_w_down1, m_norm_mix, m_w_in, m_a_sink, m_w_out, m_norm_ffn2, m_w_gate2, m_w_up2, m_w_down2, m_norm_final, v_norm_ffn1, v_w_gate1, v_w_up1, v_w_down1, v_norm_mix, v_w_in, v_a_sink, v_w_out, v_norm_ffn2, v_w_gate2, v_w_up2, v_w_down2, v_norm_final):
    T, D = x.shape[1], x.shape[2]
    shards = dict(wg1=w_gate1[0], wu1=w_up1[0], wd1=w_down1[0], w_in=w_in[0], w_out=w_out[0], wg2=w_gate2[0], wu2=w_up2[0], wd2=w_down2[0])
    moms = dict(wg1=(m_w_gate1, v_w_gate1), wu1=(m_w_up1, v_w_up1), wd1=(m_w_down1, v_w_down1), w_in=(m_w_in, v_w_in),
                w_out=(m_w_out, v_w_out), wg2=(m_w_gate2, v_w_gate2), wu2=(m_w_up2, v_w_up2), wd2=(m_w_down2, v_w_down2))

    comm = _Comm(shards)

    norms = (norm_ffn1, norm_mix, norm_ffn2, norm_final.reshape(1, D))
    grad_x, small = _local_step(x[0], positions[0], loss_target[0], norms, a_sink[0], comm)

    partial = comm.finish()

    def pad_row(a):
        a = a.reshape(-1)
        return jnp.pad(a, (0, D - a.shape[0]))

    row4 = pad_row(jnp.concatenate([small["sink"], small["loss"].reshape(1)]))
    vec = jnp.stack([small["g1"], small["gm"], small["g2"], small["gf"], row4] + [jnp.zeros((D,), F32)] * 3, axis=0)
    red = _allreduce_small(vec)
    loss = red[4, 8]
    g_small = jnp.stack([red[0], red[1], red[2], red[3], pad_row(red[4, 0:8])] + [jnp.zeros((D,), F32)] * 3, axis=0)

    def small_stack(a1, am, a2, af, ask):
        return jnp.stack([pad_row(a1), pad_row(am), pad_row(a2), pad_row(af), pad_row(ask)] + [jnp.zeros((D,), F32)] * 3, axis=0)

    w_small = small_stack(norm_ffn1, norm_mix, norm_ffn2, norm_final, a_sink)
    m_small = small_stack(m_norm_ffn1, m_norm_mix, m_norm_ffn2, m_norm_final, m_a_sink)
    v_small = small_stack(v_norm_ffn1, v_norm_mix, v_norm_ffn2, v_norm_final, v_a_sink)
    live = small_stack(jnp.ones_like(norm_ffn1), jnp.ones_like(norm_mix), jnp.ones_like(norm_ffn2), jnp.ones_like(norm_final), jnp.ones_like(a_sink))
    v_small = jnp.where(live > 0, v_small, 1.0)

    upd = {}
    for k in BIG:
        m_, v_ = moms[k]
        shp = m_.shape
        upd[k] = tuple(a.reshape(shp) for a in _adamw(shards[k], partial[k][0], partial[k][1], m_[0], v_[0], f"adamw_{k}"))
    _, ds_, nms_, nvs_ = _adamw(w_small, g_small, jnp.zeros_like(g_small), m_small, v_small, "adamw_small")

    def small_out(arr):
        return [arr[0].reshape(1, D), arr[1].reshape(1, D), arr[2].reshape(1, D), arr[3], arr[4, 0:8].reshape(1, 8)]

    gs_, dss, nmss, nvss = small_out(g_small), small_out(ds_), small_out(nms_), small_out(nvs_)

    def ordered(i):
        sm = (gs_, dss, nmss, nvss)[i]
        return [sm[0], upd["wg1"][i], upd["wu1"][i], upd["wd1"][i], sm[1], upd["w_in"][i], sm[4], upd["w_out"][i], sm[2],
                upd["wg2"][i], upd["wu2"][i], upd["wd2"][i], sm[3]]

    return (loss, grad_x[None], *ordered(0), *ordered(1), *ordered(2), *ordered(3))
```

```python
import jax
import jax.numpy as jnp
from jax import lax
from jax.experimental import pallas as pl
from jax.experimental.pallas import tpu as pltpu

F32 = jnp.float32
BF16 = jnp.bfloat16

HEAD_DIM = 64
LANES = 128
SUBLANES = 8
A_Q_W, A_KV_W, B_W = 512, 128, 512
A_HALF_WINDOW = 128
B_PATTERNS = ((128, 1), (512, 4), (2048, 16))
ROPE_THETA = 10000.0
NORM_EPS = 1e-6
FFN_RES_WEIGHT = 0.5
ADAM_LR, ADAM_B1, ADAM_B2, ADAM_EPS, ADAM_WD, ADAM_STEP = 0.001, 0.9, 0.999, 1e-08, 0.01, 10
N_CHIPS = 4
N_DEV = 8
QB = 128
NEG = -1e30
VMEM_LIMIT = 56 * 1024 * 1024
MESH = pl.DeviceIdType.MESH
ANY = pl.BlockSpec(memory_space=pl.ANY)


def _params(sem=None):
    return pltpu.CompilerParams(dimension_semantics=sem, vmem_limit_bytes=VMEM_LIMIT)


def _sds(shape, dtype):
    return jax.ShapeDtypeStruct(tuple(shape), dtype)


def _dot(a, b):
    return jnp.dot(a, b, preferred_element_type=F32)


def _dot_nt(a, b):
    return lax.dot_general(a, b, (((1,), (1,)), ((), ())), preferred_element_type=F32)


def _dot_tn(a, b):
    return lax.dot_general(a, b, (((0,), (0,)), ((), ())), preferred_element_type=F32)


def _rms_stats(x):
    r = lax.rsqrt(jnp.mean(x * x, axis=-1, keepdims=True) + NORM_EPS)
    return x * r, r


def _rms_bwd(dh, x, g):
    xhat, r = _rms_stats(x)
    dxn = dh * g
    dx = r * (dxn - xhat * jnp.mean(dxn * xhat, axis=-1, keepdims=True))
    tm, d = x.shape
    dg = (dh * xhat).reshape(tm // SUBLANES, SUBLANES, d).sum(axis=0)
    return dx, dg


def _sigmoid(x):
    return 1.0 / (1.0 + jnp.exp(-x))


def _swap32(t):
    n = t.shape[-1]
    lane = lax.broadcasted_iota(jnp.int32, t.shape, t.ndim - 1)
    return jnp.where((lane % HEAD_DIM) < HEAD_DIM // 2, pltpu.roll(t, n - HEAD_DIM // 2, axis=t.ndim - 1),
                     pltpu.roll(t, HEAD_DIM // 2, axis=t.ndim - 1))


def _cast_place(me_arr, w, name):
    R, C = w.shape
    tr = R // 2 if (R // 2) % 16 == 0 else R

    def body(me_ref, w_ref, o_ref):
        o_ref[...] = w_ref[...].astype(BF16)

    grid_spec = pltpu.PrefetchScalarGridSpec(
        num_scalar_prefetch=1, grid=(R // tr,), in_specs=[pl.BlockSpec((tr, C), lambda t, me: (t, 0))],
        out_specs=pl.BlockSpec((None, tr, C), lambda t, me: (me[0], t, 0)))
    return pl.pallas_call(body, name=name, grid_spec=grid_spec, out_shape=_sds((N_CHIPS, R, C), BF16),
                          compiler_params=_params(("parallel",)))(me_arr, w)


HBM = pl.BlockSpec(memory_space=pltpu.HBM)
SEM = pl.BlockSpec(memory_space=pltpu.SEMAPHORE)


def _push_start(name, bufs, ncopies, plan, after):
    nb = len(bufs)

    def body(*refs):
        send, recv, token = refs[nb + 1], refs[nb + 2], refs[-1]
        for i, (src, dst, dev) in enumerate(plan(refs[:nb])):
            pltpu.make_async_remote_copy(src_ref=src, dst_ref=dst, send_sem=send.at[i], recv_sem=recv.at[i],
                                         device_id=dev, device_id_type=MESH).start()
        token[...] = jnp.zeros_like(token)

    outs = pl.pallas_call(
        body, name=name,
        out_shape=(pltpu.SemaphoreType.DMA((ncopies,)), pltpu.SemaphoreType.DMA((ncopies,)), *[pltpu.HBM(b.shape, b.dtype) for b in bufs],
                   _sds((SUBLANES, LANES), F32)),
        in_specs=[HBM] * nb + [ANY], out_specs=(SEM, SEM, *([HBM] * nb), pl.BlockSpec(memory_space=pltpu.VMEM)),
        input_output_aliases={i: 2 + i for i in range(nb)},
        compiler_params=pltpu.CompilerParams(has_side_effects=pltpu.SideEffectType.DATAFLOW_SIDE_EFFECTING),
    )(*[pltpu.with_memory_space_constraint(b, pltpu.HBM) for b in bufs], after)
    return outs[0], outs[1], list(outs[2:2 + nb]), outs[-1]


def _push_wait(name, send, recv, bufs, plan, after):
    nb = len(bufs)

    def body(*refs):
        send_ref, recv_ref = refs[nb], refs[nb + 1]
        for i, (src, dst, dev) in enumerate(plan(refs[:nb])):
            cp = pltpu.make_async_remote_copy(src_ref=src, dst_ref=dst, send_sem=send_ref.at[i], recv_sem=recv_ref.at[i],
                                              device_id=dev, device_id_type=MESH)
            cp.wait_send()
            cp.wait_recv()

    outs = pl.pallas_call(
        body, name=name, out_shape=tuple(pltpu.HBM(b.shape, b.dtype) for b in bufs),
        in_specs=[HBM] * nb + [SEM, SEM, ANY], out_specs=tuple([HBM] * nb), input_output_aliases={i: i for i in range(nb)},
        compiler_params=pltpu.CompilerParams(has_side_effects=pltpu.SideEffectType.DATAFLOW_SIDE_EFFECTING),
    )(*bufs, send, recv, after)
    return list(outs)


def _mesh_pos():
    return lax.axis_index("x"), lax.axis_index("y"), lax.axis_index("c")


def _chip_peers(x, y, c):
    return [((1 - x, y, c), 2 * (1 - x) + y), ((x, 1 - y, c), 2 * x + (1 - y)), ((1 - x, 1 - y, c), 2 * (1 - x) + (1 - y))]


def _gather_plan(n):
    def plan(refs):
        x, y, c = _mesh_pos()
        me = 2 * x + y
        return [(refs[k].at[me], refs[k].at[me], dev) for k in range(n) for dev, _ in _chip_peers(x, y, c)]
    return plan


def _gather_weights(fulls):
    n = len(fulls)

    def body(*refs):
        ins, outs = refs[:n], refs[n:2 * n]
        ici_send, ici_recv, d2d_send, d2d_recv = refs[2 * n:]
        x, y, c = _mesh_pos()
        me = 2 * x + y
        sibling = (x, y, 1 - c)
        peers = _chip_peers(x, y, c)

        def half(k, who):
            r2 = fulls[k].shape[1] // 2
            return pl.ds(pl.multiple_of(who * r2, 16), r2)

        first = []
        for k in range(n):
            for rel, (dev, _) in enumerate(peers):
                cp = pltpu.make_async_remote_copy(src_ref=ins[k].at[me, half(k, c), :], dst_ref=outs[k].at[me, half(k, c), :],
                                                  send_sem=ici_send.at[k * 3 + rel], recv_sem=ici_recv.at[k * 3 + rel],
                                                  device_id=dev, device_id_type=MESH)
                cp.start()
                first.append(cp)
        passed = []
        for k in range(n):
            for rel, (dev, chip) in enumerate(peers):
                blk = outs[k].at[chip, half(k, c), :]
                pltpu.make_async_remote_copy(src_ref=blk, dst_ref=blk, send_sem=ici_send.at[k * 3 + rel], recv_sem=ici_recv.at[k * 3 + rel],
                                             device_id=dev, device_id_type=MESH).wait_recv()
                cp = pltpu.make_async_remote_copy(src_ref=blk, dst_ref=blk, send_sem=d2d_send.at[k * 3 + rel], recv_sem=d2d_recv.at[k * 3 + rel],
                                                  device_id=sibling, device_id_type=MESH)
                cp.start()
                passed.append(cp)
        for k in range(n):
            for rel, (dev, chip) in enumerate(peers):
                blk = outs[k].at[chip, half(k, 1 - c), :]
                pltpu.make_async_remote_copy(src_ref=blk, dst_ref=blk, send_sem=d2d_send.at[k * 3 + rel], recv_sem=d2d_recv.at[k * 3 + rel],
                                             device_id=sibling, device_id_type=MESH).wait_recv()
        for cp in first + passed:
            cp.wait_send()

    return pl.pallas_call(
        body, name="gather_weights", out_shape=[_sds(f.shape, BF16) for f in fulls],
        in_specs=[ANY] * n, out_specs=[ANY] * n, input_output_aliases={k: k for k in range(n)},
        scratch_shapes=[pltpu.SemaphoreType.DMA((n * 3,))] * 4, compiler_params=_params())(*fulls)


def _resident(shape):
    return pl.BlockSpec(shape, lambda i: (0,) * len(shape), pipeline_mode=pl.Buffered(1))


def _ffn_fwd(x, g, wg, wu, wd, name, tm=256):
    T, D = x.shape
    ns, _, fs = wg.shape

    def body(x_ref, g_ref, wg_ref, wu_ref, wd_ref, xo_ref, h_ref, gate_ref, up_ref, act_ref, h_scr):
        xv = x_ref[...]
        xhat, _ = _rms_stats(xv)
        hb = (xhat * g_ref[...]).astype(BF16)
        h_scr[...] = hb
        h_ref[...] = hb
        acc = None
        for j in range(ns):
            h = h_scr[...]
            gate = _dot(h, wg_ref[j])
            up = _dot(h, wu_ref[j])
            act = ((gate * _sigmoid(gate)) * up).astype(BF16)
            gate_ref[j] = gate.astype(BF16)
            up_ref[j] = up.astype(BF16)
            act_ref[j] = act
            d = _dot(act, wd_ref[j])
            acc = d if acc is None else acc + d
        xo_ref[...] = xv + FFN_RES_WEIGHT * acc

    row = pl.BlockSpec((tm, D), lambda i: (i, 0))
    saved = pl.BlockSpec((ns, tm, fs), lambda i: (0, i, 0))
    return pl.pallas_call(
        body, name=name, grid=(T // tm,),
        in_specs=[row, pl.BlockSpec((1, D), lambda i: (0, 0)), _resident(wg.shape), _resident(wu.shape), _resident(wd.shape)],
        out_specs=[row, row, saved, saved, saved],
        out_shape=[_sds((T, D), F32), _sds((T, D), BF16), _sds((ns, T, fs), BF16), _sds((ns, T, fs), BF16), _sds((ns, T, fs), BF16)],
        scratch_shapes=[pltpu.VMEM((tm, D), BF16)], compiler_params=_params(("parallel",)))(x, g, wg, wu, wd)


def _ffn_dx(dxo, x, g, gate_s, up_s, wg, wu, wd, name, tm=256):
    T, D = x.shape
    ns, _, fs = wg.shape

    def body(dxo_ref, x_ref, g_ref, gate_ref, up_ref, wg_ref, wu_ref, wd_ref, dx_ref, dff_ref, dgate_ref, dup_ref, dg_ref, dff_scr):
        @pl.when(pl.program_id(0) == 0)
        def _():
            dg_ref[...] = jnp.zeros_like(dg_ref)

        d = (FFN_RES_WEIGHT * dxo_ref[...]).astype(BF16)
        dff_scr[...] = d
        dff_ref[...] = d
        dh = None
        for j in range(ns):
            da = _dot_nt(dff_scr[...], wd_ref[j])
            gate = gate_ref[j].astype(F32)
            up = up_ref[j].astype(F32)
            s = _sigmoid(gate)
            silu = gate * s
            dup = (da * silu).astype(BF16)
            dgate = (da * up * (s * (1.0 + gate * (1.0 - s)))).astype(BF16)
            dgate_ref[j] = dgate
            dup_ref[j] = dup
            t = _dot_nt(dgate, wg_ref[j]) + _dot_nt(dup, wu_ref[j])
            dh = t if dh is None else dh + t
        dxn, dg = _rms_bwd(dh, x_ref[...], g_ref[...])
        dg_ref[...] += dg
        dx_ref[...] = dxo_ref[...] + dxn

    row = pl.BlockSpec((tm, D), lambda i: (i, 0))
    saved = pl.BlockSpec((ns, tm, fs), lambda i: (0, i, 0))
    return pl.pallas_call(
        body, name=name, grid=(T // tm,),
        in_specs=[row, row, pl.BlockSpec((1, D), lambda i: (0, 0)), saved, saved, _resident(wg.shape), _resident(wu.shape),
                  _resident(wd.shape)],
        out_specs=[row, row, saved, saved, pl.BlockSpec((SUBLANES, D), lambda i: (0, 0))],
        out_shape=[_sds((T, D), F32), _sds((T, D), BF16), _sds((ns, T, fs), BF16), _sds((ns, T, fs), BF16), _sds((SUBLANES, D), F32)],
        scratch_shapes=[pltpu.VMEM((tm, D), BF16)], compiler_params=_params(("arbitrary",)))(dxo, x, g, gate_s, up_s, wg, wu, wd)


def _tn(a, bs, mode, name, tk=2048, nb=None, dep=None):
    nbs = len(bs)
    if mode == "shard_b":
        T, M = a.shape
        G, _, N = bs[0].shape
        a_spec = pl.BlockSpec((tk, M), lambda g, t: (t, 0))
        b_spec = pl.BlockSpec((None, tk, N), lambda g, t: (g, t, 0))
        o_spec, o_shape = pl.BlockSpec((None, M, N), lambda g, t: (g, 0, 0)), (G, M, N)
    elif mode == "shard_a":
        G, T, M = a.shape
        N = bs[0].shape[1]
        a_spec = pl.BlockSpec((None, tk, M), lambda g, t: (g, t, 0))
        b_spec = pl.BlockSpec((tk, N), lambda g, t: (t, 0))
        o_spec, o_shape = pl.BlockSpec((None, M, N), lambda g, t: (g, 0, 0)), (G, M, N)
    else:
        T, M = a.shape
        N = nb
        G = bs[0].shape[1] // nb
        a_spec = pl.BlockSpec((tk, M), lambda g, t: (t, 0))
        b_spec = pl.BlockSpec((tk, N), lambda g, t: (t, g))
        o_spec, o_shape = pl.BlockSpec((M, N), lambda g, t: (0, g)), (M, G * nb)

    nt = T // tk

    def body(a_ref, *refs):
        b_refs, o_refs, ob_refs = refs[:nbs], refs[-2 * nbs:-nbs], refs[-nbs:]
        av = a_ref[...].astype(BF16)
        for b_ref, o_ref, ob_ref in zip(b_refs, o_refs, ob_refs):
            @pl.when(pl.program_id(1) == 0)
            def _():
                o_ref[...] = jnp.zeros_like(o_ref)

            o_ref[...] += _dot_tn(av, b_ref[...].astype(BF16))

            @pl.when(pl.program_id(1) == nt - 1)
            def _():
                ob_ref[...] = o_ref[...].astype(BF16)

    outs = pl.pallas_call(
        body, name=name, grid=(G, nt), in_specs=[a_spec] + [b_spec] * nbs + ([ANY] if dep is not None else []),
        out_specs=[o_spec] * (2 * nbs), out_shape=[_sds(o_shape, F32)] * nbs + [_sds(o_shape, BF16)] * nbs,
        compiler_params=_params(("parallel", "arbitrary")))(a, *bs, *([dep] if dep is not None else []))
    return list(zip(outs[:nbs], outs[nbs:]))


def _rope_tables(pos_col, inv_freq):
    T = pos_col.shape[0]

    def body(p_ref, f_ref, c_ref, s_ref):
        ang = p_ref[...].astype(F32) * f_ref[...]
        lane = lax.broadcasted_iota(jnp.int32, ang.shape, 1)
        c_ref[...] = jnp.cos(ang)
        sn = jnp.sin(ang)
        s_ref[...] = jnp.where((lane % HEAD_DIM) < HEAD_DIM // 2, -sn, sn)

    tm = 1024
    return pl.pallas_call(
        body, name="rope_tables", grid=(T // tm,),
        in_specs=[pl.BlockSpec((tm, 1), lambda i: (i, 0)), pl.BlockSpec((1, LANES), lambda i: (0, 0))],
        out_specs=[pl.BlockSpec((tm, LANES), lambda i: (i, 0))] * 2,
        out_shape=[_sds((T, LANES), F32)] * 2, compiler_params=_params(("parallel",)))(pos_col, inv_freq)


def _deinterleave(scr, out_ref, d, tm, nblk):
    for r in range(d):
        for cb in range(nblk):
            out_ref[r, :, cb * LANES:(cb + 1) * LANES] = scr[cb, pl.ds(r, tm // d, stride=d), :].astype(out_ref.dtype)


def _interleave(in_ref, scr, d, tm, nblk):
    for r in range(d):
        for cb in range(nblk):
            scr[cb, pl.ds(r, tm // d, stride=d), :] = in_ref[r, :, cb * LANES:(cb + 1) * LANES].astype(F32)


def _proj_rope(x, g, w_in, cos, sin, tm=512):
    T, D = x.shape
    dils = [d for _, d in B_PATTERNS if d > 1]
    nbb = B_W // LANES
    scale = HEAD_DIM ** -0.5
    cuts = [0, A_Q_W, A_Q_W + A_KV_W, A_Q_W + 2 * A_KV_W, A_Q_W + 2 * A_KV_W + B_W, A_Q_W + 2 * A_KV_W + 2 * B_W,
            A_Q_W + 2 * A_KV_W + 3 * B_W]

    def body(x_ref, g_ref, w_ref, c_ref, s_ref, h_ref, aq_ref, ak_ref, av_ref, *rest):
        b_refs, scr = rest[:-1], rest[-1]
        xhat, _ = _rms_stats(x_ref[...])
        h = (xhat * g_ref[...]).astype(BF16)
        h_ref[...] = h
        cs, sn = c_ref[...], s_ref[...]

        def seg(idx, rope, mult):
            lo, hi = cuts[idx], cuts[idx + 1]
            blocks = []
            whole = _dot(h, w_ref[:, lo:hi])
            for cb in range((hi - lo) // LANES):
                p = whole[:, cb * LANES:(cb + 1) * LANES]
                if rope:
                    p = p * cs + _swap32(p) * sn
                if mult != 1.0:
                    p = p * mult
                blocks.append(p)
            return blocks

        for idx, ref, rope, mult in ((0, aq_ref, True, scale), (1, ak_ref, True, 1.0), (2, av_ref, False, 1.0)):
            for cb, p in enumerate(seg(idx, rope, mult)):
                ref[:, cb * LANES:(cb + 1) * LANES] = p.astype(BF16)
        for which, (idx, rope, mult) in enumerate(((3, True, scale), (4, True, 1.0), (5, False, 1.0))):
            for cb, p in enumerate(seg(idx, rope, mult)):
                b_refs[which][:, cb * LANES:(cb + 1) * LANES] = p.astype(BF16)
                scr[cb] = p
            for di, d in enumerate(dils):
                _deinterleave(scr, b_refs[3 * (di + 1) + which], d, tm, nbb)

    row = lambda w: pl.BlockSpec((tm, w), lambda i: (i, 0))
    out_specs = [row(D), row(A_Q_W), row(A_KV_W), row(A_KV_W)] + [row(B_W)] * 3
    out_shape = [_sds((T, D), BF16), _sds((T, A_Q_W), BF16), _sds((T, A_KV_W), BF16), _sds((T, A_KV_W), BF16)] + [_sds((T, B_W), BF16)] * 3
    for d in dils:
        out_specs += [pl.BlockSpec((d, tm // d, B_W), lambda i: (0, i, 0))] * 3
        out_shape += [_sds((d, T // d, B_W), BF16)] * 3
    return pl.pallas_call(
        body, name="proj_rope", grid=(T // tm,),
        in_specs=[row(D), pl.BlockSpec((1, D), lambda i: (0, 0)), pl.BlockSpec(w_in.shape, lambda i: (0, 0)), row(LANES), row(LANES)],
        out_specs=out_specs, out_shape=out_shape, scratch_shapes=[pltpu.VMEM((nbb, tm, LANES), F32)],
        compiler_params=_params(("parallel",)))(x, g, w_in, cos, sin)


def _band_bias(rel, kw, hw):
    ri = lax.broadcasted_iota(jnp.int32, (QB, kw), 0)
    ci = lax.broadcasted_iota(jnp.int32, (QB, kw), 1)
    return jnp.where(jnp.abs(ri + rel - ci) <= hw, 0.0, NEG).astype(F32)


def _band_setup(bias_scr, kw, hw):
    if bias_scr is not None:
        for i in range(3):
            bias_scr[i] = _band_bias(i * hw, kw, hw)


def _band_window(bias_scr, qs, L, kw, hw):
    ws = pl.multiple_of(jnp.clip(qs - hw, 0, L - kw), 64)
    if bias_scr is None:
        return ws, _band_bias(qs - ws, kw, hw)
    return ws, bias_scr[lax.shift_right_logical(qs - ws, hw.bit_length() - 1)]


def _dup_kv_head(src_ref, dst_ref, head, L):
    step = min(L, 1024)
    for r0 in range(0, L, step):
        xf = src_ref[r0:r0 + step, :].astype(F32)
        lane = lax.broadcasted_iota(jnp.int32, xf.shape, 1)
        keep = jnp.logical_xor(lane < HEAD_DIM, head == 1)
        dst_ref[r0:r0 + step, :] = jnp.where(keep, xf, pltpu.roll(xf, HEAD_DIM, axis=1)).astype(dst_ref.dtype)


def _attn_fwd(q, k, v, sink, hw, gqa, out_dtype, name):
    NB, L, Cq = q.shape
    Ls = min(L, 2048)
    kw = min(QB + 2 * hw, L)
    tables = L >= QB + 2 * hw
    unroll = min(4, Ls // QB)

    def body(sink_ref, q_ref, k_ref, v_ref, o_ref, lse_ref, *scr):
        b, s_idx = pl.program_id(1), pl.program_id(2)
        bias_scr = scr[0] if tables else None
        _band_setup(bias_scr, kw, hw)
        if gqa:
            kd, vd = scr[-2:]

            @pl.when(s_idx == 0)
            def _():
                _dup_kv_head(k_ref, kd, b // 2, L)
                _dup_kv_head(v_ref, vd, b // 2, L)
        else:
            kd, vd = k_ref, v_ref
        lane = lax.broadcasted_iota(jnp.int32, (QB, LANES), 1)
        lo = lane < HEAD_DIM

        def block(ql):
            qs = s_idx * Ls + ql
            ws, bias = _band_window(bias_scr, qs, L, kw, hw)
            qv = q_ref[pl.ds(ql, QB), :]
            kv_, vv = kd[pl.ds(ws, kw), :], vd[pl.ds(ws, kw), :]
            res = []
            for half in (0, 1):
                qm = jnp.where(lo if half == 0 else jnp.logical_not(lo), qv, jnp.zeros_like(qv))
                s = _dot_nt(qm, kv_) + bias
                m = jnp.max(s, axis=-1, keepdims=True)
                if gqa:
                    sk = sink_ref[2 * b + half]
                    m = jnp.maximum(m, sk)
                p = jnp.exp(s - m)
                den = jnp.sum(p, axis=-1, keepdims=True)
                if gqa:
                    den = den + jnp.exp(sk - m)
                res.append((_dot(p.astype(BF16), vv) * (1.0 / den), m + jnp.log(den)))
            o_ref[pl.ds(ql, QB), :] = jnp.where(lo, res[0][0], res[1][0]).astype(o_ref.dtype)
            lse_ref[pl.ds(ql, QB), :] = jnp.where(lo, res[0][1], res[1][1])

        def step(n, carry):
            for u in range(unroll):
                block(pl.multiple_of((n * unroll + u) * QB, QB))
            return carry

        lax.fori_loop(0, Ls // (QB * unroll), step, 0)

    kv_map = (lambda r, b, s: (r, 0, 0)) if gqa else (lambda r, b, s: (r, 0, b))
    seg = pl.BlockSpec((None, Ls, LANES), lambda r, b, s: (r, s, b))
    return pl.pallas_call(
        body, name=name, grid=(NB, Cq // LANES, L // Ls),
        in_specs=[pl.BlockSpec(memory_space=pltpu.SMEM), seg, pl.BlockSpec((None, L, LANES), kv_map), pl.BlockSpec((None, L, LANES), kv_map)],
        out_specs=[seg, seg], out_shape=[_sds((NB, L, Cq), out_dtype), _sds((NB, L, Cq), F32)],
        scratch_shapes=([pltpu.VMEM((3, QB, kw), F32)] if tables else []) + ([pltpu.VMEM((L, LANES), BF16)] * 2 if gqa else []),
        compiler_params=_params(("parallel", "parallel", "arbitrary")))(sink, q, k, v)


def _attn_bwd(q, k, v, do, lse, delta, sink, hw, gqa, name):
    NB, L, Cq = q.shape
    Ck = k.shape[2]
    Ls = min(L, 2048)
    kw = min(QB + 2 * hw, L)
    reps = kw // LANES
    nseg = L // Ls
    scale = HEAD_DIM ** -0.5
    tables = L >= QB + 2 * hw
    unroll = min(4, Ls // QB)

    def body(sink_ref, q_ref, do_ref, lse_ref, dl_ref, k_ref, v_ref, dq_ref, dk_ref, dv_ref, dsk_ref, *scr):
        b, s_idx = pl.program_id(1), pl.program_id(2)
        lane = lax.broadcasted_iota(jnp.int32, (QB, LANES), 1)
        lo = lane < HEAD_DIM
        bias_scr = scr[0] if tables else None
        _band_setup(bias_scr, kw, hw)
        if gqa:
            kd, vd, dk_acc, dv_acc, dsk_acc = scr[-5:]

            @pl.when(s_idx == 0)
            def _():
                _dup_kv_head(k_ref, kd, b // 2, L)
                _dup_kv_head(v_ref, vd, b // 2, L)
                dk_acc[...] = jnp.zeros_like(dk_acc)
                dv_acc[...] = jnp.zeros_like(dv_acc)
                dsk_acc[...] = jnp.zeros_like(dsk_acc)

            @pl.when((s_idx == 0) & (b == 0))
            def _():
                dk_ref[...] = jnp.zeros_like(dk_ref)
                dv_ref[...] = jnp.zeros_like(dv_ref)
        else:
            kd, vd, dk_acc, dv_acc = k_ref, v_ref, dk_ref, dv_ref

            @pl.when(s_idx == 0)
            def _():
                dk_ref[...] = jnp.zeros_like(dk_ref)
                dv_ref[...] = jnp.zeros_like(dv_ref)

        def block(ql):
            qs = s_idx * Ls + ql
            ws, bias = _band_window(bias_scr, qs, L, kw, hw)
            qv, dov = q_ref[pl.ds(ql, QB), :], do_ref[pl.ds(ql, QB), :]
            lse, dl = lse_ref[pl.ds(ql, QB), :], dl_ref[pl.ds(ql, QB), :]
            kv_, vv = kd[pl.ds(ws, kw), :], vd[pl.ds(ws, kw), :]
            lse_sw, dl_sw = pltpu.roll(lse, HEAD_DIM, axis=1), pltpu.roll(dl, HEAD_DIM, axis=1)
            dqs = []
            dk_c = jnp.zeros((kw, LANES), F32)
            dv_c = jnp.zeros((kw, LANES), F32)
            for half in (0, 1):
                msk = lo if half == 0 else jnp.logical_not(lo)
                qm = jnp.where(msk, qv, jnp.zeros_like(qv))
                dom = jnp.where(msk, dov, jnp.zeros_like(dov))
                lse_h = jnp.where(msk, lse, lse_sw)
                dl_h = jnp.where(msk, dl, dl_sw)
                s = _dot_nt(qm, kv_) + bias
                p = jnp.exp(s - jnp.tile(lse_h, (1, reps)))
                dp = _dot_nt(dom, vv)
                ds = (p * (dp - jnp.tile(dl_h, (1, reps)))).astype(BF16)
                dqs.append(_dot(ds, kv_))
                dk_c = dk_c + _dot_tn(ds, qm)
                dv_c = dv_c + _dot_tn(p.astype(BF16), dom)
            dq_ref[pl.ds(ql, QB), :] = (jnp.where(lo, dqs[0], dqs[1]) * scale).astype(dq_ref.dtype)
            dk_acc[pl.ds(ws, kw), :] += dk_c
            dv_acc[pl.ds(ws, kw), :] += dv_c
            if gqa:
                sk = jnp.where(lo, sink_ref[2 * b], sink_ref[2 * b + 1])
                dsk_acc[...] += -jnp.exp(sk - lse) * dl

        def step(n, carry):
            for u in range(unroll):
                block(pl.multiple_of((n * unroll + u) * QB, QB))
            return carry

        lax.fori_loop(0, Ls // (QB * unroll), step, 0)

        if gqa:
            @pl.when(s_idx == nseg - 1)
            def _():
                step_rows = min(L, 1024)
                for r0 in range(0, L, step_rows):
                    lanek = lax.broadcasted_iota(jnp.int32, (step_rows, LANES), 1)
                    mine = jnp.logical_xor(lanek < HEAD_DIM, (b // 2) == 1)
                    for acc, ref in ((dk_acc, dk_ref), (dv_acc, dv_ref)):
                        a = acc[r0:r0 + step_rows, :]
                        ref[r0:r0 + step_rows, :] += jnp.where(mine, a + pltpu.roll(a, HEAD_DIM, axis=1), 0.0)
                dsk_ref[...] = dsk_acc[...].reshape(QB // SUBLANES, SUBLANES, LANES).sum(axis=0)
        else:
            dsk_ref[...] = jnp.zeros_like(dsk_ref)

    kv_map = (lambda r, b, s: (r, 0, 0)) if gqa else (lambda r, b, s: (r, 0, b))
    seg = pl.BlockSpec((None, Ls, LANES), lambda r, b, s: (r, s, b))
    full = pl.BlockSpec((None, L, LANES), kv_map)
    scratch = [pltpu.VMEM((3, QB, kw), F32)] if tables else []
    if gqa:
        scratch += [pltpu.VMEM((L, LANES), BF16)] * 2 + [pltpu.VMEM((L, LANES), F32)] * 2 + [pltpu.VMEM((QB, LANES), F32)]
    return pl.pallas_call(
        body, name=name, grid=(NB, Cq // LANES, nseg),
        in_specs=[pl.BlockSpec(memory_space=pltpu.SMEM), seg, seg, seg, seg, full, full],
        out_specs=[seg, full, full, pl.BlockSpec((None, None, SUBLANES, LANES), lambda r, b, s: (r, b, 0, 0))],
        out_shape=[_sds((NB, L, Cq), BF16), _sds((NB, L, Ck), F32), _sds((NB, L, Ck), F32),
                   _sds((NB, Cq // LANES, SUBLANES, LANES), F32)],
        scratch_shapes=scratch,
        compiler_params=_params(("arbitrary", "arbitrary", "arbitrary")))(sink, q, do, lse, delta, k, v)


def _merge_b(a_out, o1, l1, o4, l4, o16, l16, tm=512):
    T = a_out.shape[0]
    nbb = B_W // LANES

    def body(a_ref, o1_ref, l1_ref, o4_ref, l4_ref, o16_ref, l16_ref, cat_ref, lg1_ref, lg4_ref, lg16_ref, so, sl, slg):
        _interleave(o4_ref, so.at[0], 4, tm, nbb)
        _interleave(l4_ref, sl.at[0], 4, tm, nbb)
        _interleave(o16_ref, so.at[1], 16, tm, nbb)
        _interleave(l16_ref, sl.at[1], 16, tm, nbb)
        cat_ref[:, 0:A_Q_W] = a_ref[...]
        for cb in range(nbb):
            cols = slice(cb * LANES, (cb + 1) * LANES)
            os_ = (o1_ref[:, cols], so[0, cb], so[1, cb])
            ls_ = (l1_ref[:, cols], sl[0, cb], sl[1, cb])
            m = jnp.maximum(jnp.maximum(ls_[0], ls_[1]), ls_[2])
            es = [jnp.exp(l - m) for l in ls_]
            den = es[0] + es[1] + es[2]
            out = (es[0] * os_[0] + es[1] * os_[1] + es[2] * os_[2]) * (1.0 / den)
            lg = m + jnp.log(den)
            cat_ref[:, A_Q_W + cb * LANES:A_Q_W + (cb + 1) * LANES] = out.astype(BF16)
            lg1_ref[:, cols] = lg
            slg[cb] = lg
        _deinterleave(slg, lg4_ref, 4, tm, nbb)
        _deinterleave(slg, lg16_ref, 16, tm, nbb)

    row = lambda w: pl.BlockSpec((tm, w), lambda i: (i, 0))
    perm = lambda d: pl.BlockSpec((d, tm // d, B_W), lambda i: (0, i, 0))
    return pl.pallas_call(
        body, name="merge_patterns", grid=(T // tm,),
        in_specs=[row(A_Q_W), row(B_W), row(B_W), perm(4), perm(4), perm(16), perm(16)],
        out_specs=[row(A_Q_W + B_W), row(B_W), perm(4), perm(16)],
        out_shape=[_sds((T, A_Q_W + B_W), BF16), _sds((T, B_W), F32), _sds((4, T // 4, B_W), F32), _sds((16, T // 16, B_W), F32)],
        scratch_shapes=[pltpu.VMEM((2, nbb, tm, LANES), F32), pltpu.VMEM((2, nbb, tm, LANES), F32), pltpu.VMEM((nbb, tm, LANES), F32)],
        compiler_params=_params(("parallel",)))(a_out, o1, l1, o4, l4, o16, l16)


def _out_proj(x, cat, w_out, tm=512):
    T, D = x.shape

    def body(x_ref, c_ref, w_ref, o_ref):
        o_ref[...] = x_ref[...] + _dot(c_ref[...], w_ref[...])

    row = lambda w: pl.BlockSpec((tm, w), lambda i: (i, 0))
    return pl.pallas_call(
        body, name="out_proj", grid=(T // tm,), in_specs=[row(D), row(cat.shape[1]), pl.BlockSpec(w_out.shape, lambda i: (0, 0))],
        out_specs=row(D), out_shape=_sds((T, D), F32), compiler_params=_params(("parallel",)))(x, cat, w_out)


def _final_loss(x, g, target, tm=512):
    T, D = x.shape

    def body(x_ref, g_ref, t_ref, dx_ref, dg_ref, loss_ref):
        @pl.when(pl.program_id(0) == 0)
        def _():
            dg_ref[...] = jnp.zeros_like(dg_ref)
            loss_ref[...] = jnp.zeros_like(loss_ref)

        xv, gv = x_ref[...], g_ref[...]
        xhat, _ = _rms_stats(xv)
        err = xhat * gv - t_ref[...]
        loss_ref[...] += 0.5 * jnp.sum(jnp.sum(err * err, axis=-1, keepdims=True) * (1.0 / D), axis=0, keepdims=True)
        dx, dg = _rms_bwd(err * (1.0 / D), xv, gv)
        dx_ref[...] = dx
        dg_ref[...] += dg

    row = pl.BlockSpec((tm, D), lambda i: (i, 0))
    return pl.pallas_call(
        body, name="final_loss", grid=(T // tm,), in_specs=[row, pl.BlockSpec((1, D), lambda i: (0, 0)), row],
        out_specs=[row, pl.BlockSpec((SUBLANES, D), lambda i: (0, 0)), pl.BlockSpec((SUBLANES, LANES), lambda i: (0, 0))],
        out_shape=[_sds((T, D), F32), _sds((SUBLANES, D), F32), _sds((SUBLANES, LANES), F32)],
        compiler_params=_params(("arbitrary",)))(x, g, target)


def _dcat(dx, w_out, cat, tm=512):
    T, D = dx.shape
    C = cat.shape[1]
    nba, nbb = A_Q_W // LANES, B_W // LANES

    def body(dx_ref, w_ref, cat_ref, doa_ref, dla_ref, dob1_ref, dlb1_ref, dob4_ref, dlb4_ref, dob16_ref, dlb16_ref, sdo, sdl):
        dc = _dot_nt(dx_ref[...].astype(BF16), w_ref[...])
        ri = lax.broadcasted_iota(jnp.int32, (LANES, LANES), 0)
        ci = lax.broadcasted_iota(jnp.int32, (LANES, LANES), 1)
        same_head = ((ri // HEAD_DIM) == (ci // HEAD_DIM)).astype(BF16)
        for cb in range(C // LANES):
            cols = slice(cb * LANES, (cb + 1) * LANES)
            blk = dc[:, cols]
            prod = blk * cat_ref[:, cols].astype(F32)
            hi = prod.astype(BF16)
            lo_ = (prod - hi.astype(F32)).astype(BF16)
            dl = _dot(hi, same_head) + _dot(lo_, same_head)
            if cb < nba:
                doa_ref[:, cols] = blk.astype(BF16)
                dla_ref[:, cols] = dl
            else:
                bcols = slice((cb - nba) * LANES, (cb - nba + 1) * LANES)
                dob1_ref[:, bcols] = blk.astype(BF16)
                dlb1_ref[:, bcols] = dl
                sdo[cb - nba] = blk
                sdl[cb - nba] = dl
        _deinterleave(sdo, dob4_ref, 4, tm, nbb)
        _deinterleave(sdl, dlb4_ref, 4, tm, nbb)
        _deinterleave(sdo, dob16_ref, 16, tm, nbb)
        _deinterleave(sdl, dlb16_ref, 16, tm, nbb)

    row = lambda w: pl.BlockSpec((tm, w), lambda i: (i, 0))
    perm = lambda d: pl.BlockSpec((d, tm // d, B_W), lambda i: (0, i, 0))
    return pl.pallas_call(
        body, name="dcat", grid=(T // tm,), in_specs=[row(D), pl.BlockSpec(w_out.shape, lambda i: (0, 0)), row(C)],
        out_specs=[row(A_Q_W), row(A_Q_W), row(B_W), row(B_W), perm(4), perm(4), perm(16), perm(16)],
        out_shape=[_sds((T, A_Q_W), BF16), _sds((T, A_Q_W), F32), _sds((T, B_W), BF16), _sds((T, B_W), F32),
                   _sds((4, T // 4, B_W), BF16), _sds((4, T // 4, B_W), F32), _sds((16, T // 16, B_W), BF16), _sds((16, T // 16, B_W), F32)],
        scratch_shapes=[pltpu.VMEM((nbb, tm, LANES), F32)] * 2, compiler_params=_params(("parallel",)))(dx, w_out, cat)


def _rope_bwd_assemble(dqa, dka, dva, b1, b4, b16, cos, sin, tm=512):
    T = dqa.shape[0]
    nbb = B_W // LANES
    width = A_Q_W + 2 * A_KV_W + 3 * B_W

    def body(dqa_ref, dka_ref, dva_ref, q1, k1, v1, q4, k4, v4, q16, k16, v16, c_ref, s_ref, o_ref, scr):
        cs, sn = c_ref[...], s_ref[...]

        def unrope(t):
            return t * cs + _swap32(t * sn)

        col = 0
        for ref, rope in ((dqa_ref, True), (dka_ref, True), (dva_ref, False)):
            for cb in range(ref.shape[1] // LANES):
                t = ref[:, cb * LANES:(cb + 1) * LANES].astype(F32)
                o_ref[:, col:col + LANES] = (unrope(t) if rope else t).astype(BF16)
                col += LANES
        for which, (r1, r4, r16, rope) in enumerate(((q1, q4, q16, True), (k1, k4, k16, True), (v1, v4, v16, False))):
            _interleave(r4, scr.at[0], 4, tm, nbb)
            _interleave(r16, scr.at[1], 16, tm, nbb)
            for cb in range(nbb):
                t = r1[:, cb * LANES:(cb + 1) * LANES].astype(F32) + scr[0, cb] + scr[1, cb]
                o_ref[:, col:col + LANES] = (unrope(t) if rope else t).astype(BF16)
                col += LANES

    row = lambda w: pl.BlockSpec((tm, w), lambda i: (i, 0))
    perm = lambda d: pl.BlockSpec((d, tm // d, B_W), lambda i: (0, i, 0))
    return pl.pallas_call(
        body, name="rope_bwd", grid=(T // tm,),
        in_specs=[row(A_Q_W), row(A_KV_W), row(A_KV_W)] + [row(B_W)] * 3 + [perm(4)] * 3 + [perm(16)] * 3 + [row(LANES), row(LANES)],
        out_specs=row(width), out_shape=_sds((T, width), BF16), scratch_shapes=[pltpu.VMEM((2, nbb, tm, LANES), F32)],
        compiler_params=_params(("parallel",)))(dqa, dka, dva, *b1, *b4, *b16, cos, sin)


def _dh_norm(dproj, w_in, x, g, dres, tm=512):
    T, D = x.shape

    def body(dp_ref, w_ref, x_ref, g_ref, dr_ref, dx_ref, dg_ref):
        @pl.when(pl.program_id(0) == 0)
        def _():
            dg_ref[...] = jnp.zeros_like(dg_ref)

        dxn, dg = _rms_bwd(_dot_nt(dp_ref[...], w_ref[...]), x_ref[...], g_ref[...])
        dg_ref[...] += dg
        dx_ref[...] = dr_ref[...] + dxn

    row = lambda w: pl.BlockSpec((tm, w), lambda i: (i, 0))
    return pl.pallas_call(
        body, name="dh_norm", grid=(T // tm,),
        in_specs=[row(dproj.shape[1]), pl.BlockSpec(w_in.shape, lambda i: (0, 0)), row(D), pl.BlockSpec((1, D), lambda i: (0, 0)), row(D)],
        out_specs=[row(D), pl.BlockSpec((SUBLANES, D), lambda i: (0, 0))],
        out_shape=[_sds((T, D), F32), _sds((SUBLANES, D), F32)], compiler_params=_params(("arbitrary",)))(dproj, w_in, x, g, dres)


def _grad_push_plan(n):
    def plan(refs):
        x, y, c = _mesh_pos()
        return [(refs[k].at[chip], refs[n + k].at[rel], dev) for k in range(n) for rel, (dev, chip) in enumerate(_chip_peers(x, y, c))]
    return plan


def _sum_own(me_arr, g, landed, name):
    ns, R, C = g.shape
    tr = R // 2 if (R // 2) % 16 == 0 else R

    def body(me_ref, g_ref, x_ref, o_ref):
        acc = g_ref[...]
        for rel in range(ns - 1):
            acc = acc + x_ref[rel].astype(F32)
        o_ref[...] = acc

    grid_spec = pltpu.PrefetchScalarGridSpec(
        num_scalar_prefetch=1, grid=(R // tr,),
        in_specs=[pl.BlockSpec((None, tr, C), lambda t, me: (me[0], t, 0)), pl.BlockSpec((ns - 1, tr, C), lambda t, me: (0, t, 0))],
        out_specs=pl.BlockSpec((tr, C), lambda t, me: (t, 0)))
    return pl.pallas_call(body, name=name, grid_spec=grid_spec, out_shape=_sds((R, C), F32),
                          compiler_params=_params(("parallel",)))(me_arr, g, landed)


def _pair_swap(ps, name):
    n = len(ps)

    def body(*refs):
        ins, outs = refs[:n], refs[n:2 * n]
        send, recv = refs[2 * n:]
        x, y, c = _mesh_pos()
        cps = []
        for k in range(n):
            cp = pltpu.make_async_remote_copy(src_ref=ins[k], dst_ref=outs[k], send_sem=send.at[k], recv_sem=recv.at[k],
                                              device_id=(x, y, 1 - c), device_id_type=MESH)
            cp.start()
            cps.append(cp)
        for cp in cps:
            cp.wait()

    return pl.pallas_call(
        body, name=name, out_shape=[_sds(p.shape, F32) for p in ps], in_specs=[ANY] * n, out_specs=[ANY] * n,
        scratch_shapes=[pltpu.SemaphoreType.DMA((n,)), pltpu.SemaphoreType.DMA((n,))], compiler_params=_params())(*ps)


def _allreduce_small(v):
    rows, W = v.shape

    def body(v_ref, o_ref, buf, send, recv):
        x, y, c = _mesh_pos()
        me = 4 * x + 2 * y + c
        cps = []
        for m in range(1, N_DEV):
            dev = (x ^ (m >> 2), y ^ ((m >> 1) & 1), c ^ (m & 1))
            cp = pltpu.make_async_remote_copy(src_ref=v_ref, dst_ref=buf.at[me], send_sem=send.at[m - 1], recv_sem=recv.at[m - 1],
                                              device_id=dev, device_id_type=MESH)
            cp.start()
            cps.append(cp)
        for m in range(1, N_DEV):
            pltpu.make_async_remote_copy(src_ref=v_ref, dst_ref=buf.at[me ^ m], send_sem=send.at[m - 1], recv_sem=recv.at[m - 1],
                                         device_id=(x, y, c), device_id_type=MESH).wait_recv()
        for cp in cps:
            cp.wait_send()
        buf[me] = v_ref[...]
        acc = buf[0]
        for i in range(1, N_DEV):
            acc = acc + buf[i]
        o_ref[...] = acc

    return pl.pallas_call(
        body, name="allreduce_small", out_shape=_sds((rows, W), F32),
        scratch_shapes=[pltpu.VMEM((N_DEV, rows, W), F32), pltpu.SemaphoreType.DMA((N_DEV - 1,)), pltpu.SemaphoreType.DMA((N_DEV - 1,))],
        compiler_params=_params())(v)


def _adamw(w, gp, gq, m, v, name):
    R, C = w.shape
    tr = R // 2 if (R // 2) % SUBLANES == 0 else R
    c1 = 1.0 / (1.0 - ADAM_B1 ** ADAM_STEP)
    c2 = 1.0 / (1.0 - ADAM_B2 ** ADAM_STEP)

    def body(w_ref, gp_ref, gq_ref, m_ref, v_ref, g_ref, d_ref, nm_ref, nv_ref):
        gv = gp_ref[...] + gq_ref[...]
        nm = ADAM_B1 * m_ref[...] + (1.0 - ADAM_B1) * gv
        nv = ADAM_B2 * v_ref[...] + (1.0 - ADAM_B2) * (gv * gv)
        g_ref[...] = gv
        d_ref[...] = -ADAM_LR * ((nm * c1) / (jnp.sqrt(nv * c2) + ADAM_EPS) + ADAM_WD * w_ref[...])
        nm_ref[...] = nm
        nv_ref[...] = nv

    blk = pl.BlockSpec((tr, C), lambda t: (t, 0))
    return pl.pallas_call(body, name=name, grid=(R // tr,), in_specs=[blk] * 5, out_specs=[blk] * 4,
                          out_shape=[_sds((R, C), F32)] * 4, compiler_params=_params(("parallel",)))(w, gp, gq, m, v)


def _local_step(x, positions, target, norms, a_sink, comm):
    T, D = x.shape
    g1, gm, g2, gf = norms
    inv_freq = 1.0 / (ROPE_THETA ** (jnp.arange(0, HEAD_DIM, 2, dtype=F32) / HEAD_DIM))
    inv_freq = jnp.tile(inv_freq, LANES // (HEAD_DIM // 2)).reshape(1, LANES)
    cos, sin = _rope_tables(positions.reshape(T, 1), inv_freq)
    no_sink = jnp.zeros((2 * (B_W // LANES),), F32)
    W = {k: comm.weight(k, x) for k in ("wg1", "wu1", "wd1")}

    x1, h1, gate1, up1, act1 = _ffn_fwd(x, comm.order(g1), W["wg1"], W["wu1"], W["wd1"], "ffn1_fwd")
    W["w_in"] = comm.weight("w_in", x1)
    (h2, aq, ak, av, bq1, bk1, bv1, bq4, bk4, bv4, bq16, bk16, bv16) = _proj_rope(x1, gm, W["w_in"], cos, sin)
    a_out, a_lse = _attn_fwd(aq[None], ak[None], av[None], a_sink, A_HALF_WINDOW, True, BF16, "attn_a_fwd")
    bqs = {1: (bq1[None], bk1[None], bv1[None]), 4: (bq4, bk4, bv4), 16: (bq16, bk16, bv16)}
    b_o, b_l = {}, {}
    for w, d in B_PATTERNS:
        q_, k_, v_ = bqs[d]
        b_o[d], b_l[d] = _attn_fwd(q_, k_, v_, no_sink, w // (2 * d), False, F32, f"attn_b{d}_fwd")
    cat, lg1, lg4, lg16 = _merge_b(a_out[0], b_o[1][0], b_l[1][0], b_o[4], b_l[4], b_o[16], b_l[16])
    W["w_out"] = comm.weight("w_out", cat)
    x2 = _out_proj(x1, cat, W["w_out"])
    for k in ("wg2", "wu2", "wd2"):
        W[k] = comm.weight(k, x2)
    x3, h3, gate2, up2, act2 = _ffn_fwd(x2, g2, W["wg2"], W["wu2"], W["wd2"], "ffn2_fwd")

    dx3, dgf, loss8 = _final_loss(x3, gf, target)
    dx2, dff2, dgate2, dup2, dg2 = _ffn_dx(dx3, x2, g2, gate2, up2, W["wg2"], W["wu2"], W["wd2"], "ffn2_dx")
    (dwg2,) = _tn(h3, [dgate2], "shard_b", "ffn2_dw_gate")
    (dwu2,) = _tn(h3, [dup2], "shard_b", "ffn2_dw_up")
    (dwd2,) = _tn(act2, [dff2], "shard_a", "ffn2_dw_down")
    comm.ready(dict(wg2=dwg2, wu2=dwu2, wd2=dwd2), dwd2[0])

    doa, dla, dob1, dlb1, dob4, dlb4, dob16, dlb16 = _dcat(dx2, W["w_out"], cat)
    (dw_out,) = _tn(cat, [dx2], "nblock", "w_out_dw", nb=D, dep=comm.dep())
    dqa, dka, dva, dsk = _attn_bwd(aq[None], ak[None], av[None], doa[None], a_lse, dla[None], comm.order(a_sink), A_HALF_WINDOW, True,
                                   "attn_a_bwd")
    bwd_in = {1: (dob1[None], lg1[None], dlb1[None]), 4: (dob4, lg4, dlb4), 16: (dob16, lg16, dlb16)}
    bg = {}
    for w, d in B_PATTERNS:
        q_, k_, v_ = bqs[d]
        do_, l_, dl_ = bwd_in[d]
        bg[d] = _attn_bwd(q_, k_, v_, do_, l_, dl_, no_sink, w // (2 * d), False, f"attn_b{d}_bwd")[:3]
    dproj = _rope_bwd_assemble(dqa[0], dka[0], dva[0], [t[0] for t in bg[1]], bg[4], bg[16], cos, sin)
    (dw_in,) = _tn(h2, [dproj], "nblock", "w_in_dw", nb=dproj.shape[1] // 2)
    comm.ready(dict(w_in=dw_in, w_out=dw_out), dw_in[0])
    dx1, dgm = _dh_norm(dproj, W["w_in"], x1, comm.order(gm), dx2)

    dx0, dff1, dgate1, dup1, dg1 = _ffn_dx(dx1, x, g1, gate1, up1, W["wg1"], W["wu1"], W["wd1"], "ffn1_dx")
    (dwd1,) = _tn(act1, [dff1], "shard_a", "ffn1_dw_down")
    comm.ready(dict(wd1=dwd1), dwd1[0])
    (dwg1,) = _tn(h1, [dgate1], "shard_b", "ffn1_dw_gate", dep=comm.dep())
    comm.ready(dict(wg1=dwg1), dwg1[0])
    (dwu1,) = _tn(h1, [dup1], "shard_b", "ffn1_dw_up", dep=comm.dep())
    comm.ready(dict(wu1=dwu1), dwu1[0])

    dsink = dsk[0, :, :, ::HEAD_DIM].sum(axis=1).reshape(-1)
    small = dict(g1=dg1.sum(axis=0), gm=dgm.sum(axis=0), g2=dg2.sum(axis=0), gf=dgf.sum(axis=0), sink=dsink, loss=loss8[0, 0])
    return dx0, small


BIG = ("wg1", "wu1", "wd1", "w_in", "w_out", "wg2", "wu2", "wd2")
GATHER_GROUPS = (("w_in",), ("w_out",), ("wg2", "wu2", "wd2"))


class _Comm:
    def __init__(self, shards):
        x, y, c = _mesh_pos()
        self.me = (2 * x + y).astype(jnp.int32).reshape(1)
        self.shards = shards
        self.tokens = []
        self.waiting = {}
        self.groups = []
        fulls = {k: _cast_place(self.me, shards[k], f"cast_{k}") for k in BIG}
        first = ("wg1", "wu1", "wd1")
        self.full = dict(zip(first, _gather_weights([fulls[k] for k in first])))
        dep = self.full["wd1"]
        for gi, names in enumerate(GATHER_GROUPS):
            plan = _gather_plan(len(names))
            send, recv, bufs, tok = _push_start(f"gather_start_{gi}", [fulls[k] for k in names], 3 * len(names), plan, dep)
            self.tokens.append(tok)
            dep = tok
            for k in names:
                self.waiting[k] = (gi, names, send, recv, bufs, plan)

    def order(self, a):
        for tok in self.tokens:
            a = a + tok[0, 0]
        self.tokens = []
        return a

    def dep(self):
        return self.tokens[-1] if self.tokens else None

    def weight(self, name, after):
        if name in self.waiting:
            gi, names, send, recv, bufs, plan = self.waiting[name]
            for k, buf in zip(names, _push_wait(f"gather_wait_{gi}", send, recv, bufs, plan, after)):
                self.full[k] = buf
                del self.waiting[k]
        full = self.full[name]
        if name == "w_in":
            return jnp.concatenate([full[j] for j in range(N_CHIPS)], axis=1)
        if name == "w_out":
            return full.reshape(N_CHIPS * full.shape[1], full.shape[2])
        return full

    def ready(self, grads, after):
        names = list(grads)
        f32s, b16s = [], []
        for k in names:
            gf, gb = grads[k]
            if k == "w_in":
                cols = self.shards[k].shape[1]
                gf = jnp.stack([gf[:, j * cols:(j + 1) * cols] for j in range(N_CHIPS)], axis=0)
                gb = jnp.stack([gb[:, j * cols:(j + 1) * cols] for j in range(N_CHIPS)], axis=0)
            if k == "w_out":
                gf, gb = gf.reshape((N_CHIPS,) + self.shards[k].shape), gb.reshape((N_CHIPS,) + self.shards[k].shape)
            f32s.append(gf)
            b16s.append(gb)
        n = len(names)
        lands = [lax.empty((N_CHIPS - 1,) + self.shards[k].shape, BF16) for k in names]
        plan = _grad_push_plan(n)
        gi = len(self.groups)
        send, recv, bufs, tok = _push_start(f"grad_start_{gi}", b16s + lands, 3 * n, plan, after)
        self.tokens.append(tok)
        self.groups.append((names, f32s, send, recv, bufs, plan))

    def finish(self):
        out = {}
        after = self.tokens[-1]
        for gi, (names, f32s, send, recv, bufs, plan) in enumerate(self.groups):
            n = len(names)
            bufs = _push_wait(f"grad_wait_{gi}", send, recv, bufs, plan, after)
            mine = [_sum_own(self.me, f32s[i], bufs[n + i], f"sum_{k}") for i, k in enumerate(names)]
            theirs = _pair_swap(mine, f"grad_pair_swap_{gi}")
            for k, p, q in zip(names, mine, theirs):
                out[k] = (p, q)
            after = theirs[-1]
        return out


def kernel(x, positions, norm_ffn1, w_gate1, w_up1, w_down1, norm_mix, w_in, a_sink, w_out, norm_ffn2, w_gate2, w_up2, w_down2, norm_final, loss_target, m_norm_ffn1, m_w_gate1, m_w_up1, m_w_down1, m_norm_mix, m_w_in, m_a_sink, m_w_out, m_norm_ffn2, m_w_gate2, m_w_up2, m_w_down2, m_norm_final, v_norm_ffn1, v_w_gate1, v_w_up1, v_w_down1, v_norm_mix, v_w_in, v_a_sink, v_w_out, v_norm_ffn2, v_w_gate2, v_w_up2, v_w_down2, v_norm_final):
    T, D = x.shape[1], x.shape[2]
    shards = dict(wg1=w_gate1[0], wu1=w_up1[0], wd1=w_down1[0], w_in=w_in[0], w_out=w_out[0], wg2=w_gate2[0], wu2=w_up2[0], wd2=w_down2[0])
    moms = dict(wg1=(m_w_gate1, v_w_gate1), wu1=(m_w_up1, v_w_up1), wd1=(m_w_down1, v_w_down1), w_in=(m_w_in, v_w_in),
                w_out=(m_w_out, v_w_out), wg2=(m_w_gate2, v_w_gate2), wu2=(m_w_up2, v_w_up2), wd2=(m_w_down2, v_w_down2))

    comm = _Comm(shards)

    norms = (norm_ffn1, norm_mix, norm_ffn2, norm_final.reshape(1, D))
    grad_x, small = _local_step(x[0], positions[0], loss_target[0], norms, a_sink[0], comm)

    partial = comm.finish()

    def pad_row(a):
        a = a.reshape(-1)
        return jnp.pad(a, (0, D - a.shape[0]))

    row4 = pad_row(jnp.concatenate([small["sink"], small["loss"].reshape(1)]))
    vec = jnp.stack([small["g1"], small["gm"], small["g2"], small["gf"], row4] + [jnp.zeros((D,), F32)] * 3, axis=0)
    red = _allreduce_small(vec)
    loss = red[4, 8]
    g_small = jnp.stack([red[0], red[1], red[2], red[3], pad_row(red[4, 0:8])] + [jnp.zeros((D,), F32)] * 3, axis=0)

    def small_stack(a1, am, a2, af, ask):
        return jnp.stack([pad_row(a1), pad_row(am), pad_row(a2), pad_row(af), pad_row(ask)] + [jnp.zeros((D,), F32)] * 3, axis=0)

    w_small = small_stack(norm_ffn1, norm_mix, norm_ffn2, norm_final, a_sink)
    m_small = small_stack(m_norm_ffn1, m_norm_mix, m_norm_ffn2, m_norm_final, m_a_sink)
    v_small = small_stack(v_norm_ffn1, v_norm_mix, v_norm_ffn2, v_norm_final, v_a_sink)
    live = small_stack(jnp.ones_like(norm_ffn1), jnp.ones_like(norm_mix), jnp.ones_like(norm_ffn2), jnp.ones_like(norm_final), jnp.ones_like(a_sink))
    v_small = jnp.where(live > 0, v_small, 1.0)

    upd = {}
    for k in BIG:
        m_, v_ = moms[k]
        shp = m_.shape
        upd[k] = tuple(a.reshape(shp) for a in _adamw(shards[k], partial[k][0], partial[k][1], m_[0], v_[0], f"adamw_{k}"))
    _, ds_, nms_, nvs_ = _adamw(w_small, g_small, jnp.zeros_like(g_small), m_small, v_small, "adamw_small")

    def small_out(arr):
        return [arr[0].reshape(1, D), arr[1].reshape(1, D), arr[2].reshape(1, D), arr[3], arr[4, 0:8].reshape(1, 8)]

    gs_, dss, nmss, nvss = small_out(g_small), small_out(ds_), small_out(nms_), small_out(nvs_)

    def ordered(i):
        sm = (gs_, dss, nmss, nvss)[i]
        return [sm[0], upd["wg1"][i], upd["wu1"][i], upd["wd1"][i], sm[1], upd["w_in"][i], sm[4], upd["w_out"][i], sm[2],
                upd["wg2"][i], upd["wu2"][i], upd["wd2"][i], sm[3]]

    return (loss, grad_x[None], *ordered(0), *ordered(1), *ordered(2), *ordered(3))
```

```python
import jax
import jax.numpy as jnp
from jax import lax
from jax.experimental import pallas as pl
from jax.experimental.pallas import tpu as pltpu

F32 = jnp.float32
BF16 = jnp.bfloat16

HEAD_DIM = 64
LANES = 128
SUBLANES = 8
A_Q_W, A_KV_W, B_W = 512, 128, 512
A_HALF_WINDOW = 128
B_PATTERNS = ((128, 1), (512, 4), (2048, 16))
ROPE_THETA = 10000.0
NORM_EPS = 1e-6
FFN_RES_WEIGHT = 0.5
ADAM_LR, ADAM_B1, ADAM_B2, ADAM_EPS, ADAM_WD, ADAM_STEP = 0.001, 0.9, 0.999, 1e-08, 0.01, 10
N_CHIPS = 4
N_DEV = 8
QB = 128
NEG = -1e30
VMEM_LIMIT = 56 * 1024 * 1024
MESH = pl.DeviceIdType.MESH
ANY = pl.BlockSpec(memory_space=pl.ANY)


def _params(sem=None):
    return pltpu.CompilerParams(dimension_semantics=sem, vmem_limit_bytes=VMEM_LIMIT)


def _sds(shape, dtype):
    return jax.ShapeDtypeStruct(tuple(shape), dtype)


def _dot(a, b):
    return jnp.dot(a, b, preferred_element_type=F32)


def _dot_nt(a, b):
    return lax.dot_general(a, b, (((1,), (1,)), ((), ())), preferred_element_type=F32)


def _dot_tn(a, b):
    return lax.dot_general(a, b, (((0,), (0,)), ((), ())), preferred_element_type=F32)


def _rms_stats(x):
    r = lax.rsqrt(jnp.mean(x * x, axis=-1, keepdims=True) + NORM_EPS)
    return x * r, r


def _rms_bwd(dh, x, g):
    xhat, r = _rms_stats(x)
    dxn = dh * g
    dx = r * (dxn - xhat * jnp.mean(dxn * xhat, axis=-1, keepdims=True))
    tm, d = x.shape
    dg = (dh * xhat).reshape(tm // SUBLANES, SUBLANES, d).sum(axis=0)
    return dx, dg


def _sigmoid(x):
    return 1.0 / (1.0 + jnp.exp(-x))


def _swap32(t):
    n = t.shape[-1]
    lane = lax.broadcasted_iota(jnp.int32, t.shape, t.ndim - 1)
    return jnp.where((lane % HEAD_DIM) < HEAD_DIM // 2, pltpu.roll(t, n - HEAD_DIM // 2, axis=t.ndim - 1),
                     pltpu.roll(t, HEAD_DIM // 2, axis=t.ndim - 1))


def _cast_place(me_arr, w, name):
    R, C = w.shape
    tr = R // 2 if (R // 2) % 16 == 0 else R

    def body(me_ref, w_ref, o_ref):
        o_ref[...] = w_ref[...].astype(BF16)

    grid_spec = pltpu.PrefetchScalarGridSpec(
        num_scalar_prefetch=1, grid=(R // tr,), in_specs=[pl.BlockSpec((tr, C), lambda t, me: (t, 0))],
        out_specs=pl.BlockSpec((None, tr, C), lambda t, me: (me[0], t, 0)))
    return pl.pallas_call(body, name=name, grid_spec=grid_spec, out_shape=_sds((N_CHIPS, R, C), BF16),
                          compiler_params=_params(("parallel",)))(me_arr, w)


HBM = pl.BlockSpec(memory_space=pltpu.HBM)
SEM = pl.BlockSpec(memory_space=pltpu.SEMAPHORE)


def _push_start(name, bufs, ncopies, plan, after):
    nb = len(bufs)

    def body(*refs):
        send, recv, token = refs[nb + 1], refs[nb + 2], refs[-1]
        for i, (src, dst, dev) in enumerate(plan(refs[:nb])):
            pltpu.make_async_remote_copy(src_ref=src, dst_ref=dst, send_sem=send.at[i], recv_sem=recv.at[i],
                                         device_id=dev, device_id_type=MESH).start()
        token[...] = jnp.zeros_like(token)

    outs = pl.pallas_call(
        body, name=name,
        out_shape=(pltpu.SemaphoreType.DMA((ncopies,)), pltpu.SemaphoreType.DMA((ncopies,)), *[pltpu.HBM(b.shape, b.dtype) for b in bufs],
                   _sds((SUBLANES, LANES), F32)),
        in_specs=[HBM] * nb + [ANY], out_specs=(SEM, SEM, *([HBM] * nb), pl.BlockSpec(memory_space=pltpu.VMEM)),
        input_output_aliases={i: 2 + i for i in range(nb)},
        compiler_params=pltpu.CompilerParams(has_side_effects=pltpu.SideEffectType.DATAFLOW_SIDE_EFFECTING),
    )(*[pltpu.with_memory_space_constraint(b, pltpu.HBM) for b in bufs], after)
    return outs[0], outs[1], list(outs[2:2 + nb]), outs[-1]


def _push_wait(name, send, recv, bufs, plan, after):
    nb = len(bufs)

    def body(*refs):
        send_ref, recv_ref = refs[nb], refs[nb + 1]
        for i, (src, dst, dev) in enumerate(plan(refs[:nb])):
            cp = pltpu.make_async_remote_copy(src_ref=src, dst_ref=dst, send_sem=send_ref.at[i], recv_sem=recv_ref.at[i],
                                              device_id=dev, device_id_type=MESH)
            cp.wait_send()
            cp.wait_recv()

    outs = pl.pallas_call(
        body, name=name, out_shape=tuple(pltpu.HBM(b.shape, b.dtype) for b in bufs),
        in_specs=[HBM] * nb + [SEM, SEM, ANY], out_specs=tuple([HBM] * nb), input_output_aliases={i: i for i in range(nb)},
        compiler_params=pltpu.CompilerParams(has_side_effects=pltpu.SideEffectType.DATAFLOW_SIDE_EFFECTING),
    )(*bufs, send, recv, after)
    return list(outs)


def _mesh_pos():
    return lax.axis_index("x"), lax.axis_index("y"), lax.axis_index("c")


def _chip_peers(x, y, c):
    return [((1 - x, y, c), 2 * (1 - x) + y), ((x, 1 - y, c), 2 * x + (1 - y)), ((1 - x, 1 - y, c), 2 * (1 - x) + (1 - y))]


def _gather_plan(n):
    def plan(refs):
        x, y, c = _mesh_pos()
        me = 2 * x + y
        return [(refs[k].at[me], refs[k].at[me], dev) for k in range(n) for dev, _ in _chip_peers(x, y, c)]
    return plan


def _gather_weights(fulls):
    n = len(fulls)

    def body(*refs):
        ins, outs = refs[:n], refs[n:2 * n]
        ici_send, ici_recv, d2d_send, d2d_recv = refs[2 * n:]
        x, y, c = _mesh_pos()
        me = 2 * x + y
        sibling = (x, y, 1 - c)
        peers = _chip_peers(x, y, c)

        def half(k, who):
            r2 = fulls[k].shape[1] // 2
            return pl.ds(pl.multiple_of(who * r2, 16), r2)

        first = []
        for k in range(n):
            for rel, (dev, _) in enumerate(peers):
                cp = pltpu.make_async_remote_copy(src_ref=ins[k].at[me, half(k, c), :], dst_ref=outs[k].at[me, half(k, c), :],
                                                  send_sem=ici_send.at[k * 3 + rel], recv_sem=ici_recv.at[k * 3 + rel],
                                                  device_id=dev, device_id_type=MESH)
                cp.start()
                first.append(cp)
        passed = []
        for k in range(n):
            for rel, (dev, chip) in enumerate(peers):
                blk = outs[k].at[chip, half(k, c), :]
                pltpu.make_async_remote_copy(src_ref=blk, dst_ref=blk, send_sem=ici_send.at[k * 3 + rel], recv_sem=ici_recv.at[k * 3 + rel],
                                             device_id=dev, device_id_type=MESH).wait_recv()
                cp = pltpu.make_async_remote_copy(src_ref=blk, dst_ref=blk, send_sem=d2d_send.at[k * 3 + rel], recv_sem=d2d_recv.at[k * 3 + rel],
                                                  device_id=sibling, device_id_type=MESH)
                cp.start()
                passed.append(cp)
        for k in range(n):
            for rel, (dev, chip) in enumerate(peers):
                blk = outs[k].at[chip, half(k, 1 - c), :]
                pltpu.make_async_remote_copy(src_ref=blk, dst_ref=blk, send_sem=d2d_send.at[k * 3 + rel], recv_sem=d2d_recv.at[k * 3 + rel],
                                             device_id=sibling, device_id_type=MESH).wait_recv()
        for cp in first + passed:
            cp.wait_send()

    return pl.pallas_call(
        body, name="gather_weights", out_shape=[_sds(f.shape, BF16) for f in fulls],
        in_specs=[ANY] * n, out_specs=[ANY] * n, input_output_aliases={k: k for k in range(n)},
        scratch_shapes=[pltpu.SemaphoreType.DMA((n * 3,))] * 4, compiler_params=_params())(*fulls)


def _resident(shape):
    return pl.BlockSpec(shape, lambda i: (0,) * len(shape), pipeline_mode=pl.Buffered(1))


FFN_CHUNK = 512


def _chunks(n, step):
    return [(c0, min(step, n - c0)) for c0 in range(0, n, step)]


def _ffn_fwd(x, g, wgt, wut, wd, name, tm=256):
    T, D = x.shape
    F = wd.shape[0]

    def body(x_ref, g_ref, wg_ref, wu_ref, wd_ref, xo_ref, h_ref, gate_ref, up_ref, act_ref, h_scr):
        xv = x_ref[...]
        xhat, _ = _rms_stats(xv)
        hb = (xhat * g_ref[...]).astype(BF16)
        h_scr[...] = hb
        h_ref[...] = hb
        acc = None
        for c0, cw in _chunks(F, FFN_CHUNK):
            h = h_scr[...]
            gate = _dot_nt(h, wg_ref[c0:c0 + cw, :])
            up = _dot_nt(h, wu_ref[c0:c0 + cw, :])
            act = ((gate * _sigmoid(gate)) * up).astype(BF16)
            gate_ref[:, c0:c0 + cw] = gate.astype(BF16)
            up_ref[:, c0:c0 + cw] = up.astype(BF16)
            act_ref[:, c0:c0 + cw] = act
            d = _dot(act, wd_ref[c0:c0 + cw, :])
            acc = d if acc is None else acc + d
        xo_ref[...] = xv + FFN_RES_WEIGHT * acc

    row = pl.BlockSpec((tm, D), lambda i: (i, 0))
    saved = pl.BlockSpec((tm, F), lambda i: (i, 0))
    return pl.pallas_call(
        body, name=name, grid=(T // tm,),
        in_specs=[row, pl.BlockSpec((1, D), lambda i: (0, 0)), _resident(wgt.shape), _resident(wut.shape), _resident(wd.shape)],
        out_specs=[row, row, saved, saved, saved],
        out_shape=[_sds((T, D), F32), _sds((T, D), BF16), _sds((T, F), BF16), _sds((T, F), BF16), _sds((T, F), BF16)],
        scratch_shapes=[pltpu.VMEM((tm, D), BF16)], compiler_params=_params(("parallel",)))(x, g, wgt, wut, wd)


def _ffn_dx(dxo, x, g, gate_s, up_s, wgt, wut, wd, name, tm=256):
    T, D = x.shape
    F = wd.shape[0]

    def body(dxo_ref, x_ref, g_ref, gate_ref, up_ref, wg_ref, wu_ref, wd_ref, dx_ref, dff_ref, dgate_ref, dup_ref, dg_ref, dff_scr):
        @pl.when(pl.program_id(0) == 0)
        def _():
            dg_ref[...] = jnp.zeros_like(dg_ref)

        d = (FFN_RES_WEIGHT * dxo_ref[...]).astype(BF16)
        dff_scr[...] = d
        dff_ref[...] = d
        dh = None
        for c0, cw in _chunks(F, FFN_CHUNK):
            da = _dot_nt(dff_scr[...], wd_ref[c0:c0 + cw, :])
            gate = gate_ref[:, c0:c0 + cw].astype(F32)
            up = up_ref[:, c0:c0 + cw].astype(F32)
            s = _sigmoid(gate)
            silu = gate * s
            dup = (da * silu).astype(BF16)
            dgate = (da * up * (s * (1.0 + gate * (1.0 - s)))).astype(BF16)
            dgate_ref[:, c0:c0 + cw] = dgate
            dup_ref[:, c0:c0 + cw] = dup
            t = _dot(dgate, wg_ref[c0:c0 + cw, :]) + _dot(dup, wu_ref[c0:c0 + cw, :])
            dh = t if dh is None else dh + t
        dxn, dg = _rms_bwd(dh, x_ref[...], g_ref[...])
        dg_ref[...] += dg
        dx_ref[...] = dxo_ref[...] + dxn

    row = pl.BlockSpec((tm, D), lambda i: (i, 0))
    saved = pl.BlockSpec((tm, F), lambda i: (i, 0))
    return pl.pallas_call(
        body, name=name, grid=(T // tm,),
        in_specs=[row, row, pl.BlockSpec((1, D), lambda i: (0, 0)), saved, saved, _resident(wgt.shape), _resident(wut.shape),
                  _resident(wd.shape)],
        out_specs=[row, row, saved, saved, pl.BlockSpec((SUBLANES, D), lambda i: (0, 0))],
        out_shape=[_sds((T, D), F32), _sds((T, D), BF16), _sds((T, F), BF16), _sds((T, F), BF16), _sds((SUBLANES, D), F32)],
        scratch_shapes=[pltpu.VMEM((tm, D), BF16)], compiler_params=_params(("arbitrary",)))(dxo, x, g, gate_s, up_s, wgt, wut, wd)


def _tn(a, b, mb, name, tk=2048, dep=None):
    T, M = a.shape
    N = b.shape[1]
    nt = T // tk

    def body(a_ref, b_ref, *refs):
        o_ref, ob_ref = refs[-2:]

        @pl.when(pl.program_id(1) == 0)
        def _():
            o_ref[...] = jnp.zeros_like(o_ref)

        o_ref[...] += _dot_tn(a_ref[...].astype(BF16), b_ref[...].astype(BF16))

        @pl.when(pl.program_id(1) == nt - 1)
        def _():
            ob_ref[...] = o_ref[...].astype(BF16)

    o_spec = pl.BlockSpec((mb, N), lambda g, t: (g, 0))
    return pl.pallas_call(
        body, name=name, grid=(M // mb, nt),
        in_specs=[pl.BlockSpec((tk, mb), lambda g, t: (t, g)), pl.BlockSpec((tk, N), lambda g, t: (t, 0))] + ([ANY] if dep is not None else []),
        out_specs=[o_spec, o_spec], out_shape=[_sds((M, N), F32), _sds((M, N), BF16)],
        compiler_params=_params(("parallel", "arbitrary")))(a, b, *([dep] if dep is not None else []))


def _rope_tables(pos_col, inv_freq):
    T = pos_col.shape[0]

    def body(p_ref, f_ref, c_ref, s_ref):
        ang = p_ref[...].astype(F32) * f_ref[...]
        lane = lax.broadcasted_iota(jnp.int32, ang.shape, 1)
        c_ref[...] = jnp.cos(ang)
        sn = jnp.sin(ang)
        s_ref[...] = jnp.where((lane % HEAD_DIM) < HEAD_DIM // 2, -sn, sn)

    tm = 1024
    return pl.pallas_call(
        body, name="rope_tables", grid=(T // tm,),
        in_specs=[pl.BlockSpec((tm, 1), lambda i: (i, 0)), pl.BlockSpec((1, LANES), lambda i: (0, 0))],
        out_specs=[pl.BlockSpec((tm, LANES), lambda i: (i, 0))] * 2,
        out_shape=[_sds((T, LANES), F32)] * 2, compiler_params=_params(("parallel",)))(pos_col, inv_freq)


def _deinterleave(scr, out_ref, d, tm, nblk):
    for r in range(d):
        for cb in range(nblk):
            out_ref[r, :, cb * LANES:(cb + 1) * LANES] = scr[cb, pl.ds(r, tm // d, stride=d), :].astype(out_ref.dtype)


def _interleave(in_ref, scr, d, tm, nblk):
    for r in range(d):
        for cb in range(nblk):
            scr[cb, pl.ds(r, tm // d, stride=d), :] = in_ref[r, :, cb * LANES:(cb + 1) * LANES].astype(F32)


def _proj_rope(x, g, w_in, cos, sin, tm=512):
    T, D = x.shape
    dils = [d for _, d in B_PATTERNS if d > 1]
    nbb = B_W // LANES
    scale = HEAD_DIM ** -0.5
    cuts = [0, A_Q_W, A_Q_W + A_KV_W, A_Q_W + 2 * A_KV_W, A_Q_W + 2 * A_KV_W + B_W, A_Q_W + 2 * A_KV_W + 2 * B_W,
            A_Q_W + 2 * A_KV_W + 3 * B_W]

    def body(x_ref, g_ref, w_ref, c_ref, s_ref, h_ref, aq_ref, ak_ref, av_ref, *rest):
        b_refs, scr = rest[:-1], rest[-1]
        xhat, _ = _rms_stats(x_ref[...])
        h = (xhat * g_ref[...]).astype(BF16)
        h_ref[...] = h
        cs, sn = c_ref[...], s_ref[...]

        def seg(idx, rope, mult):
            lo, hi = cuts[idx], cuts[idx + 1]
            blocks = []
            whole = _dot_nt(h, w_ref[lo:hi, :])
            for cb in range((hi - lo) // LANES):
                p = whole[:, cb * LANES:(cb + 1) * LANES]
                if rope:
                    p = p * cs + _swap32(p) * sn
                if mult != 1.0:
                    p = p * mult
                blocks.append(p)
            return blocks

        for idx, ref, rope, mult in ((0, aq_ref, True, scale), (1, ak_ref, True, 1.0), (2, av_ref, False, 1.0)):
            for cb, p in enumerate(seg(idx, rope, mult)):
                ref[:, cb * LANES:(cb + 1) * LANES] = p.astype(BF16)
        for which, (idx, rope, mult) in enumerate(((3, True, scale), (4, True, 1.0), (5, False, 1.0))):
            for cb, p in enumerate(seg(idx, rope, mult)):
                b_refs[which][:, cb * LANES:(cb + 1) * LANES] = p.astype(BF16)
                scr[cb] = p
            for di, d in enumerate(dils):
                _deinterleave(scr, b_refs[3 * (di + 1) + which], d, tm, nbb)

    row = lambda w: pl.BlockSpec((tm, w), lambda i: (i, 0))
    out_specs = [row(D), row(A_Q_W), row(A_KV_W), row(A_KV_W)] + [row(B_W)] * 3
    out_shape = [_sds((T, D), BF16), _sds((T, A_Q_W), BF16), _sds((T, A_KV_W), BF16), _sds((T, A_KV_W), BF16)] + [_sds((T, B_W), BF16)] * 3
    for d in dils:
        out_specs += [pl.BlockSpec((d, tm // d, B_W), lambda i: (0, i, 0))] * 3
        out_shape += [_sds((d, T // d, B_W), BF16)] * 3
    return pl.pallas_call(
        body, name="proj_rope", grid=(T // tm,),
        in_specs=[row(D), pl.BlockSpec((1, D), lambda i: (0, 0)), pl.BlockSpec(w_in.shape, lambda i: (0, 0)), row(LANES), row(LANES)],
        out_specs=out_specs, out_shape=out_shape, scratch_shapes=[pltpu.VMEM((nbb, tm, LANES), F32)],
        compiler_params=_params(("parallel",)))(x, g, w_in, cos, sin)


def _band_bias(rel, kw, hw):
    ri = lax.broadcasted_iota(jnp.int32, (QB, kw), 0)
    ci = lax.broadcasted_iota(jnp.int32, (QB, kw), 1)
    return jnp.where(jnp.abs(ri + rel - ci) <= hw, 0.0, NEG).astype(F32)


def _band_setup(bias_scr, kw, hw):
    if bias_scr is not None:
        for i in range(3):
            bias_scr[i] = _band_bias(i * hw, kw, hw)


def _band_window(bias_scr, qs, L, kw, hw):
    ws = pl.multiple_of(jnp.clip(qs - hw, 0, L - kw), 64)
    if bias_scr is None:
        return ws, _band_bias(qs - ws, kw, hw)
    return ws, bias_scr[lax.shift_right_logical(qs - ws, hw.bit_length() - 1)]


def _dup_kv_head(src_ref, dst_ref, head, L):
    step = min(L, 1024)
    for r0 in range(0, L, step):
        xf = src_ref[r0:r0 + step, :].astype(F32)
        lane = lax.broadcasted_iota(jnp.int32, xf.shape, 1)
        keep = jnp.logical_xor(lane < HEAD_DIM, head == 1)
        dst_ref[r0:r0 + step, :] = jnp.where(keep, xf, pltpu.roll(xf, HEAD_DIM, axis=1)).astype(dst_ref.dtype)


def _attn_fwd(q, k, v, sink, hw, gqa, out_dtype, name):
    NB, L, Cq = q.shape
    Ls = min(L, 2048)
    kw = min(QB + 2 * hw, L)
    tables = L >= QB + 2 * hw
    unroll = min(4, Ls // QB)

    def body(sink_ref, q_ref, k_ref, v_ref, o_ref, lse_ref, *scr):
        b, s_idx = pl.program_id(1), pl.program_id(2)
        bias_scr = scr[0] if tables else None
        _band_setup(bias_scr, kw, hw)
        if gqa:
            kd, vd = scr[-2:]

            @pl.when(s_idx == 0)
            def _():
                _dup_kv_head(k_ref, kd, b // 2, L)
                _dup_kv_head(v_ref, vd, b // 2, L)
        else:
            kd, vd = k_ref, v_ref
        lane = lax.broadcasted_iota(jnp.int32, (QB, LANES), 1)
        lo = lane < HEAD_DIM

        def block(ql):
            qs = s_idx * Ls + ql
            ws, bias = _band_window(bias_scr, qs, L, kw, hw)
            qv = q_ref[pl.ds(ql, QB), :]
            kv_, vv = kd[pl.ds(ws, kw), :], vd[pl.ds(ws, kw), :]
            res = []
            for half in (0, 1):
                qm = jnp.where(lo if half == 0 else jnp.logical_not(lo), qv, jnp.zeros_like(qv))
                s = _dot_nt(qm, kv_) + bias
                m = jnp.max(s, axis=-1, keepdims=True)
                if gqa:
                    sk = sink_ref[2 * b + half]
                    m = jnp.maximum(m, sk)
                p = jnp.exp(s - m)
                den = jnp.sum(p, axis=-1, keepdims=True)
                if gqa:
                    den = den + jnp.exp(sk - m)
                res.append((_dot(p.astype(BF16), vv) * (1.0 / den), m + jnp.log(den)))
            o_ref[pl.ds(ql, QB), :] = jnp.where(lo, res[0][0], res[1][0]).astype(o_ref.dtype)
            lse_ref[pl.ds(ql, QB), :] = jnp.where(lo, res[0][1], res[1][1])

        def step(n, carry):
            for u in range(unroll):
                block(pl.multiple_of((n * unroll + u) * QB, QB))
            return carry

        lax.fori_loop(0, Ls // (QB * unroll), step, 0)

    kv_map = (lambda r, b, s: (r, 0, 0)) if gqa else (lambda r, b, s: (r, 0, b))
    seg = pl.BlockSpec((None, Ls, LANES), lambda r, b, s: (r, s, b))
    return pl.pallas_call(
        body, name=name, grid=(NB, Cq // LANES, L // Ls),
        in_specs=[pl.BlockSpec(memory_space=pltpu.SMEM), seg, pl.BlockSpec((None, L, LANES), kv_map), pl.BlockSpec((None, L, LANES), kv_map)],
        out_specs=[seg, seg], out_shape=[_sds((NB, L, Cq), out_dtype), _sds((NB, L, Cq), F32)],
        scratch_shapes=([pltpu.VMEM((3, QB, kw), F32)] if tables else []) + ([pltpu.VMEM((L, LANES), BF16)] * 2 if gqa else []),
        compiler_params=_params(("parallel", "parallel", "arbitrary")))(sink, q, k, v)


def _attn_bwd(q, k, v, do, lse, delta, sink, hw, gqa, name):
    NB, L, Cq = q.shape
    Ck = k.shape[2]
    Ls = min(L, 2048)
    kw = min(QB + 2 * hw, L)
    reps = kw // LANES
    nseg = L // Ls
    scale = HEAD_DIM ** -0.5
    tables = L >= QB + 2 * hw
    unroll = min(4, Ls // QB)

    def body(sink_ref, q_ref, do_ref, lse_ref, dl_ref, k_ref, v_ref, dq_ref, dk_ref, dv_ref, dsk_ref, *scr):
        b, s_idx = pl.program_id(1), pl.program_id(2)
        lane = lax.broadcasted_iota(jnp.int32, (QB, LANES), 1)
        lo = lane < HEAD_DIM
        bias_scr = scr[0] if tables else None
        _band_setup(bias_scr, kw, hw)
        if gqa:
            kd, vd, dk_acc, dv_acc, dsk_acc = scr[-5:]

            @pl.when(s_idx == 0)
            def _():
                _dup_kv_head(k_ref, kd, b // 2, L)
                _dup_kv_head(v_ref, vd, b // 2, L)
                dk_acc[...] = jnp.zeros_like(dk_acc)
                dv_acc[...] = jnp.zeros_like(dv_acc)
                dsk_acc[...] = jnp.zeros_like(dsk_acc)

            @pl.when((s_idx == 0) & (b == 0))
            def _():
                dk_ref[...] = jnp.zeros_like(dk_ref)
                dv_ref[...] = jnp.zeros_like(dv_ref)
        else:
            kd, vd, dk_acc, dv_acc = k_ref, v_ref, dk_ref, dv_ref

            @pl.when(s_idx == 0)
            def _():
                dk_ref[...] = jnp.zeros_like(dk_ref)
                dv_ref[...] = jnp.zeros_like(dv_ref)

        def block(ql):
            qs = s_idx * Ls + ql
            ws, bias = _band_window(bias_scr, qs, L, kw, hw)
            qv, dov = q_ref[pl.ds(ql, QB), :], do_ref[pl.ds(ql, QB), :]
            lse, dl = lse_ref[pl.ds(ql, QB), :], dl_ref[pl.ds(ql, QB), :]
            kv_, vv = kd[pl.ds(ws, kw), :], vd[pl.ds(ws, kw), :]
            lse_sw, dl_sw = pltpu.roll(lse, HEAD_DIM, axis=1), pltpu.roll(dl, HEAD_DIM, axis=1)
            dqs = []
            dk_c = jnp.zeros((kw, LANES), F32)
            dv_c = jnp.zeros((kw, LANES), F32)
            for half in (0, 1):
                msk = lo if half == 0 else jnp.logical_not(lo)
                qm = jnp.where(msk, qv, jnp.zeros_like(qv))
                dom = jnp.where(msk, dov, jnp.zeros_like(dov))
                lse_h = jnp.where(msk, lse, lse_sw)
                dl_h = jnp.where(msk, dl, dl_sw)
                s = _dot_nt(qm, kv_) + bias
                p = jnp.exp(s - jnp.tile(lse_h, (1, reps)))
                dp = _dot_nt(dom, vv)
                ds = (p * (dp - jnp.tile(dl_h, (1, reps)))).astype(BF16)
                dqs.append(_dot(ds, kv_))
                dk_c = dk_c + _dot_tn(ds, qm)
                dv_c = dv_c + _dot_tn(p.astype(BF16), dom)
            dq_ref[pl.ds(ql, QB), :] = (jnp.where(lo, dqs[0], dqs[1]) * scale).astype(dq_ref.dtype)
            dk_acc[pl.ds(ws, kw), :] += dk_c
            dv_acc[pl.ds(ws, kw), :] += dv_c
            if gqa:
                sk = jnp.where(lo, sink_ref[2 * b], sink_ref[2 * b + 1])
                dsk_acc[...] += -jnp.exp(sk - lse) * dl

        def step(n, carry):
            for u in range(unroll):
                block(pl.multiple_of((n * unroll + u) * QB, QB))
            return carry

        lax.fori_loop(0, Ls // (QB * unroll), step, 0)

        if gqa:
            @pl.when(s_idx == nseg - 1)
            def _():
                step_rows = min(L, 1024)
                for r0 in range(0, L, step_rows):
                    lanek = lax.broadcasted_iota(jnp.int32, (step_rows, LANES), 1)
                    mine = jnp.logical_xor(lanek < HEAD_DIM, (b // 2) == 1)
                    for acc, ref in ((dk_acc, dk_ref), (dv_acc, dv_ref)):
                        a = acc[r0:r0 + step_rows, :]
                        ref[r0:r0 + step_rows, :] += jnp.where(mine, a + pltpu.roll(a, HEAD_DIM, axis=1), 0.0)
                dsk_ref[...] = dsk_acc[...].reshape(QB // SUBLANES, SUBLANES, LANES).sum(axis=0)
        else:
            dsk_ref[...] = jnp.zeros_like(dsk_ref)

    kv_map = (lambda r, b, s: (r, 0, 0)) if gqa else (lambda r, b, s: (r, 0, b))
    seg = pl.BlockSpec((None, Ls, LANES), lambda r, b, s: (r, s, b))
    full = pl.BlockSpec((None, L, LANES), kv_map)
    scratch = [pltpu.VMEM((3, QB, kw), F32)] if tables else []
    if gqa:
        scratch += [pltpu.VMEM((L, LANES), BF16)] * 2 + [pltpu.VMEM((L, LANES), F32)] * 2 + [pltpu.VMEM((QB, LANES), F32)]
    return pl.pallas_call(
        body, name=name, grid=(NB, Cq // LANES, nseg),
        in_specs=[pl.BlockSpec(memory_space=pltpu.SMEM), seg, seg, seg, seg, full, full],
        out_specs=[seg, full, full, pl.BlockSpec((None, None, SUBLANES, LANES), lambda r, b, s: (r, b, 0, 0))],
        out_shape=[_sds((NB, L, Cq), BF16), _sds((NB, L, Ck), F32), _sds((NB, L, Ck), F32),
                   _sds((NB, Cq // LANES, SUBLANES, LANES), F32)],
        scratch_shapes=scratch,
        compiler_params=_params(("arbitrary", "arbitrary", "arbitrary")))(sink, q, do, lse, delta, k, v)


def _merge_b(a_out, o1, l1, o4, l4, o16, l16, tm=512):
    T = a_out.shape[0]
    nbb = B_W // LANES

    def body(a_ref, o1_ref, l1_ref, o4_ref, l4_ref, o16_ref, l16_ref, cat_ref, lg1_ref, lg4_ref, lg16_ref, so, sl, slg):
        _interleave(o4_ref, so.at[0], 4, tm, nbb)
        _interleave(l4_ref, sl.at[0], 4, tm, nbb)
        _interleave(o16_ref, so.at[1], 16, tm, nbb)
        _interleave(l16_ref, sl.at[1], 16, tm, nbb)
        cat_ref[:, 0:A_Q_W] = a_ref[...]
        for cb in range(nbb):
            cols = slice(cb * LANES, (cb + 1) * LANES)
            os_ = (o1_ref[:, cols], so[0, cb], so[1, cb])
            ls_ = (l1_ref[:, cols], sl[0, cb], sl[1, cb])
            m = jnp.maximum(jnp.maximum(ls_[0], ls_[1]), ls_[2])
            es = [jnp.exp(l - m) for l in ls_]
            den = es[0] + es[1] + es[2]
            out = (es[0] * os_[0] + es[1] * os_[1] + es[2] * os_[2]) * (1.0 / den)
            lg = m + jnp.log(den)
            cat_ref[:, A_Q_W + cb * LANES:A_Q_W + (cb + 1) * LANES] = out.astype(BF16)
            lg1_ref[:, cols] = lg
            slg[cb] = lg
        _deinterleave(slg, lg4_ref, 4, tm, nbb)
        _deinterleave(slg, lg16_ref, 16, tm, nbb)

    row = lambda w: pl.BlockSpec((tm, w), lambda i: (i, 0))
    perm = lambda d: pl.BlockSpec((d, tm // d, B_W), lambda i: (0, i, 0))
    return pl.pallas_call(
        body, name="merge_patterns", grid=(T // tm,),
        in_specs=[row(A_Q_W), row(B_W), row(B_W), perm(4), perm(4), perm(16), perm(16)],
        out_specs=[row(A_Q_W + B_W), row(B_W), perm(4), perm(16)],
        out_shape=[_sds((T, A_Q_W + B_W), BF16), _sds((T, B_W), F32), _sds((4, T // 4, B_W), F32), _sds((16, T // 16, B_W), F32)],
        scratch_shapes=[pltpu.VMEM((2, nbb, tm, LANES), F32), pltpu.VMEM((2, nbb, tm, LANES), F32), pltpu.VMEM((nbb, tm, LANES), F32)],
        compiler_params=_params(("parallel",)))(a_out, o1, l1, o4, l4, o16, l16)


def _out_proj(x, cat, w_out, tm=512):
    T, D = x.shape

    def body(x_ref, c_ref, w_ref, o_ref):
        o_ref[...] = x_ref[...] + _dot(c_ref[...], w_ref[...])

    row = lambda w: pl.BlockSpec((tm, w), lambda i: (i, 0))
    return pl.pallas_call(
        body, name="out_proj", grid=(T // tm,), in_specs=[row(D), row(cat.shape[1]), pl.BlockSpec(w_out.shape, lambda i: (0, 0))],
        out_specs=row(D), out_shape=_sds((T, D), F32), compiler_params=_params(("parallel",)))(x, cat, w_out)


def _final_loss(x, g, target, tm=512):
    T, D = x.shape

    def body(x_ref, g_ref, t_ref, dx_ref, dg_ref, loss_ref):
        @pl.when(pl.program_id(0) == 0)
        def _():
            dg_ref[...] = jnp.zeros_like(dg_ref)
            loss_ref[...] = jnp.zeros_like(loss_ref)

        xv, gv = x_ref[...], g_ref[...]
        xhat, _ = _rms_stats(xv)
        err = xhat * gv - t_ref[...]
        loss_ref[...] += 0.5 * jnp.sum(jnp.sum(err * err, axis=-1, keepdims=True) * (1.0 / D), axis=0, keepdims=True)
        dx, dg = _rms_bwd(err * (1.0 / D), xv, gv)
        dx_ref[...] = dx
        dg_ref[...] += dg

    row = pl.BlockSpec((tm, D), lambda i: (i, 0))
    return pl.pallas_call(
        body, name="final_loss", grid=(T // tm,), in_specs=[row, pl.BlockSpec((1, D), lambda i: (0, 0)), row],
        out_specs=[row, pl.BlockSpec((SUBLANES, D), lambda i: (0, 0)), pl.BlockSpec((SUBLANES, LANES), lambda i: (0, 0))],
        out_shape=[_sds((T, D), F32), _sds((SUBLANES, D), F32), _sds((SUBLANES, LANES), F32)],
        compiler_params=_params(("arbitrary",)))(x, g, target)


def _dcat(dx, w_out, cat, tm=512):
    T, D = dx.shape
    C = cat.shape[1]
    nba, nbb = A_Q_W // LANES, B_W // LANES

    def body(dx_ref, w_ref, cat_ref, doa_ref, dla_ref, dob1_ref, dlb1_ref, dob4_ref, dlb4_ref, dob16_ref, dlb16_ref, sdo, sdl):
        dc = _dot_nt(dx_ref[...].astype(BF16), w_ref[...])
        ri = lax.broadcasted_iota(jnp.int32, (LANES, LANES), 0)
        ci = lax.broadcasted_iota(jnp.int32, (LANES, LANES), 1)
        same_head = ((ri // HEAD_DIM) == (ci // HEAD_DIM)).astype(BF16)
        for cb in range(C // LANES):
            cols = slice(cb * LANES, (cb + 1) * LANES)
            blk = dc[:, cols]
            prod = blk * cat_ref[:, cols].astype(F32)
            hi = prod.astype(BF16)
            lo_ = (prod - hi.astype(F32)).astype(BF16)
            dl = _dot(hi, same_head) + _dot(lo_, same_head)
            if cb < nba:
                doa_ref[:, cols] = blk.astype(BF16)
                dla_ref[:, cols] = dl
            else:
                bcols = slice((cb - nba) * LANES, (cb - nba + 1) * LANES)
                dob1_ref[:, bcols] = blk.astype(BF16)
                dlb1_ref[:, bcols] = dl
                sdo[cb - nba] = blk
                sdl[cb - nba] = dl
        _deinterleave(sdo, dob4_ref, 4, tm, nbb)
        _deinterleave(sdl, dlb4_ref, 4, tm, nbb)
        _deinterleave(sdo, dob16_ref, 16, tm, nbb)
        _deinterleave(sdl, dlb16_ref, 16, tm, nbb)

    row = lambda w: pl.BlockSpec((tm, w), lambda i: (i, 0))
    perm = lambda d: pl.BlockSpec((d, tm // d, B_W), lambda i: (0, i, 0))
    return pl.pallas_call(
        body, name="dcat", grid=(T // tm,), in_specs=[row(D), pl.BlockSpec(w_out.shape, lambda i: (0, 0)), row(C)],
        out_specs=[row(A_Q_W), row(A_Q_W), row(B_W), row(B_W), perm(4), perm(4), perm(16), perm(16)],
        out_shape=[_sds((T, A_Q_W), BF16), _sds((T, A_Q_W), F32), _sds((T, B_W), BF16), _sds((T, B_W), F32),
                   _sds((4, T // 4, B_W), BF16), _sds((4, T // 4, B_W), F32), _sds((16, T // 16, B_W), BF16), _sds((16, T // 16, B_W), F32)],
        scratch_shapes=[pltpu.VMEM((nbb, tm, LANES), F32)] * 2, compiler_params=_params(("parallel",)))(dx, w_out, cat)


def _rope_bwd_assemble(dqa, dka, dva, b1, b4, b16, cos, sin, tm=512):
    T = dqa.shape[0]
    nbb = B_W // LANES
    width = A_Q_W + 2 * A_KV_W + 3 * B_W

    def body(dqa_ref, dka_ref, dva_ref, q1, k1, v1, q4, k4, v4, q16, k16, v16, c_ref, s_ref, o_ref, scr):
        cs, sn = c_ref[...], s_ref[...]

        def unrope(t):
            return t * cs + _swap32(t * sn)

        col = 0
        for ref, rope in ((dqa_ref, True), (dka_ref, True), (dva_ref, False)):
            for cb in range(ref.shape[1] // LANES):
                t = ref[:, cb * LANES:(cb + 1) * LANES].astype(F32)
                o_ref[:, col:col + LANES] = (unrope(t) if rope else t).astype(BF16)
                col += LANES
        for which, (r1, r4, r16, rope) in enumerate(((q1, q4, q16, True), (k1, k4, k16, True), (v1, v4, v16, False))):
            _interleave(r4, scr.at[0], 4, tm, nbb)
            _interleave(r16, scr.at[1], 16, tm, nbb)
            for cb in range(nbb):
                t = r1[:, cb * LANES:(cb + 1) * LANES].astype(F32) + scr[0, cb] + scr[1, cb]
                o_ref[:, col:col + LANES] = (unrope(t) if rope else t).astype(BF16)
                col += LANES

    row = lambda w: pl.BlockSpec((tm, w), lambda i: (i, 0))
    perm = lambda d: pl.BlockSpec((d, tm // d, B_W), lambda i: (0, i, 0))
    return pl.pallas_call(
        body, name="rope_bwd", grid=(T // tm,),
        in_specs=[row(A_Q_W), row(A_KV_W), row(A_KV_W)] + [row(B_W)] * 3 + [perm(4)] * 3 + [perm(16)] * 3 + [row(LANES), row(LANES)],
        out_specs=row(width), out_shape=_sds((T, width), BF16), scratch_shapes=[pltpu.VMEM((2, nbb, tm, LANES), F32)],
        compiler_params=_params(("parallel",)))(dqa, dka, dva, *b1, *b4, *b16, cos, sin)


def _dh_norm(dproj, w_in, x, g, dres, tm=512):
    T, D = x.shape

    def body(dp_ref, w_ref, x_ref, g_ref, dr_ref, dx_ref, dg_ref):
        @pl.when(pl.program_id(0) == 0)
        def _():
            dg_ref[...] = jnp.zeros_like(dg_ref)

        dxn, dg = _rms_bwd(_dot(dp_ref[...], w_ref[...]), x_ref[...], g_ref[...])
        dg_ref[...] += dg
        dx_ref[...] = dr_ref[...] + dxn

    row = lambda w: pl.BlockSpec((tm, w), lambda i: (i, 0))
    return pl.pallas_call(
        body, name="dh_norm", grid=(T // tm,),
        in_specs=[row(dproj.shape[1]), pl.BlockSpec(w_in.shape, lambda i: (0, 0)), row(D), pl.BlockSpec((1, D), lambda i: (0, 0)), row(D)],
        out_specs=[row(D), pl.BlockSpec((SUBLANES, D), lambda i: (0, 0))],
        out_shape=[_sds((T, D), F32), _sds((SUBLANES, D), F32)], compiler_params=_params(("arbitrary",)))(dproj, w_in, x, g, dres)


def _grad_push_plan(n):
    def plan(refs):
        x, y, c = _mesh_pos()
        return [(refs[k].at[chip], refs[n + k].at[rel], dev) for k in range(n) for rel, (dev, chip) in enumerate(_chip_peers(x, y, c))]
    return plan


def _sum_own(me_arr, g, landed, name):
    ns, R, C = g.shape
    tr = R // 2 if (R // 2) % 16 == 0 else R

    def body(me_ref, g_ref, x_ref, o_ref):
        acc = g_ref[...]
        for rel in range(ns - 1):
            acc = acc + x_ref[rel].astype(F32)
        o_ref[...] = acc

    grid_spec = pltpu.PrefetchScalarGridSpec(
        num_scalar_prefetch=1, grid=(R // tr,),
        in_specs=[pl.BlockSpec((None, tr, C), lambda t, me: (me[0], t, 0)), pl.BlockSpec((ns - 1, tr, C), lambda t, me: (0, t, 0))],
        out_specs=pl.BlockSpec((tr, C), lambda t, me: (t, 0)))
    return pl.pallas_call(body, name=name, grid_spec=grid_spec, out_shape=_sds((R, C), F32),
                          compiler_params=_params(("parallel",)))(me_arr, g, landed)


def _pair_swap(ps, name):
    n = len(ps)

    def body(*refs):
        ins, outs = refs[:n], refs[n:2 * n]
        send, recv = refs[2 * n:]
        x, y, c = _mesh_pos()
        cps = []
        for k in range(n):
            cp = pltpu.make_async_remote_copy(src_ref=ins[k], dst_ref=outs[k], send_sem=send.at[k], recv_sem=recv.at[k],
                                              device_id=(x, y, 1 - c), device_id_type=MESH)
            cp.start()
            cps.append(cp)
        for cp in cps:
            cp.wait()

    return pl.pallas_call(
        body, name=name, out_shape=[_sds(p.shape, F32) for p in ps], in_specs=[ANY] * n, out_specs=[ANY] * n,
        scratch_shapes=[pltpu.SemaphoreType.DMA((n,)), pltpu.SemaphoreType.DMA((n,))], compiler_params=_params())(*ps)


def _allreduce_small(v):
    rows, W = v.shape

    def body(v_ref, o_ref, buf, send, recv):
        x, y, c = _mesh_pos()
        me = 4 * x + 2 * y + c
        cps = []
        for m in range(1, N_DEV):
            dev = (x ^ (m >> 2), y ^ ((m >> 1) & 1), c ^ (m & 1))
            cp = pltpu.make_async_remote_copy(src_ref=v_ref, dst_ref=buf.at[me], send_sem=send.at[m - 1], recv_sem=recv.at[m - 1],
                                              device_id=dev, device_id_type=MESH)
            cp.start()
            cps.append(cp)
        for m in range(1, N_DEV):
            pltpu.make_async_remote_copy(src_ref=v_ref, dst_ref=buf.at[me ^ m], send_sem=send.at[m - 1], recv_sem=recv.at[m - 1],
                                         device_id=(x, y, c), device_id_type=MESH).wait_recv()
        for cp in cps:
            cp.wait_send()
        buf[me] = v_ref[...]
        acc = buf[0]
        for i in range(1, N_DEV):
            acc = acc + buf[i]
        o_ref[...] = acc

    return pl.pallas_call(
        body, name="allreduce_small", out_shape=_sds((rows, W), F32),
        scratch_shapes=[pltpu.VMEM((N_DEV, rows, W), F32), pltpu.SemaphoreType.DMA((N_DEV - 1,)), pltpu.SemaphoreType.DMA((N_DEV - 1,))],
        compiler_params=_params())(v)


def _adamw(w, gp, gq, m, v, name):
    R, C = w.shape
    tr = R // 2 if (R // 2) % SUBLANES == 0 else R
    c1 = 1.0 / (1.0 - ADAM_B1 ** ADAM_STEP)
    c2 = 1.0 / (1.0 - ADAM_B2 ** ADAM_STEP)

    def body(w_ref, gp_ref, gq_ref, m_ref, v_ref, g_ref, d_ref, nm_ref, nv_ref):
        gv = gp_ref[...] + gq_ref[...]
        nm = ADAM_B1 * m_ref[...] + (1.0 - ADAM_B1) * gv
        nv = ADAM_B2 * v_ref[...] + (1.0 - ADAM_B2) * (gv * gv)
        g_ref[...] = gv
        d_ref[...] = -ADAM_LR * ((nm * c1) / (jnp.sqrt(nv * c2) + ADAM_EPS) + ADAM_WD * w_ref[...])
        nm_ref[...] = nm
        nv_ref[...] = nv

    blk = pl.BlockSpec((tr, C), lambda t: (t, 0))
    return pl.pallas_call(body, name=name, grid=(R // tr,), in_specs=[blk] * 5, out_specs=[blk] * 4,
                          out_shape=[_sds((R, C), F32)] * 4, compiler_params=_params(("parallel",)))(w, gp, gq, m, v)


def _local_step(x, positions, target, norms, a_sink, comm):
    T, D = x.shape
    g1, gm, g2, gf = norms
    inv_freq = 1.0 / (ROPE_THETA ** (jnp.arange(0, HEAD_DIM, 2, dtype=F32) / HEAD_DIM))
    inv_freq = jnp.tile(inv_freq, LANES // (HEAD_DIM // 2)).reshape(1, LANES)
    cos, sin = _rope_tables(positions.reshape(T, 1), inv_freq)
    no_sink = jnp.zeros((2 * (B_W // LANES),), F32)
    W = {k: comm.weight(k, x) for k in ("wg1", "wu1", "wd1")}

    x1, h1, gate1, up1, act1 = _ffn_fwd(x, comm.order(g1), W["wg1"], W["wu1"], W["wd1"], "ffn1_fwd")
    W["w_in"] = comm.weight("w_in", x1)
    (h2, aq, ak, av, bq1, bk1, bv1, bq4, bk4, bv4, bq16, bk16, bv16) = _proj_rope(x1, gm, W["w_in"], cos, sin)
    a_out, a_lse = _attn_fwd(aq[None], ak[None], av[None], a_sink, A_HALF_WINDOW, True, BF16, "attn_a_fwd")
    bqs = {1: (bq1[None], bk1[None], bv1[None]), 4: (bq4, bk4, bv4), 16: (bq16, bk16, bv16)}
    b_o, b_l = {}, {}
    for w, d in B_PATTERNS:
        q_, k_, v_ = bqs[d]
        b_o[d], b_l[d] = _attn_fwd(q_, k_, v_, no_sink, w // (2 * d), False, F32, f"attn_b{d}_fwd")
    cat, lg1, lg4, lg16 = _merge_b(a_out[0], b_o[1][0], b_l[1][0], b_o[4], b_l[4], b_o[16], b_l[16])
    W["w_out"] = comm.weight("w_out", cat)
    x2 = _out_proj(x1, cat, W["w_out"])
    for k in ("wg2", "wu2", "wd2"):
        W[k] = comm.weight(k, x2)
    x3, h3, gate2, up2, act2 = _ffn_fwd(x2, g2, W["wg2"], W["wu2"], W["wd2"], "ffn2_fwd")

    dx3, dgf, loss8 = _final_loss(x3, gf, target)
    dx2, dff2, dgate2, dup2, dg2 = _ffn_dx(dx3, x2, g2, gate2, up2, W["wg2"], W["wu2"], W["wd2"], "ffn2_dx")
    fb = gate2.shape[1] // 2
    dwg2 = _tn(dgate2, h3, fb, "ffn2_dw_gate")
    dwu2 = _tn(dup2, h3, fb, "ffn2_dw_up")
    dwd2 = _tn(act2, dff2, fb, "ffn2_dw_down")
    comm.ready(dict(wg2=dwg2, wu2=dwu2, wd2=dwd2), dwd2[0])

    doa, dla, dob1, dlb1, dob4, dlb4, dob16, dlb16 = _dcat(dx2, W["w_out"], cat)
    dw_out = _tn(cat, dx2, cat.shape[1], "w_out_dw", dep=comm.dep())
    dqa, dka, dva, dsk = _attn_bwd(aq[None], ak[None], av[None], doa[None], a_lse, dla[None], comm.order(a_sink), A_HALF_WINDOW, True,
                                   "attn_a_bwd")
    bwd_in = {1: (dob1[None], lg1[None], dlb1[None]), 4: (dob4, lg4, dlb4), 16: (dob16, lg16, dlb16)}
    bg = {}
    for w, d in B_PATTERNS:
        q_, k_, v_ = bqs[d]
        do_, l_, dl_ = bwd_in[d]
        bg[d] = _attn_bwd(q_, k_, v_, do_, l_, dl_, no_sink, w // (2 * d), False, f"attn_b{d}_bwd")[:3]
    dproj = _rope_bwd_assemble(dqa[0], dka[0], dva[0], [t[0] for t in bg[1]], bg[4], bg[16], cos, sin)
    dw_in = _tn(dproj, h2, dproj.shape[1] // 2, "w_in_dw")
    comm.ready(dict(w_in=dw_in, w_out=dw_out), dw_in[0])
    dx1, dgm = _dh_norm(dproj, W["w_in"], x1, comm.order(gm), dx2)

    dx0, dff1, dgate1, dup1, dg1 = _ffn_dx(dx1, x, g1, gate1, up1, W["wg1"], W["wu1"], W["wd1"], "ffn1_dx")
    dwd1 = _tn(act1, dff1, fb, "ffn1_dw_down")
    comm.ready(dict(wd1=dwd1), dwd1[0])
    dwg1 = _tn(dgate1, h1, fb, "ffn1_dw_gate", dep=comm.dep())
    comm.ready(dict(wg1=dwg1), dwg1[0])
    dwu1 = _tn(dup1, h1, fb, "ffn1_dw_up", dep=comm.dep())
    comm.ready(dict(wu1=dwu1), dwu1[0])

    dsink = dsk[0, :, :, ::HEAD_DIM].sum(axis=1).reshape(-1)
    small = dict(g1=dg1.sum(axis=0), gm=dgm.sum(axis=0), g2=dg2.sum(axis=0), gf=dgf.sum(axis=0), sink=dsink, loss=loss8[0, 0])
    return dx0, small


BIG = ("wg1", "wu1", "wd1", "w_in", "w_out", "wg2", "wu2", "wd2")
GATHER_GROUPS = (("w_in",), ("w_out",), ("wg2", "wu2", "wd2"))


class _Comm:
    def __init__(self, shards):
        x, y, c = _mesh_pos()
        self.me = (2 * x + y).astype(jnp.int32).reshape(1)
        self.shards = shards
        self.tokens = []
        self.waiting = {}
        self.groups = []
        fulls = {k: _cast_place(self.me, shards[k], f"cast_{k}") for k in BIG}
        first = ("wg1", "wu1", "wd1")
        self.full = dict(zip(first, _gather_weights([fulls[k] for k in first])))
        dep = self.full["wd1"]
        for gi, names in enumerate(GATHER_GROUPS):
            plan = _gather_plan(len(names))
            send, recv, bufs, tok = _push_start(f"gather_start_{gi}", [fulls[k] for k in names], 3 * len(names), plan, dep)
            self.tokens.append(tok)
            dep = tok
            for k in names:
                self.waiting[k] = (gi, names, send, recv, bufs, plan)

    def order(self, a):
        for tok in self.tokens:
            a = a + tok[0, 0]
        self.tokens = []
        return a

    def dep(self):
        return self.tokens[-1] if self.tokens else None

    def weight(self, name, after):
        if name in self.waiting:
            gi, names, send, recv, bufs, plan = self.waiting[name]
            for k, buf in zip(names, _push_wait(f"gather_wait_{gi}", send, recv, bufs, plan, after)):
                self.full[k] = buf
                del self.waiting[k]
        full = self.full[name]
        return full.reshape(N_CHIPS * full.shape[1], full.shape[2])

    def ready(self, grads, after):
        names = list(grads)
        f32s, b16s = [], []
        for k in names:
            gf, gb = grads[k]
            f32s.append(gf.reshape((N_CHIPS,) + self.shards[k].shape))
            b16s.append(gb.reshape((N_CHIPS,) + self.shards[k].shape))
        n = len(names)
        lands = [lax.empty((N_CHIPS - 1,) + self.shards[k].shape, BF16) for k in names]
        plan = _grad_push_plan(n)
        gi = len(self.groups)
        send, recv, bufs, tok = _push_start(f"grad_start_{gi}", b16s + lands, 3 * n, plan, after)
        self.tokens.append(tok)
        self.groups.append((names, f32s, send, recv, bufs, plan))

    def finish(self):
        out = {}
        after = self.tokens[-1]
        for gi, (names, f32s, send, recv, bufs, plan) in enumerate(self.groups):
            n = len(names)
            bufs = _push_wait(f"grad_wait_{gi}", send, recv, bufs, plan, after)
            mine = [_sum_own(self.me, f32s[i], bufs[n + i], f"sum_{k}") for i, k in enumerate(names)]
            theirs = _pair_swap(mine, f"grad_pair_swap_{gi}")
            for k, p, q in zip(names, mine, theirs):
                out[k] = (p, q)
            after = theirs[-1]
        return out


def kernel(x, positions, norm_ffn1, w_gate1, w_up1, w_down1, norm_mix, w_in, a_sink, w_out, norm_ffn2, w_gate2, w_up2, w_down2, norm_final, loss_target, m_norm_ffn1, m_w_gate1, m_w_up1, m_w_down1, m_norm_mix, m_w_in, m_a_sink, m_w_out, m_norm_ffn2, m_w_gate2, m_w_up2, m_w_down2, m_norm_final, v_norm_ffn1, v_w_gate1, v_w_up1, v_w_down1, v_norm_mix, v_w_in, v_a_sink, v_w_out, v_norm_ffn2, v_w_gate2, v_w_up2, v_w_down2, v_norm_final):
    T, D = x.shape[1], x.shape[2]
    flip = ("wg1", "wu1", "w_in", "wg2", "wu2")

    def rows(k, a):
        return a[0].T if k in flip else a[0]

    given = dict(wg1=(w_gate1, m_w_gate1, v_w_gate1), wu1=(w_up1, m_w_up1, v_w_up1), wd1=(w_down1, m_w_down1, v_w_down1),
                 w_in=(w_in, m_w_in, v_w_in), w_out=(w_out, m_w_out, v_w_out), wg2=(w_gate2, m_w_gate2, v_w_gate2),
                 wu2=(w_up2, m_w_up2, v_w_up2), wd2=(w_down2, m_w_down2, v_w_down2))
    shards = {k: rows(k, given[k][0]) for k in BIG}

    comm = _Comm(shards)

    norms = (norm_ffn1, norm_mix, norm_ffn2, norm_final.reshape(1, D))
    grad_x, small = _local_step(x[0], positions[0], loss_target[0], norms, a_sink[0], comm)

    partial = comm.finish()

    def pad_row(a):
        a = a.reshape(-1)
        return jnp.pad(a, (0, D - a.shape[0]))

    row4 = pad_row(jnp.concatenate([small["sink"], small["loss"].reshape(1)]))
    vec = jnp.stack([small["g1"], small["gm"], small["g2"], small["gf"], row4] + [jnp.zeros((D,), F32)] * 3, axis=0)
    red = _allreduce_small(vec)
    loss = red[4, 8]
    g_small = jnp.stack([red[0], red[1], red[2], red[3], pad_row(red[4, 0:8])] + [jnp.zeros((D,), F32)] * 3, axis=0)

    def small_stack(a1, am, a2, af, ask):
        return jnp.stack([pad_row(a1), pad_row(am), pad_row(a2), pad_row(af), pad_row(ask)] + [jnp.zeros((D,), F32)] * 3, axis=0)

    w_small = small_stack(norm_ffn1, norm_mix, norm_ffn2, norm_final, a_sink)
    m_small = small_stack(m_norm_ffn1, m_norm_mix, m_norm_ffn2, m_norm_final, m_a_sink)
    v_small = small_stack(v_norm_ffn1, v_norm_mix, v_norm_ffn2, v_norm_final, v_a_sink)
    live = small_stack(jnp.ones_like(norm_ffn1), jnp.ones_like(norm_mix), jnp.ones_like(norm_ffn2), jnp.ones_like(norm_final), jnp.ones_like(a_sink))
    v_small = jnp.where(live > 0, v_small, 1.0)

    upd = {}
    for k in BIG:
        outs = _adamw(shards[k], partial[k][0], partial[k][1], rows(k, given[k][1]), rows(k, given[k][2]), f"adamw_{k}")
        upd[k] = tuple((a.T if k in flip else a)[None] for a in outs)
    _, ds_, nms_, nvs_ = _adamw(w_small, g_small, jnp.zeros_like(g_small), m_small, v_small, "adamw_small")

    def small_out(arr):
        return [arr[0].reshape(1, D), arr[1].reshape(1, D), arr[2].reshape(1, D), arr[3], arr[4, 0:8].reshape(1, 8)]

    gs_, dss, nmss, nvss = small_out(g_small), small_out(ds_), small_out(nms_), small_out(nvs_)

    def ordered(i):
        sm = (gs_, dss, nmss, nvss)[i]
        return [sm[0], upd["wg1"][i], upd["wu1"][i], upd["wd1"][i], sm[1], upd["w_in"][i], sm[4], upd["w_out"][i], sm[2],
                upd["wg2"][i], upd["wu2"][i], upd["wd2"][i], sm[3]]

    return (loss, grad_x[None], *ordered(0), *ordered(1), *ordered(2), *ordered(3))
```

```python
import jax
import jax.numpy as jnp
from jax import lax
from jax.experimental import pallas as pl
from jax.experimental.pallas import tpu as pltpu

F32 = jnp.float32
BF16 = jnp.bfloat16

HEAD_DIM = 64
LANES = 128
SUBLANES = 8
A_Q_W, A_KV_W, B_W = 512, 128, 512
A_HALF_WINDOW = 128
B_PATTERNS = ((128, 1), (512, 4), (2048, 16))
ROPE_THETA = 10000.0
NORM_EPS = 1e-6
FFN_RES_WEIGHT = 0.5
ADAM_LR, ADAM_B1, ADAM_B2, ADAM_EPS, ADAM_WD, ADAM_STEP = 0.001, 0.9, 0.999, 1e-08, 0.01, 10
N_CHIPS = 4
N_DEV = 8
QB = 128
NEG = -1e30
VMEM_LIMIT = 56 * 1024 * 1024
MESH = pl.DeviceIdType.MESH
ANY = pl.BlockSpec(memory_space=pl.ANY)


def _params(sem=None):
    return pltpu.CompilerParams(dimension_semantics=sem, vmem_limit_bytes=VMEM_LIMIT)


def _sds(shape, dtype):
    return jax.ShapeDtypeStruct(tuple(shape), dtype)


def _dot(a, b):
    return jnp.dot(a, b, preferred_element_type=F32)


def _dot_nt(a, b):
    return lax.dot_general(a, b, (((1,), (1,)), ((), ())), preferred_element_type=F32)


def _dot_tn(a, b):
    return lax.dot_general(a, b, (((0,), (0,)), ((), ())), preferred_element_type=F32)


def _rms_stats(x):
    r = lax.rsqrt(jnp.mean(x * x, axis=-1, keepdims=True) + NORM_EPS)
    return x * r, r


def _rms_bwd(dh, x, g):
    xhat, r = _rms_stats(x)
    dxn = dh * g
    dx = r * (dxn - xhat * jnp.mean(dxn * xhat, axis=-1, keepdims=True))
    tm, d = x.shape
    dg = (dh * xhat).reshape(tm // SUBLANES, SUBLANES, d).sum(axis=0)
    return dx, dg


def _sigmoid(x):
    return 1.0 / (1.0 + jnp.exp(-x))


def _swap32(t):
    n = t.shape[-1]
    lane = lax.broadcasted_iota(jnp.int32, t.shape, t.ndim - 1)
    return jnp.where((lane % HEAD_DIM) < HEAD_DIM // 2, pltpu.roll(t, n - HEAD_DIM // 2, axis=t.ndim - 1),
                     pltpu.roll(t, HEAD_DIM // 2, axis=t.ndim - 1))


def _cast_place(me_arr, w, name):
    R, C = w.shape
    tr = R // 2 if (R // 2) % 16 == 0 else R

    def body(me_ref, w_ref, o_ref):
        o_ref[...] = w_ref[...].astype(BF16)

    grid_spec = pltpu.PrefetchScalarGridSpec(
        num_scalar_prefetch=1, grid=(R // tr,), in_specs=[pl.BlockSpec((tr, C), lambda t, me: (t, 0))],
        out_specs=pl.BlockSpec((None, tr, C), lambda t, me: (me[0], t, 0)))
    return pl.pallas_call(body, name=name, grid_spec=grid_spec, out_shape=_sds((N_CHIPS, R, C), BF16),
                          compiler_params=_params(("parallel",)))(me_arr, w)


HBM = pl.BlockSpec(memory_space=pltpu.HBM)
SEM = pl.BlockSpec(memory_space=pltpu.SEMAPHORE)


def _push_start(name, bufs, ncopies, plan, after):
    nb = len(bufs)

    def body(*refs):
        send, recv, token = refs[nb + 1], refs[nb + 2], refs[-1]
        for i, (src, dst, dev) in enumerate(plan(refs[:nb])):
            pltpu.make_async_remote_copy(src_ref=src, dst_ref=dst, send_sem=send.at[i], recv_sem=recv.at[i],
                                         device_id=dev, device_id_type=MESH).start()
        token[...] = jnp.zeros_like(token)

    outs = pl.pallas_call(
        body, name=name,
        out_shape=(pltpu.SemaphoreType.DMA((ncopies,)), pltpu.SemaphoreType.DMA((ncopies,)), *[pltpu.HBM(b.shape, b.dtype) for b in bufs],
                   _sds((SUBLANES, LANES), F32)),
        in_specs=[HBM] * nb + [ANY], out_specs=(SEM, SEM, *([HBM] * nb), pl.BlockSpec(memory_space=pltpu.VMEM)),
        input_output_aliases={i: 2 + i for i in range(nb)},
        compiler_params=pltpu.CompilerParams(has_side_effects=pltpu.SideEffectType.DATAFLOW_SIDE_EFFECTING),
    )(*[pltpu.with_memory_space_constraint(b, pltpu.HBM) for b in bufs], after)
    return outs[0], outs[1], list(outs[2:2 + nb]), outs[-1]


def _push_wait(name, send, recv, bufs, plan, after):
    nb = len(bufs)

    def body(*refs):
        send_ref, recv_ref = refs[nb], refs[nb + 1]
        for i, (src, dst, dev) in enumerate(plan(refs[:nb])):
            cp = pltpu.make_async_remote_copy(src_ref=src, dst_ref=dst, send_sem=send_ref.at[i], recv_sem=recv_ref.at[i],
                                              device_id=dev, device_id_type=MESH)
            cp.wait_send()
            cp.wait_recv()

    outs = pl.pallas_call(
        body, name=name, out_shape=tuple(pltpu.HBM(b.shape, b.dtype) for b in bufs),
        in_specs=[HBM] * nb + [SEM, SEM, ANY], out_specs=tuple([HBM] * nb), input_output_aliases={i: i for i in range(nb)},
        compiler_params=pltpu.CompilerParams(has_side_effects=pltpu.SideEffectType.DATAFLOW_SIDE_EFFECTING),
    )(*bufs, send, recv, after)
    return list(outs)


def _mesh_pos():
    return lax.axis_index("x"), lax.axis_index("y"), lax.axis_index("c")


def _chip_peers(x, y, c):
    return [((1 - x, y, c), 2 * (1 - x) + y), ((x, 1 - y, c), 2 * x + (1 - y)), ((1 - x, 1 - y, c), 2 * (1 - x) + (1 - y))]


def _gather_plan(n):
    def plan(refs):
        x, y, c = _mesh_pos()
        me = 2 * x + y
        return [(refs[k].at[me], refs[k].at[me], dev) for k in range(n) for dev, _ in _chip_peers(x, y, c)]
    return plan


def _gather_weights(fulls):
    n = len(fulls)

    def body(*refs):
        ins, outs = refs[:n], refs[n:2 * n]
        ici_send, ici_recv, d2d_send, d2d_recv = refs[2 * n:]
        x, y, c = _mesh_pos()
        me = 2 * x + y
        sibling = (x, y, 1 - c)
        peers = _chip_peers(x, y, c)

        def half(k, who):
            r2 = fulls[k].shape[1] // 2
            return pl.ds(pl.multiple_of(who * r2, 16), r2)

        first = []
        for k in range(n):
            for rel, (dev, _) in enumerate(peers):
                cp = pltpu.make_async_remote_copy(src_ref=ins[k].at[me, half(k, c), :], dst_ref=outs[k].at[me, half(k, c), :],
                                                  send_sem=ici_send.at[k * 3 + rel], recv_sem=ici_recv.at[k * 3 + rel],
                                                  device_id=dev, device_id_type=MESH)
                cp.start()
                first.append(cp)
        passed = []
        for k in range(n):
            for rel, (dev, chip) in enumerate(peers):
                blk = outs[k].at[chip, half(k, c), :]
                pltpu.make_async_remote_copy(src_ref=blk, dst_ref=blk, send_sem=ici_send.at[k * 3 + rel], recv_sem=ici_recv.at[k * 3 + rel],
                                             device_id=dev, device_id_type=MESH).wait_recv()
                cp = pltpu.make_async_remote_copy(src_ref=blk, dst_ref=blk, send_sem=d2d_send.at[k * 3 + rel], recv_sem=d2d_recv.at[k * 3 + rel],
                                                  device_id=sibling, device_id_type=MESH)
                cp.start()
                passed.append(cp)
        for k in range(n):
            for rel, (dev, chip) in enumerate(peers):
                blk = outs[k].at[chip, half(k, 1 - c), :]
                pltpu.make_async_remote_copy(src_ref=blk, dst_ref=blk, send_sem=d2d_send.at[k * 3 + rel], recv_sem=d2d_recv.at[k * 3 + rel],
                                             device_id=sibling, device_id_type=MESH).wait_recv()
        for cp in first + passed:
            cp.wait_send()

    return pl.pallas_call(
        body, name="gather_weights", out_shape=[_sds(f.shape, BF16) for f in fulls],
        in_specs=[ANY] * n, out_specs=[ANY] * n, input_output_aliases={k: k for k in range(n)},
        scratch_shapes=[pltpu.SemaphoreType.DMA((n * 3,))] * 4, compiler_params=_params())(*fulls)


def _resident(shape):
    return pl.BlockSpec(shape, lambda i: (0,) * len(shape), pipeline_mode=pl.Buffered(1))


FFN_CHUNK = 768


def _chunks(n, step):
    return [(c0, min(step, n - c0)) for c0 in range(0, n, step)]


def _ffn_fwd(x, g, wgt, wut, wd, name, tm=256):
    T, D = x.shape
    F = wd.shape[0]

    def body(x_ref, g_ref, wg_ref, wu_ref, wd_ref, xo_ref, h_ref, gate_ref, up_ref, act_ref, h_scr):
        xv = x_ref[...]
        xhat, _ = _rms_stats(xv)
        hb = (xhat * g_ref[...]).astype(BF16)
        h_scr[...] = hb
        h_ref[...] = hb
        acc = None
        for c0, cw in _chunks(F, FFN_CHUNK):
            h = h_scr[...]
            gate = _dot_nt(h, wg_ref[c0:c0 + cw, :])
            up = _dot_nt(h, wu_ref[c0:c0 + cw, :])
            act = ((gate * _sigmoid(gate)) * up).astype(BF16)
            gate_ref[:, c0:c0 + cw] = gate.astype(BF16)
            up_ref[:, c0:c0 + cw] = up.astype(BF16)
            act_ref[:, c0:c0 + cw] = act
            d = _dot(act, wd_ref[c0:c0 + cw, :])
            acc = d if acc is None else acc + d
        xo_ref[...] = xv + FFN_RES_WEIGHT * acc

    row = pl.BlockSpec((tm, D), lambda i: (i, 0))
    saved = pl.BlockSpec((tm, F), lambda i: (i, 0))
    return pl.pallas_call(
        body, name=name, grid=(T // tm,),
        in_specs=[row, pl.BlockSpec((1, D), lambda i: (0, 0)), _resident(wgt.shape), _resident(wut.shape), _resident(wd.shape)],
        out_specs=[row, row, saved, saved, saved],
        out_shape=[_sds((T, D), F32), _sds((T, D), BF16), _sds((T, F), BF16), _sds((T, F), BF16), _sds((T, F), BF16)],
        scratch_shapes=[pltpu.VMEM((tm, D), BF16)], compiler_params=_params(("parallel",)))(x, g, wgt, wut, wd)


def _ffn_dx(dxo, x, g, gate_s, up_s, wgt, wut, wd, name, tm=256):
    T, D = x.shape
    F = wd.shape[0]

    def body(dxo_ref, x_ref, g_ref, gate_ref, up_ref, wg_ref, wu_ref, wd_ref, dx_ref, dff_ref, dgate_ref, dup_ref, dg_ref, dff_scr):
        @pl.when(pl.program_id(0) == 0)
        def _():
            dg_ref[...] = jnp.zeros_like(dg_ref)

        d = (FFN_RES_WEIGHT * dxo_ref[...]).astype(BF16)
        dff_scr[...] = d
        dff_ref[...] = d
        dh = None
        for c0, cw in _chunks(F, FFN_CHUNK):
            da = _dot_nt(dff_scr[...], wd_ref[c0:c0 + cw, :])
            gate = gate_ref[:, c0:c0 + cw].astype(F32)
            up = up_ref[:, c0:c0 + cw].astype(F32)
            s = _sigmoid(gate)
            silu = gate * s
            dup = (da * silu).astype(BF16)
            dgate = (da * up * (s * (1.0 + gate * (1.0 - s)))).astype(BF16)
            dgate_ref[:, c0:c0 + cw] = dgate
            dup_ref[:, c0:c0 + cw] = dup
            t = _dot(dgate, wg_ref[c0:c0 + cw, :]) + _dot(dup, wu_ref[c0:c0 + cw, :])
            dh = t if dh is None else dh + t
        dxn, dg = _rms_bwd(dh, x_ref[...], g_ref[...])
        dg_ref[...] += dg
        dx_ref[...] = dxo_ref[...] + dxn

    row = pl.BlockSpec((tm, D), lambda i: (i, 0))
    saved = pl.BlockSpec((tm, F), lambda i: (i, 0))
    return pl.pallas_call(
        body, name=name, grid=(T // tm,),
        in_specs=[row, row, pl.BlockSpec((1, D), lambda i: (0, 0)), saved, saved, _resident(wgt.shape), _resident(wut.shape),
                  _resident(wd.shape)],
        out_specs=[row, row, saved, saved, pl.BlockSpec((SUBLANES, D), lambda i: (0, 0))],
        out_shape=[_sds((T, D), F32), _sds((T, D), BF16), _sds((T, F), BF16), _sds((T, F), BF16), _sds((SUBLANES, D), F32)],
        scratch_shapes=[pltpu.VMEM((tm, D), BF16)], compiler_params=_params(("arbitrary",)))(dxo, x, g, gate_s, up_s, wgt, wut, wd)


def _tn(a, b, mb, name, tk=2048, dep=None):
    T, M = a.shape
    N = b.shape[1]
    nt = T // tk

    def body(a_ref, b_ref, *refs):
        o_ref, ob_ref = refs[-2:]

        @pl.when(pl.program_id(1) == 0)
        def _():
            o_ref[...] = jnp.zeros_like(o_ref)

        o_ref[...] += _dot_tn(a_ref[...].astype(BF16), b_ref[...].astype(BF16))

        @pl.when(pl.program_id(1) == nt - 1)
        def _():
            ob_ref[...] = o_ref[...].astype(BF16)

    o_spec = pl.BlockSpec((mb, N), lambda g, t: (g, 0))
    return pl.pallas_call(
        body, name=name, grid=(M // mb, nt),
        in_specs=[pl.BlockSpec((tk, mb), lambda g, t: (t, g)), pl.BlockSpec((tk, N), lambda g, t: (t, 0))] + ([ANY] if dep is not None else []),
        out_specs=[o_spec, o_spec], out_shape=[_sds((M, N), F32), _sds((M, N), BF16)],
        compiler_params=_params(("parallel", "arbitrary")))(a, b, *([dep] if dep is not None else []))


def _rope_tables(pos_col, inv_freq):
    T = pos_col.shape[0]

    def body(p_ref, f_ref, c_ref, s_ref):
        ang = p_ref[...].astype(F32) * f_ref[...]
        lane = lax.broadcasted_iota(jnp.int32, ang.shape, 1)
        c_ref[...] = jnp.cos(ang)
        sn = jnp.sin(ang)
        s_ref[...] = jnp.where((lane % HEAD_DIM) < HEAD_DIM // 2, -sn, sn)

    tm = 1024
    return pl.pallas_call(
        body, name="rope_tables", grid=(T // tm,),
        in_specs=[pl.BlockSpec((tm, 1), lambda i: (i, 0)), pl.BlockSpec((1, LANES), lambda i: (0, 0))],
        out_specs=[pl.BlockSpec((tm, LANES), lambda i: (i, 0))] * 2,
        out_shape=[_sds((T, LANES), F32)] * 2, compiler_params=_params(("parallel",)))(pos_col, inv_freq)


def _deinterleave(scr, out_ref, d, tm, nblk):
    for r in range(d):
        for cb in range(nblk):
            out_ref[r, :, cb * LANES:(cb + 1) * LANES] = scr[cb, pl.ds(r, tm // d, stride=d), :].astype(out_ref.dtype)


def _interleave(in_ref, scr, d, tm, nblk):
    for r in range(d):
        for cb in range(nblk):
            scr[cb, pl.ds(r, tm // d, stride=d), :] = in_ref[r, :, cb * LANES:(cb + 1) * LANES].astype(F32)


def _proj_rope(x, g, w_in, cos, sin, tm=512):
    T, D = x.shape
    dils = [d for _, d in B_PATTERNS if d > 1]
    nbb = B_W // LANES
    scale = HEAD_DIM ** -0.5
    cuts = [0, A_Q_W, A_Q_W + A_KV_W, A_Q_W + 2 * A_KV_W, A_Q_W + 2 * A_KV_W + B_W, A_Q_W + 2 * A_KV_W + 2 * B_W,
            A_Q_W + 2 * A_KV_W + 3 * B_W]

    def body(x_ref, g_ref, w_ref, c_ref, s_ref, h_ref, aq_ref, ak_ref, av_ref, *rest):
        b_refs, scr = rest[:-1], rest[-1]
        xhat, _ = _rms_stats(x_ref[...])
        h = (xhat * g_ref[...]).astype(BF16)
        h_ref[...] = h
        cs, sn = c_ref[...], s_ref[...]

        def seg(idx, rope, mult):
            lo, hi = cuts[idx], cuts[idx + 1]
            blocks = []
            whole = _dot_nt(h, w_ref[lo:hi, :])
            for cb in range((hi - lo) // LANES):
                p = whole[:, cb * LANES:(cb + 1) * LANES]
                if rope:
                    p = p * cs + _swap32(p) * sn
                if mult != 1.0:
                    p = p * mult
                blocks.append(p)
            return blocks

        for idx, ref, rope, mult in ((0, aq_ref, True, scale), (1, ak_ref, True, 1.0), (2, av_ref, False, 1.0)):
            for cb, p in enumerate(seg(idx, rope, mult)):
                ref[:, cb * LANES:(cb + 1) * LANES] = p.astype(BF16)
        for which, (idx, rope, mult) in enumerate(((3, True, scale), (4, True, 1.0), (5, False, 1.0))):
            for cb, p in enumerate(seg(idx, rope, mult)):
                b_refs[which][:, cb * LANES:(cb + 1) * LANES] = p.astype(BF16)
                scr[cb] = p
            for di, d in enumerate(dils):
                _deinterleave(scr, b_refs[3 * (di + 1) + which], d, tm, nbb)

    row = lambda w: pl.BlockSpec((tm, w), lambda i: (i, 0))
    out_specs = [row(D), row(A_Q_W), row(A_KV_W), row(A_KV_W)] + [row(B_W)] * 3
    out_shape = [_sds((T, D), BF16), _sds((T, A_Q_W), BF16), _sds((T, A_KV_W), BF16), _sds((T, A_KV_W), BF16)] + [_sds((T, B_W), BF16)] * 3
    for d in dils:
        out_specs += [pl.BlockSpec((d, tm // d, B_W), lambda i: (0, i, 0))] * 3
        out_shape += [_sds((d, T // d, B_W), BF16)] * 3
    return pl.pallas_call(
        body, name="proj_rope", grid=(T // tm,),
        in_specs=[row(D), pl.BlockSpec((1, D), lambda i: (0, 0)), pl.BlockSpec(w_in.shape, lambda i: (0, 0)), row(LANES), row(LANES)],
        out_specs=out_specs, out_shape=out_shape, scratch_shapes=[pltpu.VMEM((nbb, tm, LANES), F32)],
        compiler_params=_params(("parallel",)))(x, g, w_in, cos, sin)


def _band_bias(rel, kw, hw):
    ri = lax.broadcasted_iota(jnp.int32, (2 * QB, kw), 0) & (QB - 1)
    ci = lax.broadcasted_iota(jnp.int32, (2 * QB, kw), 1)
    return jnp.where(jnp.abs(ri + rel - ci) <= hw, 0.0, NEG).astype(F32)


def _stack_heads(x, lo):
    z = jnp.zeros_like(x)
    return jnp.concatenate([jnp.where(lo, x, z), jnp.where(lo, z, x)], axis=0)


def _unstack_heads(y, lo):
    return jnp.where(lo, y[:QB], y[QB:])


def _band_setup(bias_scr, kw, hw):
    if bias_scr is not None:
        for i in range(3):
            bias_scr[i] = _band_bias(i * hw, kw, hw)


def _band_window(bias_scr, qs, L, kw, hw):
    ws = pl.multiple_of(jnp.clip(qs - hw, 0, L - kw), 64)
    if bias_scr is None:
        return ws, _band_bias(qs - ws, kw, hw)
    return ws, bias_scr[lax.shift_right_logical(qs - ws, hw.bit_length() - 1)]


def _dup_kv_head(src_ref, dst_ref, head, L):
    step = min(L, 1024)
    for r0 in range(0, L, step):
        xf = src_ref[r0:r0 + step, :].astype(F32)
        lane = lax.broadcasted_iota(jnp.int32, xf.shape, 1)
        keep = jnp.logical_xor(lane < HEAD_DIM, head == 1)
        dst_ref[r0:r0 + step, :] = jnp.where(keep, xf, pltpu.roll(xf, HEAD_DIM, axis=1)).astype(dst_ref.dtype)


def _attn_fwd(q, k, v, sink, hw, gqa, out_dtype, name, blocks_per_step=8):
    NB, L, Cq = q.shape
    Ls = min(L, 2048)
    kw = min(QB + 2 * hw, L)
    tables = L >= QB + 2 * hw
    unroll = min(blocks_per_step, Ls // QB)

    def body(sink_ref, q_ref, k_ref, v_ref, o_ref, lse_ref, *scr):
        b, s_idx = pl.program_id(1), pl.program_id(2)
        bias_scr = scr[0] if tables else None
        _band_setup(bias_scr, kw, hw)
        if gqa:
            kd, vd = scr[-2:]

            @pl.when(s_idx == 0)
            def _():
                _dup_kv_head(k_ref, kd, b // 2, L)
                _dup_kv_head(v_ref, vd, b // 2, L)
        else:
            kd, vd = k_ref, v_ref
        lane = lax.broadcasted_iota(jnp.int32, (QB, LANES), 1)
        lo = lane < HEAD_DIM
        if gqa:
            row = lax.broadcasted_iota(jnp.int32, (2 * QB, 1), 0)
            sk = jnp.where(row < QB, sink_ref[2 * b], sink_ref[2 * b + 1])

        def block(ql):
            qs = s_idx * Ls + ql
            ws, bias = _band_window(bias_scr, qs, L, kw, hw)
            kv_, vv = kd[pl.ds(ws, kw), :], vd[pl.ds(ws, kw), :]
            s = _dot_nt(_stack_heads(q_ref[pl.ds(ql, QB), :], lo), kv_) + bias
            m = jnp.max(s, axis=-1, keepdims=True)
            if gqa:
                m = jnp.maximum(m, sk)
            p = jnp.exp(s - m)
            den = jnp.sum(p, axis=-1, keepdims=True)
            if gqa:
                den = den + jnp.exp(sk - m)
            o_ref[pl.ds(ql, QB), :] = _unstack_heads(_dot(p.astype(BF16), vv) * (1.0 / den), lo).astype(o_ref.dtype)
            lse_ref[pl.ds(ql, QB), :] = _unstack_heads(m + jnp.log(den), lo)

        def step(n, carry):
            for u in range(unroll):
                block(pl.multiple_of((n * unroll + u) * QB, QB))
            return carry

        lax.fori_loop(0, Ls // (QB * unroll), step, 0)

    kv_map = (lambda r, b, s: (r, 0, 0)) if gqa else (lambda r, b, s: (r, 0, b))
    seg = pl.BlockSpec((None, Ls, LANES), lambda r, b, s: (r, s, b))
    return pl.pallas_call(
        body, name=name, grid=(NB, Cq // LANES, L // Ls),
        in_specs=[pl.BlockSpec(memory_space=pltpu.SMEM), seg, pl.BlockSpec((None, L, LANES), kv_map), pl.BlockSpec((None, L, LANES), kv_map)],
        out_specs=[seg, seg], out_shape=[_sds((NB, L, Cq), out_dtype), _sds((NB, L, Cq), F32)],
        scratch_shapes=([pltpu.VMEM((3, 2 * QB, kw), F32)] if tables else []) + ([pltpu.VMEM((L, LANES), BF16)] * 2 if gqa else []),
        compiler_params=_params(("parallel", "parallel", "arbitrary")))(sink, q, k, v)


def _attn_bwd(q, k, v, do, lse, delta, sink, hw, gqa, name, blocks_per_step=8):
    NB, L, Cq = q.shape
    Ck = k.shape[2]
    Ls = min(L, 2048)
    kw = min(QB + 2 * hw, L)
    reps = kw // LANES
    nseg = L // Ls
    scale = HEAD_DIM ** -0.5
    tables = L >= QB + 2 * hw
    unroll = min(blocks_per_step, Ls // QB)

    def body(sink_ref, q_ref, do_ref, lse_ref, dl_ref, k_ref, v_ref, dq_ref, dk_ref, dv_ref, dsk_ref, *scr):
        b, s_idx = pl.program_id(1), pl.program_id(2)
        lane = lax.broadcasted_iota(jnp.int32, (QB, LANES), 1)
        lo = lane < HEAD_DIM
        bias_scr = scr[0] if tables else None
        _band_setup(bias_scr, kw, hw)
        if gqa:
            kd, vd, dk_acc, dv_acc, dsk_acc = scr[-5:]

            @pl.when(s_idx == 0)
            def _():
                _dup_kv_head(k_ref, kd, b // 2, L)
                _dup_kv_head(v_ref, vd, b // 2, L)
                dk_acc[...] = jnp.zeros_like(dk_acc)
                dv_acc[...] = jnp.zeros_like(dv_acc)
                dsk_acc[...] = jnp.zeros_like(dsk_acc)

            @pl.when((s_idx == 0) & (b == 0))
            def _():
                dk_ref[...] = jnp.zeros_like(dk_ref)
                dv_ref[...] = jnp.zeros_like(dv_ref)
        else:
            kd, vd, dk_acc, dv_acc = k_ref, v_ref, dk_ref, dv_ref

            @pl.when(s_idx == 0)
            def _():
                dk_ref[...] = jnp.zeros_like(dk_ref)
                dv_ref[...] = jnp.zeros_like(dv_ref)

        def block(ql):
            qs = s_idx * Ls + ql
            ws, bias = _band_window(bias_scr, qs, L, kw, hw)
            qv, dov = q_ref[pl.ds(ql, QB), :], do_ref[pl.ds(ql, QB), :]
            lse, dl = lse_ref[pl.ds(ql, QB), :], dl_ref[pl.ds(ql, QB), :]
            kv_, vv = kd[pl.ds(ws, kw), :], vd[pl.ds(ws, kw), :]
            q2, do2 = _stack_heads(qv, lo), _stack_heads(dov, lo)
            lse_sw, dl_sw = pltpu.roll(lse, HEAD_DIM, axis=1), pltpu.roll(dl, HEAD_DIM, axis=1)
            lse2 = jnp.concatenate([jnp.where(lo, lse, lse_sw), jnp.where(lo, lse_sw, lse)], axis=0)
            dl2 = jnp.concatenate([jnp.where(lo, dl, dl_sw), jnp.where(lo, dl_sw, dl)], axis=0)
            p = jnp.exp(_dot_nt(q2, kv_) + bias - jnp.tile(lse2, (1, reps)))
            ds = (p * (_dot_nt(do2, vv) - jnp.tile(dl2, (1, reps)))).astype(BF16)
            dq_ref[pl.ds(ql, QB), :] = (_unstack_heads(_dot(ds, kv_), lo) * scale).astype(dq_ref.dtype)
            dk_acc[pl.ds(ws, kw), :] += _dot_tn(ds, q2)
            dv_acc[pl.ds(ws, kw), :] += _dot_tn(p.astype(BF16), do2)
            if gqa:
                sk = jnp.where(lo, sink_ref[2 * b], sink_ref[2 * b + 1])
                dsk_acc[...] += -jnp.exp(sk - lse) * dl

        def step(n, carry):
            for u in range(unroll):
                block(pl.multiple_of((n * unroll + u) * QB, QB))
            return carry

        lax.fori_loop(0, Ls // (QB * unroll), step, 0)

        if gqa:
            @pl.when(s_idx == nseg - 1)
            def _():
                step_rows = min(L, 1024)
                for r0 in range(0, L, step_rows):
                    lanek = lax.broadcasted_iota(jnp.int32, (step_rows, LANES), 1)
                    mine = jnp.logical_xor(lanek < HEAD_DIM, (b // 2) == 1)
                    for acc, ref in ((dk_acc, dk_ref), (dv_acc, dv_ref)):
                        a = acc[r0:r0 + step_rows, :]
                        ref[r0:r0 + step_rows, :] += jnp.where(mine, a + pltpu.roll(a, HEAD_DIM, axis=1), 0.0)
                dsk_ref[...] = dsk_acc[...].reshape(QB // SUBLANES, SUBLANES, LANES).sum(axis=0)
        else:
            dsk_ref[...] = jnp.zeros_like(dsk_ref)

    kv_map = (lambda r, b, s: (r, 0, 0)) if gqa else (lambda r, b, s: (r, 0, b))
    seg = pl.BlockSpec((None, Ls, LANES), lambda r, b, s: (r, s, b))
    full = pl.BlockSpec((None, L, LANES), kv_map)
    scratch = [pltpu.VMEM((3, 2 * QB, kw), F32)] if tables else []
    if gqa:
        scratch += [pltpu.VMEM((L, LANES), BF16)] * 2 + [pltpu.VMEM((L, LANES), F32)] * 2 + [pltpu.VMEM((QB, LANES), F32)]
    return pl.pallas_call(
        body, name=name, grid=(NB, Cq // LANES, nseg),
        in_specs=[pl.BlockSpec(memory_space=pltpu.SMEM), seg, seg, seg, seg, full, full],
        out_specs=[seg, full, full, pl.BlockSpec((None, None, SUBLANES, LANES), lambda r, b, s: (r, b, 0, 0))],
        out_shape=[_sds((NB, L, Cq), BF16), _sds((NB, L, Ck), F32), _sds((NB, L, Ck), F32),
                   _sds((NB, Cq // LANES, SUBLANES, LANES), F32)],
        scratch_shapes=scratch,
        compiler_params=_params(("arbitrary", "arbitrary", "arbitrary")))(sink, q, do, lse, delta, k, v)


def _merge_b(a_out, o1, l1, o4, l4, o16, l16, tm=512):
    T = a_out.shape[0]
    nbb = B_W // LANES

    def body(a_ref, o1_ref, l1_ref, o4_ref, l4_ref, o16_ref, l16_ref, cat_ref, lg1_ref, lg4_ref, lg16_ref, so, sl, slg):
        _interleave(o4_ref, so.at[0], 4, tm, nbb)
        _interleave(l4_ref, sl.at[0], 4, tm, nbb)
        _interleave(o16_ref, so.at[1], 16, tm, nbb)
        _interleave(l16_ref, sl.at[1], 16, tm, nbb)
        cat_ref[:, 0:A_Q_W] = a_ref[...]
        for cb in range(nbb):
            cols = slice(cb * LANES, (cb + 1) * LANES)
            os_ = (o1_ref[:, cols], so[0, cb], so[1, cb])
            ls_ = (l1_ref[:, cols], sl[0, cb], sl[1, cb])
            m = jnp.maximum(jnp.maximum(ls_[0], ls_[1]), ls_[2])
            es = [jnp.exp(l - m) for l in ls_]
            den = es[0] + es[1] + es[2]
            out = (es[0] * os_[0] + es[1] * os_[1] + es[2] * os_[2]) * (1.0 / den)
            lg = m + jnp.log(den)
            cat_ref[:, A_Q_W + cb * LANES:A_Q_W + (cb + 1) * LANES] = out.astype(BF16)
            lg1_ref[:, cols] = lg
            slg[cb] = lg
        _deinterleave(slg, lg4_ref, 4, tm, nbb)
        _deinterleave(slg, lg16_ref, 16, tm, nbb)

    row = lambda w: pl.BlockSpec((tm, w), lambda i: (i, 0))
    perm = lambda d: pl.BlockSpec((d, tm // d, B_W), lambda i: (0, i, 0))
    return pl.pallas_call(
        body, name="merge_patterns", grid=(T // tm,),
        in_specs=[row(A_Q_W), row(B_W), row(B_W), perm(4), perm(4), perm(16), perm(16)],
        out_specs=[row(A_Q_W + B_W), row(B_W), perm(4), perm(16)],
        out_shape=[_sds((T, A_Q_W + B_W), BF16), _sds((T, B_W), F32), _sds((4, T // 4, B_W), F32), _sds((16, T // 16, B_W), F32)],
        scratch_shapes=[pltpu.VMEM((2, nbb, tm, LANES), F32), pltpu.VMEM((2, nbb, tm, LANES), F32), pltpu.VMEM((nbb, tm, LANES), F32)],
        compiler_params=_params(("parallel",)))(a_out, o1, l1, o4, l4, o16, l16)


def _out_proj(x, cat, w_out, tm=512):
    T, D = x.shape

    def body(x_ref, c_ref, w_ref, o_ref):
        o_ref[...] = x_ref[...] + _dot(c_ref[...], w_ref[...])

    row = lambda w: pl.BlockSpec((tm, w), lambda i: (i, 0))
    return pl.pallas_call(
        body, name="out_proj", grid=(T // tm,), in_specs=[row(D), row(cat.shape[1]), pl.BlockSpec(w_out.shape, lambda i: (0, 0))],
        out_specs=row(D), out_shape=_sds((T, D), F32), compiler_params=_params(("parallel",)))(x, cat, w_out)


def _final_loss(x, g, target, tm=512):
    T, D = x.shape

    def body(x_ref, g_ref, t_ref, dx_ref, dg_ref, loss_ref):
        @pl.when(pl.program_id(0) == 0)
        def _():
            dg_ref[...] = jnp.zeros_like(dg_ref)
            loss_ref[...] = jnp.zeros_like(loss_ref)

        xv, gv = x_ref[...], g_ref[...]
        xhat, _ = _rms_stats(xv)
        err = xhat * gv - t_ref[...]
        loss_ref[...] += 0.5 * jnp.sum(jnp.sum(err * err, axis=-1, keepdims=True) * (1.0 / D), axis=0, keepdims=True)
        dx, dg = _rms_bwd(err * (1.0 / D), xv, gv)
        dx_ref[...] = dx
        dg_ref[...] += dg

    row = pl.BlockSpec((tm, D), lambda i: (i, 0))
    return pl.pallas_call(
        body, name="final_loss", grid=(T // tm,), in_specs=[row, pl.BlockSpec((1, D), lambda i: (0, 0)), row],
        out_specs=[row, pl.BlockSpec((SUBLANES, D), lambda i: (0, 0)), pl.BlockSpec((SUBLANES, LANES), lambda i: (0, 0))],
        out_shape=[_sds((T, D), F32), _sds((SUBLANES, D), F32), _sds((SUBLANES, LANES), F32)],
        compiler_params=_params(("arbitrary",)))(x, g, target)


def _dcat(dx, w_out, cat, tm=512):
    T, D = dx.shape
    C = cat.shape[1]
    nba, nbb = A_Q_W // LANES, B_W // LANES

    def body(dx_ref, w_ref, cat_ref, doa_ref, dla_ref, dob1_ref, dlb1_ref, dob4_ref, dlb4_ref, dob16_ref, dlb16_ref, sdo, sdl):
        dc = _dot_nt(dx_ref[...].astype(BF16), w_ref[...])
        ri = lax.broadcasted_iota(jnp.int32, (LANES, LANES), 0)
        ci = lax.broadcasted_iota(jnp.int32, (LANES, LANES), 1)
        same_head = ((ri // HEAD_DIM) == (ci // HEAD_DIM)).astype(BF16)
        for cb in range(C // LANES):
            cols = slice(cb * LANES, (cb + 1) * LANES)
            blk = dc[:, cols]
            prod = blk * cat_ref[:, cols].astype(F32)
            hi = prod.astype(BF16)
            lo_ = (prod - hi.astype(F32)).astype(BF16)
            dl = _dot(hi, same_head) + _dot(lo_, same_head)
            if cb < nba:
                doa_ref[:, cols] = blk.astype(BF16)
                dla_ref[:, cols] = dl
            else:
                bcols = slice((cb - nba) * LANES, (cb - nba + 1) * LANES)
                dob1_ref[:, bcols] = blk.astype(BF16)
                dlb1_ref[:, bcols] = dl
                sdo[cb - nba] = blk
                sdl[cb - nba] = dl
        _deinterleave(sdo, dob4_ref, 4, tm, nbb)
        _deinterleave(sdl, dlb4_ref, 4, tm, nbb)
        _deinterleave(sdo, dob16_ref, 16, tm, nbb)
        _deinterleave(sdl, dlb16_ref, 16, tm, nbb)

    row = lambda w: pl.BlockSpec((tm, w), lambda i: (i, 0))
    perm = lambda d: pl.BlockSpec((d, tm // d, B_W), lambda i: (0, i, 0))
    return pl.pallas_call(
        body, name="dcat", grid=(T // tm,), in_specs=[row(D), pl.BlockSpec(w_out.shape, lambda i: (0, 0)), row(C)],
        out_specs=[row(A_Q_W), row(A_Q_W), row(B_W), row(B_W), perm(4), perm(4), perm(16), perm(16)],
        out_shape=[_sds((T, A_Q_W), BF16), _sds((T, A_Q_W), F32), _sds((T, B_W), BF16), _sds((T, B_W), F32),
                   _sds((4, T // 4, B_W), BF16), _sds((4, T // 4, B_W), F32), _sds((16, T // 16, B_W), BF16), _sds((16, T // 16, B_W), F32)],
        scratch_shapes=[pltpu.VMEM((nbb, tm, LANES), F32)] * 2, compiler_params=_params(("parallel",)))(dx, w_out, cat)


def _rope_bwd_assemble(dqa, dka, dva, b1, b4, b16, cos, sin, tm=512):
    T = dqa.shape[0]
    nbb = B_W // LANES
    width = A_Q_W + 2 * A_KV_W + 3 * B_W

    def body(dqa_ref, dka_ref, dva_ref, q1, k1, v1, q4, k4, v4, q16, k16, v16, c_ref, s_ref, o_ref, scr):
        cs, sn = c_ref[...], s_ref[...]

        def unrope(t):
            return t * cs + _swap32(t * sn)

        col = 0
        for ref, rope in ((dqa_ref, True), (dka_ref, True), (dva_ref, False)):
            for cb in range(ref.shape[1] // LANES):
                t = ref[:, cb * LANES:(cb + 1) * LANES].astype(F32)
                o_ref[:, col:col + LANES] = (unrope(t) if rope else t).astype(BF16)
                col += LANES
        for which, (r1, r4, r16, rope) in enumerate(((q1, q4, q16, True), (k1, k4, k16, True), (v1, v4, v16, False))):
            _interleave(r4, scr.at[0], 4, tm, nbb)
            _interleave(r16, scr.at[1], 16, tm, nbb)
            for cb in range(nbb):
                t = r1[:, cb * LANES:(cb + 1) * LANES].astype(F32) + scr[0, cb] + scr[1, cb]
                o_ref[:, col:col + LANES] = (unrope(t) if rope else t).astype(BF16)
                col += LANES

    row = lambda w: pl.BlockSpec((tm, w), lambda i: (i, 0))
    perm = lambda d: pl.BlockSpec((d, tm // d, B_W), lambda i: (0, i, 0))
    return pl.pallas_call(
        body, name="rope_bwd", grid=(T // tm,),
        in_specs=[row(A_Q_W), row(A_KV_W), row(A_KV_W)] + [row(B_W)] * 3 + [perm(4)] * 3 + [perm(16)] * 3 + [row(LANES), row(LANES)],
        out_specs=row(width), out_shape=_sds((T, width), BF16), scratch_shapes=[pltpu.VMEM((2, nbb, tm, LANES), F32)],
        compiler_params=_params(("parallel",)))(dqa, dka, dva, *b1, *b4, *b16, cos, sin)


def _dh_norm(dproj, w_in, x, g, dres, tm=512):
    T, D = x.shape

    def body(dp_ref, w_ref, x_ref, g_ref, dr_ref, dx_ref, dg_ref):
        @pl.when(pl.program_id(0) == 0)
        def _():
            dg_ref[...] = jnp.zeros_like(dg_ref)

        dxn, dg = _rms_bwd(_dot(dp_ref[...], w_ref[...]), x_ref[...], g_ref[...])
        dg_ref[...] += dg
        dx_ref[...] = dr_ref[...] + dxn

    row = lambda w: pl.BlockSpec((tm, w), lambda i: (i, 0))
    return pl.pallas_call(
        body, name="dh_norm", grid=(T // tm,),
        in_specs=[row(dproj.shape[1]), pl.BlockSpec(w_in.shape, lambda i: (0, 0)), row(D), pl.BlockSpec((1, D), lambda i: (0, 0)), row(D)],
        out_specs=[row(D), pl.BlockSpec((SUBLANES, D), lambda i: (0, 0))],
        out_shape=[_sds((T, D), F32), _sds((SUBLANES, D), F32)], compiler_params=_params(("arbitrary",)))(dproj, w_in, x, g, dres)


def _grad_push_plan(n):
    def plan(refs):
        x, y, c = _mesh_pos()
        return [(refs[k].at[chip], refs[n + k].at[rel], dev) for k in range(n) for rel, (dev, chip) in enumerate(_chip_peers(x, y, c))]
    return plan


def _sum_own(me_arr, g, landed, name):
    ns, R, C = g.shape
    tr = R // 2 if (R // 2) % 16 == 0 else R

    def body(me_ref, g_ref, x_ref, o_ref):
        acc = g_ref[...]
        for rel in range(ns - 1):
            acc = acc + x_ref[rel].astype(F32)
        o_ref[...] = acc

    grid_spec = pltpu.PrefetchScalarGridSpec(
        num_scalar_prefetch=1, grid=(R // tr,),
        in_specs=[pl.BlockSpec((None, tr, C), lambda t, me: (me[0], t, 0)), pl.BlockSpec((ns - 1, tr, C), lambda t, me: (0, t, 0))],
        out_specs=pl.BlockSpec((tr, C), lambda t, me: (t, 0)))
    return pl.pallas_call(body, name=name, grid_spec=grid_spec, out_shape=_sds((R, C), F32),
                          compiler_params=_params(("parallel",)))(me_arr, g, landed)


def _swap_plan(n):
    def plan(refs):
        x, y, c = _mesh_pos()
        return [(refs[k], refs[n + k], (x, y, 1 - c)) for k in range(n)]
    return plan


def _allreduce_small(v):
    rows, W = v.shape

    def body(v_ref, o_ref, buf, send, recv):
        x, y, c = _mesh_pos()
        me = 4 * x + 2 * y + c
        cps = []
        for m in range(1, N_DEV):
            dev = (x ^ (m >> 2), y ^ ((m >> 1) & 1), c ^ (m & 1))
            cp = pltpu.make_async_remote_copy(src_ref=v_ref, dst_ref=buf.at[me], send_sem=send.at[m - 1], recv_sem=recv.at[m - 1],
                                              device_id=dev, device_id_type=MESH)
            cp.start()
            cps.append(cp)
        for m in range(1, N_DEV):
            pltpu.make_async_remote_copy(src_ref=v_ref, dst_ref=buf.at[me ^ m], send_sem=send.at[m - 1], recv_sem=recv.at[m - 1],
                                         device_id=(x, y, c), device_id_type=MESH).wait_recv()
        for cp in cps:
            cp.wait_send()
        buf[me] = v_ref[...]
        acc = buf[0]
        for i in range(1, N_DEV):
            acc = acc + buf[i]
        o_ref[...] = acc

    return pl.pallas_call(
        body, name="allreduce_small", out_shape=_sds((rows, W), F32),
        scratch_shapes=[pltpu.VMEM((N_DEV, rows, W), F32), pltpu.SemaphoreType.DMA((N_DEV - 1,)), pltpu.SemaphoreType.DMA((N_DEV - 1,))],
        compiler_params=_params())(v)


def _adamw(w, gp, gq, m, v, name):
    R, C = w.shape
    tr = R // 2 if (R // 2) % SUBLANES == 0 else R
    c1 = 1.0 / (1.0 - ADAM_B1 ** ADAM_STEP)
    c2 = 1.0 / (1.0 - ADAM_B2 ** ADAM_STEP)

    def body(w_ref, gp_ref, gq_ref, m_ref, v_ref, g_ref, d_ref, nm_ref, nv_ref):
        gv = gp_ref[...] + gq_ref[...]
        nm = ADAM_B1 * m_ref[...] + (1.0 - ADAM_B1) * gv
        nv = ADAM_B2 * v_ref[...] + (1.0 - ADAM_B2) * (gv * gv)
        g_ref[...] = gv
        d_ref[...] = -ADAM_LR * ((nm * c1) / (jnp.sqrt(nv * c2) + ADAM_EPS) + ADAM_WD * w_ref[...])
        nm_ref[...] = nm
        nv_ref[...] = nv

    blk = pl.BlockSpec((tr, C), lambda t: (t, 0))
    return pl.pallas_call(body, name=name, grid=(R // tr,), in_specs=[blk] * 5, out_specs=[blk] * 4,
                          out_shape=[_sds((R, C), F32)] * 4, compiler_params=_params(("parallel",)))(w, gp, gq, m, v)


def _local_step(x, positions, target, norms, a_sink, comm):
    T, D = x.shape
    g1, gm, g2, gf = norms
    inv_freq = 1.0 / (ROPE_THETA ** (jnp.arange(0, HEAD_DIM, 2, dtype=F32) / HEAD_DIM))
    inv_freq = jnp.tile(inv_freq, LANES // (HEAD_DIM // 2)).reshape(1, LANES)
    cos, sin = _rope_tables(positions.reshape(T, 1), inv_freq)
    no_sink = jnp.zeros((2 * (B_W // LANES),), F32)
    W = {k: comm.weight(k, x) for k in ("wg1", "wu1", "wd1")}

    x1, h1, gate1, up1, act1 = _ffn_fwd(x, comm.order(g1), W["wg1"], W["wu1"], W["wd1"], "ffn1_fwd")
    W["w_in"] = comm.weight("w_in", x1)
    (h2, aq, ak, av, bq1, bk1, bv1, bq4, bk4, bv4, bq16, bk16, bv16) = _proj_rope(x1, gm, W["w_in"], cos, sin)
    a_out, a_lse = _attn_fwd(aq[None], ak[None], av[None], a_sink, A_HALF_WINDOW, True, BF16, "attn_a_fwd")
    bqs = {1: (bq1[None], bk1[None], bv1[None]), 4: (bq4, bk4, bv4), 16: (bq16, bk16, bv16)}
    b_o, b_l = {}, {}
    for w, d in B_PATTERNS:
        q_, k_, v_ = bqs[d]
        b_o[d], b_l[d] = _attn_fwd(q_, k_, v_, no_sink, w // (2 * d), False, F32, f"attn_b{d}_fwd")
    cat, lg1, lg4, lg16 = _merge_b(a_out[0], b_o[1][0], b_l[1][0], b_o[4], b_l[4], b_o[16], b_l[16])
    W["w_out"] = comm.weight("w_out", cat)
    x2 = _out_proj(x1, cat, W["w_out"])
    for k in ("wg2", "wu2", "wd2"):
        W[k] = comm.weight(k, x2)
    x3, h3, gate2, up2, act2 = _ffn_fwd(x2, g2, W["wg2"], W["wu2"], W["wd2"], "ffn2_fwd")

    dx3, dgf, loss8 = _final_loss(x3, gf, target)
    dx2, dff2, dgate2, dup2, dg2 = _ffn_dx(dx3, x2, g2, gate2, up2, W["wg2"], W["wu2"], W["wd2"], "ffn2_dx")
    fb = gate2.shape[1] // 2
    dwg2 = _tn(dgate2, h3, fb, "ffn2_dw_gate")
    dwu2 = _tn(dup2, h3, fb, "ffn2_dw_up")
    dwd2 = _tn(act2, dff2, fb, "ffn2_dw_down")
    comm.ready(dict(wg2=dwg2, wu2=dwu2, wd2=dwd2), dwd2[0])

    doa, dla, dob1, dlb1, dob4, dlb4, dob16, dlb16 = _dcat(dx2, W["w_out"], cat)
    dw_out = _tn(cat, dx2, cat.shape[1], "w_out_dw", dep=comm.dep())
    dqa, dka, dva, dsk = _attn_bwd(aq[None], ak[None], av[None], doa[None], a_lse, dla[None], comm.order(a_sink), A_HALF_WINDOW, True,
                                   "attn_a_bwd")
    bwd_in = {1: (dob1[None], lg1[None], dlb1[None]), 4: (dob4, lg4, dlb4), 16: (dob16, lg16, dlb16)}
    bg = {}
    for w, d in B_PATTERNS:
        q_, k_, v_ = bqs[d]
        do_, l_, dl_ = bwd_in[d]
        bg[d] = _attn_bwd(q_, k_, v_, do_, l_, dl_, no_sink, w // (2 * d), False, f"attn_b{d}_bwd")[:3]
    dproj = _rope_bwd_assemble(dqa[0], dka[0], dva[0], [t[0] for t in bg[1]], bg[4], bg[16], cos, sin)
    dw_in = _tn(dproj, h2, dproj.shape[1] // 2, "w_in_dw")
    comm.ready(dict(w_in=dw_in, w_out=dw_out), dw_in[0])
    dx1, dgm = _dh_norm(dproj, W["w_in"], x1, comm.order(gm), dx2)

    dx0, dff1, dgate1, dup1, dg1 = _ffn_dx(dx1, x, g1, gate1, up1, W["wg1"], W["wu1"], W["wd1"], "ffn1_dx")
    dwd1 = _tn(act1, dff1, fb, "ffn1_dw_down")
    comm.ready(dict(wd1=dwd1), dwd1[0])
    dwg1 = _tn(dgate1, h1, fb, "ffn1_dw_gate", dep=comm.dep())
    comm.ready(dict(wg1=dwg1), dwg1[0])
    dwu1 = _tn(dup1, h1, fb, "ffn1_dw_up", dep=comm.dep())
    comm.ready(dict(wu1=dwu1), dwu1[0])

    dsink = dsk[0, :, :, ::HEAD_DIM].sum(axis=1).reshape(-1)
    small = dict(g1=dg1.sum(axis=0), gm=dgm.sum(axis=0), g2=dg2.sum(axis=0), gf=dgf.sum(axis=0), sink=dsink, loss=loss8[0, 0])
    return dx0, small


BIG = ("wg1", "wu1", "wd1", "w_in", "w_out", "wg2", "wu2", "wd2")
GATHER_GROUPS = (("w_in",), ("w_out",), ("wg2", "wu2", "wd2"))


class _Comm:
    def __init__(self, shards):
        x, y, c = _mesh_pos()
        self.me = (2 * x + y).astype(jnp.int32).reshape(1)
        self.shards = shards
        self.tokens = []
        self.waiting = {}
        self.groups = []
        fulls = {k: _cast_place(self.me, shards[k], f"cast_{k}") for k in BIG}
        first = ("wg1", "wu1", "wd1")
        self.full = dict(zip(first, _gather_weights([fulls[k] for k in first])))
        dep = self.full["wd1"]
        for gi, names in enumerate(GATHER_GROUPS):
            plan = _gather_plan(len(names))
            send, recv, bufs, tok = _push_start(f"gather_start_{gi}", [fulls[k] for k in names], 3 * len(names), plan, dep)
            self.tokens.append(tok)
            dep = tok
            for k in names:
                self.waiting[k] = (gi, names, send, recv, bufs, plan)

    def order(self, a):
        for tok in self.tokens:
            a = a + tok[0, 0]
        self.tokens = []
        return a

    def dep(self):
        return self.tokens[-1] if self.tokens else None

    def weight(self, name, after):
        if name in self.waiting:
            gi, names, send, recv, bufs, plan = self.waiting[name]
            for k, buf in zip(names, _push_wait(f"gather_wait_{gi}", send, recv, bufs, plan, after)):
                self.full[k] = buf
                del self.waiting[k]
        full = self.full[name]
        return full.reshape(N_CHIPS * full.shape[1], full.shape[2])

    def ready(self, grads, after):
        names = list(grads)
        f32s, b16s = [], []
        for k in names:
            gf, gb = grads[k]
            f32s.append(gf.reshape((N_CHIPS,) + self.shards[k].shape))
            b16s.append(gb.reshape((N_CHIPS,) + self.shards[k].shape))
        n = len(names)
        lands = [lax.empty((N_CHIPS - 1,) + self.shards[k].shape, BF16) for k in names]
        plan = _grad_push_plan(n)
        gi = len(self.groups)
        send, recv, bufs, tok = _push_start(f"grad_start_{gi}", b16s + lands, 3 * n, plan, after)
        self.tokens.append(tok)
        self.groups.append((names, f32s, send, recv, bufs, plan))

    def finish(self):
        out, swaps = {}, []
        after = self.tokens[-1]
        for gi, (names, f32s, send, recv, bufs, plan) in enumerate(self.groups):
            n = len(names)
            bufs = _push_wait(f"grad_wait_{gi}", send, recv, bufs, plan, after)
            mine = [_sum_own(self.me, f32s[i], bufs[n + i], f"sum_{k}") for i, k in enumerate(names)]
            lands = [lax.empty(p.shape, F32) for p in mine]
            send2, recv2, both, after = _push_start(f"swap_start_{gi}", mine + lands, n, _swap_plan(n), mine[-1])
            swaps.append((names, send2, recv2, both))
        for gi, (names, send2, recv2, both) in enumerate(swaps):
            n = len(names)
            both = _push_wait(f"swap_wait_{gi}", send2, recv2, both, _swap_plan(n), after)
            for i, k in enumerate(names):
                out[k] = (both[i], both[n + i])
        return out


def kernel(x, positions, norm_ffn1, w_gate1, w_up1, w_down1, norm_mix, w_in, a_sink, w_out, norm_ffn2, w_gate2, w_up2, w_down2, norm_final, loss_target, m_norm_ffn1, m_w_gate1, m_w_up1, m_w_down1, m_norm_mix, m_w_in, m_a_sink, m_w_out, m_norm_ffn2, m_w_gate2, m_w_up2, m_w_down2, m_norm_final, v_norm_ffn1, v_w_gate1, v_w_up1, v_w_down1, v_norm_mix, v_w_in, v_a_sink, v_w_out, v_norm_ffn2, v_w_gate2, v_w_up2, v_w_down2, v_norm_final):
    T, D = x.shape[1], x.shape[2]
    flip = ("wg1", "wu1", "w_in", "wg2", "wu2")

    def rows(k, a):
        return a[0].T if k in flip else a[0]

    given = dict(wg1=(w_gate1, m_w_gate1, v_w_gate1), wu1=(w_up1, m_w_up1, v_w_up1), wd1=(w_down1, m_w_down1, v_w_down1),
                 w_in=(w_in, m_w_in, v_w_in), w_out=(w_out, m_w_out, v_w_out), wg2=(w_gate2, m_w_gate2, v_w_gate2),
                 wu2=(w_up2, m_w_up2, v_w_up2), wd2=(w_down2, m_w_down2, v_w_down2))
    shards = {k: rows(k, given[k][0]) for k in BIG}

    comm = _Comm(shards)

    norms = (norm_ffn1, norm_mix, norm_ffn2, norm_final.reshape(1, D))
    grad_x, small = _local_step(x[0], positions[0], loss_target[0], norms, a_sink[0], comm)

    partial = comm.finish()

    def pad_row(a):
        a = a.reshape(-1)
        return jnp.pad(a, (0, D - a.shape[0]))

    row4 = pad_row(jnp.concatenate([small["sink"], small["loss"].reshape(1)]))
    vec = jnp.stack([small["g1"], small["gm"], small["g2"], small["gf"], row4] + [jnp.zeros((D,), F32)] * 3, axis=0)
    red = _allreduce_small(vec)
    loss = red[4, 8]
    g_small = jnp.stack([red[0], red[1], red[2], red[3], pad_row(red[4, 0:8])] + [jnp.zeros((D,), F32)] * 3, axis=0)

    def small_stack(a1, am, a2, af, ask):
        return jnp.stack([pad_row(a1), pad_row(am), pad_row(a2), pad_row(af), pad_row(ask)] + [jnp.zeros((D,), F32)] * 3, axis=0)

    w_small = small_stack(norm_ffn1, norm_mix, norm_ffn2, norm_final, a_sink)
    m_small = small_stack(m_norm_ffn1, m_norm_mix, m_norm_ffn2, m_norm_final, m_a_sink)
    v_small = small_stack(v_norm_ffn1, v_norm_mix, v_norm_ffn2, v_norm_final, v_a_sink)
    live = small_stack(jnp.ones_like(norm_ffn1), jnp.ones_like(norm_mix), jnp.ones_like(norm_ffn2), jnp.ones_like(norm_final), jnp.ones_like(a_sink))
    v_small = jnp.where(live > 0, v_small, 1.0)

    upd = {}
    for k in BIG:
        outs = _adamw(shards[k], partial[k][0], partial[k][1], rows(k, given[k][1]), rows(k, given[k][2]), f"adamw_{k}")
        upd[k] = tuple((a.T if k in flip else a)[None] for a in outs)
    _, ds_, nms_, nvs_ = _adamw(w_small, g_small, jnp.zeros_like(g_small), m_small, v_small, "adamw_small")

    def small_out(arr):
        return [arr[0].reshape(1, D), arr[1].reshape(1, D), arr[2].reshape(1, D), arr[3], arr[4, 0:8].reshape(1, 8)]

    gs_, dss, nmss, nvss = small_out(g_small), small_out(ds_), small_out(nms_), small_out(nvs_)

    def ordered(i):
        sm = (gs_, dss, nmss, nvss)[i]
        return [sm[0], upd["wg1"][i], upd["wu1"][i], upd["wd1"][i], sm[1], upd["w_in"][i], sm[4], upd["w_out"][i], sm[2],
                upd["wg2"][i], upd["wu2"][i], upd["wd2"][i], sm[3]]

    return (loss, grad_x[None], *ordered(0), *ordered(1), *ordered(2), *ordered(3))
```

```python
import jax
import jax.numpy as jnp
from jax import lax
from jax.experimental import pallas as pl
from jax.experimental.pallas import tpu as pltpu

F32 = jnp.float32
BF16 = jnp.bfloat16

HEAD_DIM = 64
LANES = 128
SUBLANES = 8
A_Q_W, A_KV_W, B_W = 512, 128, 512
A_HALF_WINDOW = 128
B_PATTERNS = ((128, 1), (512, 4), (2048, 16))
ROPE_THETA = 10000.0
NORM_EPS = 1e-6
FFN_RES_WEIGHT = 0.5
ADAM_LR, ADAM_B1, ADAM_B2, ADAM_EPS, ADAM_WD, ADAM_STEP = 0.001, 0.9, 0.999, 1e-08, 0.01, 10
N_CHIPS = 4
N_DEV = 8
QB = 128
SHORT_SEQ = 512
NEG = -1e30
VMEM_LIMIT = 56 * 1024 * 1024
MESH = pl.DeviceIdType.MESH
ANY = pl.BlockSpec(memory_space=pl.ANY)


def _params(sem=None):
    return pltpu.CompilerParams(dimension_semantics=sem, vmem_limit_bytes=VMEM_LIMIT)


def _sds(shape, dtype):
    return jax.ShapeDtypeStruct(tuple(shape), dtype)


def _dot(a, b):
    return jnp.dot(a, b, preferred_element_type=F32)


def _dot_nt(a, b):
    return lax.dot_general(a, b, (((1,), (1,)), ((), ())), preferred_element_type=F32)


def _dot_tn(a, b):
    return lax.dot_general(a, b, (((0,), (0,)), ((), ())), preferred_element_type=F32)


def _rms_stats(x):
    r = lax.rsqrt(jnp.mean(x * x, axis=-1, keepdims=True) + NORM_EPS)
    return x * r, r


def _rms_bwd(dh, x, g):
    xhat, r = _rms_stats(x)
    dxn = dh * g
    dx = r * (dxn - xhat * jnp.mean(dxn * xhat, axis=-1, keepdims=True))
    tm, d = x.shape
    dg = (dh * xhat).reshape(tm // SUBLANES, SUBLANES, d).sum(axis=0)
    return dx, dg


def _sigmoid(x):
    return 1.0 / (1.0 + jnp.exp(-x))


def _swap32(t):
    n = t.shape[-1]
    lane = lax.broadcasted_iota(jnp.int32, t.shape, t.ndim - 1)
    return jnp.where((lane % HEAD_DIM) < HEAD_DIM // 2, pltpu.roll(t, n - HEAD_DIM // 2, axis=t.ndim - 1),
                     pltpu.roll(t, HEAD_DIM // 2, axis=t.ndim - 1))


def _cast_place(me_arr, w, name):
    R, C = w.shape
    tr = R // 2 if (R // 2) % 16 == 0 else R

    def body(me_ref, w_ref, o_ref):
        o_ref[...] = w_ref[...].astype(BF16)

    grid_spec = pltpu.PrefetchScalarGridSpec(
        num_scalar_prefetch=1, grid=(R // tr,), in_specs=[pl.BlockSpec((tr, C), lambda t, me: (t, 0))],
        out_specs=pl.BlockSpec((None, tr, C), lambda t, me: (me[0], t, 0)))
    return pl.pallas_call(body, name=name, grid_spec=grid_spec, out_shape=_sds((N_CHIPS, R, C), BF16),
                          compiler_params=_params(("parallel",)))(me_arr, w)


HBM = pl.BlockSpec(memory_space=pltpu.HBM)
SEM = pl.BlockSpec(memory_space=pltpu.SEMAPHORE)


def _push_start(name, bufs, ncopies, plan, after):
    nb = len(bufs)

    def body(*refs):
        send, recv, token = refs[nb + 1], refs[nb + 2], refs[-1]
        for i, (src, dst, dev) in enumerate(plan(refs[:nb])):
            pltpu.make_async_remote_copy(src_ref=src, dst_ref=dst, send_sem=send.at[i], recv_sem=recv.at[i],
                                         device_id=dev, device_id_type=MESH).start()
        token[...] = jnp.zeros_like(token)

    outs = pl.pallas_call(
        body, name=name,
        out_shape=(pltpu.SemaphoreType.DMA((ncopies,)), pltpu.SemaphoreType.DMA((ncopies,)), *[pltpu.HBM(b.shape, b.dtype) for b in bufs],
                   _sds((SUBLANES, LANES), F32)),
        in_specs=[HBM] * nb + [ANY], out_specs=(SEM, SEM, *([HBM] * nb), pl.BlockSpec(memory_space=pltpu.VMEM)),
        input_output_aliases={i: 2 + i for i in range(nb)},
        compiler_params=pltpu.CompilerParams(has_side_effects=pltpu.SideEffectType.DATAFLOW_SIDE_EFFECTING),
    )(*[pltpu.with_memory_space_constraint(b, pltpu.HBM) for b in bufs], after)
    return outs[0], outs[1], list(outs[2:2 + nb]), outs[-1]


def _push_wait(name, send, recv, bufs, plan, after):
    nb = len(bufs)

    def body(*refs):
        send_ref, recv_ref = refs[nb], refs[nb + 1]
        for i, (src, dst, dev) in enumerate(plan(refs[:nb])):
            cp = pltpu.make_async_remote_copy(src_ref=src, dst_ref=dst, send_sem=send_ref.at[i], recv_sem=recv_ref.at[i],
                                              device_id=dev, device_id_type=MESH)
            cp.wait_send()
            cp.wait_recv()

    outs = pl.pallas_call(
        body, name=name, out_shape=tuple(pltpu.HBM(b.shape, b.dtype) for b in bufs),
        in_specs=[HBM] * nb + [SEM, SEM, ANY], out_specs=tuple([HBM] * nb), input_output_aliases={i: i for i in range(nb)},
        compiler_params=pltpu.CompilerParams(has_side_effects=pltpu.SideEffectType.DATAFLOW_SIDE_EFFECTING),
    )(*bufs, send, recv, after)
    return list(outs)


def _mesh_pos():
    return lax.axis_index("x"), lax.axis_index("y"), lax.axis_index("c")


def _chip_peers(x, y, c):
    return [((1 - x, y, c), 2 * (1 - x) + y), ((x, 1 - y, c), 2 * x + (1 - y)), ((1 - x, 1 - y, c), 2 * (1 - x) + (1 - y))]


def _gather_plan(n):
    def plan(refs):
        x, y, c = _mesh_pos()
        me = 2 * x + y
        return [(refs[k].at[me], refs[k].at[me], dev) for k in range(n) for dev, _ in _chip_peers(x, y, c)]
    return plan


def _gather_weights(fulls):
    n = len(fulls)

    def body(*refs):
        ins, outs = refs[:n], refs[n:2 * n]
        ici_send, ici_recv, d2d_send, d2d_recv = refs[2 * n:]
        x, y, c = _mesh_pos()
        me = 2 * x + y
        sibling = (x, y, 1 - c)
        peers = _chip_peers(x, y, c)

        def half(k, who):
            r2 = fulls[k].shape[1] // 2
            return pl.ds(pl.multiple_of(who * r2, 16), r2)

        first = []
        for k in range(n):
            for rel, (dev, _) in enumerate(peers):
                cp = pltpu.make_async_remote_copy(src_ref=ins[k].at[me, half(k, c), :], dst_ref=outs[k].at[me, half(k, c), :],
                                                  send_sem=ici_send.at[k * 3 + rel], recv_sem=ici_recv.at[k * 3 + rel],
                                                  device_id=dev, device_id_type=MESH)
                cp.start()
                first.append(cp)
        passed = []
        for k in range(n):
            for rel, (dev, chip) in enumerate(peers):
                blk = outs[k].at[chip, half(k, c), :]
                pltpu.make_async_remote_copy(src_ref=blk, dst_ref=blk, send_sem=ici_send.at[k * 3 + rel], recv_sem=ici_recv.at[k * 3 + rel],
                                             device_id=dev, device_id_type=MESH).wait_recv()
                cp = pltpu.make_async_remote_copy(src_ref=blk, dst_ref=blk, send_sem=d2d_send.at[k * 3 + rel], recv_sem=d2d_recv.at[k * 3 + rel],
                                                  device_id=sibling, device_id_type=MESH)
                cp.start()
                passed.append(cp)
        for k in range(n):
            for rel, (dev, chip) in enumerate(peers):
                blk = outs[k].at[chip, half(k, 1 - c), :]
                pltpu.make_async_remote_copy(src_ref=blk, dst_ref=blk, send_sem=d2d_send.at[k * 3 + rel], recv_sem=d2d_recv.at[k * 3 + rel],
                                             device_id=sibling, device_id_type=MESH).wait_recv()
        for cp in first + passed:
            cp.wait_send()

    return pl.pallas_call(
        body, name="gather_weights", out_shape=[_sds(f.shape, BF16) for f in fulls],
        in_specs=[ANY] * n, out_specs=[ANY] * n, input_output_aliases={k: k for k in range(n)},
        scratch_shapes=[pltpu.SemaphoreType.DMA((n * 3,))] * 4, compiler_params=_params())(*fulls)


def _resident(shape):
    return pl.BlockSpec(shape, lambda i: (0,) * len(shape), pipeline_mode=pl.Buffered(1))


FFN_CHUNK = 768


def _chunks(n, step):
    return [(c0, min(step, n - c0)) for c0 in range(0, n, step)]


def _ffn_fwd(x, g, wgt, wut, wd, name, tm=256):
    T, D = x.shape
    F = wd.shape[0]

    def body(x_ref, g_ref, wg_ref, wu_ref, wd_ref, xo_ref, h_ref, gate_ref, up_ref, act_ref, h_scr):
        xv = x_ref[...]
        xhat, _ = _rms_stats(xv)
        hb = (xhat * g_ref[...]).astype(BF16)
        h_scr[...] = hb
        h_ref[...] = hb
        acc = None
        for c0, cw in _chunks(F, FFN_CHUNK):
            h = h_scr[...]
            gate = _dot_nt(h, wg_ref[c0:c0 + cw, :])
            up = _dot_nt(h, wu_ref[c0:c0 + cw, :])
            act = ((gate * _sigmoid(gate)) * up).astype(BF16)
            gate_ref[:, c0:c0 + cw] = gate.astype(BF16)
            up_ref[:, c0:c0 + cw] = up.astype(BF16)
            act_ref[:, c0:c0 + cw] = act
            d = _dot(act, wd_ref[c0:c0 + cw, :])
            acc = d if acc is None else acc + d
        xo_ref[...] = xv + FFN_RES_WEIGHT * acc

    row = pl.BlockSpec((tm, D), lambda i: (i, 0))
    saved = pl.BlockSpec((tm, F), lambda i: (i, 0))
    return pl.pallas_call(
        body, name=name, grid=(T // tm,),
        in_specs=[row, pl.BlockSpec((1, D), lambda i: (0, 0)), _resident(wgt.shape), _resident(wut.shape), _resident(wd.shape)],
        out_specs=[row, row, saved, saved, saved],
        out_shape=[_sds((T, D), F32), _sds((T, D), BF16), _sds((T, F), BF16), _sds((T, F), BF16), _sds((T, F), BF16)],
        scratch_shapes=[pltpu.VMEM((tm, D), BF16)], compiler_params=_params(("parallel",)))(x, g, wgt, wut, wd)


def _ffn_dx(dxo, x, g, gate_s, up_s, wgt, wut, wd, name, tm=256):
    T, D = x.shape
    F = wd.shape[0]

    def body(dxo_ref, x_ref, g_ref, gate_ref, up_ref, wg_ref, wu_ref, wd_ref, dx_ref, dff_ref, dgate_ref, dup_ref, dg_ref, dff_scr):
        @pl.when(pl.program_id(0) == 0)
        def _():
            dg_ref[...] = jnp.zeros_like(dg_ref)

        d = (FFN_RES_WEIGHT * dxo_ref[...]).astype(BF16)
        dff_scr[...] = d
        dff_ref[...] = d
        dh = None
        for c0, cw in _chunks(F, FFN_CHUNK):
            da = _dot_nt(dff_scr[...], wd_ref[c0:c0 + cw, :])
            gate = gate_ref[:, c0:c0 + cw].astype(F32)
            up = up_ref[:, c0:c0 + cw].astype(F32)
            s = _sigmoid(gate)
            silu = gate * s
            dup = (da * silu).astype(BF16)
            dgate = (da * up * (s * (1.0 + gate * (1.0 - s)))).astype(BF16)
            dgate_ref[:, c0:c0 + cw] = dgate
            dup_ref[:, c0:c0 + cw] = dup
            t = _dot(dgate, wg_ref[c0:c0 + cw, :]) + _dot(dup, wu_ref[c0:c0 + cw, :])
            dh = t if dh is None else dh + t
        dxn, dg = _rms_bwd(dh, x_ref[...], g_ref[...])
        dg_ref[...] += dg
        dx_ref[...] = dxo_ref[...] + dxn

    row = pl.BlockSpec((tm, D), lambda i: (i, 0))
    saved = pl.BlockSpec((tm, F), lambda i: (i, 0))
    return pl.pallas_call(
        body, name=name, grid=(T // tm,),
        in_specs=[row, row, pl.BlockSpec((1, D), lambda i: (0, 0)), saved, saved, _resident(wgt.shape), _resident(wut.shape),
                  _resident(wd.shape)],
        out_specs=[row, row, saved, saved, pl.BlockSpec((SUBLANES, D), lambda i: (0, 0))],
        out_shape=[_sds((T, D), F32), _sds((T, D), BF16), _sds((T, F), BF16), _sds((T, F), BF16), _sds((SUBLANES, D), F32)],
        scratch_shapes=[pltpu.VMEM((tm, D), BF16)], compiler_params=_params(("arbitrary",)))(dxo, x, g, gate_s, up_s, wgt, wut, wd)


def _tn(a, b, mb, name, tk=2048, dep=None):
    T, M = a.shape
    N = b.shape[1]
    nt = T // tk

    def body(a_ref, b_ref, *refs):
        o_ref, ob_ref = refs[-2:]

        @pl.when(pl.program_id(1) == 0)
        def _():
            o_ref[...] = jnp.zeros_like(o_ref)

        o_ref[...] += _dot_tn(a_ref[...].astype(BF16), b_ref[...].astype(BF16))

        @pl.when(pl.program_id(1) == nt - 1)
        def _():
            ob_ref[...] = o_ref[...].astype(BF16)

    o_spec = pl.BlockSpec((mb, N), lambda g, t: (g, 0))
    return pl.pallas_call(
        body, name=name, grid=(M // mb, nt),
        in_specs=[pl.BlockSpec((tk, mb), lambda g, t: (t, g)), pl.BlockSpec((tk, N), lambda g, t: (t, 0))] + ([ANY] if dep is not None else []),
        out_specs=[o_spec, o_spec], out_shape=[_sds((M, N), F32), _sds((M, N), BF16)],
        compiler_params=_params(("parallel", "arbitrary")))(a, b, *([dep] if dep is not None else []))


def _rope_tables(pos_col, inv_freq):
    T = pos_col.shape[0]

    def body(p_ref, f_ref, c_ref, s_ref):
        ang = p_ref[...].astype(F32) * f_ref[...]
        lane = lax.broadcasted_iota(jnp.int32, ang.shape, 1)
        c_ref[...] = jnp.cos(ang)
        sn = jnp.sin(ang)
        s_ref[...] = jnp.where((lane % HEAD_DIM) < HEAD_DIM // 2, -sn, sn)

    tm = 1024
    return pl.pallas_call(
        body, name="rope_tables", grid=(T // tm,),
        in_specs=[pl.BlockSpec((tm, 1), lambda i: (i, 0)), pl.BlockSpec((1, LANES), lambda i: (0, 0))],
        out_specs=[pl.BlockSpec((tm, LANES), lambda i: (i, 0))] * 2,
        out_shape=[_sds((T, LANES), F32)] * 2, compiler_params=_params(("parallel",)))(pos_col, inv_freq)


def _deinterleave(scr, out_ref, d, tm, nblk):
    for r in range(d):
        for cb in range(nblk):
            out_ref[r, :, cb * LANES:(cb + 1) * LANES] = scr[cb, pl.ds(r, tm // d, stride=d), :].astype(out_ref.dtype)


def _interleave(in_ref, scr, d, tm, nblk):
    for r in range(d):
        for cb in range(nblk):
            scr[cb, pl.ds(r, tm // d, stride=d), :] = in_ref[r, :, cb * LANES:(cb + 1) * LANES].astype(F32)


def _proj_rope(x, g, w_in, cos, sin, tm=512):
    T, D = x.shape
    dils = [d for _, d in B_PATTERNS if d > 1]
    nbb = B_W // LANES
    scale = HEAD_DIM ** -0.5
    cuts = [0, A_Q_W, A_Q_W + A_KV_W, A_Q_W + 2 * A_KV_W, A_Q_W + 2 * A_KV_W + B_W, A_Q_W + 2 * A_KV_W + 2 * B_W,
            A_Q_W + 2 * A_KV_W + 3 * B_W]

    def body(x_ref, g_ref, w_ref, c_ref, s_ref, h_ref, aq_ref, ak_ref, av_ref, *rest):
        b_refs, scr = rest[:-1], rest[-1]
        xhat, _ = _rms_stats(x_ref[...])
        h = (xhat * g_ref[...]).astype(BF16)
        h_ref[...] = h
        cs, sn = c_ref[...], s_ref[...]

        def seg(idx, rope, mult):
            lo, hi = cuts[idx], cuts[idx + 1]
            blocks = []
            whole = _dot_nt(h, w_ref[lo:hi, :])
            for cb in range((hi - lo) // LANES):
                p = whole[:, cb * LANES:(cb + 1) * LANES]
                if rope:
                    p = p * cs + _swap32(p) * sn
                if mult != 1.0:
                    p = p * mult
                blocks.append(p)
            return blocks

        for idx, ref, rope, mult in ((0, aq_ref, True, scale), (1, ak_ref, True, 1.0), (2, av_ref, False, 1.0)):
            for cb, p in enumerate(seg(idx, rope, mult)):
                ref[:, cb * LANES:(cb + 1) * LANES] = p.astype(BF16)
        for which, (idx, rope, mult) in enumerate(((3, True, scale), (4, True, 1.0), (5, False, 1.0))):
            for cb, p in enumerate(seg(idx, rope, mult)):
                b_refs[which][:, cb * LANES:(cb + 1) * LANES] = p.astype(BF16)
                scr[cb] = p
            for di, d in enumerate(dils):
                _deinterleave(scr, b_refs[3 * (di + 1) + which], d, tm, nbb)

    row = lambda w: pl.BlockSpec((tm, w), lambda i: (i, 0))
    out_specs = [row(D), row(A_Q_W), row(A_KV_W), row(A_KV_W)] + [row(B_W)] * 3
    out_shape = [_sds((T, D), BF16), _sds((T, A_Q_W), BF16), _sds((T, A_KV_W), BF16), _sds((T, A_KV_W), BF16)] + [_sds((T, B_W), BF16)] * 3
    for d in dils:
        out_specs += [pl.BlockSpec((d, tm // d, B_W), lambda i: (0, i, 0))] * 3
        out_shape += [_sds((d, T // d, B_W), BF16)] * 3
    return pl.pallas_call(
        body, name="proj_rope", grid=(T // tm,),
        in_specs=[row(D), pl.BlockSpec((1, D), lambda i: (0, 0)), pl.BlockSpec(w_in.shape, lambda i: (0, 0)), row(LANES), row(LANES)],
        out_specs=out_specs, out_shape=out_shape, scratch_shapes=[pltpu.VMEM((nbb, tm, LANES), F32)],
        compiler_params=_params(("parallel",)))(x, g, w_in, cos, sin)


def _band_bias(rel, kw, hw):
    ri = lax.broadcasted_iota(jnp.int32, (2 * QB, kw), 0) & (QB - 1)
    ci = lax.broadcasted_iota(jnp.int32, (2 * QB, kw), 1)
    return jnp.where(jnp.abs(ri + rel - ci) <= hw, 0.0, NEG).astype(F32)


def _stack_heads(x, lo):
    z = jnp.zeros_like(x)
    return jnp.concatenate([jnp.where(lo, x, z), jnp.where(lo, z, x)], axis=0)


def _unstack_heads(y, lo):
    return jnp.where(lo, y[:QB], y[QB:])


def _band_setup(bias_scr, kw, hw):
    if bias_scr is not None:
        for i in range(3):
            bias_scr[i] = _band_bias(i * hw, kw, hw)


def _band_window(bias_scr, qs, L, kw, hw):
    ws = pl.multiple_of(jnp.clip(qs - hw, 0, L - kw), 64)
    if bias_scr is None:
        return ws, _band_bias(qs - ws, kw, hw)
    return ws, bias_scr[lax.shift_right_logical(qs - ws, hw.bit_length() - 1)]


def _dup_kv_head(src_ref, dst_ref, head, L):
    step = min(L, 1024)
    for r0 in range(0, L, step):
        xf = src_ref[r0:r0 + step, :].astype(F32)
        lane = lax.broadcasted_iota(jnp.int32, xf.shape, 1)
        keep = jnp.logical_xor(lane < HEAD_DIM, head == 1)
        dst_ref[r0:r0 + step, :] = jnp.where(keep, xf, pltpu.roll(xf, HEAD_DIM, axis=1)).astype(dst_ref.dtype)


def _attn_fwd(q, k, v, sink, hw, gqa, out_dtype, name, blocks_per_step=8):
    NB, L, Cq = q.shape
    Ls = min(L, 2048)
    kw = min(QB + 2 * hw, L)
    tables = L >= QB + 2 * hw
    unroll = min(blocks_per_step, Ls // QB)
    nlb = 1 if (gqa or L > SHORT_SEQ) else Cq // LANES

    def body(sink_ref, q_ref, k_ref, v_ref, o_ref, lse_ref, *scr):
        b, s_idx = pl.program_id(1), pl.program_id(2)
        bias_scr = scr[0] if tables else None
        _band_setup(bias_scr, kw, hw)
        if gqa:
            kd, vd = scr[-2:]

            @pl.when(s_idx == 0)
            def _():
                _dup_kv_head(k_ref, kd, b // 2, L)
                _dup_kv_head(v_ref, vd, b // 2, L)
        else:
            kd, vd = k_ref, v_ref
        lane = lax.broadcasted_iota(jnp.int32, (QB, LANES), 1)
        lo = lane < HEAD_DIM
        if gqa:
            row = lax.broadcasted_iota(jnp.int32, (2 * QB, 1), 0)
            sk = jnp.where(row < QB, sink_ref[2 * b], sink_ref[2 * b + 1])

        def block(ql, col):
            qs = s_idx * Ls + ql
            ws, bias = _band_window(bias_scr, qs, L, kw, hw)
            kv_, vv = kd[pl.ds(ws, kw), col], vd[pl.ds(ws, kw), col]
            s = _dot_nt(_stack_heads(q_ref[pl.ds(ql, QB), col], lo), kv_) + bias
            m = jnp.max(s, axis=-1, keepdims=True)
            if gqa:
                m = jnp.maximum(m, sk)
            p = jnp.exp(s - m)
            den = jnp.sum(p, axis=-1, keepdims=True)
            if gqa:
                den = den + jnp.exp(sk - m)
            o_ref[pl.ds(ql, QB), col] = _unstack_heads(_dot(p.astype(BF16), vv) * (1.0 / den), lo).astype(o_ref.dtype)
            lse_ref[pl.ds(ql, QB), col] = _unstack_heads(m + jnp.log(den), lo)

        for lb in range(nlb):
            def step(n, carry, col=slice(lb * LANES, (lb + 1) * LANES)):
                for u in range(unroll):
                    block(pl.multiple_of((n * unroll + u) * QB, QB), col)
                return carry

            lax.fori_loop(0, Ls // (QB * unroll), step, 0)

    kv_map = (lambda r, b, s: (r, 0, 0)) if gqa else (lambda r, b, s: (r, 0, b))
    seg = pl.BlockSpec((None, Ls, nlb * LANES), lambda r, b, s: (r, s, b))
    return pl.pallas_call(
        body, name=name, grid=(NB, Cq // (nlb * LANES), L // Ls),
        in_specs=[pl.BlockSpec(memory_space=pltpu.SMEM), seg, pl.BlockSpec((None, L, nlb * LANES), kv_map),
                  pl.BlockSpec((None, L, nlb * LANES), kv_map)],
        out_specs=[seg, seg], out_shape=[_sds((NB, L, Cq), out_dtype), _sds((NB, L, Cq), F32)],
        scratch_shapes=([pltpu.VMEM((3, 2 * QB, kw), F32)] if tables else []) + ([pltpu.VMEM((L, LANES), BF16)] * 2 if gqa else []),
        compiler_params=_params(("parallel", "parallel", "arbitrary")))(sink, q, k, v)


def _attn_bwd(q, k, v, do, lse, delta, sink, hw, gqa, name, blocks_per_step=8):
    NB, L, Cq = q.shape
    Ck = k.shape[2]
    Ls = min(L, 2048)
    kw = min(QB + 2 * hw, L)
    reps = kw // LANES
    nseg = L // Ls
    scale = HEAD_DIM ** -0.5
    tables = L >= QB + 2 * hw
    unroll = min(blocks_per_step, Ls // QB)
    nlb = 1 if (gqa or L > SHORT_SEQ) else Cq // LANES

    def body(sink_ref, q_ref, do_ref, lse_ref, dl_ref, k_ref, v_ref, dq_ref, dk_ref, dv_ref, dsk_ref, *scr):
        b, s_idx = pl.program_id(1), pl.program_id(2)
        lane = lax.broadcasted_iota(jnp.int32, (QB, LANES), 1)
        lo = lane < HEAD_DIM
        bias_scr = scr[0] if tables else None
        _band_setup(bias_scr, kw, hw)
        if gqa:
            kd, vd, dk_acc, dv_acc, dsk_acc = scr[-5:]

            @pl.when(s_idx == 0)
            def _():
                _dup_kv_head(k_ref, kd, b // 2, L)
                _dup_kv_head(v_ref, vd, b // 2, L)
                dk_acc[...] = jnp.zeros_like(dk_acc)
                dv_acc[...] = jnp.zeros_like(dv_acc)
                dsk_acc[...] = jnp.zeros_like(dsk_acc)

            @pl.when((s_idx == 0) & (b == 0))
            def _():
                dk_ref[...] = jnp.zeros_like(dk_ref)
                dv_ref[...] = jnp.zeros_like(dv_ref)
        else:
            kd, vd = k_ref, v_ref
            dk_acc, dv_acc = scr[-2:]

            @pl.when(s_idx == 0)
            def _():
                dk_acc[...] = jnp.zeros_like(dk_acc)
                dv_acc[...] = jnp.zeros_like(dv_acc)

        def block(ql, col):
            qs = s_idx * Ls + ql
            ws, bias = _band_window(bias_scr, qs, L, kw, hw)
            qv, dov = q_ref[pl.ds(ql, QB), col], do_ref[pl.ds(ql, QB), col]
            lse, dl = lse_ref[pl.ds(ql, QB), col], dl_ref[pl.ds(ql, QB), col]
            kv_, vv = kd[pl.ds(ws, kw), col], vd[pl.ds(ws, kw), col]
            q2, do2 = _stack_heads(qv, lo), _stack_heads(dov, lo)
            lse_sw, dl_sw = pltpu.roll(lse, HEAD_DIM, axis=1), pltpu.roll(dl, HEAD_DIM, axis=1)
            lse2 = jnp.concatenate([jnp.where(lo, lse, lse_sw), jnp.where(lo, lse_sw, lse)], axis=0)
            dl2 = jnp.concatenate([jnp.where(lo, dl, dl_sw), jnp.where(lo, dl_sw, dl)], axis=0)
            p = jnp.exp(_dot_nt(q2, kv_) + bias - jnp.tile(lse2, (1, reps)))
            ds = (p * (_dot_nt(do2, vv) - jnp.tile(dl2, (1, reps)))).astype(BF16)
            dq_ref[pl.ds(ql, QB), col] = (_unstack_heads(_dot(ds, kv_), lo) * scale).astype(dq_ref.dtype)
            dk_acc[pl.ds(ws, kw), col] += _dot_tn(ds, q2)
            dv_acc[pl.ds(ws, kw), col] += _dot_tn(p.astype(BF16), do2)
            if gqa:
                sk = jnp.where(lo, sink_ref[2 * b], sink_ref[2 * b + 1])
                dsk_acc[...] += -jnp.exp(sk - lse) * dl

        for lb in range(nlb):
            def step(n, carry, col=slice(lb * LANES, (lb + 1) * LANES)):
                for u in range(unroll):
                    block(pl.multiple_of((n * unroll + u) * QB, QB), col)
                return carry

            lax.fori_loop(0, Ls // (QB * unroll), step, 0)

        if gqa:
            @pl.when(s_idx == nseg - 1)
            def _():
                step_rows = min(L, 1024)
                for r0 in range(0, L, step_rows):
                    lanek = lax.broadcasted_iota(jnp.int32, (step_rows, LANES), 1)
                    mine = jnp.logical_xor(lanek < HEAD_DIM, (b // 2) == 1)
                    for acc, ref in ((dk_acc, dk_ref), (dv_acc, dv_ref)):
                        a = acc[r0:r0 + step_rows, :]
                        ref[r0:r0 + step_rows, :] += jnp.where(mine, a + pltpu.roll(a, HEAD_DIM, axis=1), 0.0)
                dsk_ref[...] = dsk_acc[...].reshape(QB // SUBLANES, SUBLANES, LANES).sum(axis=0)
        else:
            dsk_ref[...] = jnp.zeros_like(dsk_ref)

            @pl.when(s_idx == nseg - 1)
            def _():
                dk_ref[...] = dk_acc[...].astype(dk_ref.dtype)
                dv_ref[...] = dv_acc[...].astype(dv_ref.dtype)

    kv_map = (lambda r, b, s: (r, 0, 0)) if gqa else (lambda r, b, s: (r, 0, b))
    seg = pl.BlockSpec((None, Ls, nlb * LANES), lambda r, b, s: (r, s, b))
    full = pl.BlockSpec((None, L, nlb * LANES), kv_map)
    scratch = [pltpu.VMEM((3, 2 * QB, kw), F32)] if tables else []
    if gqa:
        scratch += [pltpu.VMEM((L, LANES), BF16)] * 2 + [pltpu.VMEM((L, LANES), F32)] * 2 + [pltpu.VMEM((QB, LANES), F32)]
    else:
        scratch += [pltpu.VMEM((L, nlb * LANES), F32)] * 2
    kv_dtype = F32 if gqa else BF16
    return pl.pallas_call(
        body, name=name, grid=(NB, Cq // (nlb * LANES), nseg),
        in_specs=[pl.BlockSpec(memory_space=pltpu.SMEM), seg, seg, seg, seg, full, full],
        out_specs=[seg, full, full, pl.BlockSpec((None, None, SUBLANES, LANES), lambda r, b, s: (r, b, 0, 0))],
        out_shape=[_sds((NB, L, Cq), BF16), _sds((NB, L, Ck), kv_dtype), _sds((NB, L, Ck), kv_dtype),
                   _sds((NB, Cq // LANES, SUBLANES, LANES), F32)],
        scratch_shapes=scratch,
        compiler_params=_params(("arbitrary", "arbitrary", "arbitrary")))(sink, q, do, lse, delta, k, v)


def _merge_b(a_out, o1, l1, o4, l4, o16, l16, tm=512):
    T = a_out.shape[0]
    nbb = B_W // LANES

    def body(a_ref, o1_ref, l1_ref, o4_ref, l4_ref, o16_ref, l16_ref, cat_ref, lg1_ref, lg4_ref, lg16_ref, so, sl, slg):
        _interleave(o4_ref, so.at[0], 4, tm, nbb)
        _interleave(l4_ref, sl.at[0], 4, tm, nbb)
        _interleave(o16_ref, so.at[1], 16, tm, nbb)
        _interleave(l16_ref, sl.at[1], 16, tm, nbb)
        cat_ref[:, 0:A_Q_W] = a_ref[...]
        for cb in range(nbb):
            cols = slice(cb * LANES, (cb + 1) * LANES)
            os_ = (o1_ref[:, cols], so[0, cb], so[1, cb])
            ls_ = (l1_ref[:, cols], sl[0, cb], sl[1, cb])
            m = jnp.maximum(jnp.maximum(ls_[0], ls_[1]), ls_[2])
            es = [jnp.exp(l - m) for l in ls_]
            den = es[0] + es[1] + es[2]
            out = (es[0] * os_[0] + es[1] * os_[1] + es[2] * os_[2]) * (1.0 / den)
            lg = m + jnp.log(den)
            cat_ref[:, A_Q_W + cb * LANES:A_Q_W + (cb + 1) * LANES] = out.astype(BF16)
            lg1_ref[:, cols] = lg
            slg[cb] = lg
        _deinterleave(slg, lg4_ref, 4, tm, nbb)
        _deinterleave(slg, lg16_ref, 16, tm, nbb)

    row = lambda w: pl.BlockSpec((tm, w), lambda i: (i, 0))
    perm = lambda d: pl.BlockSpec((d, tm // d, B_W), lambda i: (0, i, 0))
    return pl.pallas_call(
        body, name="merge_patterns", grid=(T // tm,),
        in_specs=[row(A_Q_W), row(B_W), row(B_W), perm(4), perm(4), perm(16), perm(16)],
        out_specs=[row(A_Q_W + B_W), row(B_W), perm(4), perm(16)],
        out_shape=[_sds((T, A_Q_W + B_W), BF16), _sds((T, B_W), F32), _sds((4, T // 4, B_W), F32), _sds((16, T // 16, B_W), F32)],
        scratch_shapes=[pltpu.VMEM((2, nbb, tm, LANES), F32), pltpu.VMEM((2, nbb, tm, LANES), F32), pltpu.VMEM((nbb, tm, LANES), F32)],
        compiler_params=_params(("parallel",)))(a_out, o1, l1, o4, l4, o16, l16)


def _out_proj(x, cat, w_out, tm=512):
    T, D = x.shape

    def body(x_ref, c_ref, w_ref, o_ref):
        o_ref[...] = x_ref[...] + _dot(c_ref[...], w_ref[...])

    row = lambda w: pl.BlockSpec((tm, w), lambda i: (i, 0))
    return pl.pallas_call(
        body, name="out_proj", grid=(T // tm,), in_specs=[row(D), row(cat.shape[1]), pl.BlockSpec(w_out.shape, lambda i: (0, 0))],
        out_specs=row(D), out_shape=_sds((T, D), F32), compiler_params=_params(("parallel",)))(x, cat, w_out)


def _final_loss(x, g, target, tm=512):
    T, D = x.shape

    def body(x_ref, g_ref, t_ref, dx_ref, dg_ref, loss_ref):
        @pl.when(pl.program_id(0) == 0)
        def _():
            dg_ref[...] = jnp.zeros_like(dg_ref)
            loss_ref[...] = jnp.zeros_like(loss_ref)

        xv, gv = x_ref[...], g_ref[...]
        xhat, _ = _rms_stats(xv)
        err = xhat * gv - t_ref[...]
        loss_ref[...] += 0.5 * jnp.sum(jnp.sum(err * err, axis=-1, keepdims=True) * (1.0 / D), axis=0, keepdims=True)
        dx, dg = _rms_bwd(err * (1.0 / D), xv, gv)
        dx_ref[...] = dx
        dg_ref[...] += dg

    row = pl.BlockSpec((tm, D), lambda i: (i, 0))
    return pl.pallas_call(
        body, name="final_loss", grid=(T // tm,), in_specs=[row, pl.BlockSpec((1, D), lambda i: (0, 0)), row],
        out_specs=[row, pl.BlockSpec((SUBLANES, D), lambda i: (0, 0)), pl.BlockSpec((SUBLANES, LANES), lambda i: (0, 0))],
        out_shape=[_sds((T, D), F32), _sds((SUBLANES, D), F32), _sds((SUBLANES, LANES), F32)],
        compiler_params=_params(("arbitrary",)))(x, g, target)


def _dcat(dx, w_out, cat, tm=512):
    T, D = dx.shape
    C = cat.shape[1]
    nba, nbb = A_Q_W // LANES, B_W // LANES

    def body(dx_ref, w_ref, cat_ref, doa_ref, dla_ref, dob1_ref, dlb1_ref, dob4_ref, dlb4_ref, dob16_ref, dlb16_ref, sdo, sdl):
        dc = _dot_nt(dx_ref[...].astype(BF16), w_ref[...])
        ri = lax.broadcasted_iota(jnp.int32, (LANES, LANES), 0)
        ci = lax.broadcasted_iota(jnp.int32, (LANES, LANES), 1)
        same_head = ((ri // HEAD_DIM) == (ci // HEAD_DIM)).astype(BF16)
        for cb in range(C // LANES):
            cols = slice(cb * LANES, (cb + 1) * LANES)
            blk = dc[:, cols]
            prod = blk * cat_ref[:, cols].astype(F32)
            hi = prod.astype(BF16)
            lo_ = (prod - hi.astype(F32)).astype(BF16)
            dl = _dot(hi, same_head) + _dot(lo_, same_head)
            if cb < nba:
                doa_ref[:, cols] = blk.astype(BF16)
                dla_ref[:, cols] = dl
            else:
                bcols = slice((cb - nba) * LANES, (cb - nba + 1) * LANES)
                dob1_ref[:, bcols] = blk.astype(BF16)
                dlb1_ref[:, bcols] = dl
                sdo[cb - nba] = blk
                sdl[cb - nba] = dl
        _deinterleave(sdo, dob4_ref, 4, tm, nbb)
        _deinterleave(sdl, dlb4_ref, 4, tm, nbb)
        _deinterleave(sdo, dob16_ref, 16, tm, nbb)
        _deinterleave(sdl, dlb16_ref, 16, tm, nbb)

    row = lambda w: pl.BlockSpec((tm, w), lambda i: (i, 0))
    perm = lambda d: pl.BlockSpec((d, tm // d, B_W), lambda i: (0, i, 0))
    return pl.pallas_call(
        body, name="dcat", grid=(T // tm,), in_specs=[row(D), pl.BlockSpec(w_out.shape, lambda i: (0, 0)), row(C)],
        out_specs=[row(A_Q_W), row(A_Q_W), row(B_W), row(B_W), perm(4), perm(4), perm(16), perm(16)],
        out_shape=[_sds((T, A_Q_W), BF16), _sds((T, A_Q_W), F32), _sds((T, B_W), BF16), _sds((T, B_W), F32),
                   _sds((4, T // 4, B_W), BF16), _sds((4, T // 4, B_W), F32), _sds((16, T // 16, B_W), BF16), _sds((16, T // 16, B_W), F32)],
        scratch_shapes=[pltpu.VMEM((nbb, tm, LANES), F32)] * 2, compiler_params=_params(("parallel",)))(dx, w_out, cat)


def _rope_bwd_assemble(dqa, dka, dva, b1, b4, b16, cos, sin, tm=512):
    T = dqa.shape[0]
    nbb = B_W // LANES
    width = A_Q_W + 2 * A_KV_W + 3 * B_W

    def body(dqa_ref, dka_ref, dva_ref, q1, k1, v1, q4, k4, v4, q16, k16, v16, c_ref, s_ref, o_ref, scr):
        cs, sn = c_ref[...], s_ref[...]

        def unrope(t):
            return t * cs + _swap32(t * sn)

        col = 0
        for ref, rope in ((dqa_ref, True), (dka_ref, True), (dva_ref, False)):
            for cb in range(ref.shape[1] // LANES):
                t = ref[:, cb * LANES:(cb + 1) * LANES].astype(F32)
                o_ref[:, col:col + LANES] = (unrope(t) if rope else t).astype(BF16)
                col += LANES
        for which, (r1, r4, r16, rope) in enumerate(((q1, q4, q16, True), (k1, k4, k16, True), (v1, v4, v16, False))):
            _interleave(r4, scr.at[0], 4, tm, nbb)
            _interleave(r16, scr.at[1], 16, tm, nbb)
            for cb in range(nbb):
                t = r1[:, cb * LANES:(cb + 1) * LANES].astype(F32) + scr[0, cb] + scr[1, cb]
                o_ref[:, col:col + LANES] = (unrope(t) if rope else t).astype(BF16)
                col += LANES

    row = lambda w: pl.BlockSpec((tm, w), lambda i: (i, 0))
    perm = lambda d: pl.BlockSpec((d, tm // d, B_W), lambda i: (0, i, 0))
    return pl.pallas_call(
        body, name="rope_bwd", grid=(T // tm,),
        in_specs=[row(A_Q_W), row(A_KV_W), row(A_KV_W)] + [row(B_W)] * 3 + [perm(4)] * 3 + [perm(16)] * 3 + [row(LANES), row(LANES)],
        out_specs=row(width), out_shape=_sds((T, width), BF16), scratch_shapes=[pltpu.VMEM((2, nbb, tm, LANES), F32)],
        compiler_params=_params(("parallel",)))(dqa, dka, dva, *b1, *b4, *b16, cos, sin)


def _dh_norm(dproj, w_in, x, g, dres, tm=512):
    T, D = x.shape

    def body(dp_ref, w_ref, x_ref, g_ref, dr_ref, dx_ref, dg_ref):
        @pl.when(pl.program_id(0) == 0)
        def _():
            dg_ref[...] = jnp.zeros_like(dg_ref)

        dxn, dg = _rms_bwd(_dot(dp_ref[...], w_ref[...]), x_ref[...], g_ref[...])
        dg_ref[...] += dg
        dx_ref[...] = dr_ref[...] + dxn

    row = lambda w: pl.BlockSpec((tm, w), lambda i: (i, 0))
    return pl.pallas_call(
        body, name="dh_norm", grid=(T // tm,),
        in_specs=[row(dproj.shape[1]), pl.BlockSpec(w_in.shape, lambda i: (0, 0)), row(D), pl.BlockSpec((1, D), lambda i: (0, 0)), row(D)],
        out_specs=[row(D), pl.BlockSpec((SUBLANES, D), lambda i: (0, 0))],
        out_shape=[_sds((T, D), F32), _sds((SUBLANES, D), F32)], compiler_params=_params(("arbitrary",)))(dproj, w_in, x, g, dres)


def _grad_push_plan(n):
    def plan(refs):
        x, y, c = _mesh_pos()
        return [(refs[k].at[chip], refs[n + k].at[rel], dev) for k in range(n) for rel, (dev, chip) in enumerate(_chip_peers(x, y, c))]
    return plan


def _sum_own(me_arr, g, landed, name):
    ns, R, C = g.shape
    tr = R // 2 if (R // 2) % 16 == 0 else R

    def body(me_ref, g_ref, x_ref, o_ref):
        acc = g_ref[...]
        for rel in range(ns - 1):
            acc = acc + x_ref[rel].astype(F32)
        o_ref[...] = acc

    grid_spec = pltpu.PrefetchScalarGridSpec(
        num_scalar_prefetch=1, grid=(R // tr,),
        in_specs=[pl.BlockSpec((None, tr, C), lambda t, me: (me[0], t, 0)), pl.BlockSpec((ns - 1, tr, C), lambda t, me: (0, t, 0))],
        out_specs=pl.BlockSpec((tr, C), lambda t, me: (t, 0)))
    return pl.pallas_call(body, name=name, grid_spec=grid_spec, out_shape=_sds((R, C), F32),
                          compiler_params=_params(("parallel",)))(me_arr, g, landed)


def _swap_plan(n):
    def plan(refs):
        x, y, c = _mesh_pos()
        return [(refs[k], refs[n + k], (x, y, 1 - c)) for k in range(n)]
    return plan


def _allreduce_small(v):
    rows, W = v.shape

    def body(v_ref, o_ref, buf, send, recv):
        x, y, c = _mesh_pos()
        me = 4 * x + 2 * y + c
        cps = []
        for m in range(1, N_DEV):
            dev = (x ^ (m >> 2), y ^ ((m >> 1) & 1), c ^ (m & 1))
            cp = pltpu.make_async_remote_copy(src_ref=v_ref, dst_ref=buf.at[me], send_sem=send.at[m - 1], recv_sem=recv.at[m - 1],
                                              device_id=dev, device_id_type=MESH)
            cp.start()
            cps.append(cp)
        for m in range(1, N_DEV):
            pltpu.make_async_remote_copy(src_ref=v_ref, dst_ref=buf.at[me ^ m], send_sem=send.at[m - 1], recv_sem=recv.at[m - 1],
                                         device_id=(x, y, c), device_id_type=MESH).wait_recv()
        for cp in cps:
            cp.wait_send()
        buf[me] = v_ref[...]
        acc = buf[0]
        for i in range(1, N_DEV):
            acc = acc + buf[i]
        o_ref[...] = acc

    return pl.pallas_call(
        body, name="allreduce_small", out_shape=_sds((rows, W), F32),
        scratch_shapes=[pltpu.VMEM((N_DEV, rows, W), F32), pltpu.SemaphoreType.DMA((N_DEV - 1,)), pltpu.SemaphoreType.DMA((N_DEV - 1,))],
        compiler_params=_params())(v)


def _adamw(w, gp, gq, m, v, name):
    R, C = w.shape
    tr = R // 2 if (R // 2) % SUBLANES == 0 else R
    c1 = 1.0 / (1.0 - ADAM_B1 ** ADAM_STEP)
    c2 = 1.0 / (1.0 - ADAM_B2 ** ADAM_STEP)

    def body(w_ref, gp_ref, gq_ref, m_ref, v_ref, g_ref, d_ref, nm_ref, nv_ref):
        gv = gp_ref[...] + gq_ref[...]
        nm = ADAM_B1 * m_ref[...] + (1.0 - ADAM_B1) * gv
        nv = ADAM_B2 * v_ref[...] + (1.0 - ADAM_B2) * (gv * gv)
        g_ref[...] = gv
        d_ref[...] = -ADAM_LR * ((nm * c1) / (jnp.sqrt(nv * c2) + ADAM_EPS) + ADAM_WD * w_ref[...])
        nm_ref[...] = nm
        nv_ref[...] = nv

    blk = pl.BlockSpec((tr, C), lambda t: (t, 0))
    return pl.pallas_call(body, name=name, grid=(R // tr,), in_specs=[blk] * 5, out_specs=[blk] * 4,
                          out_shape=[_sds((R, C), F32)] * 4, compiler_params=_params(("parallel",)))(w, gp, gq, m, v)


def _local_step(x, positions, target, norms, a_sink, comm):
    T, D = x.shape
    g1, gm, g2, gf = norms
    inv_freq = 1.0 / (ROPE_THETA ** (jnp.arange(0, HEAD_DIM, 2, dtype=F32) / HEAD_DIM))
    inv_freq = jnp.tile(inv_freq, LANES // (HEAD_DIM // 2)).reshape(1, LANES)
    cos, sin = _rope_tables(positions.reshape(T, 1), inv_freq)
    no_sink = jnp.zeros((2 * (B_W // LANES),), F32)
    W = {k: comm.weight(k, x) for k in ("wg1", "wu1", "wd1")}

    x1, h1, gate1, up1, act1 = _ffn_fwd(x, comm.order(g1), W["wg1"], W["wu1"], W["wd1"], "ffn1_fwd")
    W["w_in"] = comm.weight("w_in", x1)
    (h2, aq, ak, av, bq1, bk1, bv1, bq4, bk4, bv4, bq16, bk16, bv16) = _proj_rope(x1, gm, W["w_in"], cos, sin)
    a_out, a_lse = _attn_fwd(aq[None], ak[None], av[None], a_sink, A_HALF_WINDOW, True, BF16, "attn_a_fwd")
    bqs = {1: (bq1[None], bk1[None], bv1[None]), 4: (bq4, bk4, bv4), 16: (bq16, bk16, bv16)}
    b_o, b_l = {}, {}
    for w, d in B_PATTERNS:
        q_, k_, v_ = bqs[d]
        b_o[d], b_l[d] = _attn_fwd(q_, k_, v_, no_sink, w // (2 * d), False, BF16, f"attn_b{d}_fwd")
    cat, lg1, lg4, lg16 = _merge_b(a_out[0], b_o[1][0], b_l[1][0], b_o[4], b_l[4], b_o[16], b_l[16])
    W["w_out"] = comm.weight("w_out", cat)
    x2 = _out_proj(x1, cat, W["w_out"])
    for k in ("wg2", "wu2", "wd2"):
        W[k] = comm.weight(k, x2)
    x3, h3, gate2, up2, act2 = _ffn_fwd(x2, g2, W["wg2"], W["wu2"], W["wd2"], "ffn2_fwd")

    dx3, dgf, loss8 = _final_loss(x3, gf, target)
    dx2, dff2, dgate2, dup2, dg2 = _ffn_dx(dx3, x2, g2, gate2, up2, W["wg2"], W["wu2"], W["wd2"], "ffn2_dx")
    fb = gate2.shape[1] // 2
    dwg2 = _tn(dgate2, h3, fb, "ffn2_dw_gate")
    dwu2 = _tn(dup2, h3, fb, "ffn2_dw_up")
    dwd2 = _tn(act2, dff2, fb, "ffn2_dw_down")
    comm.ready(dict(wg2=dwg2, wu2=dwu2, wd2=dwd2), dwd2[0])

    doa, dla, dob1, dlb1, dob4, dlb4, dob16, dlb16 = _dcat(dx2, W["w_out"], cat)
    dw_out = _tn(cat, dx2, cat.shape[1], "w_out_dw", dep=comm.dep())
    dqa, dka, dva, dsk = _attn_bwd(aq[None], ak[None], av[None], doa[None], a_lse, dla[None], comm.order(a_sink), A_HALF_WINDOW, True,
                                   "attn_a_bwd")
    bwd_in = {1: (dob1[None], lg1[None], dlb1[None]), 4: (dob4, lg4, dlb4), 16: (dob16, lg16, dlb16)}
    bg = {}
    for w, d in B_PATTERNS:
        q_, k_, v_ = bqs[d]
        do_, l_, dl_ = bwd_in[d]
        bg[d] = _attn_bwd(q_, k_, v_, do_, l_, dl_, no_sink, w // (2 * d), False, f"attn_b{d}_bwd")[:3]
    dproj = _rope_bwd_assemble(dqa[0], dka[0], dva[0], [t[0] for t in bg[1]], bg[4], bg[16], cos, sin)
    dw_in = _tn(dproj, h2, dproj.shape[1] // 2, "w_in_dw")
    comm.ready(dict(w_in=dw_in, w_out=dw_out), dw_in[0])
    dx1, dgm = _dh_norm(dproj, W["w_in"], x1, comm.order(gm), dx2)

    dx0, dff1, dgate1, dup1, dg1 = _ffn_dx(dx1, x, g1, gate1, up1, W["wg1"], W["wu1"], W["wd1"], "ffn1_dx")
    dwd1 = _tn(act1, dff1, fb, "ffn1_dw_down")
    comm.ready(dict(wd1=dwd1), dwd1[0])
    dwg1 = _tn(dgate1, h1, fb, "ffn1_dw_gate", dep=comm.dep())
    comm.ready(dict(wg1=dwg1), dwg1[0])
    dwu1 = _tn(dup1, h1, fb, "ffn1_dw_up", dep=comm.dep())
    comm.ready(dict(wu1=dwu1), dwu1[0])

    dsink = dsk[0, :, :, ::HEAD_DIM].sum(axis=1).reshape(-1)
    small = dict(g1=dg1.sum(axis=0), gm=dgm.sum(axis=0), g2=dg2.sum(axis=0), gf=dgf.sum(axis=0), sink=dsink, loss=loss8[0, 0])
    return dx0, small


BIG = ("wg1", "wu1", "wd1", "w_in", "w_out", "wg2", "wu2", "wd2")
GATHER_GROUPS = (("w_in",), ("w_out",), ("wg2", "wu2", "wd2"))


class _Comm:
    def __init__(self, shards):
        x, y, c = _mesh_pos()
        self.me = (2 * x + y).astype(jnp.int32).reshape(1)
        self.shards = shards
        self.tokens = []
        self.waiting = {}
        self.groups = []
        fulls = {k: _cast_place(self.me, shards[k], f"cast_{k}") for k in BIG}
        first = ("wg1", "wu1", "wd1")
        self.full = dict(zip(first, _gather_weights([fulls[k] for k in first])))
        dep = self.full["wd1"]
        for gi, names in enumerate(GATHER_GROUPS):
            plan = _gather_plan(len(names))
            send, recv, bufs, tok = _push_start(f"gather_start_{gi}", [fulls[k] for k in names], 3 * len(names), plan, dep)
            self.tokens.append(tok)
            dep = tok
            for k in names:
                self.waiting[k] = (gi, names, send, recv, bufs, plan)

    def order(self, a):
        for tok in self.tokens:
            a = a + tok[0, 0]
        self.tokens = []
        return a

    def dep(self):
        return self.tokens[-1] if self.tokens else None

    def weight(self, name, after):
        if name in self.waiting:
            gi, names, send, recv, bufs, plan = self.waiting[name]
            for k, buf in zip(names, _push_wait(f"gather_wait_{gi}", send, recv, bufs, plan, after)):
                self.full[k] = buf
                del self.waiting[k]
        full = self.full[name]
        return full.reshape(N_CHIPS * full.shape[1], full.shape[2])

    def ready(self, grads, after):
        names = list(grads)
        f32s, b16s = [], []
        for k in names:
            gf, gb = grads[k]
            f32s.append(gf.reshape((N_CHIPS,) + self.shards[k].shape))
            b16s.append(gb.reshape((N_CHIPS,) + self.shards[k].shape))
        n = len(names)
        lands = [lax.empty((N_CHIPS - 1,) + self.shards[k].shape, BF16) for k in names]
        plan = _grad_push_plan(n)
        gi = len(self.groups)
        send, recv, bufs, tok = _push_start(f"grad_start_{gi}", b16s + lands, 3 * n, plan, after)
        self.tokens.append(tok)
        self.groups.append((names, f32s, send, recv, bufs, plan))

    def finish(self):
        out, swaps = {}, []
        after = self.tokens[-1]
        for gi, (names, f32s, send, recv, bufs, plan) in enumerate(self.groups):
            n = len(names)
            bufs = _push_wait(f"grad_wait_{gi}", send, recv, bufs, plan, after)
            mine = [_sum_own(self.me, f32s[i], bufs[n + i], f"sum_{k}") for i, k in enumerate(names)]
            lands = [lax.empty(p.shape, F32) for p in mine]
            send2, recv2, both, after = _push_start(f"swap_start_{gi}", mine + lands, n, _swap_plan(n), mine[-1])
            swaps.append((names, send2, recv2, both))
        for gi, (names, send2, recv2, both) in enumerate(swaps):
            n = len(names)
            both = _push_wait(f"swap_wait_{gi}", send2, recv2, both, _swap_plan(n), after)
            for i, k in enumerate(names):
                out[k] = (both[i], both[n + i])
        return out


def kernel(x, positions, norm_ffn1, w_gate1, w_up1, w_down1, norm_mix, w_in, a_sink, w_out, norm_ffn2, w_gate2, w_up2, w_down2, norm_final, loss_target, m_norm_ffn1, m_w_gate1, m_w_up1, m_w_down1, m_norm_mix, m_w_in, m_a_sink, m_w_out, m_norm_ffn2, m_w_gate2, m_w_up2, m_w_down2, m_norm_final, v_norm_ffn1, v_w_gate1, v_w_up1, v_w_down1, v_norm_mix, v_w_in, v_a_sink, v_w_out, v_norm_ffn2, v_w_gate2, v_w_up2, v_w_down2, v_norm_final):
    T, D = x.shape[1], x.shape[2]
    flip = ("wg1", "wu1", "w_in", "wg2", "wu2")

    def rows(k, a):
        return a[0].T if k in flip else a[0]

    given = dict(wg1=(w_gate1, m_w_gate1, v_w_gate1), wu1=(w_up1, m_w_up1, v_w_up1), wd1=(w_down1, m_w_down1, v_w_down1),
                 w_in=(w_in, m_w_in, v_w_in), w_out=(w_out, m_w_out, v_w_out), wg2=(w_gate2, m_w_gate2, v_w_gate2),
                 wu2=(w_up2, m_w_up2, v_w_up2), wd2=(w_down2, m_w_down2, v_w_down2))
    shards = {k: rows(k, given[k][0]) for k in BIG}

    comm = _Comm(shards)

    norms = (norm_ffn1, norm_mix, norm_ffn2, norm_final.reshape(1, D))
    grad_x, small = _local_step(x[0], positions[0], loss_target[0], norms, a_sink[0], comm)

    partial = comm.finish()

    def pad_row(a):
        a = a.reshape(-1)
        return jnp.pad(a, (0, D - a.shape[0]))

    row4 = pad_row(jnp.concatenate([small["sink"], small["loss"].reshape(1)]))
    vec = jnp.stack([small["g1"], small["gm"], small["g2"], small["gf"], row4] + [jnp.zeros((D,), F32)] * 3, axis=0)
    red = _allreduce_small(vec)
    loss = red[4, 8]
    g_small = jnp.stack([red[0], red[1], red[2], red[3], pad_row(red[4, 0:8])] + [jnp.zeros((D,), F32)] * 3, axis=0)

    def small_stack(a1, am, a2, af, ask):
        return jnp.stack([pad_row(a1), pad_row(am), pad_row(a2), pad_row(af), pad_row(ask)] + [jnp.zeros((D,), F32)] * 3, axis=0)

    w_small = small_stack(norm_ffn1, norm_mix, norm_ffn2, norm_final, a_sink)
    m_small = small_stack(m_norm_ffn1, m_norm_mix, m_norm_ffn2, m_norm_final, m_a_sink)
    v_small = small_stack(v_norm_ffn1, v_norm_mix, v_norm_ffn2, v_norm_final, v_a_sink)
    live = small_stack(jnp.ones_like(norm_ffn1), jnp.ones_like(norm_mix), jnp.ones_like(norm_ffn2), jnp.ones_like(norm_final), jnp.ones_like(a_sink))
    v_small = jnp.where(live > 0, v_small, 1.0)

    upd = {}
    for k in BIG:
        outs = _adamw(shards[k], partial[k][0], partial[k][1], rows(k, given[k][1]), rows(k, given[k][2]), f"adamw_{k}")
        upd[k] = tuple((a.T if k in flip else a)[None] for a in outs)
    _, ds_, nms_, nvs_ = _adamw(w_small, g_small, jnp.zeros_like(g_small), m_small, v_small, "adamw_small")

    def small_out(arr):
        return [arr[0].reshape(1, D), arr[1].reshape(1, D), arr[2].reshape(1, D), arr[3], arr[4, 0:8].reshape(1, 8)]

    gs_, dss, nmss, nvss = small_out(g_small), small_out(ds_), small_out(nms_), small_out(nvs_)

    def ordered(i):
        sm = (gs_, dss, nmss, nvss)[i]
        return [sm[0], upd["wg1"][i], upd["wu1"][i], upd["wd1"][i], sm[1], upd["w_in"][i], sm[4], upd["w_out"][i], sm[2],
                upd["wg2"][i], upd["wu2"][i], upd["wd2"][i], sm[3]]

    return (loss, grad_x[None], *ordered(0), *ordered(1), *ordered(2), *ordered(3))
```

```python
import jax
import jax.numpy as jnp
from jax import lax
from jax.experimental import pallas as pl
from jax.experimental.pallas import tpu as pltpu

F32 = jnp.float32
BF16 = jnp.bfloat16

HEAD_DIM = 64
LANES = 128
SUBLANES = 8
A_Q_W, A_KV_W, B_W = 512, 128, 512
A_HALF_WINDOW = 128
B_PATTERNS = ((128, 1), (512, 4), (2048, 16))
ROPE_THETA = 10000.0
NORM_EPS = 1e-6
FFN_RES_WEIGHT = 0.5
ADAM_LR, ADAM_B1, ADAM_B2, ADAM_EPS, ADAM_WD, ADAM_STEP = 0.001, 0.9, 0.999, 1e-08, 0.01, 10
N_CHIPS = 4
N_DEV = 8
QB = 128
SHORT_SEQ = 512
NEG = -1e30
VMEM_LIMIT = 56 * 1024 * 1024
MESH = pl.DeviceIdType.MESH
ANY = pl.BlockSpec(memory_space=pl.ANY)


def _params(sem=None):
    return pltpu.CompilerParams(dimension_semantics=sem, vmem_limit_bytes=VMEM_LIMIT)


def _sds(shape, dtype):
    return jax.ShapeDtypeStruct(tuple(shape), dtype)


def _dot(a, b):
    return jnp.dot(a, b, preferred_element_type=F32)


def _dot_nt(a, b):
    return lax.dot_general(a, b, (((1,), (1,)), ((), ())), preferred_element_type=F32)


def _dot_tn(a, b):
    return lax.dot_general(a, b, (((0,), (0,)), ((), ())), preferred_element_type=F32)


def _rms_stats(x):
    r = lax.rsqrt(jnp.mean(x * x, axis=-1, keepdims=True) + NORM_EPS)
    return x * r, r


def _rms_bwd(dh, x, g):
    xhat, r = _rms_stats(x)
    dxn = dh * g
    dx = r * (dxn - xhat * jnp.mean(dxn * xhat, axis=-1, keepdims=True))
    tm, d = x.shape
    dg = (dh * xhat).reshape(tm // SUBLANES, SUBLANES, d).sum(axis=0)
    return dx, dg


def _sigmoid(x):
    return 1.0 / (1.0 + jnp.exp(-x))


def _swap32(t):
    n = t.shape[-1]
    lane = lax.broadcasted_iota(jnp.int32, t.shape, t.ndim - 1)
    return jnp.where((lane % HEAD_DIM) < HEAD_DIM // 2, pltpu.roll(t, n - HEAD_DIM // 2, axis=t.ndim - 1),
                     pltpu.roll(t, HEAD_DIM // 2, axis=t.ndim - 1))


def _cast_place(me_arr, w, name):
    R, C = w.shape
    tr = R // 2 if (R // 2) % 16 == 0 else R

    def body(me_ref, w_ref, o_ref):
        o_ref[...] = w_ref[...].astype(BF16)

    grid_spec = pltpu.PrefetchScalarGridSpec(
        num_scalar_prefetch=1, grid=(R // tr,), in_specs=[pl.BlockSpec((tr, C), lambda t, me: (t, 0))],
        out_specs=pl.BlockSpec((None, tr, C), lambda t, me: (me[0], t, 0)))
    return pl.pallas_call(body, name=name, grid_spec=grid_spec, out_shape=_sds((N_CHIPS, R, C), BF16),
                          compiler_params=_params(("parallel",)))(me_arr, w)


HBM = pl.BlockSpec(memory_space=pltpu.HBM)
SEM = pl.BlockSpec(memory_space=pltpu.SEMAPHORE)


def _push_start(name, bufs, ncopies, plan, after):
    nb = len(bufs)

    def body(*refs):
        send, recv, token = refs[nb + 1], refs[nb + 2], refs[-1]
        for i, (src, dst, dev) in enumerate(plan(refs[:nb])):
            pltpu.make_async_remote_copy(src_ref=src, dst_ref=dst, send_sem=send.at[i], recv_sem=recv.at[i],
                                         device_id=dev, device_id_type=MESH).start()
        token[...] = jnp.zeros_like(token)

    outs = pl.pallas_call(
        body, name=name,
        out_shape=(pltpu.SemaphoreType.DMA((ncopies,)), pltpu.SemaphoreType.DMA((ncopies,)), *[pltpu.HBM(b.shape, b.dtype) for b in bufs],
                   _sds((SUBLANES, LANES), F32)),
        in_specs=[HBM] * nb + [ANY], out_specs=(SEM, SEM, *([HBM] * nb), pl.BlockSpec(memory_space=pltpu.VMEM)),
        input_output_aliases={i: 2 + i for i in range(nb)},
        compiler_params=pltpu.CompilerParams(has_side_effects=pltpu.SideEffectType.DATAFLOW_SIDE_EFFECTING),
    )(*[pltpu.with_memory_space_constraint(b, pltpu.HBM) for b in bufs], after)
    return outs[0], outs[1], list(outs[2:2 + nb]), outs[-1]


def _push_wait(name, send, recv, bufs, plan, after):
    nb = len(bufs)

    def body(*refs):
        send_ref, recv_ref = refs[nb], refs[nb + 1]
        for i, (src, dst, dev) in enumerate(plan(refs[:nb])):
            cp = pltpu.make_async_remote_copy(src_ref=src, dst_ref=dst, send_sem=send_ref.at[i], recv_sem=recv_ref.at[i],
                                              device_id=dev, device_id_type=MESH)
            cp.wait_send()
            cp.wait_recv()

    outs = pl.pallas_call(
        body, name=name, out_shape=tuple(pltpu.HBM(b.shape, b.dtype) for b in bufs),
        in_specs=[HBM] * nb + [SEM, SEM, ANY], out_specs=tuple([HBM] * nb), input_output_aliases={i: i for i in range(nb)},
        compiler_params=pltpu.CompilerParams(has_side_effects=pltpu.SideEffectType.DATAFLOW_SIDE_EFFECTING),
    )(*bufs, send, recv, after)
    return list(outs)


def _mesh_pos():
    return lax.axis_index("x"), lax.axis_index("y"), lax.axis_index("c")


def _chip_peers(x, y, c):
    return [((1 - x, y, c), 2 * (1 - x) + y), ((x, 1 - y, c), 2 * x + (1 - y)), ((1 - x, 1 - y, c), 2 * (1 - x) + (1 - y))]


def _gather_plan(n):
    def plan(refs):
        x, y, c = _mesh_pos()
        me = 2 * x + y
        return [(refs[k].at[me], refs[k].at[me], dev) for k in range(n) for dev, _ in _chip_peers(x, y, c)]
    return plan


def _gather_weights(fulls):
    n = len(fulls)

    def body(*refs):
        ins, outs = refs[:n], refs[n:2 * n]
        ici_send, ici_recv, d2d_send, d2d_recv = refs[2 * n:]
        x, y, c = _mesh_pos()
        me = 2 * x + y
        sibling = (x, y, 1 - c)
        peers = _chip_peers(x, y, c)

        def half(k, who):
            r2 = fulls[k].shape[1] // 2
            return pl.ds(pl.multiple_of(who * r2, 16), r2)

        first = []
        for k in range(n):
            for rel, (dev, _) in enumerate(peers):
                cp = pltpu.make_async_remote_copy(src_ref=ins[k].at[me, half(k, c), :], dst_ref=outs[k].at[me, half(k, c), :],
                                                  send_sem=ici_send.at[k * 3 + rel], recv_sem=ici_recv.at[k * 3 + rel],
                                                  device_id=dev, device_id_type=MESH)
                cp.start()
                first.append(cp)
        passed = []
        for k in range(n):
            for rel, (dev, chip) in enumerate(peers):
                blk = outs[k].at[chip, half(k, c), :]
                pltpu.make_async_remote_copy(src_ref=blk, dst_ref=blk, send_sem=ici_send.at[k * 3 + rel], recv_sem=ici_recv.at[k * 3 + rel],
                                             device_id=dev, device_id_type=MESH).wait_recv()
                cp = pltpu.make_async_remote_copy(src_ref=blk, dst_ref=blk, send_sem=d2d_send.at[k * 3 + rel], recv_sem=d2d_recv.at[k * 3 + rel],
                                                  device_id=sibling, device_id_type=MESH)
                cp.start()
                passed.append(cp)
        for k in range(n):
            for rel, (dev, chip) in enumerate(peers):
                blk = outs[k].at[chip, half(k, 1 - c), :]
                pltpu.make_async_remote_copy(src_ref=blk, dst_ref=blk, send_sem=d2d_send.at[k * 3 + rel], recv_sem=d2d_recv.at[k * 3 + rel],
                                             device_id=sibling, device_id_type=MESH).wait_recv()
        for cp in first + passed:
            cp.wait_send()

    return pl.pallas_call(
        body, name="gather_weights", out_shape=[_sds(f.shape, BF16) for f in fulls],
        in_specs=[ANY] * n, out_specs=[ANY] * n, input_output_aliases={k: k for k in range(n)},
        scratch_shapes=[pltpu.SemaphoreType.DMA((n * 3,))] * 4, compiler_params=_params())(*fulls)


def _resident(shape):
    return pl.BlockSpec(shape, lambda i: (0,) * len(shape), pipeline_mode=pl.Buffered(1))


FFN_CHUNK = 768


def _chunks(n, step):
    return [(c0, min(step, n - c0)) for c0 in range(0, n, step)]


def _ffn_fwd(x, g, wgt, wut, wd, name, tm=256):
    T, D = x.shape
    F = wd.shape[0]

    def body(x_ref, g_ref, wg_ref, wu_ref, wd_ref, xo_ref, h_ref, gate_ref, up_ref, act_ref, h_scr):
        xv = x_ref[...]
        xhat, _ = _rms_stats(xv)
        hb = (xhat * g_ref[...]).astype(BF16)
        h_scr[...] = hb
        h_ref[...] = hb
        acc = None
        for c0, cw in _chunks(F, FFN_CHUNK):
            h = h_scr[...]
            gate = _dot_nt(h, wg_ref[c0:c0 + cw, :])
            up = _dot_nt(h, wu_ref[c0:c0 + cw, :])
            act = ((gate * _sigmoid(gate)) * up).astype(BF16)
            gate_ref[:, c0:c0 + cw] = gate.astype(BF16)
            up_ref[:, c0:c0 + cw] = up.astype(BF16)
            act_ref[:, c0:c0 + cw] = act
            d = _dot(act, wd_ref[c0:c0 + cw, :])
            acc = d if acc is None else acc + d
        xo_ref[...] = xv + FFN_RES_WEIGHT * acc

    row = pl.BlockSpec((tm, D), lambda i: (i, 0))
    saved = pl.BlockSpec((tm, F), lambda i: (i, 0))
    return pl.pallas_call(
        body, name=name, grid=(T // tm,),
        in_specs=[row, pl.BlockSpec((1, D), lambda i: (0, 0)), _resident(wgt.shape), _resident(wut.shape), _resident(wd.shape)],
        out_specs=[row, row, saved, saved, saved],
        out_shape=[_sds((T, D), F32), _sds((T, D), BF16), _sds((T, F), BF16), _sds((T, F), BF16), _sds((T, F), BF16)],
        scratch_shapes=[pltpu.VMEM((tm, D), BF16)], compiler_params=_params(("parallel",)))(x, g, wgt, wut, wd)


def _ffn_dx(dxo, x, g, gate_s, up_s, wgt, wut, wd, name, tm=256):
    T, D = x.shape
    F = wd.shape[0]

    def body(dxo_ref, x_ref, g_ref, gate_ref, up_ref, wg_ref, wu_ref, wd_ref, dx_ref, dff_ref, dgate_ref, dup_ref, dg_ref, dff_scr):
        @pl.when(pl.program_id(0) == 0)
        def _():
            dg_ref[...] = jnp.zeros_like(dg_ref)

        d = (FFN_RES_WEIGHT * dxo_ref[...]).astype(BF16)
        dff_scr[...] = d
        dff_ref[...] = d
        dh = None
        for c0, cw in _chunks(F, FFN_CHUNK):
            da = _dot_nt(dff_scr[...], wd_ref[c0:c0 + cw, :])
            gate = gate_ref[:, c0:c0 + cw].astype(F32)
            up = up_ref[:, c0:c0 + cw].astype(F32)
            s = _sigmoid(gate)
            silu = gate * s
            dup = (da * silu).astype(BF16)
            dgate = (da * up * (s * (1.0 + gate * (1.0 - s)))).astype(BF16)
            dgate_ref[:, c0:c0 + cw] = dgate
            dup_ref[:, c0:c0 + cw] = dup
            t = _dot(dgate, wg_ref[c0:c0 + cw, :]) + _dot(dup, wu_ref[c0:c0 + cw, :])
            dh = t if dh is None else dh + t
        dxn, dg = _rms_bwd(dh, x_ref[...], g_ref[...])
        dg_ref[...] += dg
        dx_ref[...] = dxo_ref[...] + dxn

    row = pl.BlockSpec((tm, D), lambda i: (i, 0))
    saved = pl.BlockSpec((tm, F), lambda i: (i, 0))
    return pl.pallas_call(
        body, name=name, grid=(T // tm,),
        in_specs=[row, row, pl.BlockSpec((1, D), lambda i: (0, 0)), saved, saved, _resident(wgt.shape), _resident(wut.shape),
                  _resident(wd.shape)],
        out_specs=[row, row, saved, saved, pl.BlockSpec((SUBLANES, D), lambda i: (0, 0))],
        out_shape=[_sds((T, D), F32), _sds((T, D), BF16), _sds((T, F), BF16), _sds((T, F), BF16), _sds((SUBLANES, D), F32)],
        scratch_shapes=[pltpu.VMEM((tm, D), BF16)], compiler_params=_params(("arbitrary",)))(dxo, x, g, gate_s, up_s, wgt, wut, wd)


def _tn(a, b, mb, name, tk=2048, dep=None):
    T, M = a.shape
    N = b.shape[1]
    nt = T // tk

    def body(a_ref, b_ref, *refs):
        o_ref, ob_ref = refs[-2:]

        @pl.when(pl.program_id(1) == 0)
        def _():
            o_ref[...] = jnp.zeros_like(o_ref)

        o_ref[...] += _dot_tn(a_ref[...].astype(BF16), b_ref[...].astype(BF16))

        @pl.when(pl.program_id(1) == nt - 1)
        def _():
            ob_ref[...] = o_ref[...].astype(BF16)

    o_spec = pl.BlockSpec((mb, N), lambda g, t: (g, 0))
    return pl.pallas_call(
        body, name=name, grid=(M // mb, nt),
        in_specs=[pl.BlockSpec((tk, mb), lambda g, t: (t, g)), pl.BlockSpec((tk, N), lambda g, t: (t, 0))] + ([ANY] if dep is not None else []),
        out_specs=[o_spec, o_spec], out_shape=[_sds((M, N), F32), _sds((M, N), BF16)],
        compiler_params=_params(("parallel", "arbitrary")))(a, b, *([dep] if dep is not None else []))


def _rope_tables(pos_col, inv_freq):
    T = pos_col.shape[0]

    def body(p_ref, f_ref, c_ref, s_ref):
        ang = p_ref[...].astype(F32) * f_ref[...]
        lane = lax.broadcasted_iota(jnp.int32, ang.shape, 1)
        c_ref[...] = jnp.cos(ang)
        sn = jnp.sin(ang)
        s_ref[...] = jnp.where((lane % HEAD_DIM) < HEAD_DIM // 2, -sn, sn)

    tm = 1024
    return pl.pallas_call(
        body, name="rope_tables", grid=(T // tm,),
        in_specs=[pl.BlockSpec((tm, 1), lambda i: (i, 0)), pl.BlockSpec((1, LANES), lambda i: (0, 0))],
        out_specs=[pl.BlockSpec((tm, LANES), lambda i: (i, 0))] * 2,
        out_shape=[_sds((T, LANES), F32)] * 2, compiler_params=_params(("parallel",)))(pos_col, inv_freq)


def _deinterleave(scr, out_ref, d, tm, nblk):
    for r in range(d):
        for cb in range(nblk):
            out_ref[r, :, cb * LANES:(cb + 1) * LANES] = scr[cb, pl.ds(r, tm // d, stride=d), :].astype(out_ref.dtype)


def _interleave(in_ref, scr, d, tm, nblk):
    for r in range(d):
        for cb in range(nblk):
            scr[cb, pl.ds(r, tm // d, stride=d), :] = in_ref[r, :, cb * LANES:(cb + 1) * LANES].astype(F32)


def _proj_rope(x, g, w_in, cos, sin, tm=512):
    T, D = x.shape
    dils = [d for _, d in B_PATTERNS if d > 1]
    nbb = B_W // LANES
    scale = HEAD_DIM ** -0.5
    cuts = [0, A_Q_W, A_Q_W + A_KV_W, A_Q_W + 2 * A_KV_W, A_Q_W + 2 * A_KV_W + B_W, A_Q_W + 2 * A_KV_W + 2 * B_W,
            A_Q_W + 2 * A_KV_W + 3 * B_W]

    def body(x_ref, g_ref, w_ref, c_ref, s_ref, h_ref, aq_ref, ak_ref, av_ref, *rest):
        b_refs, scr = rest[:-1], rest[-1]
        xhat, _ = _rms_stats(x_ref[...])
        h = (xhat * g_ref[...]).astype(BF16)
        h_ref[...] = h
        cs, sn = c_ref[...], s_ref[...]

        def seg(idx, rope, mult):
            lo, hi = cuts[idx], cuts[idx + 1]
            blocks = []
            whole = _dot_nt(h, w_ref[lo:hi, :])
            for cb in range((hi - lo) // LANES):
                p = whole[:, cb * LANES:(cb + 1) * LANES]
                if rope:
                    p = p * cs + _swap32(p) * sn
                if mult != 1.0:
                    p = p * mult
                blocks.append(p)
            return blocks

        for idx, ref, rope, mult in ((0, aq_ref, True, scale), (1, ak_ref, True, 1.0), (2, av_ref, False, 1.0)):
            for cb, p in enumerate(seg(idx, rope, mult)):
                ref[:, cb * LANES:(cb + 1) * LANES] = p.astype(BF16)
        for which, (idx, rope, mult) in enumerate(((3, True, scale), (4, True, 1.0), (5, False, 1.0))):
            for cb, p in enumerate(seg(idx, rope, mult)):
                b_refs[which][:, cb * LANES:(cb + 1) * LANES] = p.astype(BF16)
                scr[cb] = p
            for di, d in enumerate(dils):
                _deinterleave(scr, b_refs[3 * (di + 1) + which], d, tm, nbb)

    row = lambda w: pl.BlockSpec((tm, w), lambda i: (i, 0))
    out_specs = [row(D), row(A_Q_W), row(A_KV_W), row(A_KV_W)] + [row(B_W)] * 3
    out_shape = [_sds((T, D), BF16), _sds((T, A_Q_W), BF16), _sds((T, A_KV_W), BF16), _sds((T, A_KV_W), BF16)] + [_sds((T, B_W), BF16)] * 3
    for d in dils:
        out_specs += [pl.BlockSpec((d, tm // d, B_W), lambda i: (0, i, 0))] * 3
        out_shape += [_sds((d, T // d, B_W), BF16)] * 3
    return pl.pallas_call(
        body, name="proj_rope", grid=(T // tm,),
        in_specs=[row(D), pl.BlockSpec((1, D), lambda i: (0, 0)), pl.BlockSpec(w_in.shape, lambda i: (0, 0)), row(LANES), row(LANES)],
        out_specs=out_specs, out_shape=out_shape, scratch_shapes=[pltpu.VMEM((nbb, tm, LANES), F32)],
        compiler_params=_params(("parallel",)))(x, g, w_in, cos, sin)


def _band_bias(rel, qb, kw, hw):
    ri = lax.broadcasted_iota(jnp.int32, (2 * qb, kw), 0) & (qb - 1)
    ci = lax.broadcasted_iota(jnp.int32, (2 * qb, kw), 1)
    return jnp.where(jnp.abs(ri + rel - ci) <= hw, 0.0, NEG).astype(F32)


def _stack_heads(x, lo):
    z = jnp.zeros_like(x)
    return jnp.concatenate([jnp.where(lo, x, z), jnp.where(lo, z, x)], axis=0)


def _unstack_heads(y, lo):
    qb = y.shape[0] // 2
    return jnp.where(lo, y[:qb], y[qb:])


def _band_setup(bias_scr, qb, kw, hw):
    if bias_scr is not None:
        for i in range(3):
            bias_scr[i] = _band_bias(i * hw, qb, kw, hw)


def _band_window(bias_scr, qs, L, qb, kw, hw):
    ws = pl.multiple_of(jnp.clip(qs - hw, 0, L - kw), 64)
    if bias_scr is None:
        return ws, _band_bias(qs - ws, qb, kw, hw)
    return ws, bias_scr[lax.shift_right_logical(qs - ws, hw.bit_length() - 1)]


def _dup_kv_head(src_ref, dst_ref, head, L):
    step = min(L, 1024)
    for r0 in range(0, L, step):
        xf = src_ref[r0:r0 + step, :].astype(F32)
        lane = lax.broadcasted_iota(jnp.int32, xf.shape, 1)
        keep = jnp.logical_xor(lane < HEAD_DIM, head == 1)
        dst_ref[r0:r0 + step, :] = jnp.where(keep, xf, pltpu.roll(xf, HEAD_DIM, axis=1)).astype(dst_ref.dtype)


def _attn_fwd(q, k, v, sink, hw, gqa, out_dtype, name, qb=QB, blocks_per_step=8):
    NB, L, Cq = q.shape
    Ls = min(L, 2048)
    kw = min(qb + 2 * hw, L)
    tables = L >= qb + 2 * hw
    unroll = min(blocks_per_step, Ls // qb)
    nlb = 1 if (gqa or L > SHORT_SEQ) else Cq // LANES

    def body(sink_ref, q_ref, k_ref, v_ref, o_ref, lse_ref, *scr):
        b, s_idx = pl.program_id(1), pl.program_id(2)
        bias_scr = scr[0] if tables else None
        _band_setup(bias_scr, qb, kw, hw)
        if gqa:
            kd, vd = scr[-2:]

            @pl.when(s_idx == 0)
            def _():
                _dup_kv_head(k_ref, kd, b // 2, L)
                _dup_kv_head(v_ref, vd, b // 2, L)
        else:
            kd, vd = k_ref, v_ref
        lane = lax.broadcasted_iota(jnp.int32, (qb, LANES), 1)
        lo = lane < HEAD_DIM
        if gqa:
            row = lax.broadcasted_iota(jnp.int32, (2 * qb, 1), 0)
            sk = jnp.where(row < qb, sink_ref[2 * b], sink_ref[2 * b + 1])

        def block(ql, col):
            qs = s_idx * Ls + ql
            ws, bias = _band_window(bias_scr, qs, L, qb, kw, hw)
            kv_, vv = kd[pl.ds(ws, kw), col], vd[pl.ds(ws, kw), col]
            s = _dot_nt(_stack_heads(q_ref[pl.ds(ql, qb), col], lo), kv_) + bias
            m = jnp.max(s, axis=-1, keepdims=True)
            if gqa:
                m = jnp.maximum(m, sk)
            p = jnp.exp(s - m)
            den = jnp.sum(p, axis=-1, keepdims=True)
            if gqa:
                den = den + jnp.exp(sk - m)
            o_ref[pl.ds(ql, qb), col] = _unstack_heads(_dot(p.astype(BF16), vv) * (1.0 / den), lo).astype(o_ref.dtype)
            lse_ref[pl.ds(ql, qb), col] = _unstack_heads(m + jnp.log(den), lo)

        for lb in range(nlb):
            def step(n, carry, col=slice(lb * LANES, (lb + 1) * LANES)):
                for u in range(unroll):
                    block(pl.multiple_of((n * unroll + u) * qb, qb), col)
                return carry

            lax.fori_loop(0, Ls // (qb * unroll), step, 0)

    kv_map = (lambda r, b, s: (r, 0, 0)) if gqa else (lambda r, b, s: (r, 0, b))
    seg = pl.BlockSpec((None, Ls, nlb * LANES), lambda r, b, s: (r, s, b))
    return pl.pallas_call(
        body, name=name, grid=(NB, Cq // (nlb * LANES), L // Ls),
        in_specs=[pl.BlockSpec(memory_space=pltpu.SMEM), seg, pl.BlockSpec((None, L, nlb * LANES), kv_map),
                  pl.BlockSpec((None, L, nlb * LANES), kv_map)],
        out_specs=[seg, seg], out_shape=[_sds((NB, L, Cq), out_dtype), _sds((NB, L, Cq), F32)],
        scratch_shapes=([pltpu.VMEM((3, 2 * qb, kw), F32)] if tables else []) + ([pltpu.VMEM((L, LANES), BF16)] * 2 if gqa else []),
        compiler_params=_params(("parallel", "parallel", "arbitrary")))(sink, q, k, v)


def _attn_bwd(q, k, v, do, lse, delta, sink, hw, gqa, name, qb=QB, blocks_per_step=8):
    NB, L, Cq = q.shape
    Ck = k.shape[2]
    Ls = min(L, 2048)
    kw = min(qb + 2 * hw, L)
    reps = kw // LANES
    nseg = L // Ls
    scale = HEAD_DIM ** -0.5
    tables = L >= qb + 2 * hw
    unroll = min(blocks_per_step, Ls // qb)
    nlb = 1 if (gqa or L > SHORT_SEQ) else Cq // LANES

    def body(sink_ref, q_ref, do_ref, lse_ref, dl_ref, k_ref, v_ref, dq_ref, dk_ref, dv_ref, dsk_ref, *scr):
        b, s_idx = pl.program_id(1), pl.program_id(2)
        lane = lax.broadcasted_iota(jnp.int32, (qb, LANES), 1)
        lo = lane < HEAD_DIM
        bias_scr = scr[0] if tables else None
        _band_setup(bias_scr, qb, kw, hw)
        if gqa:
            kd, vd, dk_acc, dv_acc, dsk_acc = scr[-5:]

            @pl.when(s_idx == 0)
            def _():
                _dup_kv_head(k_ref, kd, b // 2, L)
                _dup_kv_head(v_ref, vd, b // 2, L)
                dk_acc[...] = jnp.zeros_like(dk_acc)
                dv_acc[...] = jnp.zeros_like(dv_acc)
                dsk_acc[...] = jnp.zeros_like(dsk_acc)

            @pl.when((s_idx == 0) & (b == 0))
            def _():
                dk_ref[...] = jnp.zeros_like(dk_ref)
                dv_ref[...] = jnp.zeros_like(dv_ref)
        else:
            kd, vd = k_ref, v_ref
            dk_acc, dv_acc = scr[-2:]

            @pl.when(s_idx == 0)
            def _():
                dk_acc[...] = jnp.zeros_like(dk_acc)
                dv_acc[...] = jnp.zeros_like(dv_acc)

        def block(ql, col):
            qs = s_idx * Ls + ql
            ws, bias = _band_window(bias_scr, qs, L, qb, kw, hw)
            qv, dov = q_ref[pl.ds(ql, qb), col], do_ref[pl.ds(ql, qb), col]
            lse, dl = lse_ref[pl.ds(ql, qb), col], dl_ref[pl.ds(ql, qb), col]
            kv_, vv = kd[pl.ds(ws, kw), col], vd[pl.ds(ws, kw), col]
            q2, do2 = _stack_heads(qv, lo), _stack_heads(dov, lo)
            lse_sw, dl_sw = pltpu.roll(lse, HEAD_DIM, axis=1), pltpu.roll(dl, HEAD_DIM, axis=1)
            lse2 = jnp.concatenate([jnp.where(lo, lse, lse_sw), jnp.where(lo, lse_sw, lse)], axis=0)
            dl2 = jnp.concatenate([jnp.where(lo, dl, dl_sw), jnp.where(lo, dl_sw, dl)], axis=0)
            p = jnp.exp(_dot_nt(q2, kv_) + bias - jnp.tile(lse2, (1, reps)))
            ds = (p * (_dot_nt(do2, vv) - jnp.tile(dl2, (1, reps)))).astype(BF16)
            dq_ref[pl.ds(ql, qb), col] = (_unstack_heads(_dot(ds, kv_), lo) * scale).astype(dq_ref.dtype)
            both = _dot_tn(jnp.concatenate([ds, p.astype(BF16)], axis=1), jnp.concatenate([q2, do2], axis=1))
            dk_acc[pl.ds(ws, kw), col] += both[:kw, :LANES]
            dv_acc[pl.ds(ws, kw), col] += both[kw:, LANES:]
            if gqa:
                sk = jnp.where(lo, sink_ref[2 * b], sink_ref[2 * b + 1])
                dsk_acc[...] += -jnp.exp(sk - lse) * dl

        for lb in range(nlb):
            def step(n, carry, col=slice(lb * LANES, (lb + 1) * LANES)):
                for u in range(unroll):
                    block(pl.multiple_of((n * unroll + u) * qb, qb), col)
                return carry

            lax.fori_loop(0, Ls // (qb * unroll), step, 0)

        if gqa:
            @pl.when(s_idx == nseg - 1)
            def _():
                step_rows = min(L, 1024)
                for r0 in range(0, L, step_rows):
                    lanek = lax.broadcasted_iota(jnp.int32, (step_rows, LANES), 1)
                    mine = jnp.logical_xor(lanek < HEAD_DIM, (b // 2) == 1)
                    for acc, ref in ((dk_acc, dk_ref), (dv_acc, dv_ref)):
                        a = acc[r0:r0 + step_rows, :]
                        ref[r0:r0 + step_rows, :] += jnp.where(mine, a + pltpu.roll(a, HEAD_DIM, axis=1), 0.0)
                dsk_ref[...] = dsk_acc[...].reshape(qb // SUBLANES, SUBLANES, LANES).sum(axis=0)
        else:
            dsk_ref[...] = jnp.zeros_like(dsk_ref)

            @pl.when(s_idx == nseg - 1)
            def _():
                dk_ref[...] = dk_acc[...].astype(dk_ref.dtype)
                dv_ref[...] = dv_acc[...].astype(dv_ref.dtype)

    kv_map = (lambda r, b, s: (r, 0, 0)) if gqa else (lambda r, b, s: (r, 0, b))
    seg = pl.BlockSpec((None, Ls, nlb * LANES), lambda r, b, s: (r, s, b))
    full = pl.BlockSpec((None, L, nlb * LANES), kv_map)
    scratch = [pltpu.VMEM((3, 2 * qb, kw), F32)] if tables else []
    if gqa:
        scratch += [pltpu.VMEM((L, LANES), BF16)] * 2 + [pltpu.VMEM((L, LANES), F32)] * 2 + [pltpu.VMEM((qb, LANES), F32)]
    else:
        scratch += [pltpu.VMEM((L, nlb * LANES), F32)] * 2
    kv_dtype = F32 if gqa else BF16
    return pl.pallas_call(
        body, name=name, grid=(NB, Cq // (nlb * LANES), nseg),
        in_specs=[pl.BlockSpec(memory_space=pltpu.SMEM), seg, seg, seg, seg, full, full],
        out_specs=[seg, full, full, pl.BlockSpec((None, None, SUBLANES, LANES), lambda r, b, s: (r, b, 0, 0))],
        out_shape=[_sds((NB, L, Cq), BF16), _sds((NB, L, Ck), kv_dtype), _sds((NB, L, Ck), kv_dtype),
                   _sds((NB, Cq // LANES, SUBLANES, LANES), F32)],
        scratch_shapes=scratch,
        compiler_params=_params(("arbitrary", "arbitrary", "arbitrary")))(sink, q, do, lse, delta, k, v)


def _merge_b(a_out, o1, l1, o4, l4, o16, l16, tm=512):
    T = a_out.shape[0]
    nbb = B_W // LANES

    def body(a_ref, o1_ref, l1_ref, o4_ref, l4_ref, o16_ref, l16_ref, cat_ref, lg1_ref, lg4_ref, lg16_ref, so, sl, slg):
        _interleave(o4_ref, so.at[0], 4, tm, nbb)
        _interleave(l4_ref, sl.at[0], 4, tm, nbb)
        _interleave(o16_ref, so.at[1], 16, tm, nbb)
        _interleave(l16_ref, sl.at[1], 16, tm, nbb)
        cat_ref[:, 0:A_Q_W] = a_ref[...]
        for cb in range(nbb):
            cols = slice(cb * LANES, (cb + 1) * LANES)
            os_ = (o1_ref[:, cols], so[0, cb], so[1, cb])
            ls_ = (l1_ref[:, cols], sl[0, cb], sl[1, cb])
            m = jnp.maximum(jnp.maximum(ls_[0], ls_[1]), ls_[2])
            es = [jnp.exp(l - m) for l in ls_]
            den = es[0] + es[1] + es[2]
            out = (es[0] * os_[0] + es[1] * os_[1] + es[2] * os_[2]) * (1.0 / den)
            lg = m + jnp.log(den)
            cat_ref[:, A_Q_W + cb * LANES:A_Q_W + (cb + 1) * LANES] = out.astype(BF16)
            lg1_ref[:, cols] = lg
            slg[cb] = lg
        _deinterleave(slg, lg4_ref, 4, tm, nbb)
        _deinterleave(slg, lg16_ref, 16, tm, nbb)

    row = lambda w: pl.BlockSpec((tm, w), lambda i: (i, 0))
    perm = lambda d: pl.BlockSpec((d, tm // d, B_W), lambda i: (0, i, 0))
    return pl.pallas_call(
        body, name="merge_patterns", grid=(T // tm,),
        in_specs=[row(A_Q_W), row(B_W), row(B_W), perm(4), perm(4), perm(16), perm(16)],
        out_specs=[row(A_Q_W + B_W), row(B_W), perm(4), perm(16)],
        out_shape=[_sds((T, A_Q_W + B_W), BF16), _sds((T, B_W), F32), _sds((4, T // 4, B_W), F32), _sds((16, T // 16, B_W), F32)],
        scratch_shapes=[pltpu.VMEM((2, nbb, tm, LANES), F32), pltpu.VMEM((2, nbb, tm, LANES), F32), pltpu.VMEM((nbb, tm, LANES), F32)],
        compiler_params=_params(("parallel",)))(a_out, o1, l1, o4, l4, o16, l16)


def _out_proj(x, cat, w_out, tm=512):
    T, D = x.shape

    def body(x_ref, c_ref, w_ref, o_ref):
        o_ref[...] = x_ref[...] + _dot(c_ref[...], w_ref[...])

    row = lambda w: pl.BlockSpec((tm, w), lambda i: (i, 0))
    return pl.pallas_call(
        body, name="out_proj", grid=(T // tm,), in_specs=[row(D), row(cat.shape[1]), pl.BlockSpec(w_out.shape, lambda i: (0, 0))],
        out_specs=row(D), out_shape=_sds((T, D), F32), compiler_params=_params(("parallel",)))(x, cat, w_out)


def _final_loss(x, g, target, tm=512):
    T, D = x.shape

    def body(x_ref, g_ref, t_ref, dx_ref, dg_ref, loss_ref):
        @pl.when(pl.program_id(0) == 0)
        def _():
            dg_ref[...] = jnp.zeros_like(dg_ref)
            loss_ref[...] = jnp.zeros_like(loss_ref)

        xv, gv = x_ref[...], g_ref[...]
        xhat, _ = _rms_stats(xv)
        err = xhat * gv - t_ref[...]
        loss_ref[...] += 0.5 * jnp.sum(jnp.sum(err * err, axis=-1, keepdims=True) * (1.0 / D), axis=0, keepdims=True)
        dx, dg = _rms_bwd(err * (1.0 / D), xv, gv)
        dx_ref[...] = dx
        dg_ref[...] += dg

    row = pl.BlockSpec((tm, D), lambda i: (i, 0))
    return pl.pallas_call(
        body, name="final_loss", grid=(T // tm,), in_specs=[row, pl.BlockSpec((1, D), lambda i: (0, 0)), row],
        out_specs=[row, pl.BlockSpec((SUBLANES, D), lambda i: (0, 0)), pl.BlockSpec((SUBLANES, LANES), lambda i: (0, 0))],
        out_shape=[_sds((T, D), F32), _sds((SUBLANES, D), F32), _sds((SUBLANES, LANES), F32)],
        compiler_params=_params(("arbitrary",)))(x, g, target)


def _dcat(dx, w_out, cat, tm=512):
    T, D = dx.shape
    C = cat.shape[1]
    nba, nbb = A_Q_W // LANES, B_W // LANES

    def body(dx_ref, w_ref, cat_ref, doa_ref, dla_ref, dob1_ref, dlb1_ref, dob4_ref, dlb4_ref, dob16_ref, dlb16_ref, sdo, sdl):
        dc = _dot_nt(dx_ref[...].astype(BF16), w_ref[...])
        ri = lax.broadcasted_iota(jnp.int32, (LANES, LANES), 0)
        ci = lax.broadcasted_iota(jnp.int32, (LANES, LANES), 1)
        same_head = ((ri // HEAD_DIM) == (ci // HEAD_DIM)).astype(BF16)
        for cb in range(C // LANES):
            cols = slice(cb * LANES, (cb + 1) * LANES)
            blk = dc[:, cols]
            prod = blk * cat_ref[:, cols].astype(F32)
            hi = prod.astype(BF16)
            lo_ = (prod - hi.astype(F32)).astype(BF16)
            dl = _dot(hi, same_head) + _dot(lo_, same_head)
            if cb < nba:
                doa_ref[:, cols] = blk.astype(BF16)
                dla_ref[:, cols] = dl
            else:
                bcols = slice((cb - nba) * LANES, (cb - nba + 1) * LANES)
                dob1_ref[:, bcols] = blk.astype(BF16)
                dlb1_ref[:, bcols] = dl
                sdo[cb - nba] = blk
                sdl[cb - nba] = dl
        _deinterleave(sdo, dob4_ref, 4, tm, nbb)
        _deinterleave(sdl, dlb4_ref, 4, tm, nbb)
        _deinterleave(sdo, dob16_ref, 16, tm, nbb)
        _deinterleave(sdl, dlb16_ref, 16, tm, nbb)

    row = lambda w: pl.BlockSpec((tm, w), lambda i: (i, 0))
    perm = lambda d: pl.BlockSpec((d, tm // d, B_W), lambda i: (0, i, 0))
    return pl.pallas_call(
        body, name="dcat", grid=(T // tm,), in_specs=[row(D), pl.BlockSpec(w_out.shape, lambda i: (0, 0)), row(C)],
        out_specs=[row(A_Q_W), row(A_Q_W), row(B_W), row(B_W), perm(4), perm(4), perm(16), perm(16)],
        out_shape=[_sds((T, A_Q_W), BF16), _sds((T, A_Q_W), F32), _sds((T, B_W), BF16), _sds((T, B_W), F32),
                   _sds((4, T // 4, B_W), BF16), _sds((4, T // 4, B_W), F32), _sds((16, T // 16, B_W), BF16), _sds((16, T // 16, B_W), F32)],
        scratch_shapes=[pltpu.VMEM((nbb, tm, LANES), F32)] * 2, compiler_params=_params(("parallel",)))(dx, w_out, cat)


def _rope_bwd_assemble(dqa, dka, dva, b1, b4, b16, cos, sin, tm=512):
    T = dqa.shape[0]
    nbb = B_W // LANES
    width = A_Q_W + 2 * A_KV_W + 3 * B_W

    def body(dqa_ref, dka_ref, dva_ref, q1, k1, v1, q4, k4, v4, q16, k16, v16, c_ref, s_ref, o_ref, scr):
        cs, sn = c_ref[...], s_ref[...]

        def unrope(t):
            return t * cs + _swap32(t * sn)

        col = 0
        for ref, rope in ((dqa_ref, True), (dka_ref, True), (dva_ref, False)):
            for cb in range(ref.shape[1] // LANES):
                t = ref[:, cb * LANES:(cb + 1) * LANES].astype(F32)
                o_ref[:, col:col + LANES] = (unrope(t) if rope else t).astype(BF16)
                col += LANES
        for which, (r1, r4, r16, rope) in enumerate(((q1, q4, q16, True), (k1, k4, k16, True), (v1, v4, v16, False))):
            _interleave(r4, scr.at[0], 4, tm, nbb)
            _interleave(r16, scr.at[1], 16, tm, nbb)
            for cb in range(nbb):
                t = r1[:, cb * LANES:(cb + 1) * LANES].astype(F32) + scr[0, cb] + scr[1, cb]
                o_ref[:, col:col + LANES] = (unrope(t) if rope else t).astype(BF16)
                col += LANES

    row = lambda w: pl.BlockSpec((tm, w), lambda i: (i, 0))
    perm = lambda d: pl.BlockSpec((d, tm // d, B_W), lambda i: (0, i, 0))
    return pl.pallas_call(
        body, name="rope_bwd", grid=(T // tm,),
        in_specs=[row(A_Q_W), row(A_KV_W), row(A_KV_W)] + [row(B_W)] * 3 + [perm(4)] * 3 + [perm(16)] * 3 + [row(LANES), row(LANES)],
        out_specs=row(width), out_shape=_sds((T, width), BF16), scratch_shapes=[pltpu.VMEM((2, nbb, tm, LANES), F32)],
        compiler_params=_params(("parallel",)))(dqa, dka, dva, *b1, *b4, *b16, cos, sin)


def _dh_norm(dproj, w_in, x, g, dres, tm=512):
    T, D = x.shape

    def body(dp_ref, w_ref, x_ref, g_ref, dr_ref, dx_ref, dg_ref):
        @pl.when(pl.program_id(0) == 0)
        def _():
            dg_ref[...] = jnp.zeros_like(dg_ref)

        dxn, dg = _rms_bwd(_dot(dp_ref[...], w_ref[...]), x_ref[...], g_ref[...])
        dg_ref[...] += dg
        dx_ref[...] = dr_ref[...] + dxn

    row = lambda w: pl.BlockSpec((tm, w), lambda i: (i, 0))
    return pl.pallas_call(
        body, name="dh_norm", grid=(T // tm,),
        in_specs=[row(dproj.shape[1]), pl.BlockSpec(w_in.shape, lambda i: (0, 0)), row(D), pl.BlockSpec((1, D), lambda i: (0, 0)), row(D)],
        out_specs=[row(D), pl.BlockSpec((SUBLANES, D), lambda i: (0, 0))],
        out_shape=[_sds((T, D), F32), _sds((SUBLANES, D), F32)], compiler_params=_params(("arbitrary",)))(dproj, w_in, x, g, dres)


def _grad_push_plan(n):
    def plan(refs):
        x, y, c = _mesh_pos()
        return [(refs[k].at[chip], refs[n + k].at[rel], dev) for k in range(n) for rel, (dev, chip) in enumerate(_chip_peers(x, y, c))]
    return plan


def _sum_own(me_arr, g, landed, name):
    ns, R, C = g.shape
    tr = R // 2 if (R // 2) % 16 == 0 else R

    def body(me_ref, g_ref, x_ref, o_ref):
        acc = g_ref[...]
        for rel in range(ns - 1):
            acc = acc + x_ref[rel].astype(F32)
        o_ref[...] = acc

    grid_spec = pltpu.PrefetchScalarGridSpec(
        num_scalar_prefetch=1, grid=(R // tr,),
        in_specs=[pl.BlockSpec((None, tr, C), lambda t, me: (me[0], t, 0)), pl.BlockSpec((ns - 1, tr, C), lambda t, me: (0, t, 0))],
        out_specs=pl.BlockSpec((tr, C), lambda t, me: (t, 0)))
    return pl.pallas_call(body, name=name, grid_spec=grid_spec, out_shape=_sds((R, C), F32),
                          compiler_params=_params(("parallel",)))(me_arr, g, landed)


def _swap_plan(n):
    def plan(refs):
        x, y, c = _mesh_pos()
        return [(refs[k], refs[n + k], (x, y, 1 - c)) for k in range(n)]
    return plan


def _allreduce_small(v):
    rows, W = v.shape

    def body(v_ref, o_ref, buf, send, recv):
        x, y, c = _mesh_pos()
        me = 4 * x + 2 * y + c
        cps = []
        for m in range(1, N_DEV):
            dev = (x ^ (m >> 2), y ^ ((m >> 1) & 1), c ^ (m & 1))
            cp = pltpu.make_async_remote_copy(src_ref=v_ref, dst_ref=buf.at[me], send_sem=send.at[m - 1], recv_sem=recv.at[m - 1],
                                              device_id=dev, device_id_type=MESH)
            cp.start()
            cps.append(cp)
        for m in range(1, N_DEV):
            pltpu.make_async_remote_copy(src_ref=v_ref, dst_ref=buf.at[me ^ m], send_sem=send.at[m - 1], recv_sem=recv.at[m - 1],
                                         device_id=(x, y, c), device_id_type=MESH).wait_recv()
        for cp in cps:
            cp.wait_send()
        buf[me] = v_ref[...]
        acc = buf[0]
        for i in range(1, N_DEV):
            acc = acc + buf[i]
        o_ref[...] = acc

    return pl.pallas_call(
        body, name="allreduce_small", out_shape=_sds((rows, W), F32),
        scratch_shapes=[pltpu.VMEM((N_DEV, rows, W), F32), pltpu.SemaphoreType.DMA((N_DEV - 1,)), pltpu.SemaphoreType.DMA((N_DEV - 1,))],
        compiler_params=_params())(v)


def _adamw(w, gp, gq, m, v, name):
    R, C = w.shape
    tr = R // 2 if (R // 2) % SUBLANES == 0 else R
    c1 = 1.0 / (1.0 - ADAM_B1 ** ADAM_STEP)
    c2 = 1.0 / (1.0 - ADAM_B2 ** ADAM_STEP)

    def body(w_ref, gp_ref, gq_ref, m_ref, v_ref, g_ref, d_ref, nm_ref, nv_ref):
        gv = gp_ref[...] + gq_ref[...]
        nm = ADAM_B1 * m_ref[...] + (1.0 - ADAM_B1) * gv
        nv = ADAM_B2 * v_ref[...] + (1.0 - ADAM_B2) * (gv * gv)
        g_ref[...] = gv
        d_ref[...] = -ADAM_LR * ((nm * c1) / (jnp.sqrt(nv * c2) + ADAM_EPS) + ADAM_WD * w_ref[...])
        nm_ref[...] = nm
        nv_ref[...] = nv

    blk = pl.BlockSpec((tr, C), lambda t: (t, 0))
    return pl.pallas_call(body, name=name, grid=(R // tr,), in_specs=[blk] * 5, out_specs=[blk] * 4,
                          out_shape=[_sds((R, C), F32)] * 4, compiler_params=_params(("parallel",)))(w, gp, gq, m, v)


def _local_step(x, positions, target, norms, a_sink, comm):
    T, D = x.shape
    g1, gm, g2, gf = norms
    inv_freq = 1.0 / (ROPE_THETA ** (jnp.arange(0, HEAD_DIM, 2, dtype=F32) / HEAD_DIM))
    inv_freq = jnp.tile(inv_freq, LANES // (HEAD_DIM // 2)).reshape(1, LANES)
    cos, sin = _rope_tables(positions.reshape(T, 1), inv_freq)
    no_sink = jnp.zeros((2 * (B_W // LANES),), F32)
    W = {k: comm.weight(k, x) for k in ("wg1", "wu1", "wd1")}

    x1, h1, gate1, up1, act1 = _ffn_fwd(x, comm.order(g1), W["wg1"], W["wu1"], W["wd1"], "ffn1_fwd")
    W["w_in"] = comm.weight("w_in", x1)
    (h2, aq, ak, av, bq1, bk1, bv1, bq4, bk4, bv4, bq16, bk16, bv16) = _proj_rope(x1, gm, W["w_in"], cos, sin)
    a_out, a_lse = _attn_fwd(aq[None], ak[None], av[None], a_sink, A_HALF_WINDOW, True, BF16, "attn_a_fwd", qb=2 * QB, blocks_per_step=4)
    bqs = {1: (bq1[None], bk1[None], bv1[None]), 4: (bq4, bk4, bv4), 16: (bq16, bk16, bv16)}
    b_o, b_l = {}, {}
    for w, d in B_PATTERNS:
        q_, k_, v_ = bqs[d]
        b_o[d], b_l[d] = _attn_fwd(q_, k_, v_, no_sink, w // (2 * d), False, BF16, f"attn_b{d}_fwd")
    cat, lg1, lg4, lg16 = _merge_b(a_out[0], b_o[1][0], b_l[1][0], b_o[4], b_l[4], b_o[16], b_l[16])
    W["w_out"] = comm.weight("w_out", cat)
    x2 = _out_proj(x1, cat, W["w_out"])
    for k in ("wg2", "wu2", "wd2"):
        W[k] = comm.weight(k, x2)
    x3, h3, gate2, up2, act2 = _ffn_fwd(x2, g2, W["wg2"], W["wu2"], W["wd2"], "ffn2_fwd")

    dx3, dgf, loss8 = _final_loss(x3, gf, target)
    dx2, dff2, dgate2, dup2, dg2 = _ffn_dx(dx3, x2, g2, gate2, up2, W["wg2"], W["wu2"], W["wd2"], "ffn2_dx")
    fb = gate2.shape[1] // 2
    dwg2 = _tn(dgate2, h3, fb, "ffn2_dw_gate")
    dwu2 = _tn(dup2, h3, fb, "ffn2_dw_up")
    dwd2 = _tn(act2, dff2, fb, "ffn2_dw_down")
    comm.ready(dict(wg2=dwg2, wu2=dwu2, wd2=dwd2), dwd2[0])

    doa, dla, dob1, dlb1, dob4, dlb4, dob16, dlb16 = _dcat(dx2, W["w_out"], cat)
    dw_out = _tn(cat, dx2, cat.shape[1], "w_out_dw", dep=comm.dep())
    dqa, dka, dva, dsk = _attn_bwd(aq[None], ak[None], av[None], doa[None], a_lse, dla[None], comm.order(a_sink), A_HALF_WINDOW, True,
                                   "attn_a_bwd")
    bwd_in = {1: (dob1[None], lg1[None], dlb1[None]), 4: (dob4, lg4, dlb4), 16: (dob16, lg16, dlb16)}
    bg = {}
    for w, d in B_PATTERNS:
        q_, k_, v_ = bqs[d]
        do_, l_, dl_ = bwd_in[d]
        bg[d] = _attn_bwd(q_, k_, v_, do_, l_, dl_, no_sink, w // (2 * d), False, f"attn_b{d}_bwd")[:3]
    dproj = _rope_bwd_assemble(dqa[0], dka[0], dva[0], [t[0] for t in bg[1]], bg[4], bg[16], cos, sin)
    dw_in = _tn(dproj, h2, dproj.shape[1] // 2, "w_in_dw")
    comm.ready(dict(w_in=dw_in, w_out=dw_out), dw_in[0])
    dx1, dgm = _dh_norm(dproj, W["w_in"], x1, comm.order(gm), dx2)

    dx0, dff1, dgate1, dup1, dg1 = _ffn_dx(dx1, x, g1, gate1, up1, W["wg1"], W["wu1"], W["wd1"], "ffn1_dx")
    dwd1 = _tn(act1, dff1, fb, "ffn1_dw_down")
    comm.ready(dict(wd1=dwd1), dwd1[0])
    dwg1 = _tn(dgate1, h1, fb, "ffn1_dw_gate", dep=comm.dep())
    comm.ready(dict(wg1=dwg1), dwg1[0])
    dwu1 = _tn(dup1, h1, fb, "ffn1_dw_up", dep=comm.dep())
    comm.ready(dict(wu1=dwu1), dwu1[0])

    dsink = dsk[0, :, :, ::HEAD_DIM].sum(axis=1).reshape(-1)
    small = dict(g1=dg1.sum(axis=0), gm=dgm.sum(axis=0), g2=dg2.sum(axis=0), gf=dgf.sum(axis=0), sink=dsink, loss=loss8[0, 0])
    return dx0, small


BIG = ("wg1", "wu1", "wd1", "w_in", "w_out", "wg2", "wu2", "wd2")
GATHER_GROUPS = (("w_in",), ("w_out",), ("wg2", "wu2", "wd2"))


class _Comm:
    def __init__(self, shards):
        x, y, c = _mesh_pos()
        self.me = (2 * x + y).astype(jnp.int32).reshape(1)
        self.shards = shards
        self.tokens = []
        self.waiting = {}
        self.groups = []
        fulls = {k: _cast_place(self.me, shards[k], f"cast_{k}") for k in BIG}
        first = ("wg1", "wu1", "wd1")
        self.full = dict(zip(first, _gather_weights([fulls[k] for k in first])))
        dep = self.full["wd1"]
        for gi, names in enumerate(GATHER_GROUPS):
            plan = _gather_plan(len(names))
            send, recv, bufs, tok = _push_start(f"gather_start_{gi}", [fulls[k] for k in names], 3 * len(names), plan, dep)
            self.tokens.append(tok)
            dep = tok
            for k in names:
                self.waiting[k] = (gi, names, send, recv, bufs, plan)

    def order(self, a):
        for tok in self.tokens:
            a = a + tok[0, 0]
        self.tokens = []
        return a

    def dep(self):
        return self.tokens[-1] if self.tokens else None

    def weight(self, name, after):
        if name in self.waiting:
            gi, names, send, recv, bufs, plan = self.waiting[name]
            for k, buf in zip(names, _push_wait(f"gather_wait_{gi}", send, recv, bufs, plan, after)):
                self.full[k] = buf
                del self.waiting[k]
        full = self.full[name]
        return full.reshape(N_CHIPS * full.shape[1], full.shape[2])

    def ready(self, grads, after):
        names = list(grads)
        f32s, b16s = [], []
        for k in names:
            gf, gb = grads[k]
            f32s.append(gf.reshape((N_CHIPS,) + self.shards[k].shape))
            b16s.append(gb.reshape((N_CHIPS,) + self.shards[k].shape))
        n = len(names)
        lands = [lax.empty((N_CHIPS - 1,) + self.shards[k].shape, BF16) for k in names]
        plan = _grad_push_plan(n)
        gi = len(self.groups)
        send, recv, bufs, tok = _push_start(f"grad_start_{gi}", b16s + lands, 3 * n, plan, after)
        self.tokens.append(tok)
        self.groups.append((names, f32s, send, recv, bufs, plan))

    def finish(self):
        out, swaps = {}, []
        after = self.tokens[-1]
        for gi, (names, f32s, send, recv, bufs, plan) in enumerate(self.groups):
            n = len(names)
            bufs = _push_wait(f"grad_wait_{gi}", send, recv, bufs, plan, after)
            mine = [_sum_own(self.me, f32s[i], bufs[n + i], f"sum_{k}") for i, k in enumerate(names)]
            lands = [lax.empty(p.shape, F32) for p in mine]
            send2, recv2, both, after = _push_start(f"swap_start_{gi}", mine + lands, n, _swap_plan(n), mine[-1])
            swaps.append((names, send2, recv2, both))
        for gi, (names, send2, recv2, both) in enumerate(swaps):
            n = len(names)
            both = _push_wait(f"swap_wait_{gi}", send2, recv2, both, _swap_plan(n), after)
            for i, k in enumerate(names):
                out[k] = (both[i], both[n + i])
        return out


def kernel(x, positions, norm_ffn1, w_gate1, w_up1, w_down1, norm_mix, w_in, a_sink, w_out, norm_ffn2, w_gate2, w_up2, w_down2, norm_final, loss_target, m_norm_ffn1, m_w_gate1, m_w_up1, m_w_down1, m_norm_mix, m_w_in, m_a_sink, m_w_out, m_norm_ffn2, m_w_gate2, m_w_up2, m_w_down2, m_norm_final, v_norm_ffn1, v_w_gate1, v_w_up1, v_w_down1, v_norm_mix, v_w_in, v_a_sink, v_w_out, v_norm_ffn2, v_w_gate2, v_w_up2, v_w_down2, v_norm_final):
    T, D = x.shape[1], x.shape[2]
    flip = ("wg1", "wu1", "w_in", "wg2", "wu2")

    def rows(k, a):
        return a[0].T if k in flip else a[0]

    given = dict(wg1=(w_gate1, m_w_gate1, v_w_gate1), wu1=(w_up1, m_w_up1, v_w_up1), wd1=(w_down1, m_w_down1, v_w_down1),
                 w_in=(w_in, m_w_in, v_w_in), w_out=(w_out, m_w_out, v_w_out), wg2=(w_gate2, m_w_gate2, v_w_gate2),
                 wu2=(w_up2, m_w_up2, v_w_up2), wd2=(w_down2, m_w_down2, v_w_down2))
    shards = {k: rows(k, given[k][0]) for k in BIG}

    comm = _Comm(shards)

    norms = (norm_ffn1, norm_mix, norm_ffn2, norm_final.reshape(1, D))
    grad_x, small = _local_step(x[0], positions[0], loss_target[0], norms, a_sink[0], comm)

    partial = comm.finish()

    def pad_row(a):
        a = a.reshape(-1)
        return jnp.pad(a, (0, D - a.shape[0]))

    row4 = pad_row(jnp.concatenate([small["sink"], small["loss"].reshape(1)]))
    vec = jnp.stack([small["g1"], small["gm"], small["g2"], small["gf"], row4] + [jnp.zeros((D,), F32)] * 3, axis=0)
    red = _allreduce_small(vec)
    loss = red[4, 8]
    g_small = jnp.stack([red[0], red[1], red[2], red[3], pad_row(red[4, 0:8])] + [jnp.zeros((D,), F32)] * 3, axis=0)

    def small_stack(a1, am, a2, af, ask):
        return jnp.stack([pad_row(a1), pad_row(am), pad_row(a2), pad_row(af), pad_row(ask)] + [jnp.zeros((D,), F32)] * 3, axis=0)

    w_small = small_stack(norm_ffn1, norm_mix, norm_ffn2, norm_final, a_sink)
    m_small = small_stack(m_norm_ffn1, m_norm_mix, m_norm_ffn2, m_norm_final, m_a_sink)
    v_small = small_stack(v_norm_ffn1, v_norm_mix, v_norm_ffn2, v_norm_final, v_a_sink)
    live = small_stack(jnp.ones_like(norm_ffn1), jnp.ones_like(norm_mix), jnp.ones_like(norm_ffn2), jnp.ones_like(norm_final), jnp.ones_like(a_sink))
    v_small = jnp.where(live > 0, v_small, 1.0)

    upd = {}
    for k in BIG:
        outs = _adamw(shards[k], partial[k][0], partial[k][1], rows(k, given[k][1]), rows(k, given[k][2]), f"adamw_{k}")
        upd[k] = tuple((a.T if k in flip else a)[None] for a in outs)
    _, ds_, nms_, nvs_ = _adamw(w_small, g_small, jnp.zeros_like(g_small), m_small, v_small, "adamw_small")

    def small_out(arr):
        return [arr[0].reshape(1, D), arr[1].reshape(1, D), arr[2].reshape(1, D), arr[3], arr[4, 0:8].reshape(1, 8)]

    gs_, dss, nmss, nvss = small_out(g_small), small_out(ds_), small_out(nms_), small_out(nvs_)

    def ordered(i):
        sm = (gs_, dss, nmss, nvss)[i]
        return [sm[0], upd["wg1"][i], upd["wu1"][i], upd["wd1"][i], sm[1], upd["w_in"][i], sm[4], upd["w_out"][i], sm[2],
                upd["wg2"][i], upd["wu2"][i], upd["wd2"][i], sm[3]]

    return (loss, grad_x[None], *ordered(0), *ordered(1), *ordered(2), *ordered(3))
```

```python
import jax
import jax.numpy as jnp
from jax import lax
from jax.experimental import pallas as pl
from jax.experimental.pallas import tpu as pltpu

F32 = jnp.float32
BF16 = jnp.bfloat16

HEAD_DIM = 64
LANES = 128
SUBLANES = 8
A_Q_W, A_KV_W, B_W = 512, 128, 512
A_HALF_WINDOW = 128
B_PATTERNS = ((128, 1), (512, 4), (2048, 16))
ROPE_THETA = 10000.0
NORM_EPS = 1e-6
FFN_RES_WEIGHT = 0.5
ADAM_LR, ADAM_B1, ADAM_B2, ADAM_EPS, ADAM_WD, ADAM_STEP = 0.001, 0.9, 0.999, 1e-08, 0.01, 10
N_CHIPS = 4
N_DEV = 8
QB = 128
SHORT_SEQ = 512
NEG = -1e30
VMEM_LIMIT = 56 * 1024 * 1024
MESH = pl.DeviceIdType.MESH
ANY = pl.BlockSpec(memory_space=pl.ANY)


def _params(sem=None):
    return pltpu.CompilerParams(dimension_semantics=sem, vmem_limit_bytes=VMEM_LIMIT)


def _sds(shape, dtype):
    return jax.ShapeDtypeStruct(tuple(shape), dtype)


def _dot(a, b):
    return jnp.dot(a, b, preferred_element_type=F32)


def _dot_nt(a, b):
    return lax.dot_general(a, b, (((1,), (1,)), ((), ())), preferred_element_type=F32)


def _dot_tn(a, b):
    return lax.dot_general(a, b, (((0,), (0,)), ((), ())), preferred_element_type=F32)


def _rms_stats(x):
    r = lax.rsqrt(jnp.mean(x * x, axis=-1, keepdims=True) + NORM_EPS)
    return x * r, r


def _rms_bwd(dh, x, g):
    xhat, r = _rms_stats(x)
    dxn = dh * g
    dx = r * (dxn - xhat * jnp.mean(dxn * xhat, axis=-1, keepdims=True))
    tm, d = x.shape
    dg = (dh * xhat).reshape(tm // SUBLANES, SUBLANES, d).sum(axis=0)
    return dx, dg


def _sigmoid(x):
    return 1.0 / (1.0 + jnp.exp(-x))


def _swap32(t):
    n = t.shape[-1]
    lane = lax.broadcasted_iota(jnp.int32, t.shape, t.ndim - 1)
    return jnp.where((lane % HEAD_DIM) < HEAD_DIM // 2, pltpu.roll(t, n - HEAD_DIM // 2, axis=t.ndim - 1),
                     pltpu.roll(t, HEAD_DIM // 2, axis=t.ndim - 1))


def _cast_place(me_arr, w, name):
    R, C = w.shape
    tr = R // 2 if (R // 2) % 16 == 0 else R

    def body(me_ref, w_ref, o_ref):
        o_ref[...] = w_ref[...].astype(BF16)

    grid_spec = pltpu.PrefetchScalarGridSpec(
        num_scalar_prefetch=1, grid=(R // tr,), in_specs=[pl.BlockSpec((tr, C), lambda t, me: (t, 0))],
        out_specs=pl.BlockSpec((None, tr, C), lambda t, me: (me[0], t, 0)))
    return pl.pallas_call(body, name=name, grid_spec=grid_spec, out_shape=_sds((N_CHIPS, R, C), BF16),
                          compiler_params=_params(("parallel",)))(me_arr, w)


HBM = pl.BlockSpec(memory_space=pltpu.HBM)
SEM = pl.BlockSpec(memory_space=pltpu.SEMAPHORE)


def _push_start(name, bufs, ncopies, plan, after):
    nb = len(bufs)

    def body(*refs):
        send, recv, token = refs[nb + 1], refs[nb + 2], refs[-1]
        for i, (src, dst, dev) in enumerate(plan(refs[:nb])):
            pltpu.make_async_remote_copy(src_ref=src, dst_ref=dst, send_sem=send.at[i], recv_sem=recv.at[i],
                                         device_id=dev, device_id_type=MESH).start()
        token[...] = jnp.zeros_like(token)

    outs = pl.pallas_call(
        body, name=name,
        out_shape=(pltpu.SemaphoreType.DMA((ncopies,)), pltpu.SemaphoreType.DMA((ncopies,)), *[pltpu.HBM(b.shape, b.dtype) for b in bufs],
                   _sds((SUBLANES, LANES), F32)),
        in_specs=[HBM] * nb + [ANY], out_specs=(SEM, SEM, *([HBM] * nb), pl.BlockSpec(memory_space=pltpu.VMEM)),
        input_output_aliases={i: 2 + i for i in range(nb)},
        compiler_params=pltpu.CompilerParams(has_side_effects=pltpu.SideEffectType.DATAFLOW_SIDE_EFFECTING),
    )(*[pltpu.with_memory_space_constraint(b, pltpu.HBM) for b in bufs], after)
    return outs[0], outs[1], list(outs[2:2 + nb]), outs[-1]


def _push_wait(name, send, recv, bufs, plan, after):
    nb = len(bufs)

    def body(*refs):
        send_ref, recv_ref = refs[nb], refs[nb + 1]
        for i, (src, dst, dev) in enumerate(plan(refs[:nb])):
            cp = pltpu.make_async_remote_copy(src_ref=src, dst_ref=dst, send_sem=send_ref.at[i], recv_sem=recv_ref.at[i],
                                              device_id=dev, device_id_type=MESH)
            cp.wait_send()
            cp.wait_recv()

    outs = pl.pallas_call(
        body, name=name, out_shape=tuple(pltpu.HBM(b.shape, b.dtype) for b in bufs),
        in_specs=[HBM] * nb + [SEM, SEM, ANY], out_specs=tuple([HBM] * nb), input_output_aliases={i: i for i in range(nb)},
        compiler_params=pltpu.CompilerParams(has_side_effects=pltpu.SideEffectType.DATAFLOW_SIDE_EFFECTING),
    )(*bufs, send, recv, after)
    return list(outs)


def _mesh_pos():
    return lax.axis_index("x"), lax.axis_index("y"), lax.axis_index("c")


def _chip_peers(x, y, c):
    return [((1 - x, y, c), 2 * (1 - x) + y), ((x, 1 - y, c), 2 * x + (1 - y)), ((1 - x, 1 - y, c), 2 * (1 - x) + (1 - y))]


def _gather_plan(n):
    def plan(refs):
        x, y, c = _mesh_pos()
        me = 2 * x + y
        return [(refs[k].at[me], refs[k].at[me], dev) for k in range(n) for dev, _ in _chip_peers(x, y, c)]
    return plan


def _gather_weights(fulls):
    n = len(fulls)

    def body(*refs):
        ins, outs = refs[:n], refs[n:2 * n]
        ici_send, ici_recv, d2d_send, d2d_recv = refs[2 * n:]
        x, y, c = _mesh_pos()
        me = 2 * x + y
        sibling = (x, y, 1 - c)
        peers = _chip_peers(x, y, c)

        def half(k, who):
            r2 = fulls[k].shape[1] // 2
            return pl.ds(pl.multiple_of(who * r2, 16), r2)

        first = []
        for k in range(n):
            for rel, (dev, _) in enumerate(peers):
                cp = pltpu.make_async_remote_copy(src_ref=ins[k].at[me, half(k, c), :], dst_ref=outs[k].at[me, half(k, c), :],
                                                  send_sem=ici_send.at[k * 3 + rel], recv_sem=ici_recv.at[k * 3 + rel],
                                                  device_id=dev, device_id_type=MESH)
                cp.start()
                first.append(cp)
        passed = []
        for k in range(n):
            for rel, (dev, chip) in enumerate(peers):
                blk = outs[k].at[chip, half(k, c), :]
                pltpu.make_async_remote_copy(src_ref=blk, dst_ref=blk, send_sem=ici_send.at[k * 3 + rel], recv_sem=ici_recv.at[k * 3 + rel],
                                             device_id=dev, device_id_type=MESH).wait_recv()
                cp = pltpu.make_async_remote_copy(src_ref=blk, dst_ref=blk, send_sem=d2d_send.at[k * 3 + rel], recv_sem=d2d_recv.at[k * 3 + rel],
                                                  device_id=sibling, device_id_type=MESH)
                cp.start()
                passed.append(cp)
        for k in range(n):
            for rel, (dev, chip) in enumerate(peers):
                blk = outs[k].at[chip, half(k, 1 - c), :]
                pltpu.make_async_remote_copy(src_ref=blk, dst_ref=blk, send_sem=d2d_send.at[k * 3 + rel], recv_sem=d2d_recv.at[k * 3 + rel],
                                             device_id=sibling, device_id_type=MESH).wait_recv()
        for cp in first + passed:
            cp.wait_send()

    return pl.pallas_call(
        body, name="gather_weights", out_shape=[_sds(f.shape, BF16) for f in fulls],
        in_specs=[ANY] * n, out_specs=[ANY] * n, input_output_aliases={k: k for k in range(n)},
        scratch_shapes=[pltpu.SemaphoreType.DMA((n * 3,))] * 4, compiler_params=_params())(*fulls)


def _resident(shape):
    return pl.BlockSpec(shape, lambda i: (0,) * len(shape), pipeline_mode=pl.Buffered(1))


FFN_FWD_CHUNK = 256
FFN_DX_CHUNK = 512


def _chunks(n, step):
    return [(c0, min(step, n - c0)) for c0 in range(0, n, step)]


def _two_phase(chunks, first, second):
    held = {}
    for ci, ch in enumerate(chunks):
        held[ci] = first(*ch)
        if ci >= 1:
            second(*chunks[ci - 1], held.pop(ci - 1))
    last = len(chunks) - 1
    second(*chunks[last], held.pop(last))


def _ffn_fwd(x, g, wgt, wut, wd, name, tm=256):
    T, D = x.shape
    F = wd.shape[0]

    def body(x_ref, g_ref, wg_ref, wu_ref, wd_ref, xo_ref, h_ref, gate_ref, up_ref, act_ref):
        xv = x_ref[...]
        xhat, _ = _rms_stats(xv)
        h = (xhat * g_ref[...]).astype(BF16)
        h_ref[...] = h
        acc = []

        def first(c0, cw):
            return _dot_nt(h, wg_ref[c0:c0 + cw, :]), _dot_nt(h, wu_ref[c0:c0 + cw, :])

        def second(c0, cw, gate_up):
            gate, up = gate_up
            act = ((gate * _sigmoid(gate)) * up).astype(BF16)
            gate_ref[:, c0:c0 + cw] = gate.astype(BF16)
            up_ref[:, c0:c0 + cw] = up.astype(BF16)
            act_ref[:, c0:c0 + cw] = act
            d = _dot(act, wd_ref[c0:c0 + cw, :])
            acc[:] = [d if not acc else acc[0] + d]

        _two_phase(_chunks(F, FFN_FWD_CHUNK), first, second)
        xo_ref[...] = xv + FFN_RES_WEIGHT * acc[0]

    row = pl.BlockSpec((tm, D), lambda i: (i, 0))
    saved = pl.BlockSpec((tm, F), lambda i: (i, 0))
    return pl.pallas_call(
        body, name=name, grid=(T // tm,),
        in_specs=[row, pl.BlockSpec((1, D), lambda i: (0, 0)), _resident(wgt.shape), _resident(wut.shape), _resident(wd.shape)],
        out_specs=[row, row, saved, saved, saved],
        out_shape=[_sds((T, D), F32), _sds((T, D), BF16), _sds((T, F), BF16), _sds((T, F), BF16), _sds((T, F), BF16)],
        compiler_params=_params(("parallel",)))(x, g, wgt, wut, wd)


def _ffn_dx(dxo, x, g, gate_s, up_s, wgt, wut, wd, name, tm=256):
    T, D = x.shape
    F = wd.shape[0]

    def body(dxo_ref, x_ref, g_ref, gate_ref, up_ref, wg_ref, wu_ref, wd_ref, dx_ref, dff_ref, dgate_ref, dup_ref, dg_ref):
        @pl.when(pl.program_id(0) == 0)
        def _():
            dg_ref[...] = jnp.zeros_like(dg_ref)

        d = (FFN_RES_WEIGHT * dxo_ref[...]).astype(BF16)
        dff_ref[...] = d
        dh = []

        def first(c0, cw):
            return _dot_nt(d, wd_ref[c0:c0 + cw, :])

        def second(c0, cw, da):
            gate = gate_ref[:, c0:c0 + cw].astype(F32)
            up = up_ref[:, c0:c0 + cw].astype(F32)
            s = _sigmoid(gate)
            silu = gate * s
            dup = (da * silu).astype(BF16)
            dgate = (da * up * (s * (1.0 + gate * (1.0 - s)))).astype(BF16)
            dgate_ref[:, c0:c0 + cw] = dgate
            dup_ref[:, c0:c0 + cw] = dup
            t = _dot(dgate, wg_ref[c0:c0 + cw, :]) + _dot(dup, wu_ref[c0:c0 + cw, :])
            dh[:] = [t if not dh else dh[0] + t]

        _two_phase(_chunks(F, FFN_DX_CHUNK), first, second)
        dxn, dg = _rms_bwd(dh[0], x_ref[...], g_ref[...])
        dg_ref[...] += dg
        dx_ref[...] = dxo_ref[...] + dxn

    row = pl.BlockSpec((tm, D), lambda i: (i, 0))
    saved = pl.BlockSpec((tm, F), lambda i: (i, 0))
    return pl.pallas_call(
        body, name=name, grid=(T // tm,),
        in_specs=[row, row, pl.BlockSpec((1, D), lambda i: (0, 0)), saved, saved, _resident(wgt.shape), _resident(wut.shape),
                  _resident(wd.shape)],
        out_specs=[row, row, saved, saved, pl.BlockSpec((SUBLANES, D), lambda i: (0, 0))],
        out_shape=[_sds((T, D), F32), _sds((T, D), BF16), _sds((T, F), BF16), _sds((T, F), BF16), _sds((SUBLANES, D), F32)],
        compiler_params=_params(("arbitrary",)))(dxo, x, g, gate_s, up_s, wgt, wut, wd)


def _tn(a, b, mb, name, tk=2048, dep=None):
    T, M = a.shape
    N = b.shape[1]
    nt = T // tk

    def body(a_ref, b_ref, *refs):
        o_ref, ob_ref = refs[-2:]

        @pl.when(pl.program_id(1) == 0)
        def _():
            o_ref[...] = jnp.zeros_like(o_ref)

        o_ref[...] += _dot_tn(a_ref[...].astype(BF16), b_ref[...].astype(BF16))

        @pl.when(pl.program_id(1) == nt - 1)
        def _():
            ob_ref[...] = o_ref[...].astype(BF16)

    o_spec = pl.BlockSpec((mb, N), lambda g, t: (g, 0))
    return pl.pallas_call(
        body, name=name, grid=(M // mb, nt),
        in_specs=[pl.BlockSpec((tk, mb), lambda g, t: (t, g)), pl.BlockSpec((tk, N), lambda g, t: (t, 0))] + ([ANY] if dep is not None else []),
        out_specs=[o_spec, o_spec], out_shape=[_sds((M, N), F32), _sds((M, N), BF16)],
        compiler_params=_params(("parallel", "arbitrary")))(a, b, *([dep] if dep is not None else []))


def _rope_tables(pos_col, inv_freq):
    T = pos_col.shape[0]

    def body(p_ref, f_ref, c_ref, s_ref):
        ang = p_ref[...].astype(F32) * f_ref[...]
        lane = lax.broadcasted_iota(jnp.int32, ang.shape, 1)
        c_ref[...] = jnp.cos(ang)
        sn = jnp.sin(ang)
        s_ref[...] = jnp.where((lane % HEAD_DIM) < HEAD_DIM // 2, -sn, sn)

    tm = 1024
    return pl.pallas_call(
        body, name="rope_tables", grid=(T // tm,),
        in_specs=[pl.BlockSpec((tm, 1), lambda i: (i, 0)), pl.BlockSpec((1, LANES), lambda i: (0, 0))],
        out_specs=[pl.BlockSpec((tm, LANES), lambda i: (i, 0))] * 2,
        out_shape=[_sds((T, LANES), F32)] * 2, compiler_params=_params(("parallel",)))(pos_col, inv_freq)


def _deinterleave(scr, out_ref, d, tm, nblk):
    for r in range(d):
        for cb in range(nblk):
            out_ref[r, :, cb * LANES:(cb + 1) * LANES] = scr[cb, pl.ds(r, tm // d, stride=d), :].astype(out_ref.dtype)


def _interleave(in_ref, scr, d, tm, nblk):
    for r in range(d):
        for cb in range(nblk):
            scr[cb, pl.ds(r, tm // d, stride=d), :] = in_ref[r, :, cb * LANES:(cb + 1) * LANES].astype(F32)


def _proj_rope(x, g, w_in, cos, sin, tm=512):
    T, D = x.shape
    dils = [d for _, d in B_PATTERNS if d > 1]
    nbb = B_W // LANES
    scale = HEAD_DIM ** -0.5
    cuts = [0, A_Q_W, A_Q_W + A_KV_W, A_Q_W + 2 * A_KV_W, A_Q_W + 2 * A_KV_W + B_W, A_Q_W + 2 * A_KV_W + 2 * B_W,
            A_Q_W + 2 * A_KV_W + 3 * B_W]

    def body(x_ref, g_ref, w_ref, c_ref, s_ref, h_ref, aq_ref, ak_ref, av_ref, *rest):
        b_refs, scr = rest[:-1], rest[-1]
        xhat, _ = _rms_stats(x_ref[...])
        h = (xhat * g_ref[...]).astype(BF16)
        h_ref[...] = h
        cs, sn = c_ref[...], s_ref[...]

        def seg(idx, rope, mult):
            lo, hi = cuts[idx], cuts[idx + 1]
            blocks = []
            whole = _dot_nt(h, w_ref[lo:hi, :])
            for cb in range((hi - lo) // LANES):
                p = whole[:, cb * LANES:(cb + 1) * LANES]
                if rope:
                    p = p * cs + _swap32(p) * sn
                if mult != 1.0:
                    p = p * mult
                blocks.append(p)
            return blocks

        for idx, ref, rope, mult in ((0, aq_ref, True, scale), (1, ak_ref, True, 1.0), (2, av_ref, False, 1.0)):
            for cb, p in enumerate(seg(idx, rope, mult)):
                ref[:, cb * LANES:(cb + 1) * LANES] = p.astype(BF16)
        for which, (idx, rope, mult) in enumerate(((3, True, scale), (4, True, 1.0), (5, False, 1.0))):
            for cb, p in enumerate(seg(idx, rope, mult)):
                b_refs[which][:, cb * LANES:(cb + 1) * LANES] = p.astype(BF16)
                scr[cb] = p
            for di, d in enumerate(dils):
                _deinterleave(scr, b_refs[3 * (di + 1) + which], d, tm, nbb)

    row = lambda w: pl.BlockSpec((tm, w), lambda i: (i, 0))
    out_specs = [row(D), row(A_Q_W), row(A_KV_W), row(A_KV_W)] + [row(B_W)] * 3
    out_shape = [_sds((T, D), BF16), _sds((T, A_Q_W), BF16), _sds((T, A_KV_W), BF16), _sds((T, A_KV_W), BF16)] + [_sds((T, B_W), BF16)] * 3
    for d in dils:
        out_specs += [pl.BlockSpec((d, tm // d, B_W), lambda i: (0, i, 0))] * 3
        out_shape += [_sds((d, T // d, B_W), BF16)] * 3
    return pl.pallas_call(
        body, name="proj_rope", grid=(T // tm,),
        in_specs=[row(D), pl.BlockSpec((1, D), lambda i: (0, 0)), pl.BlockSpec(w_in.shape, lambda i: (0, 0)), row(LANES), row(LANES)],
        out_specs=out_specs, out_shape=out_shape, scratch_shapes=[pltpu.VMEM((nbb, tm, LANES), F32)],
        compiler_params=_params(("parallel",)))(x, g, w_in, cos, sin)


def _band_bias(rel, qb, kw, hw):
    ri = lax.broadcasted_iota(jnp.int32, (2 * qb, kw), 0) & (qb - 1)
    ci = lax.broadcasted_iota(jnp.int32, (2 * qb, kw), 1)
    return jnp.where(jnp.abs(ri + rel - ci) <= hw, 0.0, NEG).astype(F32)


def _stack_heads(x, lo):
    z = jnp.zeros_like(x)
    return jnp.concatenate([jnp.where(lo, x, z), jnp.where(lo, z, x)], axis=0)


def _unstack_heads(y, lo):
    qb = y.shape[0] // 2
    return jnp.where(lo, y[:qb], y[qb:])


def _band_setup(bias_scr, qb, kw, hw):
    if bias_scr is not None:
        for i in range(3):
            bias_scr[i] = _band_bias(i * hw, qb, kw, hw)


def _band_window(bias_scr, qs, L, qb, kw, hw):
    ws = pl.multiple_of(jnp.clip(qs - hw, 0, L - kw), 64)
    if bias_scr is None:
        return ws, _band_bias(qs - ws, qb, kw, hw)
    return ws, bias_scr[lax.shift_right_logical(qs - ws, hw.bit_length() - 1)]


def _dup_kv_head(src_ref, dst_ref, head, L):
    step = min(L, 1024)
    for r0 in range(0, L, step):
        xf = src_ref[r0:r0 + step, :].astype(F32)
        lane = lax.broadcasted_iota(jnp.int32, xf.shape, 1)
        keep = jnp.logical_xor(lane < HEAD_DIM, head == 1)
        dst_ref[r0:r0 + step, :] = jnp.where(keep, xf, pltpu.roll(xf, HEAD_DIM, axis=1)).astype(dst_ref.dtype)


def _attn_fwd(q, k, v, sink, hw, gqa, out_dtype, name, qb=QB, blocks_per_step=8):
    NB, L, Cq = q.shape
    Ls = min(L, 2048)
    kw = min(qb + 2 * hw, L)
    tables = L >= qb + 2 * hw
    unroll = min(blocks_per_step, Ls // qb)
    nlb = 1 if (gqa or L > SHORT_SEQ) else Cq // LANES

    def body(sink_ref, q_ref, k_ref, v_ref, o_ref, lse_ref, *scr):
        b, s_idx = pl.program_id(1), pl.program_id(2)
        bias_scr = scr[0] if tables else None
        _band_setup(bias_scr, qb, kw, hw)
        if gqa:
            kd, vd = scr[-2:]

            @pl.when(s_idx == 0)
            def _():
                _dup_kv_head(k_ref, kd, b // 2, L)
                _dup_kv_head(v_ref, vd, b // 2, L)
        else:
            kd, vd = k_ref, v_ref
        lane = lax.broadcasted_iota(jnp.int32, (qb, LANES), 1)
        lo = lane < HEAD_DIM
        if gqa:
            row = lax.broadcasted_iota(jnp.int32, (2 * qb, 1), 0)
            sk = jnp.where(row < qb, sink_ref[2 * b], sink_ref[2 * b + 1])

        def block(ql, col):
            qs = s_idx * Ls + ql
            ws, bias = _band_window(bias_scr, qs, L, qb, kw, hw)
            return ws, _dot_nt(_stack_heads(q_ref[pl.ds(ql, qb), col], lo), kd[pl.ds(ws, kw), col]) + bias

        def finish(ql, col, scores):
            ws, s = scores
            m = jnp.max(s, axis=-1, keepdims=True)
            if gqa:
                m = jnp.maximum(m, sk)
            p = jnp.exp(s - m)
            den = jnp.sum(p, axis=-1, keepdims=True)
            if gqa:
                den = den + jnp.exp(sk - m)
            o = _dot(p.astype(BF16), vd[pl.ds(ws, kw), col]) * (1.0 / den)
            o_ref[pl.ds(ql, qb), col] = _unstack_heads(o, lo).astype(o_ref.dtype)
            lse_ref[pl.ds(ql, qb), col] = _unstack_heads(m + jnp.log(den), lo)

        for lb in range(nlb):
            def step(n, carry, col=slice(lb * LANES, (lb + 1) * LANES)):
                _two_phase([(pl.multiple_of((n * unroll + u) * qb, qb), col) for u in range(unroll)], block, finish)
                return carry

            lax.fori_loop(0, Ls // (qb * unroll), step, 0)

    kv_map = (lambda r, b, s: (r, 0, 0)) if gqa else (lambda r, b, s: (r, 0, b))
    seg = pl.BlockSpec((None, Ls, nlb * LANES), lambda r, b, s: (r, s, b))
    return pl.pallas_call(
        body, name=name, grid=(NB, Cq // (nlb * LANES), L // Ls),
        in_specs=[pl.BlockSpec(memory_space=pltpu.SMEM), seg, pl.BlockSpec((None, L, nlb * LANES), kv_map),
                  pl.BlockSpec((None, L, nlb * LANES), kv_map)],
        out_specs=[seg, seg], out_shape=[_sds((NB, L, Cq), out_dtype), _sds((NB, L, Cq), F32)],
        scratch_shapes=([pltpu.VMEM((3, 2 * qb, kw), F32)] if tables else []) + ([pltpu.VMEM((L, LANES), BF16)] * 2 if gqa else []),
        compiler_params=_params(("parallel", "parallel", "arbitrary")))(sink, q, k, v)


def _attn_bwd(q, k, v, do, lse, delta, sink, hw, gqa, name, qb=QB, blocks_per_step=8):
    NB, L, Cq = q.shape
    Ck = k.shape[2]
    Ls = min(L, 2048)
    kw = min(qb + 2 * hw, L)
    reps = kw // LANES
    nseg = L // Ls
    scale = HEAD_DIM ** -0.5
    tables = L >= qb + 2 * hw
    unroll = min(blocks_per_step, Ls // qb)
    nlb = 1 if (gqa or L > SHORT_SEQ) else Cq // LANES

    def body(sink_ref, q_ref, do_ref, lse_ref, dl_ref, k_ref, v_ref, dq_ref, dk_ref, dv_ref, dsk_ref, *scr):
        b, s_idx = pl.program_id(1), pl.program_id(2)
        lane = lax.broadcasted_iota(jnp.int32, (qb, LANES), 1)
        lo = lane < HEAD_DIM
        bias_scr = scr[0] if tables else None
        _band_setup(bias_scr, qb, kw, hw)
        if gqa:
            kd, vd, dk_acc, dv_acc, dsk_acc = scr[-5:]

            @pl.when(s_idx == 0)
            def _():
                _dup_kv_head(k_ref, kd, b // 2, L)
                _dup_kv_head(v_ref, vd, b // 2, L)
                dk_acc[...] = jnp.zeros_like(dk_acc)
                dv_acc[...] = jnp.zeros_like(dv_acc)
                dsk_acc[...] = jnp.zeros_like(dsk_acc)

            @pl.when((s_idx == 0) & (b == 0))
            def _():
                dk_ref[...] = jnp.zeros_like(dk_ref)
                dv_ref[...] = jnp.zeros_like(dv_ref)
        else:
            kd, vd = k_ref, v_ref
            dk_acc, dv_acc = scr[-2:]

            @pl.when(s_idx == 0)
            def _():
                dk_acc[...] = jnp.zeros_like(dk_acc)
                dv_acc[...] = jnp.zeros_like(dv_acc)

        def block(ql, col):
            qs = s_idx * Ls + ql
            ws, bias = _band_window(bias_scr, qs, L, qb, kw, hw)
            qv, dov = q_ref[pl.ds(ql, qb), col], do_ref[pl.ds(ql, qb), col]
            lse, dl = lse_ref[pl.ds(ql, qb), col], dl_ref[pl.ds(ql, qb), col]
            kv_, vv = kd[pl.ds(ws, kw), col], vd[pl.ds(ws, kw), col]
            q2, do2 = _stack_heads(qv, lo), _stack_heads(dov, lo)
            return ws, q2, do2, lse, dl, _dot_nt(q2, kv_) + bias, _dot_nt(do2, vv)

        def finish(ql, col, held):
            ws, q2, do2, lse, dl, s, dp = held
            lse_sw, dl_sw = pltpu.roll(lse, HEAD_DIM, axis=1), pltpu.roll(dl, HEAD_DIM, axis=1)
            lse2 = jnp.concatenate([jnp.where(lo, lse, lse_sw), jnp.where(lo, lse_sw, lse)], axis=0)
            dl2 = jnp.concatenate([jnp.where(lo, dl, dl_sw), jnp.where(lo, dl_sw, dl)], axis=0)
            p = jnp.exp(s - jnp.tile(lse2, (1, reps)))
            ds = (p * (dp - jnp.tile(dl2, (1, reps)))).astype(BF16)
            dq_ref[pl.ds(ql, qb), col] = (_unstack_heads(_dot(ds, kd[pl.ds(ws, kw), col]), lo) * scale).astype(dq_ref.dtype)
            both = _dot_tn(jnp.concatenate([ds, p.astype(BF16)], axis=1), jnp.concatenate([q2, do2], axis=1))
            dk_acc[pl.ds(ws, kw), col] += both[:kw, :LANES]
            dv_acc[pl.ds(ws, kw), col] += both[kw:, LANES:]
            if gqa:
                sk = jnp.where(lo, sink_ref[2 * b], sink_ref[2 * b + 1])
                dsk_acc[...] += -jnp.exp(sk - lse) * dl

        for lb in range(nlb):
            def step(n, carry, col=slice(lb * LANES, (lb + 1) * LANES)):
                _two_phase([(pl.multiple_of((n * unroll + u) * qb, qb), col) for u in range(unroll)], block, finish)
                return carry

            lax.fori_loop(0, Ls // (qb * unroll), step, 0)

        if gqa:
            @pl.when(s_idx == nseg - 1)
            def _():
                step_rows = min(L, 1024)
                for r0 in range(0, L, step_rows):
                    lanek = lax.broadcasted_iota(jnp.int32, (step_rows, LANES), 1)
                    mine = jnp.logical_xor(lanek < HEAD_DIM, (b // 2) == 1)
                    for acc, ref in ((dk_acc, dk_ref), (dv_acc, dv_ref)):
                        a = acc[r0:r0 + step_rows, :]
                        ref[r0:r0 + step_rows, :] += jnp.where(mine, a + pltpu.roll(a, HEAD_DIM, axis=1), 0.0)
                dsk_ref[...] = dsk_acc[...].reshape(qb // SUBLANES, SUBLANES, LANES).sum(axis=0)
        else:
            dsk_ref[...] = jnp.zeros_like(dsk_ref)

            @pl.when(s_idx == nseg - 1)
            def _():
                dk_ref[...] = dk_acc[...].astype(dk_ref.dtype)
                dv_ref[...] = dv_acc[...].astype(dv_ref.dtype)

    kv_map = (lambda r, b, s: (r, 0, 0)) if gqa else (lambda r, b, s: (r, 0, b))
    seg = pl.BlockSpec((None, Ls, nlb * LANES), lambda r, b, s: (r, s, b))
    full = pl.BlockSpec((None, L, nlb * LANES), kv_map)
    scratch = [pltpu.VMEM((3, 2 * qb, kw), F32)] if tables else []
    if gqa:
        scratch += [pltpu.VMEM((L, LANES), BF16)] * 2 + [pltpu.VMEM((L, LANES), F32)] * 2 + [pltpu.VMEM((qb, LANES), F32)]
    else:
        scratch += [pltpu.VMEM((L, nlb * LANES), F32)] * 2
    kv_dtype = F32 if gqa else BF16
    return pl.pallas_call(
        body, name=name, grid=(NB, Cq // (nlb * LANES), nseg),
        in_specs=[pl.BlockSpec(memory_space=pltpu.SMEM), seg, seg, seg, seg, full, full],
        out_specs=[seg, full, full, pl.BlockSpec((None, None, SUBLANES, LANES), lambda r, b, s: (r, b, 0, 0))],
        out_shape=[_sds((NB, L, Cq), BF16), _sds((NB, L, Ck), kv_dtype), _sds((NB, L, Ck), kv_dtype),
                   _sds((NB, Cq // LANES, SUBLANES, LANES), F32)],
        scratch_shapes=scratch,
        compiler_params=_params(("arbitrary", "arbitrary", "arbitrary")))(sink, q, do, lse, delta, k, v)


def _merge_b(a_out, o1, l1, o4, l4, o16, l16, tm=512):
    T = a_out.shape[0]
    nbb = B_W // LANES

    def body(a_ref, o1_ref, l1_ref, o4_ref, l4_ref, o16_ref, l16_ref, cat_ref, lg1_ref, lg4_ref, lg16_ref, so, sl, slg):
        _interleave(o4_ref, so.at[0], 4, tm, nbb)
        _interleave(l4_ref, sl.at[0], 4, tm, nbb)
        _interleave(o16_ref, so.at[1], 16, tm, nbb)
        _interleave(l16_ref, sl.at[1], 16, tm, nbb)
        cat_ref[:, 0:A_Q_W] = a_ref[...]
        for cb in range(nbb):
            cols = slice(cb * LANES, (cb + 1) * LANES)
            os_ = (o1_ref[:, cols], so[0, cb], so[1, cb])
            ls_ = (l1_ref[:, cols], sl[0, cb], sl[1, cb])
            m = jnp.maximum(jnp.maximum(ls_[0], ls_[1]), ls_[2])
            es = [jnp.exp(l - m) for l in ls_]
            den = es[0] + es[1] + es[2]
            out = (es[0] * os_[0] + es[1] * os_[1] + es[2] * os_[2]) * (1.0 / den)
            lg = m + jnp.log(den)
            cat_ref[:, A_Q_W + cb * LANES:A_Q_W + (cb + 1) * LANES] = out.astype(BF16)
            lg1_ref[:, cols] = lg
            slg[cb] = lg
        _deinterleave(slg, lg4_ref, 4, tm, nbb)
        _deinterleave(slg, lg16_ref, 16, tm, nbb)

    row = lambda w: pl.BlockSpec((tm, w), lambda i: (i, 0))
    perm = lambda d: pl.BlockSpec((d, tm // d, B_W), lambda i: (0, i, 0))
    return pl.pallas_call(
        body, name="merge_patterns", grid=(T // tm,),
        in_specs=[row(A_Q_W), row(B_W), row(B_W), perm(4), perm(4), perm(16), perm(16)],
        out_specs=[row(A_Q_W + B_W), row(B_W), perm(4), perm(16)],
        out_shape=[_sds((T, A_Q_W + B_W), BF16), _sds((T, B_W), F32), _sds((4, T // 4, B_W), F32), _sds((16, T // 16, B_W), F32)],
        scratch_shapes=[pltpu.VMEM((2, nbb, tm, LANES), F32), pltpu.VMEM((2, nbb, tm, LANES), F32), pltpu.VMEM((nbb, tm, LANES), F32)],
        compiler_params=_params(("parallel",)))(a_out, o1, l1, o4, l4, o16, l16)


def _out_proj(x, cat, w_out, tm=512):
    T, D = x.shape

    def body(x_ref, c_ref, w_ref, o_ref):
        o_ref[...] = x_ref[...] + _dot(c_ref[...], w_ref[...])

    row = lambda w: pl.BlockSpec((tm, w), lambda i: (i, 0))
    return pl.pallas_call(
        body, name="out_proj", grid=(T // tm,), in_specs=[row(D), row(cat.shape[1]), pl.BlockSpec(w_out.shape, lambda i: (0, 0))],
        out_specs=row(D), out_shape=_sds((T, D), F32), compiler_params=_params(("parallel",)))(x, cat, w_out)


def _final_loss(x, g, target, tm=512):
    T, D = x.shape

    def body(x_ref, g_ref, t_ref, dx_ref, dg_ref, loss_ref):
        @pl.when(pl.program_id(0) == 0)
        def _():
            dg_ref[...] = jnp.zeros_like(dg_ref)
            loss_ref[...] = jnp.zeros_like(loss_ref)

        xv, gv = x_ref[...], g_ref[...]
        xhat, _ = _rms_stats(xv)
        err = xhat * gv - t_ref[...]
        loss_ref[...] += 0.5 * jnp.sum(jnp.sum(err * err, axis=-1, keepdims=True) * (1.0 / D), axis=0, keepdims=True)
        dx, dg = _rms_bwd(err * (1.0 / D), xv, gv)
        dx_ref[...] = dx
        dg_ref[...] += dg

    row = pl.BlockSpec((tm, D), lambda i: (i, 0))
    return pl.pallas_call(
        body, name="final_loss", grid=(T // tm,), in_specs=[row, pl.BlockSpec((1, D), lambda i: (0, 0)), row],
        out_specs=[row, pl.BlockSpec((SUBLANES, D), lambda i: (0, 0)), pl.BlockSpec((SUBLANES, LANES), lambda i: (0, 0))],
        out_shape=[_sds((T, D), F32), _sds((SUBLANES, D), F32), _sds((SUBLANES, LANES), F32)],
        compiler_params=_params(("arbitrary",)))(x, g, target)


def _dcat(dx, w_out, cat, tm=512):
    T, D = dx.shape
    C = cat.shape[1]
    nba, nbb = A_Q_W // LANES, B_W // LANES

    def body(dx_ref, w_ref, cat_ref, doa_ref, dla_ref, dob1_ref, dlb1_ref, dob4_ref, dlb4_ref, dob16_ref, dlb16_ref, sdo, sdl):
        dc = _dot_nt(dx_ref[...].astype(BF16), w_ref[...])
        ri = lax.broadcasted_iota(jnp.int32, (LANES, LANES), 0)
        ci = lax.broadcasted_iota(jnp.int32, (LANES, LANES), 1)
        same_head = ((ri // HEAD_DIM) == (ci // HEAD_DIM)).astype(BF16)
        for cb in range(C // LANES):
            cols = slice(cb * LANES, (cb + 1) * LANES)
            blk = dc[:, cols]
            prod = blk * cat_ref[:, cols].astype(F32)
            hi = prod.astype(BF16)
            lo_ = (prod - hi.astype(F32)).astype(BF16)
            dl = _dot(hi, same_head) + _dot(lo_, same_head)
            if cb < nba:
                doa_ref[:, cols] = blk.astype(BF16)
                dla_ref[:, cols] = dl
            else:
                bcols = slice((cb - nba) * LANES, (cb - nba + 1) * LANES)
                dob1_ref[:, bcols] = blk.astype(BF16)
                dlb1_ref[:, bcols] = dl
                sdo[cb - nba] = blk
                sdl[cb - nba] = dl
        _deinterleave(sdo, dob4_ref, 4, tm, nbb)
        _deinterleave(sdl, dlb4_ref, 4, tm, nbb)
        _deinterleave(sdo, dob16_ref, 16, tm, nbb)
        _deinterleave(sdl, dlb16_ref, 16, tm, nbb)

    row = lambda w: pl.BlockSpec((tm, w), lambda i: (i, 0))
    perm = lambda d: pl.BlockSpec((d, tm // d, B_W), lambda i: (0, i, 0))
    return pl.pallas_call(
        body, name="dcat", grid=(T // tm,), in_specs=[row(D), pl.BlockSpec(w_out.shape, lambda i: (0, 0)), row(C)],
        out_specs=[row(A_Q_W), row(A_Q_W), row(B_W), row(B_W), perm(4), perm(4), perm(16), perm(16)],
        out_shape=[_sds((T, A_Q_W), BF16), _sds((T, A_Q_W), F32), _sds((T, B_W), BF16), _sds((T, B_W), F32),
                   _sds((4, T // 4, B_W), BF16), _sds((4, T // 4, B_W), F32), _sds((16, T // 16, B_W), BF16), _sds((16, T // 16, B_W), F32)],
        scratch_shapes=[pltpu.VMEM((nbb, tm, LANES), F32)] * 2, compiler_params=_params(("parallel",)))(dx, w_out, cat)


def _rope_bwd_assemble(dqa, dka, dva, b1, b4, b16, cos, sin, tm=512):
    T = dqa.shape[0]
    nbb = B_W // LANES
    width = A_Q_W + 2 * A_KV_W + 3 * B_W

    def body(dqa_ref, dka_ref, dva_ref, q1, k1, v1, q4, k4, v4, q16, k16, v16, c_ref, s_ref, o_ref, scr):
        cs, sn = c_ref[...], s_ref[...]

        def unrope(t):
            return t * cs + _swap32(t * sn)

        col = 0
        for ref, rope in ((dqa_ref, True), (dka_ref, True), (dva_ref, False)):
            for cb in range(ref.shape[1] // LANES):
                t = ref[:, cb * LANES:(cb + 1) * LANES].astype(F32)
                o_ref[:, col:col + LANES] = (unrope(t) if rope else t).astype(BF16)
                col += LANES
        for which, (r1, r4, r16, rope) in enumerate(((q1, q4, q16, True), (k1, k4, k16, True), (v1, v4, v16, False))):
            _interleave(r4, scr.at[0], 4, tm, nbb)
            _interleave(r16, scr.at[1], 16, tm, nbb)
            for cb in range(nbb):
                t = r1[:, cb * LANES:(cb + 1) * LANES].astype(F32) + scr[0, cb] + scr[1, cb]
                o_ref[:, col:col + LANES] = (unrope(t) if rope else t).astype(BF16)
                col += LANES

    row = lambda w: pl.BlockSpec((tm, w), lambda i: (i, 0))
    perm = lambda d: pl.BlockSpec((d, tm // d, B_W), lambda i: (0, i, 0))
    return pl.pallas_call(
        body, name="rope_bwd", grid=(T // tm,),
        in_specs=[row(A_Q_W), row(A_KV_W), row(A_KV_W)] + [row(B_W)] * 3 + [perm(4)] * 3 + [perm(16)] * 3 + [row(LANES), row(LANES)],
        out_specs=row(width), out_shape=_sds((T, width), BF16), scratch_shapes=[pltpu.VMEM((2, nbb, tm, LANES), F32)],
        compiler_params=_params(("parallel",)))(dqa, dka, dva, *b1, *b4, *b16, cos, sin)


def _dh_norm(dproj, w_in, x, g, dres, tm=512):
    T, D = x.shape

    def body(dp_ref, w_ref, x_ref, g_ref, dr_ref, dx_ref, dg_ref):
        @pl.when(pl.program_id(0) == 0)
        def _():
            dg_ref[...] = jnp.zeros_like(dg_ref)

        dxn, dg = _rms_bwd(_dot(dp_ref[...], w_ref[...]), x_ref[...], g_ref[...])
        dg_ref[...] += dg
        dx_ref[...] = dr_ref[...] + dxn

    row = lambda w: pl.BlockSpec((tm, w), lambda i: (i, 0))
    return pl.pallas_call(
        body, name="dh_norm", grid=(T // tm,),
        in_specs=[row(dproj.shape[1]), pl.BlockSpec(w_in.shape, lambda i: (0, 0)), row(D), pl.BlockSpec((1, D), lambda i: (0, 0)), row(D)],
        out_specs=[row(D), pl.BlockSpec((SUBLANES, D), lambda i: (0, 0))],
        out_shape=[_sds((T, D), F32), _sds((SUBLANES, D), F32)], compiler_params=_params(("arbitrary",)))(dproj, w_in, x, g, dres)


def _grad_push_plan(n):
    def plan(refs):
        x, y, c = _mesh_pos()
        return [(refs[k].at[chip], refs[n + k].at[rel], dev) for k in range(n) for rel, (dev, chip) in enumerate(_chip_peers(x, y, c))]
    return plan


def _sum_own(me_arr, g, landed, name):
    ns, R, C = g.shape
    tr = R // 2 if (R // 2) % 16 == 0 else R

    def body(me_ref, g_ref, x_ref, o_ref):
        acc = g_ref[...]
        for rel in range(ns - 1):
            acc = acc + x_ref[rel].astype(F32)
        o_ref[...] = acc

    grid_spec = pltpu.PrefetchScalarGridSpec(
        num_scalar_prefetch=1, grid=(R // tr,),
        in_specs=[pl.BlockSpec((None, tr, C), lambda t, me: (me[0], t, 0)), pl.BlockSpec((ns - 1, tr, C), lambda t, me: (0, t, 0))],
        out_specs=pl.BlockSpec((tr, C), lambda t, me: (t, 0)))
    return pl.pallas_call(body, name=name, grid_spec=grid_spec, out_shape=_sds((R, C), F32),
                          compiler_params=_params(("parallel",)))(me_arr, g, landed)


def _swap_plan(n):
    def plan(refs):
        x, y, c = _mesh_pos()
        return [(refs[k], refs[n + k], (x, y, 1 - c)) for k in range(n)]
    return plan


def _allreduce_small(v):
    rows, W = v.shape

    def body(v_ref, o_ref, buf, send, recv):
        x, y, c = _mesh_pos()
        me = 4 * x + 2 * y + c
        cps = []
        for m in range(1, N_DEV):
            dev = (x ^ (m >> 2), y ^ ((m >> 1) & 1), c ^ (m & 1))
            cp = pltpu.make_async_remote_copy(src_ref=v_ref, dst_ref=buf.at[me], send_sem=send.at[m - 1], recv_sem=recv.at[m - 1],
                                              device_id=dev, device_id_type=MESH)
            cp.start()
            cps.append(cp)
        for m in range(1, N_DEV):
            pltpu.make_async_remote_copy(src_ref=v_ref, dst_ref=buf.at[me ^ m], send_sem=send.at[m - 1], recv_sem=recv.at[m - 1],
                                         device_id=(x, y, c), device_id_type=MESH).wait_recv()
        for cp in cps:
            cp.wait_send()
        buf[me] = v_ref[...]
        acc = buf[0]
        for i in range(1, N_DEV):
            acc = acc + buf[i]
        o_ref[...] = acc

    return pl.pallas_call(
        body, name="allreduce_small", out_shape=_sds((rows, W), F32),
        scratch_shapes=[pltpu.VMEM((N_DEV, rows, W), F32), pltpu.SemaphoreType.DMA((N_DEV - 1,)), pltpu.SemaphoreType.DMA((N_DEV - 1,))],
        compiler_params=_params())(v)


def _adamw(w, gp, gq, m, v, name):
    R, C = w.shape
    tr = R // 2 if (R // 2) % SUBLANES == 0 else R
    c1 = 1.0 / (1.0 - ADAM_B1 ** ADAM_STEP)
    c2 = 1.0 / (1.0 - ADAM_B2 ** ADAM_STEP)

    def body(w_ref, gp_ref, gq_ref, m_ref, v_ref, g_ref, d_ref, nm_ref, nv_ref):
        gv = gp_ref[...] + gq_ref[...]
        nm = ADAM_B1 * m_ref[...] + (1.0 - ADAM_B1) * gv
        nv = ADAM_B2 * v_ref[...] + (1.0 - ADAM_B2) * (gv * gv)
        g_ref[...] = gv
        d_ref[...] = -ADAM_LR * ((nm * c1) / (jnp.sqrt(nv * c2) + ADAM_EPS) + ADAM_WD * w_ref[...])
        nm_ref[...] = nm
        nv_ref[...] = nv

    blk = pl.BlockSpec((tr, C), lambda t: (t, 0))
    return pl.pallas_call(body, name=name, grid=(R // tr,), in_specs=[blk] * 5, out_specs=[blk] * 4,
                          out_shape=[_sds((R, C), F32)] * 4, compiler_params=_params(("parallel",)))(w, gp, gq, m, v)


def _local_step(x, positions, target, norms, a_sink, comm):
    T, D = x.shape
    g1, gm, g2, gf = norms
    inv_freq = 1.0 / (ROPE_THETA ** (jnp.arange(0, HEAD_DIM, 2, dtype=F32) / HEAD_DIM))
    inv_freq = jnp.tile(inv_freq, LANES // (HEAD_DIM // 2)).reshape(1, LANES)
    cos, sin = _rope_tables(positions.reshape(T, 1), inv_freq)
    no_sink = jnp.zeros((2 * (B_W // LANES),), F32)
    W = {k: comm.weight(k, x) for k in ("wg1", "wu1", "wd1")}

    x1, h1, gate1, up1, act1 = _ffn_fwd(x, comm.order(g1), W["wg1"], W["wu1"], W["wd1"], "ffn1_fwd")
    W["w_in"] = comm.weight("w_in", x1)
    (h2, aq, ak, av, bq1, bk1, bv1, bq4, bk4, bv4, bq16, bk16, bv16) = _proj_rope(x1, gm, W["w_in"], cos, sin)
    a_out, a_lse = _attn_fwd(aq[None], ak[None], av[None], a_sink, A_HALF_WINDOW, True, BF16, "attn_a_fwd", qb=2 * QB, blocks_per_step=4)
    bqs = {1: (bq1[None], bk1[None], bv1[None]), 4: (bq4, bk4, bv4), 16: (bq16, bk16, bv16)}
    b_o, b_l = {}, {}
    for w, d in B_PATTERNS:
        q_, k_, v_ = bqs[d]
        b_o[d], b_l[d] = _attn_fwd(q_, k_, v_, no_sink, w // (2 * d), False, BF16, f"attn_b{d}_fwd")
    cat, lg1, lg4, lg16 = _merge_b(a_out[0], b_o[1][0], b_l[1][0], b_o[4], b_l[4], b_o[16], b_l[16])
    W["w_out"] = comm.weight("w_out", cat)
    x2 = _out_proj(x1, cat, W["w_out"])
    for k in ("wg2", "wu2", "wd2"):
        W[k] = comm.weight(k, x2)
    x3, h3, gate2, up2, act2 = _ffn_fwd(x2, g2, W["wg2"], W["wu2"], W["wd2"], "ffn2_fwd")

    dx3, dgf, loss8 = _final_loss(x3, gf, target)
    dx2, dff2, dgate2, dup2, dg2 = _ffn_dx(dx3, x2, g2, gate2, up2, W["wg2"], W["wu2"], W["wd2"], "ffn2_dx")
    fb = gate2.shape[1] // 2
    dwg2 = _tn(dgate2, h3, fb, "ffn2_dw_gate")
    dwu2 = _tn(dup2, h3, fb, "ffn2_dw_up")
    dwd2 = _tn(act2, dff2, fb, "ffn2_dw_down")
    comm.ready(dict(wg2=dwg2, wu2=dwu2, wd2=dwd2), dwd2[0])

    doa, dla, dob1, dlb1, dob4, dlb4, dob16, dlb16 = _dcat(dx2, W["w_out"], cat)
    dw_out = _tn(cat, dx2, cat.shape[1], "w_out_dw", dep=comm.dep())
    dqa, dka, dva, dsk = _attn_bwd(aq[None], ak[None], av[None], doa[None], a_lse, dla[None], comm.order(a_sink), A_HALF_WINDOW, True,
                                   "attn_a_bwd")
    bwd_in = {1: (dob1[None], lg1[None], dlb1[None]), 4: (dob4, lg4, dlb4), 16: (dob16, lg16, dlb16)}
    bg = {}
    for w, d in B_PATTERNS:
        q_, k_, v_ = bqs[d]
        do_, l_, dl_ = bwd_in[d]
        bg[d] = _attn_bwd(q_, k_, v_, do_, l_, dl_, no_sink, w // (2 * d), False, f"attn_b{d}_bwd")[:3]
    dproj = _rope_bwd_assemble(dqa[0], dka[0], dva[0], [t[0] for t in bg[1]], bg[4], bg[16], cos, sin)
    dw_in = _tn(dproj, h2, dproj.shape[1] // 2, "w_in_dw")
    comm.ready(dict(w_in=dw_in, w_out=dw_out), dw_in[0])
    dx1, dgm = _dh_norm(dproj, W["w_in"], x1, comm.order(gm), dx2)

    dx0, dff1, dgate1, dup1, dg1 = _ffn_dx(dx1, x, g1, gate1, up1, W["wg1"], W["wu1"], W["wd1"], "ffn1_dx")
    dwd1 = _tn(act1, dff1, fb, "ffn1_dw_down")
    comm.ready(dict(wd1=dwd1), dwd1[0])
    dwg1 = _tn(dgate1, h1, fb, "ffn1_dw_gate", dep=comm.dep())
    comm.ready(dict(wg1=dwg1), dwg1[0])
    dwu1 = _tn(dup1, h1, fb, "ffn1_dw_up", dep=comm.dep())
    comm.ready(dict(wu1=dwu1), dwu1[0])

    dsink = dsk[0, :, :, ::HEAD_DIM].sum(axis=1).reshape(-1)
    small = dict(g1=dg1.sum(axis=0), gm=dgm.sum(axis=0), g2=dg2.sum(axis=0), gf=dgf.sum(axis=0), sink=dsink, loss=loss8[0, 0])
    return dx0, small


BIG = ("wg1", "wu1", "wd1", "w_in", "w_out", "wg2", "wu2", "wd2")
GATHER_GROUPS = (("w_in",), ("w_out",), ("wg2", "wu2", "wd2"))


class _Comm:
    def __init__(self, shards):
        x, y, c = _mesh_pos()
        self.me = (2 * x + y).astype(jnp.int32).reshape(1)
        self.shards = shards
        self.tokens = []
        self.waiting = {}
        self.groups = []
        fulls = {k: _cast_place(self.me, shards[k], f"cast_{k}") for k in BIG}
        first = ("wg1", "wu1", "wd1")
        self.full = dict(zip(first, _gather_weights([fulls[k] for k in first])))
        dep = self.full["wd1"]
        for gi, names in enumerate(GATHER_GROUPS):
            plan = _gather_plan(len(names))
            send, recv, bufs, tok = _push_start(f"gather_start_{gi}", [fulls[k] for k in names], 3 * len(names), plan, dep)
            self.tokens.append(tok)
            dep = tok
            for k in names:
                self.waiting[k] = (gi, names, send, recv, bufs, plan)

    def order(self, a):
        for tok in self.tokens:
            a = a + tok[0, 0]
        self.tokens = []
        return a

    def dep(self):
        return self.tokens[-1] if self.tokens else None

    def weight(self, name, after):
        if name in self.waiting:
            gi, names, send, recv, bufs, plan = self.waiting[name]
            for k, buf in zip(names, _push_wait(f"gather_wait_{gi}", send, recv, bufs, plan, after)):
                self.full[k] = buf
                del self.waiting[k]
        full = self.full[name]
        return full.reshape(N_CHIPS * full.shape[1], full.shape[2])

    def ready(self, grads, after):
        names = list(grads)
        f32s, b16s = [], []
        for k in names:
            gf, gb = grads[k]
            f32s.append(gf.reshape((N_CHIPS,) + self.shards[k].shape))
            b16s.append(gb.reshape((N_CHIPS,) + self.shards[k].shape))
        n = len(names)
        lands = [lax.empty((N_CHIPS - 1,) + self.shards[k].shape, BF16) for k in names]
        plan = _grad_push_plan(n)
        gi = len(self.groups)
        send, recv, bufs, tok = _push_start(f"grad_start_{gi}", b16s + lands, 3 * n, plan, after)
        self.tokens.append(tok)
        self.groups.append((names, f32s, send, recv, bufs, plan))

    def finish(self):
        out, swaps = {}, []
        after = self.tokens[-1]
        for gi, (names, f32s, send, recv, bufs, plan) in enumerate(self.groups):
            n = len(names)
            bufs = _push_wait(f"grad_wait_{gi}", send, recv, bufs, plan, after)
            mine = [_sum_own(self.me, f32s[i], bufs[n + i], f"sum_{k}") for i, k in enumerate(names)]
            lands = [lax.empty(p.shape, F32) for p in mine]
            send2, recv2, both, after = _push_start(f"swap_start_{gi}", mine + lands, n, _swap_plan(n), mine[-1])
            swaps.append((names, send2, recv2, both))
        for gi, (names, send2, recv2, both) in enumerate(swaps):
            n = len(names)
            both = _push_wait(f"swap_wait_{gi}", send2, recv2, both, _swap_plan(n), after)
            for i, k in enumerate(names):
                out[k] = (both[i], both[n + i])
        return out


def kernel(x, positions, norm_ffn1, w_gate1, w_up1, w_down1, norm_mix, w_in, a_sink, w_out, norm_ffn2, w_gate2, w_up2, w_down2, norm_final, loss_target, m_norm_ffn1, m_w_gate1, m_w_up1, m_w_down1, m_norm_mix, m_w_in, m_a_sink, m_w_out, m_norm_ffn2, m_w_gate2, m_w_up2, m_w_down2, m_norm_final, v_norm_ffn1, v_w_gate1, v_w_up1, v_w_down1, v_norm_mix, v_w_in, v_a_sink, v_w_out, v_norm_ffn2, v_w_gate2, v_w_up2, v_w_down2, v_norm_final):
    T, D = x.shape[1], x.shape[2]
    flip = ("wg1", "wu1", "w_in", "wg2", "wu2")

    def rows(k, a):
        return a[0].T if k in flip else a[0]

    given = dict(wg1=(w_gate1, m_w_gate1, v_w_gate1), wu1=(w_up1, m_w_up1, v_w_up1), wd1=(w_down1, m_w_down1, v_w_down1),
                 w_in=(w_in, m_w_in, v_w_in), w_out=(w_out, m_w_out, v_w_out), wg2=(w_gate2, m_w_gate2, v_w_gate2),
                 wu2=(w_up2, m_w_up2, v_w_up2), wd2=(w_down2, m_w_down2, v_w_down2))
    shards = {k: rows(k, given[k][0]) for k in BIG}

    comm = _Comm(shards)

    norms = (norm_ffn1, norm_mix, norm_ffn2, norm_final.reshape(1, D))
    grad_x, small = _local_step(x[0], positions[0], loss_target[0], norms, a_sink[0], comm)

    partial = comm.finish()

    def pad_row(a):
        a = a.reshape(-1)
        return jnp.pad(a, (0, D - a.shape[0]))

    row4 = pad_row(jnp.concatenate([small["sink"], small["loss"].reshape(1)]))
    vec = jnp.stack([small["g1"], small["gm"], small["g2"], small["gf"], row4] + [jnp.zeros((D,), F32)] * 3, axis=0)
    red = _allreduce_small(vec)
    loss = red[4, 8]
    g_small = jnp.stack([red[0], red[1], red[2], red[3], pad_row(red[4, 0:8])] + [jnp.zeros((D,), F32)] * 3, axis=0)

    def small_stack(a1, am, a2, af, ask):
        return jnp.stack([pad_row(a1), pad_row(am), pad_row(a2), pad_row(af), pad_row(ask)] + [jnp.zeros((D,), F32)] * 3, axis=0)

    w_small = small_stack(norm_ffn1, norm_mix, norm_ffn2, norm_final, a_sink)
    m_small = small_stack(m_norm_ffn1, m_norm_mix, m_norm_ffn2, m_norm_final, m_a_sink)
    v_small = small_stack(v_norm_ffn1, v_norm_mix, v_norm_ffn2, v_norm_final, v_a_sink)
    live = small_stack(jnp.ones_like(norm_ffn1), jnp.ones_like(norm_mix), jnp.ones_like(norm_ffn2), jnp.ones_like(norm_final), jnp.ones_like(a_sink))
    v_small = jnp.where(live > 0, v_small, 1.0)

    upd = {}
    for k in BIG:
        outs = _adamw(shards[k], partial[k][0], partial[k][1], rows(k, given[k][1]), rows(k, given[k][2]), f"adamw_{k}")
        upd[k] = tuple((a.T if k in flip else a)[None] for a in outs)
    _, ds_, nms_, nvs_ = _adamw(w_small, g_small, jnp.zeros_like(g_small), m_small, v_small, "adamw_small")

    def small_out(arr):
        return [arr[0].reshape(1, D), arr[1].reshape(1, D), arr[2].reshape(1, D), arr[3], arr[4, 0:8].reshape(1, 8)]

    gs_, dss, nmss, nvss = small_out(g_small), small_out(ds_), small_out(nms_), small_out(nvs_)

    def ordered(i):
        sm = (gs_, dss, nmss, nvss)[i]
        return [sm[0], upd["wg1"][i], upd["wu1"][i], upd["wd1"][i], sm[1], upd["w_in"][i], sm[4], upd["w_out"][i], sm[2],
                upd["wg2"][i], upd["wu2"][i], upd["wd2"][i], sm[3]]

    return (loss, grad_x[None], *ordered(0), *ordered(1), *ordered(2), *ordered(3))
```

```python
import jax
import jax.numpy as jnp
from jax import lax
from jax.experimental import pallas as pl
from jax.experimental.pallas import tpu as pltpu

F32 = jnp.float32
BF16 = jnp.bfloat16

HEAD_DIM = 64
LANES = 128
SUBLANES = 8
A_Q_W, A_KV_W, B_W = 512, 128, 512
A_HALF_WINDOW = 128
B_PATTERNS = ((128, 1), (512, 4), (2048, 16))
ROPE_THETA = 10000.0
NORM_EPS = 1e-6
FFN_RES_WEIGHT = 0.5
ADAM_LR, ADAM_B1, ADAM_B2, ADAM_EPS, ADAM_WD, ADAM_STEP = 0.001, 0.9, 0.999, 1e-08, 0.01, 10
N_CHIPS = 4
N_DEV = 8
QB = 128
SHORT_SEQ = 512
NEG = -1e30
VMEM_LIMIT = 56 * 1024 * 1024
MESH = pl.DeviceIdType.MESH
ANY = pl.BlockSpec(memory_space=pl.ANY)


def _params(sem=None):
    return pltpu.CompilerParams(dimension_semantics=sem, vmem_limit_bytes=VMEM_LIMIT)


def _sds(shape, dtype):
    return jax.ShapeDtypeStruct(tuple(shape), dtype)


def _dot(a, b):
    return jnp.dot(a, b, preferred_element_type=F32)


def _dot_nt(a, b):
    return lax.dot_general(a, b, (((1,), (1,)), ((), ())), preferred_element_type=F32)


def _dot_tn(a, b):
    return lax.dot_general(a, b, (((0,), (0,)), ((), ())), preferred_element_type=F32)


def _rms_stats(x):
    r = lax.rsqrt(jnp.mean(x * x, axis=-1, keepdims=True) + NORM_EPS)
    return x * r, r


def _rms_bwd(dh, x, g):
    xhat, r = _rms_stats(x)
    dxn = dh * g
    dx = r * (dxn - xhat * jnp.mean(dxn * xhat, axis=-1, keepdims=True))
    tm, d = x.shape
    dg = (dh * xhat).reshape(tm // SUBLANES, SUBLANES, d).sum(axis=0)
    return dx, dg


def _sigmoid(x):
    return 1.0 / (1.0 + jnp.exp(-x))


def _swap32(t):
    n = t.shape[-1]
    lane = lax.broadcasted_iota(jnp.int32, t.shape, t.ndim - 1)
    return jnp.where((lane % HEAD_DIM) < HEAD_DIM // 2, pltpu.roll(t, n - HEAD_DIM // 2, axis=t.ndim - 1),
                     pltpu.roll(t, HEAD_DIM // 2, axis=t.ndim - 1))


def _cast_place(me_arr, w, name):
    R, C = w.shape
    tr = R // 2 if (R // 2) % 16 == 0 else R

    def body(me_ref, w_ref, o_ref):
        o_ref[...] = w_ref[...].astype(BF16)

    grid_spec = pltpu.PrefetchScalarGridSpec(
        num_scalar_prefetch=1, grid=(R // tr,), in_specs=[pl.BlockSpec((tr, C), lambda t, me: (t, 0))],
        out_specs=pl.BlockSpec((None, tr, C), lambda t, me: (me[0], t, 0)))
    return pl.pallas_call(body, name=name, grid_spec=grid_spec, out_shape=_sds((N_CHIPS, R, C), BF16),
                          compiler_params=_params(("parallel",)))(me_arr, w)


HBM = pl.BlockSpec(memory_space=pltpu.HBM)
SEM = pl.BlockSpec(memory_space=pltpu.SEMAPHORE)


def _push_start(name, bufs, ncopies, plan, after):
    nb = len(bufs)

    def body(*refs):
        send, recv, token = refs[nb + 1], refs[nb + 2], refs[-1]
        for i, (src, dst, dev) in enumerate(plan(refs[:nb])):
            pltpu.make_async_remote_copy(src_ref=src, dst_ref=dst, send_sem=send.at[i], recv_sem=recv.at[i],
                                         device_id=dev, device_id_type=MESH).start()
        token[...] = jnp.zeros_like(token)

    outs = pl.pallas_call(
        body, name=name,
        out_shape=(pltpu.SemaphoreType.DMA((ncopies,)), pltpu.SemaphoreType.DMA((ncopies,)), *[pltpu.HBM(b.shape, b.dtype) for b in bufs],
                   _sds((SUBLANES, LANES), F32)),
        in_specs=[HBM] * nb + [ANY], out_specs=(SEM, SEM, *([HBM] * nb), pl.BlockSpec(memory_space=pltpu.VMEM)),
        input_output_aliases={i: 2 + i for i in range(nb)},
        compiler_params=pltpu.CompilerParams(has_side_effects=pltpu.SideEffectType.DATAFLOW_SIDE_EFFECTING),
    )(*[pltpu.with_memory_space_constraint(b, pltpu.HBM) for b in bufs], after)
    return outs[0], outs[1], list(outs[2:2 + nb]), outs[-1]


def _push_wait(name, send, recv, bufs, plan, after):
    nb = len(bufs)

    def body(*refs):
        send_ref, recv_ref = refs[nb], refs[nb + 1]
        for i, (src, dst, dev) in enumerate(plan(refs[:nb])):
            cp = pltpu.make_async_remote_copy(src_ref=src, dst_ref=dst, send_sem=send_ref.at[i], recv_sem=recv_ref.at[i],
                                              device_id=dev, device_id_type=MESH)
            cp.wait_send()
            cp.wait_recv()

    afters = list(after) if isinstance(after, (list, tuple)) else [after]
    outs = pl.pallas_call(
        body, name=name, out_shape=tuple(pltpu.HBM(b.shape, b.dtype) for b in bufs),
        in_specs=[HBM] * nb + [SEM, SEM] + [ANY] * len(afters), out_specs=tuple([HBM] * nb),
        input_output_aliases={i: i for i in range(nb)},
        compiler_params=pltpu.CompilerParams(has_side_effects=pltpu.SideEffectType.DATAFLOW_SIDE_EFFECTING),
    )(*bufs, send, recv, *afters)
    return list(outs)


def _mesh_pos():
    return lax.axis_index("x"), lax.axis_index("y"), lax.axis_index("c")


def _chip_peers(x, y, c):
    return [((1 - x, y, c), 2 * (1 - x) + y), ((x, 1 - y, c), 2 * x + (1 - y)), ((1 - x, 1 - y, c), 2 * (1 - x) + (1 - y))]


def _gather_plan(n):
    def plan(refs):
        x, y, c = _mesh_pos()
        me = 2 * x + y
        return [(refs[k].at[me], refs[k].at[me], dev) for k in range(n) for dev, _ in _chip_peers(x, y, c)]
    return plan


def _rows_of(shape, who, quarter=None):
    r2 = shape[1] // 2
    if quarter is None:
        return pl.ds(pl.multiple_of(who * r2, 16), r2)
    return pl.ds(pl.multiple_of(who * r2 + quarter * (r2 // 2), 16), r2 // 2)


def _neighbour_plan(shapes):
    def plan(refs):
        x, y, c = _mesh_pos()
        me = 2 * x + y
        return [(refs[k].at[me, _rows_of(shp, c), :], refs[k].at[me, _rows_of(shp, c), :], dev)
                for k, shp in enumerate(shapes) for dev in ((1 - x, y, c), (x, 1 - y, c))]
    return plan


def _gather_forward(fulls):
    n = len(fulls)

    def body(*refs):
        ins, outs = refs[:n], refs[n:2 * n]
        ici_send, ici_recv, d2d_send, d2d_recv = refs[2 * n:]
        x, y, c = _mesh_pos()
        cx, cy, cd = 2 * (1 - x) + y, 2 * x + (1 - y), 2 * (1 - x) + (1 - y)
        sibling, x_nbr, y_nbr = (x, y, 1 - c), (1 - x, y, c), (x, 1 - y, c)
        started = []

        def push(src, dst, send, recv, dev):
            cp = pltpu.make_async_remote_copy(src_ref=src, dst_ref=dst, send_sem=send, recv_sem=recv, device_id=dev, device_id_type=MESH)
            cp.start()
            started.append(cp)

        def arrived(blk, send, recv):
            pltpu.make_async_remote_copy(src_ref=blk, dst_ref=blk, send_sem=send, recv_sem=recv, device_id=sibling,
                                         device_id_type=MESH).wait_recv()

        for k in range(n):
            shp = fulls[k].shape
            for j, chip in enumerate((cx, cy)):
                push(ins[k].at[chip, _rows_of(shp, c), :], outs[k].at[chip, _rows_of(shp, c), :],
                     d2d_send.at[3 * k + j], d2d_recv.at[3 * k + j], sibling)
            push(ins[k].at[cx, _rows_of(shp, c, 0), :], outs[k].at[cx, _rows_of(shp, c, 0), :], ici_send.at[2 * k], ici_recv.at[2 * k], y_nbr)
            push(ins[k].at[cy, _rows_of(shp, c, 1), :], outs[k].at[cy, _rows_of(shp, c, 1), :], ici_send.at[2 * k + 1], ici_recv.at[2 * k + 1],
                 x_nbr)
        for k in range(n):
            shp = fulls[k].shape
            for q in (0, 1):
                arrived(outs[k].at[cd, _rows_of(shp, c, q), :], ici_send.at[2 * k + q], ici_recv.at[2 * k + q])
            blk = outs[k].at[cd, _rows_of(shp, c), :]
            push(blk, blk, d2d_send.at[3 * k + 2], d2d_recv.at[3 * k + 2], sibling)
        for k in range(n):
            for j, chip in enumerate((cx, cy, cd)):
                arrived(outs[k].at[chip, _rows_of(fulls[k].shape, 1 - c), :], d2d_send.at[3 * k + j], d2d_recv.at[3 * k + j])
        for cp in started:
            cp.wait_send()

    return pl.pallas_call(
        body, name="gather_forward", out_shape=[_sds(f.shape, BF16) for f in fulls],
        in_specs=[ANY] * n, out_specs=[ANY] * n, input_output_aliases={k: k for k in range(n)},
        scratch_shapes=[pltpu.SemaphoreType.DMA((n * 2,))] * 2 + [pltpu.SemaphoreType.DMA((n * 3,))] * 2,
        compiler_params=_params())(*fulls)


def _resident(shape):
    return pl.BlockSpec(shape, lambda i: (0,) * len(shape), pipeline_mode=pl.Buffered(1))


FFN_FWD_CHUNK = 256
FFN_DX_CHUNK = 512


def _chunks(n, step):
    return [(c0, min(step, n - c0)) for c0 in range(0, n, step)]


def _two_phase(chunks, first, second):
    held = {}
    for ci, ch in enumerate(chunks):
        held[ci] = first(*ch)
        if ci >= 1:
            second(*chunks[ci - 1], held.pop(ci - 1))
    last = len(chunks) - 1
    second(*chunks[last], held.pop(last))


def _ffn_fwd(x, g, wgt, wut, wd, name, tm=512):
    T, D = x.shape
    F = wd.shape[0]

    def body(x_ref, g_ref, wg_ref, wu_ref, wd_ref, xo_ref, h_ref, gate_ref, up_ref, act_ref):
        xv = x_ref[...]
        xhat, _ = _rms_stats(xv)
        h = (xhat * g_ref[...]).astype(BF16)
        h_ref[...] = h
        acc = []

        def first(c0, cw):
            return _dot_nt(h, wg_ref[c0:c0 + cw, :]), _dot_nt(h, wu_ref[c0:c0 + cw, :])

        def second(c0, cw, gate_up):
            gate, up = gate_up
            act = ((gate * _sigmoid(gate)) * up).astype(BF16)
            gate_ref[:, c0:c0 + cw] = gate.astype(BF16)
            up_ref[:, c0:c0 + cw] = up.astype(BF16)
            act_ref[:, c0:c0 + cw] = act
            d = _dot(act, wd_ref[c0:c0 + cw, :])
            acc[:] = [d if not acc else acc[0] + d]

        _two_phase(_chunks(F, FFN_FWD_CHUNK), first, second)
        xo_ref[...] = xv + FFN_RES_WEIGHT * acc[0]

    row = pl.BlockSpec((tm, D), lambda i: (i, 0))
    saved = pl.BlockSpec((tm, F), lambda i: (i, 0))
    return pl.pallas_call(
        body, name=name, grid=(T // tm,),
        in_specs=[row, pl.BlockSpec((1, D), lambda i: (0, 0)), _resident(wgt.shape), _resident(wut.shape), _resident(wd.shape)],
        out_specs=[row, row, saved, saved, saved],
        out_shape=[_sds((T, D), F32), _sds((T, D), BF16), _sds((T, F), BF16), _sds((T, F), BF16), _sds((T, F), BF16)],
        compiler_params=_params(("parallel",)))(x, g, wgt, wut, wd)


def _ffn_dx(dxo, x, g, gate_s, up_s, wgt, wut, wd, name, tm=256):
    T, D = x.shape
    F = wd.shape[0]

    def body(dxo_ref, x_ref, g_ref, gate_ref, up_ref, wg_ref, wu_ref, wd_ref, dx_ref, dff_ref, dgate_ref, dup_ref, dg_ref):
        @pl.when(pl.program_id(0) == 0)
        def _():
            dg_ref[...] = jnp.zeros_like(dg_ref)

        d = (FFN_RES_WEIGHT * dxo_ref[...]).astype(BF16)
        dff_ref[...] = d
        dh = []

        def first(c0, cw):
            return _dot_nt(d, wd_ref[c0:c0 + cw, :])

        def second(c0, cw, da):
            gate = gate_ref[:, c0:c0 + cw].astype(F32)
            up = up_ref[:, c0:c0 + cw].astype(F32)
            s = _sigmoid(gate)
            silu = gate * s
            dup = (da * silu).astype(BF16)
            dgate = (da * up * (s * (1.0 + gate * (1.0 - s)))).astype(BF16)
            dgate_ref[:, c0:c0 + cw] = dgate
            dup_ref[:, c0:c0 + cw] = dup
            t = _dot(dgate, wg_ref[c0:c0 + cw, :]) + _dot(dup, wu_ref[c0:c0 + cw, :])
            dh[:] = [t if not dh else dh[0] + t]

        _two_phase(_chunks(F, FFN_DX_CHUNK), first, second)
        dxn, dg = _rms_bwd(dh[0], x_ref[...], g_ref[...])
        dg_ref[...] += dg
        dx_ref[...] = dxo_ref[...] + dxn

    row = pl.BlockSpec((tm, D), lambda i: (i, 0))
    saved = pl.BlockSpec((tm, F), lambda i: (i, 0))
    return pl.pallas_call(
        body, name=name, grid=(T // tm,),
        in_specs=[row, row, pl.BlockSpec((1, D), lambda i: (0, 0)), saved, saved, _resident(wgt.shape), _resident(wut.shape),
                  _resident(wd.shape)],
        out_specs=[row, row, saved, saved, pl.BlockSpec((SUBLANES, D), lambda i: (0, 0))],
        out_shape=[_sds((T, D), F32), _sds((T, D), BF16), _sds((T, F), BF16), _sds((T, F), BF16), _sds((SUBLANES, D), F32)],
        compiler_params=_params(("arbitrary",)))(dxo, x, g, gate_s, up_s, wgt, wut, wd)


def _tn(a, b, mb, name, tk=2048, dep=None):
    T, M = a.shape
    N = b.shape[1]
    nt = T // tk

    def body(a_ref, b_ref, *refs):
        o_ref, ob_ref = refs[-2:]

        @pl.when(pl.program_id(1) == 0)
        def _():
            o_ref[...] = jnp.zeros_like(o_ref)

        o_ref[...] += _dot_tn(a_ref[...].astype(BF16), b_ref[...].astype(BF16))

        @pl.when(pl.program_id(1) == nt - 1)
        def _():
            ob_ref[...] = o_ref[...].astype(BF16)

    o_spec = pl.BlockSpec((mb, N), lambda g, t: (g, 0))
    return pl.pallas_call(
        body, name=name, grid=(M // mb, nt),
        in_specs=[pl.BlockSpec((tk, mb), lambda g, t: (t, g)), pl.BlockSpec((tk, N), lambda g, t: (t, 0))] + ([ANY] if dep is not None else []),
        out_specs=[o_spec, o_spec], out_shape=[_sds((M, N), F32), _sds((M, N), BF16)],
        compiler_params=_params(("parallel", "arbitrary")))(a, b, *([dep] if dep is not None else []))


def _rope_tables(pos_col, inv_freq):
    T = pos_col.shape[0]

    def body(p_ref, f_ref, c_ref, s_ref):
        ang = p_ref[...].astype(F32) * f_ref[...]
        lane = lax.broadcasted_iota(jnp.int32, ang.shape, 1)
        c_ref[...] = jnp.cos(ang)
        sn = jnp.sin(ang)
        s_ref[...] = jnp.where((lane % HEAD_DIM) < HEAD_DIM // 2, -sn, sn)

    tm = 1024
    return pl.pallas_call(
        body, name="rope_tables", grid=(T // tm,),
        in_specs=[pl.BlockSpec((tm, 1), lambda i: (i, 0)), pl.BlockSpec((1, LANES), lambda i: (0, 0))],
        out_specs=[pl.BlockSpec((tm, LANES), lambda i: (i, 0))] * 2,
        out_shape=[_sds((T, LANES), F32)] * 2, compiler_params=_params(("parallel",)))(pos_col, inv_freq)


def _deinterleave(scr, out_ref, d, tm, nblk):
    for r in range(d):
        for cb in range(nblk):
            out_ref[r, :, cb * LANES:(cb + 1) * LANES] = scr[cb, pl.ds(r, tm // d, stride=d), :].astype(out_ref.dtype)


def _interleave(in_ref, scr, d, tm, nblk):
    for r in range(d):
        for cb in range(nblk):
            scr[cb, pl.ds(r, tm // d, stride=d), :] = in_ref[r, :, cb * LANES:(cb + 1) * LANES].astype(F32)


def _proj_rope(x, g, w_in, cos, sin, tm=512):
    T, D = x.shape
    dils = [d for _, d in B_PATTERNS if d > 1]
    nbb = B_W // LANES
    scale = HEAD_DIM ** -0.5
    cuts = [0, A_Q_W, A_Q_W + A_KV_W, A_Q_W + 2 * A_KV_W, A_Q_W + 2 * A_KV_W + B_W, A_Q_W + 2 * A_KV_W + 2 * B_W,
            A_Q_W + 2 * A_KV_W + 3 * B_W]

    def body(x_ref, g_ref, w_ref, c_ref, s_ref, h_ref, aq_ref, ak_ref, av_ref, *rest):
        b_refs, scr = rest[:-1], rest[-1]
        xhat, _ = _rms_stats(x_ref[...])
        h = (xhat * g_ref[...]).astype(BF16)
        h_ref[...] = h
        cs, sn = c_ref[...], s_ref[...]

        def project(idx, ref, rope, mult, which):
            return _dot_nt(h, w_ref[cuts[idx]:cuts[idx + 1], :])

        def finish(idx, ref, rope, mult, which, whole):
            for cb in range((cuts[idx + 1] - cuts[idx]) // LANES):
                p = whole[:, cb * LANES:(cb + 1) * LANES]
                if rope:
                    p = p * cs + _swap32(p) * sn
                if mult != 1.0:
                    p = p * mult
                ref[:, cb * LANES:(cb + 1) * LANES] = p.astype(BF16)
                if which is not None:
                    scr[which, cb] = p
            if which is not None:
                for di, d in enumerate(dils):
                    _deinterleave(scr.at[which], b_refs[3 * (di + 1) + which], d, tm, nbb)

        _two_phase([(0, aq_ref, True, scale, None), (1, ak_ref, True, 1.0, None), (2, av_ref, False, 1.0, None),
                    (3, b_refs[0], True, scale, 0), (4, b_refs[1], True, 1.0, 1), (5, b_refs[2], False, 1.0, 2)], project, finish)

    row = lambda w: pl.BlockSpec((tm, w), lambda i: (i, 0))
    out_specs = [row(D), row(A_Q_W), row(A_KV_W), row(A_KV_W)] + [row(B_W)] * 3
    out_shape = [_sds((T, D), BF16), _sds((T, A_Q_W), BF16), _sds((T, A_KV_W), BF16), _sds((T, A_KV_W), BF16)] + [_sds((T, B_W), BF16)] * 3
    for d in dils:
        out_specs += [pl.BlockSpec((d, tm // d, B_W), lambda i: (0, i, 0))] * 3
        out_shape += [_sds((d, T // d, B_W), BF16)] * 3
    return pl.pallas_call(
        body, name="proj_rope", grid=(T // tm,),
        in_specs=[row(D), pl.BlockSpec((1, D), lambda i: (0, 0)), pl.BlockSpec(w_in.shape, lambda i: (0, 0)), row(LANES), row(LANES)],
        out_specs=out_specs, out_shape=out_shape, scratch_shapes=[pltpu.VMEM((3, nbb, tm, LANES), F32)],
        compiler_params=_params(("parallel",)))(x, g, w_in, cos, sin)


def _band_bias(rel, qb, kw, hw):
    ri = lax.broadcasted_iota(jnp.int32, (2 * qb, kw), 0) & (qb - 1)
    ci = lax.broadcasted_iota(jnp.int32, (2 * qb, kw), 1)
    return jnp.where(jnp.abs(ri + rel - ci) <= hw, 0.0, NEG).astype(F32)


def _stack_heads(x, lo):
    z = jnp.zeros_like(x)
    return jnp.concatenate([jnp.where(lo, x, z), jnp.where(lo, z, x)], axis=0)


def _unstack_heads(y, lo):
    qb = y.shape[0] // 2
    return jnp.where(lo, y[:qb], y[qb:])


def _band_setup(bias_scr, qb, kw, hw):
    if bias_scr is not None:
        for i in range(3):
            bias_scr[i] = _band_bias(i * hw, qb, kw, hw)


def _band_window(bias_scr, qs, L, qb, kw, hw):
    ws = pl.multiple_of(jnp.clip(qs - hw, 0, L - kw), 64)
    if bias_scr is None:
        return ws, _band_bias(qs - ws, qb, kw, hw)
    return ws, bias_scr[lax.shift_right_logical(qs - ws, hw.bit_length() - 1)]


def _dup_kv_head(src_ref, dst_ref, head, L):
    step = min(L, 1024)
    for r0 in range(0, L, step):
        xf = src_ref[r0:r0 + step, :].astype(F32)
        lane = lax.broadcasted_iota(jnp.int32, xf.shape, 1)
        keep = jnp.logical_xor(lane < HEAD_DIM, head == 1)
        dst_ref[r0:r0 + step, :] = jnp.where(keep, xf, pltpu.roll(xf, HEAD_DIM, axis=1)).astype(dst_ref.dtype)


def _attn_fwd(q, k, v, sink, hw, gqa, out_dtype, name, qb=QB, blocks_per_step=8):
    NB, L, Cq = q.shape
    Ls = min(L, 2048)
    kw = min(qb + 2 * hw, L)
    tables = L >= qb + 2 * hw
    unroll = min(blocks_per_step, Ls // qb)
    nlb = 1 if (gqa or L > SHORT_SEQ) else Cq // LANES

    def body(sink_ref, q_ref, k_ref, v_ref, o_ref, lse_ref, *scr):
        b, s_idx = pl.program_id(1), pl.program_id(2)
        bias_scr = scr[0] if tables else None
        _band_setup(bias_scr, qb, kw, hw)
        if gqa:
            kd, vd = scr[-2:]

            @pl.when(s_idx == 0)
            def _():
                _dup_kv_head(k_ref, kd, b // 2, L)
                _dup_kv_head(v_ref, vd, b // 2, L)
        else:
            kd, vd = k_ref, v_ref
        lane = lax.broadcasted_iota(jnp.int32, (qb, LANES), 1)
        lo = lane < HEAD_DIM
        if gqa:
            row = lax.broadcasted_iota(jnp.int32, (2 * qb, 1), 0)
            sk = jnp.where(row < qb, sink_ref[2 * b], sink_ref[2 * b + 1])

        def block(ql, col):
            qs = s_idx * Ls + ql
            ws, bias = _band_window(bias_scr, qs, L, qb, kw, hw)
            return ws, _dot_nt(_stack_heads(q_ref[pl.ds(ql, qb), col], lo), kd[pl.ds(ws, kw), col]) + bias

        def finish(ql, col, scores):
            ws, s = scores
            m = jnp.max(s, axis=-1, keepdims=True)
            if gqa:
                m = jnp.maximum(m, sk)
            p = jnp.exp(s - m)
            den = jnp.sum(p, axis=-1, keepdims=True)
            if gqa:
                den = den + jnp.exp(sk - m)
            o = _dot(p.astype(BF16), vd[pl.ds(ws, kw), col]) * (1.0 / den)
            o_ref[pl.ds(ql, qb), col] = _unstack_heads(o, lo).astype(o_ref.dtype)
            lse_ref[pl.ds(ql, qb), col] = _unstack_heads(m + jnp.log(den), lo)

        for lb in range(nlb):
            def step(n, carry, col=slice(lb * LANES, (lb + 1) * LANES)):
                _two_phase([(pl.multiple_of((n * unroll + u) * qb, qb), col) for u in range(unroll)], block, finish)
                return carry

            lax.fori_loop(0, Ls // (qb * unroll), step, 0)

    kv_map = (lambda r, b, s: (r, 0, 0)) if gqa else (lambda r, b, s: (r, 0, b))
    seg = pl.BlockSpec((None, Ls, nlb * LANES), lambda r, b, s: (r, s, b))
    return pl.pallas_call(
        body, name=name, grid=(NB, Cq // (nlb * LANES), L // Ls),
        in_specs=[pl.BlockSpec(memory_space=pltpu.SMEM), seg, pl.BlockSpec((None, L, nlb * LANES), kv_map),
                  pl.BlockSpec((None, L, nlb * LANES), kv_map)],
        out_specs=[seg, seg], out_shape=[_sds((NB, L, Cq), out_dtype), _sds((NB, L, Cq), F32)],
        scratch_shapes=([pltpu.VMEM((3, 2 * qb, kw), F32)] if tables else []) + ([pltpu.VMEM((L, LANES), BF16)] * 2 if gqa else []),
        compiler_params=_params(("parallel", "parallel", "arbitrary")))(sink, q, k, v)


def _attn_bwd(q, k, v, do, lse, delta, sink, hw, gqa, name, qb=QB, blocks_per_step=8):
    NB, L, Cq = q.shape
    Ck = k.shape[2]
    Ls = min(L, 2048)
    kw = min(qb + 2 * hw, L)
    reps = kw // LANES
    nseg = L // Ls
    scale = HEAD_DIM ** -0.5
    tables = L >= qb + 2 * hw
    unroll = min(blocks_per_step, Ls // qb)
    nlb = 1 if (gqa or L > SHORT_SEQ) else Cq // LANES

    def body(sink_ref, q_ref, do_ref, lse_ref, dl_ref, k_ref, v_ref, dq_ref, dk_ref, dv_ref, dsk_ref, *scr):
        b, s_idx = pl.program_id(1), pl.program_id(2)
        lane = lax.broadcasted_iota(jnp.int32, (qb, LANES), 1)
        lo = lane < HEAD_DIM
        bias_scr = scr[0] if tables else None
        _band_setup(bias_scr, qb, kw, hw)
        if gqa:
            kd, vd, dk_acc, dv_acc, dsk_acc = scr[-5:]

            @pl.when(s_idx == 0)
            def _():
                _dup_kv_head(k_ref, kd, b // 2, L)
                _dup_kv_head(v_ref, vd, b // 2, L)
                dk_acc[...] = jnp.zeros_like(dk_acc)
                dv_acc[...] = jnp.zeros_like(dv_acc)
                dsk_acc[...] = jnp.zeros_like(dsk_acc)

            @pl.when((s_idx == 0) & (b == 0))
            def _():
                dk_ref[...] = jnp.zeros_like(dk_ref)
                dv_ref[...] = jnp.zeros_like(dv_ref)
        else:
            kd, vd = k_ref, v_ref
            dk_acc, dv_acc = scr[-2:]

            @pl.when(s_idx == 0)
            def _():
                dk_acc[...] = jnp.zeros_like(dk_acc)
                dv_acc[...] = jnp.zeros_like(dv_acc)

        def block(ql, col):
            qs = s_idx * Ls + ql
            ws, bias = _band_window(bias_scr, qs, L, qb, kw, hw)
            qv, dov = q_ref[pl.ds(ql, qb), col], do_ref[pl.ds(ql, qb), col]
            lse, dl = lse_ref[pl.ds(ql, qb), col], dl_ref[pl.ds(ql, qb), col]
            kv_, vv = kd[pl.ds(ws, kw), col], vd[pl.ds(ws, kw), col]
            q2, do2 = _stack_heads(qv, lo), _stack_heads(dov, lo)
            return ws, q2, do2, lse, dl, _dot_nt(q2, kv_) + bias, _dot_nt(do2, vv)

        def finish(ql, col, held):
            ws, q2, do2, lse, dl, s, dp = held
            lse_sw, dl_sw = pltpu.roll(lse, HEAD_DIM, axis=1), pltpu.roll(dl, HEAD_DIM, axis=1)
            lse2 = jnp.concatenate([jnp.where(lo, lse, lse_sw), jnp.where(lo, lse_sw, lse)], axis=0)
            dl2 = jnp.concatenate([jnp.where(lo, dl, dl_sw), jnp.where(lo, dl_sw, dl)], axis=0)
            p = jnp.exp(s - jnp.tile(lse2, (1, reps)))
            ds = (p * (dp - jnp.tile(dl2, (1, reps)))).astype(BF16)
            dq_ref[pl.ds(ql, qb), col] = (_unstack_heads(_dot(ds, kd[pl.ds(ws, kw), col]), lo) * scale).astype(dq_ref.dtype)
            both = _dot_tn(jnp.concatenate([ds, p.astype(BF16)], axis=1), jnp.concatenate([q2, do2], axis=1))
            dk_acc[pl.ds(ws, kw), col] += both[:kw, :LANES]
            dv_acc[pl.ds(ws, kw), col] += both[kw:, LANES:]
            if gqa:
                sk = jnp.where(lo, sink_ref[2 * b], sink_ref[2 * b + 1])
                dsk_acc[...] += -jnp.exp(sk - lse) * dl

        for lb in range(nlb):
            def step(n, carry, col=slice(lb * LANES, (lb + 1) * LANES)):
                _two_phase([(pl.multiple_of((n * unroll + u) * qb, qb), col) for u in range(unroll)], block, finish)
                return carry

            lax.fori_loop(0, Ls // (qb * unroll), step, 0)

        if gqa:
            @pl.when(s_idx == nseg - 1)
            def _():
                step_rows = min(L, 1024)
                for r0 in range(0, L, step_rows):
                    lanek = lax.broadcasted_iota(jnp.int32, (step_rows, LANES), 1)
                    mine = jnp.logical_xor(lanek < HEAD_DIM, (b // 2) == 1)
                    for acc, ref in ((dk_acc, dk_ref), (dv_acc, dv_ref)):
                        a = acc[r0:r0 + step_rows, :]
                        ref[r0:r0 + step_rows, :] += jnp.where(mine, a + pltpu.roll(a, HEAD_DIM, axis=1), 0.0)
                dsk_ref[...] = dsk_acc[...].reshape(qb // SUBLANES, SUBLANES, LANES).sum(axis=0)
        else:
            dsk_ref[...] = jnp.zeros_like(dsk_ref)

            @pl.when(s_idx == nseg - 1)
            def _():
                dk_ref[...] = dk_acc[...].astype(dk_ref.dtype)
                dv_ref[...] = dv_acc[...].astype(dv_ref.dtype)

    kv_map = (lambda r, b, s: (r, 0, 0)) if gqa else (lambda r, b, s: (r, 0, b))
    seg = pl.BlockSpec((None, Ls, nlb * LANES), lambda r, b, s: (r, s, b))
    full = pl.BlockSpec((None, L, nlb * LANES), kv_map)
    scratch = [pltpu.VMEM((3, 2 * qb, kw), F32)] if tables else []
    if gqa:
        scratch += [pltpu.VMEM((L, LANES), BF16)] * 2 + [pltpu.VMEM((L, LANES), F32)] * 2 + [pltpu.VMEM((qb, LANES), F32)]
    else:
        scratch += [pltpu.VMEM((L, nlb * LANES), F32)] * 2
    kv_dtype = F32 if gqa else BF16
    return pl.pallas_call(
        body, name=name, grid=(NB, Cq // (nlb * LANES), nseg),
        in_specs=[pl.BlockSpec(memory_space=pltpu.SMEM), seg, seg, seg, seg, full, full],
        out_specs=[seg, full, full, pl.BlockSpec((None, None, SUBLANES, LANES), lambda r, b, s: (r, b, 0, 0))],
        out_shape=[_sds((NB, L, Cq), BF16), _sds((NB, L, Ck), kv_dtype), _sds((NB, L, Ck), kv_dtype),
                   _sds((NB, Cq // LANES, SUBLANES, LANES), F32)],
        scratch_shapes=scratch,
        compiler_params=_params(("arbitrary", "arbitrary", "arbitrary")))(sink, q, do, lse, delta, k, v)


def _merge_b(a_out, o1, l1, o4, l4, o16, l16, tm=512):
    T = a_out.shape[0]
    nbb = B_W // LANES

    def body(a_ref, o1_ref, l1_ref, o4_ref, l4_ref, o16_ref, l16_ref, cat_ref, lg1_ref, lg4_ref, lg16_ref, so, sl, slg):
        _interleave(o4_ref, so.at[0], 4, tm, nbb)
        _interleave(l4_ref, sl.at[0], 4, tm, nbb)
        _interleave(o16_ref, so.at[1], 16, tm, nbb)
        _interleave(l16_ref, sl.at[1], 16, tm, nbb)
        cat_ref[:, 0:A_Q_W] = a_ref[...]
        for cb in range(nbb):
            cols = slice(cb * LANES, (cb + 1) * LANES)
            os_ = (o1_ref[:, cols], so[0, cb], so[1, cb])
            ls_ = (l1_ref[:, cols], sl[0, cb], sl[1, cb])
            m = jnp.maximum(jnp.maximum(ls_[0], ls_[1]), ls_[2])
            es = [jnp.exp(l - m) for l in ls_]
            den = es[0] + es[1] + es[2]
            out = (es[0] * os_[0] + es[1] * os_[1] + es[2] * os_[2]) * (1.0 / den)
            lg = m + jnp.log(den)
            cat_ref[:, A_Q_W + cb * LANES:A_Q_W + (cb + 1) * LANES] = out.astype(BF16)
            lg1_ref[:, cols] = lg
            slg[cb] = lg
        _deinterleave(slg, lg4_ref, 4, tm, nbb)
        _deinterleave(slg, lg16_ref, 16, tm, nbb)

    row = lambda w: pl.BlockSpec((tm, w), lambda i: (i, 0))
    perm = lambda d: pl.BlockSpec((d, tm // d, B_W), lambda i: (0, i, 0))
    return pl.pallas_call(
        body, name="merge_patterns", grid=(T // tm,),
        in_specs=[row(A_Q_W), row(B_W), row(B_W), perm(4), perm(4), perm(16), perm(16)],
        out_specs=[row(A_Q_W + B_W), row(B_W), perm(4), perm(16)],
        out_shape=[_sds((T, A_Q_W + B_W), BF16), _sds((T, B_W), F32), _sds((4, T // 4, B_W), F32), _sds((16, T // 16, B_W), F32)],
        scratch_shapes=[pltpu.VMEM((2, nbb, tm, LANES), F32), pltpu.VMEM((2, nbb, tm, LANES), F32), pltpu.VMEM((nbb, tm, LANES), F32)],
        compiler_params=_params(("parallel",)))(a_out, o1, l1, o4, l4, o16, l16)


def _out_proj(x, cat, w_out, tm=512):
    T, D = x.shape

    def body(x_ref, c_ref, w_ref, o_ref):
        o_ref[...] = x_ref[...] + _dot(c_ref[...], w_ref[...])

    row = lambda w: pl.BlockSpec((tm, w), lambda i: (i, 0))
    return pl.pallas_call(
        body, name="out_proj", grid=(T // tm,), in_specs=[row(D), row(cat.shape[1]), pl.BlockSpec(w_out.shape, lambda i: (0, 0))],
        out_specs=row(D), out_shape=_sds((T, D), F32), compiler_params=_params(("parallel",)))(x, cat, w_out)


def _final_loss(x, g, target, tm=512):
    T, D = x.shape

    def body(x_ref, g_ref, t_ref, dx_ref, dg_ref, loss_ref):
        @pl.when(pl.program_id(0) == 0)
        def _():
            dg_ref[...] = jnp.zeros_like(dg_ref)
            loss_ref[...] = jnp.zeros_like(loss_ref)

        xv, gv = x_ref[...], g_ref[...]
        xhat, _ = _rms_stats(xv)
        err = xhat * gv - t_ref[...]
        loss_ref[...] += 0.5 * jnp.sum(jnp.sum(err * err, axis=-1, keepdims=True) * (1.0 / D), axis=0, keepdims=True)
        dx, dg = _rms_bwd(err * (1.0 / D), xv, gv)
        dx_ref[...] = dx
        dg_ref[...] += dg

    row = pl.BlockSpec((tm, D), lambda i: (i, 0))
    return pl.pallas_call(
        body, name="final_loss", grid=(T // tm,), in_specs=[row, pl.BlockSpec((1, D), lambda i: (0, 0)), row],
        out_specs=[row, pl.BlockSpec((SUBLANES, D), lambda i: (0, 0)), pl.BlockSpec((SUBLANES, LANES), lambda i: (0, 0))],
        out_shape=[_sds((T, D), F32), _sds((SUBLANES, D), F32), _sds((SUBLANES, LANES), F32)],
        compiler_params=_params(("arbitrary",)))(x, g, target)


def _dcat(dx, w_out, cat, tm=512):
    T, D = dx.shape
    C = cat.shape[1]
    nba, nbb = A_Q_W // LANES, B_W // LANES

    def body(dx_ref, w_ref, cat_ref, doa_ref, dla_ref, dob1_ref, dlb1_ref, dob4_ref, dlb4_ref, dob16_ref, dlb16_ref, sdo, sdl):
        dc = _dot_nt(dx_ref[...].astype(BF16), w_ref[...])
        ri = lax.broadcasted_iota(jnp.int32, (LANES, LANES), 0)
        ci = lax.broadcasted_iota(jnp.int32, (LANES, LANES), 1)
        same_head = ((ri // HEAD_DIM) == (ci // HEAD_DIM)).astype(BF16)
        for cb in range(C // LANES):
            cols = slice(cb * LANES, (cb + 1) * LANES)
            blk = dc[:, cols]
            prod = blk * cat_ref[:, cols].astype(F32)
            hi = prod.astype(BF16)
            lo_ = (prod - hi.astype(F32)).astype(BF16)
            dl = _dot(hi, same_head) + _dot(lo_, same_head)
            if cb < nba:
                doa_ref[:, cols] = blk.astype(BF16)
                dla_ref[:, cols] = dl
            else:
                bcols = slice((cb - nba) * LANES, (cb - nba + 1) * LANES)
                dob1_ref[:, bcols] = blk.astype(BF16)
                dlb1_ref[:, bcols] = dl
                sdo[cb - nba] = blk
                sdl[cb - nba] = dl
        _deinterleave(sdo, dob4_ref, 4, tm, nbb)
        _deinterleave(sdl, dlb4_ref, 4, tm, nbb)
        _deinterleave(sdo, dob16_ref, 16, tm, nbb)
        _deinterleave(sdl, dlb16_ref, 16, tm, nbb)

    row = lambda w: pl.BlockSpec((tm, w), lambda i: (i, 0))
    perm = lambda d: pl.BlockSpec((d, tm // d, B_W), lambda i: (0, i, 0))
    return pl.pallas_call(
        body, name="dcat", grid=(T // tm,), in_specs=[row(D), pl.BlockSpec(w_out.shape, lambda i: (0, 0)), row(C)],
        out_specs=[row(A_Q_W), row(A_Q_W), row(B_W), row(B_W), perm(4), perm(4), perm(16), perm(16)],
        out_shape=[_sds((T, A_Q_W), BF16), _sds((T, A_Q_W), F32), _sds((T, B_W), BF16), _sds((T, B_W), F32),
                   _sds((4, T // 4, B_W), BF16), _sds((4, T // 4, B_W), F32), _sds((16, T // 16, B_W), BF16), _sds((16, T // 16, B_W), F32)],
        scratch_shapes=[pltpu.VMEM((nbb, tm, LANES), F32)] * 2, compiler_params=_params(("parallel",)))(dx, w_out, cat)


def _rope_bwd_assemble(dqa, dka, dva, b1, b4, b16, cos, sin, tm=512):
    T = dqa.shape[0]
    nbb = B_W // LANES
    width = A_Q_W + 2 * A_KV_W + 3 * B_W

    def body(dqa_ref, dka_ref, dva_ref, q1, k1, v1, q4, k4, v4, q16, k16, v16, c_ref, s_ref, o_ref, scr):
        cs, sn = c_ref[...], s_ref[...]

        def unrope(t):
            return t * cs + _swap32(t * sn)

        col = 0
        for ref, rope in ((dqa_ref, True), (dka_ref, True), (dva_ref, False)):
            for cb in range(ref.shape[1] // LANES):
                t = ref[:, cb * LANES:(cb + 1) * LANES].astype(F32)
                o_ref[:, col:col + LANES] = (unrope(t) if rope else t).astype(BF16)
                col += LANES
        for which, (r1, r4, r16, rope) in enumerate(((q1, q4, q16, True), (k1, k4, k16, True), (v1, v4, v16, False))):
            _interleave(r4, scr.at[0], 4, tm, nbb)
            _interleave(r16, scr.at[1], 16, tm, nbb)
            for cb in range(nbb):
                t = r1[:, cb * LANES:(cb + 1) * LANES].astype(F32) + scr[0, cb] + scr[1, cb]
                o_ref[:, col:col + LANES] = (unrope(t) if rope else t).astype(BF16)
                col += LANES

    row = lambda w: pl.BlockSpec((tm, w), lambda i: (i, 0))
    perm = lambda d: pl.BlockSpec((d, tm // d, B_W), lambda i: (0, i, 0))
    return pl.pallas_call(
        body, name="rope_bwd", grid=(T // tm,),
        in_specs=[row(A_Q_W), row(A_KV_W), row(A_KV_W)] + [row(B_W)] * 3 + [perm(4)] * 3 + [perm(16)] * 3 + [row(LANES), row(LANES)],
        out_specs=row(width), out_shape=_sds((T, width), BF16), scratch_shapes=[pltpu.VMEM((2, nbb, tm, LANES), F32)],
        compiler_params=_params(("parallel",)))(dqa, dka, dva, *b1, *b4, *b16, cos, sin)


def _dh_norm(dproj, w_in, x, g, dres, tm=512):
    T, D = x.shape

    def body(dp_ref, w_ref, x_ref, g_ref, dr_ref, dx_ref, dg_ref):
        @pl.when(pl.program_id(0) == 0)
        def _():
            dg_ref[...] = jnp.zeros_like(dg_ref)

        dxn, dg = _rms_bwd(_dot(dp_ref[...], w_ref[...]), x_ref[...], g_ref[...])
        dg_ref[...] += dg
        dx_ref[...] = dr_ref[...] + dxn

    row = lambda w: pl.BlockSpec((tm, w), lambda i: (i, 0))
    return pl.pallas_call(
        body, name="dh_norm", grid=(T // tm,),
        in_specs=[row(dproj.shape[1]), pl.BlockSpec(w_in.shape, lambda i: (0, 0)), row(D), pl.BlockSpec((1, D), lambda i: (0, 0)), row(D)],
        out_specs=[row(D), pl.BlockSpec((SUBLANES, D), lambda i: (0, 0))],
        out_shape=[_sds((T, D), F32), _sds((SUBLANES, D), F32)], compiler_params=_params(("arbitrary",)))(dproj, w_in, x, g, dres)


def _grad_push_plan(n):
    def plan(refs):
        x, y, c = _mesh_pos()
        return [(refs[k].at[chip], refs[n + k].at[rel], dev) for k in range(n) for rel, (dev, chip) in enumerate(_chip_peers(x, y, c))]
    return plan


def _sum_own(me_arr, g, landed, name):
    ns, R, C = g.shape
    tr = R // 2 if (R // 2) % 16 == 0 else R

    def body(me_ref, g_ref, x_ref, o_ref):
        acc = g_ref[...]
        for rel in range(ns - 1):
            acc = acc + x_ref[rel].astype(F32)
        o_ref[...] = acc

    grid_spec = pltpu.PrefetchScalarGridSpec(
        num_scalar_prefetch=1, grid=(R // tr,),
        in_specs=[pl.BlockSpec((None, tr, C), lambda t, me: (me[0], t, 0)), pl.BlockSpec((ns - 1, tr, C), lambda t, me: (0, t, 0))],
        out_specs=pl.BlockSpec((tr, C), lambda t, me: (t, 0)))
    return pl.pallas_call(body, name=name, grid_spec=grid_spec, out_shape=_sds((R, C), F32),
                          compiler_params=_params(("parallel",)))(me_arr, g, landed)


def _swap_plan(n):
    def plan(refs):
        x, y, c = _mesh_pos()
        return [(refs[k], refs[n + k], (x, y, 1 - c)) for k in range(n)]
    return plan


def _allreduce_small(v):
    rows, W = v.shape

    def body(v_ref, o_ref, buf, send, recv):
        x, y, c = _mesh_pos()
        me = 4 * x + 2 * y + c
        cps = []
        for m in range(1, N_DEV):
            dev = (x ^ (m >> 2), y ^ ((m >> 1) & 1), c ^ (m & 1))
            cp = pltpu.make_async_remote_copy(src_ref=v_ref, dst_ref=buf.at[me], send_sem=send.at[m - 1], recv_sem=recv.at[m - 1],
                                              device_id=dev, device_id_type=MESH)
            cp.start()
            cps.append(cp)
        for m in range(1, N_DEV):
            pltpu.make_async_remote_copy(src_ref=v_ref, dst_ref=buf.at[me ^ m], send_sem=send.at[m - 1], recv_sem=recv.at[m - 1],
                                         device_id=(x, y, c), device_id_type=MESH).wait_recv()
        for cp in cps:
            cp.wait_send()
        buf[me] = v_ref[...]
        acc = buf[0]
        for i in range(1, N_DEV):
            acc = acc + buf[i]
        o_ref[...] = acc

    return pl.pallas_call(
        body, name="allreduce_small", out_shape=_sds((rows, W), F32),
        scratch_shapes=[pltpu.VMEM((N_DEV, rows, W), F32), pltpu.SemaphoreType.DMA((N_DEV - 1,)), pltpu.SemaphoreType.DMA((N_DEV - 1,))],
        compiler_params=_params())(v)


def _adamw(w, gp, gq, m, v, name):
    R, C = w.shape
    tr = R // 2 if (R // 2) % SUBLANES == 0 else R
    c1 = 1.0 / (1.0 - ADAM_B1 ** ADAM_STEP)
    c2 = 1.0 / (1.0 - ADAM_B2 ** ADAM_STEP)

    def body(w_ref, gp_ref, gq_ref, m_ref, v_ref, g_ref, d_ref, nm_ref, nv_ref):
        gv = gp_ref[...] + gq_ref[...]
        nm = ADAM_B1 * m_ref[...] + (1.0 - ADAM_B1) * gv
        nv = ADAM_B2 * v_ref[...] + (1.0 - ADAM_B2) * (gv * gv)
        g_ref[...] = gv
        d_ref[...] = -ADAM_LR * ((nm * c1) / (jnp.sqrt(nv * c2) + ADAM_EPS) + ADAM_WD * w_ref[...])
        nm_ref[...] = nm
        nv_ref[...] = nv

    blk = pl.BlockSpec((tr, C), lambda t: (t, 0))
    return pl.pallas_call(body, name=name, grid=(R // tr,), in_specs=[blk] * 5, out_specs=[blk] * 4,
                          out_shape=[_sds((R, C), F32)] * 4, compiler_params=_params(("parallel",)))(w, gp, gq, m, v)


def _rope(positions, after):
    inv_freq = 1.0 / (ROPE_THETA ** (jnp.arange(0, HEAD_DIM, 2, dtype=F32) / HEAD_DIM))
    inv_freq = jnp.tile(inv_freq, LANES // (HEAD_DIM // 2)).reshape(1, LANES) + after[0, 0]
    return _rope_tables(positions.reshape(-1, 1), inv_freq)


def _local_step(x, rope, target, norms, a_sink, comm):
    T, D = x.shape
    g1, gm, g2, gf = norms
    cos, sin = rope
    no_sink = jnp.zeros((2 * (B_W // LANES),), F32)
    W = {k: comm.weight(k, x) for k in ("wg1", "wu1", "wd1")}

    x1, h1, gate1, up1, act1 = _ffn_fwd(x, comm.order(g1), W["wg1"], W["wu1"], W["wd1"], "ffn1_fwd")
    W["w_in"] = comm.weight("w_in", x1)
    (h2, aq, ak, av, bq1, bk1, bv1, bq4, bk4, bv4, bq16, bk16, bv16) = _proj_rope(x1, gm, W["w_in"], cos, sin)
    a_out, a_lse = _attn_fwd(aq[None], ak[None], av[None], a_sink, A_HALF_WINDOW, True, BF16, "attn_a_fwd", qb=2 * QB, blocks_per_step=4)
    bqs = {1: (bq1[None], bk1[None], bv1[None]), 4: (bq4, bk4, bv4), 16: (bq16, bk16, bv16)}
    b_o, b_l = {}, {}
    for w, d in B_PATTERNS:
        q_, k_, v_ = bqs[d]
        b_o[d], b_l[d] = _attn_fwd(q_, k_, v_, no_sink, w // (2 * d), False, BF16, f"attn_b{d}_fwd")
    cat, lg1, lg4, lg16 = _merge_b(a_out[0], b_o[1][0], b_l[1][0], b_o[4], b_l[4], b_o[16], b_l[16])
    W["w_out"] = comm.weight("w_out", cat)
    x2 = _out_proj(x1, cat, W["w_out"])
    for k in ("wg2", "wu2", "wd2"):
        W[k] = comm.weight(k, x2)
    x3, h3, gate2, up2, act2 = _ffn_fwd(x2, g2, W["wg2"], W["wu2"], W["wd2"], "ffn2_fwd")

    dx3, dgf, loss8 = _final_loss(x3, gf, target)
    dx2, dff2, dgate2, dup2, dg2 = _ffn_dx(dx3, x2, g2, gate2, up2, W["wg2"], W["wu2"], W["wd2"], "ffn2_dx")
    fb = gate2.shape[1] // 2
    dwg2 = _tn(dgate2, h3, fb, "ffn2_dw_gate")
    dwu2 = _tn(dup2, h3, fb, "ffn2_dw_up")
    dwd2 = _tn(act2, dff2, fb, "ffn2_dw_down")
    comm.ready(dict(wg2=dwg2, wu2=dwu2, wd2=dwd2), dwd2[0])

    doa, dla, dob1, dlb1, dob4, dlb4, dob16, dlb16 = _dcat(dx2, W["w_out"], cat)
    dw_out = _tn(cat, dx2, cat.shape[1], "w_out_dw", dep=comm.dep())
    dqa, dka, dva, dsk = _attn_bwd(aq[None], ak[None], av[None], doa[None], a_lse, dla[None], comm.order(a_sink), A_HALF_WINDOW, True,
                                   "attn_a_bwd")
    bwd_in = {1: (dob1[None], lg1[None], dlb1[None]), 4: (dob4, lg4, dlb4), 16: (dob16, lg16, dlb16)}
    bg = {}
    for w, d in B_PATTERNS:
        q_, k_, v_ = bqs[d]
        do_, l_, dl_ = bwd_in[d]
        bg[d] = _attn_bwd(q_, k_, v_, do_, l_, dl_, no_sink, w // (2 * d), False, f"attn_b{d}_bwd")[:3]
    dproj = _rope_bwd_assemble(dqa[0], dka[0], dva[0], [t[0] for t in bg[1]], bg[4], bg[16], cos, sin)
    dw_in = _tn(dproj, h2, dproj.shape[1] // 2, "w_in_dw")
    comm.ready(dict(w_in=dw_in, w_out=dw_out), dw_in[0])
    dx1, dgm = _dh_norm(dproj, W["w_in"], x1, comm.order(gm), dx2)

    dx0, dff1, dgate1, dup1, dg1 = _ffn_dx(dx1, x, g1, gate1, up1, W["wg1"], W["wu1"], W["wd1"], "ffn1_dx")
    dwd1 = _tn(act1, dff1, fb, "ffn1_dw_down")
    comm.ready(dict(wd1=dwd1), dwd1[0])
    dwg1 = _tn(dgate1, h1, fb, "ffn1_dw_gate", dep=comm.dep())
    comm.ready(dict(wg1=dwg1), dwg1[0])
    dwu1 = _tn(dup1, h1, fb, "ffn1_dw_up", dep=comm.dep())
    comm.ready(dict(wu1=dwu1), dwu1[0])

    dsink = dsk[0, :, :, ::HEAD_DIM].sum(axis=1).reshape(-1)
    small = dict(g1=dg1.sum(axis=0), gm=dgm.sum(axis=0), g2=dg2.sum(axis=0), gf=dgf.sum(axis=0), sink=dsink, loss=loss8[0, 0])
    return dx0, small


BIG = ("wg1", "wu1", "wd1", "w_in", "w_out", "wg2", "wu2", "wd2")
GATHER_GROUPS = (("w_in",), ("w_out",), ("wg2", "wu2", "wd2"))


class _Comm:
    def __init__(self, shards, meanwhile):
        x, y, c = _mesh_pos()
        self.me = (2 * x + y).astype(jnp.int32).reshape(1)
        self.shards = shards
        self.tokens = []
        self.waiting = {}
        self.groups = []
        first = ("wg1", "wu1", "wd1")
        fulls = {k: _cast_place(self.me, shards[k], f"cast_{k}") for k in first}
        plan = _neighbour_plan([fulls[k].shape for k in first])
        send, recv, bufs, tok = _push_start("gather_first_start", [fulls[k] for k in first], 2 * len(first), plan, fulls["wd1"])
        self.side = meanwhile(tok)
        fulls.update({k: _cast_place(self.me, shards[k], f"cast_{k}") for k in BIG if k not in first})
        bufs = _push_wait("gather_first_wait", send, recv, bufs, plan, [fulls["wd2"], *self.side])
        self.full = dict(zip(first, _gather_forward(bufs)))
        dep = self.full["wd1"]
        for gi, names in enumerate(GATHER_GROUPS):
            plan = _gather_plan(len(names))
            send, recv, bufs, tok = _push_start(f"gather_start_{gi}", [fulls[k] for k in names], 3 * len(names), plan, dep)
            self.tokens.append(tok)
            dep = tok
            for k in names:
                self.waiting[k] = (gi, names, send, recv, bufs, plan)

    def order(self, a):
        for tok in self.tokens:
            a = a + tok[0, 0]
        self.tokens = []
        return a

    def dep(self):
        return self.tokens[-1] if self.tokens else None

    def weight(self, name, after):
        if name in self.waiting:
            gi, names, send, recv, bufs, plan = self.waiting[name]
            for k, buf in zip(names, _push_wait(f"gather_wait_{gi}", send, recv, bufs, plan, after)):
                self.full[k] = buf
                del self.waiting[k]
        full = self.full[name]
        return full.reshape(N_CHIPS * full.shape[1], full.shape[2])

    def ready(self, grads, after):
        names = list(grads)
        f32s, b16s = [], []
        for k in names:
            gf, gb = grads[k]
            f32s.append(gf.reshape((N_CHIPS,) + self.shards[k].shape))
            b16s.append(gb.reshape((N_CHIPS,) + self.shards[k].shape))
        n = len(names)
        lands = [lax.empty((N_CHIPS - 1,) + self.shards[k].shape, BF16) for k in names]
        plan = _grad_push_plan(n)
        gi = len(self.groups)
        send, recv, bufs, tok = _push_start(f"grad_start_{gi}", b16s + lands, 3 * n, plan, after)
        self.tokens.append(tok)
        self.groups.append((names, f32s, send, recv, bufs, plan))

    def finish(self):
        out, swaps = {}, []
        after = self.tokens[-1]
        for gi, (names, f32s, send, recv, bufs, plan) in enumerate(self.groups):
            n = len(names)
            bufs = _push_wait(f"grad_wait_{gi}", send, recv, bufs, plan, after)
            mine = [_sum_own(self.me, f32s[i], bufs[n + i], f"sum_{k}") for i, k in enumerate(names)]
            lands = [lax.empty(p.shape, F32) for p in mine]
            send2, recv2, both, after = _push_start(f"swap_start_{gi}", mine + lands, n, _swap_plan(n), mine[-1])
            swaps.append((names, send2, recv2, both))
        for gi, (names, send2, recv2, both) in enumerate(swaps):
            n = len(names)
            both = _push_wait(f"swap_wait_{gi}", send2, recv2, both, _swap_plan(n), after)
            for i, k in enumerate(names):
                out[k] = (both[i], both[n + i])
        return out


def kernel(x, positions, norm_ffn1, w_gate1, w_up1, w_down1, norm_mix, w_in, a_sink, w_out, norm_ffn2, w_gate2, w_up2, w_down2, norm_final, loss_target, m_norm_ffn1, m_w_gate1, m_w_up1, m_w_down1, m_norm_mix, m_w_in, m_a_sink, m_w_out, m_norm_ffn2, m_w_gate2, m_w_up2, m_w_down2, m_norm_final, v_norm_ffn1, v_w_gate1, v_w_up1, v_w_down1, v_norm_mix, v_w_in, v_a_sink, v_w_out, v_norm_ffn2, v_w_gate2, v_w_up2, v_w_down2, v_norm_final):
    T, D = x.shape[1], x.shape[2]
    flip = ("wg1", "wu1", "w_in", "wg2", "wu2")

    def rows(k, a):
        return a[0].T if k in flip else a[0]

    given = dict(wg1=(w_gate1, m_w_gate1, v_w_gate1), wu1=(w_up1, m_w_up1, v_w_up1), wd1=(w_down1, m_w_down1, v_w_down1),
                 w_in=(w_in, m_w_in, v_w_in), w_out=(w_out, m_w_out, v_w_out), wg2=(w_gate2, m_w_gate2, v_w_gate2),
                 wu2=(w_up2, m_w_up2, v_w_up2), wd2=(w_down2, m_w_down2, v_w_down2))
    shards = {k: rows(k, given[k][0]) for k in BIG}

    comm = _Comm(shards, lambda tok: _rope(positions[0], tok))

    norms = (norm_ffn1, norm_mix, norm_ffn2, norm_final.reshape(1, D))
    grad_x, small = _local_step(x[0], comm.side, loss_target[0], norms, a_sink[0], comm)

    partial = comm.finish()

    def pad_row(a):
        a = a.reshape(-1)
        return jnp.pad(a, (0, D - a.shape[0]))

    row4 = pad_row(jnp.concatenate([small["sink"], small["loss"].reshape(1)]))
    vec = jnp.stack([small["g1"], small["gm"], small["g2"], small["gf"], row4] + [jnp.zeros((D,), F32)] * 3, axis=0)
    red = _allreduce_small(vec)
    loss = red[4, 8]
    g_small = jnp.stack([red[0], red[1], red[2], red[3], pad_row(red[4, 0:8])] + [jnp.zeros((D,), F32)] * 3, axis=0)

    def small_stack(a1, am, a2, af, ask):
        return jnp.stack([pad_row(a1), pad_row(am), pad_row(a2), pad_row(af), pad_row(ask)] + [jnp.zeros((D,), F32)] * 3, axis=0)

    w_small = small_stack(norm_ffn1, norm_mix, norm_ffn2, norm_final, a_sink)
    m_small = small_stack(m_norm_ffn1, m_norm_mix, m_norm_ffn2, m_norm_final, m_a_sink)
    v_small = small_stack(v_norm_ffn1, v_norm_mix, v_norm_ffn2, v_norm_final, v_a_sink)
    live = small_stack(jnp.ones_like(norm_ffn1), jnp.ones_like(norm_mix), jnp.ones_like(norm_ffn2), jnp.ones_like(norm_final), jnp.ones_like(a_sink))
    v_small = jnp.where(live > 0, v_small, 1.0)

    upd = {}
    for k in BIG:
        outs = _adamw(shards[k], partial[k][0], partial[k][1], rows(k, given[k][1]), rows(k, given[k][2]), f"adamw_{k}")
        upd[k] = tuple((a.T if k in flip else a)[None] for a in outs)
    _, ds_, nms_, nvs_ = _adamw(w_small, g_small, jnp.zeros_like(g_small), m_small, v_small, "adamw_small")

    def small_out(arr):
        return [arr[0].reshape(1, D), arr[1].reshape(1, D), arr[2].reshape(1, D), arr[3], arr[4, 0:8].reshape(1, 8)]

    gs_, dss, nmss, nvss = small_out(g_small), small_out(ds_), small_out(nms_), small_out(nvs_)

    def ordered(i):
        sm = (gs_, dss, nmss, nvss)[i]
        return [sm[0], upd["wg1"][i], upd["wu1"][i], upd["wd1"][i], sm[1], upd["w_in"][i], sm[4], upd["w_out"][i], sm[2],
                upd["wg2"][i], upd["wu2"][i], upd["wd2"][i], sm[3]]

    return (loss, grad_x[None], *ordered(0), *ordered(1), *ordered(2), *ordered(3))
```

```python
import jax
import jax.numpy as jnp
from jax import lax
from jax.experimental import pallas as pl
from jax.experimental.pallas import tpu as pltpu

F32 = jnp.float32
BF16 = jnp.bfloat16

HEAD_DIM = 64
LANES = 128
SUBLANES = 8
A_Q_W, A_KV_W, B_W = 512, 128, 512
A_HALF_WINDOW = 128
B_PATTERNS = ((128, 1), (512, 4), (2048, 16))
ROPE_THETA = 10000.0
NORM_EPS = 1e-6
FFN_RES_WEIGHT = 0.5
ADAM_LR, ADAM_B1, ADAM_B2, ADAM_EPS, ADAM_WD, ADAM_STEP = 0.001, 0.9, 0.999, 1e-08, 0.01, 10
N_CHIPS = 4
N_DEV = 8
QB = 128
SHORT_SEQ = 512
NEG = -1e30
VMEM_LIMIT = 56 * 1024 * 1024
MESH = pl.DeviceIdType.MESH
ANY = pl.BlockSpec(memory_space=pl.ANY)


def _params(sem=None):
    return pltpu.CompilerParams(dimension_semantics=sem, vmem_limit_bytes=VMEM_LIMIT)


def _sds(shape, dtype):
    return jax.ShapeDtypeStruct(tuple(shape), dtype)


def _dot(a, b):
    return jnp.dot(a, b, preferred_element_type=F32)


def _dot_nt(a, b):
    return lax.dot_general(a, b, (((1,), (1,)), ((), ())), preferred_element_type=F32)


def _dot_tn(a, b):
    return lax.dot_general(a, b, (((0,), (0,)), ((), ())), preferred_element_type=F32)


def _rms_stats(x):
    r = lax.rsqrt(jnp.mean(x * x, axis=-1, keepdims=True) + NORM_EPS)
    return x * r, r


def _rms_bwd(dh, x, g):
    xhat, r = _rms_stats(x)
    dxn = dh * g
    dx = r * (dxn - xhat * jnp.mean(dxn * xhat, axis=-1, keepdims=True))
    tm, d = x.shape
    dg = (dh * xhat).reshape(tm // SUBLANES, SUBLANES, d).sum(axis=0)
    return dx, dg


def _sigmoid(x):
    return 1.0 / (1.0 + jnp.exp(-x))


def _swap32(t):
    n = t.shape[-1]
    lane = lax.broadcasted_iota(jnp.int32, t.shape, t.ndim - 1)
    return jnp.where((lane % HEAD_DIM) < HEAD_DIM // 2, pltpu.roll(t, n - HEAD_DIM // 2, axis=t.ndim - 1),
                     pltpu.roll(t, HEAD_DIM // 2, axis=t.ndim - 1))


def _cast_place(me_arr, w, name):
    R, C = w.shape
    tr = R // 2 if (R // 2) % 16 == 0 else R

    def body(me_ref, w_ref, o_ref):
        o_ref[...] = w_ref[...].astype(BF16)

    grid_spec = pltpu.PrefetchScalarGridSpec(
        num_scalar_prefetch=1, grid=(R // tr,), in_specs=[pl.BlockSpec((tr, C), lambda t, me: (t, 0))],
        out_specs=pl.BlockSpec((None, tr, C), lambda t, me: (me[0], t, 0)))
    return pl.pallas_call(body, name=name, grid_spec=grid_spec, out_shape=_sds((N_CHIPS, R, C), BF16),
                          compiler_params=_params(("parallel",)))(me_arr, w)


HBM = pl.BlockSpec(memory_space=pltpu.HBM)
SEM = pl.BlockSpec(memory_space=pltpu.SEMAPHORE)


def _push_start(name, bufs, ncopies, plan, after):
    nb = len(bufs)

    def body(*refs):
        send, recv, token = refs[nb + 1], refs[nb + 2], refs[-1]
        for i, (src, dst, dev) in enumerate(plan(refs[:nb])):
            pltpu.make_async_remote_copy(src_ref=src, dst_ref=dst, send_sem=send.at[i], recv_sem=recv.at[i],
                                         device_id=dev, device_id_type=MESH).start()
        token[...] = jnp.zeros_like(token)

    outs = pl.pallas_call(
        body, name=name,
        out_shape=(pltpu.SemaphoreType.DMA((ncopies,)), pltpu.SemaphoreType.DMA((ncopies,)), *[pltpu.HBM(b.shape, b.dtype) for b in bufs],
                   _sds((SUBLANES, LANES), F32)),
        in_specs=[HBM] * nb + [ANY], out_specs=(SEM, SEM, *([HBM] * nb), pl.BlockSpec(memory_space=pltpu.VMEM)),
        input_output_aliases={i: 2 + i for i in range(nb)},
        compiler_params=pltpu.CompilerParams(has_side_effects=pltpu.SideEffectType.DATAFLOW_SIDE_EFFECTING),
    )(*[pltpu.with_memory_space_constraint(b, pltpu.HBM) for b in bufs], after)
    return outs[0], outs[1], list(outs[2:2 + nb]), outs[-1]


def _push_wait(name, send, recv, bufs, plan, after):
    nb = len(bufs)

    def body(*refs):
        send_ref, recv_ref = refs[nb], refs[nb + 1]
        for i, (src, dst, dev) in enumerate(plan(refs[:nb])):
            cp = pltpu.make_async_remote_copy(src_ref=src, dst_ref=dst, send_sem=send_ref.at[i], recv_sem=recv_ref.at[i],
                                              device_id=dev, device_id_type=MESH)
            cp.wait_send()
            cp.wait_recv()

    afters = list(after) if isinstance(after, (list, tuple)) else [after]
    outs = pl.pallas_call(
        body, name=name, out_shape=tuple(pltpu.HBM(b.shape, b.dtype) for b in bufs),
        in_specs=[HBM] * nb + [SEM, SEM] + [ANY] * len(afters), out_specs=tuple([HBM] * nb),
        input_output_aliases={i: i for i in range(nb)},
        compiler_params=pltpu.CompilerParams(has_side_effects=pltpu.SideEffectType.DATAFLOW_SIDE_EFFECTING),
    )(*bufs, send, recv, *afters)
    return list(outs)


def _mesh_pos():
    return lax.axis_index("x"), lax.axis_index("y"), lax.axis_index("c")


def _chip_peers(x, y, c):
    return [((1 - x, y, c), 2 * (1 - x) + y), ((x, 1 - y, c), 2 * x + (1 - y)), ((1 - x, 1 - y, c), 2 * (1 - x) + (1 - y))]


def _gather_plan(n):
    def plan(refs):
        x, y, c = _mesh_pos()
        me = 2 * x + y
        return [(refs[k].at[me], refs[k].at[me], dev) for k in range(n) for dev, _ in _chip_peers(x, y, c)]
    return plan


def _rows_of(shape, who, quarter=None):
    r2 = shape[1] // 2
    if quarter is None:
        return pl.ds(pl.multiple_of(who * r2, 16), r2)
    return pl.ds(pl.multiple_of(who * r2 + quarter * (r2 // 2), 16), r2 // 2)


def _neighbour_plan(shapes):
    def plan(refs):
        x, y, c = _mesh_pos()
        me = 2 * x + y
        return [(refs[k].at[me, _rows_of(shp, c), :], refs[k].at[me, _rows_of(shp, c), :], dev)
                for k, shp in enumerate(shapes) for dev in ((1 - x, y, c), (x, 1 - y, c))]
    return plan


def _gather_forward(fulls):
    n = len(fulls)

    def body(*refs):
        ins, outs = refs[:n], refs[n:2 * n]
        ici_send, ici_recv, d2d_send, d2d_recv = refs[2 * n:]
        x, y, c = _mesh_pos()
        cx, cy, cd = 2 * (1 - x) + y, 2 * x + (1 - y), 2 * (1 - x) + (1 - y)
        sibling, x_nbr, y_nbr = (x, y, 1 - c), (1 - x, y, c), (x, 1 - y, c)
        started = []

        def push(src, dst, send, recv, dev):
            cp = pltpu.make_async_remote_copy(src_ref=src, dst_ref=dst, send_sem=send, recv_sem=recv, device_id=dev, device_id_type=MESH)
            cp.start()
            started.append(cp)

        def arrived(blk, send, recv):
            pltpu.make_async_remote_copy(src_ref=blk, dst_ref=blk, send_sem=send, recv_sem=recv, device_id=sibling,
                                         device_id_type=MESH).wait_recv()

        for k in range(n):
            shp = fulls[k].shape
            for j, chip in enumerate((cx, cy)):
                push(ins[k].at[chip, _rows_of(shp, c), :], outs[k].at[chip, _rows_of(shp, c), :],
                     d2d_send.at[3 * k + j], d2d_recv.at[3 * k + j], sibling)
            push(ins[k].at[cx, _rows_of(shp, c, 0), :], outs[k].at[cx, _rows_of(shp, c, 0), :], ici_send.at[2 * k], ici_recv.at[2 * k], y_nbr)
            push(ins[k].at[cy, _rows_of(shp, c, 1), :], outs[k].at[cy, _rows_of(shp, c, 1), :], ici_send.at[2 * k + 1], ici_recv.at[2 * k + 1],
                 x_nbr)
        for k in range(n):
            shp = fulls[k].shape
            for q in (0, 1):
                arrived(outs[k].at[cd, _rows_of(shp, c, q), :], ici_send.at[2 * k + q], ici_recv.at[2 * k + q])
            blk = outs[k].at[cd, _rows_of(shp, c), :]
            push(blk, blk, d2d_send.at[3 * k + 2], d2d_recv.at[3 * k + 2], sibling)
        for k in range(n):
            for j, chip in enumerate((cx, cy, cd)):
                arrived(outs[k].at[chip, _rows_of(fulls[k].shape, 1 - c), :], d2d_send.at[3 * k + j], d2d_recv.at[3 * k + j])
        for cp in started:
            cp.wait_send()

    return pl.pallas_call(
        body, name="gather_forward", out_shape=[_sds(f.shape, BF16) for f in fulls],
        in_specs=[ANY] * n, out_specs=[ANY] * n, input_output_aliases={k: k for k in range(n)},
        scratch_shapes=[pltpu.SemaphoreType.DMA((n * 2,))] * 2 + [pltpu.SemaphoreType.DMA((n * 3,))] * 2,
        compiler_params=_params())(*fulls)


def _resident(shape):
    return pl.BlockSpec(shape, lambda i: (0,) * len(shape), pipeline_mode=pl.Buffered(1))


FFN_FWD_CHUNK = 256
FFN_DX_CHUNK = 512


def _chunks(n, step):
    return [(c0, min(step, n - c0)) for c0 in range(0, n, step)]


def _two_phase(chunks, first, second):
    held = {}
    for ci, ch in enumerate(chunks):
        held[ci] = first(*ch)
        if ci >= 1:
            second(*chunks[ci - 1], held.pop(ci - 1))
    last = len(chunks) - 1
    second(*chunks[last], held.pop(last))


def _ffn_fwd(x, g, wgt, wut, wd, name, tm=512):
    T, D = x.shape
    F = wd.shape[0]

    def body(x_ref, g_ref, wg_ref, wu_ref, wd_ref, xo_ref, h_ref, gate_ref, up_ref, act_ref):
        xv = x_ref[...]
        xhat, _ = _rms_stats(xv)
        h = (xhat * g_ref[...]).astype(BF16)
        h_ref[...] = h
        acc = []

        def first(c0, cw):
            return _dot_nt(h, wg_ref[c0:c0 + cw, :]), _dot_nt(h, wu_ref[c0:c0 + cw, :])

        def second(c0, cw, gate_up):
            gate, up = gate_up
            act = ((gate * _sigmoid(gate)) * up).astype(BF16)
            gate_ref[:, c0:c0 + cw] = gate.astype(BF16)
            up_ref[:, c0:c0 + cw] = up.astype(BF16)
            act_ref[:, c0:c0 + cw] = act
            d = _dot(act, wd_ref[c0:c0 + cw, :])
            acc[:] = [d if not acc else acc[0] + d]

        _two_phase(_chunks(F, FFN_FWD_CHUNK), first, second)
        xo_ref[...] = xv + FFN_RES_WEIGHT * acc[0]

    row = pl.BlockSpec((tm, D), lambda i: (i, 0))
    saved = pl.BlockSpec((tm, F), lambda i: (i, 0))
    return pl.pallas_call(
        body, name=name, grid=(T // tm,),
        in_specs=[row, pl.BlockSpec((1, D), lambda i: (0, 0)), _resident(wgt.shape), _resident(wut.shape), _resident(wd.shape)],
        out_specs=[row, row, saved, saved, saved],
        out_shape=[_sds((T, D), F32), _sds((T, D), BF16), _sds((T, F), BF16), _sds((T, F), BF16), _sds((T, F), BF16)],
        compiler_params=_params(("parallel",)))(x, g, wgt, wut, wd)


def _ffn_dx(dxo, x, g, gate_s, up_s, wgt, wut, wd, name, tm=256):
    T, D = x.shape
    F = wd.shape[0]

    def body(dxo_ref, x_ref, g_ref, gate_ref, up_ref, wg_ref, wu_ref, wd_ref, dx_ref, dff_ref, dgate_ref, dup_ref, dg_ref):
        @pl.when(pl.program_id(0) == 0)
        def _():
            dg_ref[...] = jnp.zeros_like(dg_ref)

        d = (FFN_RES_WEIGHT * dxo_ref[...]).astype(BF16)
        dff_ref[...] = d
        dh = []

        def first(c0, cw):
            return _dot_nt(d, wd_ref[c0:c0 + cw, :])

        def second(c0, cw, da):
            gate = gate_ref[:, c0:c0 + cw].astype(F32)
            up = up_ref[:, c0:c0 + cw].astype(F32)
            s = _sigmoid(gate)
            silu = gate * s
            dup = (da * silu).astype(BF16)
            dgate = (da * up * (s * (1.0 + gate * (1.0 - s)))).astype(BF16)
            dgate_ref[:, c0:c0 + cw] = dgate
            dup_ref[:, c0:c0 + cw] = dup
            t = _dot(dgate, wg_ref[c0:c0 + cw, :]) + _dot(dup, wu_ref[c0:c0 + cw, :])
            dh[:] = [t if not dh else dh[0] + t]

        _two_phase(_chunks(F, FFN_DX_CHUNK), first, second)
        dxn, dg = _rms_bwd(dh[0], x_ref[...], g_ref[...])
        dg_ref[...] += dg
        dx_ref[...] = dxo_ref[...] + dxn

    row = pl.BlockSpec((tm, D), lambda i: (i, 0))
    saved = pl.BlockSpec((tm, F), lambda i: (i, 0))
    return pl.pallas_call(
        body, name=name, grid=(T // tm,),
        in_specs=[row, row, pl.BlockSpec((1, D), lambda i: (0, 0)), saved, saved, _resident(wgt.shape), _resident(wut.shape),
                  _resident(wd.shape)],
        out_specs=[row, row, saved, saved, pl.BlockSpec((SUBLANES, D), lambda i: (0, 0))],
        out_shape=[_sds((T, D), F32), _sds((T, D), BF16), _sds((T, F), BF16), _sds((T, F), BF16), _sds((SUBLANES, D), F32)],
        compiler_params=_params(("arbitrary",)))(dxo, x, g, gate_s, up_s, wgt, wut, wd)


def _tn(a, b, mb, name, tk=2048, dep=None):
    T, M = a.shape
    N = b.shape[1]
    nt = T // tk

    def body(a_ref, b_ref, *refs):
        o_ref, ob_ref = refs[-2:]

        @pl.when(pl.program_id(1) == 0)
        def _():
            o_ref[...] = jnp.zeros_like(o_ref)

        o_ref[...] += _dot_tn(a_ref[...].astype(BF16), b_ref[...].astype(BF16))

        @pl.when(pl.program_id(1) == nt - 1)
        def _():
            ob_ref[...] = o_ref[...].astype(BF16)

    o_spec = pl.BlockSpec((mb, N), lambda g, t: (g, 0))
    return pl.pallas_call(
        body, name=name, grid=(M // mb, nt),
        in_specs=[pl.BlockSpec((tk, mb), lambda g, t: (t, g)), pl.BlockSpec((tk, N), lambda g, t: (t, 0))] + ([ANY] if dep is not None else []),
        out_specs=[o_spec, o_spec], out_shape=[_sds((M, N), F32), _sds((M, N), BF16)],
        compiler_params=_params(("parallel", "arbitrary")))(a, b, *([dep] if dep is not None else []))


def _rope_tables(pos_col, inv_freq):
    T = pos_col.shape[0]

    def body(p_ref, f_ref, c_ref, s_ref):
        ang = p_ref[...].astype(F32) * f_ref[...]
        lane = lax.broadcasted_iota(jnp.int32, ang.shape, 1)
        c_ref[...] = jnp.cos(ang)
        sn = jnp.sin(ang)
        s_ref[...] = jnp.where((lane % HEAD_DIM) < HEAD_DIM // 2, -sn, sn)

    tm = 1024
    return pl.pallas_call(
        body, name="rope_tables", grid=(T // tm,),
        in_specs=[pl.BlockSpec((tm, 1), lambda i: (i, 0)), pl.BlockSpec((1, LANES), lambda i: (0, 0))],
        out_specs=[pl.BlockSpec((tm, LANES), lambda i: (i, 0))] * 2,
        out_shape=[_sds((T, LANES), F32)] * 2, compiler_params=_params(("parallel",)))(pos_col, inv_freq)


def _deinterleave(scr, out_ref, d, tm, nblk):
    for r in range(d):
        for cb in range(nblk):
            out_ref[r, :, cb * LANES:(cb + 1) * LANES] = scr[cb, pl.ds(r, tm // d, stride=d), :].astype(out_ref.dtype)


def _interleave(in_ref, scr, d, tm, nblk):
    for r in range(d):
        for cb in range(nblk):
            scr[cb, pl.ds(r, tm // d, stride=d), :] = in_ref[r, :, cb * LANES:(cb + 1) * LANES].astype(F32)


def _proj_rope(x, g, w_in, cos, sin, tm=512):
    T, D = x.shape
    dils = [d for _, d in B_PATTERNS if d > 1]
    nbb = B_W // LANES
    scale = HEAD_DIM ** -0.5
    cuts = [0, A_Q_W, A_Q_W + A_KV_W, A_Q_W + 2 * A_KV_W, A_Q_W + 2 * A_KV_W + B_W, A_Q_W + 2 * A_KV_W + 2 * B_W,
            A_Q_W + 2 * A_KV_W + 3 * B_W]

    def body(x_ref, g_ref, w_ref, c_ref, s_ref, h_ref, aq_ref, ak_ref, av_ref, *rest):
        b_refs, scr = rest[:-1], rest[-1]
        xhat, _ = _rms_stats(x_ref[...])
        h = (xhat * g_ref[...]).astype(BF16)
        h_ref[...] = h
        cs, sn = c_ref[...], s_ref[...]

        def project(idx, ref, rope, mult, which):
            return _dot_nt(h, w_ref[cuts[idx]:cuts[idx + 1], :])

        def finish(idx, ref, rope, mult, which, whole):
            for cb in range((cuts[idx + 1] - cuts[idx]) // LANES):
                p = whole[:, cb * LANES:(cb + 1) * LANES]
                if rope:
                    p = p * cs + _swap32(p) * sn
                if mult != 1.0:
                    p = p * mult
                ref[:, cb * LANES:(cb + 1) * LANES] = p.astype(BF16)
                if which is not None:
                    scr[which, cb] = p
            if which is not None:
                for di, d in enumerate(dils):
                    _deinterleave(scr.at[which], b_refs[3 * (di + 1) + which], d, tm, nbb)

        _two_phase([(0, aq_ref, True, scale, None), (1, ak_ref, True, 1.0, None), (2, av_ref, False, 1.0, None),
                    (3, b_refs[0], True, scale, 0), (4, b_refs[1], True, 1.0, 1), (5, b_refs[2], False, 1.0, 2)], project, finish)

    row = lambda w: pl.BlockSpec((tm, w), lambda i: (i, 0))
    out_specs = [row(D), row(A_Q_W), row(A_KV_W), row(A_KV_W)] + [row(B_W)] * 3
    out_shape = [_sds((T, D), BF16), _sds((T, A_Q_W), BF16), _sds((T, A_KV_W), BF16), _sds((T, A_KV_W), BF16)] + [_sds((T, B_W), BF16)] * 3
    for d in dils:
        out_specs += [pl.BlockSpec((d, tm // d, B_W), lambda i: (0, i, 0))] * 3
        out_shape += [_sds((d, T // d, B_W), BF16)] * 3
    return pl.pallas_call(
        body, name="proj_rope", grid=(T // tm,),
        in_specs=[row(D), pl.BlockSpec((1, D), lambda i: (0, 0)), pl.BlockSpec(w_in.shape, lambda i: (0, 0)), row(LANES), row(LANES)],
        out_specs=out_specs, out_shape=out_shape, scratch_shapes=[pltpu.VMEM((3, nbb, tm, LANES), F32)],
        compiler_params=_params(("parallel",)))(x, g, w_in, cos, sin)


def _band_bias(rel, qb, kw, hw):
    ri = lax.broadcasted_iota(jnp.int32, (2 * qb, kw), 0) & (qb - 1)
    ci = lax.broadcasted_iota(jnp.int32, (2 * qb, kw), 1)
    return jnp.where(jnp.abs(ri + rel - ci) <= hw, 0.0, NEG).astype(F32)


def _stack_heads(x, lo):
    z = jnp.zeros_like(x)
    return jnp.concatenate([jnp.where(lo, x, z), jnp.where(lo, z, x)], axis=0)


def _unstack_heads(y, lo):
    qb = y.shape[0] // 2
    return jnp.where(lo, y[:qb], y[qb:])


def _band_setup(bias_scr, qb, kw, hw):
    if bias_scr is not None:
        for i in range(3):
            bias_scr[i] = _band_bias(i * hw, qb, kw, hw)


def _band_window(bias_scr, qs, L, qb, kw, hw):
    ws = pl.multiple_of(jnp.clip(qs - hw, 0, L - kw), 64)
    if bias_scr is None:
        return ws, _band_bias(qs - ws, qb, kw, hw)
    return ws, bias_scr[lax.shift_right_logical(qs - ws, hw.bit_length() - 1)]


def _dup_kv_head(src_ref, dst_ref, head, L):
    step = min(L, 1024)
    for r0 in range(0, L, step):
        xf = src_ref[r0:r0 + step, :].astype(F32)
        lane = lax.broadcasted_iota(jnp.int32, xf.shape, 1)
        keep = jnp.logical_xor(lane < HEAD_DIM, head == 1)
        dst_ref[r0:r0 + step, :] = jnp.where(keep, xf, pltpu.roll(xf, HEAD_DIM, axis=1)).astype(dst_ref.dtype)


def _attn_fwd(q, k, v, sink, hw, gqa, out_dtype, name, qb=QB, blocks_per_step=8):
    NB, L, Cq = q.shape
    Ls = min(L, 2048)
    kw = min(qb + 2 * hw, L)
    tables = L >= qb + 2 * hw
    unroll = min(blocks_per_step, Ls // qb)
    nlb = 1 if (gqa or L > SHORT_SEQ) else Cq // LANES

    def body(sink_ref, q_ref, k_ref, v_ref, o_ref, lse_ref, *scr):
        b, s_idx = pl.program_id(1), pl.program_id(2)
        bias_scr = scr[0] if tables else None
        _band_setup(bias_scr, qb, kw, hw)
        if gqa:
            kd, vd = scr[-2:]

            @pl.when(s_idx == 0)
            def _():
                _dup_kv_head(k_ref, kd, b // 2, L)
                _dup_kv_head(v_ref, vd, b // 2, L)
        else:
            kd, vd = k_ref, v_ref
        lane = lax.broadcasted_iota(jnp.int32, (qb, LANES), 1)
        lo = lane < HEAD_DIM
        if gqa:
            row = lax.broadcasted_iota(jnp.int32, (2 * qb, 1), 0)
            sk = jnp.where(row < qb, sink_ref[2 * b], sink_ref[2 * b + 1])

        def block(ql, col):
            qs = s_idx * Ls + ql
            ws, bias = _band_window(bias_scr, qs, L, qb, kw, hw)
            return ws, _dot_nt(_stack_heads(q_ref[pl.ds(ql, qb), col], lo), kd[pl.ds(ws, kw), col]) + bias

        def finish(ql, col, scores):
            ws, s = scores
            m = jnp.max(s, axis=-1, keepdims=True)
            if gqa:
                m = jnp.maximum(m, sk)
            p = jnp.exp(s - m)
            den = jnp.sum(p, axis=-1, keepdims=True)
            if gqa:
                den = den + jnp.exp(sk - m)
            o = _dot(p.astype(BF16), vd[pl.ds(ws, kw), col]) * (1.0 / den)
            o_ref[pl.ds(ql, qb), col] = _unstack_heads(o, lo).astype(o_ref.dtype)
            lse_ref[pl.ds(ql, qb), col] = _unstack_heads(m + jnp.log(den), lo)

        for lb in range(nlb):
            def step(n, carry, col=slice(lb * LANES, (lb + 1) * LANES)):
                _two_phase([(pl.multiple_of((n * unroll + u) * qb, qb), col) for u in range(unroll)], block, finish)
                return carry

            lax.fori_loop(0, Ls // (qb * unroll), step, 0)

    kv_map = (lambda r, b, s: (r, 0, 0)) if gqa else (lambda r, b, s: (r, 0, b))
    seg = pl.BlockSpec((None, Ls, nlb * LANES), lambda r, b, s: (r, s, b))
    return pl.pallas_call(
        body, name=name, grid=(NB, Cq // (nlb * LANES), L // Ls),
        in_specs=[pl.BlockSpec(memory_space=pltpu.SMEM), seg, pl.BlockSpec((None, L, nlb * LANES), kv_map),
                  pl.BlockSpec((None, L, nlb * LANES), kv_map)],
        out_specs=[seg, seg], out_shape=[_sds((NB, L, Cq), out_dtype), _sds((NB, L, Cq), F32)],
        scratch_shapes=([pltpu.VMEM((3, 2 * qb, kw), F32)] if tables else []) + ([pltpu.VMEM((L, LANES), BF16)] * 2 if gqa else []),
        compiler_params=_params(("parallel", "parallel", "arbitrary")))(sink, q, k, v)


def _attn_bwd(q, k, v, do, lse, delta, sink, hw, gqa, name, qb=QB, blocks_per_step=8):
    NB, L, Cq = q.shape
    Ck = k.shape[2]
    Ls = min(L, 2048)
    kw = min(qb + 2 * hw, L)
    reps = kw // LANES
    nseg = L // Ls
    scale = HEAD_DIM ** -0.5
    tables = L >= qb + 2 * hw
    unroll = min(blocks_per_step, Ls // qb)
    nlb = 1 if (gqa or L > SHORT_SEQ) else Cq // LANES

    def body(sink_ref, q_ref, do_ref, lse_ref, dl_ref, k_ref, v_ref, dq_ref, dk_ref, dv_ref, dsk_ref, *scr):
        b, s_idx = pl.program_id(1), pl.program_id(2)
        lane = lax.broadcasted_iota(jnp.int32, (qb, LANES), 1)
        lo = lane < HEAD_DIM
        bias_scr = scr[0] if tables else None
        _band_setup(bias_scr, qb, kw, hw)
        if gqa:
            kd, vd, dk_acc, dv_acc, dsk_acc = scr[-5:]

            @pl.when(s_idx == 0)
            def _():
                _dup_kv_head(k_ref, kd, b // 2, L)
                _dup_kv_head(v_ref, vd, b // 2, L)
                dk_acc[...] = jnp.zeros_like(dk_acc)
                dv_acc[...] = jnp.zeros_like(dv_acc)
                dsk_acc[...] = jnp.zeros_like(dsk_acc)

            @pl.when((s_idx == 0) & (b == 0))
            def _():
                dk_ref[...] = jnp.zeros_like(dk_ref)
                dv_ref[...] = jnp.zeros_like(dv_ref)
        else:
            kd, vd = k_ref, v_ref
            dk_acc, dv_acc = scr[-2:]

            @pl.when(s_idx == 0)
            def _():
                dk_acc[...] = jnp.zeros_like(dk_acc)
                dv_acc[...] = jnp.zeros_like(dv_acc)

        def block(ql, col):
            qs = s_idx * Ls + ql
            ws, bias = _band_window(bias_scr, qs, L, qb, kw, hw)
            qv, dov = q_ref[pl.ds(ql, qb), col], do_ref[pl.ds(ql, qb), col]
            lse, dl = lse_ref[pl.ds(ql, qb), col], dl_ref[pl.ds(ql, qb), col]
            kv_, vv = kd[pl.ds(ws, kw), col], vd[pl.ds(ws, kw), col]
            q2, do2 = _stack_heads(qv, lo), _stack_heads(dov, lo)
            return ws, q2, do2, lse, dl, _dot_nt(q2, kv_) + bias, _dot_nt(do2, vv)

        def finish(ql, col, held):
            ws, q2, do2, lse, dl, s, dp = held
            lse_sw, dl_sw = pltpu.roll(lse, HEAD_DIM, axis=1), pltpu.roll(dl, HEAD_DIM, axis=1)
            lse2 = jnp.concatenate([jnp.where(lo, lse, lse_sw), jnp.where(lo, lse_sw, lse)], axis=0)
            dl2 = jnp.concatenate([jnp.where(lo, dl, dl_sw), jnp.where(lo, dl_sw, dl)], axis=0)
            p = jnp.exp(s - jnp.tile(lse2, (1, reps)))
            ds = (p * (dp - jnp.tile(dl2, (1, reps)))).astype(BF16)
            dq_ref[pl.ds(ql, qb), col] = (_unstack_heads(_dot(ds, kd[pl.ds(ws, kw), col]), lo) * scale).astype(dq_ref.dtype)
            both = _dot_tn(jnp.concatenate([ds, p.astype(BF16)], axis=1), jnp.concatenate([q2, do2], axis=1))
            dk_acc[pl.ds(ws, kw), col] += both[:kw, :LANES]
            dv_acc[pl.ds(ws, kw), col] += both[kw:, LANES:]
            if gqa:
                sk = jnp.where(lo, sink_ref[2 * b], sink_ref[2 * b + 1])
                dsk_acc[...] += -jnp.exp(sk - lse) * dl

        for lb in range(nlb):
            def step(n, carry, col=slice(lb * LANES, (lb + 1) * LANES)):
                _two_phase([(pl.multiple_of((n * unroll + u) * qb, qb), col) for u in range(unroll)], block, finish)
                return carry

            lax.fori_loop(0, Ls // (qb * unroll), step, 0)

        if gqa:
            @pl.when(s_idx == nseg - 1)
            def _():
                step_rows = min(L, 1024)
                for r0 in range(0, L, step_rows):
                    lanek = lax.broadcasted_iota(jnp.int32, (step_rows, LANES), 1)
                    mine = jnp.logical_xor(lanek < HEAD_DIM, (b // 2) == 1)
                    for acc, ref in ((dk_acc, dk_ref), (dv_acc, dv_ref)):
                        a = acc[r0:r0 + step_rows, :]
                        ref[r0:r0 + step_rows, :] += jnp.where(mine, a + pltpu.roll(a, HEAD_DIM, axis=1), 0.0)
                dsk_ref[...] = dsk_acc[...].reshape(qb // SUBLANES, SUBLANES, LANES).sum(axis=0)
        else:
            dsk_ref[...] = jnp.zeros_like(dsk_ref)

            @pl.when(s_idx == nseg - 1)
            def _():
                dk_ref[...] = dk_acc[...].astype(dk_ref.dtype)
                dv_ref[...] = dv_acc[...].astype(dv_ref.dtype)

    kv_map = (lambda r, b, s: (r, 0, 0)) if gqa else (lambda r, b, s: (r, 0, b))
    seg = pl.BlockSpec((None, Ls, nlb * LANES), lambda r, b, s: (r, s, b))
    full = pl.BlockSpec((None, L, nlb * LANES), kv_map)
    scratch = [pltpu.VMEM((3, 2 * qb, kw), F32)] if tables else []
    if gqa:
        scratch += [pltpu.VMEM((L, LANES), BF16)] * 2 + [pltpu.VMEM((L, LANES), F32)] * 2 + [pltpu.VMEM((qb, LANES), F32)]
    else:
        scratch += [pltpu.VMEM((L, nlb * LANES), F32)] * 2
    kv_dtype = F32 if gqa else BF16
    return pl.pallas_call(
        body, name=name, grid=(NB, Cq // (nlb * LANES), nseg),
        in_specs=[pl.BlockSpec(memory_space=pltpu.SMEM), seg, seg, seg, seg, full, full],
        out_specs=[seg, full, full, pl.BlockSpec((None, None, SUBLANES, LANES), lambda r, b, s: (r, b, 0, 0))],
        out_shape=[_sds((NB, L, Cq), BF16), _sds((NB, L, Ck), kv_dtype), _sds((NB, L, Ck), kv_dtype),
                   _sds((NB, Cq // LANES, SUBLANES, LANES), F32)],
        scratch_shapes=scratch,
        compiler_params=_params(("arbitrary", "arbitrary", "arbitrary")))(sink, q, do, lse, delta, k, v)


def _merge_b(a_out, o1, l1, o4, l4, o16, l16, tm=512):
    T = a_out.shape[0]
    nbb = B_W // LANES

    def body(a_ref, o1_ref, l1_ref, o4_ref, l4_ref, o16_ref, l16_ref, cat_ref, lg1_ref, lg4_ref, lg16_ref, so, sl, slg):
        _interleave(o4_ref, so.at[0], 4, tm, nbb)
        _interleave(l4_ref, sl.at[0], 4, tm, nbb)
        _interleave(o16_ref, so.at[1], 16, tm, nbb)
        _interleave(l16_ref, sl.at[1], 16, tm, nbb)
        cat_ref[:, 0:A_Q_W] = a_ref[...]
        for cb in range(nbb):
            cols = slice(cb * LANES, (cb + 1) * LANES)
            os_ = (o1_ref[:, cols], so[0, cb], so[1, cb])
            ls_ = (l1_ref[:, cols], sl[0, cb], sl[1, cb])
            m = jnp.maximum(jnp.maximum(ls_[0], ls_[1]), ls_[2])
            es = [jnp.exp(l - m) for l in ls_]
            den = es[0] + es[1] + es[2]
            out = (es[0] * os_[0] + es[1] * os_[1] + es[2] * os_[2]) * (1.0 / den)
            lg = m + jnp.log(den)
            cat_ref[:, A_Q_W + cb * LANES:A_Q_W + (cb + 1) * LANES] = out.astype(BF16)
            lg1_ref[:, cols] = lg
            slg[cb] = lg
        _deinterleave(slg, lg4_ref, 4, tm, nbb)
        _deinterleave(slg, lg16_ref, 16, tm, nbb)

    row = lambda w: pl.BlockSpec((tm, w), lambda i: (i, 0))
    perm = lambda d: pl.BlockSpec((d, tm // d, B_W), lambda i: (0, i, 0))
    return pl.pallas_call(
        body, name="merge_patterns", grid=(T // tm,),
        in_specs=[row(A_Q_W), row(B_W), row(B_W), perm(4), perm(4), perm(16), perm(16)],
        out_specs=[row(A_Q_W + B_W), row(B_W), perm(4), perm(16)],
        out_shape=[_sds((T, A_Q_W + B_W), BF16), _sds((T, B_W), F32), _sds((4, T // 4, B_W), F32), _sds((16, T // 16, B_W), F32)],
        scratch_shapes=[pltpu.VMEM((2, nbb, tm, LANES), F32), pltpu.VMEM((2, nbb, tm, LANES), F32), pltpu.VMEM((nbb, tm, LANES), F32)],
        compiler_params=_params(("parallel",)))(a_out, o1, l1, o4, l4, o16, l16)


def _out_proj(x, cat, w_out, tm=512):
    T, D = x.shape

    def body(x_ref, c_ref, w_ref, o_ref):
        o_ref[...] = x_ref[...] + _dot(c_ref[...], w_ref[...])

    row = lambda w: pl.BlockSpec((tm, w), lambda i: (i, 0))
    return pl.pallas_call(
        body, name="out_proj", grid=(T // tm,), in_specs=[row(D), row(cat.shape[1]), pl.BlockSpec(w_out.shape, lambda i: (0, 0))],
        out_specs=row(D), out_shape=_sds((T, D), F32), compiler_params=_params(("parallel",)))(x, cat, w_out)


def _final_loss(x, g, target, tm=512):
    T, D = x.shape

    def body(x_ref, g_ref, t_ref, dx_ref, dg_ref, loss_ref):
        @pl.when(pl.program_id(0) == 0)
        def _():
            dg_ref[...] = jnp.zeros_like(dg_ref)
            loss_ref[...] = jnp.zeros_like(loss_ref)

        xv, gv = x_ref[...], g_ref[...]
        xhat, _ = _rms_stats(xv)
        err = xhat * gv - t_ref[...]
        loss_ref[...] += 0.5 * jnp.sum(jnp.sum(err * err, axis=-1, keepdims=True) * (1.0 / D), axis=0, keepdims=True)
        dx, dg = _rms_bwd(err * (1.0 / D), xv, gv)
        dx_ref[...] = dx
        dg_ref[...] += dg

    row = pl.BlockSpec((tm, D), lambda i: (i, 0))
    return pl.pallas_call(
        body, name="final_loss", grid=(T // tm,), in_specs=[row, pl.BlockSpec((1, D), lambda i: (0, 0)), row],
        out_specs=[row, pl.BlockSpec((SUBLANES, D), lambda i: (0, 0)), pl.BlockSpec((SUBLANES, LANES), lambda i: (0, 0))],
        out_shape=[_sds((T, D), F32), _sds((SUBLANES, D), F32), _sds((SUBLANES, LANES), F32)],
        compiler_params=_params(("arbitrary",)))(x, g, target)


def _dcat(dx, w_out, cat, tm=512):
    T, D = dx.shape
    C = cat.shape[1]
    nba, nbb = A_Q_W // LANES, B_W // LANES

    def body(dx_ref, w_ref, cat_ref, doa_ref, dla_ref, dob1_ref, dlb1_ref, dob4_ref, dlb4_ref, dob16_ref, dlb16_ref, sdo, sdl):
        dc = _dot_nt(dx_ref[...].astype(BF16), w_ref[...])
        ri = lax.broadcasted_iota(jnp.int32, (LANES, LANES), 0)
        ci = lax.broadcasted_iota(jnp.int32, (LANES, LANES), 1)
        same_head = ((ri // HEAD_DIM) == (ci // HEAD_DIM)).astype(BF16)
        for cb in range(C // LANES):
            cols = slice(cb * LANES, (cb + 1) * LANES)
            blk = dc[:, cols]
            prod = blk * cat_ref[:, cols].astype(F32)
            hi = prod.astype(BF16)
            lo_ = (prod - hi.astype(F32)).astype(BF16)
            dl = _dot(hi, same_head) + _dot(lo_, same_head)
            if cb < nba:
                doa_ref[:, cols] = blk.astype(BF16)
                dla_ref[:, cols] = dl
            else:
                bcols = slice((cb - nba) * LANES, (cb - nba + 1) * LANES)
                dob1_ref[:, bcols] = blk.astype(BF16)
                dlb1_ref[:, bcols] = dl
                sdo[cb - nba] = blk
                sdl[cb - nba] = dl
        _deinterleave(sdo, dob4_ref, 4, tm, nbb)
        _deinterleave(sdl, dlb4_ref, 4, tm, nbb)
        _deinterleave(sdo, dob16_ref, 16, tm, nbb)
        _deinterleave(sdl, dlb16_ref, 16, tm, nbb)

    row = lambda w: pl.BlockSpec((tm, w), lambda i: (i, 0))
    perm = lambda d: pl.BlockSpec((d, tm // d, B_W), lambda i: (0, i, 0))
    return pl.pallas_call(
        body, name="dcat", grid=(T // tm,), in_specs=[row(D), pl.BlockSpec(w_out.shape, lambda i: (0, 0)), row(C)],
        out_specs=[row(A_Q_W), row(A_Q_W), row(B_W), row(B_W), perm(4), perm(4), perm(16), perm(16)],
        out_shape=[_sds((T, A_Q_W), BF16), _sds((T, A_Q_W), F32), _sds((T, B_W), BF16), _sds((T, B_W), F32),
                   _sds((4, T // 4, B_W), BF16), _sds((4, T // 4, B_W), F32), _sds((16, T // 16, B_W), BF16), _sds((16, T // 16, B_W), F32)],
        scratch_shapes=[pltpu.VMEM((nbb, tm, LANES), F32)] * 2, compiler_params=_params(("parallel",)))(dx, w_out, cat)


def _rope_bwd_assemble(dqa, dka, dva, b1, b4, b16, cos, sin, tm=512):
    T = dqa.shape[0]
    nbb = B_W // LANES
    width = A_Q_W + 2 * A_KV_W + 3 * B_W

    def body(dqa_ref, dka_ref, dva_ref, q1, k1, v1, q4, k4, v4, q16, k16, v16, c_ref, s_ref, o_ref, scr):
        cs, sn = c_ref[...], s_ref[...]

        def unrope(t):
            return t * cs + _swap32(t * sn)

        col = 0
        for ref, rope in ((dqa_ref, True), (dka_ref, True), (dva_ref, False)):
            for cb in range(ref.shape[1] // LANES):
                t = ref[:, cb * LANES:(cb + 1) * LANES].astype(F32)
                o_ref[:, col:col + LANES] = (unrope(t) if rope else t).astype(BF16)
                col += LANES
        for which, (r1, r4, r16, rope) in enumerate(((q1, q4, q16, True), (k1, k4, k16, True), (v1, v4, v16, False))):
            _interleave(r4, scr.at[0], 4, tm, nbb)
            _interleave(r16, scr.at[1], 16, tm, nbb)
            for cb in range(nbb):
                t = r1[:, cb * LANES:(cb + 1) * LANES].astype(F32) + scr[0, cb] + scr[1, cb]
                o_ref[:, col:col + LANES] = (unrope(t) if rope else t).astype(BF16)
                col += LANES

    row = lambda w: pl.BlockSpec((tm, w), lambda i: (i, 0))
    perm = lambda d: pl.BlockSpec((d, tm // d, B_W), lambda i: (0, i, 0))
    return pl.pallas_call(
        body, name="rope_bwd", grid=(T // tm,),
        in_specs=[row(A_Q_W), row(A_KV_W), row(A_KV_W)] + [row(B_W)] * 3 + [perm(4)] * 3 + [perm(16)] * 3 + [row(LANES), row(LANES)],
        out_specs=row(width), out_shape=_sds((T, width), BF16), scratch_shapes=[pltpu.VMEM((2, nbb, tm, LANES), F32)],
        compiler_params=_params(("parallel",)))(dqa, dka, dva, *b1, *b4, *b16, cos, sin)


def _dh_norm(dproj, w_in, x, g, dres, tm=512):
    T, D = x.shape

    def body(dp_ref, w_ref, x_ref, g_ref, dr_ref, dx_ref, dg_ref):
        @pl.when(pl.program_id(0) == 0)
        def _():
            dg_ref[...] = jnp.zeros_like(dg_ref)

        dxn, dg = _rms_bwd(_dot(dp_ref[...], w_ref[...]), x_ref[...], g_ref[...])
        dg_ref[...] += dg
        dx_ref[...] = dr_ref[...] + dxn

    row = lambda w: pl.BlockSpec((tm, w), lambda i: (i, 0))
    return pl.pallas_call(
        body, name="dh_norm", grid=(T // tm,),
        in_specs=[row(dproj.shape[1]), pl.BlockSpec(w_in.shape, lambda i: (0, 0)), row(D), pl.BlockSpec((1, D), lambda i: (0, 0)), row(D)],
        out_specs=[row(D), pl.BlockSpec((SUBLANES, D), lambda i: (0, 0))],
        out_shape=[_sds((T, D), F32), _sds((SUBLANES, D), F32)], compiler_params=_params(("arbitrary",)))(dproj, w_in, x, g, dres)


def _grad_push_plan(n):
    def plan(refs):
        x, y, c = _mesh_pos()
        return [(refs[k].at[chip], refs[n + k].at[rel], dev) for k in range(n) for rel, (dev, chip) in enumerate(_chip_peers(x, y, c))]
    return plan


def _sum_own(me_arr, g, landed, name):
    ns, R, C = g.shape
    tr = R // 2 if (R // 2) % 16 == 0 else R

    def body(me_ref, g_ref, x_ref, o_ref):
        acc = g_ref[...]
        for rel in range(ns - 1):
            acc = acc + x_ref[rel].astype(F32)
        o_ref[...] = acc

    grid_spec = pltpu.PrefetchScalarGridSpec(
        num_scalar_prefetch=1, grid=(R // tr,),
        in_specs=[pl.BlockSpec((None, tr, C), lambda t, me: (me[0], t, 0)), pl.BlockSpec((ns - 1, tr, C), lambda t, me: (0, t, 0))],
        out_specs=pl.BlockSpec((tr, C), lambda t, me: (t, 0)))
    return pl.pallas_call(body, name=name, grid_spec=grid_spec, out_shape=_sds((R, C), F32),
                          compiler_params=_params(("parallel",)))(me_arr, g, landed)


def _swap_plan(n):
    def plan(refs):
        x, y, c = _mesh_pos()
        return [(refs[k], refs[n + k], (x, y, 1 - c)) for k in range(n)]
    return plan


def _allreduce_small(v):
    rows, W = v.shape

    def body(v_ref, o_ref, buf, send, recv):
        x, y, c = _mesh_pos()
        me = 4 * x + 2 * y + c
        cps = []
        for m in range(1, N_DEV):
            dev = (x ^ (m >> 2), y ^ ((m >> 1) & 1), c ^ (m & 1))
            cp = pltpu.make_async_remote_copy(src_ref=v_ref, dst_ref=buf.at[me], send_sem=send.at[m - 1], recv_sem=recv.at[m - 1],
                                              device_id=dev, device_id_type=MESH)
            cp.start()
            cps.append(cp)
        for m in range(1, N_DEV):
            pltpu.make_async_remote_copy(src_ref=v_ref, dst_ref=buf.at[me ^ m], send_sem=send.at[m - 1], recv_sem=recv.at[m - 1],
                                         device_id=(x, y, c), device_id_type=MESH).wait_recv()
        for cp in cps:
            cp.wait_send()
        buf[me] = v_ref[...]
        acc = buf[0]
        for i in range(1, N_DEV):
            acc = acc + buf[i]
        o_ref[...] = acc

    return pl.pallas_call(
        body, name="allreduce_small", out_shape=_sds((rows, W), F32),
        scratch_shapes=[pltpu.VMEM((N_DEV, rows, W), F32), pltpu.SemaphoreType.DMA((N_DEV - 1,)), pltpu.SemaphoreType.DMA((N_DEV - 1,))],
        compiler_params=_params())(v)


def _adamw(w, gp, gq, m, v, name):
    R, C = w.shape
    tr = R // 2 if (R // 2) % SUBLANES == 0 else R
    c1 = 1.0 / (1.0 - ADAM_B1 ** ADAM_STEP)
    c2 = 1.0 / (1.0 - ADAM_B2 ** ADAM_STEP)

    def body(w_ref, gp_ref, gq_ref, m_ref, v_ref, g_ref, d_ref, nm_ref, nv_ref):
        gv = gp_ref[...] + gq_ref[...]
        nm = ADAM_B1 * m_ref[...] + (1.0 - ADAM_B1) * gv
        nv = ADAM_B2 * v_ref[...] + (1.0 - ADAM_B2) * (gv * gv)
        g_ref[...] = gv
        d_ref[...] = -ADAM_LR * ((nm * c1) / (jnp.sqrt(nv * c2) + ADAM_EPS) + ADAM_WD * w_ref[...])
        nm_ref[...] = nm
        nv_ref[...] = nv

    blk = pl.BlockSpec((tr, C), lambda t: (t, 0))
    return pl.pallas_call(body, name=name, grid=(R // tr,), in_specs=[blk] * 5, out_specs=[blk] * 4,
                          out_shape=[_sds((R, C), F32)] * 4, compiler_params=_params(("parallel",)))(w, gp, gq, m, v)


def _rope(positions, after):
    inv_freq = 1.0 / (ROPE_THETA ** (jnp.arange(0, HEAD_DIM, 2, dtype=F32) / HEAD_DIM))
    inv_freq = jnp.tile(inv_freq, LANES // (HEAD_DIM // 2)).reshape(1, LANES) + after[0, 0]
    return _rope_tables(positions.reshape(-1, 1), inv_freq)


def _local_step(x, rope, target, norms, a_sink, comm):
    T, D = x.shape
    g1, gm, g2, gf = norms
    cos, sin = rope
    no_sink = jnp.zeros((2 * (B_W // LANES),), F32)
    W = {k: comm.weight(k, x) for k in ("wg1", "wu1", "wd1")}

    x1, h1, gate1, up1, act1 = _ffn_fwd(x, comm.order(g1), W["wg1"], W["wu1"], W["wd1"], "ffn1_fwd")
    W["w_in"] = comm.weight("w_in", x1)
    (h2, aq, ak, av, bq1, bk1, bv1, bq4, bk4, bv4, bq16, bk16, bv16) = _proj_rope(x1, gm, W["w_in"], cos, sin)
    a_out, a_lse = _attn_fwd(aq[None], ak[None], av[None], a_sink, A_HALF_WINDOW, True, BF16, "attn_a_fwd", qb=2 * QB, blocks_per_step=4)
    bqs = {1: (bq1[None], bk1[None], bv1[None]), 4: (bq4, bk4, bv4), 16: (bq16, bk16, bv16)}
    b_o, b_l = {}, {}
    for w, d in B_PATTERNS:
        q_, k_, v_ = bqs[d]
        b_o[d], b_l[d] = _attn_fwd(q_, k_, v_, no_sink, w // (2 * d), False, BF16, f"attn_b{d}_fwd")
    cat, lg1, lg4, lg16 = _merge_b(a_out[0], b_o[1][0], b_l[1][0], b_o[4], b_l[4], b_o[16], b_l[16])
    W["w_out"] = comm.weight("w_out", cat)
    x2 = _out_proj(x1, cat, W["w_out"])
    for k in ("wg2", "wu2", "wd2"):
        W[k] = comm.weight(k, x2)
    x3, h3, gate2, up2, act2 = _ffn_fwd(x2, g2, W["wg2"], W["wu2"], W["wd2"], "ffn2_fwd")

    dx3, dgf, loss8 = _final_loss(x3, gf, target)
    dx2, dff2, dgate2, dup2, dg2 = _ffn_dx(dx3, x2, g2, gate2, up2, W["wg2"], W["wu2"], W["wd2"], "ffn2_dx")
    fb = gate2.shape[1] // 2
    dwg2 = _tn(dgate2, h3, fb, "ffn2_dw_gate")
    dwu2 = _tn(dup2, h3, fb, "ffn2_dw_up")
    dwd2 = _tn(act2, dff2, fb, "ffn2_dw_down")
    comm.ready(dict(wg2=dwg2, wu2=dwu2, wd2=dwd2), dwd2[0])

    doa, dla, dob1, dlb1, dob4, dlb4, dob16, dlb16 = _dcat(dx2, W["w_out"], cat)
    dw_out = _tn(cat, dx2, cat.shape[1], "w_out_dw", dep=comm.dep())
    dqa, dka, dva, dsk = _attn_bwd(aq[None], ak[None], av[None], doa[None], a_lse, dla[None], comm.order(a_sink), A_HALF_WINDOW, True,
                                   "attn_a_bwd")
    bwd_in = {1: (dob1[None], lg1[None], dlb1[None]), 4: (dob4, lg4, dlb4), 16: (dob16, lg16, dlb16)}
    bg = {}
    for w, d in B_PATTERNS:
        q_, k_, v_ = bqs[d]
        do_, l_, dl_ = bwd_in[d]
        bg[d] = _attn_bwd(q_, k_, v_, do_, l_, dl_, no_sink, w // (2 * d), False, f"attn_b{d}_bwd")[:3]
    dproj = _rope_bwd_assemble(dqa[0], dka[0], dva[0], [t[0] for t in bg[1]], bg[4], bg[16], cos, sin)
    dw_in = _tn(dproj, h2, dproj.shape[1] // 2, "w_in_dw")
    comm.ready(dict(w_in=dw_in, w_out=dw_out), dw_in[0])
    dx1, dgm = _dh_norm(dproj, W["w_in"], x1, comm.order(gm), dx2)

    dx0, dff1, dgate1, dup1, dg1 = _ffn_dx(dx1, x, g1, gate1, up1, W["wg1"], W["wu1"], W["wd1"], "ffn1_dx")
    dwd1 = _tn(act1, dff1, fb, "ffn1_dw_down")
    comm.ready(dict(wd1=dwd1), dwd1[0])
    dwg1 = _tn(dgate1, h1, fb, "ffn1_dw_gate", dep=comm.dep())
    comm.ready(dict(wg1=dwg1), dwg1[0])
    dwu1 = _tn(dup1, h1, fb, "ffn1_dw_up", dep=comm.dep())
    comm.ready(dict(wu1=dwu1), dwu1[0])

    dsink = dsk[0, :, :, ::HEAD_DIM].sum(axis=1).reshape(-1)
    small = dict(g1=dg1.sum(axis=0), gm=dgm.sum(axis=0), g2=dg2.sum(axis=0), gf=dgf.sum(axis=0), sink=dsink, loss=loss8[0, 0])
    return dx0, small


BIG = ("wg1", "wu1", "wd1", "w_in", "w_out", "wg2", "wu2", "wd2")
GATHER_GROUPS = (("w_in",), ("w_out",), ("wg2", "wu2", "wd2"))


class _Comm:
    def __init__(self, shards, meanwhile):
        x, y, c = _mesh_pos()
        self.me = (2 * x + y).astype(jnp.int32).reshape(1)
        self.shards = shards
        self.tokens = []
        self.waiting = {}
        self.groups = []
        first = ("wg1", "wu1", "wd1")
        fulls = {k: _cast_place(self.me, shards[k], f"cast_{k}") for k in first}
        plan = _neighbour_plan([fulls[k].shape for k in first])
        send, recv, bufs, tok = _push_start("gather_first_start", [fulls[k] for k in first], 2 * len(first), plan, self.me)
        self.side = meanwhile(tok)
        fulls.update({k: _cast_place(self.me, shards[k], f"cast_{k}") for k in BIG if k not in first})
        bufs = _push_wait("gather_first_wait", send, recv, bufs, plan, [fulls["wd2"], *self.side])
        self.full = dict(zip(first, _gather_forward(bufs)))
        dep = self.full["wd1"]
        for gi, names in enumerate(GATHER_GROUPS):
            plan = _gather_plan(len(names))
            send, recv, bufs, tok = _push_start(f"gather_start_{gi}", [fulls[k] for k in names], 3 * len(names), plan, dep)
            self.tokens.append(tok)
            dep = tok
            for k in names:
                self.waiting[k] = (gi, names, send, recv, bufs, plan)

    def order(self, a):
        for tok in self.tokens:
            a = a + tok[0, 0]
        self.tokens = []
        return a

    def dep(self):
        return self.tokens[-1] if self.tokens else None

    def weight(self, name, after):
        if name in self.waiting:
            gi, names, send, recv, bufs, plan = self.waiting[name]
            for k, buf in zip(names, _push_wait(f"gather_wait_{gi}", send, recv, bufs, plan, after)):
                self.full[k] = buf
                del self.waiting[k]
        full = self.full[name]
        return full.reshape(N_CHIPS * full.shape[1], full.shape[2])

    def ready(self, grads, after):
        names = list(grads)
        f32s, b16s = [], []
        for k in names:
            gf, gb = grads[k]
            f32s.append(gf.reshape((N_CHIPS,) + self.shards[k].shape))
            b16s.append(gb.reshape((N_CHIPS,) + self.shards[k].shape))
        n = len(names)
        lands = [lax.empty((N_CHIPS - 1,) + self.shards[k].shape, BF16) for k in names]
        plan = _grad_push_plan(n)
        gi = len(self.groups)
        send, recv, bufs, tok = _push_start(f"grad_start_{gi}", b16s + lands, 3 * n, plan, after)
        self.tokens.append(tok)
        self.groups.append((names, f32s, send, recv, bufs, plan))

    def finish(self):
        out, swaps = {}, []
        after = self.tokens[-1]
        for gi, (names, f32s, send, recv, bufs, plan) in enumerate(self.groups):
            n = len(names)
            bufs = _push_wait(f"grad_wait_{gi}", send, recv, bufs, plan, after)
            mine = [_sum_own(self.me, f32s[i], bufs[n + i], f"sum_{k}") for i, k in enumerate(names)]
            lands = [lax.empty(p.shape, F32) for p in mine]
            send2, recv2, both, after = _push_start(f"swap_start_{gi}", mine + lands, n, _swap_plan(n), after)
            swaps.append((names, send2, recv2, both))
        for gi, (names, send2, recv2, both) in enumerate(swaps):
            n = len(names)
            both = _push_wait(f"swap_wait_{gi}", send2, recv2, both, _swap_plan(n), after)
            for i, k in enumerate(names):
                out[k] = (both[i], both[n + i])
        return out


def kernel(x, positions, norm_ffn1, w_gate1, w_up1, w_down1, norm_mix, w_in, a_sink, w_out, norm_ffn2, w_gate2, w_up2, w_down2, norm_final, loss_target, m_norm_ffn1, m_w_gate1, m_w_up1, m_w_down1, m_norm_mix, m_w_in, m_a_sink, m_w_out, m_norm_ffn2, m_w_gate2, m_w_up2, m_w_down2, m_norm_final, v_norm_ffn1, v_w_gate1, v_w_up1, v_w_down1, v_norm_mix, v_w_in, v_a_sink, v_w_out, v_norm_ffn2, v_w_gate2, v_w_up2, v_w_down2, v_norm_final):
    T, D = x.shape[1], x.shape[2]
    flip = ("wg1", "wu1", "w_in", "wg2", "wu2")

    def rows(k, a):
        return a[0].T if k in flip else a[0]

    given = dict(wg1=(w_gate1, m_w_gate1, v_w_gate1), wu1=(w_up1, m_w_up1, v_w_up1), wd1=(w_down1, m_w_down1, v_w_down1),
                 w_in=(w_in, m_w_in, v_w_in), w_out=(w_out, m_w_out, v_w_out), wg2=(w_gate2, m_w_gate2, v_w_gate2),
                 wu2=(w_up2, m_w_up2, v_w_up2), wd2=(w_down2, m_w_down2, v_w_down2))
    shards = {k: rows(k, given[k][0]) for k in BIG}

    comm = _Comm(shards, lambda tok: _rope(positions[0], tok))

    norms = (norm_ffn1, norm_mix, norm_ffn2, norm_final.reshape(1, D))
    grad_x, small = _local_step(x[0], comm.side, loss_target[0], norms, a_sink[0], comm)

    partial = comm.finish()

    def pad_row(a):
        a = a.reshape(-1)
        return jnp.pad(a, (0, D - a.shape[0]))

    row4 = pad_row(jnp.concatenate([small["sink"], small["loss"].reshape(1)]))
    vec = jnp.stack([small["g1"], small["gm"], small["g2"], small["gf"], row4] + [jnp.zeros((D,), F32)] * 3, axis=0)
    red = _allreduce_small(vec)
    loss = red[4, 8]
    g_small = jnp.stack([red[0], red[1], red[2], red[3], pad_row(red[4, 0:8])] + [jnp.zeros((D,), F32)] * 3, axis=0)

    def small_stack(a1, am, a2, af, ask):
        return jnp.stack([pad_row(a1), pad_row(am), pad_row(a2), pad_row(af), pad_row(ask)] + [jnp.zeros((D,), F32)] * 3, axis=0)

    w_small = small_stack(norm_ffn1, norm_mix, norm_ffn2, norm_final, a_sink)
    m_small = small_stack(m_norm_ffn1, m_norm_mix, m_norm_ffn2, m_norm_final, m_a_sink)
    v_small = small_stack(v_norm_ffn1, v_norm_mix, v_norm_ffn2, v_norm_final, v_a_sink)
    live = small_stack(jnp.ones_like(norm_ffn1), jnp.ones_like(norm_mix), jnp.ones_like(norm_ffn2), jnp.ones_like(norm_final), jnp.ones_like(a_sink))
    v_small = jnp.where(live > 0, v_small, 1.0)

    upd = {}
    for k in BIG:
        outs = _adamw(shards[k], partial[k][0], partial[k][1], rows(k, given[k][1]), rows(k, given[k][2]), f"adamw_{k}")
        upd[k] = tuple((a.T if k in flip else a)[None] for a in outs)
    _, ds_, nms_, nvs_ = _adamw(w_small, g_small, jnp.zeros_like(g_small), m_small, v_small, "adamw_small")

    def small_out(arr):
        return [arr[0].reshape(1, D), arr[1].reshape(1, D), arr[2].reshape(1, D), arr[3], arr[4, 0:8].reshape(1, 8)]

    gs_, dss, nmss, nvss = small_out(g_small), small_out(ds_), small_out(nms_), small_out(nvs_)

    def ordered(i):
        sm = (gs_, dss, nmss, nvss)[i]
        return [sm[0], upd["wg1"][i], upd["wu1"][i], upd["wd1"][i], sm[1], upd["w_in"][i], sm[4], upd["w_out"][i], sm[2],
                upd["wg2"][i], upd["wu2"][i], upd["wd2"][i], sm[3]]

    return (loss, grad_x[None], *ordered(0), *ordered(1), *ordered(2), *ordered(3))
```

```python
import jax
import jax.numpy as jnp
from jax import lax
from jax.experimental import pallas as pl
from jax.experimental.pallas import tpu as pltpu

F32 = jnp.float32
BF16 = jnp.bfloat16

HEAD_DIM = 64
LANES = 128
SUBLANES = 8
A_Q_W, A_KV_W, B_W = 512, 128, 512
A_HALF_WINDOW = 128
B_PATTERNS = ((128, 1), (512, 4), (2048, 16))
ROPE_THETA = 10000.0
NORM_EPS = 1e-6
FFN_RES_WEIGHT = 0.5
ADAM_LR, ADAM_B1, ADAM_B2, ADAM_EPS, ADAM_WD, ADAM_STEP = 0.001, 0.9, 0.999, 1e-08, 0.01, 10
N_CHIPS = 4
N_DEV = 8
QB = 128
SHORT_SEQ = 512
NEG = -1e30
VMEM_LIMIT = 56 * 1024 * 1024
MESH = pl.DeviceIdType.MESH
ANY = pl.BlockSpec(memory_space=pl.ANY)


def _params(sem=None):
    return pltpu.CompilerParams(dimension_semantics=sem, vmem_limit_bytes=VMEM_LIMIT)


def _sds(shape, dtype):
    return jax.ShapeDtypeStruct(tuple(shape), dtype)


def _dot(a, b):
    return jnp.dot(a, b, preferred_element_type=F32)


def _dot_nt(a, b):
    return lax.dot_general(a, b, (((1,), (1,)), ((), ())), preferred_element_type=F32)


def _dot_tn(a, b):
    return lax.dot_general(a, b, (((0,), (0,)), ((), ())), preferred_element_type=F32)


def _rms_stats(x):
    r = lax.rsqrt(jnp.mean(x * x, axis=-1, keepdims=True) + NORM_EPS)
    return x * r, r


def _rms_bwd(dh, x, g):
    xhat, r = _rms_stats(x)
    dxn = dh * g
    dx = r * (dxn - xhat * jnp.mean(dxn * xhat, axis=-1, keepdims=True))
    tm, d = x.shape
    dg = (dh * xhat).reshape(tm // SUBLANES, SUBLANES, d).sum(axis=0)
    return dx, dg


def _sigmoid(x):
    return 1.0 / (1.0 + jnp.exp(-x))


def _swap32(t):
    n = t.shape[-1]
    lane = lax.broadcasted_iota(jnp.int32, t.shape, t.ndim - 1)
    return jnp.where((lane % HEAD_DIM) < HEAD_DIM // 2, pltpu.roll(t, n - HEAD_DIM // 2, axis=t.ndim - 1),
                     pltpu.roll(t, HEAD_DIM // 2, axis=t.ndim - 1))


def _cast_place(me_arr, w, name):
    R, C = w.shape
    tr = R // 2 if (R // 2) % 16 == 0 else R

    def body(me_ref, w_ref, o_ref):
        o_ref[...] = w_ref[...].astype(BF16)

    grid_spec = pltpu.PrefetchScalarGridSpec(
        num_scalar_prefetch=1, grid=(R // tr,), in_specs=[pl.BlockSpec((tr, C), lambda t, me: (t, 0))],
        out_specs=pl.BlockSpec((None, tr, C), lambda t, me: (me[0], t, 0)))
    return pl.pallas_call(body, name=name, grid_spec=grid_spec, out_shape=_sds((N_CHIPS, R, C), BF16),
                          compiler_params=_params(("parallel",)))(me_arr, w)


HBM = pl.BlockSpec(memory_space=pltpu.HBM)
SEM = pl.BlockSpec(memory_space=pltpu.SEMAPHORE)


def _push_start(name, bufs, ncopies, plan, after):
    nb = len(bufs)

    def body(*refs):
        send, recv, token = refs[nb + 1], refs[nb + 2], refs[-1]
        for i, (src, dst, dev) in enumerate(plan(refs[:nb])):
            pltpu.make_async_remote_copy(src_ref=src, dst_ref=dst, send_sem=send.at[i], recv_sem=recv.at[i],
                                         device_id=dev, device_id_type=MESH).start()
        token[...] = jnp.zeros_like(token)

    outs = pl.pallas_call(
        body, name=name,
        out_shape=(pltpu.SemaphoreType.DMA((ncopies,)), pltpu.SemaphoreType.DMA((ncopies,)), *[pltpu.HBM(b.shape, b.dtype) for b in bufs],
                   _sds((SUBLANES, LANES), F32)),
        in_specs=[HBM] * nb + [ANY], out_specs=(SEM, SEM, *([HBM] * nb), pl.BlockSpec(memory_space=pltpu.VMEM)),
        input_output_aliases={i: 2 + i for i in range(nb)},
        compiler_params=pltpu.CompilerParams(has_side_effects=pltpu.SideEffectType.DATAFLOW_SIDE_EFFECTING),
    )(*[pltpu.with_memory_space_constraint(b, pltpu.HBM) for b in bufs], after)
    return outs[0], outs[1], list(outs[2:2 + nb]), outs[-1]


def _push_wait(name, send, recv, bufs, plan, after):
    nb = len(bufs)

    def body(*refs):
        send_ref, recv_ref = refs[nb], refs[nb + 1]
        for i, (src, dst, dev) in enumerate(plan(refs[:nb])):
            cp = pltpu.make_async_remote_copy(src_ref=src, dst_ref=dst, send_sem=send_ref.at[i], recv_sem=recv_ref.at[i],
                                              device_id=dev, device_id_type=MESH)
            cp.wait_send()
            cp.wait_recv()

    afters = list(after) if isinstance(after, (list, tuple)) else [after]
    outs = pl.pallas_call(
        body, name=name, out_shape=tuple(pltpu.HBM(b.shape, b.dtype) for b in bufs),
        in_specs=[HBM] * nb + [SEM, SEM] + [ANY] * len(afters), out_specs=tuple([HBM] * nb),
        input_output_aliases={i: i for i in range(nb)},
        compiler_params=pltpu.CompilerParams(has_side_effects=pltpu.SideEffectType.DATAFLOW_SIDE_EFFECTING),
    )(*bufs, send, recv, *afters)
    return list(outs)


def _mesh_pos():
    return lax.axis_index("x"), lax.axis_index("y"), lax.axis_index("c")


def _chip_peers(x, y, c):
    return [((1 - x, y, c), 2 * (1 - x) + y), ((x, 1 - y, c), 2 * x + (1 - y)), ((1 - x, 1 - y, c), 2 * (1 - x) + (1 - y))]


def _gather_plan(n):
    def plan(refs):
        x, y, c = _mesh_pos()
        me = 2 * x + y
        return [(refs[k].at[me], refs[k].at[me], dev) for k in range(n) for dev, _ in _chip_peers(x, y, c)]
    return plan


def _rows_of(shape, who, quarter=None):
    r2 = shape[1] // 2
    if quarter is None:
        return pl.ds(pl.multiple_of(who * r2, 16), r2)
    return pl.ds(pl.multiple_of(who * r2 + quarter * (r2 // 2), 16), r2 // 2)


def _neighbour_plan(shapes):
    def plan(refs):
        x, y, c = _mesh_pos()
        me = 2 * x + y
        return [(refs[k].at[me, _rows_of(shp, c), :], refs[k].at[me, _rows_of(shp, c), :], dev)
                for k, shp in enumerate(shapes) for dev in ((1 - x, y, c), (x, 1 - y, c))]
    return plan


def _gather_forward(fulls):
    n = len(fulls)

    def body(*refs):
        ins, outs = refs[:n], refs[n:2 * n]
        ici_send, ici_recv, d2d_send, d2d_recv = refs[2 * n:]
        x, y, c = _mesh_pos()
        cx, cy, cd = 2 * (1 - x) + y, 2 * x + (1 - y), 2 * (1 - x) + (1 - y)
        sibling, x_nbr, y_nbr = (x, y, 1 - c), (1 - x, y, c), (x, 1 - y, c)
        started = []

        def push(src, dst, send, recv, dev):
            cp = pltpu.make_async_remote_copy(src_ref=src, dst_ref=dst, send_sem=send, recv_sem=recv, device_id=dev, device_id_type=MESH)
            cp.start()
            started.append(cp)

        def arrived(blk, send, recv):
            pltpu.make_async_remote_copy(src_ref=blk, dst_ref=blk, send_sem=send, recv_sem=recv, device_id=sibling,
                                         device_id_type=MESH).wait_recv()

        for k in range(n):
            shp = fulls[k].shape
            for j, chip in enumerate((cx, cy)):
                push(ins[k].at[chip, _rows_of(shp, c), :], outs[k].at[chip, _rows_of(shp, c), :],
                     d2d_send.at[3 * k + j], d2d_recv.at[3 * k + j], sibling)
            push(ins[k].at[cx, _rows_of(shp, c, 0), :], outs[k].at[cx, _rows_of(shp, c, 0), :], ici_send.at[2 * k], ici_recv.at[2 * k], y_nbr)
            push(ins[k].at[cy, _rows_of(shp, c, 1), :], outs[k].at[cy, _rows_of(shp, c, 1), :], ici_send.at[2 * k + 1], ici_recv.at[2 * k + 1],
                 x_nbr)
        for k in range(n):
            shp = fulls[k].shape
            for q in (0, 1):
                arrived(outs[k].at[cd, _rows_of(shp, c, q), :], ici_send.at[2 * k + q], ici_recv.at[2 * k + q])
            blk = outs[k].at[cd, _rows_of(shp, c), :]
            push(blk, blk, d2d_send.at[3 * k + 2], d2d_recv.at[3 * k + 2], sibling)
        for k in range(n):
            for j, chip in enumerate((cx, cy, cd)):
                arrived(outs[k].at[chip, _rows_of(fulls[k].shape, 1 - c), :], d2d_send.at[3 * k + j], d2d_recv.at[3 * k + j])
        for cp in started:
            cp.wait_send()

    return pl.pallas_call(
        body, name="gather_forward", out_shape=[_sds(f.shape, BF16) for f in fulls],
        in_specs=[ANY] * n, out_specs=[ANY] * n, input_output_aliases={k: k for k in range(n)},
        scratch_shapes=[pltpu.SemaphoreType.DMA((n * 2,))] * 2 + [pltpu.SemaphoreType.DMA((n * 3,))] * 2,
        compiler_params=_params())(*fulls)


def _resident(shape):
    return pl.BlockSpec(shape, lambda i: (0,) * len(shape), pipeline_mode=pl.Buffered(1))


FFN_FWD_CHUNK = 256
FFN_DX_CHUNK = 512


def _chunks(n, step):
    return [(c0, min(step, n - c0)) for c0 in range(0, n, step)]


def _two_phase(chunks, first, second):
    held = {}
    for ci, ch in enumerate(chunks):
        held[ci] = first(*ch)
        if ci >= 1:
            second(*chunks[ci - 1], held.pop(ci - 1))
    last = len(chunks) - 1
    second(*chunks[last], held.pop(last))


def _ffn_fwd(x, g, wgt, wut, wd, name, tm=512):
    T, D = x.shape
    F = wd.shape[0]

    def body(x_ref, g_ref, wg_ref, wu_ref, wd_ref, xo_ref, h_ref, gate_ref, up_ref, act_ref):
        xv = x_ref[...]
        xhat, _ = _rms_stats(xv)
        h = (xhat * g_ref[...]).astype(BF16)
        h_ref[...] = h
        acc = []

        def first(c0, cw):
            return _dot_nt(h, wg_ref[c0:c0 + cw, :]), _dot_nt(h, wu_ref[c0:c0 + cw, :])

        def second(c0, cw, gate_up):
            gate, up = gate_up
            act = ((gate * _sigmoid(gate)) * up).astype(BF16)
            gate_ref[:, c0:c0 + cw] = gate.astype(BF16)
            up_ref[:, c0:c0 + cw] = up.astype(BF16)
            act_ref[:, c0:c0 + cw] = act
            d = _dot(act, wd_ref[c0:c0 + cw, :])
            acc[:] = [d if not acc else acc[0] + d]

        _two_phase(_chunks(F, FFN_FWD_CHUNK), first, second)
        xo_ref[...] = xv + FFN_RES_WEIGHT * acc[0]

    row = pl.BlockSpec((tm, D), lambda i: (i, 0))
    saved = pl.BlockSpec((tm, F), lambda i: (i, 0))
    return pl.pallas_call(
        body, name=name, grid=(T // tm,),
        in_specs=[row, pl.BlockSpec((1, D), lambda i: (0, 0)), _resident(wgt.shape), _resident(wut.shape), _resident(wd.shape)],
        out_specs=[row, row, saved, saved, saved],
        out_shape=[_sds((T, D), F32), _sds((T, D), BF16), _sds((T, F), BF16), _sds((T, F), BF16), _sds((T, F), BF16)],
        compiler_params=_params(("parallel",)))(x, g, wgt, wut, wd)


def _ffn_dx(dxo, x, g, gate_s, up_s, wgt, wut, wd, name, tm=256):
    T, D = x.shape
    F = wd.shape[0]

    def body(dxo_ref, x_ref, g_ref, gate_ref, up_ref, wg_ref, wu_ref, wd_ref, dx_ref, dff_ref, dgate_ref, dup_ref, dg_ref):
        @pl.when(pl.program_id(0) == 0)
        def _():
            dg_ref[...] = jnp.zeros_like(dg_ref)

        d = (FFN_RES_WEIGHT * dxo_ref[...]).astype(BF16)
        dff_ref[...] = d
        dh = []

        def first(c0, cw):
            return _dot_nt(d, wd_ref[c0:c0 + cw, :])

        def second(c0, cw, da):
            gate = gate_ref[:, c0:c0 + cw].astype(F32)
            up = up_ref[:, c0:c0 + cw].astype(F32)
            s = _sigmoid(gate)
            silu = gate * s
            dup = (da * silu).astype(BF16)
            dgate = (da * up * (s * (1.0 + gate * (1.0 - s)))).astype(BF16)
            dgate_ref[:, c0:c0 + cw] = dgate
            dup_ref[:, c0:c0 + cw] = dup
            t = _dot(dgate, wg_ref[c0:c0 + cw, :]) + _dot(dup, wu_ref[c0:c0 + cw, :])
            dh[:] = [t if not dh else dh[0] + t]

        _two_phase(_chunks(F, FFN_DX_CHUNK), first, second)
        dxn, dg = _rms_bwd(dh[0], x_ref[...], g_ref[...])
        dg_ref[...] += dg
        dx_ref[...] = dxo_ref[...] + dxn

    row = pl.BlockSpec((tm, D), lambda i: (i, 0))
    saved = pl.BlockSpec((tm, F), lambda i: (i, 0))
    return pl.pallas_call(
        body, name=name, grid=(T // tm,),
        in_specs=[row, row, pl.BlockSpec((1, D), lambda i: (0, 0)), saved, saved, _resident(wgt.shape), _resident(wut.shape),
                  _resident(wd.shape)],
        out_specs=[row, row, saved, saved, pl.BlockSpec((SUBLANES, D), lambda i: (0, 0))],
        out_shape=[_sds((T, D), F32), _sds((T, D), BF16), _sds((T, F), BF16), _sds((T, F), BF16), _sds((SUBLANES, D), F32)],
        compiler_params=_params(("arbitrary",)))(dxo, x, g, gate_s, up_s, wgt, wut, wd)


def _tn(a, b, mb, name, tk=2048, dep=None):
    T, M = a.shape
    N = b.shape[1]
    nt = T // tk

    def body(a_ref, b_ref, *refs):
        o_ref, ob_ref = refs[-2:]

        @pl.when(pl.program_id(1) == 0)
        def _():
            o_ref[...] = jnp.zeros_like(o_ref)

        o_ref[...] += _dot_tn(a_ref[...].astype(BF16), b_ref[...].astype(BF16))

        @pl.when(pl.program_id(1) == nt - 1)
        def _():
            ob_ref[...] = o_ref[...].astype(BF16)

    o_spec = pl.BlockSpec((mb, N), lambda g, t: (g, 0))
    return pl.pallas_call(
        body, name=name, grid=(M // mb, nt),
        in_specs=[pl.BlockSpec((tk, mb), lambda g, t: (t, g)), pl.BlockSpec((tk, N), lambda g, t: (t, 0))] + ([ANY] if dep is not None else []),
        out_specs=[o_spec, o_spec], out_shape=[_sds((M, N), F32), _sds((M, N), BF16)],
        compiler_params=_params(("parallel", "arbitrary")))(a, b, *([dep] if dep is not None else []))


def _rope_tables(pos_col, inv_freq):
    T = pos_col.shape[0]

    def body(p_ref, f_ref, c_ref, s_ref):
        ang = p_ref[...].astype(F32) * f_ref[...]
        lane = lax.broadcasted_iota(jnp.int32, ang.shape, 1)
        c_ref[...] = jnp.cos(ang)
        sn = jnp.sin(ang)
        s_ref[...] = jnp.where((lane % HEAD_DIM) < HEAD_DIM // 2, -sn, sn)

    tm = 1024
    return pl.pallas_call(
        body, name="rope_tables", grid=(T // tm,),
        in_specs=[pl.BlockSpec((tm, 1), lambda i: (i, 0)), pl.BlockSpec((1, LANES), lambda i: (0, 0))],
        out_specs=[pl.BlockSpec((tm, LANES), lambda i: (i, 0))] * 2,
        out_shape=[_sds((T, LANES), F32)] * 2, compiler_params=_params(("parallel",)))(pos_col, inv_freq)


def _deinterleave(scr, out_ref, d, tm, nblk):
    for r in range(d):
        for cb in range(nblk):
            out_ref[r, :, cb * LANES:(cb + 1) * LANES] = scr[cb, pl.ds(r, tm // d, stride=d), :].astype(out_ref.dtype)


def _interleave(in_ref, scr, d, tm, nblk):
    for r in range(d):
        for cb in range(nblk):
            scr[cb, pl.ds(r, tm // d, stride=d), :] = in_ref[r, :, cb * LANES:(cb + 1) * LANES].astype(F32)


def _proj_rope(x, g, w_in, cos, sin, tm=512):
    T, D = x.shape
    dils = [d for _, d in B_PATTERNS if d > 1]
    nbb = B_W // LANES
    scale = HEAD_DIM ** -0.5
    cuts = [0, A_Q_W, A_Q_W + A_KV_W, A_Q_W + 2 * A_KV_W, A_Q_W + 2 * A_KV_W + B_W, A_Q_W + 2 * A_KV_W + 2 * B_W,
            A_Q_W + 2 * A_KV_W + 3 * B_W]

    def body(x_ref, g_ref, w_ref, c_ref, s_ref, h_ref, aq_ref, ak_ref, av_ref, *rest):
        b_refs, scr = rest[:-1], rest[-1]
        xhat, _ = _rms_stats(x_ref[...])
        h = (xhat * g_ref[...]).astype(BF16)
        h_ref[...] = h
        cs, sn = c_ref[...], s_ref[...]

        def project(idx, ref, rope, mult, which):
            return _dot_nt(h, w_ref[cuts[idx]:cuts[idx + 1], :])

        def finish(idx, ref, rope, mult, which, whole):
            for cb in range((cuts[idx + 1] - cuts[idx]) // LANES):
                p = whole[:, cb * LANES:(cb + 1) * LANES]
                if rope:
                    p = p * cs + _swap32(p) * sn
                if mult != 1.0:
                    p = p * mult
                ref[:, cb * LANES:(cb + 1) * LANES] = p.astype(BF16)
                if which is not None:
                    scr[which, cb] = p
            if which is not None:
                for di, d in enumerate(dils):
                    _deinterleave(scr.at[which], b_refs[3 * (di + 1) + which], d, tm, nbb)

        _two_phase([(0, aq_ref, True, scale, None), (1, ak_ref, True, 1.0, None), (2, av_ref, False, 1.0, None),
                    (3, b_refs[0], True, scale, 0), (4, b_refs[1], True, 1.0, 1), (5, b_refs[2], False, 1.0, 2)], project, finish)

    row = lambda w: pl.BlockSpec((tm, w), lambda i: (i, 0))
    out_specs = [row(D), row(A_Q_W), row(A_KV_W), row(A_KV_W)] + [row(B_W)] * 3
    out_shape = [_sds((T, D), BF16), _sds((T, A_Q_W), BF16), _sds((T, A_KV_W), BF16), _sds((T, A_KV_W), BF16)] + [_sds((T, B_W), BF16)] * 3
    for d in dils:
        out_specs += [pl.BlockSpec((d, tm // d, B_W), lambda i: (0, i, 0))] * 3
        out_shape += [_sds((d, T // d, B_W), BF16)] * 3
    return pl.pallas_call(
        body, name="proj_rope", grid=(T // tm,),
        in_specs=[row(D), pl.BlockSpec((1, D), lambda i: (0, 0)), pl.BlockSpec(w_in.shape, lambda i: (0, 0)), row(LANES), row(LANES)],
        out_specs=out_specs, out_shape=out_shape, scratch_shapes=[pltpu.VMEM((3, nbb, tm, LANES), F32)],
        compiler_params=_params(("parallel",)))(x, g, w_in, cos, sin)


def _band_bias(rel, qb, kw, hw):
    ri = lax.broadcasted_iota(jnp.int32, (2 * qb, kw), 0) & (qb - 1)
    ci = lax.broadcasted_iota(jnp.int32, (2 * qb, kw), 1)
    return jnp.where(jnp.abs(ri + rel - ci) <= hw, 0.0, NEG).astype(F32)


def _stack_heads(x, lo):
    z = jnp.zeros_like(x)
    return jnp.concatenate([jnp.where(lo, x, z), jnp.where(lo, z, x)], axis=0)


def _unstack_heads(y, lo):
    qb = y.shape[0] // 2
    return jnp.where(lo, y[:qb], y[qb:])


def _band_setup(bias_scr, qb, kw, hw):
    if bias_scr is not None:
        for i in range(3):
            bias_scr[i] = _band_bias(i * hw, qb, kw, hw)


def _band_window(bias_scr, qs, L, qb, kw, hw):
    ws = pl.multiple_of(jnp.clip(qs - hw, 0, L - kw), 64)
    if bias_scr is None:
        return ws, _band_bias(qs - ws, qb, kw, hw)
    return ws, bias_scr[lax.shift_right_logical(qs - ws, hw.bit_length() - 1)]


def _dup_kv_head(src_ref, dst_ref, head, L):
    step = min(L, 1024)
    for r0 in range(0, L, step):
        xf = src_ref[r0:r0 + step, :].astype(F32)
        lane = lax.broadcasted_iota(jnp.int32, xf.shape, 1)
        keep = jnp.logical_xor(lane < HEAD_DIM, head == 1)
        dst_ref[r0:r0 + step, :] = jnp.where(keep, xf, pltpu.roll(xf, HEAD_DIM, axis=1)).astype(dst_ref.dtype)


def _attn_fwd(q, k, v, sink, hw, gqa, out_dtype, name, qb=QB, blocks_per_step=8, out_cols=None):
    NB, L, Cq = q.shape
    Ls = min(L, 2048)
    kw = min(qb + 2 * hw, L)
    tables = L >= qb + 2 * hw
    unroll = min(blocks_per_step, Ls // qb)
    nlb = 1 if (gqa or L > SHORT_SEQ) else Cq // LANES

    def body(sink_ref, q_ref, k_ref, v_ref, o_ref, lse_ref, *scr):
        b, s_idx = pl.program_id(1), pl.program_id(2)
        bias_scr = scr[0] if tables else None
        _band_setup(bias_scr, qb, kw, hw)
        if gqa:
            kd, vd = scr[-2:]

            @pl.when(s_idx == 0)
            def _():
                _dup_kv_head(k_ref, kd, b // 2, L)
                _dup_kv_head(v_ref, vd, b // 2, L)
        else:
            kd, vd = k_ref, v_ref
        lane = lax.broadcasted_iota(jnp.int32, (qb, LANES), 1)
        lo = lane < HEAD_DIM
        if gqa:
            row = lax.broadcasted_iota(jnp.int32, (2 * qb, 1), 0)
            sk = jnp.where(row < qb, sink_ref[2 * b], sink_ref[2 * b + 1])

        def block(ql, col):
            qs = s_idx * Ls + ql
            ws, bias = _band_window(bias_scr, qs, L, qb, kw, hw)
            return ws, _dot_nt(_stack_heads(q_ref[pl.ds(ql, qb), col], lo), kd[pl.ds(ws, kw), col]) + bias

        def finish(ql, col, scores):
            ws, s = scores
            m = jnp.max(s, axis=-1, keepdims=True)
            if gqa:
                m = jnp.maximum(m, sk)
            p = jnp.exp(s - m)
            den = jnp.sum(p, axis=-1, keepdims=True)
            if gqa:
                den = den + jnp.exp(sk - m)
            o = _dot(p.astype(BF16), vd[pl.ds(ws, kw), col]) * (1.0 / den)
            o_ref[pl.ds(ql, qb), col] = _unstack_heads(o, lo).astype(o_ref.dtype)
            lse_ref[pl.ds(ql, qb), col] = _unstack_heads(m + jnp.log(den), lo)

        for lb in range(nlb):
            def step(n, carry, col=slice(lb * LANES, (lb + 1) * LANES)):
                _two_phase([(pl.multiple_of((n * unroll + u) * qb, qb), col) for u in range(unroll)], block, finish)
                return carry

            lax.fori_loop(0, Ls // (qb * unroll), step, 0)

    kv_map = (lambda r, b, s: (r, 0, 0)) if gqa else (lambda r, b, s: (r, 0, b))
    seg = pl.BlockSpec((None, Ls, nlb * LANES), lambda r, b, s: (r, s, b))
    return pl.pallas_call(
        body, name=name, grid=(NB, Cq // (nlb * LANES), L // Ls),
        in_specs=[pl.BlockSpec(memory_space=pltpu.SMEM), seg, pl.BlockSpec((None, L, nlb * LANES), kv_map),
                  pl.BlockSpec((None, L, nlb * LANES), kv_map)],
        out_specs=[seg, seg], out_shape=[_sds((NB, L, out_cols or Cq), out_dtype), _sds((NB, L, Cq), F32)],
        scratch_shapes=([pltpu.VMEM((3, 2 * qb, kw), F32)] if tables else []) + ([pltpu.VMEM((L, LANES), BF16)] * 2 if gqa else []),
        compiler_params=_params(("parallel", "parallel", "arbitrary")))(sink, q, k, v)


def _attn_bwd(q, k, v, do, lse, delta, sink, hw, gqa, name, qb=QB, blocks_per_step=8):
    NB, L, Cq = q.shape
    Ck = k.shape[2]
    Ls = min(L, 2048)
    kw = min(qb + 2 * hw, L)
    reps = kw // LANES
    nseg = L // Ls
    scale = HEAD_DIM ** -0.5
    tables = L >= qb + 2 * hw
    unroll = min(blocks_per_step, Ls // qb)
    nlb = 1 if (gqa or L > SHORT_SEQ) else Cq // LANES

    def body(sink_ref, q_ref, do_ref, lse_ref, dl_ref, k_ref, v_ref, dq_ref, dk_ref, dv_ref, dsk_ref, *scr):
        b, s_idx = pl.program_id(1), pl.program_id(2)
        lane = lax.broadcasted_iota(jnp.int32, (qb, LANES), 1)
        lo = lane < HEAD_DIM
        bias_scr = scr[0] if tables else None
        _band_setup(bias_scr, qb, kw, hw)
        if gqa:
            kd, vd, dk_acc, dv_acc, dsk_acc = scr[-5:]

            @pl.when(s_idx == 0)
            def _():
                _dup_kv_head(k_ref, kd, b // 2, L)
                _dup_kv_head(v_ref, vd, b // 2, L)
                dk_acc[...] = jnp.zeros_like(dk_acc)
                dv_acc[...] = jnp.zeros_like(dv_acc)
                dsk_acc[...] = jnp.zeros_like(dsk_acc)

            @pl.when((s_idx == 0) & (b == 0))
            def _():
                dk_ref[...] = jnp.zeros_like(dk_ref)
                dv_ref[...] = jnp.zeros_like(dv_ref)
        else:
            kd, vd = k_ref, v_ref
            dk_acc, dv_acc = scr[-2:]

            @pl.when(s_idx == 0)
            def _():
                dk_acc[...] = jnp.zeros_like(dk_acc)
                dv_acc[...] = jnp.zeros_like(dv_acc)

        def block(ql, col):
            qs = s_idx * Ls + ql
            ws, bias = _band_window(bias_scr, qs, L, qb, kw, hw)
            qv, dov = q_ref[pl.ds(ql, qb), col], do_ref[pl.ds(ql, qb), col]
            lse, dl = lse_ref[pl.ds(ql, qb), col], dl_ref[pl.ds(ql, qb), col]
            kv_, vv = kd[pl.ds(ws, kw), col], vd[pl.ds(ws, kw), col]
            q2, do2 = _stack_heads(qv, lo), _stack_heads(dov, lo)
            return ws, q2, do2, lse, dl, _dot_nt(q2, kv_) + bias, _dot_nt(do2, vv)

        def finish(ql, col, held):
            ws, q2, do2, lse, dl, s, dp = held
            lse_sw, dl_sw = pltpu.roll(lse, HEAD_DIM, axis=1), pltpu.roll(dl, HEAD_DIM, axis=1)
            lse2 = jnp.concatenate([jnp.where(lo, lse, lse_sw), jnp.where(lo, lse_sw, lse)], axis=0)
            dl2 = jnp.concatenate([jnp.where(lo, dl, dl_sw), jnp.where(lo, dl_sw, dl)], axis=0)
            p = jnp.exp(s - jnp.tile(lse2, (1, reps)))
            ds = (p * (dp - jnp.tile(dl2, (1, reps)))).astype(BF16)
            dq_ref[pl.ds(ql, qb), col] = (_unstack_heads(_dot(ds, kd[pl.ds(ws, kw), col]), lo) * scale).astype(dq_ref.dtype)
            both = _dot_tn(jnp.concatenate([ds, p.astype(BF16)], axis=1), jnp.concatenate([q2, do2], axis=1))
            dk_acc[pl.ds(ws, kw), col] += both[:kw, :LANES]
            dv_acc[pl.ds(ws, kw), col] += both[kw:, LANES:]
            if gqa:
                sk = jnp.where(lo, sink_ref[2 * b], sink_ref[2 * b + 1])
                dsk_acc[...] += -jnp.exp(sk - lse) * dl

        for lb in range(nlb):
            def step(n, carry, col=slice(lb * LANES, (lb + 1) * LANES)):
                _two_phase([(pl.multiple_of((n * unroll + u) * qb, qb), col) for u in range(unroll)], block, finish)
                return carry

            lax.fori_loop(0, Ls // (qb * unroll), step, 0)

        if gqa:
            @pl.when(s_idx == nseg - 1)
            def _():
                step_rows = min(L, 1024)
                for r0 in range(0, L, step_rows):
                    lanek = lax.broadcasted_iota(jnp.int32, (step_rows, LANES), 1)
                    mine = jnp.logical_xor(lanek < HEAD_DIM, (b // 2) == 1)
                    for acc, ref in ((dk_acc, dk_ref), (dv_acc, dv_ref)):
                        a = acc[r0:r0 + step_rows, :]
                        ref[r0:r0 + step_rows, :] += jnp.where(mine, a + pltpu.roll(a, HEAD_DIM, axis=1), 0.0)
                dsk_ref[...] = dsk_acc[...].reshape(qb // SUBLANES, SUBLANES, LANES).sum(axis=0)
        else:
            dsk_ref[...] = jnp.zeros_like(dsk_ref)

            @pl.when(s_idx == nseg - 1)
            def _():
                dk_ref[...] = dk_acc[...].astype(dk_ref.dtype)
                dv_ref[...] = dv_acc[...].astype(dv_ref.dtype)

    kv_map = (lambda r, b, s: (r, 0, 0)) if gqa else (lambda r, b, s: (r, 0, b))
    seg = pl.BlockSpec((None, Ls, nlb * LANES), lambda r, b, s: (r, s, b))
    full = pl.BlockSpec((None, L, nlb * LANES), kv_map)
    scratch = [pltpu.VMEM((3, 2 * qb, kw), F32)] if tables else []
    if gqa:
        scratch += [pltpu.VMEM((L, LANES), BF16)] * 2 + [pltpu.VMEM((L, LANES), F32)] * 2 + [pltpu.VMEM((qb, LANES), F32)]
    else:
        scratch += [pltpu.VMEM((L, nlb * LANES), F32)] * 2
    kv_dtype = F32 if gqa else BF16
    return pl.pallas_call(
        body, name=name, grid=(NB, Cq // (nlb * LANES), nseg),
        in_specs=[pl.BlockSpec(memory_space=pltpu.SMEM), seg, seg, seg, seg, full, full],
        out_specs=[seg, full, full, pl.BlockSpec((None, None, SUBLANES, LANES), lambda r, b, s: (r, b, 0, 0))],
        out_shape=[_sds((NB, L, Cq), BF16), _sds((NB, L, Ck), kv_dtype), _sds((NB, L, Ck), kv_dtype),
                   _sds((NB, Cq // LANES, SUBLANES, LANES), F32)],
        scratch_shapes=scratch,
        compiler_params=_params(("arbitrary", "arbitrary", "arbitrary")))(sink, q, do, lse, delta, k, v)


def _dilated_fwd(cat, qkv, hw, tile=2048):
    T = cat.shape[0]
    dils = sorted(qkv)
    nbb, na = B_W // LANES, A_Q_W // LANES
    qb, kw = QB, QB + 2 * hw
    rows_merge = 256
    assert T % tile == 0 and all(tile % (d * qb) == 0 and T // d >= kw for d in dils)

    def body(cat_in, *refs):
        qkv_refs = {d: refs[3 * j:3 * j + 3] for j, d in enumerate(dils)}
        cat_ref, lg_refs = refs[3 * len(dils)], refs[3 * len(dils) + 1:4 * len(dils) + 1]
        o_scr, l_scr, bias_scr = refs[4 * len(dils) + 1:]
        i = pl.program_id(1)
        _band_setup(bias_scr, qb, kw, hw)
        lane = lax.broadcasted_iota(jnp.int32, (qb, LANES), 1)
        lo = lane < HEAD_DIM
        for pi, d in enumerate(dils):
            q_ref, k_ref, v_ref = qkv_refs[d]
            L, rows = T // d, tile // d

            def place(r, n, d=d):
                return pl.ds(r + d * n * qb, qb, stride=d) if d > 1 else pl.ds(n * qb, qb)

            def scores(r, n, q_ref=q_ref, k_ref=k_ref, L=L, rows=rows):
                ws, bias = _band_window(bias_scr, i * rows + n * qb, L, qb, kw, hw)
                return ws, _dot_nt(_stack_heads(q_ref[r, n * qb:(n + 1) * qb, :], lo), k_ref[r, pl.ds(ws, kw), :]) + bias

            def finish(r, n, held, v_ref=v_ref, pi=pi, place=place):
                ws, s = held
                m = jnp.max(s, axis=-1, keepdims=True)
                p = jnp.exp(s - m)
                den = jnp.sum(p, axis=-1, keepdims=True)
                o = _dot(p.astype(BF16), v_ref[r, pl.ds(ws, kw), :]) * (1.0 / den)
                o_scr[pi, place(r, n), :] = _unstack_heads(o, lo)
                l_scr[pi, place(r, n), :] = _unstack_heads(m + jnp.log(den), lo)

            blocks = [(r, n) for r in range(d) for n in range(rows // qb)]
            for g0 in range(0, len(blocks), 8):
                _two_phase(blocks[g0:g0 + 8], scores, finish)

        for r0 in range(0, tile, rows_merge):
            rs = slice(r0, r0 + rows_merge)
            ls_ = [l_scr[pi, rs, :] for pi in range(len(dils))]
            m = ls_[0]
            for l in ls_[1:]:
                m = jnp.maximum(m, l)
            es = [jnp.exp(l - m) for l in ls_]
            den, out = es[0], es[0] * o_scr[0, rs, :]
            for pi in range(1, len(dils)):
                den = den + es[pi]
                out = out + es[pi] * o_scr[pi, rs, :]
            cat_ref[rs, :] = (out * (1.0 / den)).astype(BF16)
            l_scr[0, rs, :] = m + jnp.log(den)
        for lg_ref, d in zip(lg_refs, dils):
            for r in range(d):
                lg_ref[r] = l_scr[0, pl.ds(r, tile // d, stride=d), :] if d > 1 else l_scr[0]

    in_specs = [pl.BlockSpec(memory_space=pl.ANY)]
    operands = [cat]
    for d in dils:
        in_specs += [pl.BlockSpec((d, tile // d, LANES), lambda b, i: (0, i, b))] + [pl.BlockSpec((d, T // d, LANES), lambda b, i: (0, 0, b))] * 2
        operands += list(qkv[d])
    return pl.pallas_call(
        body, name="dilated_fwd", grid=(nbb, T // tile), in_specs=in_specs,
        out_specs=[pl.BlockSpec((tile, LANES), lambda b, i: (i, na + b))] + [pl.BlockSpec((d, tile // d, LANES), lambda b, i: (0, i, b)) for d in dils],
        out_shape=[_sds(cat.shape, BF16)] + [_sds((d, T // d, B_W), F32) for d in dils],
        input_output_aliases={0: 0},
        scratch_shapes=[pltpu.VMEM((len(dils), tile, LANES), F32)] * 2 + [pltpu.VMEM((3, 2 * qb, kw), F32)],
        compiler_params=_params(("parallel", "arbitrary")))(*operands)


def _out_proj(x, cat, w_out, tm=512):
    T, D = x.shape

    def body(x_ref, c_ref, w_ref, o_ref):
        o_ref[...] = x_ref[...] + _dot(c_ref[...], w_ref[...])

    row = lambda w: pl.BlockSpec((tm, w), lambda i: (i, 0))
    return pl.pallas_call(
        body, name="out_proj", grid=(T // tm,), in_specs=[row(D), row(cat.shape[1]), pl.BlockSpec(w_out.shape, lambda i: (0, 0))],
        out_specs=row(D), out_shape=_sds((T, D), F32), compiler_params=_params(("parallel",)))(x, cat, w_out)


def _final_loss(x, g, target, tm=512):
    T, D = x.shape

    def body(x_ref, g_ref, t_ref, dx_ref, dg_ref, loss_ref):
        @pl.when(pl.program_id(0) == 0)
        def _():
            dg_ref[...] = jnp.zeros_like(dg_ref)
            loss_ref[...] = jnp.zeros_like(loss_ref)

        xv, gv = x_ref[...], g_ref[...]
        xhat, _ = _rms_stats(xv)
        err = xhat * gv - t_ref[...]
        loss_ref[...] += 0.5 * jnp.sum(jnp.sum(err * err, axis=-1, keepdims=True) * (1.0 / D), axis=0, keepdims=True)
        dx, dg = _rms_bwd(err * (1.0 / D), xv, gv)
        dx_ref[...] = dx
        dg_ref[...] += dg

    row = pl.BlockSpec((tm, D), lambda i: (i, 0))
    return pl.pallas_call(
        body, name="final_loss", grid=(T // tm,), in_specs=[row, pl.BlockSpec((1, D), lambda i: (0, 0)), row],
        out_specs=[row, pl.BlockSpec((SUBLANES, D), lambda i: (0, 0)), pl.BlockSpec((SUBLANES, LANES), lambda i: (0, 0))],
        out_shape=[_sds((T, D), F32), _sds((SUBLANES, D), F32), _sds((SUBLANES, LANES), F32)],
        compiler_params=_params(("arbitrary",)))(x, g, target)


def _dcat(dx, w_out, cat, tm=512):
    T, D = dx.shape
    C = cat.shape[1]
    nba, nbb = A_Q_W // LANES, B_W // LANES

    def body(dx_ref, w_ref, cat_ref, doa_ref, dla_ref, dob1_ref, dlb1_ref, dob4_ref, dlb4_ref, dob16_ref, dlb16_ref, sdo, sdl):
        dc = _dot_nt(dx_ref[...].astype(BF16), w_ref[...])
        ri = lax.broadcasted_iota(jnp.int32, (LANES, LANES), 0)
        ci = lax.broadcasted_iota(jnp.int32, (LANES, LANES), 1)
        same_head = ((ri // HEAD_DIM) == (ci // HEAD_DIM)).astype(BF16)
        for cb in range(C // LANES):
            cols = slice(cb * LANES, (cb + 1) * LANES)
            blk = dc[:, cols]
            prod = blk * cat_ref[:, cols].astype(F32)
            hi = prod.astype(BF16)
            lo_ = (prod - hi.astype(F32)).astype(BF16)
            dl = _dot(hi, same_head) + _dot(lo_, same_head)
            if cb < nba:
                doa_ref[:, cols] = blk.astype(BF16)
                dla_ref[:, cols] = dl
            else:
                bcols = slice((cb - nba) * LANES, (cb - nba + 1) * LANES)
                dob1_ref[:, bcols] = blk.astype(BF16)
                dlb1_ref[:, bcols] = dl
                sdo[cb - nba] = blk
                sdl[cb - nba] = dl
        _deinterleave(sdo, dob4_ref, 4, tm, nbb)
        _deinterleave(sdl, dlb4_ref, 4, tm, nbb)
        _deinterleave(sdo, dob16_ref, 16, tm, nbb)
        _deinterleave(sdl, dlb16_ref, 16, tm, nbb)

    row = lambda w: pl.BlockSpec((tm, w), lambda i: (i, 0))
    perm = lambda d: pl.BlockSpec((d, tm // d, B_W), lambda i: (0, i, 0))
    return pl.pallas_call(
        body, name="dcat", grid=(T // tm,), in_specs=[row(D), pl.BlockSpec(w_out.shape, lambda i: (0, 0)), row(C)],
        out_specs=[row(A_Q_W), row(A_Q_W), row(B_W), row(B_W), perm(4), perm(4), perm(16), perm(16)],
        out_shape=[_sds((T, A_Q_W), BF16), _sds((T, A_Q_W), F32), _sds((T, B_W), BF16), _sds((T, B_W), F32),
                   _sds((4, T // 4, B_W), BF16), _sds((4, T // 4, B_W), F32), _sds((16, T // 16, B_W), BF16), _sds((16, T // 16, B_W), F32)],
        scratch_shapes=[pltpu.VMEM((nbb, tm, LANES), F32)] * 2, compiler_params=_params(("parallel",)))(dx, w_out, cat)


def _rope_bwd_assemble(dqa, dka, dva, b1, b4, b16, cos, sin, tm=512):
    T = dqa.shape[0]
    nbb = B_W // LANES
    width = A_Q_W + 2 * A_KV_W + 3 * B_W

    def body(dqa_ref, dka_ref, dva_ref, q1, k1, v1, q4, k4, v4, q16, k16, v16, c_ref, s_ref, o_ref, scr):
        cs, sn = c_ref[...], s_ref[...]

        def unrope(t):
            return t * cs + _swap32(t * sn)

        col = 0
        for ref, rope in ((dqa_ref, True), (dka_ref, True), (dva_ref, False)):
            for cb in range(ref.shape[1] // LANES):
                t = ref[:, cb * LANES:(cb + 1) * LANES].astype(F32)
                o_ref[:, col:col + LANES] = (unrope(t) if rope else t).astype(BF16)
                col += LANES
        for which, (r1, r4, r16, rope) in enumerate(((q1, q4, q16, True), (k1, k4, k16, True), (v1, v4, v16, False))):
            _interleave(r4, scr.at[0], 4, tm, nbb)
            _interleave(r16, scr.at[1], 16, tm, nbb)
            for cb in range(nbb):
                t = r1[:, cb * LANES:(cb + 1) * LANES].astype(F32) + scr[0, cb] + scr[1, cb]
                o_ref[:, col:col + LANES] = (unrope(t) if rope else t).astype(BF16)
                col += LANES

    row = lambda w: pl.BlockSpec((tm, w), lambda i: (i, 0))
    perm = lambda d: pl.BlockSpec((d, tm // d, B_W), lambda i: (0, i, 0))
    return pl.pallas_call(
        body, name="rope_bwd", grid=(T // tm,),
        in_specs=[row(A_Q_W), row(A_KV_W), row(A_KV_W)] + [row(B_W)] * 3 + [perm(4)] * 3 + [perm(16)] * 3 + [row(LANES), row(LANES)],
        out_specs=row(width), out_shape=_sds((T, width), BF16), scratch_shapes=[pltpu.VMEM((2, nbb, tm, LANES), F32)],
        compiler_params=_params(("parallel",)))(dqa, dka, dva, *b1, *b4, *b16, cos, sin)


def _dh_norm(dproj, w_in, x, g, dres, tm=512):
    T, D = x.shape

    def body(dp_ref, w_ref, x_ref, g_ref, dr_ref, dx_ref, dg_ref):
        @pl.when(pl.program_id(0) == 0)
        def _():
            dg_ref[...] = jnp.zeros_like(dg_ref)

        dxn, dg = _rms_bwd(_dot(dp_ref[...], w_ref[...]), x_ref[...], g_ref[...])
        dg_ref[...] += dg
        dx_ref[...] = dr_ref[...] + dxn

    row = lambda w: pl.BlockSpec((tm, w), lambda i: (i, 0))
    return pl.pallas_call(
        body, name="dh_norm", grid=(T // tm,),
        in_specs=[row(dproj.shape[1]), pl.BlockSpec(w_in.shape, lambda i: (0, 0)), row(D), pl.BlockSpec((1, D), lambda i: (0, 0)), row(D)],
        out_specs=[row(D), pl.BlockSpec((SUBLANES, D), lambda i: (0, 0))],
        out_shape=[_sds((T, D), F32), _sds((SUBLANES, D), F32)], compiler_params=_params(("arbitrary",)))(dproj, w_in, x, g, dres)


def _grad_push_plan(n):
    def plan(refs):
        x, y, c = _mesh_pos()
        return [(refs[k].at[chip], refs[n + k].at[rel], dev) for k in range(n) for rel, (dev, chip) in enumerate(_chip_peers(x, y, c))]
    return plan


def _sum_own(me_arr, g, landed, name):
    ns, R, C = g.shape
    tr = R // 2 if (R // 2) % 16 == 0 else R

    def body(me_ref, g_ref, x_ref, o_ref):
        acc = g_ref[...]
        for rel in range(ns - 1):
            acc = acc + x_ref[rel].astype(F32)
        o_ref[...] = acc

    grid_spec = pltpu.PrefetchScalarGridSpec(
        num_scalar_prefetch=1, grid=(R // tr,),
        in_specs=[pl.BlockSpec((None, tr, C), lambda t, me: (me[0], t, 0)), pl.BlockSpec((ns - 1, tr, C), lambda t, me: (0, t, 0))],
        out_specs=pl.BlockSpec((tr, C), lambda t, me: (t, 0)))
    return pl.pallas_call(body, name=name, grid_spec=grid_spec, out_shape=_sds((R, C), F32),
                          compiler_params=_params(("parallel",)))(me_arr, g, landed)


def _swap_plan(n):
    def plan(refs):
        x, y, c = _mesh_pos()
        return [(refs[k], refs[n + k], (x, y, 1 - c)) for k in range(n)]
    return plan


def _allreduce_small(v):
    rows, W = v.shape

    def body(v_ref, o_ref, buf, send, recv):
        x, y, c = _mesh_pos()
        me = 4 * x + 2 * y + c
        cps = []
        for m in range(1, N_DEV):
            dev = (x ^ (m >> 2), y ^ ((m >> 1) & 1), c ^ (m & 1))
            cp = pltpu.make_async_remote_copy(src_ref=v_ref, dst_ref=buf.at[me], send_sem=send.at[m - 1], recv_sem=recv.at[m - 1],
                                              device_id=dev, device_id_type=MESH)
            cp.start()
            cps.append(cp)
        for m in range(1, N_DEV):
            pltpu.make_async_remote_copy(src_ref=v_ref, dst_ref=buf.at[me ^ m], send_sem=send.at[m - 1], recv_sem=recv.at[m - 1],
                                         device_id=(x, y, c), device_id_type=MESH).wait_recv()
        for cp in cps:
            cp.wait_send()
        buf[me] = v_ref[...]
        acc = buf[0]
        for i in range(1, N_DEV):
            acc = acc + buf[i]
        o_ref[...] = acc

    return pl.pallas_call(
        body, name="allreduce_small", out_shape=_sds((rows, W), F32),
        scratch_shapes=[pltpu.VMEM((N_DEV, rows, W), F32), pltpu.SemaphoreType.DMA((N_DEV - 1,)), pltpu.SemaphoreType.DMA((N_DEV - 1,))],
        compiler_params=_params())(v)


def _adamw(w, gp, gq, m, v, name):
    R, C = w.shape
    tr = R // 2 if (R // 2) % SUBLANES == 0 else R
    c1 = 1.0 / (1.0 - ADAM_B1 ** ADAM_STEP)
    c2 = 1.0 / (1.0 - ADAM_B2 ** ADAM_STEP)

    def body(w_ref, gp_ref, gq_ref, m_ref, v_ref, g_ref, d_ref, nm_ref, nv_ref):
        gv = gp_ref[...] + gq_ref[...]
        nm = ADAM_B1 * m_ref[...] + (1.0 - ADAM_B1) * gv
        nv = ADAM_B2 * v_ref[...] + (1.0 - ADAM_B2) * (gv * gv)
        g_ref[...] = gv
        d_ref[...] = -ADAM_LR * ((nm * c1) / (jnp.sqrt(nv * c2) + ADAM_EPS) + ADAM_WD * w_ref[...])
        nm_ref[...] = nm
        nv_ref[...] = nv

    blk = pl.BlockSpec((tr, C), lambda t: (t, 0))
    return pl.pallas_call(body, name=name, grid=(R // tr,), in_specs=[blk] * 5, out_specs=[blk] * 4,
                          out_shape=[_sds((R, C), F32)] * 4, compiler_params=_params(("parallel",)))(w, gp, gq, m, v)


def _rope(positions, after):
    inv_freq = 1.0 / (ROPE_THETA ** (jnp.arange(0, HEAD_DIM, 2, dtype=F32) / HEAD_DIM))
    inv_freq = jnp.tile(inv_freq, LANES // (HEAD_DIM // 2)).reshape(1, LANES) + after[0, 0]
    return _rope_tables(positions.reshape(-1, 1), inv_freq)


def _local_step(x, rope, target, norms, a_sink, comm):
    T, D = x.shape
    g1, gm, g2, gf = norms
    cos, sin = rope
    no_sink = jnp.zeros((2 * (B_W // LANES),), F32)
    W = {k: comm.weight(k, x) for k in ("wg1", "wu1", "wd1")}

    x1, h1, gate1, up1, act1 = _ffn_fwd(x, comm.order(g1), W["wg1"], W["wu1"], W["wd1"], "ffn1_fwd")
    W["w_in"] = comm.weight("w_in", x1)
    (h2, aq, ak, av, bq1, bk1, bv1, bq4, bk4, bv4, bq16, bk16, bv16) = _proj_rope(x1, gm, W["w_in"], cos, sin)
    cat, a_lse = _attn_fwd(aq[None], ak[None], av[None], a_sink, A_HALF_WINDOW, True, BF16, "attn_a_fwd", qb=2 * QB, blocks_per_step=4,
                           out_cols=A_Q_W + B_W)
    bqs = {1: (bq1[None], bk1[None], bv1[None]), 4: (bq4, bk4, bv4), 16: (bq16, bk16, bv16)}
    (b_hw,) = {w // (2 * d) for w, d in B_PATTERNS}
    cat, lg1, lg4, lg16 = _dilated_fwd(cat[0], bqs, b_hw)
    lg1 = lg1[0]
    W["w_out"] = comm.weight("w_out", cat)
    x2 = _out_proj(x1, cat, W["w_out"])
    for k in ("wg2", "wu2", "wd2"):
        W[k] = comm.weight(k, x2)
    x3, h3, gate2, up2, act2 = _ffn_fwd(x2, g2, W["wg2"], W["wu2"], W["wd2"], "ffn2_fwd")

    dx3, dgf, loss8 = _final_loss(x3, gf, target)
    dx2, dff2, dgate2, dup2, dg2 = _ffn_dx(dx3, x2, g2, gate2, up2, W["wg2"], W["wu2"], W["wd2"], "ffn2_dx")
    fb = gate2.shape[1] // 2
    dwg2 = _tn(dgate2, h3, fb, "ffn2_dw_gate")
    dwu2 = _tn(dup2, h3, fb, "ffn2_dw_up")
    dwd2 = _tn(act2, dff2, fb, "ffn2_dw_down")
    comm.ready(dict(wg2=dwg2, wu2=dwu2, wd2=dwd2), dwd2[0])

    doa, dla, dob1, dlb1, dob4, dlb4, dob16, dlb16 = _dcat(dx2, W["w_out"], cat)
    dw_out = _tn(cat, dx2, cat.shape[1], "w_out_dw", dep=comm.dep())
    dqa, dka, dva, dsk = _attn_bwd(aq[None], ak[None], av[None], doa[None], a_lse, dla[None], comm.order(a_sink), A_HALF_WINDOW, True,
                                   "attn_a_bwd")
    bwd_in = {1: (dob1[None], lg1[None], dlb1[None]), 4: (dob4, lg4, dlb4), 16: (dob16, lg16, dlb16)}
    bg = {}
    for w, d in B_PATTERNS:
        q_, k_, v_ = bqs[d]
        do_, l_, dl_ = bwd_in[d]
        bg[d] = _attn_bwd(q_, k_, v_, do_, l_, dl_, no_sink, w // (2 * d), False, f"attn_b{d}_bwd")[:3]
    dproj = _rope_bwd_assemble(dqa[0], dka[0], dva[0], [t[0] for t in bg[1]], bg[4], bg[16], cos, sin)
    dw_in = _tn(dproj, h2, dproj.shape[1] // 2, "w_in_dw")
    comm.ready(dict(w_in=dw_in, w_out=dw_out), dw_in[0])
    dx1, dgm = _dh_norm(dproj, W["w_in"], x1, comm.order(gm), dx2)

    dx0, dff1, dgate1, dup1, dg1 = _ffn_dx(dx1, x, g1, gate1, up1, W["wg1"], W["wu1"], W["wd1"], "ffn1_dx")
    dwd1 = _tn(act1, dff1, fb, "ffn1_dw_down")
    comm.ready(dict(wd1=dwd1), dwd1[0])
    dwg1 = _tn(dgate1, h1, fb, "ffn1_dw_gate", dep=comm.dep())
    comm.ready(dict(wg1=dwg1), dwg1[0])
    dwu1 = _tn(dup1, h1, fb, "ffn1_dw_up", dep=comm.dep())
    comm.ready(dict(wu1=dwu1), dwu1[0])

    dsink = dsk[0, :, :, ::HEAD_DIM].sum(axis=1).reshape(-1)
    small = dict(g1=dg1.sum(axis=0), gm=dgm.sum(axis=0), g2=dg2.sum(axis=0), gf=dgf.sum(axis=0), sink=dsink, loss=loss8[0, 0])
    return dx0, small


BIG = ("wg1", "wu1", "wd1", "w_in", "w_out", "wg2", "wu2", "wd2")
GATHER_GROUPS = (("w_in",), ("w_out",), ("wg2", "wu2", "wd2"))


class _Comm:
    def __init__(self, shards, meanwhile):
        x, y, c = _mesh_pos()
        self.me = (2 * x + y).astype(jnp.int32).reshape(1)
        self.shards = shards
        self.tokens = []
        self.waiting = {}
        self.groups = []
        first = ("wg1", "wu1", "wd1")
        fulls = {k: _cast_place(self.me, shards[k], f"cast_{k}") for k in first}
        plan = _neighbour_plan([fulls[k].shape for k in first])
        send, recv, bufs, tok = _push_start("gather_first_start", [fulls[k] for k in first], 2 * len(first), plan, self.me)
        self.side = meanwhile(tok)
        fulls.update({k: _cast_place(self.me, shards[k], f"cast_{k}") for k in BIG if k not in first})
        bufs = _push_wait("gather_first_wait", send, recv, bufs, plan, [fulls["wd2"], *self.side])
        self.full = dict(zip(first, _gather_forward(bufs)))
        dep = self.full["wd1"]
        for gi, names in enumerate(GATHER_GROUPS):
            plan = _gather_plan(len(names))
            send, recv, bufs, tok = _push_start(f"gather_start_{gi}", [fulls[k] for k in names], 3 * len(names), plan, dep)
            self.tokens.append(tok)
            dep = tok
            for k in names:
                self.waiting[k] = (gi, names, send, recv, bufs, plan)

    def order(self, a):
        for tok in self.tokens:
            a = a + tok[0, 0]
        self.tokens = []
        return a

    def dep(self):
        return self.tokens[-1] if self.tokens else None

    def weight(self, name, after):
        if name in self.waiting:
            gi, names, send, recv, bufs, plan = self.waiting[name]
            for k, buf in zip(names, _push_wait(f"gather_wait_{gi}", send, recv, bufs, plan, after)):
                self.full[k] = buf
                del self.waiting[k]
        full = self.full[name]
        return full.reshape(N_CHIPS * full.shape[1], full.shape[2])

    def ready(self, grads, after):
        names = list(grads)
        f32s, b16s = [], []
        for k in names:
            gf, gb = grads[k]
            f32s.append(gf.reshape((N_CHIPS,) + self.shards[k].shape))
            b16s.append(gb.reshape((N_CHIPS,) + self.shards[k].shape))
        n = len(names)
        lands = [lax.empty((N_CHIPS - 1,) + self.shards[k].shape, BF16) for k in names]
        plan = _grad_push_plan(n)
        gi = len(self.groups)
        send, recv, bufs, tok = _push_start(f"grad_start_{gi}", b16s + lands, 3 * n, plan, after)
        self.tokens.append(tok)
        self.groups.append((names, f32s, send, recv, bufs, plan))

    def finish(self):
        out, swaps = {}, []
        after = self.tokens[-1]
        for gi, (names, f32s, send, recv, bufs, plan) in enumerate(self.groups):
            n = len(names)
            bufs = _push_wait(f"grad_wait_{gi}", send, recv, bufs, plan, after)
            mine = [_sum_own(self.me, f32s[i], bufs[n + i], f"sum_{k}") for i, k in enumerate(names)]
            lands = [lax.empty(p.shape, F32) for p in mine]
            send2, recv2, both, after = _push_start(f"swap_start_{gi}", mine + lands, n, _swap_plan(n), after)
            swaps.append((names, send2, recv2, both))
        for gi, (names, send2, recv2, both) in enumerate(swaps):
            n = len(names)
            both = _push_wait(f"swap_wait_{gi}", send2, recv2, both, _swap_plan(n), after)
            for i, k in enumerate(names):
                out[k] = (both[i], both[n + i])
        return out


def kernel(x, positions, norm_ffn1, w_gate1, w_up1, w_down1, norm_mix, w_in, a_sink, w_out, norm_ffn2, w_gate2, w_up2, w_down2, norm_final, loss_target, m_norm_ffn1, m_w_gate1, m_w_up1, m_w_down1, m_norm_mix, m_w_in, m_a_sink, m_w_out, m_norm_ffn2, m_w_gate2, m_w_up2, m_w_down2, m_norm_final, v_norm_ffn1, v_w_gate1, v_w_up1, v_w_down1, v_norm_mix, v_w_in, v_a_sink, v_w_out, v_norm_ffn2, v_w_gate2, v_w_up2, v_w_down2, v_norm_final):
    T, D = x.shape[1], x.shape[2]
    flip = ("wg1", "wu1", "w_in", "wg2", "wu2")

    def rows(k, a):
        return a[0].T if k in flip else a[0]

    given = dict(wg1=(w_gate1, m_w_gate1, v_w_gate1), wu1=(w_up1, m_w_up1, v_w_up1), wd1=(w_down1, m_w_down1, v_w_down1),
                 w_in=(w_in, m_w_in, v_w_in), w_out=(w_out, m_w_out, v_w_out), wg2=(w_gate2, m_w_gate2, v_w_gate2),
                 wu2=(w_up2, m_w_up2, v_w_up2), wd2=(w_down2, m_w_down2, v_w_down2))
    shards = {k: rows(k, given[k][0]) for k in BIG}

    comm = _Comm(shards, lambda tok: _rope(positions[0], tok))

    norms = (norm_ffn1, norm_mix, norm_ffn2, norm_final.reshape(1, D))
    grad_x, small = _local_step(x[0], comm.side, loss_target[0], norms, a_sink[0], comm)

    partial = comm.finish()

    def pad_row(a):
        a = a.reshape(-1)
        return jnp.pad(a, (0, D - a.shape[0]))

    row4 = pad_row(jnp.concatenate([small["sink"], small["loss"].reshape(1)]))
    vec = jnp.stack([small["g1"], small["gm"], small["g2"], small["gf"], row4] + [jnp.zeros((D,), F32)] * 3, axis=0)
    red = _allreduce_small(vec)
    loss = red[4, 8]
    g_small = jnp.stack([red[0], red[1], red[2], red[3], pad_row(red[4, 0:8])] + [jnp.zeros((D,), F32)] * 3, axis=0)

    def small_stack(a1, am, a2, af, ask):
        return jnp.stack([pad_row(a1), pad_row(am), pad_row(a2), pad_row(af), pad_row(ask)] + [jnp.zeros((D,), F32)] * 3, axis=0)

    w_small = small_stack(norm_ffn1, norm_mix, norm_ffn2, norm_final, a_sink)
    m_small = small_stack(m_norm_ffn1, m_norm_mix, m_norm_ffn2, m_norm_final, m_a_sink)
    v_small = small_stack(v_norm_ffn1, v_norm_mix, v_norm_ffn2, v_norm_final, v_a_sink)
    live = small_stack(jnp.ones_like(norm_ffn1), jnp.ones_like(norm_mix), jnp.ones_like(norm_ffn2), jnp.ones_like(norm_final), jnp.ones_like(a_sink))
    v_small = jnp.where(live > 0, v_small, 1.0)

    upd = {}
    for k in BIG:
        outs = _adamw(shards[k], partial[k][0], partial[k][1], rows(k, given[k][1]), rows(k, given[k][2]), f"adamw_{k}")
        upd[k] = tuple((a.T if k in flip else a)[None] for a in outs)
    _, ds_, nms_, nvs_ = _adamw(w_small, g_small, jnp.zeros_like(g_small), m_small, v_small, "adamw_small")

    def small_out(arr):
        return [arr[0].reshape(1, D), arr[1].reshape(1, D), arr[2].reshape(1, D), arr[3], arr[4, 0:8].reshape(1, 8)]

    gs_, dss, nmss, nvss = small_out(g_small), small_out(ds_), small_out(nms_), small_out(nvs_)

    def ordered(i):
        sm = (gs_, dss, nmss, nvss)[i]
        return [sm[0], upd["wg1"][i], upd["wu1"][i], upd["wd1"][i], sm[1], upd["w_in"][i], sm[4], upd["w_out"][i], sm[2],
                upd["wg2"][i], upd["wu2"][i], upd["wd2"][i], sm[3]]

    return (loss, grad_x[None], *ordered(0), *ordered(1), *ordered(2), *ordered(3))
```

```python
import jax
import jax.numpy as jnp
from jax import lax
from jax.experimental import pallas as pl
from jax.experimental.pallas import tpu as pltpu

F32 = jnp.float32
BF16 = jnp.bfloat16

HEAD_DIM = 64
LANES = 128
SUBLANES = 8
A_Q_W, A_KV_W, B_W = 512, 128, 512
A_HALF_WINDOW = 128
B_PATTERNS = ((128, 1), (512, 4), (2048, 16))
ROPE_THETA = 10000.0
NORM_EPS = 1e-6
FFN_RES_WEIGHT = 0.5
ADAM_LR, ADAM_B1, ADAM_B2, ADAM_EPS, ADAM_WD, ADAM_STEP = 0.001, 0.9, 0.999, 1e-08, 0.01, 10
N_CHIPS = 4
N_DEV = 8
QB = 128
SHORT_SEQ = 512
NEG = -1e30
VMEM_LIMIT = 56 * 1024 * 1024
MESH = pl.DeviceIdType.MESH
ANY = pl.BlockSpec(memory_space=pl.ANY)


def _params(sem=None):
    return pltpu.CompilerParams(dimension_semantics=sem, vmem_limit_bytes=VMEM_LIMIT)


def _sds(shape, dtype):
    return jax.ShapeDtypeStruct(tuple(shape), dtype)


def _dot(a, b):
    return jnp.dot(a, b, preferred_element_type=F32)


def _dot_nt(a, b):
    return lax.dot_general(a, b, (((1,), (1,)), ((), ())), preferred_element_type=F32)


def _dot_tn(a, b):
    return lax.dot_general(a, b, (((0,), (0,)), ((), ())), preferred_element_type=F32)


def _rms_stats(x):
    r = lax.rsqrt(jnp.mean(x * x, axis=-1, keepdims=True) + NORM_EPS)
    return x * r, r


def _rms_bwd(dh, x, g):
    xhat, r = _rms_stats(x)
    dxn = dh * g
    dx = r * (dxn - xhat * jnp.mean(dxn * xhat, axis=-1, keepdims=True))
    tm, d = x.shape
    dg = (dh * xhat).reshape(tm // SUBLANES, SUBLANES, d).sum(axis=0)
    return dx, dg


def _sigmoid(x):
    return 1.0 / (1.0 + jnp.exp(-x))


def _swap32(t):
    n = t.shape[-1]
    lane = lax.broadcasted_iota(jnp.int32, t.shape, t.ndim - 1)
    return jnp.where((lane % HEAD_DIM) < HEAD_DIM // 2, pltpu.roll(t, n - HEAD_DIM // 2, axis=t.ndim - 1),
                     pltpu.roll(t, HEAD_DIM // 2, axis=t.ndim - 1))


def _cast_place(me_arr, w, name):
    R, C = w.shape
    tr = R // 2 if (R // 2) % 16 == 0 else R

    def body(me_ref, w_ref, o_ref):
        o_ref[...] = w_ref[...].astype(BF16)

    grid_spec = pltpu.PrefetchScalarGridSpec(
        num_scalar_prefetch=1, grid=(R // tr,), in_specs=[pl.BlockSpec((tr, C), lambda t, me: (t, 0))],
        out_specs=pl.BlockSpec((None, tr, C), lambda t, me: (me[0], t, 0)))
    return pl.pallas_call(body, name=name, grid_spec=grid_spec, out_shape=_sds((N_CHIPS, R, C), BF16),
                          compiler_params=_params(("parallel",)))(me_arr, w)


HBM = pl.BlockSpec(memory_space=pltpu.HBM)
SEM = pl.BlockSpec(memory_space=pltpu.SEMAPHORE)


def _push_start(name, bufs, ncopies, plan, after):
    nb = len(bufs)

    def body(*refs):
        send, recv, token = refs[nb + 1], refs[nb + 2], refs[-1]
        for i, (src, dst, dev) in enumerate(plan(refs[:nb])):
            pltpu.make_async_remote_copy(src_ref=src, dst_ref=dst, send_sem=send.at[i], recv_sem=recv.at[i],
                                         device_id=dev, device_id_type=MESH).start()
        token[...] = jnp.zeros_like(token)

    outs = pl.pallas_call(
        body, name=name,
        out_shape=(pltpu.SemaphoreType.DMA((ncopies,)), pltpu.SemaphoreType.DMA((ncopies,)), *[pltpu.HBM(b.shape, b.dtype) for b in bufs],
                   _sds((SUBLANES, LANES), F32)),
        in_specs=[HBM] * nb + [ANY], out_specs=(SEM, SEM, *([HBM] * nb), pl.BlockSpec(memory_space=pltpu.VMEM)),
        input_output_aliases={i: 2 + i for i in range(nb)},
        compiler_params=pltpu.CompilerParams(has_side_effects=pltpu.SideEffectType.DATAFLOW_SIDE_EFFECTING),
    )(*[pltpu.with_memory_space_constraint(b, pltpu.HBM) for b in bufs], after)
    return outs[0], outs[1], list(outs[2:2 + nb]), outs[-1]


def _push_wait(name, send, recv, bufs, plan, after):
    nb = len(bufs)

    def body(*refs):
        send_ref, recv_ref = refs[nb], refs[nb + 1]
        for i, (src, dst, dev) in enumerate(plan(refs[:nb])):
            cp = pltpu.make_async_remote_copy(src_ref=src, dst_ref=dst, send_sem=send_ref.at[i], recv_sem=recv_ref.at[i],
                                              device_id=dev, device_id_type=MESH)
            cp.wait_send()
            cp.wait_recv()

    afters = list(after) if isinstance(after, (list, tuple)) else [after]
    outs = pl.pallas_call(
        body, name=name, out_shape=tuple(pltpu.HBM(b.shape, b.dtype) for b in bufs),
        in_specs=[HBM] * nb + [SEM, SEM] + [ANY] * len(afters), out_specs=tuple([HBM] * nb),
        input_output_aliases={i: i for i in range(nb)},
        compiler_params=pltpu.CompilerParams(has_side_effects=pltpu.SideEffectType.DATAFLOW_SIDE_EFFECTING),
    )(*bufs, send, recv, *afters)
    return list(outs)


def _mesh_pos():
    return lax.axis_index("x"), lax.axis_index("y"), lax.axis_index("c")


def _chip_peers(x, y, c):
    return [((1 - x, y, c), 2 * (1 - x) + y), ((x, 1 - y, c), 2 * x + (1 - y)), ((1 - x, 1 - y, c), 2 * (1 - x) + (1 - y))]


def _gather_plan(n):
    def plan(refs):
        x, y, c = _mesh_pos()
        me = 2 * x + y
        return [(refs[k].at[me], refs[k].at[me], dev) for k in range(n) for dev, _ in _chip_peers(x, y, c)]
    return plan


def _rows_of(shape, who, quarter=None):
    r2 = shape[1] // 2
    if quarter is None:
        return pl.ds(pl.multiple_of(who * r2, 16), r2)
    return pl.ds(pl.multiple_of(who * r2 + quarter * (r2 // 2), 16), r2 // 2)


def _neighbour_plan(shapes):
    def plan(refs):
        x, y, c = _mesh_pos()
        me = 2 * x + y
        return [(refs[k].at[me, _rows_of(shp, c), :], refs[k].at[me, _rows_of(shp, c), :], dev)
                for k, shp in enumerate(shapes) for dev in ((1 - x, y, c), (x, 1 - y, c))]
    return plan


def _gather_forward(fulls):
    n = len(fulls)

    def body(*refs):
        ins, outs = refs[:n], refs[n:2 * n]
        ici_send, ici_recv, d2d_send, d2d_recv = refs[2 * n:]
        x, y, c = _mesh_pos()
        cx, cy, cd = 2 * (1 - x) + y, 2 * x + (1 - y), 2 * (1 - x) + (1 - y)
        sibling, x_nbr, y_nbr = (x, y, 1 - c), (1 - x, y, c), (x, 1 - y, c)
        started = []

        def push(src, dst, send, recv, dev):
            cp = pltpu.make_async_remote_copy(src_ref=src, dst_ref=dst, send_sem=send, recv_sem=recv, device_id=dev, device_id_type=MESH)
            cp.start()
            started.append(cp)

        def arrived(blk, send, recv):
            pltpu.make_async_remote_copy(src_ref=blk, dst_ref=blk, send_sem=send, recv_sem=recv, device_id=sibling,
                                         device_id_type=MESH).wait_recv()

        for k in range(n):
            shp = fulls[k].shape
            for j, chip in enumerate((cx, cy)):
                push(ins[k].at[chip, _rows_of(shp, c), :], outs[k].at[chip, _rows_of(shp, c), :],
                     d2d_send.at[3 * k + j], d2d_recv.at[3 * k + j], sibling)
            push(ins[k].at[cx, _rows_of(shp, c, 0), :], outs[k].at[cx, _rows_of(shp, c, 0), :], ici_send.at[2 * k], ici_recv.at[2 * k], y_nbr)
            push(ins[k].at[cy, _rows_of(shp, c, 1), :], outs[k].at[cy, _rows_of(shp, c, 1), :], ici_send.at[2 * k + 1], ici_recv.at[2 * k + 1],
                 x_nbr)
        for k in range(n):
            shp = fulls[k].shape
            for q in (0, 1):
                arrived(outs[k].at[cd, _rows_of(shp, c, q), :], ici_send.at[2 * k + q], ici_recv.at[2 * k + q])
            blk = outs[k].at[cd, _rows_of(shp, c), :]
            push(blk, blk, d2d_send.at[3 * k + 2], d2d_recv.at[3 * k + 2], sibling)
        for k in range(n):
            for j, chip in enumerate((cx, cy, cd)):
                arrived(outs[k].at[chip, _rows_of(fulls[k].shape, 1 - c), :], d2d_send.at[3 * k + j], d2d_recv.at[3 * k + j])
        for cp in started:
            cp.wait_send()

    return pl.pallas_call(
        body, name="gather_forward", out_shape=[_sds(f.shape, BF16) for f in fulls],
        in_specs=[ANY] * n, out_specs=[ANY] * n, input_output_aliases={k: k for k in range(n)},
        scratch_shapes=[pltpu.SemaphoreType.DMA((n * 2,))] * 2 + [pltpu.SemaphoreType.DMA((n * 3,))] * 2,
        compiler_params=_params())(*fulls)


def _resident(shape):
    return pl.BlockSpec(shape, lambda i: (0,) * len(shape), pipeline_mode=pl.Buffered(1))


FFN_FWD_CHUNK = 256
FFN_DX_CHUNK = 512


def _chunks(n, step):
    return [(c0, min(step, n - c0)) for c0 in range(0, n, step)]


def _two_phase(chunks, first, second):
    held = {}
    for ci, ch in enumerate(chunks):
        held[ci] = first(*ch)
        if ci >= 1:
            second(*chunks[ci - 1], held.pop(ci - 1))
    last = len(chunks) - 1
    second(*chunks[last], held.pop(last))


def _ffn_fwd(x, g, wgt, wut, wd, name, tm=512):
    T, D = x.shape
    F = wd.shape[0]

    def body(x_ref, g_ref, wg_ref, wu_ref, wd_ref, xo_ref, h_ref, gate_ref, up_ref, act_ref):
        xv = x_ref[...]
        xhat, _ = _rms_stats(xv)
        h = (xhat * g_ref[...]).astype(BF16)
        h_ref[...] = h
        acc = []

        def first(c0, cw):
            return _dot_nt(h, wg_ref[c0:c0 + cw, :]), _dot_nt(h, wu_ref[c0:c0 + cw, :])

        def second(c0, cw, gate_up):
            gate, up = gate_up
            act = ((gate * _sigmoid(gate)) * up).astype(BF16)
            gate_ref[:, c0:c0 + cw] = gate.astype(BF16)
            up_ref[:, c0:c0 + cw] = up.astype(BF16)
            act_ref[:, c0:c0 + cw] = act
            d = _dot(act, wd_ref[c0:c0 + cw, :])
            acc[:] = [d if not acc else acc[0] + d]

        _two_phase(_chunks(F, FFN_FWD_CHUNK), first, second)
        xo_ref[...] = xv + FFN_RES_WEIGHT * acc[0]

    row = pl.BlockSpec((tm, D), lambda i: (i, 0))
    saved = pl.BlockSpec((tm, F), lambda i: (i, 0))
    return pl.pallas_call(
        body, name=name, grid=(T // tm,),
        in_specs=[row, pl.BlockSpec((1, D), lambda i: (0, 0)), _resident(wgt.shape), _resident(wut.shape), _resident(wd.shape)],
        out_specs=[row, row, saved, saved, saved],
        out_shape=[_sds((T, D), F32), _sds((T, D), BF16), _sds((T, F), BF16), _sds((T, F), BF16), _sds((T, F), BF16)],
        compiler_params=_params(("parallel",)))(x, g, wgt, wut, wd)


def _ffn_dx(dxo, x, g, gate_s, up_s, wgt, wut, wd, name, tm=256):
    T, D = x.shape
    F = wd.shape[0]

    def body(dxo_ref, x_ref, g_ref, gate_ref, up_ref, wg_ref, wu_ref, wd_ref, dx_ref, dff_ref, dgate_ref, dup_ref, dg_ref):
        @pl.when(pl.program_id(0) == 0)
        def _():
            dg_ref[...] = jnp.zeros_like(dg_ref)

        d = (FFN_RES_WEIGHT * dxo_ref[...]).astype(BF16)
        dff_ref[...] = d
        dh = []

        def first(c0, cw):
            return _dot_nt(d, wd_ref[c0:c0 + cw, :])

        def second(c0, cw, da):
            gate = gate_ref[:, c0:c0 + cw].astype(F32)
            up = up_ref[:, c0:c0 + cw].astype(F32)
            s = _sigmoid(gate)
            silu = gate * s
            dup = (da * silu).astype(BF16)
            dgate = (da * up * (s * (1.0 + gate * (1.0 - s)))).astype(BF16)
            dgate_ref[:, c0:c0 + cw] = dgate
            dup_ref[:, c0:c0 + cw] = dup
            t = _dot(dgate, wg_ref[c0:c0 + cw, :]) + _dot(dup, wu_ref[c0:c0 + cw, :])
            dh[:] = [t if not dh else dh[0] + t]

        _two_phase(_chunks(F, FFN_DX_CHUNK), first, second)
        dxn, dg = _rms_bwd(dh[0], x_ref[...], g_ref[...])
        dg_ref[...] += dg
        dx_ref[...] = dxo_ref[...] + dxn

    row = pl.BlockSpec((tm, D), lambda i: (i, 0))
    saved = pl.BlockSpec((tm, F), lambda i: (i, 0))
    return pl.pallas_call(
        body, name=name, grid=(T // tm,),
        in_specs=[row, row, pl.BlockSpec((1, D), lambda i: (0, 0)), saved, saved, _resident(wgt.shape), _resident(wut.shape),
                  _resident(wd.shape)],
        out_specs=[row, row, saved, saved, pl.BlockSpec((SUBLANES, D), lambda i: (0, 0))],
        out_shape=[_sds((T, D), F32), _sds((T, D), BF16), _sds((T, F), BF16), _sds((T, F), BF16), _sds((SUBLANES, D), F32)],
        compiler_params=_params(("arbitrary",)))(dxo, x, g, gate_s, up_s, wgt, wut, wd)


def _tn(a, b, mb, name, tk=2048, dep=None):
    T, M = a.shape
    N = b.shape[1]
    nt = T // tk

    def body(a_ref, b_ref, *refs):
        o_ref, ob_ref = refs[-2:]

        @pl.when(pl.program_id(1) == 0)
        def _():
            o_ref[...] = jnp.zeros_like(o_ref)

        o_ref[...] += _dot_tn(a_ref[...].astype(BF16), b_ref[...].astype(BF16))

        @pl.when(pl.program_id(1) == nt - 1)
        def _():
            ob_ref[...] = o_ref[...].astype(BF16)

    o_spec = pl.BlockSpec((mb, N), lambda g, t: (g, 0))
    return pl.pallas_call(
        body, name=name, grid=(M // mb, nt),
        in_specs=[pl.BlockSpec((tk, mb), lambda g, t: (t, g)), pl.BlockSpec((tk, N), lambda g, t: (t, 0))] + ([ANY] if dep is not None else []),
        out_specs=[o_spec, o_spec], out_shape=[_sds((M, N), F32), _sds((M, N), BF16)],
        compiler_params=_params(("parallel", "arbitrary")))(a, b, *([dep] if dep is not None else []))


def _rope_tables(pos_col, inv_freq):
    T = pos_col.shape[0]

    def body(p_ref, f_ref, c_ref, s_ref):
        ang = p_ref[...].astype(F32) * f_ref[...]
        lane = lax.broadcasted_iota(jnp.int32, ang.shape, 1)
        c_ref[...] = jnp.cos(ang)
        sn = jnp.sin(ang)
        s_ref[...] = jnp.where((lane % HEAD_DIM) < HEAD_DIM // 2, -sn, sn)

    tm = 1024
    return pl.pallas_call(
        body, name="rope_tables", grid=(T // tm,),
        in_specs=[pl.BlockSpec((tm, 1), lambda i: (i, 0)), pl.BlockSpec((1, LANES), lambda i: (0, 0))],
        out_specs=[pl.BlockSpec((tm, LANES), lambda i: (i, 0))] * 2,
        out_shape=[_sds((T, LANES), F32)] * 2, compiler_params=_params(("parallel",)))(pos_col, inv_freq)


def _deinterleave(scr, out_ref, d, tm, nblk):
    for r in range(d):
        for cb in range(nblk):
            out_ref[r, :, cb * LANES:(cb + 1) * LANES] = scr[cb, pl.ds(r, tm // d, stride=d), :].astype(out_ref.dtype)


def _interleave(in_ref, scr, d, tm, nblk):
    for r in range(d):
        for cb in range(nblk):
            scr[cb, pl.ds(r, tm // d, stride=d), :] = in_ref[r, :, cb * LANES:(cb + 1) * LANES].astype(F32)


def _proj_rope(x, g, w_in, cos, sin, tm=512):
    T, D = x.shape
    dils = [d for _, d in B_PATTERNS if d > 1]
    nbb = B_W // LANES
    scale = HEAD_DIM ** -0.5
    cuts = [0, A_Q_W, A_Q_W + A_KV_W, A_Q_W + 2 * A_KV_W, A_Q_W + 2 * A_KV_W + B_W, A_Q_W + 2 * A_KV_W + 2 * B_W,
            A_Q_W + 2 * A_KV_W + 3 * B_W]

    def body(x_ref, g_ref, w_ref, c_ref, s_ref, h_ref, aq_ref, ak_ref, av_ref, *rest):
        b_refs, scr = rest[:-1], rest[-1]
        xhat, _ = _rms_stats(x_ref[...])
        h = (xhat * g_ref[...]).astype(BF16)
        h_ref[...] = h
        cs, sn = c_ref[...], s_ref[...]

        def project(idx, ref, rope, mult, which):
            return _dot_nt(h, w_ref[cuts[idx]:cuts[idx + 1], :])

        def finish(idx, ref, rope, mult, which, whole):
            for cb in range((cuts[idx + 1] - cuts[idx]) // LANES):
                p = whole[:, cb * LANES:(cb + 1) * LANES]
                if rope:
                    p = p * cs + _swap32(p) * sn
                if mult != 1.0:
                    p = p * mult
                ref[:, cb * LANES:(cb + 1) * LANES] = p.astype(BF16)
                if which is not None:
                    scr[which, cb] = p
            if which is not None:
                for di, d in enumerate(dils):
                    _deinterleave(scr.at[which], b_refs[3 * (di + 1) + which], d, tm, nbb)

        _two_phase([(0, aq_ref, True, scale, None), (1, ak_ref, True, 1.0, None), (2, av_ref, False, 1.0, None),
                    (3, b_refs[0], True, scale, 0), (4, b_refs[1], True, 1.0, 1), (5, b_refs[2], False, 1.0, 2)], project, finish)

    row = lambda w: pl.BlockSpec((tm, w), lambda i: (i, 0))
    out_specs = [row(D), row(A_Q_W), row(A_KV_W), row(A_KV_W)] + [row(B_W)] * 3
    out_shape = [_sds((T, D), BF16), _sds((T, A_Q_W), BF16), _sds((T, A_KV_W), BF16), _sds((T, A_KV_W), BF16)] + [_sds((T, B_W), BF16)] * 3
    for d in dils:
        out_specs += [pl.BlockSpec((d, tm // d, B_W), lambda i: (0, i, 0))] * 3
        out_shape += [_sds((d, T // d, B_W), BF16)] * 3
    return pl.pallas_call(
        body, name="proj_rope", grid=(T // tm,),
        in_specs=[row(D), pl.BlockSpec((1, D), lambda i: (0, 0)), pl.BlockSpec(w_in.shape, lambda i: (0, 0)), row(LANES), row(LANES)],
        out_specs=out_specs, out_shape=out_shape, scratch_shapes=[pltpu.VMEM((3, nbb, tm, LANES), F32)],
        compiler_params=_params(("parallel",)))(x, g, w_in, cos, sin)


def _band_bias(rel, qb, kw, hw):
    ri = lax.broadcasted_iota(jnp.int32, (2 * qb, kw), 0) & (qb - 1)
    ci = lax.broadcasted_iota(jnp.int32, (2 * qb, kw), 1)
    return jnp.where(jnp.abs(ri + rel - ci) <= hw, 0.0, NEG).astype(F32)


def _stack_heads(x, lo):
    z = jnp.zeros_like(x)
    return jnp.concatenate([jnp.where(lo, x, z), jnp.where(lo, z, x)], axis=0)


def _unstack_heads(y, lo):
    qb = y.shape[0] // 2
    return jnp.where(lo, y[:qb], y[qb:])


def _band_setup(bias_scr, qb, kw, hw):
    if bias_scr is not None:
        for i in range(3):
            bias_scr[i] = _band_bias(i * hw, qb, kw, hw)


def _band_window(bias_scr, qs, L, qb, kw, hw):
    ws = pl.multiple_of(jnp.clip(qs - hw, 0, L - kw), 64)
    if bias_scr is None:
        return ws, _band_bias(qs - ws, qb, kw, hw)
    return ws, bias_scr[lax.shift_right_logical(qs - ws, hw.bit_length() - 1)]


def _dup_kv_head(src_ref, dst_ref, head, L):
    step = min(L, 1024)
    for r0 in range(0, L, step):
        xf = src_ref[r0:r0 + step, :].astype(F32)
        lane = lax.broadcasted_iota(jnp.int32, xf.shape, 1)
        keep = jnp.logical_xor(lane < HEAD_DIM, head == 1)
        dst_ref[r0:r0 + step, :] = jnp.where(keep, xf, pltpu.roll(xf, HEAD_DIM, axis=1)).astype(dst_ref.dtype)


def _attn_fwd(q, k, v, sink, hw, gqa, out_dtype, name, qb=QB, blocks_per_step=8, out_cols=None):
    NB, L, Cq = q.shape
    Ls = min(L, 2048)
    kw = min(qb + 2 * hw, L)
    tables = L >= qb + 2 * hw
    unroll = min(blocks_per_step, Ls // qb)
    nlb = 1 if (gqa or L > SHORT_SEQ) else Cq // LANES

    def body(sink_ref, q_ref, k_ref, v_ref, o_ref, lse_ref, *scr):
        b, s_idx = pl.program_id(1), pl.program_id(2)
        bias_scr = scr[0] if tables else None
        _band_setup(bias_scr, qb, kw, hw)
        if gqa:
            kd, vd = scr[-2:]

            @pl.when(s_idx == 0)
            def _():
                _dup_kv_head(k_ref, kd, b // 2, L)
                _dup_kv_head(v_ref, vd, b // 2, L)
        else:
            kd, vd = k_ref, v_ref
        lane = lax.broadcasted_iota(jnp.int32, (qb, LANES), 1)
        lo = lane < HEAD_DIM
        if gqa:
            row = lax.broadcasted_iota(jnp.int32, (2 * qb, 1), 0)
            sk = jnp.where(row < qb, sink_ref[2 * b], sink_ref[2 * b + 1])

        def block(ql, col):
            qs = s_idx * Ls + ql
            ws, bias = _band_window(bias_scr, qs, L, qb, kw, hw)
            return ws, _dot_nt(_stack_heads(q_ref[pl.ds(ql, qb), col], lo), kd[pl.ds(ws, kw), col]) + bias

        def finish(ql, col, scores):
            ws, s = scores
            m = jnp.max(s, axis=-1, keepdims=True)
            if gqa:
                m = jnp.maximum(m, sk)
            p = jnp.exp(s - m)
            den = jnp.sum(p, axis=-1, keepdims=True)
            if gqa:
                den = den + jnp.exp(sk - m)
            o = _dot(p.astype(BF16), vd[pl.ds(ws, kw), col]) * (1.0 / den)
            o_ref[pl.ds(ql, qb), col] = _unstack_heads(o, lo).astype(o_ref.dtype)
            lse_ref[pl.ds(ql, qb), col] = _unstack_heads(m + jnp.log(den), lo)

        for lb in range(nlb):
            def step(n, carry, col=slice(lb * LANES, (lb + 1) * LANES)):
                _two_phase([(pl.multiple_of((n * unroll + u) * qb, qb), col) for u in range(unroll)], block, finish)
                return carry

            lax.fori_loop(0, Ls // (qb * unroll), step, 0)

    kv_map = (lambda r, b, s: (r, 0, 0)) if gqa else (lambda r, b, s: (r, 0, b))
    seg = pl.BlockSpec((None, Ls, nlb * LANES), lambda r, b, s: (r, s, b))
    return pl.pallas_call(
        body, name=name, grid=(NB, Cq // (nlb * LANES), L // Ls),
        in_specs=[pl.BlockSpec(memory_space=pltpu.SMEM), seg, pl.BlockSpec((None, L, nlb * LANES), kv_map),
                  pl.BlockSpec((None, L, nlb * LANES), kv_map)],
        out_specs=[seg, seg], out_shape=[_sds((NB, L, out_cols or Cq), out_dtype), _sds((NB, L, Cq), F32)],
        scratch_shapes=([pltpu.VMEM((3, 2 * qb, kw), F32)] if tables else []) + ([pltpu.VMEM((L, LANES), BF16)] * 2 if gqa else []),
        compiler_params=_params(("parallel", "parallel", "arbitrary")))(sink, q, k, v)


def _attn_bwd(q, k, v, do, lse, delta, sink, hw, gqa, name, qb=QB, blocks_per_step=8):
    NB, L, Cq = q.shape
    Ck = k.shape[2]
    Ls = min(L, 2048)
    kw = min(qb + 2 * hw, L)
    reps = kw // LANES
    nseg = L // Ls
    scale = HEAD_DIM ** -0.5
    tables = L >= qb + 2 * hw
    unroll = min(blocks_per_step, Ls // qb)
    nlb = 1 if (gqa or L > SHORT_SEQ) else Cq // LANES

    def body(sink_ref, q_ref, do_ref, lse_ref, dl_ref, k_ref, v_ref, dq_ref, dk_ref, dv_ref, dsk_ref, *scr):
        b, s_idx = pl.program_id(1), pl.program_id(2)
        lane = lax.broadcasted_iota(jnp.int32, (qb, LANES), 1)
        lo = lane < HEAD_DIM
        bias_scr = scr[0] if tables else None
        _band_setup(bias_scr, qb, kw, hw)
        if gqa:
            kd, vd, dk_acc, dv_acc, dsk_acc = scr[-5:]

            @pl.when(s_idx == 0)
            def _():
                _dup_kv_head(k_ref, kd, b // 2, L)
                _dup_kv_head(v_ref, vd, b // 2, L)
                dk_acc[...] = jnp.zeros_like(dk_acc)
                dv_acc[...] = jnp.zeros_like(dv_acc)
                dsk_acc[...] = jnp.zeros_like(dsk_acc)

            @pl.when((s_idx == 0) & (b == 0))
            def _():
                dk_ref[...] = jnp.zeros_like(dk_ref)
                dv_ref[...] = jnp.zeros_like(dv_ref)
        else:
            kd, vd = k_ref, v_ref
            dk_acc, dv_acc = scr[-2:]

            @pl.when(s_idx == 0)
            def _():
                dk_acc[...] = jnp.zeros_like(dk_acc)
                dv_acc[...] = jnp.zeros_like(dv_acc)

        def block(ql, col):
            qs = s_idx * Ls + ql
            ws, bias = _band_window(bias_scr, qs, L, qb, kw, hw)
            qv, dov = q_ref[pl.ds(ql, qb), col], do_ref[pl.ds(ql, qb), col]
            lse, dl = lse_ref[pl.ds(ql, qb), col], dl_ref[pl.ds(ql, qb), col]
            kv_, vv = kd[pl.ds(ws, kw), col], vd[pl.ds(ws, kw), col]
            q2, do2 = _stack_heads(qv, lo), _stack_heads(dov, lo)
            return ws, q2, do2, lse, dl, _dot_nt(q2, kv_) + bias, _dot_nt(do2, vv)

        def finish(ql, col, held):
            ws, q2, do2, lse, dl, s, dp = held
            lse_sw, dl_sw = pltpu.roll(lse, HEAD_DIM, axis=1), pltpu.roll(dl, HEAD_DIM, axis=1)
            lse2 = jnp.concatenate([jnp.where(lo, lse, lse_sw), jnp.where(lo, lse_sw, lse)], axis=0)
            dl2 = jnp.concatenate([jnp.where(lo, dl, dl_sw), jnp.where(lo, dl_sw, dl)], axis=0)
            p = jnp.exp(s - jnp.tile(lse2, (1, reps)))
            ds = (p * (dp - jnp.tile(dl2, (1, reps)))).astype(BF16)
            dq_ref[pl.ds(ql, qb), col] = (_unstack_heads(_dot(ds, kd[pl.ds(ws, kw), col]), lo) * scale).astype(dq_ref.dtype)
            both = _dot_tn(jnp.concatenate([ds, p.astype(BF16)], axis=1), jnp.concatenate([q2, do2], axis=1))
            dk_acc[pl.ds(ws, kw), col] += both[:kw, :LANES]
            dv_acc[pl.ds(ws, kw), col] += both[kw:, LANES:]
            if gqa:
                sk = jnp.where(lo, sink_ref[2 * b], sink_ref[2 * b + 1])
                dsk_acc[...] += -jnp.exp(sk - lse) * dl

        for lb in range(nlb):
            def step(n, carry, col=slice(lb * LANES, (lb + 1) * LANES)):
                _two_phase([(pl.multiple_of((n * unroll + u) * qb, qb), col) for u in range(unroll)], block, finish)
                return carry

            lax.fori_loop(0, Ls // (qb * unroll), step, 0)

        if gqa:
            @pl.when(s_idx == nseg - 1)
            def _():
                step_rows = min(L, 1024)
                for r0 in range(0, L, step_rows):
                    lanek = lax.broadcasted_iota(jnp.int32, (step_rows, LANES), 1)
                    mine = jnp.logical_xor(lanek < HEAD_DIM, (b // 2) == 1)
                    for acc, ref in ((dk_acc, dk_ref), (dv_acc, dv_ref)):
                        a = acc[r0:r0 + step_rows, :]
                        ref[r0:r0 + step_rows, :] += jnp.where(mine, a + pltpu.roll(a, HEAD_DIM, axis=1), 0.0)
                dsk_ref[...] = dsk_acc[...].reshape(qb // SUBLANES, SUBLANES, LANES).sum(axis=0)
        else:
            dsk_ref[...] = jnp.zeros_like(dsk_ref)

            @pl.when(s_idx == nseg - 1)
            def _():
                dk_ref[...] = dk_acc[...].astype(dk_ref.dtype)
                dv_ref[...] = dv_acc[...].astype(dv_ref.dtype)

    kv_map = (lambda r, b, s: (r, 0, 0)) if gqa else (lambda r, b, s: (r, 0, b))
    seg = pl.BlockSpec((None, Ls, nlb * LANES), lambda r, b, s: (r, s, b))
    full = pl.BlockSpec((None, L, nlb * LANES), kv_map)
    scratch = [pltpu.VMEM((3, 2 * qb, kw), F32)] if tables else []
    if gqa:
        scratch += [pltpu.VMEM((L, LANES), BF16)] * 2 + [pltpu.VMEM((L, LANES), F32)] * 2 + [pltpu.VMEM((qb, LANES), F32)]
    else:
        scratch += [pltpu.VMEM((L, nlb * LANES), F32)] * 2
    kv_dtype = F32 if gqa else BF16
    return pl.pallas_call(
        body, name=name, grid=(NB, Cq // (nlb * LANES), nseg),
        in_specs=[pl.BlockSpec(memory_space=pltpu.SMEM), seg, seg, seg, seg, full, full],
        out_specs=[seg, full, full, pl.BlockSpec((None, None, SUBLANES, LANES), lambda r, b, s: (r, b, 0, 0))],
        out_shape=[_sds((NB, L, Cq), BF16), _sds((NB, L, Ck), kv_dtype), _sds((NB, L, Ck), kv_dtype),
                   _sds((NB, Cq // LANES, SUBLANES, LANES), F32)],
        scratch_shapes=scratch,
        compiler_params=_params(("arbitrary", "arbitrary", "arbitrary")))(sink, q, do, lse, delta, k, v)


def _dilated_fwd(cat, qkv, hw, tile=2048):
    T = cat.shape[0]
    dils = sorted(qkv)
    nbb, na = B_W // LANES, A_Q_W // LANES
    qb, kw = QB, QB + 2 * hw
    rows_merge = 256
    assert T % tile == 0 and all(tile % (d * qb) == 0 and T // d >= kw for d in dils)

    def body(cat_in, *refs):
        qkv_refs = {d: refs[3 * j:3 * j + 3] for j, d in enumerate(dils)}
        cat_ref, lg_refs = refs[3 * len(dils)], refs[3 * len(dils) + 1:4 * len(dils) + 1]
        o_scr, l_scr, bias_scr = refs[4 * len(dils) + 1:]
        i = pl.program_id(1)
        _band_setup(bias_scr, qb, kw, hw)
        lane = lax.broadcasted_iota(jnp.int32, (qb, LANES), 1)
        lo = lane < HEAD_DIM
        for pi, d in enumerate(dils):
            q_ref, k_ref, v_ref = qkv_refs[d]
            L, rows = T // d, tile // d

            def place(r, n, d=d):
                return pl.ds(r + d * n * qb, qb, stride=d) if d > 1 else pl.ds(n * qb, qb)

            def scores(r, n, q_ref=q_ref, k_ref=k_ref, L=L, rows=rows):
                ws, bias = _band_window(bias_scr, i * rows + n * qb, L, qb, kw, hw)
                return ws, _dot_nt(_stack_heads(q_ref[r, n * qb:(n + 1) * qb, :], lo), k_ref[r, pl.ds(ws, kw), :]) + bias

            def finish(r, n, held, v_ref=v_ref, pi=pi, place=place):
                ws, s = held
                m = jnp.max(s, axis=-1, keepdims=True)
                p = jnp.exp(s - m)
                den = jnp.sum(p, axis=-1, keepdims=True)
                o = _dot(p.astype(BF16), v_ref[r, pl.ds(ws, kw), :]) * (1.0 / den)
                o_scr[pi, place(r, n), :] = _unstack_heads(o, lo)
                l_scr[pi, place(r, n), :] = _unstack_heads(m + jnp.log(den), lo)

            blocks = [(r, n) for r in range(d) for n in range(rows // qb)]
            for g0 in range(0, len(blocks), 8):
                _two_phase(blocks[g0:g0 + 8], scores, finish)

        for r0 in range(0, tile, rows_merge):
            rs = slice(r0, r0 + rows_merge)
            ls_ = [l_scr[pi, rs, :] for pi in range(len(dils))]
            m = ls_[0]
            for l in ls_[1:]:
                m = jnp.maximum(m, l)
            es = [jnp.exp(l - m) for l in ls_]
            den, out = es[0], es[0] * o_scr[0, rs, :]
            for pi in range(1, len(dils)):
                den = den + es[pi]
                out = out + es[pi] * o_scr[pi, rs, :]
            cat_ref[rs, :] = (out * (1.0 / den)).astype(BF16)
            l_scr[0, rs, :] = m + jnp.log(den)
        for lg_ref, d in zip(lg_refs, dils):
            for r in range(d):
                lg_ref[r] = l_scr[0, pl.ds(r, tile // d, stride=d), :] if d > 1 else l_scr[0]

    in_specs = [pl.BlockSpec(memory_space=pl.ANY)]
    operands = [cat]
    for d in dils:
        in_specs += [pl.BlockSpec((d, tile // d, LANES), lambda b, i: (0, i, b))] + [pl.BlockSpec((d, T // d, LANES), lambda b, i: (0, 0, b))] * 2
        operands += list(qkv[d])
    return pl.pallas_call(
        body, name="dilated_fwd", grid=(nbb, T // tile), in_specs=in_specs,
        out_specs=[pl.BlockSpec((tile, LANES), lambda b, i: (i, na + b))] + [pl.BlockSpec((d, tile // d, LANES), lambda b, i: (0, i, b)) for d in dils],
        out_shape=[_sds(cat.shape, BF16)] + [_sds((d, T // d, B_W), F32) for d in dils],
        input_output_aliases={0: 0},
        scratch_shapes=[pltpu.VMEM((len(dils), tile, LANES), F32)] * 2 + [pltpu.VMEM((3, 2 * qb, kw), F32)],
        compiler_params=_params(("parallel", "arbitrary")))(*operands)


def _out_proj(x, cat, w_out, tm=512):
    T, D = x.shape

    def body(x_ref, c_ref, w_ref, o_ref):
        o_ref[...] = x_ref[...] + _dot(c_ref[...], w_ref[...])

    row = lambda w: pl.BlockSpec((tm, w), lambda i: (i, 0))
    return pl.pallas_call(
        body, name="out_proj", grid=(T // tm,), in_specs=[row(D), row(cat.shape[1]), pl.BlockSpec(w_out.shape, lambda i: (0, 0))],
        out_specs=row(D), out_shape=_sds((T, D), F32), compiler_params=_params(("parallel",)))(x, cat, w_out)


def _final_loss(x, g, target, tm=512):
    T, D = x.shape

    def body(x_ref, g_ref, t_ref, dx_ref, dg_ref, loss_ref):
        @pl.when(pl.program_id(0) == 0)
        def _():
            dg_ref[...] = jnp.zeros_like(dg_ref)
            loss_ref[...] = jnp.zeros_like(loss_ref)

        xv, gv = x_ref[...], g_ref[...]
        xhat, _ = _rms_stats(xv)
        err = xhat * gv - t_ref[...]
        loss_ref[...] += 0.5 * jnp.sum(jnp.sum(err * err, axis=-1, keepdims=True) * (1.0 / D), axis=0, keepdims=True)
        dx, dg = _rms_bwd(err * (1.0 / D), xv, gv)
        dx_ref[...] = dx
        dg_ref[...] += dg

    row = pl.BlockSpec((tm, D), lambda i: (i, 0))
    return pl.pallas_call(
        body, name="final_loss", grid=(T // tm,), in_specs=[row, pl.BlockSpec((1, D), lambda i: (0, 0)), row],
        out_specs=[row, pl.BlockSpec((SUBLANES, D), lambda i: (0, 0)), pl.BlockSpec((SUBLANES, LANES), lambda i: (0, 0))],
        out_shape=[_sds((T, D), F32), _sds((SUBLANES, D), F32), _sds((SUBLANES, LANES), F32)],
        compiler_params=_params(("arbitrary",)))(x, g, target)


def _dcat(dx, w_out, cat, tm=512):
    T, D = dx.shape
    C = cat.shape[1]
    nba, nbb = A_Q_W // LANES, B_W // LANES

    def body(dx_ref, w_ref, cat_ref, doa_ref, dla_ref, dob1_ref, dlb1_ref, dob4_ref, dlb4_ref, dob16_ref, dlb16_ref, sdo, sdl):
        dc = _dot_nt(dx_ref[...].astype(BF16), w_ref[...])
        ri = lax.broadcasted_iota(jnp.int32, (LANES, LANES), 0)
        ci = lax.broadcasted_iota(jnp.int32, (LANES, LANES), 1)
        same_head = ((ri // HEAD_DIM) == (ci // HEAD_DIM)).astype(BF16)
        for cb in range(C // LANES):
            cols = slice(cb * LANES, (cb + 1) * LANES)
            blk = dc[:, cols]
            prod = blk * cat_ref[:, cols].astype(F32)
            hi = prod.astype(BF16)
            lo_ = (prod - hi.astype(F32)).astype(BF16)
            dl = _dot(hi, same_head) + _dot(lo_, same_head)
            if cb < nba:
                doa_ref[:, cols] = blk.astype(BF16)
                dla_ref[:, cols] = dl
            else:
                bcols = slice((cb - nba) * LANES, (cb - nba + 1) * LANES)
                dob1_ref[:, bcols] = blk.astype(BF16)
                dlb1_ref[:, bcols] = dl
                sdo[cb - nba] = blk
                sdl[cb - nba] = dl
        _deinterleave(sdo, dob4_ref, 4, tm, nbb)
        _deinterleave(sdl, dlb4_ref, 4, tm, nbb)
        _deinterleave(sdo, dob16_ref, 16, tm, nbb)
        _deinterleave(sdl, dlb16_ref, 16, tm, nbb)

    row = lambda w: pl.BlockSpec((tm, w), lambda i: (i, 0))
    perm = lambda d: pl.BlockSpec((d, tm // d, B_W), lambda i: (0, i, 0))
    return pl.pallas_call(
        body, name="dcat", grid=(T // tm,), in_specs=[row(D), pl.BlockSpec(w_out.shape, lambda i: (0, 0)), row(C)],
        out_specs=[row(A_Q_W), row(A_Q_W), row(B_W), row(B_W), perm(4), perm(4), perm(16), perm(16)],
        out_shape=[_sds((T, A_Q_W), BF16), _sds((T, A_Q_W), F32), _sds((T, B_W), BF16), _sds((T, B_W), F32),
                   _sds((4, T // 4, B_W), BF16), _sds((4, T // 4, B_W), F32), _sds((16, T // 16, B_W), BF16), _sds((16, T // 16, B_W), F32)],
        scratch_shapes=[pltpu.VMEM((nbb, tm, LANES), F32)] * 2, compiler_params=_params(("parallel",)))(dx, w_out, cat)


def _rope_bwd_assemble(dqa, dka, dva, b1, b4, b16, cos, sin, tm=512):
    T = dqa.shape[0]
    nbb = B_W // LANES
    width = A_Q_W + 2 * A_KV_W + 3 * B_W

    def body(dqa_ref, dka_ref, dva_ref, q1, k1, v1, q4, k4, v4, q16, k16, v16, c_ref, s_ref, o_ref, scr):
        cs, sn = c_ref[...], s_ref[...]

        def unrope(t):
            return t * cs + _swap32(t * sn)

        col = 0
        for ref, rope in ((dqa_ref, True), (dka_ref, True), (dva_ref, False)):
            for cb in range(ref.shape[1] // LANES):
                t = ref[:, cb * LANES:(cb + 1) * LANES].astype(F32)
                o_ref[:, col:col + LANES] = (unrope(t) if rope else t).astype(BF16)
                col += LANES
        for which, (r1, r4, r16, rope) in enumerate(((q1, q4, q16, True), (k1, k4, k16, True), (v1, v4, v16, False))):
            _interleave(r4, scr.at[0], 4, tm, nbb)
            _interleave(r16, scr.at[1], 16, tm, nbb)
            for cb in range(nbb):
                t = r1[:, cb * LANES:(cb + 1) * LANES].astype(F32) + scr[0, cb] + scr[1, cb]
                o_ref[:, col:col + LANES] = (unrope(t) if rope else t).astype(BF16)
                col += LANES

    row = lambda w: pl.BlockSpec((tm, w), lambda i: (i, 0))
    perm = lambda d: pl.BlockSpec((d, tm // d, B_W), lambda i: (0, i, 0))
    return pl.pallas_call(
        body, name="rope_bwd", grid=(T // tm,),
        in_specs=[row(A_Q_W), row(A_KV_W), row(A_KV_W)] + [row(B_W)] * 3 + [perm(4)] * 3 + [perm(16)] * 3 + [row(LANES), row(LANES)],
        out_specs=row(width), out_shape=_sds((T, width), BF16), scratch_shapes=[pltpu.VMEM((2, nbb, tm, LANES), F32)],
        compiler_params=_params(("parallel",)))(dqa, dka, dva, *b1, *b4, *b16, cos, sin)


def _dh_norm(dproj, w_in, x, g, dres, tm=512):
    T, D = x.shape

    def body(dp_ref, w_ref, x_ref, g_ref, dr_ref, dx_ref, dg_ref):
        @pl.when(pl.program_id(0) == 0)
        def _():
            dg_ref[...] = jnp.zeros_like(dg_ref)

        dxn, dg = _rms_bwd(_dot(dp_ref[...], w_ref[...]), x_ref[...], g_ref[...])
        dg_ref[...] += dg
        dx_ref[...] = dr_ref[...] + dxn

    row = lambda w: pl.BlockSpec((tm, w), lambda i: (i, 0))
    return pl.pallas_call(
        body, name="dh_norm", grid=(T // tm,),
        in_specs=[row(dproj.shape[1]), pl.BlockSpec(w_in.shape, lambda i: (0, 0)), row(D), pl.BlockSpec((1, D), lambda i: (0, 0)), row(D)],
        out_specs=[row(D), pl.BlockSpec((SUBLANES, D), lambda i: (0, 0))],
        out_shape=[_sds((T, D), F32), _sds((SUBLANES, D), F32)], compiler_params=_params(("arbitrary",)))(dproj, w_in, x, g, dres)


def _grad_push_plan(n):
    def plan(refs):
        x, y, c = _mesh_pos()
        return [(refs[k].at[chip], refs[n + k].at[rel], dev) for k in range(n) for rel, (dev, chip) in enumerate(_chip_peers(x, y, c))]
    return plan


def _sum_own(me_arr, g, landed, name):
    ns, R, C = g.shape
    tr = R // 2 if (R // 2) % 16 == 0 else R

    def body(me_ref, g_ref, x_ref, o_ref):
        acc = g_ref[...]
        for rel in range(ns - 1):
            acc = acc + x_ref[rel].astype(F32)
        o_ref[...] = acc

    grid_spec = pltpu.PrefetchScalarGridSpec(
        num_scalar_prefetch=1, grid=(R // tr,),
        in_specs=[pl.BlockSpec((None, tr, C), lambda t, me: (me[0], t, 0)), pl.BlockSpec((ns - 1, tr, C), lambda t, me: (0, t, 0))],
        out_specs=pl.BlockSpec((tr, C), lambda t, me: (t, 0)))
    return pl.pallas_call(body, name=name, grid_spec=grid_spec, out_shape=_sds((R, C), F32),
                          compiler_params=_params(("parallel",)))(me_arr, g, landed)


def _swap_plan(n):
    def plan(refs):
        x, y, c = _mesh_pos()
        return [(refs[k], refs[n + k], (x, y, 1 - c)) for k in range(n)]
    return plan


def _allreduce_small(v):
    rows, W = v.shape

    def body(v_ref, o_ref, buf, send, recv):
        x, y, c = _mesh_pos()
        me = 4 * x + 2 * y + c
        cps = []
        for m in range(1, N_DEV):
            dev = (x ^ (m >> 2), y ^ ((m >> 1) & 1), c ^ (m & 1))
            cp = pltpu.make_async_remote_copy(src_ref=v_ref, dst_ref=buf.at[me], send_sem=send.at[m - 1], recv_sem=recv.at[m - 1],
                                              device_id=dev, device_id_type=MESH)
            cp.start()
            cps.append(cp)
        for m in range(1, N_DEV):
            pltpu.make_async_remote_copy(src_ref=v_ref, dst_ref=buf.at[me ^ m], send_sem=send.at[m - 1], recv_sem=recv.at[m - 1],
                                         device_id=(x, y, c), device_id_type=MESH).wait_recv()
        for cp in cps:
            cp.wait_send()
        buf[me] = v_ref[...]
        acc = buf[0]
        for i in range(1, N_DEV):
            acc = acc + buf[i]
        o_ref[...] = acc

    return pl.pallas_call(
        body, name="allreduce_small", out_shape=_sds((rows, W), F32),
        scratch_shapes=[pltpu.VMEM((N_DEV, rows, W), F32), pltpu.SemaphoreType.DMA((N_DEV - 1,)), pltpu.SemaphoreType.DMA((N_DEV - 1,))],
        compiler_params=_params())(v)


def _adamw(w, gp, gq, m, v, name):
    R, C = w.shape
    tr = R // 2 if (R // 2) % SUBLANES == 0 else R
    c1 = 1.0 / (1.0 - ADAM_B1 ** ADAM_STEP)
    c2 = 1.0 / (1.0 - ADAM_B2 ** ADAM_STEP)

    def body(w_ref, gp_ref, gq_ref, m_ref, v_ref, g_ref, d_ref, nm_ref, nv_ref):
        gv = gp_ref[...] + gq_ref[...]
        nm = ADAM_B1 * m_ref[...] + (1.0 - ADAM_B1) * gv
        nv = ADAM_B2 * v_ref[...] + (1.0 - ADAM_B2) * (gv * gv)
        g_ref[...] = gv
        d_ref[...] = -ADAM_LR * ((nm * c1) / (jnp.sqrt(nv * c2) + ADAM_EPS) + ADAM_WD * w_ref[...])
        nm_ref[...] = nm
        nv_ref[...] = nv

    blk = pl.BlockSpec((tr, C), lambda t: (t, 0))
    return pl.pallas_call(body, name=name, grid=(R // tr,), in_specs=[blk] * 5, out_specs=[blk] * 4,
                          out_shape=[_sds((R, C), F32)] * 4, compiler_params=_params(("parallel",)))(w, gp, gq, m, v)


def _rope(positions, after):
    inv_freq = 1.0 / (ROPE_THETA ** (jnp.arange(0, HEAD_DIM, 2, dtype=F32) / HEAD_DIM))
    inv_freq = jnp.tile(inv_freq, LANES // (HEAD_DIM // 2)).reshape(1, LANES) + after[0, 0]
    return _rope_tables(positions.reshape(-1, 1), inv_freq)


def _local_step(x, rope, target, norms, a_sink, comm):
    T, D = x.shape
    g1, gm, g2, gf = norms
    cos, sin = rope
    no_sink = jnp.zeros((2 * (B_W // LANES),), F32)
    W = {k: comm.weight(k, x) for k in ("wg1", "wu1", "wd1")}

    x1, h1, gate1, up1, act1 = _ffn_fwd(x, comm.order(g1), W["wg1"], W["wu1"], W["wd1"], "ffn1_fwd")
    W["w_in"] = comm.weight("w_in", x1)
    (h2, aq, ak, av, bq1, bk1, bv1, bq4, bk4, bv4, bq16, bk16, bv16) = _proj_rope(x1, gm, W["w_in"], cos, sin)
    cat, a_lse = _attn_fwd(aq[None], ak[None], av[None], a_sink, A_HALF_WINDOW, True, BF16, "attn_a_fwd", qb=2 * QB, blocks_per_step=4,
                           out_cols=A_Q_W + B_W)
    bqs = {1: (bq1[None], bk1[None], bv1[None]), 4: (bq4, bk4, bv4), 16: (bq16, bk16, bv16)}
    (b_hw,) = {w // (2 * d) for w, d in B_PATTERNS}
    cat, lg1, lg4, lg16 = _dilated_fwd(cat[0], bqs, b_hw)
    lg1 = lg1[0]
    W["w_out"] = comm.weight("w_out", cat)
    x2 = _out_proj(x1, cat, W["w_out"])
    for k in ("wg2", "wu2", "wd2"):
        W[k] = comm.weight(k, x2)
    x3, h3, gate2, up2, act2 = _ffn_fwd(x2, g2, W["wg2"], W["wu2"], W["wd2"], "ffn2_fwd")

    dx3, dgf, loss8 = _final_loss(x3, gf, target)
    dx2, dff2, dgate2, dup2, dg2 = _ffn_dx(dx3, x2, g2, gate2, up2, W["wg2"], W["wu2"], W["wd2"], "ffn2_dx")
    fb = gate2.shape[1] // 2
    dwg2 = _tn(dgate2, h3, fb, "ffn2_dw_gate")
    dwu2 = _tn(dup2, h3, fb, "ffn2_dw_up")
    dwd2 = _tn(act2, dff2, fb, "ffn2_dw_down")
    comm.ready(dict(wg2=dwg2, wu2=dwu2, wd2=dwd2), dwd2[0])

    doa, dla, dob1, dlb1, dob4, dlb4, dob16, dlb16 = _dcat(dx2, W["w_out"], cat)
    dw_out = _tn(cat, dx2, cat.shape[1], "w_out_dw", dep=comm.dep())
    dqa, dka, dva, dsk = _attn_bwd(aq[None], ak[None], av[None], doa[None], a_lse, dla[None], comm.order(a_sink), A_HALF_WINDOW, True,
                                   "attn_a_bwd")
    bwd_in = {1: (dob1[None], lg1[None], dlb1[None]), 4: (dob4, lg4, dlb4), 16: (dob16, lg16, dlb16)}
    bg = {}
    for w, d in B_PATTERNS:
        q_, k_, v_ = bqs[d]
        do_, l_, dl_ = bwd_in[d]
        bg[d] = _attn_bwd(q_, k_, v_, do_, l_, dl_, no_sink, w // (2 * d), False, f"attn_b{d}_bwd")[:3]
    dproj = _rope_bwd_assemble(dqa[0], dka[0], dva[0], [t[0] for t in bg[1]], bg[4], bg[16], cos, sin)
    dw_in = _tn(dproj, h2, dproj.shape[1] // 2, "w_in_dw")
    comm.ready(dict(w_in=dw_in, w_out=dw_out), dw_in[0])
    dx1, dgm = _dh_norm(dproj, W["w_in"], x1, comm.order(gm), dx2)

    dx0, dff1, dgate1, dup1, dg1 = _ffn_dx(dx1, x, g1, gate1, up1, W["wg1"], W["wu1"], W["wd1"], "ffn1_dx")
    dwd1 = _tn(act1, dff1, fb, "ffn1_dw_down")
    comm.ready(dict(wd1=dwd1), dwd1[0])
    dwg1 = _tn(dgate1, h1, fb, "ffn1_dw_gate", dep=comm.dep())
    comm.ready(dict(wg1=dwg1), dwg1[0])
    dwu1 = _tn(dup1, h1, fb, "ffn1_dw_up", dep=comm.dep())
    comm.ready(dict(wu1=dwu1), dwu1[0])

    dsink = dsk[0, :, :, ::HEAD_DIM].sum(axis=1).reshape(-1)
    small = dict(g1=dg1.sum(axis=0), gm=dgm.sum(axis=0), g2=dg2.sum(axis=0), gf=dgf.sum(axis=0), sink=dsink, loss=loss8[0, 0])
    return dx0, small


BIG = ("wg1", "wu1", "wd1", "w_in", "w_out", "wg2", "wu2", "wd2")
GATHER_GROUPS = (("w_in",), ("w_out",), ("wg2", "wu2", "wd2"))


class _Comm:
    def __init__(self, shards, meanwhile):
        x, y, c = _mesh_pos()
        self.me = (2 * x + y).astype(jnp.int32).reshape(1)
        self.shards = shards
        self.tokens = []
        self.waiting = {}
        self.groups = []
        first = ("wg1", "wu1", "wd1")
        fulls = {k: _cast_place(self.me, shards[k], f"cast_{k}") for k in first}
        plan = _neighbour_plan([fulls[k].shape for k in first])
        send, recv, bufs, tok = _push_start("gather_first_start", [fulls[k] for k in first], 2 * len(first), plan, self.me)
        self.side = meanwhile(tok)
        fulls.update({k: _cast_place(self.me, shards[k], f"cast_{k}") for k in BIG if k not in first})
        bufs = _push_wait("gather_first_wait", send, recv, bufs, plan, [fulls[k] for k in BIG if k not in first] + list(self.side))
        self.full = dict(zip(first, _gather_forward(bufs)))
        dep = self.full["wd1"]
        for gi, names in enumerate(GATHER_GROUPS):
            plan = _gather_plan(len(names))
            send, recv, bufs, tok = _push_start(f"gather_start_{gi}", [fulls[k] for k in names], 3 * len(names), plan, dep)
            self.tokens.append(tok)
            dep = tok
            for k in names:
                self.waiting[k] = (gi, names, send, recv, bufs, plan)

    def order(self, a):
        for tok in self.tokens:
            a = a + tok[0, 0]
        self.tokens = []
        return a

    def dep(self):
        return self.tokens[-1] if self.tokens else None

    def weight(self, name, after):
        if name in self.waiting:
            gi, names, send, recv, bufs, plan = self.waiting[name]
            for k, buf in zip(names, _push_wait(f"gather_wait_{gi}", send, recv, bufs, plan, after)):
                self.full[k] = buf
                del self.waiting[k]
        full = self.full[name]
        return full.reshape(N_CHIPS * full.shape[1], full.shape[2])

    def ready(self, grads, after):
        names = list(grads)
        f32s, b16s = [], []
        for k in names:
            gf, gb = grads[k]
            f32s.append(gf.reshape((N_CHIPS,) + self.shards[k].shape))
            b16s.append(gb.reshape((N_CHIPS,) + self.shards[k].shape))
        n = len(names)
        lands = [lax.empty((N_CHIPS - 1,) + self.shards[k].shape, BF16) for k in names]
        plan = _grad_push_plan(n)
        gi = len(self.groups)
        send, recv, bufs, tok = _push_start(f"grad_start_{gi}", b16s + lands, 3 * n, plan, after)
        self.tokens.append(tok)
        self.groups.append((names, f32s, send, recv, bufs, plan))

    def finish(self):
        out, swaps = {}, []
        after = self.tokens[-1]
        for gi, (names, f32s, send, recv, bufs, plan) in enumerate(self.groups):
            n = len(names)
            bufs = _push_wait(f"grad_wait_{gi}", send, recv, bufs, plan, after)
            mine = [_sum_own(self.me, f32s[i], bufs[n + i], f"sum_{k}") for i, k in enumerate(names)]
            lands = [lax.empty(p.shape, F32) for p in mine]
            send2, recv2, both, after = _push_start(f"swap_start_{gi}", mine + lands, n, _swap_plan(n), after)
            swaps.append((names, send2, recv2, both))
        for gi, (names, send2, recv2, both) in enumerate(swaps):
            n = len(names)
            both = _push_wait(f"swap_wait_{gi}", send2, recv2, both, _swap_plan(n), after)
            for i, k in enumerate(names):
                out[k] = (both[i], both[n + i])
        return out


def kernel(x, positions, norm_ffn1, w_gate1, w_up1, w_down1, norm_mix, w_in, a_sink, w_out, norm_ffn2, w_gate2, w_up2, w_down2, norm_final, loss_target, m_norm_ffn1, m_w_gate1, m_w_up1, m_w_down1, m_norm_mix, m_w_in, m_a_sink, m_w_out, m_norm_ffn2, m_w_gate2, m_w_up2, m_w_down2, m_norm_final, v_norm_ffn1, v_w_gate1, v_w_up1, v_w_down1, v_norm_mix, v_w_in, v_a_sink, v_w_out, v_norm_ffn2, v_w_gate2, v_w_up2, v_w_down2, v_norm_final):
    T, D = x.shape[1], x.shape[2]
    flip = ("wg1", "wu1", "w_in", "wg2", "wu2")

    def rows(k, a):
        return a[0].T if k in flip else a[0]

    given = dict(wg1=(w_gate1, m_w_gate1, v_w_gate1), wu1=(w_up1, m_w_up1, v_w_up1), wd1=(w_down1, m_w_down1, v_w_down1),
                 w_in=(w_in, m_w_in, v_w_in), w_out=(w_out, m_w_out, v_w_out), wg2=(w_gate2, m_w_gate2, v_w_gate2),
                 wu2=(w_up2, m_w_up2, v_w_up2), wd2=(w_down2, m_w_down2, v_w_down2))
    shards = {k: rows(k, given[k][0]) for k in BIG}

    comm = _Comm(shards, lambda tok: _rope(positions[0], tok))

    norms = (norm_ffn1, norm_mix, norm_ffn2, norm_final.reshape(1, D))
    grad_x, small = _local_step(x[0], comm.side, loss_target[0], norms, a_sink[0], comm)

    partial = comm.finish()

    def pad_row(a):
        a = a.reshape(-1)
        return jnp.pad(a, (0, D - a.shape[0]))

    row4 = pad_row(jnp.concatenate([small["sink"], small["loss"].reshape(1)]))
    vec = jnp.stack([small["g1"], small["gm"], small["g2"], small["gf"], row4] + [jnp.zeros((D,), F32)] * 3, axis=0)
    red = _allreduce_small(vec)
    loss = red[4, 8]
    g_small = jnp.stack([red[0], red[1], red[2], red[3], pad_row(red[4, 0:8])] + [jnp.zeros((D,), F32)] * 3, axis=0)

    def small_stack(a1, am, a2, af, ask):
        return jnp.stack([pad_row(a1), pad_row(am), pad_row(a2), pad_row(af), pad_row(ask)] + [jnp.zeros((D,), F32)] * 3, axis=0)

    w_small = small_stack(norm_ffn1, norm_mix, norm_ffn2, norm_final, a_sink)
    m_small = small_stack(m_norm_ffn1, m_norm_mix, m_norm_ffn2, m_norm_final, m_a_sink)
    v_small = small_stack(v_norm_ffn1, v_norm_mix, v_norm_ffn2, v_norm_final, v_a_sink)
    live = small_stack(jnp.ones_like(norm_ffn1), jnp.ones_like(norm_mix), jnp.ones_like(norm_ffn2), jnp.ones_like(norm_final), jnp.ones_like(a_sink))
    v_small = jnp.where(live > 0, v_small, 1.0)

    upd = {}
    for k in BIG:
        outs = _adamw(shards[k], partial[k][0], partial[k][1], rows(k, given[k][1]), rows(k, given[k][2]), f"adamw_{k}")
        upd[k] = tuple((a.T if k in flip else a)[None] for a in outs)
    _, ds_, nms_, nvs_ = _adamw(w_small, g_small, jnp.zeros_like(g_small), m_small, v_small, "adamw_small")

    def small_out(arr):
        return [arr[0].reshape(1, D), arr[1].reshape(1, D), arr[2].reshape(1, D), arr[3], arr[4, 0:8].reshape(1, 8)]

    gs_, dss, nmss, nvss = small_out(g_small), small_out(ds_), small_out(nms_), small_out(nvs_)

    def ordered(i):
        sm = (gs_, dss, nmss, nvss)[i]
        return [sm[0], upd["wg1"][i], upd["wu1"][i], upd["wd1"][i], sm[1], upd["w_in"][i], sm[4], upd["w_out"][i], sm[2],
                upd["wg2"][i], upd["wu2"][i], upd["wd2"][i], sm[3]]

    return (loss, grad_x[None], *ordered(0), *ordered(1), *ordered(2), *ordered(3))
```

```python
import jax
import jax.numpy as jnp
from jax import lax
from jax.experimental import pallas as pl
from jax.experimental.pallas import tpu as pltpu

F32 = jnp.float32
BF16 = jnp.bfloat16

HEAD_DIM = 64
LANES = 128
SUBLANES = 8
A_Q_W, A_KV_W, B_W = 512, 128, 512
A_HALF_WINDOW = 128
B_PATTERNS = ((128, 1), (512, 4), (2048, 16))
ROPE_THETA = 10000.0
NORM_EPS = 1e-6
FFN_RES_WEIGHT = 0.5
ADAM_LR, ADAM_B1, ADAM_B2, ADAM_EPS, ADAM_WD, ADAM_STEP = 0.001, 0.9, 0.999, 1e-08, 0.01, 10
N_CHIPS = 4
N_DEV = 8
QB = 128
SHORT_SEQ = 512
NEG = -1e30
VMEM_LIMIT = 56 * 1024 * 1024
MESH = pl.DeviceIdType.MESH
ANY = pl.BlockSpec(memory_space=pl.ANY)


def _params(sem=None):
    return pltpu.CompilerParams(dimension_semantics=sem, vmem_limit_bytes=VMEM_LIMIT)


def _sds(shape, dtype):
    return jax.ShapeDtypeStruct(tuple(shape), dtype)


def _dot(a, b):
    return jnp.dot(a, b, preferred_element_type=F32)


def _dot_nt(a, b):
    return lax.dot_general(a, b, (((1,), (1,)), ((), ())), preferred_element_type=F32)


def _dot_tn(a, b):
    return lax.dot_general(a, b, (((0,), (0,)), ((), ())), preferred_element_type=F32)


def _rms_stats(x):
    r = lax.rsqrt(jnp.mean(x * x, axis=-1, keepdims=True) + NORM_EPS)
    return x * r, r


def _rms_bwd(dh, x, g):
    xhat, r = _rms_stats(x)
    dxn = dh * g
    dx = r * (dxn - xhat * jnp.mean(dxn * xhat, axis=-1, keepdims=True))
    tm, d = x.shape
    dg = (dh * xhat).reshape(tm // SUBLANES, SUBLANES, d).sum(axis=0)
    return dx, dg


def _sigmoid(x):
    return 1.0 / (1.0 + jnp.exp(-x))


def _swap32(t):
    n = t.shape[-1]
    lane = lax.broadcasted_iota(jnp.int32, t.shape, t.ndim - 1)
    return jnp.where((lane % HEAD_DIM) < HEAD_DIM // 2, pltpu.roll(t, n - HEAD_DIM // 2, axis=t.ndim - 1),
                     pltpu.roll(t, HEAD_DIM // 2, axis=t.ndim - 1))


def _cast_place(me_arr, w, name):
    R, C = w.shape
    tr = R // 2 if (R // 2) % 16 == 0 else R

    def body(me_ref, w_ref, o_ref):
        o_ref[...] = w_ref[...].astype(BF16)

    grid_spec = pltpu.PrefetchScalarGridSpec(
        num_scalar_prefetch=1, grid=(R // tr,), in_specs=[pl.BlockSpec((tr, C), lambda t, me: (t, 0))],
        out_specs=pl.BlockSpec((None, tr, C), lambda t, me: (me[0], t, 0)))
    return pl.pallas_call(body, name=name, grid_spec=grid_spec, out_shape=_sds((N_CHIPS, R, C), BF16),
                          compiler_params=_params(("parallel",)))(me_arr, w)


HBM = pl.BlockSpec(memory_space=pltpu.HBM)
SEM = pl.BlockSpec(memory_space=pltpu.SEMAPHORE)


def _push_start(name, bufs, ncopies, plan, after):
    nb = len(bufs)

    def body(*refs):
        send, recv, token = refs[nb + 1], refs[nb + 2], refs[-1]
        for i, (src, dst, dev) in enumerate(plan(refs[:nb])):
            pltpu.make_async_remote_copy(src_ref=src, dst_ref=dst, send_sem=send.at[i], recv_sem=recv.at[i],
                                         device_id=dev, device_id_type=MESH).start()
        token[...] = jnp.zeros_like(token)

    outs = pl.pallas_call(
        body, name=name,
        out_shape=(pltpu.SemaphoreType.DMA((ncopies,)), pltpu.SemaphoreType.DMA((ncopies,)), *[pltpu.HBM(b.shape, b.dtype) for b in bufs],
                   _sds((SUBLANES, LANES), F32)),
        in_specs=[HBM] * nb + [ANY], out_specs=(SEM, SEM, *([HBM] * nb), pl.BlockSpec(memory_space=pltpu.VMEM)),
        input_output_aliases={i: 2 + i for i in range(nb)},
        compiler_params=pltpu.CompilerParams(has_side_effects=pltpu.SideEffectType.DATAFLOW_SIDE_EFFECTING),
    )(*[pltpu.with_memory_space_constraint(b, pltpu.HBM) for b in bufs], after)
    return outs[0], outs[1], list(outs[2:2 + nb]), outs[-1]


def _push_wait(name, send, recv, bufs, plan, after):
    nb = len(bufs)

    def body(*refs):
        send_ref, recv_ref = refs[nb], refs[nb + 1]
        for i, (src, dst, dev) in enumerate(plan(refs[:nb])):
            cp = pltpu.make_async_remote_copy(src_ref=src, dst_ref=dst, send_sem=send_ref.at[i], recv_sem=recv_ref.at[i],
                                              device_id=dev, device_id_type=MESH)
            cp.wait_send()
            cp.wait_recv()

    afters = list(after) if isinstance(after, (list, tuple)) else [after]
    outs = pl.pallas_call(
        body, name=name, out_shape=tuple(pltpu.HBM(b.shape, b.dtype) for b in bufs),
        in_specs=[HBM] * nb + [SEM, SEM] + [ANY] * len(afters), out_specs=tuple([HBM] * nb),
        input_output_aliases={i: i for i in range(nb)},
        compiler_params=pltpu.CompilerParams(has_side_effects=pltpu.SideEffectType.DATAFLOW_SIDE_EFFECTING),
    )(*bufs, send, recv, *afters)
    return list(outs)


def _mesh_pos():
    return lax.axis_index("x"), lax.axis_index("y"), lax.axis_index("c")


def _chip_peers(x, y, c):
    return [((1 - x, y, c), 2 * (1 - x) + y), ((x, 1 - y, c), 2 * x + (1 - y)), ((1 - x, 1 - y, c), 2 * (1 - x) + (1 - y))]


def _gather_plan(n):
    def plan(refs):
        x, y, c = _mesh_pos()
        me = 2 * x + y
        return [(refs[k].at[me], refs[k].at[me], dev) for k in range(n) for dev, _ in _chip_peers(x, y, c)]
    return plan


def _rows_of(shape, who, quarter=None):
    r2 = shape[1] // 2
    if quarter is None:
        return pl.ds(pl.multiple_of(who * r2, 16), r2)
    return pl.ds(pl.multiple_of(who * r2 + quarter * (r2 // 2), 16), r2 // 2)


def _neighbour_plan(shapes):
    def plan(refs):
        x, y, c = _mesh_pos()
        me = 2 * x + y
        return [(refs[k].at[me, _rows_of(shp, c), :], refs[k].at[me, _rows_of(shp, c), :], dev)
                for k, shp in enumerate(shapes) for dev in ((1 - x, y, c), (x, 1 - y, c))]
    return plan


def _gather_forward(fulls):
    n = len(fulls)

    def body(*refs):
        ins, outs = refs[:n], refs[n:2 * n]
        ici_send, ici_recv, d2d_send, d2d_recv = refs[2 * n:]
        x, y, c = _mesh_pos()
        cx, cy, cd = 2 * (1 - x) + y, 2 * x + (1 - y), 2 * (1 - x) + (1 - y)
        sibling, x_nbr, y_nbr = (x, y, 1 - c), (1 - x, y, c), (x, 1 - y, c)
        started = []

        def push(src, dst, send, recv, dev):
            cp = pltpu.make_async_remote_copy(src_ref=src, dst_ref=dst, send_sem=send, recv_sem=recv, device_id=dev, device_id_type=MESH)
            cp.start()
            started.append(cp)

        def arrived(blk, send, recv):
            pltpu.make_async_remote_copy(src_ref=blk, dst_ref=blk, send_sem=send, recv_sem=recv, device_id=sibling,
                                         device_id_type=MESH).wait_recv()

        for k in range(n):
            shp = fulls[k].shape
            for j, chip in enumerate((cx, cy)):
                push(ins[k].at[chip, _rows_of(shp, c), :], outs[k].at[chip, _rows_of(shp, c), :],
                     d2d_send.at[3 * k + j], d2d_recv.at[3 * k + j], sibling)
            push(ins[k].at[cx, _rows_of(shp, c, 0), :], outs[k].at[cx, _rows_of(shp, c, 0), :], ici_send.at[2 * k], ici_recv.at[2 * k], y_nbr)
            push(ins[k].at[cy, _rows_of(shp, c, 1), :], outs[k].at[cy, _rows_of(shp, c, 1), :], ici_send.at[2 * k + 1], ici_recv.at[2 * k + 1],
                 x_nbr)
        for k in range(n):
            shp = fulls[k].shape
            for q in (0, 1):
                arrived(outs[k].at[cd, _rows_of(shp, c, q), :], ici_send.at[2 * k + q], ici_recv.at[2 * k + q])
            blk = outs[k].at[cd, _rows_of(shp, c), :]
            push(blk, blk, d2d_send.at[3 * k + 2], d2d_recv.at[3 * k + 2], sibling)
        for k in range(n):
            for j, chip in enumerate((cx, cy, cd)):
                arrived(outs[k].at[chip, _rows_of(fulls[k].shape, 1 - c), :], d2d_send.at[3 * k + j], d2d_recv.at[3 * k + j])
        for cp in started:
            cp.wait_send()

    return pl.pallas_call(
        body, name="gather_forward", out_shape=[_sds(f.shape, BF16) for f in fulls],
        in_specs=[ANY] * n, out_specs=[ANY] * n, input_output_aliases={k: k for k in range(n)},
        scratch_shapes=[pltpu.SemaphoreType.DMA((n * 2,))] * 2 + [pltpu.SemaphoreType.DMA((n * 3,))] * 2,
        compiler_params=_params())(*fulls)


def _resident(shape):
    return pl.BlockSpec(shape, lambda i: (0,) * len(shape), pipeline_mode=pl.Buffered(1))


FFN_FWD_CHUNK = 256
FFN_DX_CHUNK = 512


def _chunks(n, step):
    return [(c0, min(step, n - c0)) for c0 in range(0, n, step)]


def _two_phase(chunks, first, second):
    held = {}
    for ci, ch in enumerate(chunks):
        held[ci] = first(*ch)
        if ci >= 1:
            second(*chunks[ci - 1], held.pop(ci - 1))
    last = len(chunks) - 1
    second(*chunks[last], held.pop(last))


def _ffn_fwd(x, g, wgt, wut, wd, name, tm=512):
    T, D = x.shape
    F = wd.shape[0]

    def body(x_ref, g_ref, wg_ref, wu_ref, wd_ref, xo_ref, h_ref, gate_ref, up_ref, act_ref):
        xv = x_ref[...]
        xhat, _ = _rms_stats(xv)
        h = (xhat * g_ref[...]).astype(BF16)
        h_ref[...] = h
        acc = []

        def first(c0, cw):
            return _dot_nt(h, wg_ref[c0:c0 + cw, :]), _dot_nt(h, wu_ref[c0:c0 + cw, :])

        def second(c0, cw, gate_up):
            gate, up = gate_up
            act = ((gate * _sigmoid(gate)) * up).astype(BF16)
            gate_ref[:, c0:c0 + cw] = gate.astype(BF16)
            up_ref[:, c0:c0 + cw] = up.astype(BF16)
            act_ref[:, c0:c0 + cw] = act
            d = _dot(act, wd_ref[c0:c0 + cw, :])
            acc[:] = [d if not acc else acc[0] + d]

        _two_phase(_chunks(F, FFN_FWD_CHUNK), first, second)
        xo_ref[...] = xv + FFN_RES_WEIGHT * acc[0]

    row = pl.BlockSpec((tm, D), lambda i: (i, 0))
    saved = pl.BlockSpec((tm, F), lambda i: (i, 0))
    return pl.pallas_call(
        body, name=name, grid=(T // tm,),
        in_specs=[row, pl.BlockSpec((1, D), lambda i: (0, 0)), _resident(wgt.shape), _resident(wut.shape), _resident(wd.shape)],
        out_specs=[row, row, saved, saved, saved],
        out_shape=[_sds((T, D), F32), _sds((T, D), BF16), _sds((T, F), BF16), _sds((T, F), BF16), _sds((T, F), BF16)],
        compiler_params=_params(("parallel",)))(x, g, wgt, wut, wd)


def _ffn_dx(dxo, x, g, gate_s, up_s, wgt, wut, wd, name, tm=256):
    T, D = x.shape
    F = wd.shape[0]

    def body(dxo_ref, x_ref, g_ref, gate_ref, up_ref, wg_ref, wu_ref, wd_ref, dx_ref, dff_ref, dgate_ref, dup_ref, dg_ref):
        @pl.when(pl.program_id(0) == 0)
        def _():
            dg_ref[...] = jnp.zeros_like(dg_ref)

        d = (FFN_RES_WEIGHT * dxo_ref[...]).astype(BF16)
        dff_ref[...] = d
        dh = []

        def first(c0, cw):
            return _dot_nt(d, wd_ref[c0:c0 + cw, :])

        def second(c0, cw, da):
            gate = gate_ref[:, c0:c0 + cw].astype(F32)
            up = up_ref[:, c0:c0 + cw].astype(F32)
            s = _sigmoid(gate)
            silu = gate * s
            dup = (da * silu).astype(BF16)
            dgate = (da * up * (s * (1.0 + gate * (1.0 - s)))).astype(BF16)
            dgate_ref[:, c0:c0 + cw] = dgate
            dup_ref[:, c0:c0 + cw] = dup
            t = _dot(dgate, wg_ref[c0:c0 + cw, :]) + _dot(dup, wu_ref[c0:c0 + cw, :])
            dh[:] = [t if not dh else dh[0] + t]

        _two_phase(_chunks(F, FFN_DX_CHUNK), first, second)
        dxn, dg = _rms_bwd(dh[0], x_ref[...], g_ref[...])
        dg_ref[...] += dg
        dx_ref[...] = dxo_ref[...] + dxn

    row = pl.BlockSpec((tm, D), lambda i: (i, 0))
    saved = pl.BlockSpec((tm, F), lambda i: (i, 0))
    return pl.pallas_call(
        body, name=name, grid=(T // tm,),
        in_specs=[row, row, pl.BlockSpec((1, D), lambda i: (0, 0)), saved, saved, _resident(wgt.shape), _resident(wut.shape),
                  _resident(wd.shape)],
        out_specs=[row, row, saved, saved, pl.BlockSpec((SUBLANES, D), lambda i: (0, 0))],
        out_shape=[_sds((T, D), F32), _sds((T, D), BF16), _sds((T, F), BF16), _sds((T, F), BF16), _sds((SUBLANES, D), F32)],
        compiler_params=_params(("arbitrary",)))(dxo, x, g, gate_s, up_s, wgt, wut, wd)


def _tn(a, b, mb, name, tk=2048, dep=None):
    T, M = a.shape
    N = b.shape[1]
    nt = T // tk

    def body(a_ref, b_ref, *refs):
        o_ref, ob_ref = refs[-2:]

        @pl.when(pl.program_id(1) == 0)
        def _():
            o_ref[...] = jnp.zeros_like(o_ref)

        o_ref[...] += _dot_tn(a_ref[...].astype(BF16), b_ref[...].astype(BF16))

        @pl.when(pl.program_id(1) == nt - 1)
        def _():
            ob_ref[...] = o_ref[...].astype(BF16)

    o_spec = pl.BlockSpec((mb, N), lambda g, t: (g, 0))
    return pl.pallas_call(
        body, name=name, grid=(M // mb, nt),
        in_specs=[pl.BlockSpec((tk, mb), lambda g, t: (t, g)), pl.BlockSpec((tk, N), lambda g, t: (t, 0))] + ([ANY] if dep is not None else []),
        out_specs=[o_spec, o_spec], out_shape=[_sds((M, N), F32), _sds((M, N), BF16)],
        compiler_params=_params(("parallel", "arbitrary")))(a, b, *([dep] if dep is not None else []))


def _rope_tables(pos_col, inv_freq):
    T = pos_col.shape[0]

    def body(p_ref, f_ref, c_ref, s_ref):
        ang = p_ref[...].astype(F32) * f_ref[...]
        lane = lax.broadcasted_iota(jnp.int32, ang.shape, 1)
        c_ref[...] = jnp.cos(ang)
        sn = jnp.sin(ang)
        s_ref[...] = jnp.where((lane % HEAD_DIM) < HEAD_DIM // 2, -sn, sn)

    tm = 1024
    return pl.pallas_call(
        body, name="rope_tables", grid=(T // tm,),
        in_specs=[pl.BlockSpec((tm, 1), lambda i: (i, 0)), pl.BlockSpec((1, LANES), lambda i: (0, 0))],
        out_specs=[pl.BlockSpec((tm, LANES), lambda i: (i, 0))] * 2,
        out_shape=[_sds((T, LANES), F32)] * 2, compiler_params=_params(("parallel",)))(pos_col, inv_freq)


def _deinterleave(scr, out_ref, d, tm, nblk):
    for r in range(d):
        for cb in range(nblk):
            out_ref[r, :, cb * LANES:(cb + 1) * LANES] = scr[cb, pl.ds(r, tm // d, stride=d), :].astype(out_ref.dtype)


def _interleave(in_ref, scr, d, tm, nblk):
    for r in range(d):
        for cb in range(nblk):
            scr[cb, pl.ds(r, tm // d, stride=d), :] = in_ref[r, :, cb * LANES:(cb + 1) * LANES].astype(F32)


def _proj_rope(x, g, w_in, cos, sin, tm=512):
    T, D = x.shape
    dils = [d for _, d in B_PATTERNS if d > 1]
    nbb = B_W // LANES
    scale = HEAD_DIM ** -0.5
    cuts = [0, A_Q_W, A_Q_W + A_KV_W, A_Q_W + 2 * A_KV_W, A_Q_W + 2 * A_KV_W + B_W, A_Q_W + 2 * A_KV_W + 2 * B_W,
            A_Q_W + 2 * A_KV_W + 3 * B_W]

    def body(x_ref, g_ref, w_ref, c_ref, s_ref, h_ref, aq_ref, ak_ref, av_ref, *rest):
        b_refs, scr = rest[:-1], rest[-1]
        xhat, _ = _rms_stats(x_ref[...])
        h = (xhat * g_ref[...]).astype(BF16)
        h_ref[...] = h
        cs, sn = c_ref[...], s_ref[...]

        def project(idx, ref, rope, mult, which):
            return _dot_nt(h, w_ref[cuts[idx]:cuts[idx + 1], :])

        def finish(idx, ref, rope, mult, which, whole):
            for cb in range((cuts[idx + 1] - cuts[idx]) // LANES):
                p = whole[:, cb * LANES:(cb + 1) * LANES]
                if rope:
                    p = p * cs + _swap32(p) * sn
                if mult != 1.0:
                    p = p * mult
                ref[:, cb * LANES:(cb + 1) * LANES] = p.astype(BF16)
                if which is not None:
                    scr[which, cb] = p
            if which is not None:
                for di, d in enumerate(dils):
                    _deinterleave(scr.at[which], b_refs[3 * (di + 1) + which], d, tm, nbb)

        _two_phase([(0, aq_ref, True, scale, None), (1, ak_ref, True, 1.0, None), (2, av_ref, False, 1.0, None),
                    (3, b_refs[0], True, scale, 0), (4, b_refs[1], True, 1.0, 1), (5, b_refs[2], False, 1.0, 2)], project, finish)

    row = lambda w: pl.BlockSpec((tm, w), lambda i: (i, 0))
    out_specs = [row(D), row(A_Q_W), row(A_KV_W), row(A_KV_W)] + [row(B_W)] * 3
    out_shape = [_sds((T, D), BF16), _sds((T, A_Q_W), BF16), _sds((T, A_KV_W), BF16), _sds((T, A_KV_W), BF16)] + [_sds((T, B_W), BF16)] * 3
    for d in dils:
        out_specs += [pl.BlockSpec((d, tm // d, B_W), lambda i: (0, i, 0))] * 3
        out_shape += [_sds((d, T // d, B_W), BF16)] * 3
    return pl.pallas_call(
        body, name="proj_rope", grid=(T // tm,),
        in_specs=[row(D), pl.BlockSpec((1, D), lambda i: (0, 0)), pl.BlockSpec(w_in.shape, lambda i: (0, 0)), row(LANES), row(LANES)],
        out_specs=out_specs, out_shape=out_shape, scratch_shapes=[pltpu.VMEM((3, nbb, tm, LANES), F32)],
        compiler_params=_params(("parallel",)))(x, g, w_in, cos, sin)


def _band_bias(rel, qb, kw, hw):
    ri = lax.broadcasted_iota(jnp.int32, (2 * qb, kw), 0) & (qb - 1)
    ci = lax.broadcasted_iota(jnp.int32, (2 * qb, kw), 1)
    return jnp.where(jnp.abs(ri + rel - ci) <= hw, 0.0, NEG).astype(F32)


def _stack_heads(x, lo):
    z = jnp.zeros_like(x)
    return jnp.concatenate([jnp.where(lo, x, z), jnp.where(lo, z, x)], axis=0)


def _unstack_heads(y, lo):
    qb = y.shape[0] // 2
    return jnp.where(lo, y[:qb], y[qb:])


def _band_setup(bias_scr, qb, kw, hw):
    if bias_scr is not None:
        for i in range(3):
            bias_scr[i] = _band_bias(i * hw, qb, kw, hw)


def _band_window(bias_scr, qs, L, qb, kw, hw):
    ws = pl.multiple_of(jnp.clip(qs - hw, 0, L - kw), 64)
    if bias_scr is None:
        return ws, _band_bias(qs - ws, qb, kw, hw)
    return ws, bias_scr[lax.shift_right_logical(qs - ws, hw.bit_length() - 1)]


def _dup_kv_head(src_ref, dst_ref, head, L):
    step = min(L, 1024)
    for r0 in range(0, L, step):
        xf = src_ref[r0:r0 + step, :].astype(F32)
        lane = lax.broadcasted_iota(jnp.int32, xf.shape, 1)
        keep = jnp.logical_xor(lane < HEAD_DIM, head == 1)
        dst_ref[r0:r0 + step, :] = jnp.where(keep, xf, pltpu.roll(xf, HEAD_DIM, axis=1)).astype(dst_ref.dtype)


def _attn_fwd(q, k, v, sink, hw, gqa, out_dtype, name, qb=QB, blocks_per_step=8, out_cols=None):
    NB, L, Cq = q.shape
    Ls = min(L, 2048)
    kw = min(qb + 2 * hw, L)
    tables = L >= qb + 2 * hw
    unroll = min(blocks_per_step, Ls // qb)
    nlb = 1 if (gqa or L > SHORT_SEQ) else Cq // LANES

    def body(sink_ref, q_ref, k_ref, v_ref, o_ref, lse_ref, *scr):
        b, s_idx = pl.program_id(1), pl.program_id(2)
        bias_scr = scr[0] if tables else None
        _band_setup(bias_scr, qb, kw, hw)
        if gqa:
            kd, vd = scr[-2:]

            @pl.when(s_idx == 0)
            def _():
                _dup_kv_head(k_ref, kd, b // 2, L)
                _dup_kv_head(v_ref, vd, b // 2, L)
        else:
            kd, vd = k_ref, v_ref
        lane = lax.broadcasted_iota(jnp.int32, (qb, LANES), 1)
        lo = lane < HEAD_DIM
        if gqa:
            row = lax.broadcasted_iota(jnp.int32, (2 * qb, 1), 0)
            sk = jnp.where(row < qb, sink_ref[2 * b], sink_ref[2 * b + 1])

        def block(ql, col):
            qs = s_idx * Ls + ql
            ws, bias = _band_window(bias_scr, qs, L, qb, kw, hw)
            return ws, _dot_nt(_stack_heads(q_ref[pl.ds(ql, qb), col], lo), kd[pl.ds(ws, kw), col]) + bias

        def finish(ql, col, scores):
            ws, s = scores
            m = jnp.max(s, axis=-1, keepdims=True)
            if gqa:
                m = jnp.maximum(m, sk)
            p = jnp.exp(s - m)
            den = jnp.sum(p, axis=-1, keepdims=True)
            if gqa:
                den = den + jnp.exp(sk - m)
            o = _dot(p.astype(BF16), vd[pl.ds(ws, kw), col]) * (1.0 / den)
            o_ref[pl.ds(ql, qb), col] = _unstack_heads(o, lo).astype(o_ref.dtype)
            lse_ref[pl.ds(ql, qb), col] = _unstack_heads(m + jnp.log(den), lo)

        for lb in range(nlb):
            def step(n, carry, col=slice(lb * LANES, (lb + 1) * LANES)):
                _two_phase([(pl.multiple_of((n * unroll + u) * qb, qb), col) for u in range(unroll)], block, finish)
                return carry

            lax.fori_loop(0, Ls // (qb * unroll), step, 0)

    kv_map = (lambda r, b, s: (r, 0, 0)) if gqa else (lambda r, b, s: (r, 0, b))
    seg = pl.BlockSpec((None, Ls, nlb * LANES), lambda r, b, s: (r, s, b))
    return pl.pallas_call(
        body, name=name, grid=(NB, Cq // (nlb * LANES), L // Ls),
        in_specs=[pl.BlockSpec(memory_space=pltpu.SMEM), seg, pl.BlockSpec((None, L, nlb * LANES), kv_map),
                  pl.BlockSpec((None, L, nlb * LANES), kv_map)],
        out_specs=[seg, seg], out_shape=[_sds((NB, L, out_cols or Cq), out_dtype), _sds((NB, L, Cq), F32)],
        scratch_shapes=([pltpu.VMEM((3, 2 * qb, kw), F32)] if tables else []) + ([pltpu.VMEM((L, LANES), BF16)] * 2 if gqa else []),
        compiler_params=_params(("parallel", "parallel", "arbitrary")))(sink, q, k, v)


def _attn_bwd(q, k, v, do, lse, delta, sink, hw, gqa, name, qb=QB, blocks_per_step=8):
    NB, L, Cq = q.shape
    Ck = k.shape[2]
    Ls = min(L, 2048)
    kw = min(qb + 2 * hw, L)
    reps = kw // LANES
    nseg = L // Ls
    scale = HEAD_DIM ** -0.5
    tables = L >= qb + 2 * hw
    unroll = min(blocks_per_step, Ls // qb)
    nlb = 1 if (gqa or L > SHORT_SEQ) else Cq // LANES

    def body(sink_ref, q_ref, do_ref, lse_ref, dl_ref, k_ref, v_ref, dq_ref, dk_ref, dv_ref, dsk_ref, *scr):
        b, s_idx = pl.program_id(1), pl.program_id(2)
        lane = lax.broadcasted_iota(jnp.int32, (qb, LANES), 1)
        lo = lane < HEAD_DIM
        bias_scr = scr[0] if tables else None
        _band_setup(bias_scr, qb, kw, hw)
        if gqa:
            kd, vd, dk_acc, dv_acc, dsk_acc = scr[-5:]

            @pl.when(s_idx == 0)
            def _():
                _dup_kv_head(k_ref, kd, b // 2, L)
                _dup_kv_head(v_ref, vd, b // 2, L)
                dk_acc[...] = jnp.zeros_like(dk_acc)
                dv_acc[...] = jnp.zeros_like(dv_acc)
                dsk_acc[...] = jnp.zeros_like(dsk_acc)

            @pl.when((s_idx == 0) & (b == 0))
            def _():
                dk_ref[...] = jnp.zeros_like(dk_ref)
                dv_ref[...] = jnp.zeros_like(dv_ref)
        else:
            kd, vd = k_ref, v_ref
            dk_acc, dv_acc = scr[-2:]

            @pl.when(s_idx == 0)
            def _():
                dk_acc[...] = jnp.zeros_like(dk_acc)
                dv_acc[...] = jnp.zeros_like(dv_acc)

        def block(ql, col):
            qs = s_idx * Ls + ql
            ws, bias = _band_window(bias_scr, qs, L, qb, kw, hw)
            qv, dov = q_ref[pl.ds(ql, qb), col], do_ref[pl.ds(ql, qb), col]
            lse, dl = lse_ref[pl.ds(ql, qb), col], dl_ref[pl.ds(ql, qb), col]
            kv_, vv = kd[pl.ds(ws, kw), col], vd[pl.ds(ws, kw), col]
            q2, do2 = _stack_heads(qv, lo), _stack_heads(dov, lo)
            return ws, q2, do2, lse, dl, _dot_nt(q2, kv_) + bias, _dot_nt(do2, vv)

        def finish(ql, col, held):
            ws, q2, do2, lse, dl, s, dp = held
            lse_sw, dl_sw = pltpu.roll(lse, HEAD_DIM, axis=1), pltpu.roll(dl, HEAD_DIM, axis=1)
            lse2 = jnp.concatenate([jnp.where(lo, lse, lse_sw), jnp.where(lo, lse_sw, lse)], axis=0)
            dl2 = jnp.concatenate([jnp.where(lo, dl, dl_sw), jnp.where(lo, dl_sw, dl)], axis=0)
            p = jnp.exp(s - jnp.tile(lse2, (1, reps)))
            ds = (p * (dp - jnp.tile(dl2, (1, reps)))).astype(BF16)
            dq_ref[pl.ds(ql, qb), col] = (_unstack_heads(_dot(ds, kd[pl.ds(ws, kw), col]), lo) * scale).astype(dq_ref.dtype)
            both = _dot_tn(jnp.concatenate([ds, p.astype(BF16)], axis=1), jnp.concatenate([q2, do2], axis=1))
            dk_acc[pl.ds(ws, kw), col] += both[:kw, :LANES]
            dv_acc[pl.ds(ws, kw), col] += both[kw:, LANES:]
            if gqa:
                sk = jnp.where(lo, sink_ref[2 * b], sink_ref[2 * b + 1])
                dsk_acc[...] += -jnp.exp(sk - lse) * dl

        for lb in range(nlb):
            def step(n, carry, col=slice(lb * LANES, (lb + 1) * LANES)):
                _two_phase([(pl.multiple_of((n * unroll + u) * qb, qb), col) for u in range(unroll)], block, finish)
                return carry

            lax.fori_loop(0, Ls // (qb * unroll), step, 0)

        if gqa:
            @pl.when(s_idx == nseg - 1)
            def _():
                step_rows = min(L, 1024)
                for r0 in range(0, L, step_rows):
                    lanek = lax.broadcasted_iota(jnp.int32, (step_rows, LANES), 1)
                    mine = jnp.logical_xor(lanek < HEAD_DIM, (b // 2) == 1)
                    for acc, ref in ((dk_acc, dk_ref), (dv_acc, dv_ref)):
                        a = acc[r0:r0 + step_rows, :]
                        ref[r0:r0 + step_rows, :] += jnp.where(mine, a + pltpu.roll(a, HEAD_DIM, axis=1), 0.0)
                dsk_ref[...] = dsk_acc[...].reshape(qb // SUBLANES, SUBLANES, LANES).sum(axis=0)
        else:
            dsk_ref[...] = jnp.zeros_like(dsk_ref)

            @pl.when(s_idx == nseg - 1)
            def _():
                dk_ref[...] = dk_acc[...].astype(dk_ref.dtype)
                dv_ref[...] = dv_acc[...].astype(dv_ref.dtype)

    kv_map = (lambda r, b, s: (r, 0, 0)) if gqa else (lambda r, b, s: (r, 0, b))
    seg = pl.BlockSpec((None, Ls, nlb * LANES), lambda r, b, s: (r, s, b))
    full = pl.BlockSpec((None, L, nlb * LANES), kv_map)
    scratch = [pltpu.VMEM((3, 2 * qb, kw), F32)] if tables else []
    if gqa:
        scratch += [pltpu.VMEM((L, LANES), BF16)] * 2 + [pltpu.VMEM((L, LANES), F32)] * 2 + [pltpu.VMEM((qb, LANES), F32)]
    else:
        scratch += [pltpu.VMEM((L, nlb * LANES), F32)] * 2
    kv_dtype = F32 if gqa else BF16
    return pl.pallas_call(
        body, name=name, grid=(NB, Cq // (nlb * LANES), nseg),
        in_specs=[pl.BlockSpec(memory_space=pltpu.SMEM), seg, seg, seg, seg, full, full],
        out_specs=[seg, full, full, pl.BlockSpec((None, None, SUBLANES, LANES), lambda r, b, s: (r, b, 0, 0))],
        out_shape=[_sds((NB, L, Cq), BF16), _sds((NB, L, Ck), kv_dtype), _sds((NB, L, Ck), kv_dtype),
                   _sds((NB, Cq // LANES, SUBLANES, LANES), F32)],
        scratch_shapes=scratch,
        compiler_params=_params(("arbitrary", "arbitrary", "arbitrary")))(sink, q, do, lse, delta, k, v)


def _dilated_fwd(cat, qkv, hw, tile=2048):
    T = cat.shape[0]
    dils = sorted(qkv)
    nbb, na = B_W // LANES, A_Q_W // LANES
    qb, kw = QB, QB + 2 * hw
    rows_merge = 256
    assert T % tile == 0 and all(tile % (d * qb) == 0 and T // d >= kw for d in dils)

    def body(cat_in, *refs):
        qkv_refs = {d: refs[3 * j:3 * j + 3] for j, d in enumerate(dils)}
        cat_ref, lg_refs = refs[3 * len(dils)], refs[3 * len(dils) + 1:4 * len(dils) + 1]
        o_scr, l_scr, bias_scr = refs[4 * len(dils) + 1:]
        i = pl.program_id(1)
        _band_setup(bias_scr, qb, kw, hw)
        lane = lax.broadcasted_iota(jnp.int32, (qb, LANES), 1)
        lo = lane < HEAD_DIM
        for pi, d in enumerate(dils):
            q_ref, k_ref, v_ref = qkv_refs[d]
            L, rows = T // d, tile // d

            def place(r, n, d=d):
                return pl.ds(r + d * n * qb, qb, stride=d) if d > 1 else pl.ds(n * qb, qb)

            def scores(r, n, q_ref=q_ref, k_ref=k_ref, L=L, rows=rows):
                ws, bias = _band_window(bias_scr, i * rows + n * qb, L, qb, kw, hw)
                return ws, _dot_nt(_stack_heads(q_ref[r, n * qb:(n + 1) * qb, :], lo), k_ref[r, pl.ds(ws, kw), :]) + bias

            def finish(r, n, held, v_ref=v_ref, pi=pi, place=place):
                ws, s = held
                m = jnp.max(s, axis=-1, keepdims=True)
                p = jnp.exp(s - m)
                den = jnp.sum(p, axis=-1, keepdims=True)
                o = _dot(p.astype(BF16), v_ref[r, pl.ds(ws, kw), :]) * (1.0 / den)
                o_scr[pi, place(r, n), :] = _unstack_heads(o, lo)
                l_scr[pi, place(r, n), :] = _unstack_heads(m + jnp.log(den), lo)

            blocks = [(r, n) for r in range(d) for n in range(rows // qb)]
            for g0 in range(0, len(blocks), 8):
                _two_phase(blocks[g0:g0 + 8], scores, finish)

        for r0 in range(0, tile, rows_merge):
            rs = slice(r0, r0 + rows_merge)
            ls_ = [l_scr[pi, rs, :] for pi in range(len(dils))]
            m = ls_[0]
            for l in ls_[1:]:
                m = jnp.maximum(m, l)
            es = [jnp.exp(l - m) for l in ls_]
            den, out = es[0], es[0] * o_scr[0, rs, :]
            for pi in range(1, len(dils)):
                den = den + es[pi]
                out = out + es[pi] * o_scr[pi, rs, :]
            cat_ref[rs, :] = (out * (1.0 / den)).astype(BF16)
            l_scr[0, rs, :] = m + jnp.log(den)
        for lg_ref, d in zip(lg_refs, dils):
            for r in range(d):
                lg_ref[r] = l_scr[0, pl.ds(r, tile // d, stride=d), :] if d > 1 else l_scr[0]

    in_specs = [pl.BlockSpec(memory_space=pl.ANY)]
    operands = [cat]
    for d in dils:
        in_specs += [pl.BlockSpec((d, tile // d, LANES), lambda b, i: (0, i, b))] + [pl.BlockSpec((d, T // d, LANES), lambda b, i: (0, 0, b))] * 2
        operands += list(qkv[d])
    return pl.pallas_call(
        body, name="dilated_fwd", grid=(nbb, T // tile), in_specs=in_specs,
        out_specs=[pl.BlockSpec((tile, LANES), lambda b, i: (i, na + b))] + [pl.BlockSpec((d, tile // d, LANES), lambda b, i: (0, i, b)) for d in dils],
        out_shape=[_sds(cat.shape, BF16)] + [_sds((d, T // d, B_W), F32) for d in dils],
        input_output_aliases={0: 0},
        scratch_shapes=[pltpu.VMEM((len(dils), tile, LANES), F32)] * 2 + [pltpu.VMEM((3, 2 * qb, kw), F32)],
        compiler_params=_params(("parallel", "arbitrary")))(*operands)


def _out_proj(x, cat, w_out, tm=512):
    T, D = x.shape

    def body(x_ref, c_ref, w_ref, o_ref):
        o_ref[...] = x_ref[...] + _dot(c_ref[...], w_ref[...])

    row = lambda w: pl.BlockSpec((tm, w), lambda i: (i, 0))
    return pl.pallas_call(
        body, name="out_proj", grid=(T // tm,), in_specs=[row(D), row(cat.shape[1]), pl.BlockSpec(w_out.shape, lambda i: (0, 0))],
        out_specs=row(D), out_shape=_sds((T, D), F32), compiler_params=_params(("parallel",)))(x, cat, w_out)


def _final_loss(x, g, target, tm=512):
    T, D = x.shape

    def body(x_ref, g_ref, t_ref, dx_ref, dg_ref, loss_ref):
        @pl.when(pl.program_id(0) == 0)
        def _():
            dg_ref[...] = jnp.zeros_like(dg_ref)
            loss_ref[...] = jnp.zeros_like(loss_ref)

        xv, gv = x_ref[...], g_ref[...]
        xhat, _ = _rms_stats(xv)
        err = xhat * gv - t_ref[...]
        loss_ref[...] += 0.5 * jnp.sum(jnp.sum(err * err, axis=-1, keepdims=True) * (1.0 / D), axis=0, keepdims=True)
        dx, dg = _rms_bwd(err * (1.0 / D), xv, gv)
        dx_ref[...] = dx
        dg_ref[...] += dg

    row = pl.BlockSpec((tm, D), lambda i: (i, 0))
    return pl.pallas_call(
        body, name="final_loss", grid=(T // tm,), in_specs=[row, pl.BlockSpec((1, D), lambda i: (0, 0)), row],
        out_specs=[row, pl.BlockSpec((SUBLANES, D), lambda i: (0, 0)), pl.BlockSpec((SUBLANES, LANES), lambda i: (0, 0))],
        out_shape=[_sds((T, D), F32), _sds((SUBLANES, D), F32), _sds((SUBLANES, LANES), F32)],
        compiler_params=_params(("arbitrary",)))(x, g, target)


def _dcat(dx, w_out, cat, tm=512):
    T, D = dx.shape
    C = cat.shape[1]
    nba, nbb = A_Q_W // LANES, B_W // LANES

    def body(dx_ref, w_ref, cat_ref, doa_ref, dla_ref, dob1_ref, dlb1_ref, dob4_ref, dlb4_ref, dob16_ref, dlb16_ref, sdo, sdl):
        dc = _dot_nt(dx_ref[...].astype(BF16), w_ref[...])
        ri = lax.broadcasted_iota(jnp.int32, (LANES, LANES), 0)
        ci = lax.broadcasted_iota(jnp.int32, (LANES, LANES), 1)
        same_head = ((ri // HEAD_DIM) == (ci // HEAD_DIM)).astype(BF16)
        for cb in range(C // LANES):
            cols = slice(cb * LANES, (cb + 1) * LANES)
            blk = dc[:, cols]
            prod = blk * cat_ref[:, cols].astype(F32)
            hi = prod.astype(BF16)
            lo_ = (prod - hi.astype(F32)).astype(BF16)
            dl = _dot(hi, same_head) + _dot(lo_, same_head)
            if cb < nba:
                doa_ref[:, cols] = blk.astype(BF16)
                dla_ref[:, cols] = dl
            else:
                bcols = slice((cb - nba) * LANES, (cb - nba + 1) * LANES)
                dob1_ref[:, bcols] = blk.astype(BF16)
                dlb1_ref[:, bcols] = dl
                sdo[cb - nba] = blk
                sdl[cb - nba] = dl
        _deinterleave(sdo, dob4_ref, 4, tm, nbb)
        _deinterleave(sdl, dlb4_ref, 4, tm, nbb)
        _deinterleave(sdo, dob16_ref, 16, tm, nbb)
        _deinterleave(sdl, dlb16_ref, 16, tm, nbb)

    row = lambda w: pl.BlockSpec((tm, w), lambda i: (i, 0))
    perm = lambda d: pl.BlockSpec((d, tm // d, B_W), lambda i: (0, i, 0))
    return pl.pallas_call(
        body, name="dcat", grid=(T // tm,), in_specs=[row(D), pl.BlockSpec(w_out.shape, lambda i: (0, 0)), row(C)],
        out_specs=[row(A_Q_W), row(A_Q_W), row(B_W), row(B_W), perm(4), perm(4), perm(16), perm(16)],
        out_shape=[_sds((T, A_Q_W), BF16), _sds((T, A_Q_W), F32), _sds((T, B_W), BF16), _sds((T, B_W), F32),
                   _sds((4, T // 4, B_W), BF16), _sds((4, T // 4, B_W), F32), _sds((16, T // 16, B_W), BF16), _sds((16, T // 16, B_W), F32)],
        scratch_shapes=[pltpu.VMEM((nbb, tm, LANES), F32)] * 2, compiler_params=_params(("parallel",)))(dx, w_out, cat)


def _rope_bwd_assemble(dqa, dka, dva, b1, b4, b16, cos, sin, tm=512):
    T = dqa.shape[0]
    nbb = B_W // LANES
    width = A_Q_W + 2 * A_KV_W + 3 * B_W

    def body(dqa_ref, dka_ref, dva_ref, q1, k1, v1, q4, k4, v4, q16, k16, v16, c_ref, s_ref, o_ref, scr):
        cs, sn = c_ref[...], s_ref[...]

        def unrope(t):
            return t * cs + _swap32(t * sn)

        col = 0
        for ref, rope in ((dqa_ref, True), (dka_ref, True), (dva_ref, False)):
            for cb in range(ref.shape[1] // LANES):
                t = ref[:, cb * LANES:(cb + 1) * LANES].astype(F32)
                o_ref[:, col:col + LANES] = (unrope(t) if rope else t).astype(BF16)
                col += LANES
        for which, (r1, r4, r16, rope) in enumerate(((q1, q4, q16, True), (k1, k4, k16, True), (v1, v4, v16, False))):
            _interleave(r4, scr.at[0], 4, tm, nbb)
            _interleave(r16, scr.at[1], 16, tm, nbb)
            for cb in range(nbb):
                t = r1[:, cb * LANES:(cb + 1) * LANES].astype(F32) + scr[0, cb] + scr[1, cb]
                o_ref[:, col:col + LANES] = (unrope(t) if rope else t).astype(BF16)
                col += LANES

    row = lambda w: pl.BlockSpec((tm, w), lambda i: (i, 0))
    perm = lambda d: pl.BlockSpec((d, tm // d, B_W), lambda i: (0, i, 0))
    return pl.pallas_call(
        body, name="rope_bwd", grid=(T // tm,),
        in_specs=[row(A_Q_W), row(A_KV_W), row(A_KV_W)] + [row(B_W)] * 3 + [perm(4)] * 3 + [perm(16)] * 3 + [row(LANES), row(LANES)],
        out_specs=row(width), out_shape=_sds((T, width), BF16), scratch_shapes=[pltpu.VMEM((2, nbb, tm, LANES), F32)],
        compiler_params=_params(("parallel",)))(dqa, dka, dva, *b1, *b4, *b16, cos, sin)


def _dh_norm(dproj, w_in, x, g, dres, tm=512):
    T, D = x.shape

    def body(dp_ref, w_ref, x_ref, g_ref, dr_ref, dx_ref, dg_ref):
        @pl.when(pl.program_id(0) == 0)
        def _():
            dg_ref[...] = jnp.zeros_like(dg_ref)

        dxn, dg = _rms_bwd(_dot(dp_ref[...], w_ref[...]), x_ref[...], g_ref[...])
        dg_ref[...] += dg
        dx_ref[...] = dr_ref[...] + dxn

    row = lambda w: pl.BlockSpec((tm, w), lambda i: (i, 0))
    return pl.pallas_call(
        body, name="dh_norm", grid=(T // tm,),
        in_specs=[row(dproj.shape[1]), pl.BlockSpec(w_in.shape, lambda i: (0, 0)), row(D), pl.BlockSpec((1, D), lambda i: (0, 0)), row(D)],
        out_specs=[row(D), pl.BlockSpec((SUBLANES, D), lambda i: (0, 0))],
        out_shape=[_sds((T, D), F32), _sds((SUBLANES, D), F32)], compiler_params=_params(("arbitrary",)))(dproj, w_in, x, g, dres)


def _grad_push_plan(n):
    def plan(refs):
        x, y, c = _mesh_pos()
        return [(refs[k].at[chip], refs[n + k].at[rel], dev) for k in range(n) for rel, (dev, chip) in enumerate(_chip_peers(x, y, c))]
    return plan


def _sum_own(me_arr, g, landed, name):
    ns, R, C = g.shape
    tr = R // 2 if (R // 2) % 16 == 0 else R

    def body(me_ref, g_ref, x_ref, o_ref):
        acc = g_ref[...]
        for rel in range(ns - 1):
            acc = acc + x_ref[rel].astype(F32)
        o_ref[...] = acc

    grid_spec = pltpu.PrefetchScalarGridSpec(
        num_scalar_prefetch=1, grid=(R // tr,),
        in_specs=[pl.BlockSpec((None, tr, C), lambda t, me: (me[0], t, 0)), pl.BlockSpec((ns - 1, tr, C), lambda t, me: (0, t, 0))],
        out_specs=pl.BlockSpec((tr, C), lambda t, me: (t, 0)))
    return pl.pallas_call(body, name=name, grid_spec=grid_spec, out_shape=_sds((R, C), F32),
                          compiler_params=_params(("parallel",)))(me_arr, g, landed)


def _swap_plan(n):
    def plan(refs):
        x, y, c = _mesh_pos()
        return [(refs[k], refs[n + k], (x, y, 1 - c)) for k in range(n)]
    return plan


def _allreduce_small(v):
    rows, W = v.shape

    def body(v_ref, o_ref, buf, send, recv):
        x, y, c = _mesh_pos()
        me = 4 * x + 2 * y + c
        cps = []
        for m in range(1, N_DEV):
            dev = (x ^ (m >> 2), y ^ ((m >> 1) & 1), c ^ (m & 1))
            cp = pltpu.make_async_remote_copy(src_ref=v_ref, dst_ref=buf.at[me], send_sem=send.at[m - 1], recv_sem=recv.at[m - 1],
                                              device_id=dev, device_id_type=MESH)
            cp.start()
            cps.append(cp)
        for m in range(1, N_DEV):
            pltpu.make_async_remote_copy(src_ref=v_ref, dst_ref=buf.at[me ^ m], send_sem=send.at[m - 1], recv_sem=recv.at[m - 1],
                                         device_id=(x, y, c), device_id_type=MESH).wait_recv()
        for cp in cps:
            cp.wait_send()
        buf[me] = v_ref[...]
        acc = buf[0]
        for i in range(1, N_DEV):
            acc = acc + buf[i]
        o_ref[...] = acc

    return pl.pallas_call(
        body, name="allreduce_small", out_shape=_sds((rows, W), F32),
        scratch_shapes=[pltpu.VMEM((N_DEV, rows, W), F32), pltpu.SemaphoreType.DMA((N_DEV - 1,)), pltpu.SemaphoreType.DMA((N_DEV - 1,))],
        compiler_params=_params())(v)


def _adamw_math(w, g, m, v):
    c1 = 1.0 / (1.0 - ADAM_B1 ** ADAM_STEP)
    c2 = 1.0 / (1.0 - ADAM_B2 ** ADAM_STEP)
    nm = ADAM_B1 * m + (1.0 - ADAM_B1) * g
    nv = ADAM_B2 * v + (1.0 - ADAM_B2) * (g * g)
    return -ADAM_LR * ((nm * c1) / (jnp.sqrt(nv * c2) + ADAM_EPS) + ADAM_WD * w), nm, nv


def _adamw_small(rows, sink_g, params):
    n = len(params)

    def body(rows_ref, sink_ref, *refs):
        ins, outs = refs[:3 * n], refs[3 * n:]
        for j in range(n):
            g = sink_ref[...] if j == n - 1 else rows_ref[j:j + 1, :]
            d, nm, nv = _adamw_math(ins[3 * j][...], g, ins[3 * j + 1][...], ins[3 * j + 2][...])
            for ref, val in zip(outs[4 * j:4 * j + 4], (g, d, nm, nv)):
                ref[...] = val

    flat = [a for p in params for a in p]
    outs = pl.pallas_call(body, name="adamw_small", out_shape=[_sds(p[0].shape, F32) for p in params for _ in range(4)],
                          compiler_params=_params())(rows, sink_g, *flat)
    return [outs[4 * j:4 * j + 4] for j in range(n)]


def _adamw(w, gp, gq, m, v, name):
    R, C = w.shape
    tr = R // 2 if (R // 2) % SUBLANES == 0 else R

    def body(w_ref, gp_ref, gq_ref, m_ref, v_ref, g_ref, d_ref, nm_ref, nv_ref):
        gv = gp_ref[...] + gq_ref[...]
        g_ref[...] = gv
        d_ref[...], nm_ref[...], nv_ref[...] = _adamw_math(w_ref[...], gv, m_ref[...], v_ref[...])

    blk = pl.BlockSpec((tr, C), lambda t: (t, 0))
    return pl.pallas_call(body, name=name, grid=(R // tr,), in_specs=[blk] * 5, out_specs=[blk] * 4,
                          out_shape=[_sds((R, C), F32)] * 4, compiler_params=_params(("parallel",)))(w, gp, gq, m, v)


def _rope(positions, after):
    inv_freq = 1.0 / (ROPE_THETA ** (jnp.arange(0, HEAD_DIM, 2, dtype=F32) / HEAD_DIM))
    inv_freq = jnp.tile(inv_freq, LANES // (HEAD_DIM // 2)).reshape(1, LANES) + after[0, 0]
    return _rope_tables(positions.reshape(-1, 1), inv_freq)


def _local_step(x, rope, target, norms, a_sink, comm):
    T, D = x.shape
    g1, gm, g2, gf = norms
    cos, sin = rope
    no_sink = jnp.zeros((2 * (B_W // LANES),), F32)
    W = {k: comm.weight(k, x) for k in ("wg1", "wu1", "wd1")}

    x1, h1, gate1, up1, act1 = _ffn_fwd(x, comm.order(g1), W["wg1"], W["wu1"], W["wd1"], "ffn1_fwd")
    W["w_in"] = comm.weight("w_in", x1)
    (h2, aq, ak, av, bq1, bk1, bv1, bq4, bk4, bv4, bq16, bk16, bv16) = _proj_rope(x1, gm, W["w_in"], cos, sin)
    cat, a_lse = _attn_fwd(aq[None], ak[None], av[None], a_sink, A_HALF_WINDOW, True, BF16, "attn_a_fwd", qb=2 * QB, blocks_per_step=4,
                           out_cols=A_Q_W + B_W)
    bqs = {1: (bq1[None], bk1[None], bv1[None]), 4: (bq4, bk4, bv4), 16: (bq16, bk16, bv16)}
    (b_hw,) = {w // (2 * d) for w, d in B_PATTERNS}
    cat, lg1, lg4, lg16 = _dilated_fwd(cat[0], bqs, b_hw)
    lg1 = lg1[0]
    W["w_out"] = comm.weight("w_out", cat)
    x2 = _out_proj(x1, cat, W["w_out"])
    for k in ("wg2", "wu2", "wd2"):
        W[k] = comm.weight(k, x2)
    x3, h3, gate2, up2, act2 = _ffn_fwd(x2, g2, W["wg2"], W["wu2"], W["wd2"], "ffn2_fwd")

    dx3, dgf, loss8 = _final_loss(x3, gf, target)
    dx2, dff2, dgate2, dup2, dg2 = _ffn_dx(dx3, x2, g2, gate2, up2, W["wg2"], W["wu2"], W["wd2"], "ffn2_dx")
    fb = gate2.shape[1] // 2
    dwg2 = _tn(dgate2, h3, fb, "ffn2_dw_gate")
    dwu2 = _tn(dup2, h3, fb, "ffn2_dw_up")
    dwd2 = _tn(act2, dff2, fb, "ffn2_dw_down")
    comm.ready(dict(wg2=dwg2, wu2=dwu2, wd2=dwd2), dwd2[0])

    doa, dla, dob1, dlb1, dob4, dlb4, dob16, dlb16 = _dcat(dx2, W["w_out"], cat)
    dw_out = _tn(cat, dx2, cat.shape[1], "w_out_dw", dep=comm.dep())
    dqa, dka, dva, dsk = _attn_bwd(aq[None], ak[None], av[None], doa[None], a_lse, dla[None], comm.order(a_sink), A_HALF_WINDOW, True,
                                   "attn_a_bwd")
    bwd_in = {1: (dob1[None], lg1[None], dlb1[None]), 4: (dob4, lg4, dlb4), 16: (dob16, lg16, dlb16)}
    bg = {}
    for w, d in B_PATTERNS:
        q_, k_, v_ = bqs[d]
        do_, l_, dl_ = bwd_in[d]
        bg[d] = _attn_bwd(q_, k_, v_, do_, l_, dl_, no_sink, w // (2 * d), False, f"attn_b{d}_bwd")[:3]
    dproj = _rope_bwd_assemble(dqa[0], dka[0], dva[0], [t[0] for t in bg[1]], bg[4], bg[16], cos, sin)
    dw_in = _tn(dproj, h2, dproj.shape[1] // 2, "w_in_dw")
    comm.ready(dict(w_in=dw_in, w_out=dw_out), dw_in[0])
    dx1, dgm = _dh_norm(dproj, W["w_in"], x1, comm.order(gm), dx2)

    dx0, dff1, dgate1, dup1, dg1 = _ffn_dx(dx1, x, g1, gate1, up1, W["wg1"], W["wu1"], W["wd1"], "ffn1_dx")
    dwd1 = _tn(act1, dff1, fb, "ffn1_dw_down")
    comm.ready(dict(wd1=dwd1), dwd1[0])
    dwg1 = _tn(dgate1, h1, fb, "ffn1_dw_gate", dep=comm.dep())
    comm.ready(dict(wg1=dwg1), dwg1[0])
    dwu1 = _tn(dup1, h1, fb, "ffn1_dw_up", dep=comm.dep())
    comm.ready(dict(wu1=dwu1), dwu1[0])

    dsink = dsk[0, :, :, ::HEAD_DIM].sum(axis=1).reshape(-1)
    small = dict(g1=dg1.sum(axis=0), gm=dgm.sum(axis=0), g2=dg2.sum(axis=0), gf=dgf.sum(axis=0), sink=dsink, loss=loss8[0, 0])
    return dx0, small


BIG = ("wg1", "wu1", "wd1", "w_in", "w_out", "wg2", "wu2", "wd2")
GATHER_GROUPS = (("w_in",), ("w_out",), ("wg2", "wu2", "wd2"))


class _Comm:
    def __init__(self, shards, meanwhile):
        x, y, c = _mesh_pos()
        self.me = (2 * x + y).astype(jnp.int32).reshape(1)
        self.shards = shards
        self.tokens = []
        self.waiting = {}
        self.groups = []
        first = ("wg1", "wu1", "wd1")
        fulls = {k: _cast_place(self.me, shards[k], f"cast_{k}") for k in first}
        plan = _neighbour_plan([fulls[k].shape for k in first])
        send, recv, bufs, tok = _push_start("gather_first_start", [fulls[k] for k in first], 2 * len(first), plan, self.me)
        self.side = meanwhile(tok)
        fulls.update({k: _cast_place(self.me, shards[k], f"cast_{k}") for k in BIG if k not in first})
        bufs = _push_wait("gather_first_wait", send, recv, bufs, plan, [fulls[k] for k in BIG if k not in first] + list(self.side))
        self.full = dict(zip(first, _gather_forward(bufs)))
        dep = self.full["wd1"]
        for gi, names in enumerate(GATHER_GROUPS):
            plan = _gather_plan(len(names))
            send, recv, bufs, tok = _push_start(f"gather_start_{gi}", [fulls[k] for k in names], 3 * len(names), plan, dep)
            self.tokens.append(tok)
            dep = tok
            for k in names:
                self.waiting[k] = (gi, names, send, recv, bufs, plan)

    def order(self, a):
        for tok in self.tokens:
            a = a + tok[0, 0]
        self.tokens = []
        return a

    def dep(self):
        return self.tokens[-1] if self.tokens else None

    def weight(self, name, after):
        if name in self.waiting:
            gi, names, send, recv, bufs, plan = self.waiting[name]
            for k, buf in zip(names, _push_wait(f"gather_wait_{gi}", send, recv, bufs, plan, after)):
                self.full[k] = buf
                del self.waiting[k]
        full = self.full[name]
        return full.reshape(N_CHIPS * full.shape[1], full.shape[2])

    def ready(self, grads, after):
        names = list(grads)
        f32s, b16s = [], []
        for k in names:
            gf, gb = grads[k]
            f32s.append(gf.reshape((N_CHIPS,) + self.shards[k].shape))
            b16s.append(gb.reshape((N_CHIPS,) + self.shards[k].shape))
        n = len(names)
        lands = [lax.empty((N_CHIPS - 1,) + self.shards[k].shape, BF16) for k in names]
        plan = _grad_push_plan(n)
        gi = len(self.groups)
        send, recv, bufs, tok = _push_start(f"grad_start_{gi}", b16s + lands, 3 * n, plan, after)
        self.tokens.append(tok)
        self.groups.append((names, f32s, send, recv, bufs, plan))

    def finish(self):
        out, swaps = {}, []
        after = self.tokens[-1]
        early = 2
        for batch in (self.groups[:early], self.groups[early:]):
            names_b, mine_b = [], []
            for names, f32s, send, recv, bufs, plan in batch:
                n = len(names)
                bufs = _push_wait(f"grad_wait_{names[0]}", send, recv, bufs, plan, after)
                mine_b += [_sum_own(self.me, f32s[i], bufs[n + i], f"sum_{k}") for i, k in enumerate(names)]
                names_b += names
            lands = [lax.empty(p.shape, F32) for p in mine_b]
            n = len(names_b)
            send2, recv2, both, after = _push_start(f"swap_start_{names_b[0]}", mine_b + lands, n, _swap_plan(n), after)
            swaps.append((names_b, send2, recv2, both))
        for names_b, send2, recv2, both in swaps:
            n = len(names_b)
            both = _push_wait(f"swap_wait_{names_b[0]}", send2, recv2, both, _swap_plan(n), after)
            for i, k in enumerate(names_b):
                out[k] = (both[i], both[n + i])
        return out


def kernel(x, positions, norm_ffn1, w_gate1, w_up1, w_down1, norm_mix, w_in, a_sink, w_out, norm_ffn2, w_gate2, w_up2, w_down2, norm_final, loss_target, m_norm_ffn1, m_w_gate1, m_w_up1, m_w_down1, m_norm_mix, m_w_in, m_a_sink, m_w_out, m_norm_ffn2, m_w_gate2, m_w_up2, m_w_down2, m_norm_final, v_norm_ffn1, v_w_gate1, v_w_up1, v_w_down1, v_norm_mix, v_w_in, v_a_sink, v_w_out, v_norm_ffn2, v_w_gate2, v_w_up2, v_w_down2, v_norm_final):
    T, D = x.shape[1], x.shape[2]
    flip = ("wg1", "wu1", "w_in", "wg2", "wu2")

    def rows(k, a):
        return a[0].T if k in flip else a[0]

    given = dict(wg1=(w_gate1, m_w_gate1, v_w_gate1), wu1=(w_up1, m_w_up1, v_w_up1), wd1=(w_down1, m_w_down1, v_w_down1),
                 w_in=(w_in, m_w_in, v_w_in), w_out=(w_out, m_w_out, v_w_out), wg2=(w_gate2, m_w_gate2, v_w_gate2),
                 wu2=(w_up2, m_w_up2, v_w_up2), wd2=(w_down2, m_w_down2, v_w_down2))
    shards = {k: rows(k, given[k][0]) for k in BIG}

    comm = _Comm(shards, lambda tok: _rope(positions[0], tok))

    norms = (norm_ffn1, norm_mix, norm_ffn2, norm_final.reshape(1, D))
    grad_x, small = _local_step(x[0], comm.side, loss_target[0], norms, a_sink[0], comm)

    partial = comm.finish()

    def pad_row(a):
        a = a.reshape(-1)
        return jnp.pad(a, (0, D - a.shape[0]))

    row4 = pad_row(jnp.concatenate([small["sink"], small["loss"].reshape(1)]))
    vec = jnp.stack([small["g1"], small["gm"], small["g2"], small["gf"], row4] + [jnp.zeros((D,), F32)] * 3, axis=0)
    red = _allreduce_small(vec)
    loss = red[4, 8]

    upd = {}
    for k in BIG:
        outs = _adamw(shards[k], partial[k][0], partial[k][1], rows(k, given[k][1]), rows(k, given[k][2]), f"adamw_{k}")
        upd[k] = tuple((a.T if k in flip else a)[None] for a in outs)
    as_row = lambda a: a.reshape(1, -1)
    sm = _adamw_small(red, red[4:5, 0:8], [tuple(as_row(a) for a in p) for p in (
        (norm_ffn1, m_norm_ffn1, v_norm_ffn1), (norm_mix, m_norm_mix, v_norm_mix), (norm_ffn2, m_norm_ffn2, v_norm_ffn2),
        (norm_final, m_norm_final, v_norm_final), (a_sink, m_a_sink, v_a_sink))])
    sm[3] = [a.reshape(D) for a in sm[3]]

    def ordered(i):
        return [sm[0][i], upd["wg1"][i], upd["wu1"][i], upd["wd1"][i], sm[1][i], upd["w_in"][i], sm[4][i], upd["w_out"][i], sm[2][i],
                upd["wg2"][i], upd["wu2"][i], upd["wd2"][i], sm[3][i]]

    return (loss, grad_x[None], *ordered(0), *ordered(1), *ordered(2), *ordered(3))
```

```python
import jax
import jax.numpy as jnp
from jax import lax
from jax.experimental import pallas as pl
from jax.experimental.pallas import tpu as pltpu

F32 = jnp.float32
BF16 = jnp.bfloat16

HEAD_DIM = 64
LANES = 128
SUBLANES = 8
A_Q_W, A_KV_W, B_W = 512, 128, 512
A_HALF_WINDOW = 128
B_PATTERNS = ((128, 1), (512, 4), (2048, 16))
ROPE_THETA = 10000.0
NORM_EPS = 1e-6
FFN_RES_WEIGHT = 0.5
ADAM_LR, ADAM_B1, ADAM_B2, ADAM_EPS, ADAM_WD, ADAM_STEP = 0.001, 0.9, 0.999, 1e-08, 0.01, 10
N_CHIPS = 4
N_DEV = 8
QB = 128
SHORT_SEQ = 512
NEG = -1e30
VMEM_LIMIT = 56 * 1024 * 1024
MESH = pl.DeviceIdType.MESH
ANY = pl.BlockSpec(memory_space=pl.ANY)


def _params(sem=None):
    return pltpu.CompilerParams(dimension_semantics=sem, vmem_limit_bytes=VMEM_LIMIT)


def _sds(shape, dtype):
    return jax.ShapeDtypeStruct(tuple(shape), dtype)


def _dot(a, b):
    return jnp.dot(a, b, preferred_element_type=F32)


def _dot_nt(a, b):
    return lax.dot_general(a, b, (((1,), (1,)), ((), ())), preferred_element_type=F32)


def _dot_tn(a, b):
    return lax.dot_general(a, b, (((0,), (0,)), ((), ())), preferred_element_type=F32)


def _rms_stats(x):
    r = lax.rsqrt(jnp.mean(x * x, axis=-1, keepdims=True) + NORM_EPS)
    return x * r, r


def _rms_bwd(dh, x, g):
    xhat, r = _rms_stats(x)
    dxn = dh * g
    dx = r * (dxn - xhat * jnp.mean(dxn * xhat, axis=-1, keepdims=True))
    tm, d = x.shape
    dg = (dh * xhat).reshape(tm // SUBLANES, SUBLANES, d).sum(axis=0)
    return dx, dg


def _sigmoid(x):
    return 1.0 / (1.0 + jnp.exp(-x))


def _swap32(t):
    n = t.shape[-1]
    lane = lax.broadcasted_iota(jnp.int32, t.shape, t.ndim - 1)
    return jnp.where((lane % HEAD_DIM) < HEAD_DIM // 2, pltpu.roll(t, n - HEAD_DIM // 2, axis=t.ndim - 1),
                     pltpu.roll(t, HEAD_DIM // 2, axis=t.ndim - 1))


def _cast_place(me_arr, w, name):
    R, C = w.shape
    tr = R // 2 if (R // 2) % 16 == 0 else R

    def body(me_ref, w_ref, o_ref):
        o_ref[...] = w_ref[...].astype(BF16)

    grid_spec = pltpu.PrefetchScalarGridSpec(
        num_scalar_prefetch=1, grid=(R // tr,), in_specs=[pl.BlockSpec((tr, C), lambda t, me: (t, 0))],
        out_specs=pl.BlockSpec((None, tr, C), lambda t, me: (me[0], t, 0)))
    return pl.pallas_call(body, name=name, grid_spec=grid_spec, out_shape=_sds((N_CHIPS, R, C), BF16),
                          compiler_params=_params(("parallel",)))(me_arr, w)


HBM = pl.BlockSpec(memory_space=pltpu.HBM)
SEM = pl.BlockSpec(memory_space=pltpu.SEMAPHORE)


def _push_start(name, bufs, ncopies, plan, after):
    nb = len(bufs)

    def body(*refs):
        send, recv, token = refs[nb + 1], refs[nb + 2], refs[-1]
        for i, (src, dst, dev) in enumerate(plan(refs[:nb])):
            pltpu.make_async_remote_copy(src_ref=src, dst_ref=dst, send_sem=send.at[i], recv_sem=recv.at[i],
                                         device_id=dev, device_id_type=MESH).start()
        token[...] = jnp.zeros_like(token)

    outs = pl.pallas_call(
        body, name=name,
        out_shape=(pltpu.SemaphoreType.DMA((ncopies,)), pltpu.SemaphoreType.DMA((ncopies,)), *[pltpu.HBM(b.shape, b.dtype) for b in bufs],
                   _sds((SUBLANES, LANES), F32)),
        in_specs=[HBM] * nb + [ANY], out_specs=(SEM, SEM, *([HBM] * nb), pl.BlockSpec(memory_space=pltpu.VMEM)),
        input_output_aliases={i: 2 + i for i in range(nb)},
        compiler_params=pltpu.CompilerParams(has_side_effects=pltpu.SideEffectType.DATAFLOW_SIDE_EFFECTING),
    )(*[pltpu.with_memory_space_constraint(b, pltpu.HBM) for b in bufs], after)
    return outs[0], outs[1], list(outs[2:2 + nb]), outs[-1]


def _push_wait(name, send, recv, bufs, plan, after):
    nb = len(bufs)

    def body(*refs):
        send_ref, recv_ref = refs[nb], refs[nb + 1]
        for i, (src, dst, dev) in enumerate(plan(refs[:nb])):
            cp = pltpu.make_async_remote_copy(src_ref=src, dst_ref=dst, send_sem=send_ref.at[i], recv_sem=recv_ref.at[i],
                                              device_id=dev, device_id_type=MESH)
            cp.wait_send()
            cp.wait_recv()

    afters = list(after) if isinstance(after, (list, tuple)) else [after]
    outs = pl.pallas_call(
        body, name=name, out_shape=tuple(pltpu.HBM(b.shape, b.dtype) for b in bufs),
        in_specs=[HBM] * nb + [SEM, SEM] + [ANY] * len(afters), out_specs=tuple([HBM] * nb),
        input_output_aliases={i: i for i in range(nb)},
        compiler_params=pltpu.CompilerParams(has_side_effects=pltpu.SideEffectType.DATAFLOW_SIDE_EFFECTING),
    )(*bufs, send, recv, *afters)
    return list(outs)


def _mesh_pos():
    return lax.axis_index("x"), lax.axis_index("y"), lax.axis_index("c")


def _chip_peers(x, y, c):
    return [((1 - x, y, c), 2 * (1 - x) + y), ((x, 1 - y, c), 2 * x + (1 - y)), ((1 - x, 1 - y, c), 2 * (1 - x) + (1 - y))]


def _gather_plan(n):
    def plan(refs):
        x, y, c = _mesh_pos()
        me = 2 * x + y
        return [(refs[k].at[me], refs[k].at[me], dev) for k in range(n) for dev, _ in _chip_peers(x, y, c)]
    return plan


def _rows_of(shape, who, quarter=None):
    r2 = shape[1] // 2
    if quarter is None:
        return pl.ds(pl.multiple_of(who * r2, 16), r2)
    return pl.ds(pl.multiple_of(who * r2 + quarter * (r2 // 2), 16), r2 // 2)


def _neighbour_plan(shapes):
    def plan(refs):
        x, y, c = _mesh_pos()
        me = 2 * x + y
        return [(refs[k].at[me, _rows_of(shp, c), :], refs[k].at[me, _rows_of(shp, c), :], dev)
                for k, shp in enumerate(shapes) for dev in ((1 - x, y, c), (x, 1 - y, c))]
    return plan


def _gather_forward(fulls):
    n = len(fulls)

    def body(*refs):
        ins, outs = refs[:n], refs[n:2 * n]
        ici_send, ici_recv, d2d_send, d2d_recv = refs[2 * n:]
        x, y, c = _mesh_pos()
        cx, cy, cd = 2 * (1 - x) + y, 2 * x + (1 - y), 2 * (1 - x) + (1 - y)
        sibling, x_nbr, y_nbr = (x, y, 1 - c), (1 - x, y, c), (x, 1 - y, c)
        started = []

        def push(src, dst, send, recv, dev):
            cp = pltpu.make_async_remote_copy(src_ref=src, dst_ref=dst, send_sem=send, recv_sem=recv, device_id=dev, device_id_type=MESH)
            cp.start()
            started.append(cp)

        def arrived(blk, send, recv):
            pltpu.make_async_remote_copy(src_ref=blk, dst_ref=blk, send_sem=send, recv_sem=recv, device_id=sibling,
                                         device_id_type=MESH).wait_recv()

        for k in range(n):
            shp = fulls[k].shape
            for j, chip in enumerate((cx, cy)):
                push(ins[k].at[chip, _rows_of(shp, c), :], outs[k].at[chip, _rows_of(shp, c), :],
                     d2d_send.at[3 * k + j], d2d_recv.at[3 * k + j], sibling)
            push(ins[k].at[cx, _rows_of(shp, c, 0), :], outs[k].at[cx, _rows_of(shp, c, 0), :], ici_send.at[2 * k], ici_recv.at[2 * k], y_nbr)
            push(ins[k].at[cy, _rows_of(shp, c, 1), :], outs[k].at[cy, _rows_of(shp, c, 1), :], ici_send.at[2 * k + 1], ici_recv.at[2 * k + 1],
                 x_nbr)
        for k in range(n):
            shp = fulls[k].shape
            for q in (0, 1):
                arrived(outs[k].at[cd, _rows_of(shp, c, q), :], ici_send.at[2 * k + q], ici_recv.at[2 * k + q])
            blk = outs[k].at[cd, _rows_of(shp, c), :]
            push(blk, blk, d2d_send.at[3 * k + 2], d2d_recv.at[3 * k + 2], sibling)
        for k in range(n):
            for j, chip in enumerate((cx, cy, cd)):
                arrived(outs[k].at[chip, _rows_of(fulls[k].shape, 1 - c), :], d2d_send.at[3 * k + j], d2d_recv.at[3 * k + j])
        for cp in started:
            cp.wait_send()

    return pl.pallas_call(
        body, name="gather_forward", out_shape=[_sds(f.shape, BF16) for f in fulls],
        in_specs=[ANY] * n, out_specs=[ANY] * n, input_output_aliases={k: k for k in range(n)},
        scratch_shapes=[pltpu.SemaphoreType.DMA((n * 2,))] * 2 + [pltpu.SemaphoreType.DMA((n * 3,))] * 2,
        compiler_params=_params())(*fulls)


def _resident(shape):
    return pl.BlockSpec(shape, lambda i: (0,) * len(shape), pipeline_mode=pl.Buffered(1))


FFN_FWD_CHUNK = 256
FFN_DX_CHUNK = 512


def _chunks(n, step):
    return [(c0, min(step, n - c0)) for c0 in range(0, n, step)]


def _two_phase(chunks, first, second):
    held = {}
    for ci, ch in enumerate(chunks):
        held[ci] = first(*ch)
        if ci >= 1:
            second(*chunks[ci - 1], held.pop(ci - 1))
    last = len(chunks) - 1
    second(*chunks[last], held.pop(last))


def _ffn_fwd(x, g, wgt, wut, wd, name, tm=512):
    T, D = x.shape
    F = wd.shape[0]

    def body(x_ref, g_ref, wg_ref, wu_ref, wd_ref, xo_ref, h_ref, gate_ref, up_ref, act_ref):
        xv = x_ref[...]
        xhat, _ = _rms_stats(xv)
        h = (xhat * g_ref[...]).astype(BF16)
        h_ref[...] = h
        acc = []

        def first(c0, cw):
            return _dot_nt(h, wg_ref[c0:c0 + cw, :]), _dot_nt(h, wu_ref[c0:c0 + cw, :])

        def second(c0, cw, gate_up):
            gate, up = gate_up
            act = ((gate * _sigmoid(gate)) * up).astype(BF16)
            gate_ref[:, c0:c0 + cw] = gate.astype(BF16)
            up_ref[:, c0:c0 + cw] = up.astype(BF16)
            act_ref[:, c0:c0 + cw] = act
            d = _dot(act, wd_ref[c0:c0 + cw, :])
            acc[:] = [d if not acc else acc[0] + d]

        _two_phase(_chunks(F, FFN_FWD_CHUNK), first, second)
        xo_ref[...] = xv + FFN_RES_WEIGHT * acc[0]

    row = pl.BlockSpec((tm, D), lambda i: (i, 0))
    saved = pl.BlockSpec((tm, F), lambda i: (i, 0))
    return pl.pallas_call(
        body, name=name, grid=(T // tm,),
        in_specs=[row, pl.BlockSpec((1, D), lambda i: (0, 0)), _resident(wgt.shape), _resident(wut.shape), _resident(wd.shape)],
        out_specs=[row, row, saved, saved, saved],
        out_shape=[_sds((T, D), F32), _sds((T, D), BF16), _sds((T, F), BF16), _sds((T, F), BF16), _sds((T, F), BF16)],
        compiler_params=_params(("parallel",)))(x, g, wgt, wut, wd)


def _ffn_dx(dxo, x, g, gate_s, up_s, wgt, wut, wd, name, tm=256):
    T, D = x.shape
    F = wd.shape[0]

    def body(dxo_ref, x_ref, g_ref, gate_ref, up_ref, wg_ref, wu_ref, wd_ref, dx_ref, dff_ref, dgate_ref, dup_ref, dg_ref):
        @pl.when(pl.program_id(0) == 0)
        def _():
            dg_ref[...] = jnp.zeros_like(dg_ref)

        d = (FFN_RES_WEIGHT * dxo_ref[...]).astype(BF16)
        dff_ref[...] = d
        dh = []

        def first(c0, cw):
            return _dot_nt(d, wd_ref[c0:c0 + cw, :])

        def second(c0, cw, da):
            gate = gate_ref[:, c0:c0 + cw].astype(F32)
            up = up_ref[:, c0:c0 + cw].astype(F32)
            s = _sigmoid(gate)
            silu = gate * s
            dup = (da * silu).astype(BF16)
            dgate = (da * up * (s * (1.0 + gate * (1.0 - s)))).astype(BF16)
            dgate_ref[:, c0:c0 + cw] = dgate
            dup_ref[:, c0:c0 + cw] = dup
            t = _dot(dgate, wg_ref[c0:c0 + cw, :]) + _dot(dup, wu_ref[c0:c0 + cw, :])
            dh[:] = [t if not dh else dh[0] + t]

        _two_phase(_chunks(F, FFN_DX_CHUNK), first, second)
        dxn, dg = _rms_bwd(dh[0], x_ref[...], g_ref[...])
        dg_ref[...] += dg
        dx_ref[...] = dxo_ref[...] + dxn

    row = pl.BlockSpec((tm, D), lambda i: (i, 0))
    saved = pl.BlockSpec((tm, F), lambda i: (i, 0))
    return pl.pallas_call(
        body, name=name, grid=(T // tm,),
        in_specs=[row, row, pl.BlockSpec((1, D), lambda i: (0, 0)), saved, saved, _resident(wgt.shape), _resident(wut.shape),
                  _resident(wd.shape)],
        out_specs=[row, row, saved, saved, pl.BlockSpec((SUBLANES, D), lambda i: (0, 0))],
        out_shape=[_sds((T, D), F32), _sds((T, D), BF16), _sds((T, F), BF16), _sds((T, F), BF16), _sds((SUBLANES, D), F32)],
        compiler_params=_params(("arbitrary",)))(dxo, x, g, gate_s, up_s, wgt, wut, wd)


def _tn(a, b, mb, name, tk=2048, dep=None):
    T, M = a.shape
    N = b.shape[1]
    nt = T // tk

    def body(a_ref, b_ref, *refs):
        o_ref, ob_ref = refs[-2:]

        @pl.when(pl.program_id(1) == 0)
        def _():
            o_ref[...] = jnp.zeros_like(o_ref)

        o_ref[...] += _dot_tn(a_ref[...].astype(BF16), b_ref[...].astype(BF16))

        @pl.when(pl.program_id(1) == nt - 1)
        def _():
            ob_ref[...] = o_ref[...].astype(BF16)

    o_spec = pl.BlockSpec((mb, N), lambda g, t: (g, 0))
    return pl.pallas_call(
        body, name=name, grid=(M // mb, nt),
        in_specs=[pl.BlockSpec((tk, mb), lambda g, t: (t, g)), pl.BlockSpec((tk, N), lambda g, t: (t, 0))] + ([ANY] if dep is not None else []),
        out_specs=[o_spec, o_spec], out_shape=[_sds((M, N), F32), _sds((M, N), BF16)],
        compiler_params=_params(("parallel", "arbitrary")))(a, b, *([dep] if dep is not None else []))


def _rope_tables(pos_col, inv_freq):
    T = pos_col.shape[0]

    def body(p_ref, f_ref, c_ref, s_ref):
        ang = p_ref[...].astype(F32) * f_ref[...]
        lane = lax.broadcasted_iota(jnp.int32, ang.shape, 1)
        c_ref[...] = jnp.cos(ang)
        sn = jnp.sin(ang)
        s_ref[...] = jnp.where((lane % HEAD_DIM) < HEAD_DIM // 2, -sn, sn)

    tm = 1024
    return pl.pallas_call(
        body, name="rope_tables", grid=(T // tm,),
        in_specs=[pl.BlockSpec((tm, 1), lambda i: (i, 0)), pl.BlockSpec((1, LANES), lambda i: (0, 0))],
        out_specs=[pl.BlockSpec((tm, LANES), lambda i: (i, 0))] * 2,
        out_shape=[_sds((T, LANES), F32)] * 2, compiler_params=_params(("parallel",)))(pos_col, inv_freq)


def _deinterleave(scr, out_ref, d, tm, nblk):
    for r in range(d):
        for cb in range(nblk):
            out_ref[r, :, cb * LANES:(cb + 1) * LANES] = scr[cb, pl.ds(r, tm // d, stride=d), :].astype(out_ref.dtype)


def _interleave(in_ref, scr, d, tm, nblk):
    for r in range(d):
        for cb in range(nblk):
            scr[cb, pl.ds(r, tm // d, stride=d), :] = in_ref[r, :, cb * LANES:(cb + 1) * LANES].astype(F32)


def _proj_rope(x, g, w_in, cos, sin, tm=512):
    T, D = x.shape
    dils = [d for _, d in B_PATTERNS if d > 1]
    nbb = B_W // LANES
    scale = HEAD_DIM ** -0.5
    cuts = [0, A_Q_W, A_Q_W + A_KV_W, A_Q_W + 2 * A_KV_W, A_Q_W + 2 * A_KV_W + B_W, A_Q_W + 2 * A_KV_W + 2 * B_W,
            A_Q_W + 2 * A_KV_W + 3 * B_W]

    def body(x_ref, g_ref, w_ref, c_ref, s_ref, h_ref, aq_ref, ak_ref, av_ref, *rest):
        b_refs, scr = rest[:-1], rest[-1]
        xhat, _ = _rms_stats(x_ref[...])
        h = (xhat * g_ref[...]).astype(BF16)
        h_ref[...] = h
        cs, sn = c_ref[...], s_ref[...]

        def project(idx, ref, rope, mult, which):
            return _dot_nt(h, w_ref[cuts[idx]:cuts[idx + 1], :])

        def finish(idx, ref, rope, mult, which, whole):
            for cb in range((cuts[idx + 1] - cuts[idx]) // LANES):
                p = whole[:, cb * LANES:(cb + 1) * LANES]
                if rope:
                    p = p * cs + _swap32(p) * sn
                if mult != 1.0:
                    p = p * mult
                ref[:, cb * LANES:(cb + 1) * LANES] = p.astype(BF16)
                if which is not None:
                    scr[which, cb] = p
            if which is not None:
                for di, d in enumerate(dils):
                    _deinterleave(scr.at[which], b_refs[3 * (di + 1) + which], d, tm, nbb)

        _two_phase([(0, aq_ref, True, scale, None), (1, ak_ref, True, 1.0, None), (2, av_ref, False, 1.0, None),
                    (3, b_refs[0], True, scale, 0), (4, b_refs[1], True, 1.0, 1), (5, b_refs[2], False, 1.0, 2)], project, finish)

    row = lambda w: pl.BlockSpec((tm, w), lambda i: (i, 0))
    out_specs = [row(D), row(A_Q_W), row(A_KV_W), row(A_KV_W)] + [row(B_W)] * 3
    out_shape = [_sds((T, D), BF16), _sds((T, A_Q_W), BF16), _sds((T, A_KV_W), BF16), _sds((T, A_KV_W), BF16)] + [_sds((T, B_W), BF16)] * 3
    for d in dils:
        out_specs += [pl.BlockSpec((d, tm // d, B_W), lambda i: (0, i, 0))] * 3
        out_shape += [_sds((d, T // d, B_W), BF16)] * 3
    return pl.pallas_call(
        body, name="proj_rope", grid=(T // tm,),
        in_specs=[row(D), pl.BlockSpec((1, D), lambda i: (0, 0)), pl.BlockSpec(w_in.shape, lambda i: (0, 0)), row(LANES), row(LANES)],
        out_specs=out_specs, out_shape=out_shape, scratch_shapes=[pltpu.VMEM((3, nbb, tm, LANES), F32)],
        compiler_params=_params(("parallel",)))(x, g, w_in, cos, sin)


def _band_bias(rel, qb, kw, hw):
    ri = lax.broadcasted_iota(jnp.int32, (2 * qb, kw), 0) & (qb - 1)
    ci = lax.broadcasted_iota(jnp.int32, (2 * qb, kw), 1)
    return jnp.where(jnp.abs(ri + rel - ci) <= hw, 0.0, NEG).astype(F32)


def _stack_heads(x, lo):
    z = jnp.zeros_like(x)
    return jnp.concatenate([jnp.where(lo, x, z), jnp.where(lo, z, x)], axis=0)


def _unstack_heads(y, lo):
    qb = y.shape[0] // 2
    return jnp.where(lo, y[:qb], y[qb:])


def _band_setup(bias_scr, qb, kw, hw):
    if bias_scr is not None:
        for i in range(3):
            bias_scr[i] = _band_bias(i * hw, qb, kw, hw)


def _band_window(bias_scr, qs, L, qb, kw, hw):
    ws = pl.multiple_of(jnp.clip(qs - hw, 0, L - kw), 64)
    if bias_scr is None:
        return ws, _band_bias(qs - ws, qb, kw, hw)
    return ws, bias_scr[lax.shift_right_logical(qs - ws, hw.bit_length() - 1)]


def _dup_kv_head(src_ref, dst_ref, head, L):
    step = min(L, 1024)
    for r0 in range(0, L, step):
        xf = src_ref[r0:r0 + step, :].astype(F32)
        lane = lax.broadcasted_iota(jnp.int32, xf.shape, 1)
        keep = jnp.logical_xor(lane < HEAD_DIM, head == 1)
        dst_ref[r0:r0 + step, :] = jnp.where(keep, xf, pltpu.roll(xf, HEAD_DIM, axis=1)).astype(dst_ref.dtype)


def _attn_fwd(q, k, v, sink, hw, gqa, out_dtype, name, qb=QB, blocks_per_step=8, out_cols=None):
    NB, L, Cq = q.shape
    Ls = min(L, 2048)
    kw = min(qb + 2 * hw, L)
    tables = L >= qb + 2 * hw
    unroll = min(blocks_per_step, Ls // qb)
    nlb = 1 if (gqa or L > SHORT_SEQ) else Cq // LANES

    def body(sink_ref, q_ref, k_ref, v_ref, o_ref, lse_ref, *scr):
        b, s_idx = pl.program_id(1), pl.program_id(2)
        bias_scr = scr[0] if tables else None
        _band_setup(bias_scr, qb, kw, hw)
        if gqa:
            kd, vd = scr[-2:]

            @pl.when(s_idx == 0)
            def _():
                _dup_kv_head(k_ref, kd, b // 2, L)
                _dup_kv_head(v_ref, vd, b // 2, L)
        else:
            kd, vd = k_ref, v_ref
        lane = lax.broadcasted_iota(jnp.int32, (qb, LANES), 1)
        lo = lane < HEAD_DIM
        if gqa:
            row = lax.broadcasted_iota(jnp.int32, (2 * qb, 1), 0)
            sk = jnp.where(row < qb, sink_ref[2 * b], sink_ref[2 * b + 1])

        def block(ql, col):
            qs = s_idx * Ls + ql
            ws, bias = _band_window(bias_scr, qs, L, qb, kw, hw)
            return ws, _dot_nt(_stack_heads(q_ref[pl.ds(ql, qb), col], lo), kd[pl.ds(ws, kw), col]) + bias

        def finish(ql, col, scores):
            ws, s = scores
            m = jnp.max(s, axis=-1, keepdims=True)
            if gqa:
                m = jnp.maximum(m, sk)
            p = jnp.exp(s - m)
            den = jnp.sum(p, axis=-1, keepdims=True)
            if gqa:
                den = den + jnp.exp(sk - m)
            o = _dot(p.astype(BF16), vd[pl.ds(ws, kw), col]) * (1.0 / den)
            o_ref[pl.ds(ql, qb), col] = _unstack_heads(o, lo).astype(o_ref.dtype)
            lse_ref[pl.ds(ql, qb), col] = _unstack_heads(m + jnp.log(den), lo)

        for lb in range(nlb):
            def step(n, carry, col=slice(lb * LANES, (lb + 1) * LANES)):
                _two_phase([(pl.multiple_of((n * unroll + u) * qb, qb), col) for u in range(unroll)], block, finish)
                return carry

            lax.fori_loop(0, Ls // (qb * unroll), step, 0)

    kv_map = (lambda r, b, s: (r, 0, 0)) if gqa else (lambda r, b, s: (r, 0, b))
    seg = pl.BlockSpec((None, Ls, nlb * LANES), lambda r, b, s: (r, s, b))
    return pl.pallas_call(
        body, name=name, grid=(NB, Cq // (nlb * LANES), L // Ls),
        in_specs=[pl.BlockSpec(memory_space=pltpu.SMEM), seg, pl.BlockSpec((None, L, nlb * LANES), kv_map),
                  pl.BlockSpec((None, L, nlb * LANES), kv_map)],
        out_specs=[seg, seg], out_shape=[_sds((NB, L, out_cols or Cq), out_dtype), _sds((NB, L, Cq), F32)],
        scratch_shapes=([pltpu.VMEM((3, 2 * qb, kw), F32)] if tables else []) + ([pltpu.VMEM((L, LANES), BF16)] * 2 if gqa else []),
        compiler_params=_params(("parallel", "parallel", "arbitrary")))(sink, q, k, v)


def _attn_bwd(q, k, v, do, lse, delta, sink, hw, gqa, name, qb=QB, blocks_per_step=8):
    NB, L, Cq = q.shape
    Ck = k.shape[2]
    Ls = min(L, 2048)
    kw = min(qb + 2 * hw, L)
    reps = kw // LANES
    nseg = L // Ls
    scale = HEAD_DIM ** -0.5
    tables = L >= qb + 2 * hw
    unroll = min(blocks_per_step, Ls // qb)
    nlb = 1 if (gqa or L > SHORT_SEQ) else Cq // LANES

    def body(sink_ref, q_ref, do_ref, lse_ref, dl_ref, k_ref, v_ref, dq_ref, dk_ref, dv_ref, dsk_ref, *scr):
        b, s_idx = pl.program_id(1), pl.program_id(2)
        lane = lax.broadcasted_iota(jnp.int32, (qb, LANES), 1)
        lo = lane < HEAD_DIM
        bias_scr = scr[0] if tables else None
        _band_setup(bias_scr, qb, kw, hw)
        if gqa:
            kd, vd, dk_acc, dv_acc, dsk_acc = scr[-5:]

            @pl.when(s_idx == 0)
            def _():
                _dup_kv_head(k_ref, kd, b // 2, L)
                _dup_kv_head(v_ref, vd, b // 2, L)
                dk_acc[...] = jnp.zeros_like(dk_acc)
                dv_acc[...] = jnp.zeros_like(dv_acc)
                dsk_acc[...] = jnp.zeros_like(dsk_acc)

            @pl.when((s_idx == 0) & (b == 0))
            def _():
                dk_ref[...] = jnp.zeros_like(dk_ref)
                dv_ref[...] = jnp.zeros_like(dv_ref)
        else:
            kd, vd = k_ref, v_ref
            dk_acc, dv_acc = scr[-2:]

            @pl.when(s_idx == 0)
            def _():
                dk_acc[...] = jnp.zeros_like(dk_acc)
                dv_acc[...] = jnp.zeros_like(dv_acc)

        def block(ql, col):
            qs = s_idx * Ls + ql
            ws, bias = _band_window(bias_scr, qs, L, qb, kw, hw)
            qv, dov = q_ref[pl.ds(ql, qb), col], do_ref[pl.ds(ql, qb), col]
            lse, dl = lse_ref[pl.ds(ql, qb), col], dl_ref[pl.ds(ql, qb), col]
            kv_, vv = kd[pl.ds(ws, kw), col], vd[pl.ds(ws, kw), col]
            q2, do2 = _stack_heads(qv, lo), _stack_heads(dov, lo)
            return ws, q2, do2, lse, dl, _dot_nt(q2, kv_) + bias, _dot_nt(do2, vv)

        def finish(ql, col, held):
            ws, q2, do2, lse, dl, s, dp = held
            lse_sw, dl_sw = pltpu.roll(lse, HEAD_DIM, axis=1), pltpu.roll(dl, HEAD_DIM, axis=1)
            lse2 = jnp.concatenate([jnp.where(lo, lse, lse_sw), jnp.where(lo, lse_sw, lse)], axis=0)
            dl2 = jnp.concatenate([jnp.where(lo, dl, dl_sw), jnp.where(lo, dl_sw, dl)], axis=0)
            p = jnp.exp(s - jnp.tile(lse2, (1, reps)))
            ds = (p * (dp - jnp.tile(dl2, (1, reps)))).astype(BF16)
            dq_ref[pl.ds(ql, qb), col] = (_unstack_heads(_dot(ds, kd[pl.ds(ws, kw), col]), lo) * scale).astype(dq_ref.dtype)
            both = _dot_tn(jnp.concatenate([ds, p.astype(BF16)], axis=1), jnp.concatenate([q2, do2], axis=1))
            dk_acc[pl.ds(ws, kw), col] += both[:kw, :LANES]
            dv_acc[pl.ds(ws, kw), col] += both[kw:, LANES:]
            if gqa:
                sk = jnp.where(lo, sink_ref[2 * b], sink_ref[2 * b + 1])
                dsk_acc[...] += -jnp.exp(sk - lse) * dl

        for lb in range(nlb):
            def step(n, carry, col=slice(lb * LANES, (lb + 1) * LANES)):
                _two_phase([(pl.multiple_of((n * unroll + u) * qb, qb), col) for u in range(unroll)], block, finish)
                return carry

            lax.fori_loop(0, Ls // (qb * unroll), step, 0)

        if gqa:
            @pl.when(s_idx == nseg - 1)
            def _():
                step_rows = min(L, 1024)
                for r0 in range(0, L, step_rows):
                    lanek = lax.broadcasted_iota(jnp.int32, (step_rows, LANES), 1)
                    mine = jnp.logical_xor(lanek < HEAD_DIM, (b // 2) == 1)
                    for acc, ref in ((dk_acc, dk_ref), (dv_acc, dv_ref)):
                        a = acc[r0:r0 + step_rows, :]
                        ref[r0:r0 + step_rows, :] += jnp.where(mine, a + pltpu.roll(a, HEAD_DIM, axis=1), 0.0)
                dsk_ref[...] = dsk_acc[...].reshape(qb // SUBLANES, SUBLANES, LANES).sum(axis=0)
        else:
            dsk_ref[...] = jnp.zeros_like(dsk_ref)

            @pl.when(s_idx == nseg - 1)
            def _():
                dk_ref[...] = dk_acc[...].astype(dk_ref.dtype)
                dv_ref[...] = dv_acc[...].astype(dv_ref.dtype)

    kv_map = (lambda r, b, s: (r, 0, 0)) if gqa else (lambda r, b, s: (r, 0, b))
    seg = pl.BlockSpec((None, Ls, nlb * LANES), lambda r, b, s: (r, s, b))
    full = pl.BlockSpec((None, L, nlb * LANES), kv_map)
    scratch = [pltpu.VMEM((3, 2 * qb, kw), F32)] if tables else []
    if gqa:
        scratch += [pltpu.VMEM((L, LANES), BF16)] * 2 + [pltpu.VMEM((L, LANES), F32)] * 2 + [pltpu.VMEM((qb, LANES), F32)]
    else:
        scratch += [pltpu.VMEM((L, nlb * LANES), F32)] * 2
    kv_dtype = F32 if gqa else BF16
    return pl.pallas_call(
        body, name=name, grid=(NB, Cq // (nlb * LANES), nseg),
        in_specs=[pl.BlockSpec(memory_space=pltpu.SMEM), seg, seg, seg, seg, full, full],
        out_specs=[seg, full, full, pl.BlockSpec((None, None, SUBLANES, LANES), lambda r, b, s: (r, b, 0, 0))],
        out_shape=[_sds((NB, L, Cq), BF16), _sds((NB, L, Ck), kv_dtype), _sds((NB, L, Ck), kv_dtype),
                   _sds((NB, Cq // LANES, SUBLANES, LANES), F32)],
        scratch_shapes=scratch,
        compiler_params=_params(("arbitrary", "arbitrary", "arbitrary")))(sink, q, do, lse, delta, k, v)


def _dilated_fwd(cat, qkv, hw, tile=2048):
    T = cat.shape[0]
    dils = sorted(qkv)
    nbb, na = B_W // LANES, A_Q_W // LANES
    qb, kw = QB, QB + 2 * hw
    rows_merge = 256
    assert T % tile == 0 and all(tile % (d * qb) == 0 and T // d >= kw for d in dils)

    def body(cat_in, *refs):
        qkv_refs = {d: refs[3 * j:3 * j + 3] for j, d in enumerate(dils)}
        cat_ref, lg_refs = refs[3 * len(dils)], refs[3 * len(dils) + 1:4 * len(dils) + 1]
        o_scr, l_scr, bias_scr = refs[4 * len(dils) + 1:]
        i = pl.program_id(1)
        _band_setup(bias_scr, qb, kw, hw)
        lane = lax.broadcasted_iota(jnp.int32, (qb, LANES), 1)
        lo = lane < HEAD_DIM
        for pi, d in enumerate(dils):
            q_ref, k_ref, v_ref = qkv_refs[d]
            L, rows = T // d, tile // d

            def place(r, n, d=d):
                return pl.ds(r + d * n * qb, qb, stride=d) if d > 1 else pl.ds(n * qb, qb)

            def scores(r, n, q_ref=q_ref, k_ref=k_ref, L=L, rows=rows):
                ws, bias = _band_window(bias_scr, i * rows + n * qb, L, qb, kw, hw)
                return ws, _dot_nt(_stack_heads(q_ref[r, n * qb:(n + 1) * qb, :], lo), k_ref[r, pl.ds(ws, kw), :]) + bias

            def finish(r, n, held, v_ref=v_ref, pi=pi, place=place):
                ws, s = held
                m = jnp.max(s, axis=-1, keepdims=True)
                p = jnp.exp(s - m)
                den = jnp.sum(p, axis=-1, keepdims=True)
                o = _dot(p.astype(BF16), v_ref[r, pl.ds(ws, kw), :]) * (1.0 / den)
                o_scr[pi, place(r, n), :] = _unstack_heads(o, lo)
                l_scr[pi, place(r, n), :] = _unstack_heads(m + jnp.log(den), lo)

            blocks = [(r, n) for r in range(d) for n in range(rows // qb)]
            for g0 in range(0, len(blocks), 8):
                _two_phase(blocks[g0:g0 + 8], scores, finish)

        for r0 in range(0, tile, rows_merge):
            rs = slice(r0, r0 + rows_merge)
            ls_ = [l_scr[pi, rs, :] for pi in range(len(dils))]
            m = ls_[0]
            for l in ls_[1:]:
                m = jnp.maximum(m, l)
            es = [jnp.exp(l - m) for l in ls_]
            den, out = es[0], es[0] * o_scr[0, rs, :]
            for pi in range(1, len(dils)):
                den = den + es[pi]
                out = out + es[pi] * o_scr[pi, rs, :]
            cat_ref[rs, :] = (out * (1.0 / den)).astype(BF16)
            l_scr[0, rs, :] = m + jnp.log(den)
        for lg_ref, d in zip(lg_refs, dils):
            for r in range(d):
                lg_ref[r] = l_scr[0, pl.ds(r, tile // d, stride=d), :] if d > 1 else l_scr[0]

    in_specs = [pl.BlockSpec(memory_space=pl.ANY)]
    operands = [cat]
    for d in dils:
        in_specs += [pl.BlockSpec((d, tile // d, LANES), lambda b, i: (0, i, b))] + [pl.BlockSpec((d, T // d, LANES), lambda b, i: (0, 0, b))] * 2
        operands += list(qkv[d])
    return pl.pallas_call(
        body, name="dilated_fwd", grid=(nbb, T // tile), in_specs=in_specs,
        out_specs=[pl.BlockSpec((tile, LANES), lambda b, i: (i, na + b))] + [pl.BlockSpec((d, tile // d, LANES), lambda b, i: (0, i, b)) for d in dils],
        out_shape=[_sds(cat.shape, BF16)] + [_sds((d, T // d, B_W), F32) for d in dils],
        input_output_aliases={0: 0},
        scratch_shapes=[pltpu.VMEM((len(dils), tile, LANES), F32)] * 2 + [pltpu.VMEM((3, 2 * qb, kw), F32)],
        compiler_params=_params(("parallel", "arbitrary")))(*operands)


def _out_proj(x, cat, w_out, tm=512):
    T, D = x.shape

    def body(x_ref, c_ref, w_ref, o_ref):
        o_ref[...] = x_ref[...] + _dot(c_ref[...], w_ref[...])

    row = lambda w: pl.BlockSpec((tm, w), lambda i: (i, 0))
    return pl.pallas_call(
        body, name="out_proj", grid=(T // tm,), in_specs=[row(D), row(cat.shape[1]), pl.BlockSpec(w_out.shape, lambda i: (0, 0))],
        out_specs=row(D), out_shape=_sds((T, D), F32), compiler_params=_params(("parallel",)))(x, cat, w_out)


def _final_loss(x, g, target, tm=512):
    T, D = x.shape

    def body(x_ref, g_ref, t_ref, dx_ref, dg_ref, loss_ref):
        @pl.when(pl.program_id(0) == 0)
        def _():
            dg_ref[...] = jnp.zeros_like(dg_ref)
            loss_ref[...] = jnp.zeros_like(loss_ref)

        xv, gv = x_ref[...], g_ref[...]
        xhat, _ = _rms_stats(xv)
        err = xhat * gv - t_ref[...]
        loss_ref[...] += 0.5 * jnp.sum(jnp.sum(err * err, axis=-1, keepdims=True) * (1.0 / D), axis=0, keepdims=True)
        dx, dg = _rms_bwd(err * (1.0 / D), xv, gv)
        dx_ref[...] = dx
        dg_ref[...] += dg

    row = pl.BlockSpec((tm, D), lambda i: (i, 0))
    return pl.pallas_call(
        body, name="final_loss", grid=(T // tm,), in_specs=[row, pl.BlockSpec((1, D), lambda i: (0, 0)), row],
        out_specs=[row, pl.BlockSpec((SUBLANES, D), lambda i: (0, 0)), pl.BlockSpec((SUBLANES, LANES), lambda i: (0, 0))],
        out_shape=[_sds((T, D), F32), _sds((SUBLANES, D), F32), _sds((SUBLANES, LANES), F32)],
        compiler_params=_params(("arbitrary",)))(x, g, target)


def _dcat(dx, w_out, cat, tm=512):
    T, D = dx.shape
    C = cat.shape[1]
    nba, nbb = A_Q_W // LANES, B_W // LANES

    def body(dx_ref, w_ref, cat_ref, doa_ref, dla_ref, dob1_ref, dlb1_ref, dob4_ref, dlb4_ref, dob16_ref, dlb16_ref, sdo, sdl):
        dc = _dot_nt(dx_ref[...].astype(BF16), w_ref[...])
        ri = lax.broadcasted_iota(jnp.int32, (LANES, LANES), 0)
        ci = lax.broadcasted_iota(jnp.int32, (LANES, LANES), 1)
        same_head = ((ri // HEAD_DIM) == (ci // HEAD_DIM)).astype(BF16)
        for cb in range(C // LANES):
            cols = slice(cb * LANES, (cb + 1) * LANES)
            blk = dc[:, cols]
            prod = blk * cat_ref[:, cols].astype(F32)
            hi = prod.astype(BF16)
            lo_ = (prod - hi.astype(F32)).astype(BF16)
            dl = _dot(hi, same_head) + _dot(lo_, same_head)
            if cb < nba:
                doa_ref[:, cols] = blk.astype(BF16)
                dla_ref[:, cols] = dl
            else:
                bcols = slice((cb - nba) * LANES, (cb - nba + 1) * LANES)
                dob1_ref[:, bcols] = blk.astype(BF16)
                dlb1_ref[:, bcols] = dl
                sdo[cb - nba] = blk
                sdl[cb - nba] = dl
        _deinterleave(sdo, dob4_ref, 4, tm, nbb)
        _deinterleave(sdl, dlb4_ref, 4, tm, nbb)
        _deinterleave(sdo, dob16_ref, 16, tm, nbb)
        _deinterleave(sdl, dlb16_ref, 16, tm, nbb)

    row = lambda w: pl.BlockSpec((tm, w), lambda i: (i, 0))
    perm = lambda d: pl.BlockSpec((d, tm // d, B_W), lambda i: (0, i, 0))
    return pl.pallas_call(
        body, name="dcat", grid=(T // tm,), in_specs=[row(D), pl.BlockSpec(w_out.shape, lambda i: (0, 0)), row(C)],
        out_specs=[row(A_Q_W), row(A_Q_W), row(B_W), row(B_W), perm(4), perm(4), perm(16), perm(16)],
        out_shape=[_sds((T, A_Q_W), BF16), _sds((T, A_Q_W), F32), _sds((T, B_W), BF16), _sds((T, B_W), F32),
                   _sds((4, T // 4, B_W), BF16), _sds((4, T // 4, B_W), F32), _sds((16, T // 16, B_W), BF16), _sds((16, T // 16, B_W), F32)],
        scratch_shapes=[pltpu.VMEM((nbb, tm, LANES), F32)] * 2, compiler_params=_params(("parallel",)))(dx, w_out, cat)


def _rope_bwd_assemble(dqa, dka, dva, b1, b4, b16, cos, sin, tm=512):
    T = dqa.shape[0]
    nbb = B_W // LANES
    width = A_Q_W + 2 * A_KV_W + 3 * B_W

    def body(dqa_ref, dka_ref, dva_ref, q1, k1, v1, q4, k4, v4, q16, k16, v16, c_ref, s_ref, o_ref, scr):
        cs, sn = c_ref[...], s_ref[...]

        def unrope(t):
            return t * cs + _swap32(t * sn)

        col = 0
        for ref, rope in ((dqa_ref, True), (dka_ref, True), (dva_ref, False)):
            for cb in range(ref.shape[1] // LANES):
                t = ref[:, cb * LANES:(cb + 1) * LANES].astype(F32)
                o_ref[:, col:col + LANES] = (unrope(t) if rope else t).astype(BF16)
                col += LANES
        for which, (r1, r4, r16, rope) in enumerate(((q1, q4, q16, True), (k1, k4, k16, True), (v1, v4, v16, False))):
            _interleave(r4, scr.at[0], 4, tm, nbb)
            _interleave(r16, scr.at[1], 16, tm, nbb)
            for cb in range(nbb):
                t = r1[:, cb * LANES:(cb + 1) * LANES].astype(F32) + scr[0, cb] + scr[1, cb]
                o_ref[:, col:col + LANES] = (unrope(t) if rope else t).astype(BF16)
                col += LANES

    row = lambda w: pl.BlockSpec((tm, w), lambda i: (i, 0))
    perm = lambda d: pl.BlockSpec((d, tm // d, B_W), lambda i: (0, i, 0))
    return pl.pallas_call(
        body, name="rope_bwd", grid=(T // tm,),
        in_specs=[row(A_Q_W), row(A_KV_W), row(A_KV_W)] + [row(B_W)] * 3 + [perm(4)] * 3 + [perm(16)] * 3 + [row(LANES), row(LANES)],
        out_specs=row(width), out_shape=_sds((T, width), BF16), scratch_shapes=[pltpu.VMEM((2, nbb, tm, LANES), F32)],
        compiler_params=_params(("parallel",)))(dqa, dka, dva, *b1, *b4, *b16, cos, sin)


def _dh_norm(dproj, w_in, x, g, dres, tm=512):
    T, D = x.shape

    def body(dp_ref, w_ref, x_ref, g_ref, dr_ref, dx_ref, dg_ref):
        @pl.when(pl.program_id(0) == 0)
        def _():
            dg_ref[...] = jnp.zeros_like(dg_ref)

        dxn, dg = _rms_bwd(_dot(dp_ref[...], w_ref[...]), x_ref[...], g_ref[...])
        dg_ref[...] += dg
        dx_ref[...] = dr_ref[...] + dxn

    row = lambda w: pl.BlockSpec((tm, w), lambda i: (i, 0))
    return pl.pallas_call(
        body, name="dh_norm", grid=(T // tm,),
        in_specs=[row(dproj.shape[1]), pl.BlockSpec(w_in.shape, lambda i: (0, 0)), row(D), pl.BlockSpec((1, D), lambda i: (0, 0)), row(D)],
        out_specs=[row(D), pl.BlockSpec((SUBLANES, D), lambda i: (0, 0))],
        out_shape=[_sds((T, D), F32), _sds((SUBLANES, D), F32)], compiler_params=_params(("arbitrary",)))(dproj, w_in, x, g, dres)


def _grad_push_plan(n):
    def plan(refs):
        x, y, c = _mesh_pos()
        return [(refs[k].at[chip], refs[n + k].at[rel], dev) for k in range(n) for rel, (dev, chip) in enumerate(_chip_peers(x, y, c))]
    return plan


def _sum_own(me_arr, g, landed, name):
    ns, R, C = g.shape
    tr = R // 2 if (R // 2) % 16 == 0 else R

    def body(me_ref, g_ref, x_ref, o_ref):
        acc = g_ref[...]
        for rel in range(ns - 1):
            acc = acc + x_ref[rel].astype(F32)
        o_ref[...] = acc

    grid_spec = pltpu.PrefetchScalarGridSpec(
        num_scalar_prefetch=1, grid=(R // tr,),
        in_specs=[pl.BlockSpec((None, tr, C), lambda t, me: (me[0], t, 0)), pl.BlockSpec((ns - 1, tr, C), lambda t, me: (0, t, 0))],
        out_specs=pl.BlockSpec((tr, C), lambda t, me: (t, 0)))
    return pl.pallas_call(body, name=name, grid_spec=grid_spec, out_shape=_sds((R, C), F32),
                          compiler_params=_params(("parallel",)))(me_arr, g, landed)


def _swap_plan(n):
    def plan(refs):
        x, y, c = _mesh_pos()
        return [(refs[k], refs[n + k], (x, y, 1 - c)) for k in range(n)]
    return plan


def _allreduce_small(v):
    rows, W = v.shape

    def body(v_ref, o_ref, buf, send, recv):
        x, y, c = _mesh_pos()
        me = 4 * x + 2 * y + c
        cps = []
        for m in range(1, N_DEV):
            dev = (x ^ (m >> 2), y ^ ((m >> 1) & 1), c ^ (m & 1))
            cp = pltpu.make_async_remote_copy(src_ref=v_ref, dst_ref=buf.at[me], send_sem=send.at[m - 1], recv_sem=recv.at[m - 1],
                                              device_id=dev, device_id_type=MESH)
            cp.start()
            cps.append(cp)
        for m in range(1, N_DEV):
            pltpu.make_async_remote_copy(src_ref=v_ref, dst_ref=buf.at[me ^ m], send_sem=send.at[m - 1], recv_sem=recv.at[m - 1],
                                         device_id=(x, y, c), device_id_type=MESH).wait_recv()
        for cp in cps:
            cp.wait_send()
        buf[me] = v_ref[...]
        acc = buf[0]
        for i in range(1, N_DEV):
            acc = acc + buf[i]
        o_ref[...] = acc

    return pl.pallas_call(
        body, name="allreduce_small", out_shape=_sds((rows, W), F32),
        scratch_shapes=[pltpu.VMEM((N_DEV, rows, W), F32), pltpu.SemaphoreType.DMA((N_DEV - 1,)), pltpu.SemaphoreType.DMA((N_DEV - 1,))],
        compiler_params=_params())(v)


def _adamw_math(w, g, m, v):
    c1 = 1.0 / (1.0 - ADAM_B1 ** ADAM_STEP)
    c2 = 1.0 / (1.0 - ADAM_B2 ** ADAM_STEP)
    nm = ADAM_B1 * m + (1.0 - ADAM_B1) * g
    nv = ADAM_B2 * v + (1.0 - ADAM_B2) * (g * g)
    return -ADAM_LR * ((nm * c1) / (jnp.sqrt(nv * c2) + ADAM_EPS) + ADAM_WD * w), nm, nv


def _adamw_small(rows, sink_g, params):
    n = len(params)

    def body(rows_ref, sink_ref, *refs):
        ins, outs = refs[:3 * n], refs[3 * n:]
        for j in range(n):
            g = sink_ref[...] if j == n - 1 else rows_ref[j:j + 1, :]
            d, nm, nv = _adamw_math(ins[3 * j][...], g, ins[3 * j + 1][...], ins[3 * j + 2][...])
            for ref, val in zip(outs[4 * j:4 * j + 4], (g, d, nm, nv)):
                ref[...] = val

    flat = [a for p in params for a in p]
    outs = pl.pallas_call(body, name="adamw_small", out_shape=[_sds(p[0].shape, F32) for p in params for _ in range(4)],
                          compiler_params=_params())(rows, sink_g, *flat)
    return [outs[4 * j:4 * j + 4] for j in range(n)]


def _adamw(w, gp, gq, m, v, name):
    R, C = w.shape
    tr = R // 2 if (R // 2) % SUBLANES == 0 else R

    def body(w_ref, gp_ref, gq_ref, m_ref, v_ref, g_ref, d_ref, nm_ref, nv_ref):
        gv = gp_ref[...] + gq_ref[...]
        g_ref[...] = gv
        d_ref[...], nm_ref[...], nv_ref[...] = _adamw_math(w_ref[...], gv, m_ref[...], v_ref[...])

    blk = pl.BlockSpec((tr, C), lambda t: (t, 0))
    return pl.pallas_call(body, name=name, grid=(R // tr,), in_specs=[blk] * 5, out_specs=[blk] * 4,
                          out_shape=[_sds((R, C), F32)] * 4, compiler_params=_params(("parallel",)))(w, gp, gq, m, v)


def _rope(positions, after):
    inv_freq = 1.0 / (ROPE_THETA ** (jnp.arange(0, HEAD_DIM, 2, dtype=F32) / HEAD_DIM))
    inv_freq = jnp.tile(inv_freq, LANES // (HEAD_DIM // 2)).reshape(1, LANES) + after[0, 0]
    return _rope_tables(positions.reshape(-1, 1), inv_freq)


def _local_step(x, rope, target, norms, a_sink, comm):
    T, D = x.shape
    g1, gm, g2, gf = norms
    cos, sin = rope
    no_sink = jnp.zeros((2 * (B_W // LANES),), F32)
    W = {k: comm.weight(k, x) for k in ("wg1", "wu1", "wd1")}

    x1, h1, gate1, up1, act1 = _ffn_fwd(x, comm.order(g1), W["wg1"], W["wu1"], W["wd1"], "ffn1_fwd")
    W["w_in"] = comm.weight("w_in", x1)
    (h2, aq, ak, av, bq1, bk1, bv1, bq4, bk4, bv4, bq16, bk16, bv16) = _proj_rope(x1, gm, W["w_in"], cos, sin)
    cat, a_lse = _attn_fwd(aq[None], ak[None], av[None], a_sink, A_HALF_WINDOW, True, BF16, "attn_a_fwd", qb=2 * QB, blocks_per_step=4,
                           out_cols=A_Q_W + B_W)
    bqs = {1: (bq1[None], bk1[None], bv1[None]), 4: (bq4, bk4, bv4), 16: (bq16, bk16, bv16)}
    (b_hw,) = {w // (2 * d) for w, d in B_PATTERNS}
    cat, lg1, lg4, lg16 = _dilated_fwd(cat[0], bqs, b_hw)
    lg1 = lg1[0]
    W["w_out"] = comm.weight("w_out", cat)
    x2 = _out_proj(x1, cat, W["w_out"])
    for k in ("wg2", "wu2", "wd2"):
        W[k] = comm.weight(k, x2)
    x3, h3, gate2, up2, act2 = _ffn_fwd(x2, g2, W["wg2"], W["wu2"], W["wd2"], "ffn2_fwd")

    dx3, dgf, loss8 = _final_loss(x3, gf, target)
    dx2, dff2, dgate2, dup2, dg2 = _ffn_dx(dx3, x2, g2, gate2, up2, W["wg2"], W["wu2"], W["wd2"], "ffn2_dx")
    fb = gate2.shape[1] // 2
    dwg2 = _tn(dgate2, h3, fb, "ffn2_dw_gate")
    dwu2 = _tn(dup2, h3, fb, "ffn2_dw_up")
    dwd2 = _tn(act2, dff2, fb, "ffn2_dw_down")
    comm.ready(dict(wg2=dwg2, wu2=dwu2, wd2=dwd2), dwd2[0])

    doa, dla, dob1, dlb1, dob4, dlb4, dob16, dlb16 = _dcat(dx2, W["w_out"], cat)
    dw_out = _tn(cat, dx2, cat.shape[1], "w_out_dw", dep=comm.dep())
    dqa, dka, dva, dsk = _attn_bwd(aq[None], ak[None], av[None], doa[None], a_lse, dla[None], comm.order(a_sink), A_HALF_WINDOW, True,
                                   "attn_a_bwd")
    bwd_in = {1: (dob1[None], lg1[None], dlb1[None]), 4: (dob4, lg4, dlb4), 16: (dob16, lg16, dlb16)}
    bg = {}
    for w, d in B_PATTERNS:
        q_, k_, v_ = bqs[d]
        do_, l_, dl_ = bwd_in[d]
        bg[d] = _attn_bwd(q_, k_, v_, do_, l_, dl_, no_sink, w // (2 * d), False, f"attn_b{d}_bwd")[:3]
    dproj = _rope_bwd_assemble(dqa[0], dka[0], dva[0], [t[0] for t in bg[1]], bg[4], bg[16], cos, sin)
    dw_in = _tn(dproj, h2, dproj.shape[1] // 2, "w_in_dw")
    comm.ready(dict(w_in=dw_in, w_out=dw_out), dw_in[0])
    dx1, dgm = _dh_norm(dproj, W["w_in"], x1, comm.order(gm), dx2)

    dx0, dff1, dgate1, dup1, dg1 = _ffn_dx(dx1, x, g1, gate1, up1, W["wg1"], W["wu1"], W["wd1"], "ffn1_dx")
    comm.settle(2, dx0)
    dwd1 = _tn(act1, dff1, fb, "ffn1_dw_down", dep=comm.dep())
    comm.ready(dict(wd1=dwd1), dwd1[0])
    dwg1 = _tn(dgate1, h1, fb, "ffn1_dw_gate", dep=comm.dep())
    comm.ready(dict(wg1=dwg1), dwg1[0])
    dwu1 = _tn(dup1, h1, fb, "ffn1_dw_up", dep=comm.dep())
    comm.ready(dict(wu1=dwu1), dwu1[0])

    dsink = dsk[0, :, :, ::HEAD_DIM].sum(axis=1).reshape(-1)
    small = dict(g1=dg1.sum(axis=0), gm=dgm.sum(axis=0), g2=dg2.sum(axis=0), gf=dgf.sum(axis=0), sink=dsink, loss=loss8[0, 0])
    return dx0, small


BIG = ("wg1", "wu1", "wd1", "w_in", "w_out", "wg2", "wu2", "wd2")
GATHER_GROUPS = (("w_in",), ("w_out",), ("wg2", "wu2", "wd2"))


class _Comm:
    def __init__(self, shards, meanwhile):
        x, y, c = _mesh_pos()
        self.me = (2 * x + y).astype(jnp.int32).reshape(1)
        self.shards = shards
        self.tokens = []
        self.waiting = {}
        self.groups = []
        self.swaps = []
        first =("wg1", "wu1", "wd1")
        fulls = {k: _cast_place(self.me, shards[k], f"cast_{k}") for k in first}
        plan = _neighbour_plan([fulls[k].shape for k in first])
        send, recv, bufs, tok = _push_start("gather_first_start", [fulls[k] for k in first], 2 * len(first), plan, self.me)
        self.side = meanwhile(tok)
        fulls.update({k: _cast_place(self.me, shards[k], f"cast_{k}") for k in BIG if k not in first})
        bufs = _push_wait("gather_first_wait", send, recv, bufs, plan, [fulls[k] for k in BIG if k not in first] + list(self.side))
        self.full = dict(zip(first, _gather_forward(bufs)))
        dep = self.full["wd1"]
        for gi, names in enumerate(GATHER_GROUPS):
            plan = _gather_plan(len(names))
            send, recv, bufs, tok = _push_start(f"gather_start_{gi}", [fulls[k] for k in names], 3 * len(names), plan, dep)
            self.tokens.append(tok)
            dep = tok
            for k in names:
                self.waiting[k] = (gi, names, send, recv, bufs, plan)

    def order(self, a):
        for tok in self.tokens:
            a = a + tok[0, 0]
        self.tokens = []
        return a

    def dep(self):
        return self.tokens[-1] if self.tokens else None

    def weight(self, name, after):
        if name in self.waiting:
            gi, names, send, recv, bufs, plan = self.waiting[name]
            for k, buf in zip(names, _push_wait(f"gather_wait_{gi}", send, recv, bufs, plan, after)):
                self.full[k] = buf
                del self.waiting[k]
        full = self.full[name]
        return full.reshape(N_CHIPS * full.shape[1], full.shape[2])

    def ready(self, grads, after):
        names = list(grads)
        f32s, b16s = [], []
        for k in names:
            gf, gb = grads[k]
            f32s.append(gf.reshape((N_CHIPS,) + self.shards[k].shape))
            b16s.append(gb.reshape((N_CHIPS,) + self.shards[k].shape))
        n = len(names)
        lands = [lax.empty((N_CHIPS - 1,) + self.shards[k].shape, BF16) for k in names]
        plan = _grad_push_plan(n)
        send, recv, bufs, tok = _push_start(f"grad_start_{names[0]}", b16s + lands, 3 * n, plan, after)
        self.tokens.append(tok)
        self.groups.append((names, f32s, send, recv, bufs, plan))

    def settle(self, count, after):
        batch, self.groups = self.groups[:count], self.groups[count:]
        names_b, mine_b = [], []
        for names, f32s, send, recv, bufs, plan in batch:
            n = len(names)
            bufs = _push_wait(f"grad_wait_{names[0]}", send, recv, bufs, plan, mine_b[-1] if mine_b else after)
            mine_b += [_sum_own(self.me, f32s[i], bufs[n + i], f"sum_{k}") for i, k in enumerate(names)]
            names_b += names
        lands = [lax.empty(p.shape, F32) for p in mine_b]
        n = len(names_b)
        send2, recv2, both, tok = _push_start(f"swap_start_{names_b[0]}", mine_b + lands, n, _swap_plan(n), after)
        self.tokens.append(tok)
        self.swaps.append((names_b, send2, recv2, both))

    def partials(self, after):
        names_b, send2, recv2, both = self.swaps.pop(0)
        n = len(names_b)
        both = _push_wait(f"swap_wait_{names_b[0]}", send2, recv2, both, _swap_plan(n), after)
        return {k: (both[i], both[n + i]) for i, k in enumerate(names_b)}


def kernel(x, positions, norm_ffn1, w_gate1, w_up1, w_down1, norm_mix, w_in, a_sink, w_out, norm_ffn2, w_gate2, w_up2, w_down2, norm_final, loss_target, m_norm_ffn1, m_w_gate1, m_w_up1, m_w_down1, m_norm_mix, m_w_in, m_a_sink, m_w_out, m_norm_ffn2, m_w_gate2, m_w_up2, m_w_down2, m_norm_final, v_norm_ffn1, v_w_gate1, v_w_up1, v_w_down1, v_norm_mix, v_w_in, v_a_sink, v_w_out, v_norm_ffn2, v_w_gate2, v_w_up2, v_w_down2, v_norm_final):
    T, D = x.shape[1], x.shape[2]
    flip = ("wg1", "wu1", "w_in", "wg2", "wu2")

    def rows(k, a):
        return a[0].T if k in flip else a[0]

    given = dict(wg1=(w_gate1, m_w_gate1, v_w_gate1), wu1=(w_up1, m_w_up1, v_w_up1), wd1=(w_down1, m_w_down1, v_w_down1),
                 w_in=(w_in, m_w_in, v_w_in), w_out=(w_out, m_w_out, v_w_out), wg2=(w_gate2, m_w_gate2, v_w_gate2),
                 wu2=(w_up2, m_w_up2, v_w_up2), wd2=(w_down2, m_w_down2, v_w_down2))
    shards = {k: rows(k, given[k][0]) for k in BIG}

    comm = _Comm(shards, lambda tok: _rope(positions[0], tok))

    norms = (norm_ffn1, norm_mix, norm_ffn2, norm_final.reshape(1, D))
    grad_x, small = _local_step(x[0], comm.side, loss_target[0], norms, a_sink[0], comm)

    upd = {}

    def update(partial):
        for k in partial:
            outs = _adamw(shards[k], partial[k][0], partial[k][1], rows(k, given[k][1]), rows(k, given[k][2]), f"adamw_{k}")
            upd[k] = tuple((a.T if k in flip else a)[None] for a in outs)
        return outs[0]

    last = update(comm.partials(comm.dep()))
    comm.settle(2, last)

    def pad_row(a):
        a = a.reshape(-1)
        return jnp.pad(a, (0, D - a.shape[0]))

    row4 = pad_row(jnp.concatenate([small["sink"], small["loss"].reshape(1)]))
    vec = jnp.stack([small["g1"], small["gm"], small["g2"], small["gf"], row4] + [jnp.zeros((D,), F32)] * 3, axis=0)
    red = _allreduce_small(comm.order(vec))
    loss = red[4, 8]
    comm.settle(1, red)
    last = update(comm.partials(comm.dep()))
    update(comm.partials(last))
    as_row = lambda a: a.reshape(1, -1)
    sm = _adamw_small(red, red[4:5, 0:8], [tuple(as_row(a) for a in p) for p in (
        (norm_ffn1, m_norm_ffn1, v_norm_ffn1), (norm_mix, m_norm_mix, v_norm_mix), (norm_ffn2, m_norm_ffn2, v_norm_ffn2),
        (norm_final, m_norm_final, v_norm_final), (a_sink, m_a_sink, v_a_sink))])
    sm[3] = [a.reshape(D) for a in sm[3]]

    def ordered(i):
        return [sm[0][i], upd["wg1"][i], upd["wu1"][i], upd["wd1"][i], sm[1][i], upd["w_in"][i], sm[4][i], upd["w_out"][i], sm[2][i],
                upd["wg2"][i], upd["wu2"][i], upd["wd2"][i], sm[3][i]]

    return (loss, grad_x[None], *ordered(0), *ordered(1), *ordered(2), *ordered(3))
```

```python
import jax
import jax.numpy as jnp
from jax import lax
from jax.experimental import pallas as pl
from jax.experimental.pallas import tpu as pltpu

F32 = jnp.float32
BF16 = jnp.bfloat16

HEAD_DIM = 64
LANES = 128
SUBLANES = 8
A_Q_W, A_KV_W, B_W = 512, 128, 512
A_HALF_WINDOW = 128
B_PATTERNS = ((128, 1), (512, 4), (2048, 16))
ROPE_THETA = 10000.0
NORM_EPS = 1e-6
FFN_RES_WEIGHT = 0.5
ADAM_LR, ADAM_B1, ADAM_B2, ADAM_EPS, ADAM_WD, ADAM_STEP = 0.001, 0.9, 0.999, 1e-08, 0.01, 10
N_CHIPS = 4
N_DEV = 8
QB = 128
SHORT_SEQ = 512
NEG = -1e30
VMEM_LIMIT = 56 * 1024 * 1024
MESH = pl.DeviceIdType.MESH
ANY = pl.BlockSpec(memory_space=pl.ANY)


def _params(sem=None):
    return pltpu.CompilerParams(dimension_semantics=sem, vmem_limit_bytes=VMEM_LIMIT)


def _sds(shape, dtype):
    return jax.ShapeDtypeStruct(tuple(shape), dtype)


def _dot(a, b):
    return jnp.dot(a, b, preferred_element_type=F32)


def _dot_nt(a, b):
    return lax.dot_general(a, b, (((1,), (1,)), ((), ())), preferred_element_type=F32)


def _dot_tn(a, b):
    return lax.dot_general(a, b, (((0,), (0,)), ((), ())), preferred_element_type=F32)


def _rms_stats(x):
    r = lax.rsqrt(jnp.mean(x * x, axis=-1, keepdims=True) + NORM_EPS)
    return x * r, r


def _rms_bwd(dh, x, g):
    xhat, r = _rms_stats(x)
    dxn = dh * g
    dx = r * (dxn - xhat * jnp.mean(dxn * xhat, axis=-1, keepdims=True))
    tm, d = x.shape
    dg = (dh * xhat).reshape(tm // SUBLANES, SUBLANES, d).sum(axis=0)
    return dx, dg


def _sigmoid(x):
    return 1.0 / (1.0 + jnp.exp(-x))


def _swap32(t):
    n = t.shape[-1]
    lane = lax.broadcasted_iota(jnp.int32, t.shape, t.ndim - 1)
    return jnp.where((lane % HEAD_DIM) < HEAD_DIM // 2, pltpu.roll(t, n - HEAD_DIM // 2, axis=t.ndim - 1),
                     pltpu.roll(t, HEAD_DIM // 2, axis=t.ndim - 1))


def _ordered(body, n_in, dep):
    if dep is None:
        return body, [], []

    def ordered(*refs):
        body(*refs[:n_in], *refs[n_in + 1:])

    return ordered, [ANY], [dep]


def _cast_place(me_arr, w, name):
    R, C = w.shape
    tr = R // 2 if (R // 2) % 16 == 0 else R

    def body(me_ref, w_ref, o_ref):
        o_ref[...] = w_ref[...].astype(BF16)

    grid_spec = pltpu.PrefetchScalarGridSpec(
        num_scalar_prefetch=1, grid=(R // tr,), in_specs=[pl.BlockSpec((tr, C), lambda t, me: (t, 0))],
        out_specs=pl.BlockSpec((None, tr, C), lambda t, me: (me[0], t, 0)))
    return pl.pallas_call(body, name=name, grid_spec=grid_spec, out_shape=_sds((N_CHIPS, R, C), BF16),
                          compiler_params=_params(("parallel",)))(me_arr, w)


HBM = pl.BlockSpec(memory_space=pltpu.HBM)
SEM = pl.BlockSpec(memory_space=pltpu.SEMAPHORE)


def _push_start(name, bufs, ncopies, plan, after):
    nb = len(bufs)

    def body(*refs):
        send, recv, token = refs[nb + 1], refs[nb + 2], refs[-1]
        for i, (src, dst, dev) in enumerate(plan(refs[:nb])):
            pltpu.make_async_remote_copy(src_ref=src, dst_ref=dst, send_sem=send.at[i], recv_sem=recv.at[i],
                                         device_id=dev, device_id_type=MESH).start()
        token[...] = jnp.zeros_like(token)

    outs = pl.pallas_call(
        body, name=name,
        out_shape=(pltpu.SemaphoreType.DMA((ncopies,)), pltpu.SemaphoreType.DMA((ncopies,)), *[pltpu.HBM(b.shape, b.dtype) for b in bufs],
                   _sds((SUBLANES, LANES), F32)),
        in_specs=[HBM] * nb + [ANY], out_specs=(SEM, SEM, *([HBM] * nb), pl.BlockSpec(memory_space=pltpu.VMEM)),
        input_output_aliases={i: 2 + i for i in range(nb)},
        compiler_params=pltpu.CompilerParams(has_side_effects=pltpu.SideEffectType.DATAFLOW_SIDE_EFFECTING),
    )(*[pltpu.with_memory_space_constraint(b, pltpu.HBM) for b in bufs], after)
    return outs[0], outs[1], list(outs[2:2 + nb]), outs[-1]


def _push_wait(name, send, recv, bufs, plan, after):
    nb = len(bufs)

    def body(*refs):
        send_ref, recv_ref = refs[nb], refs[nb + 1]
        for i, (src, dst, dev) in enumerate(plan(refs[:nb])):
            cp = pltpu.make_async_remote_copy(src_ref=src, dst_ref=dst, send_sem=send_ref.at[i], recv_sem=recv_ref.at[i],
                                              device_id=dev, device_id_type=MESH)
            cp.wait_send()
            cp.wait_recv()

    afters = list(after) if isinstance(after, (list, tuple)) else [after]
    outs = pl.pallas_call(
        body, name=name, out_shape=tuple(pltpu.HBM(b.shape, b.dtype) for b in bufs),
        in_specs=[HBM] * nb + [SEM, SEM] + [ANY] * len(afters), out_specs=tuple([HBM] * nb),
        input_output_aliases={i: i for i in range(nb)},
        compiler_params=pltpu.CompilerParams(has_side_effects=pltpu.SideEffectType.DATAFLOW_SIDE_EFFECTING),
    )(*bufs, send, recv, *afters)
    return list(outs)


def _mesh_pos():
    return lax.axis_index("x"), lax.axis_index("y"), lax.axis_index("c")


def _chip_peers(x, y, c):
    return [((1 - x, y, c), 2 * (1 - x) + y), ((x, 1 - y, c), 2 * x + (1 - y)), ((1 - x, 1 - y, c), 2 * (1 - x) + (1 - y))]


def _gather_plan(n):
    def plan(refs):
        x, y, c = _mesh_pos()
        me = 2 * x + y
        return [(refs[k].at[me], refs[k].at[me], dev) for k in range(n) for dev, _ in _chip_peers(x, y, c)]
    return plan


def _rows_of(shape, who, quarter=None):
    r2 = shape[1] // 2
    if quarter is None:
        return pl.ds(pl.multiple_of(who * r2, 16), r2)
    return pl.ds(pl.multiple_of(who * r2 + quarter * (r2 // 2), 16), r2 // 2)


def _neighbour_plan(shapes):
    def plan(refs):
        x, y, c = _mesh_pos()
        me = 2 * x + y
        return [(refs[k].at[me, _rows_of(shp, c), :], refs[k].at[me, _rows_of(shp, c), :], dev)
                for k, shp in enumerate(shapes) for dev in ((1 - x, y, c), (x, 1 - y, c))]
    return plan


def _gather_forward(fulls):
    n = len(fulls)

    def body(*refs):
        ins, outs = refs[:n], refs[n:2 * n]
        ici_send, ici_recv, d2d_send, d2d_recv = refs[2 * n:]
        x, y, c = _mesh_pos()
        cx, cy, cd = 2 * (1 - x) + y, 2 * x + (1 - y), 2 * (1 - x) + (1 - y)
        sibling, x_nbr, y_nbr = (x, y, 1 - c), (1 - x, y, c), (x, 1 - y, c)
        started = []

        def push(src, dst, send, recv, dev):
            cp = pltpu.make_async_remote_copy(src_ref=src, dst_ref=dst, send_sem=send, recv_sem=recv, device_id=dev, device_id_type=MESH)
            cp.start()
            started.append(cp)

        def arrived(blk, send, recv):
            pltpu.make_async_remote_copy(src_ref=blk, dst_ref=blk, send_sem=send, recv_sem=recv, device_id=sibling,
                                         device_id_type=MESH).wait_recv()

        for k in range(n):
            shp = fulls[k].shape
            for j, chip in enumerate((cx, cy)):
                push(ins[k].at[chip, _rows_of(shp, c), :], outs[k].at[chip, _rows_of(shp, c), :],
                     d2d_send.at[3 * k + j], d2d_recv.at[3 * k + j], sibling)
            push(ins[k].at[cx, _rows_of(shp, c, 0), :], outs[k].at[cx, _rows_of(shp, c, 0), :], ici_send.at[2 * k], ici_recv.at[2 * k], y_nbr)
            push(ins[k].at[cy, _rows_of(shp, c, 1), :], outs[k].at[cy, _rows_of(shp, c, 1), :], ici_send.at[2 * k + 1], ici_recv.at[2 * k + 1],
                 x_nbr)
        for k in range(n):
            shp = fulls[k].shape
            for q in (0, 1):
                arrived(outs[k].at[cd, _rows_of(shp, c, q), :], ici_send.at[2 * k + q], ici_recv.at[2 * k + q])
            blk = outs[k].at[cd, _rows_of(shp, c), :]
            push(blk, blk, d2d_send.at[3 * k + 2], d2d_recv.at[3 * k + 2], sibling)
        for k in range(n):
            for j, chip in enumerate((cx, cy, cd)):
                arrived(outs[k].at[chip, _rows_of(fulls[k].shape, 1 - c), :], d2d_send.at[3 * k + j], d2d_recv.at[3 * k + j])
        for cp in started:
            cp.wait_send()

    return pl.pallas_call(
        body, name="gather_forward", out_shape=[_sds(f.shape, BF16) for f in fulls],
        in_specs=[ANY] * n, out_specs=[ANY] * n, input_output_aliases={k: k for k in range(n)},
        scratch_shapes=[pltpu.SemaphoreType.DMA((n * 2,))] * 2 + [pltpu.SemaphoreType.DMA((n * 3,))] * 2,
        compiler_params=_params())(*fulls)


def _resident(shape):
    return pl.BlockSpec(shape, lambda i: (0,) * len(shape), pipeline_mode=pl.Buffered(1))


FFN_FWD_CHUNK = 256
FFN_DX_CHUNK = 512


def _chunks(n, step):
    return [(c0, min(step, n - c0)) for c0 in range(0, n, step)]


def _two_phase(chunks, first, second):
    held = {}
    for ci, ch in enumerate(chunks):
        held[ci] = first(*ch)
        if ci >= 1:
            second(*chunks[ci - 1], held.pop(ci - 1))
    last = len(chunks) - 1
    second(*chunks[last], held.pop(last))


def _ffn_fwd(x, g, wgt, wut, wd, name, tm=512, dep=None):
    T, D = x.shape
    F = wd.shape[0]

    def body(x_ref, g_ref, wg_ref, wu_ref, wd_ref, xo_ref, h_ref, gate_ref, up_ref, act_ref):
        xv = x_ref[...]
        xhat, _ = _rms_stats(xv)
        h = (xhat * g_ref[...]).astype(BF16)
        h_ref[...] = h
        acc = []

        def first(c0, cw):
            return _dot_nt(h, wg_ref[c0:c0 + cw, :]), _dot_nt(h, wu_ref[c0:c0 + cw, :])

        def second(c0, cw, gate_up):
            gate, up = gate_up
            act = ((gate * _sigmoid(gate)) * up).astype(BF16)
            gate_ref[:, c0:c0 + cw] = gate.astype(BF16)
            up_ref[:, c0:c0 + cw] = up.astype(BF16)
            act_ref[:, c0:c0 + cw] = act
            d = _dot(act, wd_ref[c0:c0 + cw, :])
            acc[:] = [d if not acc else acc[0] + d]

        _two_phase(_chunks(F, FFN_FWD_CHUNK), first, second)
        xo_ref[...] = xv + FFN_RES_WEIGHT * acc[0]

    row = pl.BlockSpec((tm, D), lambda i: (i, 0))
    saved = pl.BlockSpec((tm, F), lambda i: (i, 0))
    body, dep_spec, dep_arg = _ordered(body, 5, dep)
    return pl.pallas_call(
        body, name=name, grid=(T // tm,),
        in_specs=[row, pl.BlockSpec((1, D), lambda i: (0, 0)), _resident(wgt.shape), _resident(wut.shape), _resident(wd.shape)] + dep_spec,
        out_specs=[row, row, saved, saved, saved],
        out_shape=[_sds((T, D), F32), _sds((T, D), BF16), _sds((T, F), BF16), _sds((T, F), BF16), _sds((T, F), BF16)],
        compiler_params=_params(("parallel",)))(x, g, wgt, wut, wd, *dep_arg)


def _ffn_dx(dxo, x, g, gate_s, up_s, wgt, wut, wd, name, tm=256):
    T, D = x.shape
    F = wd.shape[0]

    def body(dxo_ref, x_ref, g_ref, gate_ref, up_ref, wg_ref, wu_ref, wd_ref, dx_ref, dff_ref, dgate_ref, dup_ref, dg_ref):
        @pl.when(pl.program_id(0) == 0)
        def _():
            dg_ref[...] = jnp.zeros_like(dg_ref)

        d = (FFN_RES_WEIGHT * dxo_ref[...]).astype(BF16)
        dff_ref[...] = d
        dh = []

        def first(c0, cw):
            return _dot_nt(d, wd_ref[c0:c0 + cw, :])

        def second(c0, cw, da):
            gate = gate_ref[:, c0:c0 + cw].astype(F32)
            up = up_ref[:, c0:c0 + cw].astype(F32)
            s = _sigmoid(gate)
            silu = gate * s
            dup = (da * silu).astype(BF16)
            dgate = (da * up * (s * (1.0 + gate * (1.0 - s)))).astype(BF16)
            dgate_ref[:, c0:c0 + cw] = dgate
            dup_ref[:, c0:c0 + cw] = dup
            t = _dot(dgate, wg_ref[c0:c0 + cw, :]) + _dot(dup, wu_ref[c0:c0 + cw, :])
            dh[:] = [t if not dh else dh[0] + t]

        _two_phase(_chunks(F, FFN_DX_CHUNK), first, second)
        dxn, dg = _rms_bwd(dh[0], x_ref[...], g_ref[...])
        dg_ref[...] += dg
        dx_ref[...] = dxo_ref[...] + dxn

    row = pl.BlockSpec((tm, D), lambda i: (i, 0))
    saved = pl.BlockSpec((tm, F), lambda i: (i, 0))
    return pl.pallas_call(
        body, name=name, grid=(T // tm,),
        in_specs=[row, row, pl.BlockSpec((1, D), lambda i: (0, 0)), saved, saved, _resident(wgt.shape), _resident(wut.shape),
                  _resident(wd.shape)],
        out_specs=[row, row, saved, saved, pl.BlockSpec((SUBLANES, D), lambda i: (0, 0))],
        out_shape=[_sds((T, D), F32), _sds((T, D), BF16), _sds((T, F), BF16), _sds((T, F), BF16), _sds((SUBLANES, D), F32)],
        compiler_params=_params(("arbitrary",)))(dxo, x, g, gate_s, up_s, wgt, wut, wd)


def _tn(a, b, mb, name, tk=2048, dep=None):
    T, M = a.shape
    N = b.shape[1]
    nt = T // tk

    def body(a_ref, b_ref, o_ref, ob_ref):
        @pl.when(pl.program_id(1) == 0)
        def _():
            o_ref[...] = jnp.zeros_like(o_ref)

        o_ref[...] += _dot_tn(a_ref[...].astype(BF16), b_ref[...].astype(BF16))

        @pl.when(pl.program_id(1) == nt - 1)
        def _():
            ob_ref[...] = o_ref[...].astype(BF16)

    o_spec = pl.BlockSpec((mb, N), lambda g, t: (g, 0))
    body, dep_spec, dep_arg = _ordered(body, 2, dep)
    return pl.pallas_call(
        body, name=name, grid=(M // mb, nt),
        in_specs=[pl.BlockSpec((tk, mb), lambda g, t: (t, g)), pl.BlockSpec((tk, N), lambda g, t: (t, 0))] + dep_spec,
        out_specs=[o_spec, o_spec], out_shape=[_sds((M, N), F32), _sds((M, N), BF16)],
        compiler_params=_params(("parallel", "arbitrary")))(a, b, *dep_arg)


def _rope_tables(pos_col, inv_freq):
    T = pos_col.shape[0]

    def body(p_ref, f_ref, c_ref, s_ref):
        ang = p_ref[...].astype(F32) * f_ref[...]
        lane = lax.broadcasted_iota(jnp.int32, ang.shape, 1)
        c_ref[...] = jnp.cos(ang)
        sn = jnp.sin(ang)
        s_ref[...] = jnp.where((lane % HEAD_DIM) < HEAD_DIM // 2, -sn, sn)

    tm = 1024
    return pl.pallas_call(
        body, name="rope_tables", grid=(T // tm,),
        in_specs=[pl.BlockSpec((tm, 1), lambda i: (i, 0)), pl.BlockSpec((1, LANES), lambda i: (0, 0))],
        out_specs=[pl.BlockSpec((tm, LANES), lambda i: (i, 0))] * 2,
        out_shape=[_sds((T, LANES), F32)] * 2, compiler_params=_params(("parallel",)))(pos_col, inv_freq)


def _deinterleave(scr, out_ref, d, tm, nblk):
    for r in range(d):
        for cb in range(nblk):
            out_ref[r, :, cb * LANES:(cb + 1) * LANES] = scr[cb, pl.ds(r, tm // d, stride=d), :].astype(out_ref.dtype)


def _interleave(in_ref, scr, d, tm, nblk):
    for r in range(d):
        for cb in range(nblk):
            scr[cb, pl.ds(r, tm // d, stride=d), :] = in_ref[r, :, cb * LANES:(cb + 1) * LANES].astype(F32)


def _proj_rope(x, g, w_in, cos, sin, tm=512):
    T, D = x.shape
    dils = [d for _, d in B_PATTERNS if d > 1]
    nbb = B_W // LANES
    scale = HEAD_DIM ** -0.5
    cuts = [0, A_Q_W, A_Q_W + A_KV_W, A_Q_W + 2 * A_KV_W, A_Q_W + 2 * A_KV_W + B_W, A_Q_W + 2 * A_KV_W + 2 * B_W,
            A_Q_W + 2 * A_KV_W + 3 * B_W]

    def body(x_ref, g_ref, w_ref, c_ref, s_ref, h_ref, aq_ref, ak_ref, av_ref, *rest):
        b_refs, scr = rest[:-1], rest[-1]
        xhat, _ = _rms_stats(x_ref[...])
        h = (xhat * g_ref[...]).astype(BF16)
        h_ref[...] = h
        cs, sn = c_ref[...], s_ref[...]

        def project(idx, ref, rope, mult, which):
            return _dot_nt(h, w_ref[cuts[idx]:cuts[idx + 1], :])

        def finish(idx, ref, rope, mult, which, whole):
            for cb in range((cuts[idx + 1] - cuts[idx]) // LANES):
                p = whole[:, cb * LANES:(cb + 1) * LANES]
                if rope:
                    p = p * cs + _swap32(p) * sn
                if mult != 1.0:
                    p = p * mult
                ref[:, cb * LANES:(cb + 1) * LANES] = p.astype(BF16)
                if which is not None:
                    scr[which, cb] = p
            if which is not None:
                for di, d in enumerate(dils):
                    _deinterleave(scr.at[which], b_refs[3 * (di + 1) + which], d, tm, nbb)

        _two_phase([(0, aq_ref, True, scale, None), (1, ak_ref, True, 1.0, None), (2, av_ref, False, 1.0, None),
                    (3, b_refs[0], True, scale, 0), (4, b_refs[1], True, 1.0, 1), (5, b_refs[2], False, 1.0, 2)], project, finish)

    row = lambda w: pl.BlockSpec((tm, w), lambda i: (i, 0))
    out_specs = [row(D), row(A_Q_W), row(A_KV_W), row(A_KV_W)] + [row(B_W)] * 3
    out_shape = [_sds((T, D), BF16), _sds((T, A_Q_W), BF16), _sds((T, A_KV_W), BF16), _sds((T, A_KV_W), BF16)] + [_sds((T, B_W), BF16)] * 3
    for d in dils:
        out_specs += [pl.BlockSpec((d, tm // d, B_W), lambda i: (0, i, 0))] * 3
        out_shape += [_sds((d, T // d, B_W), BF16)] * 3
    return pl.pallas_call(
        body, name="proj_rope", grid=(T // tm,),
        in_specs=[row(D), pl.BlockSpec((1, D), lambda i: (0, 0)), pl.BlockSpec(w_in.shape, lambda i: (0, 0)), row(LANES), row(LANES)],
        out_specs=out_specs, out_shape=out_shape, scratch_shapes=[pltpu.VMEM((3, nbb, tm, LANES), F32)],
        compiler_params=_params(("parallel",)))(x, g, w_in, cos, sin)


def _band_bias(rel, qb, kw, hw):
    ri = lax.broadcasted_iota(jnp.int32, (2 * qb, kw), 0) & (qb - 1)
    ci = lax.broadcasted_iota(jnp.int32, (2 * qb, kw), 1)
    return jnp.where(jnp.abs(ri + rel - ci) <= hw, 0.0, NEG).astype(F32)


def _stack_heads(x, lo):
    z = jnp.zeros_like(x)
    return jnp.concatenate([jnp.where(lo, x, z), jnp.where(lo, z, x)], axis=0)


def _unstack_heads(y, lo):
    qb = y.shape[0] // 2
    return jnp.where(lo, y[:qb], y[qb:])


def _band_setup(bias_scr, qb, kw, hw):
    if bias_scr is not None:
        for i in range(3):
            bias_scr[i] = _band_bias(i * hw, qb, kw, hw)


def _band_window(bias_scr, qs, L, qb, kw, hw):
    ws = pl.multiple_of(jnp.clip(qs - hw, 0, L - kw), 64)
    if bias_scr is None:
        return ws, _band_bias(qs - ws, qb, kw, hw)
    return ws, bias_scr[lax.shift_right_logical(qs - ws, hw.bit_length() - 1)]


def _dup_kv_head(src_ref, dst_ref, head, L):
    step = min(L, 1024)
    for r0 in range(0, L, step):
        xf = src_ref[r0:r0 + step, :].astype(F32)
        lane = lax.broadcasted_iota(jnp.int32, xf.shape, 1)
        keep = jnp.logical_xor(lane < HEAD_DIM, head == 1)
        dst_ref[r0:r0 + step, :] = jnp.where(keep, xf, pltpu.roll(xf, HEAD_DIM, axis=1)).astype(dst_ref.dtype)


def _attn_fwd(q, k, v, sink, hw, gqa, out_dtype, name, qb=QB, blocks_per_step=8, out_cols=None):
    NB, L, Cq = q.shape
    Ls = min(L, 2048)
    kw = min(qb + 2 * hw, L)
    tables = L >= qb + 2 * hw
    unroll = min(blocks_per_step, Ls // qb)
    nlb = 1 if (gqa or L > SHORT_SEQ) else Cq // LANES

    def body(sink_ref, q_ref, k_ref, v_ref, o_ref, lse_ref, *scr):
        b, s_idx = pl.program_id(1), pl.program_id(2)
        bias_scr = scr[0] if tables else None
        _band_setup(bias_scr, qb, kw, hw)
        if gqa:
            kd, vd = scr[-2:]

            @pl.when(s_idx == 0)
            def _():
                _dup_kv_head(k_ref, kd, b // 2, L)
                _dup_kv_head(v_ref, vd, b // 2, L)
        else:
            kd, vd = k_ref, v_ref
        lane = lax.broadcasted_iota(jnp.int32, (qb, LANES), 1)
        lo = lane < HEAD_DIM
        if gqa:
            row = lax.broadcasted_iota(jnp.int32, (2 * qb, 1), 0)
            sk = jnp.where(row < qb, sink_ref[2 * b], sink_ref[2 * b + 1])

        def block(ql, col):
            qs = s_idx * Ls + ql
            ws, bias = _band_window(bias_scr, qs, L, qb, kw, hw)
            return ws, _dot_nt(_stack_heads(q_ref[pl.ds(ql, qb), col], lo), kd[pl.ds(ws, kw), col]) + bias

        def finish(ql, col, scores):
            ws, s = scores
            m = jnp.max(s, axis=-1, keepdims=True)
            if gqa:
                m = jnp.maximum(m, sk)
            p = jnp.exp(s - m)
            den = jnp.sum(p, axis=-1, keepdims=True)
            if gqa:
                den = den + jnp.exp(sk - m)
            o = _dot(p.astype(BF16), vd[pl.ds(ws, kw), col]) * (1.0 / den)
            o_ref[pl.ds(ql, qb), col] = _unstack_heads(o, lo).astype(o_ref.dtype)
            lse_ref[pl.ds(ql, qb), col] = _unstack_heads(m + jnp.log(den), lo)

        for lb in range(nlb):
            def step(n, carry, col=slice(lb * LANES, (lb + 1) * LANES)):
                _two_phase([(pl.multiple_of((n * unroll + u) * qb, qb), col) for u in range(unroll)], block, finish)
                return carry

            lax.fori_loop(0, Ls // (qb * unroll), step, 0)

    kv_map = (lambda r, b, s: (r, 0, 0)) if gqa else (lambda r, b, s: (r, 0, b))
    seg = pl.BlockSpec((None, Ls, nlb * LANES), lambda r, b, s: (r, s, b))
    return pl.pallas_call(
        body, name=name, grid=(NB, Cq // (nlb * LANES), L // Ls),
        in_specs=[pl.BlockSpec(memory_space=pltpu.SMEM), seg, pl.BlockSpec((None, L, nlb * LANES), kv_map),
                  pl.BlockSpec((None, L, nlb * LANES), kv_map)],
        out_specs=[seg, seg], out_shape=[_sds((NB, L, out_cols or Cq), out_dtype), _sds((NB, L, Cq), F32)],
        scratch_shapes=([pltpu.VMEM((3, 2 * qb, kw), F32)] if tables else []) + ([pltpu.VMEM((L, LANES), BF16)] * 2 if gqa else []),
        compiler_params=_params(("parallel", "parallel", "arbitrary")))(sink, q, k, v)


def _attn_bwd(q, k, v, do, lse, delta, sink, hw, gqa, name, qb=QB, blocks_per_step=8, dep=None):
    NB, L, Cq = q.shape
    Ck = k.shape[2]
    Ls = min(L, 2048)
    kw = min(qb + 2 * hw, L)
    reps = kw // LANES
    nseg = L // Ls
    scale = HEAD_DIM ** -0.5
    tables = L >= qb + 2 * hw
    unroll = min(blocks_per_step, Ls // qb)
    nlb = 1 if (gqa or L > SHORT_SEQ) else Cq // LANES

    def body(sink_ref, q_ref, do_ref, lse_ref, dl_ref, k_ref, v_ref, dq_ref, dk_ref, dv_ref, dsk_ref, *scr):
        b, s_idx = pl.program_id(1), pl.program_id(2)
        lane = lax.broadcasted_iota(jnp.int32, (qb, LANES), 1)
        lo = lane < HEAD_DIM
        bias_scr = scr[0] if tables else None
        _band_setup(bias_scr, qb, kw, hw)
        if gqa:
            kd, vd, dk_acc, dv_acc, dsk_acc = scr[-5:]

            @pl.when(s_idx == 0)
            def _():
                _dup_kv_head(k_ref, kd, b // 2, L)
                _dup_kv_head(v_ref, vd, b // 2, L)
                dk_acc[...] = jnp.zeros_like(dk_acc)
                dv_acc[...] = jnp.zeros_like(dv_acc)
                dsk_acc[...] = jnp.zeros_like(dsk_acc)

            @pl.when((s_idx == 0) & (b == 0))
            def _():
                dk_ref[...] = jnp.zeros_like(dk_ref)
                dv_ref[...] = jnp.zeros_like(dv_ref)
        else:
            kd, vd = k_ref, v_ref
            dk_acc, dv_acc = scr[-2:]

            @pl.when(s_idx == 0)
            def _():
                dk_acc[...] = jnp.zeros_like(dk_acc)
                dv_acc[...] = jnp.zeros_like(dv_acc)

        def block(ql, col):
            qs = s_idx * Ls + ql
            ws, bias = _band_window(bias_scr, qs, L, qb, kw, hw)
            qv, dov = q_ref[pl.ds(ql, qb), col], do_ref[pl.ds(ql, qb), col]
            lse, dl = lse_ref[pl.ds(ql, qb), col], dl_ref[pl.ds(ql, qb), col]
            kv_, vv = kd[pl.ds(ws, kw), col], vd[pl.ds(ws, kw), col]
            q2, do2 = _stack_heads(qv, lo), _stack_heads(dov, lo)
            return ws, q2, do2, lse, dl, _dot_nt(q2, kv_) + bias, _dot_nt(do2, vv)

        def finish(ql, col, held):
            ws, q2, do2, lse, dl, s, dp = held
            lse_sw, dl_sw = pltpu.roll(lse, HEAD_DIM, axis=1), pltpu.roll(dl, HEAD_DIM, axis=1)
            lse2 = jnp.concatenate([jnp.where(lo, lse, lse_sw), jnp.where(lo, lse_sw, lse)], axis=0)
            dl2 = jnp.concatenate([jnp.where(lo, dl, dl_sw), jnp.where(lo, dl_sw, dl)], axis=0)
            p = jnp.exp(s - jnp.tile(lse2, (1, reps)))
            ds = (p * (dp - jnp.tile(dl2, (1, reps)))).astype(BF16)
            dq_ref[pl.ds(ql, qb), col] = (_unstack_heads(_dot(ds, kd[pl.ds(ws, kw), col]), lo) * scale).astype(dq_ref.dtype)
            both = _dot_tn(jnp.concatenate([ds, p.astype(BF16)], axis=1), jnp.concatenate([q2, do2], axis=1))
            dk_acc[pl.ds(ws, kw), col] += both[:kw, :LANES]
            dv_acc[pl.ds(ws, kw), col] += both[kw:, LANES:]
            if gqa:
                sk = jnp.where(lo, sink_ref[2 * b], sink_ref[2 * b + 1])
                dsk_acc[...] += -jnp.exp(sk - lse) * dl

        for lb in range(nlb):
            def step(n, carry, col=slice(lb * LANES, (lb + 1) * LANES)):
                _two_phase([(pl.multiple_of((n * unroll + u) * qb, qb), col) for u in range(unroll)], block, finish)
                return carry

            lax.fori_loop(0, Ls // (qb * unroll), step, 0)

        if gqa:
            @pl.when(s_idx == nseg - 1)
            def _():
                step_rows = min(L, 1024)
                for r0 in range(0, L, step_rows):
                    lanek = lax.broadcasted_iota(jnp.int32, (step_rows, LANES), 1)
                    mine = jnp.logical_xor(lanek < HEAD_DIM, (b // 2) == 1)
                    for acc, ref in ((dk_acc, dk_ref), (dv_acc, dv_ref)):
                        a = acc[r0:r0 + step_rows, :]
                        ref[r0:r0 + step_rows, :] += jnp.where(mine, a + pltpu.roll(a, HEAD_DIM, axis=1), 0.0)
                dsk_ref[...] = dsk_acc[...].reshape(qb // SUBLANES, SUBLANES, LANES).sum(axis=0)
        else:
            dsk_ref[...] = jnp.zeros_like(dsk_ref)

            @pl.when(s_idx == nseg - 1)
            def _():
                dk_ref[...] = dk_acc[...].astype(dk_ref.dtype)
                dv_ref[...] = dv_acc[...].astype(dv_ref.dtype)

    kv_map = (lambda r, b, s: (r, 0, 0)) if gqa else (lambda r, b, s: (r, 0, b))
    seg = pl.BlockSpec((None, Ls, nlb * LANES), lambda r, b, s: (r, s, b))
    full = pl.BlockSpec((None, L, nlb * LANES), kv_map)
    scratch = [pltpu.VMEM((3, 2 * qb, kw), F32)] if tables else []
    if gqa:
        scratch += [pltpu.VMEM((L, LANES), BF16)] * 2 + [pltpu.VMEM((L, LANES), F32)] * 2 + [pltpu.VMEM((qb, LANES), F32)]
    else:
        scratch += [pltpu.VMEM((L, nlb * LANES), F32)] * 2
    kv_dtype = F32 if gqa else BF16
    body, dep_spec, dep_arg = _ordered(body, 7, dep)
    return pl.pallas_call(
        body, name=name, grid=(NB, Cq // (nlb * LANES), nseg),
        in_specs=[pl.BlockSpec(memory_space=pltpu.SMEM), seg, seg, seg, seg, full, full] + dep_spec,
        out_specs=[seg, full, full, pl.BlockSpec((None, None, SUBLANES, LANES), lambda r, b, s: (r, b, 0, 0))],
        out_shape=[_sds((NB, L, Cq), BF16), _sds((NB, L, Ck), kv_dtype), _sds((NB, L, Ck), kv_dtype),
                   _sds((NB, Cq // LANES, SUBLANES, LANES), F32)],
        scratch_shapes=scratch,
        compiler_params=_params(("arbitrary", "arbitrary", "arbitrary")))(sink, q, do, lse, delta, k, v, *dep_arg)


def _dilated_fwd(cat, qkv, hw, tile=2048):
    T = cat.shape[0]
    dils = sorted(qkv)
    nbb, na = B_W // LANES, A_Q_W // LANES
    qb, kw = QB, QB + 2 * hw
    rows_merge = 256
    assert T % tile == 0 and all(tile % (d * qb) == 0 and T // d >= kw for d in dils)

    def body(cat_in, *refs):
        qkv_refs = {d: refs[3 * j:3 * j + 3] for j, d in enumerate(dils)}
        cat_ref, lg_refs = refs[3 * len(dils)], refs[3 * len(dils) + 1:4 * len(dils) + 1]
        o_scr, l_scr, bias_scr = refs[4 * len(dils) + 1:]
        i = pl.program_id(1)
        _band_setup(bias_scr, qb, kw, hw)
        lane = lax.broadcasted_iota(jnp.int32, (qb, LANES), 1)
        lo = lane < HEAD_DIM
        for pi, d in enumerate(dils):
            q_ref, k_ref, v_ref = qkv_refs[d]
            L, rows = T // d, tile // d

            def place(r, n, d=d):
                return pl.ds(r + d * n * qb, qb, stride=d) if d > 1 else pl.ds(n * qb, qb)

            def scores(r, n, q_ref=q_ref, k_ref=k_ref, L=L, rows=rows):
                ws, bias = _band_window(bias_scr, i * rows + n * qb, L, qb, kw, hw)
                return ws, _dot_nt(_stack_heads(q_ref[r, n * qb:(n + 1) * qb, :], lo), k_ref[r, pl.ds(ws, kw), :]) + bias

            def finish(r, n, held, v_ref=v_ref, pi=pi, place=place):
                ws, s = held
                m = jnp.max(s, axis=-1, keepdims=True)
                p = jnp.exp(s - m)
                den = jnp.sum(p, axis=-1, keepdims=True)
                o = _dot(p.astype(BF16), v_ref[r, pl.ds(ws, kw), :]) * (1.0 / den)
                o_scr[pi, place(r, n), :] = _unstack_heads(o, lo)
                l_scr[pi, place(r, n), :] = _unstack_heads(m + jnp.log(den), lo)

            blocks = [(r, n) for r in range(d) for n in range(rows // qb)]
            for g0 in range(0, len(blocks), 8):
                _two_phase(blocks[g0:g0 + 8], scores, finish)

        for r0 in range(0, tile, rows_merge):
            rs = slice(r0, r0 + rows_merge)
            ls_ = [l_scr[pi, rs, :] for pi in range(len(dils))]
            m = ls_[0]
            for l in ls_[1:]:
                m = jnp.maximum(m, l)
            es = [jnp.exp(l - m) for l in ls_]
            den, out = es[0], es[0] * o_scr[0, rs, :]
            for pi in range(1, len(dils)):
                den = den + es[pi]
                out = out + es[pi] * o_scr[pi, rs, :]
            cat_ref[rs, :] = (out * (1.0 / den)).astype(BF16)
            l_scr[0, rs, :] = m + jnp.log(den)
        for lg_ref, d in zip(lg_refs, dils):
            for r in range(d):
                lg_ref[r] = l_scr[0, pl.ds(r, tile // d, stride=d), :] if d > 1 else l_scr[0]

    in_specs = [pl.BlockSpec(memory_space=pl.ANY)]
    operands = [cat]
    for d in dils:
        in_specs += [pl.BlockSpec((d, tile // d, LANES), lambda b, i: (0, i, b))] + [pl.BlockSpec((d, T // d, LANES), lambda b, i: (0, 0, b))] * 2
        operands += list(qkv[d])
    return pl.pallas_call(
        body, name="dilated_fwd", grid=(nbb, T // tile), in_specs=in_specs,
        out_specs=[pl.BlockSpec((tile, LANES), lambda b, i: (i, na + b))] + [pl.BlockSpec((d, tile // d, LANES), lambda b, i: (0, i, b)) for d in dils],
        out_shape=[_sds(cat.shape, BF16)] + [_sds((d, T // d, B_W), F32) for d in dils],
        input_output_aliases={0: 0},
        scratch_shapes=[pltpu.VMEM((len(dils), tile, LANES), F32)] * 2 + [pltpu.VMEM((3, 2 * qb, kw), F32)],
        compiler_params=_params(("parallel", "arbitrary")))(*operands)


def _out_proj(x, cat, w_out, tm=512):
    T, D = x.shape

    def body(x_ref, c_ref, w_ref, o_ref):
        o_ref[...] = x_ref[...] + _dot(c_ref[...], w_ref[...])

    row = lambda w: pl.BlockSpec((tm, w), lambda i: (i, 0))
    return pl.pallas_call(
        body, name="out_proj", grid=(T // tm,), in_specs=[row(D), row(cat.shape[1]), pl.BlockSpec(w_out.shape, lambda i: (0, 0))],
        out_specs=row(D), out_shape=_sds((T, D), F32), compiler_params=_params(("parallel",)))(x, cat, w_out)


def _final_loss(x, g, target, tm=512):
    T, D = x.shape

    def body(x_ref, g_ref, t_ref, dx_ref, dg_ref, loss_ref):
        @pl.when(pl.program_id(0) == 0)
        def _():
            dg_ref[...] = jnp.zeros_like(dg_ref)
            loss_ref[...] = jnp.zeros_like(loss_ref)

        xv, gv = x_ref[...], g_ref[...]
        xhat, _ = _rms_stats(xv)
        err = xhat * gv - t_ref[...]
        loss_ref[...] += 0.5 * jnp.sum(jnp.sum(err * err, axis=-1, keepdims=True) * (1.0 / D), axis=0, keepdims=True)
        dx, dg = _rms_bwd(err * (1.0 / D), xv, gv)
        dx_ref[...] = dx
        dg_ref[...] += dg

    row = pl.BlockSpec((tm, D), lambda i: (i, 0))
    return pl.pallas_call(
        body, name="final_loss", grid=(T // tm,), in_specs=[row, pl.BlockSpec((1, D), lambda i: (0, 0)), row],
        out_specs=[row, pl.BlockSpec((SUBLANES, D), lambda i: (0, 0)), pl.BlockSpec((SUBLANES, LANES), lambda i: (0, 0))],
        out_shape=[_sds((T, D), F32), _sds((SUBLANES, D), F32), _sds((SUBLANES, LANES), F32)],
        compiler_params=_params(("arbitrary",)))(x, g, target)


def _dcat(dx, w_out, cat, tm=512):
    T, D = dx.shape
    C = cat.shape[1]
    nba, nbb = A_Q_W // LANES, B_W // LANES

    def body(dx_ref, w_ref, cat_ref, doa_ref, dla_ref, dob1_ref, dlb1_ref, dob4_ref, dlb4_ref, dob16_ref, dlb16_ref, sdo, sdl):
        dc = _dot_nt(dx_ref[...].astype(BF16), w_ref[...])
        ri = lax.broadcasted_iota(jnp.int32, (LANES, LANES), 0)
        ci = lax.broadcasted_iota(jnp.int32, (LANES, LANES), 1)
        same_head = ((ri // HEAD_DIM) == (ci // HEAD_DIM)).astype(BF16)
        for cb in range(C // LANES):
            cols = slice(cb * LANES, (cb + 1) * LANES)
            blk = dc[:, cols]
            prod = blk * cat_ref[:, cols].astype(F32)
            hi = prod.astype(BF16)
            lo_ = (prod - hi.astype(F32)).astype(BF16)
            dl = _dot(hi, same_head) + _dot(lo_, same_head)
            if cb < nba:
                doa_ref[:, cols] = blk.astype(BF16)
                dla_ref[:, cols] = dl
            else:
                bcols = slice((cb - nba) * LANES, (cb - nba + 1) * LANES)
                dob1_ref[:, bcols] = blk.astype(BF16)
                dlb1_ref[:, bcols] = dl
                sdo[cb - nba] = blk
                sdl[cb - nba] = dl
        _deinterleave(sdo, dob4_ref, 4, tm, nbb)
        _deinterleave(sdl, dlb4_ref, 4, tm, nbb)
        _deinterleave(sdo, dob16_ref, 16, tm, nbb)
        _deinterleave(sdl, dlb16_ref, 16, tm, nbb)

    row = lambda w: pl.BlockSpec((tm, w), lambda i: (i, 0))
    perm = lambda d: pl.BlockSpec((d, tm // d, B_W), lambda i: (0, i, 0))
    return pl.pallas_call(
        body, name="dcat", grid=(T // tm,), in_specs=[row(D), pl.BlockSpec(w_out.shape, lambda i: (0, 0)), row(C)],
        out_specs=[row(A_Q_W), row(A_Q_W), row(B_W), row(B_W), perm(4), perm(4), perm(16), perm(16)],
        out_shape=[_sds((T, A_Q_W), BF16), _sds((T, A_Q_W), F32), _sds((T, B_W), BF16), _sds((T, B_W), F32),
                   _sds((4, T // 4, B_W), BF16), _sds((4, T // 4, B_W), F32), _sds((16, T // 16, B_W), BF16), _sds((16, T // 16, B_W), F32)],
        scratch_shapes=[pltpu.VMEM((nbb, tm, LANES), F32)] * 2, compiler_params=_params(("parallel",)))(dx, w_out, cat)


def _rope_bwd_assemble(dqa, dka, dva, b1, b4, b16, cos, sin, tm=512):
    T = dqa.shape[0]
    nbb = B_W // LANES
    width = A_Q_W + 2 * A_KV_W + 3 * B_W

    def body(dqa_ref, dka_ref, dva_ref, q1, k1, v1, q4, k4, v4, q16, k16, v16, c_ref, s_ref, o_ref, scr):
        cs, sn = c_ref[...], s_ref[...]

        def unrope(t):
            return t * cs + _swap32(t * sn)

        col = 0
        for ref, rope in ((dqa_ref, True), (dka_ref, True), (dva_ref, False)):
            for cb in range(ref.shape[1] // LANES):
                t = ref[:, cb * LANES:(cb + 1) * LANES].astype(F32)
                o_ref[:, col:col + LANES] = (unrope(t) if rope else t).astype(BF16)
                col += LANES
        for which, (r1, r4, r16, rope) in enumerate(((q1, q4, q16, True), (k1, k4, k16, True), (v1, v4, v16, False))):
            _interleave(r4, scr.at[0], 4, tm, nbb)
            _interleave(r16, scr.at[1], 16, tm, nbb)
            for cb in range(nbb):
                t = r1[:, cb * LANES:(cb + 1) * LANES].astype(F32) + scr[0, cb] + scr[1, cb]
                o_ref[:, col:col + LANES] = (unrope(t) if rope else t).astype(BF16)
                col += LANES

    row = lambda w: pl.BlockSpec((tm, w), lambda i: (i, 0))
    perm = lambda d: pl.BlockSpec((d, tm // d, B_W), lambda i: (0, i, 0))
    return pl.pallas_call(
        body, name="rope_bwd", grid=(T // tm,),
        in_specs=[row(A_Q_W), row(A_KV_W), row(A_KV_W)] + [row(B_W)] * 3 + [perm(4)] * 3 + [perm(16)] * 3 + [row(LANES), row(LANES)],
        out_specs=row(width), out_shape=_sds((T, width), BF16), scratch_shapes=[pltpu.VMEM((2, nbb, tm, LANES), F32)],
        compiler_params=_params(("parallel",)))(dqa, dka, dva, *b1, *b4, *b16, cos, sin)


def _dh_norm(dproj, w_in, x, g, dres, tm=512, dep=None):
    T, D = x.shape

    def body(dp_ref, w_ref, x_ref, g_ref, dr_ref, dx_ref, dg_ref):
        @pl.when(pl.program_id(0) == 0)
        def _():
            dg_ref[...] = jnp.zeros_like(dg_ref)

        dxn, dg = _rms_bwd(_dot(dp_ref[...], w_ref[...]), x_ref[...], g_ref[...])
        dg_ref[...] += dg
        dx_ref[...] = dr_ref[...] + dxn

    row = lambda w: pl.BlockSpec((tm, w), lambda i: (i, 0))
    body, dep_spec, dep_arg = _ordered(body, 5, dep)
    return pl.pallas_call(
        body, name="dh_norm", grid=(T // tm,),
        in_specs=[row(dproj.shape[1]), pl.BlockSpec(w_in.shape, lambda i: (0, 0)), row(D), pl.BlockSpec((1, D), lambda i: (0, 0)), row(D)] + dep_spec,
        out_specs=[row(D), pl.BlockSpec((SUBLANES, D), lambda i: (0, 0))],
        out_shape=[_sds((T, D), F32), _sds((SUBLANES, D), F32)], compiler_params=_params(("arbitrary",)))(dproj, w_in, x, g, dres, *dep_arg)


def _grad_push_plan(n):
    def plan(refs):
        x, y, c = _mesh_pos()
        return [(refs[k].at[chip], refs[n + k].at[rel], dev) for k in range(n) for rel, (dev, chip) in enumerate(_chip_peers(x, y, c))]
    return plan


def _sum_own(me_arr, g, landed, name):
    ns, R, C = g.shape
    tr = R // 2 if (R // 2) % 16 == 0 else R

    def body(me_ref, g_ref, x_ref, o_ref):
        acc = g_ref[...]
        for rel in range(ns - 1):
            acc = acc + x_ref[rel].astype(F32)
        o_ref[...] = acc

    grid_spec = pltpu.PrefetchScalarGridSpec(
        num_scalar_prefetch=1, grid=(R // tr,),
        in_specs=[pl.BlockSpec((None, tr, C), lambda t, me: (me[0], t, 0)), pl.BlockSpec((ns - 1, tr, C), lambda t, me: (0, t, 0))],
        out_specs=pl.BlockSpec((tr, C), lambda t, me: (t, 0)))
    return pl.pallas_call(body, name=name, grid_spec=grid_spec, out_shape=_sds((R, C), F32),
                          compiler_params=_params(("parallel",)))(me_arr, g, landed)


def _swap_plan(n):
    def plan(refs):
        x, y, c = _mesh_pos()
        return [(refs[k], refs[n + k], (x, y, 1 - c)) for k in range(n)]
    return plan


def _allreduce_small(v, dep):
    rows, W = v.shape

    def body(v_ref, o_ref, buf, send, recv):
        x, y, c = _mesh_pos()
        me = 4 * x + 2 * y + c
        cps = []
        for m in range(1, N_DEV):
            dev = (x ^ (m >> 2), y ^ ((m >> 1) & 1), c ^ (m & 1))
            cp = pltpu.make_async_remote_copy(src_ref=v_ref, dst_ref=buf.at[me], send_sem=send.at[m - 1], recv_sem=recv.at[m - 1],
                                              device_id=dev, device_id_type=MESH)
            cp.start()
            cps.append(cp)
        for m in range(1, N_DEV):
            pltpu.make_async_remote_copy(src_ref=v_ref, dst_ref=buf.at[me ^ m], send_sem=send.at[m - 1], recv_sem=recv.at[m - 1],
                                         device_id=(x, y, c), device_id_type=MESH).wait_recv()
        for cp in cps:
            cp.wait_send()
        buf[me] = v_ref[...]
        acc = buf[0]
        for i in range(1, N_DEV):
            acc = acc + buf[i]
        o_ref[...] = acc

    body, dep_spec, dep_arg = _ordered(body, 1, dep)
    return pl.pallas_call(
        body, name="allreduce_small", out_shape=_sds((rows, W), F32), in_specs=[pl.BlockSpec(memory_space=pltpu.VMEM)] + dep_spec,
        scratch_shapes=[pltpu.VMEM((N_DEV, rows, W), F32), pltpu.SemaphoreType.DMA((N_DEV - 1,)), pltpu.SemaphoreType.DMA((N_DEV - 1,))],
        compiler_params=_params())(v, *dep_arg)


def _adamw_math(w, g, m, v):
    c1 = 1.0 / (1.0 - ADAM_B1 ** ADAM_STEP)
    c2 = 1.0 / (1.0 - ADAM_B2 ** ADAM_STEP)
    nm = ADAM_B1 * m + (1.0 - ADAM_B1) * g
    nv = ADAM_B2 * v + (1.0 - ADAM_B2) * (g * g)
    return -ADAM_LR * ((nm * c1) / (jnp.sqrt(nv * c2) + ADAM_EPS) + ADAM_WD * w), nm, nv


def _adamw_small(rows, params):
    n = len(params)
    n_sink = params[-1][0].shape[1]

    def body(rows_ref, *refs):
        ins, outs = refs[:3 * n], refs[3 * n:]
        for j in range(n):
            g = rows_ref[j:j + 1, 0:n_sink] if j == n - 1 else rows_ref[j:j + 1, :]
            d, nm, nv = _adamw_math(ins[3 * j][...], g, ins[3 * j + 1][...], ins[3 * j + 2][...])
            for ref, val in zip(outs[4 * j:4 * j + 4], (g, d, nm, nv)):
                ref[...] = val
        outs[-1][...] = rows_ref[n - 1:n, n_sink:n_sink + 1]

    flat = [a for p in params for a in p]
    outs = pl.pallas_call(body, name="adamw_small", out_shape=[_sds(p[0].shape, F32) for p in params for _ in range(4)] + [_sds((1, 1), F32)],
                          compiler_params=_params())(rows, *flat)
    return [outs[4 * j:4 * j + 4] for j in range(n)], outs[-1]


def _adamw(w, gp, gq, m, v, name):
    R, C = w.shape
    tr = R // 2 if (R // 2) % SUBLANES == 0 else R

    def body(w_ref, gp_ref, gq_ref, m_ref, v_ref, g_ref, d_ref, nm_ref, nv_ref):
        gv = gp_ref[...] + gq_ref[...]
        g_ref[...] = gv
        d_ref[...], nm_ref[...], nv_ref[...] = _adamw_math(w_ref[...], gv, m_ref[...], v_ref[...])

    blk = pl.BlockSpec((tr, C), lambda t: (t, 0))
    return pl.pallas_call(body, name=name, grid=(R // tr,), in_specs=[blk] * 5, out_specs=[blk] * 4,
                          out_shape=[_sds((R, C), F32)] * 4, compiler_params=_params(("parallel",)))(w, gp, gq, m, v)


def _rope(positions, after):
    inv_freq = 1.0 / (ROPE_THETA ** (jnp.arange(0, HEAD_DIM, 2, dtype=F32) / HEAD_DIM))
    inv_freq = jnp.tile(inv_freq, LANES // (HEAD_DIM // 2)).reshape(1, LANES) + after[0, 0]
    return _rope_tables(positions.reshape(-1, 1), inv_freq)


def _local_step(x, rope, target, norms, a_sink, comm):
    T, D = x.shape
    g1, gm, g2, gf = norms
    cos, sin = rope
    no_sink = jnp.zeros((2 * (B_W // LANES),), F32)
    W = {k: comm.weight(k, x) for k in ("wg1", "wu1", "wd1")}

    x1, h1, gate1, up1, act1 = _ffn_fwd(x, g1, W["wg1"], W["wu1"], W["wd1"], "ffn1_fwd", dep=comm.dep())
    W["w_in"] = comm.weight("w_in", x1)
    (h2, aq, ak, av, bq1, bk1, bv1, bq4, bk4, bv4, bq16, bk16, bv16) = _proj_rope(x1, gm, W["w_in"], cos, sin)
    cat, a_lse = _attn_fwd(aq[None], ak[None], av[None], a_sink, A_HALF_WINDOW, True, BF16, "attn_a_fwd", qb=2 * QB, blocks_per_step=4,
                           out_cols=A_Q_W + B_W)
    bqs = {1: (bq1[None], bk1[None], bv1[None]), 4: (bq4, bk4, bv4), 16: (bq16, bk16, bv16)}
    (b_hw,) = {w // (2 * d) for w, d in B_PATTERNS}
    cat, lg1, lg4, lg16 = _dilated_fwd(cat[0], bqs, b_hw)
    lg1 = lg1[0]
    W["w_out"] = comm.weight("w_out", cat)
    x2 = _out_proj(x1, cat, W["w_out"])
    for k in ("wg2", "wu2", "wd2"):
        W[k] = comm.weight(k, x2)
    x3, h3, gate2, up2, act2 = _ffn_fwd(x2, g2, W["wg2"], W["wu2"], W["wd2"], "ffn2_fwd")

    dx3, dgf, loss8 = _final_loss(x3, gf, target)
    dx2, dff2, dgate2, dup2, dg2 = _ffn_dx(dx3, x2, g2, gate2, up2, W["wg2"], W["wu2"], W["wd2"], "ffn2_dx")
    fb = gate2.shape[1] // 2
    dwg2 = _tn(dgate2, h3, fb, "ffn2_dw_gate")
    dwu2 = _tn(dup2, h3, fb, "ffn2_dw_up")
    dwd2 = _tn(act2, dff2, fb, "ffn2_dw_down")
    comm.ready(dict(wg2=dwg2, wu2=dwu2, wd2=dwd2), dwd2[0])

    doa, dla, dob1, dlb1, dob4, dlb4, dob16, dlb16 = _dcat(dx2, W["w_out"], cat)
    dw_out = _tn(cat, dx2, cat.shape[1], "w_out_dw", dep=comm.dep())
    dqa, dka, dva, dsk = _attn_bwd(aq[None], ak[None], av[None], doa[None], a_lse, dla[None], a_sink, A_HALF_WINDOW, True, "attn_a_bwd",
                                   dep=comm.dep())
    bwd_in = {1: (dob1[None], lg1[None], dlb1[None]), 4: (dob4, lg4, dlb4), 16: (dob16, lg16, dlb16)}
    bg = {}
    for w, d in B_PATTERNS:
        q_, k_, v_ = bqs[d]
        do_, l_, dl_ = bwd_in[d]
        bg[d] = _attn_bwd(q_, k_, v_, do_, l_, dl_, no_sink, w // (2 * d), False, f"attn_b{d}_bwd")[:3]
    dproj = _rope_bwd_assemble(dqa[0], dka[0], dva[0], [t[0] for t in bg[1]], bg[4], bg[16], cos, sin)
    dw_in = _tn(dproj, h2, dproj.shape[1] // 2, "w_in_dw")
    comm.ready(dict(w_in=dw_in, w_out=dw_out), dw_in[0])
    dx1, dgm = _dh_norm(dproj, W["w_in"], x1, gm, dx2, dep=comm.dep())

    dx0, dff1, dgate1, dup1, dg1 = _ffn_dx(dx1, x, g1, gate1, up1, W["wg1"], W["wu1"], W["wd1"], "ffn1_dx")
    comm.settle(2, dx0)
    dwd1 = _tn(act1, dff1, fb, "ffn1_dw_down", dep=comm.dep())
    comm.ready(dict(wd1=dwd1), dwd1[0])
    dwg1 = _tn(dgate1, h1, fb, "ffn1_dw_gate", dep=comm.dep())
    comm.ready(dict(wg1=dwg1), dwg1[0])
    dwu1 = _tn(dup1, h1, fb, "ffn1_dw_up", dep=comm.dep())
    comm.ready(dict(wu1=dwu1), dwu1[0])

    dsink = dsk[0, :, :, ::HEAD_DIM].sum(axis=1).reshape(-1)
    small = dict(g1=dg1.sum(axis=0), gm=dgm.sum(axis=0), g2=dg2.sum(axis=0), gf=dgf.sum(axis=0), sink=dsink, loss=loss8[0, 0])
    return dx0, small


BIG = ("wg1", "wu1", "wd1", "w_in", "w_out", "wg2", "wu2", "wd2")
GATHER_GROUPS = (("w_in",), ("w_out",), ("wg2", "wu2", "wd2"))


class _Comm:
    def __init__(self, shards, meanwhile):
        x, y, c = _mesh_pos()
        self.me = (2 * x + y).astype(jnp.int32).reshape(1)
        self.shards = shards
        self.token = None
        self.waiting = {}
        self.groups = []
        self.swaps = []
        first = ("wg1", "wu1", "wd1")
        fulls = {k: _cast_place(self.me, shards[k], f"cast_{k}") for k in first}
        plan = _neighbour_plan([fulls[k].shape for k in first])
        send, recv, bufs, tok = _push_start("gather_first_start", [fulls[k] for k in first], 2 * len(first), plan, self.me)
        self.side = meanwhile(tok)
        fulls.update({k: _cast_place(self.me, shards[k], f"cast_{k}") for k in BIG if k not in first})
        bufs = _push_wait("gather_first_wait", send, recv, bufs, plan, [fulls[k] for k in BIG if k not in first] + list(self.side))
        self.full = dict(zip(first, _gather_forward(bufs)))
        dep = self.full["wd1"]
        for gi, names in enumerate(GATHER_GROUPS):
            plan = _gather_plan(len(names))
            send, recv, bufs, self.token = _push_start(f"gather_start_{gi}", [fulls[k] for k in names], 3 * len(names), plan, dep)
            dep = self.token
            for k in names:
                self.waiting[k] = (gi, names, send, recv, bufs, plan)

    def dep(self):
        return self.token

    def weight(self, name, after):
        if name in self.waiting:
            gi, names, send, recv, bufs, plan = self.waiting[name]
            for k, buf in zip(names, _push_wait(f"gather_wait_{gi}", send, recv, bufs, plan, after)):
                self.full[k] = buf
                del self.waiting[k]
        full = self.full[name]
        return full.reshape(N_CHIPS * full.shape[1], full.shape[2])

    def ready(self, grads, after):
        names = list(grads)
        f32s, b16s = [], []
        for k in names:
            gf, gb = grads[k]
            f32s.append(gf.reshape((N_CHIPS,) + self.shards[k].shape))
            b16s.append(gb.reshape((N_CHIPS,) + self.shards[k].shape))
        n = len(names)
        lands = [lax.empty((N_CHIPS - 1,) + self.shards[k].shape, BF16) for k in names]
        plan = _grad_push_plan(n)
        send, recv, bufs, self.token = _push_start(f"grad_start_{names[0]}", b16s + lands, 3 * n, plan, after)
        self.groups.append((names, f32s, send, recv, bufs, plan))

    def settle(self, count, after):
        batch, self.groups = self.groups[:count], self.groups[count:]
        names_b, mine_b = [], []
        for names, f32s, send, recv, bufs, plan in batch:
            n = len(names)
            bufs = _push_wait(f"grad_wait_{names[0]}", send, recv, bufs, plan, mine_b[-1] if mine_b else after)
            mine_b += [_sum_own(self.me, f32s[i], bufs[n + i], f"sum_{k}") for i, k in enumerate(names)]
            names_b += names
        lands = [lax.empty(p.shape, F32) for p in mine_b]
        n = len(names_b)
        send2, recv2, both, self.token = _push_start(f"swap_start_{names_b[0]}", mine_b + lands, n, _swap_plan(n), after)
        self.swaps.append((names_b, send2, recv2, both))

    def partials(self, after):
        names_b, send2, recv2, both = self.swaps.pop(0)
        n = len(names_b)
        both = _push_wait(f"swap_wait_{names_b[0]}", send2, recv2, both, _swap_plan(n), after)
        return {k: (both[i], both[n + i]) for i, k in enumerate(names_b)}


def kernel(x, positions, norm_ffn1, w_gate1, w_up1, w_down1, norm_mix, w_in, a_sink, w_out, norm_ffn2, w_gate2, w_up2, w_down2, norm_final, loss_target, m_norm_ffn1, m_w_gate1, m_w_up1, m_w_down1, m_norm_mix, m_w_in, m_a_sink, m_w_out, m_norm_ffn2, m_w_gate2, m_w_up2, m_w_down2, m_norm_final, v_norm_ffn1, v_w_gate1, v_w_up1, v_w_down1, v_norm_mix, v_w_in, v_a_sink, v_w_out, v_norm_ffn2, v_w_gate2, v_w_up2, v_w_down2, v_norm_final):
    T, D = x.shape[1], x.shape[2]
    flip = ("wg1", "wu1", "w_in", "wg2", "wu2")

    def rows(k, a):
        return a[0].T if k in flip else a[0]

    given = dict(wg1=(w_gate1, m_w_gate1, v_w_gate1), wu1=(w_up1, m_w_up1, v_w_up1), wd1=(w_down1, m_w_down1, v_w_down1),
                 w_in=(w_in, m_w_in, v_w_in), w_out=(w_out, m_w_out, v_w_out), wg2=(w_gate2, m_w_gate2, v_w_gate2),
                 wu2=(w_up2, m_w_up2, v_w_up2), wd2=(w_down2, m_w_down2, v_w_down2))
    shards = {k: rows(k, given[k][0]) for k in BIG}

    comm = _Comm(shards, lambda tok: _rope(positions[0], tok))

    norms = (norm_ffn1, norm_mix, norm_ffn2, norm_final.reshape(1, D))
    grad_x, small = _local_step(x[0], comm.side, loss_target[0], norms, a_sink[0], comm)

    upd = {}

    def update(partial):
        for k in partial:
            outs = _adamw(shards[k], partial[k][0], partial[k][1], rows(k, given[k][1]), rows(k, given[k][2]), f"adamw_{k}")
            upd[k] = tuple((a.T if k in flip else a)[None] for a in outs)
        return outs[0]

    last = update(comm.partials(comm.dep()))
    comm.settle(2, last)

    def pad_row(a):
        a = a.reshape(-1)
        return jnp.pad(a, (0, D - a.shape[0]))

    row4 = pad_row(jnp.concatenate([small["sink"], small["loss"].reshape(1)]))
    vec = jnp.stack([small["g1"], small["gm"], small["g2"], small["gf"], row4] + [jnp.zeros((D,), F32)] * 3, axis=0)
    red = _allreduce_small(vec, comm.dep())
    comm.settle(1, red)
    last = update(comm.partials(comm.dep()))
    update(comm.partials(last))
    as_row = lambda a: a.reshape(1, -1)
    sm, loss = _adamw_small(red, [tuple(as_row(a) for a in p) for p in (
        (norm_ffn1, m_norm_ffn1, v_norm_ffn1), (norm_mix, m_norm_mix, v_norm_mix), (norm_ffn2, m_norm_ffn2, v_norm_ffn2),
        (norm_final, m_norm_final, v_norm_final), (a_sink, m_a_sink, v_a_sink))])
    sm[3] = [a.reshape(D) for a in sm[3]]

    def ordered(i):
        return [sm[0][i], upd["wg1"][i], upd["wu1"][i], upd["wd1"][i], sm[1][i], upd["w_in"][i], sm[4][i], upd["w_out"][i], sm[2][i],
                upd["wg2"][i], upd["wu2"][i], upd["wd2"][i], sm[3][i]]

    return (loss.reshape(()), grad_x[None], *ordered(0), *ordered(1), *ordered(2), *ordered(3))
```

```python
import jax
import jax.numpy as jnp
from jax import lax
from jax.experimental import pallas as pl
from jax.experimental.pallas import tpu as pltpu

F32 = jnp.float32
BF16 = jnp.bfloat16

HEAD_DIM = 64
LANES = 128
SUBLANES = 8
A_Q_W, A_KV_W, B_W = 512, 128, 512
A_HALF_WINDOW = 128
B_PATTERNS = ((128, 1), (512, 4), (2048, 16))
ROPE_THETA = 10000.0
NORM_EPS = 1e-6
FFN_RES_WEIGHT = 0.5
ADAM_LR, ADAM_B1, ADAM_B2, ADAM_EPS, ADAM_WD, ADAM_STEP = 0.001, 0.9, 0.999, 1e-08, 0.01, 10
N_CHIPS = 4
N_DEV = 8
QB = 128
SHORT_SEQ = 512
NEG = -1e30
VMEM_LIMIT = 56 * 1024 * 1024
MESH = pl.DeviceIdType.MESH
ANY = pl.BlockSpec(memory_space=pl.ANY)


def _params(sem=None):
    return pltpu.CompilerParams(dimension_semantics=sem, vmem_limit_bytes=VMEM_LIMIT)


def _sds(shape, dtype):
    return jax.ShapeDtypeStruct(tuple(shape), dtype)


def _dot(a, b):
    return jnp.dot(a, b, preferred_element_type=F32)


def _dot_nt(a, b):
    return lax.dot_general(a, b, (((1,), (1,)), ((), ())), preferred_element_type=F32)


def _dot_tn(a, b):
    return lax.dot_general(a, b, (((0,), (0,)), ((), ())), preferred_element_type=F32)


def _rms_stats(x):
    r = lax.rsqrt(jnp.mean(x * x, axis=-1, keepdims=True) + NORM_EPS)
    return x * r, r


def _rms_bwd(dh, x, g):
    xhat, r = _rms_stats(x)
    dxn = dh * g
    dx = r * (dxn - xhat * jnp.mean(dxn * xhat, axis=-1, keepdims=True))
    tm, d = x.shape
    dg = (dh * xhat).reshape(tm // SUBLANES, SUBLANES, d).sum(axis=0)
    return dx, dg


def _sigmoid(x):
    return 1.0 / (1.0 + jnp.exp(-x))


def _swap32(t):
    n = t.shape[-1]
    lane = lax.broadcasted_iota(jnp.int32, t.shape, t.ndim - 1)
    return jnp.where((lane % HEAD_DIM) < HEAD_DIM // 2, pltpu.roll(t, n - HEAD_DIM // 2, axis=t.ndim - 1),
                     pltpu.roll(t, HEAD_DIM // 2, axis=t.ndim - 1))


def _ordered(body, n_in, dep):
    if dep is None:
        return body, [], []

    def ordered(*refs):
        body(*refs[:n_in], *refs[n_in + 1:])

    return ordered, [ANY], [dep]


def _cast_place(me_arr, w, name):
    R, C = w.shape
    tr = R // 2 if (R // 2) % 16 == 0 else R

    def body(me_ref, w_ref, o_ref):
        o_ref[...] = w_ref[...].astype(BF16)

    grid_spec = pltpu.PrefetchScalarGridSpec(
        num_scalar_prefetch=1, grid=(R // tr,), in_specs=[pl.BlockSpec((tr, C), lambda t, me: (t, 0))],
        out_specs=pl.BlockSpec((None, tr, C), lambda t, me: (me[0], t, 0)))
    return pl.pallas_call(body, name=name, grid_spec=grid_spec, out_shape=_sds((N_CHIPS, R, C), BF16),
                          compiler_params=_params(("parallel",)))(me_arr, w)


HBM = pl.BlockSpec(memory_space=pltpu.HBM)
SEM = pl.BlockSpec(memory_space=pltpu.SEMAPHORE)


def _push_start(name, bufs, ncopies, plan, after):
    nb = len(bufs)

    def body(*refs):
        send, recv, token = refs[nb + 1], refs[nb + 2], refs[-1]
        for i, (src, dst, dev) in enumerate(plan(refs[:nb])):
            pltpu.make_async_remote_copy(src_ref=src, dst_ref=dst, send_sem=send.at[i], recv_sem=recv.at[i],
                                         device_id=dev, device_id_type=MESH).start()
        token[...] = jnp.zeros_like(token)

    outs = pl.pallas_call(
        body, name=name,
        out_shape=(pltpu.SemaphoreType.DMA((ncopies,)), pltpu.SemaphoreType.DMA((ncopies,)), *[pltpu.HBM(b.shape, b.dtype) for b in bufs],
                   _sds((SUBLANES, LANES), F32)),
        in_specs=[HBM] * nb + [ANY], out_specs=(SEM, SEM, *([HBM] * nb), pl.BlockSpec(memory_space=pltpu.VMEM)),
        input_output_aliases={i: 2 + i for i in range(nb)},
        compiler_params=pltpu.CompilerParams(has_side_effects=pltpu.SideEffectType.DATAFLOW_SIDE_EFFECTING),
    )(*[pltpu.with_memory_space_constraint(b, pltpu.HBM) for b in bufs], after)
    return outs[0], outs[1], list(outs[2:2 + nb]), outs[-1]


def _push_wait(name, send, recv, bufs, plan, after):
    nb = len(bufs)

    def body(*refs):
        send_ref, recv_ref = refs[nb], refs[nb + 1]
        for i, (src, dst, dev) in enumerate(plan(refs[:nb])):
            cp = pltpu.make_async_remote_copy(src_ref=src, dst_ref=dst, send_sem=send_ref.at[i], recv_sem=recv_ref.at[i],
                                              device_id=dev, device_id_type=MESH)
            cp.wait_send()
            cp.wait_recv()

    afters = list(after) if isinstance(after, (list, tuple)) else [after]
    outs = pl.pallas_call(
        body, name=name, out_shape=tuple(pltpu.HBM(b.shape, b.dtype) for b in bufs),
        in_specs=[HBM] * nb + [SEM, SEM] + [ANY] * len(afters), out_specs=tuple([HBM] * nb),
        input_output_aliases={i: i for i in range(nb)},
        compiler_params=pltpu.CompilerParams(has_side_effects=pltpu.SideEffectType.DATAFLOW_SIDE_EFFECTING),
    )(*bufs, send, recv, *afters)
    return list(outs)


def _mesh_pos():
    return lax.axis_index("x"), lax.axis_index("y"), lax.axis_index("c")


def _chip_peers(x, y, c):
    return [((1 - x, y, c), 2 * (1 - x) + y), ((x, 1 - y, c), 2 * x + (1 - y)), ((1 - x, 1 - y, c), 2 * (1 - x) + (1 - y))]


def _gather_plan(n):
    def plan(refs):
        x, y, c = _mesh_pos()
        me = 2 * x + y
        return [(refs[k].at[me], refs[k].at[me], dev) for k in range(n) for dev, _ in _chip_peers(x, y, c)]
    return plan


def _rows_of(shape, who, quarter=None):
    r2 = shape[1] // 2
    if quarter is None:
        return pl.ds(pl.multiple_of(who * r2, 16), r2)
    return pl.ds(pl.multiple_of(who * r2 + quarter * (r2 // 2), 16), r2 // 2)


def _neighbour_plan(shapes):
    def plan(refs):
        x, y, c = _mesh_pos()
        me = 2 * x + y
        return [(refs[k].at[me, _rows_of(shp, c), :], refs[k].at[me, _rows_of(shp, c), :], dev)
                for k, shp in enumerate(shapes) for dev in ((1 - x, y, c), (x, 1 - y, c))]
    return plan


def _gather_forward(fulls):
    n = len(fulls)

    def body(*refs):
        ins, outs = refs[:n], refs[n:2 * n]
        ici_send, ici_recv, d2d_send, d2d_recv = refs[2 * n:]
        x, y, c = _mesh_pos()
        cx, cy, cd = 2 * (1 - x) + y, 2 * x + (1 - y), 2 * (1 - x) + (1 - y)
        sibling, x_nbr, y_nbr = (x, y, 1 - c), (1 - x, y, c), (x, 1 - y, c)
        started = []

        def push(src, dst, send, recv, dev):
            cp = pltpu.make_async_remote_copy(src_ref=src, dst_ref=dst, send_sem=send, recv_sem=recv, device_id=dev, device_id_type=MESH)
            cp.start()
            started.append(cp)

        def arrived(blk, send, recv):
            pltpu.make_async_remote_copy(src_ref=blk, dst_ref=blk, send_sem=send, recv_sem=recv, device_id=sibling,
                                         device_id_type=MESH).wait_recv()

        for k in range(n):
            shp = fulls[k].shape
            for j, chip in enumerate((cx, cy)):
                push(ins[k].at[chip, _rows_of(shp, c), :], outs[k].at[chip, _rows_of(shp, c), :],
                     d2d_send.at[3 * k + j], d2d_recv.at[3 * k + j], sibling)
            push(ins[k].at[cx, _rows_of(shp, c, 0), :], outs[k].at[cx, _rows_of(shp, c, 0), :], ici_send.at[2 * k], ici_recv.at[2 * k], y_nbr)
            push(ins[k].at[cy, _rows_of(shp, c, 1), :], outs[k].at[cy, _rows_of(shp, c, 1), :], ici_send.at[2 * k + 1], ici_recv.at[2 * k + 1],
                 x_nbr)
        for k in range(n):
            shp = fulls[k].shape
            for q in (0, 1):
                arrived(outs[k].at[cd, _rows_of(shp, c, q), :], ici_send.at[2 * k + q], ici_recv.at[2 * k + q])
            blk = outs[k].at[cd, _rows_of(shp, c), :]
            push(blk, blk, d2d_send.at[3 * k + 2], d2d_recv.at[3 * k + 2], sibling)
        for k in range(n):
            for j, chip in enumerate((cx, cy, cd)):
                arrived(outs[k].at[chip, _rows_of(fulls[k].shape, 1 - c), :], d2d_send.at[3 * k + j], d2d_recv.at[3 * k + j])
        for cp in started:
            cp.wait_send()

    return pl.pallas_call(
        body, name="gather_forward", out_shape=[_sds(f.shape, BF16) for f in fulls],
        in_specs=[ANY] * n, out_specs=[ANY] * n, input_output_aliases={k: k for k in range(n)},
        scratch_shapes=[pltpu.SemaphoreType.DMA((n * 2,))] * 2 + [pltpu.SemaphoreType.DMA((n * 3,))] * 2,
        compiler_params=_params())(*fulls)


def _resident(shape):
    return pl.BlockSpec(shape, lambda i: (0,) * len(shape), pipeline_mode=pl.Buffered(1))


FFN_FWD_CHUNK = 256
FFN_DX_CHUNK = 512


def _chunks(n, step):
    return [(c0, min(step, n - c0)) for c0 in range(0, n, step)]


def _two_phase(chunks, first, second):
    held = {}
    for ci, ch in enumerate(chunks):
        held[ci] = first(*ch)
        if ci >= 1:
            second(*chunks[ci - 1], held.pop(ci - 1))
    last = len(chunks) - 1
    second(*chunks[last], held.pop(last))


def _loss_and_grad(x, g, target):
    D = x.shape[1]
    xhat, _ = _rms_stats(x)
    err = xhat * g - target
    loss = 0.5 * jnp.sum(jnp.sum(err * err, axis=-1, keepdims=True) * (1.0 / D), axis=0, keepdims=True)
    dx, dg = _rms_bwd(err * (1.0 / D), x, g)
    return dx, dg, loss


def _ffn_fwd(x, g, wgt, wut, wd, name, tm=512, dep=None, loss=None):
    T, D = x.shape
    F = wd.shape[0]
    n_in = 5 if loss is None else 7

    def body(*refs):
        x_ref, g_ref, wg_ref, wu_ref, wd_ref = refs[:5]
        xo_ref, h_ref, gate_ref, up_ref, act_ref = refs[n_in:n_in + 5]
        xv = x_ref[...]
        xhat, _ = _rms_stats(xv)
        h = (xhat * g_ref[...]).astype(BF16)
        h_ref[...] = h
        acc = []

        def first(c0, cw):
            return _dot_nt(h, wg_ref[c0:c0 + cw, :]), _dot_nt(h, wu_ref[c0:c0 + cw, :])

        def second(c0, cw, gate_up):
            gate, up = gate_up
            act = ((gate * _sigmoid(gate)) * up).astype(BF16)
            gate_ref[:, c0:c0 + cw] = gate.astype(BF16)
            up_ref[:, c0:c0 + cw] = up.astype(BF16)
            act_ref[:, c0:c0 + cw] = act
            d = _dot(act, wd_ref[c0:c0 + cw, :])
            acc[:] = [d if not acc else acc[0] + d]

        _two_phase(_chunks(F, FFN_FWD_CHUNK), first, second)
        xo = xv + FFN_RES_WEIGHT * acc[0]
        if loss is None:
            xo_ref[...] = xo
        else:
            gf_ref, t_ref = refs[5:7]
            dgf_ref, loss_ref = refs[n_in + 5:]

            @pl.when(pl.program_id(0) == 0)
            def _():
                dgf_ref[...] = jnp.zeros_like(dgf_ref)
                loss_ref[...] = jnp.zeros_like(loss_ref)

            xo_ref[...], dgf, part = _loss_and_grad(xo, gf_ref[...], t_ref[...])
            dgf_ref[...] += dgf
            loss_ref[...] += part

    row = pl.BlockSpec((tm, D), lambda i: (i, 0))
    gain = pl.BlockSpec((1, D), lambda i: (0, 0))
    saved = pl.BlockSpec((tm, F), lambda i: (i, 0))
    in_specs = [row, gain, _resident(wgt.shape), _resident(wut.shape), _resident(wd.shape)]
    out_specs = [row, row, saved, saved, saved]
    out_shape = [_sds((T, D), F32), _sds((T, D), BF16), _sds((T, F), BF16), _sds((T, F), BF16), _sds((T, F), BF16)]
    if loss is not None:
        in_specs += [gain, row]
        out_specs += [pl.BlockSpec((SUBLANES, D), lambda i: (0, 0)), pl.BlockSpec((SUBLANES, LANES), lambda i: (0, 0))]
        out_shape += [_sds((SUBLANES, D), F32), _sds((SUBLANES, LANES), F32)]
    body, dep_spec, dep_arg = _ordered(body, n_in, dep)
    return pl.pallas_call(
        body, name=name, grid=(T // tm,), in_specs=in_specs + dep_spec, out_specs=out_specs, out_shape=out_shape,
        compiler_params=_params(("parallel",) if loss is None else ("arbitrary",)))(x, g, wgt, wut, wd, *(loss or ()), *dep_arg)


def _ffn_dx(dxo, x, g, gate_s, up_s, wgt, wut, wd, name, tm=256):
    T, D = x.shape
    F = wd.shape[0]

    def body(dxo_ref, x_ref, g_ref, gate_ref, up_ref, wg_ref, wu_ref, wd_ref, dx_ref, dff_ref, dgate_ref, dup_ref, dg_ref):
        @pl.when(pl.program_id(0) == 0)
        def _():
            dg_ref[...] = jnp.zeros_like(dg_ref)

        d = (FFN_RES_WEIGHT * dxo_ref[...]).astype(BF16)
        dff_ref[...] = d
        dh = []

        def first(c0, cw):
            return _dot_nt(d, wd_ref[c0:c0 + cw, :])

        def second(c0, cw, da):
            gate = gate_ref[:, c0:c0 + cw].astype(F32)
            up = up_ref[:, c0:c0 + cw].astype(F32)
            s = _sigmoid(gate)
            silu = gate * s
            dup = (da * silu).astype(BF16)
            dgate = (da * up * (s * (1.0 + gate * (1.0 - s)))).astype(BF16)
            dgate_ref[:, c0:c0 + cw] = dgate
            dup_ref[:, c0:c0 + cw] = dup
            t = _dot(dgate, wg_ref[c0:c0 + cw, :]) + _dot(dup, wu_ref[c0:c0 + cw, :])
            dh[:] = [t if not dh else dh[0] + t]

        _two_phase(_chunks(F, FFN_DX_CHUNK), first, second)
        dxn, dg = _rms_bwd(dh[0], x_ref[...], g_ref[...])
        dg_ref[...] += dg
        dx_ref[...] = dxo_ref[...] + dxn

    row = pl.BlockSpec((tm, D), lambda i: (i, 0))
    saved = pl.BlockSpec((tm, F), lambda i: (i, 0))
    return pl.pallas_call(
        body, name=name, grid=(T // tm,),
        in_specs=[row, row, pl.BlockSpec((1, D), lambda i: (0, 0)), saved, saved, _resident(wgt.shape), _resident(wut.shape),
                  _resident(wd.shape)],
        out_specs=[row, row, saved, saved, pl.BlockSpec((SUBLANES, D), lambda i: (0, 0))],
        out_shape=[_sds((T, D), F32), _sds((T, D), BF16), _sds((T, F), BF16), _sds((T, F), BF16), _sds((SUBLANES, D), F32)],
        compiler_params=_params(("arbitrary",)))(dxo, x, g, gate_s, up_s, wgt, wut, wd)


def _tn(a, b, mb, name, tk=2048, dep=None):
    T, M = a.shape
    N = b.shape[1]
    nt = T // tk

    def body(a_ref, b_ref, o_ref, ob_ref):
        @pl.when(pl.program_id(1) == 0)
        def _():
            o_ref[...] = jnp.zeros_like(o_ref)

        o_ref[...] += _dot_tn(a_ref[...].astype(BF16), b_ref[...].astype(BF16))

        @pl.when(pl.program_id(1) == nt - 1)
        def _():
            ob_ref[...] = o_ref[...].astype(BF16)

    o_spec = pl.BlockSpec((mb, N), lambda g, t: (g, 0))
    body, dep_spec, dep_arg = _ordered(body, 2, dep)
    return pl.pallas_call(
        body, name=name, grid=(M // mb, nt),
        in_specs=[pl.BlockSpec((tk, mb), lambda g, t: (t, g)), pl.BlockSpec((tk, N), lambda g, t: (t, 0))] + dep_spec,
        out_specs=[o_spec, o_spec], out_shape=[_sds((M, N), F32), _sds((M, N), BF16)],
        compiler_params=_params(("parallel", "arbitrary")))(a, b, *dep_arg)


def _rope_tables(pos_col, inv_freq):
    T = pos_col.shape[0]

    def body(p_ref, f_ref, c_ref, s_ref):
        ang = p_ref[...].astype(F32) * f_ref[...]
        lane = lax.broadcasted_iota(jnp.int32, ang.shape, 1)
        c_ref[...] = jnp.cos(ang)
        sn = jnp.sin(ang)
        s_ref[...] = jnp.where((lane % HEAD_DIM) < HEAD_DIM // 2, -sn, sn)

    tm = 1024
    return pl.pallas_call(
        body, name="rope_tables", grid=(T // tm,),
        in_specs=[pl.BlockSpec((tm, 1), lambda i: (i, 0)), pl.BlockSpec((1, LANES), lambda i: (0, 0))],
        out_specs=[pl.BlockSpec((tm, LANES), lambda i: (i, 0))] * 2,
        out_shape=[_sds((T, LANES), F32)] * 2, compiler_params=_params(("parallel",)))(pos_col, inv_freq)


def _deinterleave(scr, out_ref, d, tm, nblk):
    for r in range(d):
        for cb in range(nblk):
            out_ref[r, :, cb * LANES:(cb + 1) * LANES] = scr[cb, pl.ds(r, tm // d, stride=d), :].astype(out_ref.dtype)


def _interleave(in_ref, scr, d, tm, nblk):
    for r in range(d):
        for cb in range(nblk):
            scr[cb, pl.ds(r, tm // d, stride=d), :] = in_ref[r, :, cb * LANES:(cb + 1) * LANES].astype(F32)


def _proj_rope(x, g, w_in, cos, sin, tm=512):
    T, D = x.shape
    dils = [d for _, d in B_PATTERNS if d > 1]
    nbb = B_W // LANES
    scale = HEAD_DIM ** -0.5
    cuts = [0, A_Q_W, A_Q_W + A_KV_W, A_Q_W + 2 * A_KV_W, A_Q_W + 2 * A_KV_W + B_W, A_Q_W + 2 * A_KV_W + 2 * B_W,
            A_Q_W + 2 * A_KV_W + 3 * B_W]

    def body(x_ref, g_ref, w_ref, c_ref, s_ref, h_ref, aq_ref, ak_ref, av_ref, *rest):
        b_refs, scr = rest[:-1], rest[-1]
        xhat, _ = _rms_stats(x_ref[...])
        h = (xhat * g_ref[...]).astype(BF16)
        h_ref[...] = h
        cs, sn = c_ref[...], s_ref[...]

        def project(idx, ref, rope, mult, which):
            return _dot_nt(h, w_ref[cuts[idx]:cuts[idx + 1], :])

        def finish(idx, ref, rope, mult, which, whole):
            for cb in range((cuts[idx + 1] - cuts[idx]) // LANES):
                p = whole[:, cb * LANES:(cb + 1) * LANES]
                if rope:
                    p = p * cs + _swap32(p) * sn
                if mult != 1.0:
                    p = p * mult
                ref[:, cb * LANES:(cb + 1) * LANES] = p.astype(BF16)
                if which is not None:
                    scr[which, cb] = p
            if which is not None:
                for di, d in enumerate(dils):
                    _deinterleave(scr.at[which], b_refs[3 * (di + 1) + which], d, tm, nbb)

        _two_phase([(0, aq_ref, True, scale, None), (1, ak_ref, True, 1.0, None), (2, av_ref, False, 1.0, None),
                    (3, b_refs[0], True, scale, 0), (4, b_refs[1], True, 1.0, 1), (5, b_refs[2], False, 1.0, 2)], project, finish)

    row = lambda w: pl.BlockSpec((tm, w), lambda i: (i, 0))
    out_specs = [row(D), row(A_Q_W), row(A_KV_W), row(A_KV_W)] + [row(B_W)] * 3
    out_shape = [_sds((T, D), BF16), _sds((T, A_Q_W), BF16), _sds((T, A_KV_W), BF16), _sds((T, A_KV_W), BF16)] + [_sds((T, B_W), BF16)] * 3
    for d in dils:
        out_specs += [pl.BlockSpec((d, tm // d, B_W), lambda i: (0, i, 0))] * 3
        out_shape += [_sds((d, T // d, B_W), BF16)] * 3
    return pl.pallas_call(
        body, name="proj_rope", grid=(T // tm,),
        in_specs=[row(D), pl.BlockSpec((1, D), lambda i: (0, 0)), pl.BlockSpec(w_in.shape, lambda i: (0, 0)), row(LANES), row(LANES)],
        out_specs=out_specs, out_shape=out_shape, scratch_shapes=[pltpu.VMEM((3, nbb, tm, LANES), F32)],
        compiler_params=_params(("parallel",)))(x, g, w_in, cos, sin)


def _band_bias(rel, qb, kw, hw):
    ri = lax.broadcasted_iota(jnp.int32, (2 * qb, kw), 0) & (qb - 1)
    ci = lax.broadcasted_iota(jnp.int32, (2 * qb, kw), 1)
    return jnp.where(jnp.abs(ri + rel - ci) <= hw, 0.0, NEG).astype(F32)


def _stack_heads(x, lo):
    z = jnp.zeros_like(x)
    return jnp.concatenate([jnp.where(lo, x, z), jnp.where(lo, z, x)], axis=0)


def _unstack_heads(y, lo):
    qb = y.shape[0] // 2
    return jnp.where(lo, y[:qb], y[qb:])


def _band_setup(bias_scr, qb, kw, hw):
    if bias_scr is not None:
        for i in range(3):
            bias_scr[i] = _band_bias(i * hw, qb, kw, hw)


def _band_window(bias_scr, qs, L, qb, kw, hw):
    ws = pl.multiple_of(jnp.clip(qs - hw, 0, L - kw), 64)
    if bias_scr is None:
        return ws, _band_bias(qs - ws, qb, kw, hw)
    return ws, bias_scr[lax.shift_right_logical(qs - ws, hw.bit_length() - 1)]


def _dup_kv_head(src_ref, dst_ref, head, L):
    step = min(L, 1024)
    for r0 in range(0, L, step):
        xf = src_ref[r0:r0 + step, :].astype(F32)
        lane = lax.broadcasted_iota(jnp.int32, xf.shape, 1)
        keep = jnp.logical_xor(lane < HEAD_DIM, head == 1)
        dst_ref[r0:r0 + step, :] = jnp.where(keep, xf, pltpu.roll(xf, HEAD_DIM, axis=1)).astype(dst_ref.dtype)


def _attn_fwd(q, k, v, sink, hw, gqa, out_dtype, name, qb=QB, blocks_per_step=8, out_cols=None):
    NB, L, Cq = q.shape
    Ls = min(L, 2048)
    kw = min(qb + 2 * hw, L)
    tables = L >= qb + 2 * hw
    unroll = min(blocks_per_step, Ls // qb)
    nlb = 1 if (gqa or L > SHORT_SEQ) else Cq // LANES

    def body(sink_ref, q_ref, k_ref, v_ref, o_ref, lse_ref, *scr):
        b, s_idx = pl.program_id(1), pl.program_id(2)
        bias_scr = scr[0] if tables else None
        _band_setup(bias_scr, qb, kw, hw)
        if gqa:
            kd, vd = scr[-2:]

            @pl.when(s_idx == 0)
            def _():
                _dup_kv_head(k_ref, kd, b // 2, L)
                _dup_kv_head(v_ref, vd, b // 2, L)
        else:
            kd, vd = k_ref, v_ref
        lane = lax.broadcasted_iota(jnp.int32, (qb, LANES), 1)
        lo = lane < HEAD_DIM
        if gqa:
            row = lax.broadcasted_iota(jnp.int32, (2 * qb, 1), 0)
            sk = jnp.where(row < qb, sink_ref[2 * b], sink_ref[2 * b + 1])

        def block(ql, col):
            qs = s_idx * Ls + ql
            ws, bias = _band_window(bias_scr, qs, L, qb, kw, hw)
            return ws, _dot_nt(_stack_heads(q_ref[pl.ds(ql, qb), col], lo), kd[pl.ds(ws, kw), col]) + bias

        def finish(ql, col, scores):
            ws, s = scores
            m = jnp.max(s, axis=-1, keepdims=True)
            if gqa:
                m = jnp.maximum(m, sk)
            p = jnp.exp(s - m)
            den = jnp.sum(p, axis=-1, keepdims=True)
            if gqa:
                den = den + jnp.exp(sk - m)
            o = _dot(p.astype(BF16), vd[pl.ds(ws, kw), col]) * (1.0 / den)
            o_ref[pl.ds(ql, qb), col] = _unstack_heads(o, lo).astype(o_ref.dtype)
            lse_ref[pl.ds(ql, qb), col] = _unstack_heads(m + jnp.log(den), lo)

        for lb in range(nlb):
            def step(n, carry, col=slice(lb * LANES, (lb + 1) * LANES)):
                _two_phase([(pl.multiple_of((n * unroll + u) * qb, qb), col) for u in range(unroll)], block, finish)
                return carry

            lax.fori_loop(0, Ls // (qb * unroll), step, 0)

    kv_map = (lambda r, b, s: (r, 0, 0)) if gqa else (lambda r, b, s: (r, 0, b))
    seg = pl.BlockSpec((None, Ls, nlb * LANES), lambda r, b, s: (r, s, b))
    return pl.pallas_call(
        body, name=name, grid=(NB, Cq // (nlb * LANES), L // Ls),
        in_specs=[pl.BlockSpec(memory_space=pltpu.SMEM), seg, pl.BlockSpec((None, L, nlb * LANES), kv_map),
                  pl.BlockSpec((None, L, nlb * LANES), kv_map)],
        out_specs=[seg, seg], out_shape=[_sds((NB, L, out_cols or Cq), out_dtype), _sds((NB, L, Cq), F32)],
        scratch_shapes=([pltpu.VMEM((3, 2 * qb, kw), F32)] if tables else []) + ([pltpu.VMEM((L, LANES), BF16)] * 2 if gqa else []),
        compiler_params=_params(("parallel", "parallel", "arbitrary")))(sink, q, k, v)


def _attn_bwd(q, k, v, do, lse, delta, sink, hw, gqa, name, qb=QB, blocks_per_step=8, dep=None):
    NB, L, Cq = q.shape
    Ck = k.shape[2]
    Ls = min(L, 2048)
    kw = min(qb + 2 * hw, L)
    reps = kw // LANES
    nseg = L // Ls
    scale = HEAD_DIM ** -0.5
    tables = L >= qb + 2 * hw
    unroll = min(blocks_per_step, Ls // qb)
    nlb = 1 if (gqa or L > SHORT_SEQ) else Cq // LANES

    def body(sink_ref, q_ref, do_ref, lse_ref, dl_ref, k_ref, v_ref, dq_ref, dk_ref, dv_ref, dsk_ref, *scr):
        b, s_idx = pl.program_id(1), pl.program_id(2)
        lane = lax.broadcasted_iota(jnp.int32, (qb, LANES), 1)
        lo = lane < HEAD_DIM
        bias_scr = scr[0] if tables else None
        _band_setup(bias_scr, qb, kw, hw)
        if gqa:
            kd, vd, dk_acc, dv_acc, dsk_acc = scr[-5:]

            @pl.when(s_idx == 0)
            def _():
                _dup_kv_head(k_ref, kd, b // 2, L)
                _dup_kv_head(v_ref, vd, b // 2, L)
                dk_acc[...] = jnp.zeros_like(dk_acc)
                dv_acc[...] = jnp.zeros_like(dv_acc)
                dsk_acc[...] = jnp.zeros_like(dsk_acc)

            @pl.when((s_idx == 0) & (b == 0))
            def _():
                dk_ref[...] = jnp.zeros_like(dk_ref)
                dv_ref[...] = jnp.zeros_like(dv_ref)
        else:
            kd, vd = k_ref, v_ref
            dk_acc, dv_acc = scr[-2:]

            @pl.when(s_idx == 0)
            def _():
                dk_acc[...] = jnp.zeros_like(dk_acc)
                dv_acc[...] = jnp.zeros_like(dv_acc)

        def block(ql, col):
            qs = s_idx * Ls + ql
            ws, bias = _band_window(bias_scr, qs, L, qb, kw, hw)
            qv, dov = q_ref[pl.ds(ql, qb), col], do_ref[pl.ds(ql, qb), col]
            lse, dl = lse_ref[pl.ds(ql, qb), col], dl_ref[pl.ds(ql, qb), col]
            kv_, vv = kd[pl.ds(ws, kw), col], vd[pl.ds(ws, kw), col]
            q2, do2 = _stack_heads(qv, lo), _stack_heads(dov, lo)
            return ws, q2, do2, lse, dl, _dot_nt(q2, kv_) + bias, _dot_nt(do2, vv)

        def finish(ql, col, held):
            ws, q2, do2, lse, dl, s, dp = held
            lse_sw, dl_sw = pltpu.roll(lse, HEAD_DIM, axis=1), pltpu.roll(dl, HEAD_DIM, axis=1)
            lse2 = jnp.concatenate([jnp.where(lo, lse, lse_sw), jnp.where(lo, lse_sw, lse)], axis=0)
            dl2 = jnp.concatenate([jnp.where(lo, dl, dl_sw), jnp.where(lo, dl_sw, dl)], axis=0)
            p = jnp.exp(s - jnp.tile(lse2, (1, reps)))
            ds = (p * (dp - jnp.tile(dl2, (1, reps)))).astype(BF16)
            dq_ref[pl.ds(ql, qb), col] = (_unstack_heads(_dot(ds, kd[pl.ds(ws, kw), col]), lo) * scale).astype(dq_ref.dtype)
            both = _dot_tn(jnp.concatenate([ds, p.astype(BF16)], axis=1), jnp.concatenate([q2, do2], axis=1))
            dk_acc[pl.ds(ws, kw), col] += both[:kw, :LANES]
            dv_acc[pl.ds(ws, kw), col] += both[kw:, LANES:]
            if gqa:
                sk = jnp.where(lo, sink_ref[2 * b], sink_ref[2 * b + 1])
                dsk_acc[...] += -jnp.exp(sk - lse) * dl

        for lb in range(nlb):
            def step(n, carry, col=slice(lb * LANES, (lb + 1) * LANES)):
                _two_phase([(pl.multiple_of((n * unroll + u) * qb, qb), col) for u in range(unroll)], block, finish)
                return carry

            lax.fori_loop(0, Ls // (qb * unroll), step, 0)

        if gqa:
            @pl.when(s_idx == nseg - 1)
            def _():
                step_rows = min(L, 1024)
                for r0 in range(0, L, step_rows):
                    lanek = lax.broadcasted_iota(jnp.int32, (step_rows, LANES), 1)
                    mine = jnp.logical_xor(lanek < HEAD_DIM, (b // 2) == 1)
                    for acc, ref in ((dk_acc, dk_ref), (dv_acc, dv_ref)):
                        a = acc[r0:r0 + step_rows, :]
                        ref[r0:r0 + step_rows, :] += jnp.where(mine, a + pltpu.roll(a, HEAD_DIM, axis=1), 0.0)
                dsk_ref[...] = dsk_acc[...].reshape(qb // SUBLANES, SUBLANES, LANES).sum(axis=0)
        else:
            dsk_ref[...] = jnp.zeros_like(dsk_ref)

            @pl.when(s_idx == nseg - 1)
            def _():
                dk_ref[...] = dk_acc[...].astype(dk_ref.dtype)
                dv_ref[...] = dv_acc[...].astype(dv_ref.dtype)

    kv_map = (lambda r, b, s: (r, 0, 0)) if gqa else (lambda r, b, s: (r, 0, b))
    seg = pl.BlockSpec((None, Ls, nlb * LANES), lambda r, b, s: (r, s, b))
    full = pl.BlockSpec((None, L, nlb * LANES), kv_map)
    scratch = [pltpu.VMEM((3, 2 * qb, kw), F32)] if tables else []
    if gqa:
        scratch += [pltpu.VMEM((L, LANES), BF16)] * 2 + [pltpu.VMEM((L, LANES), F32)] * 2 + [pltpu.VMEM((qb, LANES), F32)]
    else:
        scratch += [pltpu.VMEM((L, nlb * LANES), F32)] * 2
    kv_dtype = F32 if gqa else BF16
    body, dep_spec, dep_arg = _ordered(body, 7, dep)
    return pl.pallas_call(
        body, name=name, grid=(NB, Cq // (nlb * LANES), nseg),
        in_specs=[pl.BlockSpec(memory_space=pltpu.SMEM), seg, seg, seg, seg, full, full] + dep_spec,
        out_specs=[seg, full, full, pl.BlockSpec((None, None, SUBLANES, LANES), lambda r, b, s: (r, b, 0, 0))],
        out_shape=[_sds((NB, L, Cq), BF16), _sds((NB, L, Ck), kv_dtype), _sds((NB, L, Ck), kv_dtype),
                   _sds((NB, Cq // LANES, SUBLANES, LANES), F32)],
        scratch_shapes=scratch,
        compiler_params=_params(("arbitrary", "arbitrary", "arbitrary")))(sink, q, do, lse, delta, k, v, *dep_arg)


def _dilated_fwd(cat, qkv, hw, tile=2048):
    T = cat.shape[0]
    dils = sorted(qkv)
    nbb, na = B_W // LANES, A_Q_W // LANES
    qb, kw = QB, QB + 2 * hw
    rows_merge = 256
    assert T % tile == 0 and all(tile % (d * qb) == 0 and T // d >= kw for d in dils)

    def body(cat_in, *refs):
        qkv_refs = {d: refs[3 * j:3 * j + 3] for j, d in enumerate(dils)}
        cat_ref, lg_refs = refs[3 * len(dils)], refs[3 * len(dils) + 1:4 * len(dils) + 1]
        o_scr, l_scr, bias_scr = refs[4 * len(dils) + 1:]
        i = pl.program_id(1)
        _band_setup(bias_scr, qb, kw, hw)
        lane = lax.broadcasted_iota(jnp.int32, (qb, LANES), 1)
        lo = lane < HEAD_DIM
        for pi, d in enumerate(dils):
            q_ref, k_ref, v_ref = qkv_refs[d]
            L, rows = T // d, tile // d

            def place(r, n, d=d):
                return pl.ds(r + d * n * qb, qb, stride=d) if d > 1 else pl.ds(n * qb, qb)

            def scores(r, n, q_ref=q_ref, k_ref=k_ref, L=L, rows=rows):
                ws, bias = _band_window(bias_scr, i * rows + n * qb, L, qb, kw, hw)
                return ws, _dot_nt(_stack_heads(q_ref[r, n * qb:(n + 1) * qb, :], lo), k_ref[r, pl.ds(ws, kw), :]) + bias

            def finish(r, n, held, v_ref=v_ref, pi=pi, place=place):
                ws, s = held
                m = jnp.max(s, axis=-1, keepdims=True)
                p = jnp.exp(s - m)
                den = jnp.sum(p, axis=-1, keepdims=True)
                o = _dot(p.astype(BF16), v_ref[r, pl.ds(ws, kw), :]) * (1.0 / den)
                o_scr[pi, place(r, n), :] = _unstack_heads(o, lo)
                l_scr[pi, place(r, n), :] = _unstack_heads(m + jnp.log(den), lo)

            blocks = [(r, n) for r in range(d) for n in range(rows // qb)]
            for g0 in range(0, len(blocks), 8):
                _two_phase(blocks[g0:g0 + 8], scores, finish)

        for r0 in range(0, tile, rows_merge):
            rs = slice(r0, r0 + rows_merge)
            ls_ = [l_scr[pi, rs, :] for pi in range(len(dils))]
            m = ls_[0]
            for l in ls_[1:]:
                m = jnp.maximum(m, l)
            es = [jnp.exp(l - m) for l in ls_]
            den, out = es[0], es[0] * o_scr[0, rs, :]
            for pi in range(1, len(dils)):
                den = den + es[pi]
                out = out + es[pi] * o_scr[pi, rs, :]
            cat_ref[rs, :] = (out * (1.0 / den)).astype(BF16)
            l_scr[0, rs, :] = m + jnp.log(den)
        for lg_ref, d in zip(lg_refs, dils):
            for r in range(d):
                lg_ref[r] = l_scr[0, pl.ds(r, tile // d, stride=d), :] if d > 1 else l_scr[0]

    in_specs = [pl.BlockSpec(memory_space=pl.ANY)]
    operands = [cat]
    for d in dils:
        in_specs += [pl.BlockSpec((d, tile // d, LANES), lambda b, i: (0, i, b))] + [pl.BlockSpec((d, T // d, LANES), lambda b, i: (0, 0, b))] * 2
        operands += list(qkv[d])
    return pl.pallas_call(
        body, name="dilated_fwd", grid=(nbb, T // tile), in_specs=in_specs,
        out_specs=[pl.BlockSpec((tile, LANES), lambda b, i: (i, na + b))] + [pl.BlockSpec((d, tile // d, LANES), lambda b, i: (0, i, b)) for d in dils],
        out_shape=[_sds(cat.shape, BF16)] + [_sds((d, T // d, B_W), F32) for d in dils],
        input_output_aliases={0: 0},
        scratch_shapes=[pltpu.VMEM((len(dils), tile, LANES), F32)] * 2 + [pltpu.VMEM((3, 2 * qb, kw), F32)],
        compiler_params=_params(("parallel", "arbitrary")))(*operands)


def _out_proj(x, cat, w_out, tm=512):
    T, D = x.shape

    def body(x_ref, c_ref, w_ref, o_ref):
        o_ref[...] = x_ref[...] + _dot(c_ref[...], w_ref[...])

    row = lambda w: pl.BlockSpec((tm, w), lambda i: (i, 0))
    return pl.pallas_call(
        body, name="out_proj", grid=(T // tm,), in_specs=[row(D), row(cat.shape[1]), pl.BlockSpec(w_out.shape, lambda i: (0, 0))],
        out_specs=row(D), out_shape=_sds((T, D), F32), compiler_params=_params(("parallel",)))(x, cat, w_out)


def _dcat(dx, w_out, cat, tm=512):
    T, D = dx.shape
    C = cat.shape[1]
    nba, nbb = A_Q_W // LANES, B_W // LANES

    def body(dx_ref, w_ref, cat_ref, doa_ref, dla_ref, dob1_ref, dlb1_ref, dob4_ref, dlb4_ref, dob16_ref, dlb16_ref, sdo, sdl):
        dc = _dot_nt(dx_ref[...].astype(BF16), w_ref[...])
        ri = lax.broadcasted_iota(jnp.int32, (LANES, LANES), 0)
        ci = lax.broadcasted_iota(jnp.int32, (LANES, LANES), 1)
        same_head = ((ri // HEAD_DIM) == (ci // HEAD_DIM)).astype(BF16)
        for cb in range(C // LANES):
            cols = slice(cb * LANES, (cb + 1) * LANES)
            blk = dc[:, cols]
            prod = blk * cat_ref[:, cols].astype(F32)
            hi = prod.astype(BF16)
            lo_ = (prod - hi.astype(F32)).astype(BF16)
            dl = _dot(hi, same_head) + _dot(lo_, same_head)
            if cb < nba:
                doa_ref[:, cols] = blk.astype(BF16)
                dla_ref[:, cols] = dl
            else:
                bcols = slice((cb - nba) * LANES, (cb - nba + 1) * LANES)
                dob1_ref[:, bcols] = blk.astype(BF16)
                dlb1_ref[:, bcols] = dl
                sdo[cb - nba] = blk
                sdl[cb - nba] = dl
        _deinterleave(sdo, dob4_ref, 4, tm, nbb)
        _deinterleave(sdl, dlb4_ref, 4, tm, nbb)
        _deinterleave(sdo, dob16_ref, 16, tm, nbb)
        _deinterleave(sdl, dlb16_ref, 16, tm, nbb)

    row = lambda w: pl.BlockSpec((tm, w), lambda i: (i, 0))
    perm = lambda d: pl.BlockSpec((d, tm // d, B_W), lambda i: (0, i, 0))
    return pl.pallas_call(
        body, name="dcat", grid=(T // tm,), in_specs=[row(D), pl.BlockSpec(w_out.shape, lambda i: (0, 0)), row(C)],
        out_specs=[row(A_Q_W), row(A_Q_W), row(B_W), row(B_W), perm(4), perm(4), perm(16), perm(16)],
        out_shape=[_sds((T, A_Q_W), BF16), _sds((T, A_Q_W), F32), _sds((T, B_W), BF16), _sds((T, B_W), F32),
                   _sds((4, T // 4, B_W), BF16), _sds((4, T // 4, B_W), F32), _sds((16, T // 16, B_W), BF16), _sds((16, T // 16, B_W), F32)],
        scratch_shapes=[pltpu.VMEM((nbb, tm, LANES), F32)] * 2, compiler_params=_params(("parallel",)))(dx, w_out, cat)


def _rope_bwd_assemble(dqa, dka, dva, b1, b4, b16, cos, sin, tm=512):
    T = dqa.shape[0]
    nbb = B_W // LANES
    width = A_Q_W + 2 * A_KV_W + 3 * B_W

    def body(dqa_ref, dka_ref, dva_ref, q1, k1, v1, q4, k4, v4, q16, k16, v16, c_ref, s_ref, o_ref, scr):
        cs, sn = c_ref[...], s_ref[...]

        def unrope(t):
            return t * cs + _swap32(t * sn)

        col = 0
        for ref, rope in ((dqa_ref, True), (dka_ref, True), (dva_ref, False)):
            for cb in range(ref.shape[1] // LANES):
                t = ref[:, cb * LANES:(cb + 1) * LANES].astype(F32)
                o_ref[:, col:col + LANES] = (unrope(t) if rope else t).astype(BF16)
                col += LANES
        for which, (r1, r4, r16, rope) in enumerate(((q1, q4, q16, True), (k1, k4, k16, True), (v1, v4, v16, False))):
            _interleave(r4, scr.at[0], 4, tm, nbb)
            _interleave(r16, scr.at[1], 16, tm, nbb)
            for cb in range(nbb):
                t = r1[:, cb * LANES:(cb + 1) * LANES].astype(F32) + scr[0, cb] + scr[1, cb]
                o_ref[:, col:col + LANES] = (unrope(t) if rope else t).astype(BF16)
                col += LANES

    row = lambda w: pl.BlockSpec((tm, w), lambda i: (i, 0))
    perm = lambda d: pl.BlockSpec((d, tm // d, B_W), lambda i: (0, i, 0))
    return pl.pallas_call(
        body, name="rope_bwd", grid=(T // tm,),
        in_specs=[row(A_Q_W), row(A_KV_W), row(A_KV_W)] + [row(B_W)] * 3 + [perm(4)] * 3 + [perm(16)] * 3 + [row(LANES), row(LANES)],
        out_specs=row(width), out_shape=_sds((T, width), BF16), scratch_shapes=[pltpu.VMEM((2, nbb, tm, LANES), F32)],
        compiler_params=_params(("parallel",)))(dqa, dka, dva, *b1, *b4, *b16, cos, sin)


def _dh_norm(dproj, w_in, x, g, dres, tm=512, dep=None):
    T, D = x.shape

    def body(dp_ref, w_ref, x_ref, g_ref, dr_ref, dx_ref, dg_ref):
        @pl.when(pl.program_id(0) == 0)
        def _():
            dg_ref[...] = jnp.zeros_like(dg_ref)

        dxn, dg = _rms_bwd(_dot(dp_ref[...], w_ref[...]), x_ref[...], g_ref[...])
        dg_ref[...] += dg
        dx_ref[...] = dr_ref[...] + dxn

    row = lambda w: pl.BlockSpec((tm, w), lambda i: (i, 0))
    body, dep_spec, dep_arg = _ordered(body, 5, dep)
    return pl.pallas_call(
        body, name="dh_norm", grid=(T // tm,),
        in_specs=[row(dproj.shape[1]), pl.BlockSpec(w_in.shape, lambda i: (0, 0)), row(D), pl.BlockSpec((1, D), lambda i: (0, 0)), row(D)] + dep_spec,
        out_specs=[row(D), pl.BlockSpec((SUBLANES, D), lambda i: (0, 0))],
        out_shape=[_sds((T, D), F32), _sds((SUBLANES, D), F32)], compiler_params=_params(("arbitrary",)))(dproj, w_in, x, g, dres, *dep_arg)


def _grad_push_plan(n):
    def plan(refs):
        x, y, c = _mesh_pos()
        return [(refs[k].at[chip], refs[n + k].at[rel], dev) for k in range(n) for rel, (dev, chip) in enumerate(_chip_peers(x, y, c))]
    return plan


def _sum_own(me_arr, g, landed, name):
    ns, R, C = g.shape
    tr = R // 2 if (R // 2) % 16 == 0 else R

    def body(me_ref, g_ref, x_ref, o_ref):
        acc = g_ref[...]
        for rel in range(ns - 1):
            acc = acc + x_ref[rel].astype(F32)
        o_ref[...] = acc

    grid_spec = pltpu.PrefetchScalarGridSpec(
        num_scalar_prefetch=1, grid=(R // tr,),
        in_specs=[pl.BlockSpec((None, tr, C), lambda t, me: (me[0], t, 0)), pl.BlockSpec((ns - 1, tr, C), lambda t, me: (0, t, 0))],
        out_specs=pl.BlockSpec((tr, C), lambda t, me: (t, 0)))
    return pl.pallas_call(body, name=name, grid_spec=grid_spec, out_shape=_sds((R, C), F32),
                          compiler_params=_params(("parallel",)))(me_arr, g, landed)


def _swap_plan(n):
    def plan(refs):
        x, y, c = _mesh_pos()
        return [(refs[k], refs[n + k], (x, y, 1 - c)) for k in range(n)]
    return plan


def _allreduce_small(v, dep):
    rows, W = v.shape

    def body(v_ref, o_ref, buf, send, recv):
        x, y, c = _mesh_pos()
        me = 4 * x + 2 * y + c
        cps = []
        for m in range(1, N_DEV):
            dev = (x ^ (m >> 2), y ^ ((m >> 1) & 1), c ^ (m & 1))
            cp = pltpu.make_async_remote_copy(src_ref=v_ref, dst_ref=buf.at[me], send_sem=send.at[m - 1], recv_sem=recv.at[m - 1],
                                              device_id=dev, device_id_type=MESH)
            cp.start()
            cps.append(cp)
        for m in range(1, N_DEV):
            pltpu.make_async_remote_copy(src_ref=v_ref, dst_ref=buf.at[me ^ m], send_sem=send.at[m - 1], recv_sem=recv.at[m - 1],
                                         device_id=(x, y, c), device_id_type=MESH).wait_recv()
        for cp in cps:
            cp.wait_send()
        buf[me] = v_ref[...]
        acc = buf[0]
        for i in range(1, N_DEV):
            acc = acc + buf[i]
        o_ref[...] = acc

    body, dep_spec, dep_arg = _ordered(body, 1, dep)
    return pl.pallas_call(
        body, name="allreduce_small", out_shape=_sds((rows, W), F32), in_specs=[pl.BlockSpec(memory_space=pltpu.VMEM)] + dep_spec,
        scratch_shapes=[pltpu.VMEM((N_DEV, rows, W), F32), pltpu.SemaphoreType.DMA((N_DEV - 1,)), pltpu.SemaphoreType.DMA((N_DEV - 1,))],
        compiler_params=_params())(v, *dep_arg)


def _adamw_math(w, g, m, v):
    c1 = 1.0 / (1.0 - ADAM_B1 ** ADAM_STEP)
    c2 = 1.0 / (1.0 - ADAM_B2 ** ADAM_STEP)
    nm = ADAM_B1 * m + (1.0 - ADAM_B1) * g
    nv = ADAM_B2 * v + (1.0 - ADAM_B2) * (g * g)
    return -ADAM_LR * ((nm * c1) / (jnp.sqrt(nv * c2) + ADAM_EPS) + ADAM_WD * w), nm, nv


def _adamw_small(rows, params):
    n = len(params)
    n_sink = params[-1][0].shape[1]

    def body(rows_ref, *refs):
        ins, outs = refs[:3 * n], refs[3 * n:]
        for j in range(n):
            g = rows_ref[j:j + 1, 0:n_sink] if j == n - 1 else rows_ref[j:j + 1, :]
            d, nm, nv = _adamw_math(ins[3 * j][...], g, ins[3 * j + 1][...], ins[3 * j + 2][...])
            for ref, val in zip(outs[4 * j:4 * j + 4], (g, d, nm, nv)):
                ref[...] = val
        outs[-1][...] = rows_ref[n - 1:n, n_sink:n_sink + 1]

    flat = [a for p in params for a in p]
    outs = pl.pallas_call(body, name="adamw_small", out_shape=[_sds(p[0].shape, F32) for p in params for _ in range(4)] + [_sds((1, 1), F32)],
                          compiler_params=_params())(rows, *flat)
    return [outs[4 * j:4 * j + 4] for j in range(n)], outs[-1]


def _adamw(w, gp, gq, m, v, name):
    R, C = w.shape
    tr = R // 2 if (R // 2) % SUBLANES == 0 else R

    def body(w_ref, gp_ref, gq_ref, m_ref, v_ref, g_ref, d_ref, nm_ref, nv_ref):
        gv = gp_ref[...] + gq_ref[...]
        g_ref[...] = gv
        d_ref[...], nm_ref[...], nv_ref[...] = _adamw_math(w_ref[...], gv, m_ref[...], v_ref[...])

    blk = pl.BlockSpec((tr, C), lambda t: (t, 0))
    return pl.pallas_call(body, name=name, grid=(R // tr,), in_specs=[blk] * 5, out_specs=[blk] * 4,
                          out_shape=[_sds((R, C), F32)] * 4, compiler_params=_params(("parallel",)))(w, gp, gq, m, v)


def _rope(positions, after):
    inv_freq = 1.0 / (ROPE_THETA ** (jnp.arange(0, HEAD_DIM, 2, dtype=F32) / HEAD_DIM))
    inv_freq = jnp.tile(inv_freq, LANES // (HEAD_DIM // 2)).reshape(1, LANES) + after[0, 0]
    return _rope_tables(positions.reshape(-1, 1), inv_freq)


def _local_step(x, rope, target, norms, a_sink, comm):
    T, D = x.shape
    g1, gm, g2, gf = norms
    cos, sin = rope
    no_sink = jnp.zeros((2 * (B_W // LANES),), F32)
    W = {k: comm.weight(k, x) for k in ("wg1", "wu1", "wd1")}

    x1, h1, gate1, up1, act1 = _ffn_fwd(x, g1, W["wg1"], W["wu1"], W["wd1"], "ffn1_fwd", dep=comm.dep())
    W["w_in"] = comm.weight("w_in", x1)
    (h2, aq, ak, av, bq1, bk1, bv1, bq4, bk4, bv4, bq16, bk16, bv16) = _proj_rope(x1, gm, W["w_in"], cos, sin)
    cat, a_lse = _attn_fwd(aq[None], ak[None], av[None], a_sink, A_HALF_WINDOW, True, BF16, "attn_a_fwd", qb=2 * QB, blocks_per_step=4,
                           out_cols=A_Q_W + B_W)
    bqs = {1: (bq1[None], bk1[None], bv1[None]), 4: (bq4, bk4, bv4), 16: (bq16, bk16, bv16)}
    (b_hw,) = {w // (2 * d) for w, d in B_PATTERNS}
    cat, lg1, lg4, lg16 = _dilated_fwd(cat[0], bqs, b_hw)
    lg1 = lg1[0]
    W["w_out"] = comm.weight("w_out", cat)
    x2 = _out_proj(x1, cat, W["w_out"])
    for k in ("wg2", "wu2", "wd2"):
        W[k] = comm.weight(k, x2)
    dx3, h3, gate2, up2, act2, dgf, loss8 = _ffn_fwd(x2, g2, W["wg2"], W["wu2"], W["wd2"], "ffn2_fwd", loss=(gf, target))

    dx2, dff2, dgate2, dup2, dg2 = _ffn_dx(dx3, x2, g2, gate2, up2, W["wg2"], W["wu2"], W["wd2"], "ffn2_dx")
    fb = gate2.shape[1] // 2
    dwg2 = _tn(dgate2, h3, fb, "ffn2_dw_gate")
    dwu2 = _tn(dup2, h3, fb, "ffn2_dw_up")
    dwd2 = _tn(act2, dff2, fb, "ffn2_dw_down")
    comm.ready(dict(wg2=dwg2, wu2=dwu2, wd2=dwd2), dwd2[0])

    doa, dla, dob1, dlb1, dob4, dlb4, dob16, dlb16 = _dcat(dx2, W["w_out"], cat)
    dw_out = _tn(cat, dx2, cat.shape[1], "w_out_dw", dep=comm.dep())
    dqa, dka, dva, dsk = _attn_bwd(aq[None], ak[None], av[None], doa[None], a_lse, dla[None], a_sink, A_HALF_WINDOW, True, "attn_a_bwd",
                                   dep=comm.dep())
    bwd_in = {1: (dob1[None], lg1[None], dlb1[None]), 4: (dob4, lg4, dlb4), 16: (dob16, lg16, dlb16)}
    bg = {}
    for w, d in B_PATTERNS:
        q_, k_, v_ = bqs[d]
        do_, l_, dl_ = bwd_in[d]
        bg[d] = _attn_bwd(q_, k_, v_, do_, l_, dl_, no_sink, w // (2 * d), False, f"attn_b{d}_bwd")[:3]
    dproj = _rope_bwd_assemble(dqa[0], dka[0], dva[0], [t[0] for t in bg[1]], bg[4], bg[16], cos, sin)
    dw_in = _tn(dproj, h2, dproj.shape[1] // 2, "w_in_dw")
    comm.ready(dict(w_in=dw_in, w_out=dw_out), dw_in[0])
    dx1, dgm = _dh_norm(dproj, W["w_in"], x1, gm, dx2, dep=comm.dep())

    dx0, dff1, dgate1, dup1, dg1 = _ffn_dx(dx1, x, g1, gate1, up1, W["wg1"], W["wu1"], W["wd1"], "ffn1_dx")
    comm.settle(2, dx0)
    dwd1 = _tn(act1, dff1, fb, "ffn1_dw_down", dep=comm.dep())
    comm.ready(dict(wd1=dwd1), dwd1[0])
    dwg1 = _tn(dgate1, h1, fb, "ffn1_dw_gate", dep=comm.dep())
    comm.ready(dict(wg1=dwg1), dwg1[0])
    dwu1 = _tn(dup1, h1, fb, "ffn1_dw_up", dep=comm.dep())
    comm.ready(dict(wu1=dwu1), dwu1[0])

    dsink = dsk[0, :, :, ::HEAD_DIM].sum(axis=1).reshape(-1)
    small = dict(g1=dg1.sum(axis=0), gm=dgm.sum(axis=0), g2=dg2.sum(axis=0), gf=dgf.sum(axis=0), sink=dsink, loss=loss8[0, 0])
    return dx0, small


BIG = ("wg1", "wu1", "wd1", "w_in", "w_out", "wg2", "wu2", "wd2")
GATHER_GROUPS = (("w_in",), ("w_out",), ("wg2", "wu2", "wd2"))


class _Comm:
    def __init__(self, shards, meanwhile):
        x, y, c = _mesh_pos()
        self.me = (2 * x + y).astype(jnp.int32).reshape(1)
        self.shards = shards
        self.token = None
        self.waiting = {}
        self.groups = []
        self.swaps = []
        first = ("wg1", "wu1", "wd1")
        fulls = {k: _cast_place(self.me, shards[k], f"cast_{k}") for k in first}
        plan = _neighbour_plan([fulls[k].shape for k in first])
        send, recv, bufs, tok = _push_start("gather_first_start", [fulls[k] for k in first], 2 * len(first), plan, self.me)
        self.side = meanwhile(tok)
        fulls.update({k: _cast_place(self.me, shards[k], f"cast_{k}") for k in BIG if k not in first})
        bufs = _push_wait("gather_first_wait", send, recv, bufs, plan, [fulls[k] for k in BIG if k not in first] + list(self.side))
        self.full = dict(zip(first, _gather_forward(bufs)))
        dep = self.full["wd1"]
        for gi, names in enumerate(GATHER_GROUPS):
            plan = _gather_plan(len(names))
            send, recv, bufs, self.token = _push_start(f"gather_start_{gi}", [fulls[k] for k in names], 3 * len(names), plan, dep)
            dep = self.token
            for k in names:
                self.waiting[k] = (gi, names, send, recv, bufs, plan)

    def dep(self):
        return self.token

    def weight(self, name, after):
        if name in self.waiting:
            gi, names, send, recv, bufs, plan = self.waiting[name]
            for k, buf in zip(names, _push_wait(f"gather_wait_{gi}", send, recv, bufs, plan, after)):
                self.full[k] = buf
                del self.waiting[k]
        full = self.full[name]
        return full.reshape(N_CHIPS * full.shape[1], full.shape[2])

    def ready(self, grads, after):
        names = list(grads)
        f32s, b16s = [], []
        for k in names:
            gf, gb = grads[k]
            f32s.append(gf.reshape((N_CHIPS,) + self.shards[k].shape))
            b16s.append(gb.reshape((N_CHIPS,) + self.shards[k].shape))
        n = len(names)
        lands = [lax.empty((N_CHIPS - 1,) + self.shards[k].shape, BF16) for k in names]
        plan = _grad_push_plan(n)
        send, recv, bufs, self.token = _push_start(f"grad_start_{names[0]}", b16s + lands, 3 * n, plan, after)
        self.groups.append((names, f32s, send, recv, bufs, plan))

    def settle(self, count, after):
        batch, self.groups = self.groups[:count], self.groups[count:]
        names_b, mine_b = [], []
        for names, f32s, send, recv, bufs, plan in batch:
            n = len(names)
            bufs = _push_wait(f"grad_wait_{names[0]}", send, recv, bufs, plan, mine_b[-1] if mine_b else after)
            mine_b += [_sum_own(self.me, f32s[i], bufs[n + i], f"sum_{k}") for i, k in enumerate(names)]
            names_b += names
        lands = [lax.empty(p.shape, F32) for p in mine_b]
        n = len(names_b)
        send2, recv2, both, self.token = _push_start(f"swap_start_{names_b[0]}", mine_b + lands, n, _swap_plan(n), after)
        self.swaps.append((names_b, send2, recv2, both))

    def partials(self, after):
        names_b, send2, recv2, both = self.swaps.pop(0)
        n = len(names_b)
        both = _push_wait(f"swap_wait_{names_b[0]}", send2, recv2, both, _swap_plan(n), after)
        return {k: (both[i], both[n + i]) for i, k in enumerate(names_b)}


def kernel(x, positions, norm_ffn1, w_gate1, w_up1, w_down1, norm_mix, w_in, a_sink, w_out, norm_ffn2, w_gate2, w_up2, w_down2, norm_final, loss_target, m_norm_ffn1, m_w_gate1, m_w_up1, m_w_down1, m_norm_mix, m_w_in, m_a_sink, m_w_out, m_norm_ffn2, m_w_gate2, m_w_up2, m_w_down2, m_norm_final, v_norm_ffn1, v_w_gate1, v_w_up1, v_w_down1, v_norm_mix, v_w_in, v_a_sink, v_w_out, v_norm_ffn2, v_w_gate2, v_w_up2, v_w_down2, v_norm_final):
    T, D = x.shape[1], x.shape[2]
    flip = ("wg1", "wu1", "w_in", "wg2", "wu2")

    def rows(k, a):
        return a[0].T if k in flip else a[0]

    given = dict(wg1=(w_gate1, m_w_gate1, v_w_gate1), wu1=(w_up1, m_w_up1, v_w_up1), wd1=(w_down1, m_w_down1, v_w_down1),
                 w_in=(w_in, m_w_in, v_w_in), w_out=(w_out, m_w_out, v_w_out), wg2=(w_gate2, m_w_gate2, v_w_gate2),
                 wu2=(w_up2, m_w_up2, v_w_up2), wd2=(w_down2, m_w_down2, v_w_down2))
    shards = {k: rows(k, given[k][0]) for k in BIG}

    comm = _Comm(shards, lambda tok: _rope(positions[0], tok))

    norms = (norm_ffn1, norm_mix, norm_ffn2, norm_final.reshape(1, D))
    grad_x, small = _local_step(x[0], comm.side, loss_target[0], norms, a_sink[0], comm)

    upd = {}

    def update(partial):
        for k in partial:
            outs = _adamw(shards[k], partial[k][0], partial[k][1], rows(k, given[k][1]), rows(k, given[k][2]), f"adamw_{k}")
            upd[k] = tuple((a.T if k in flip else a)[None] for a in outs)
        return outs[0]

    last = update(comm.partials(comm.dep()))
    comm.settle(2, last)

    def pad_row(a):
        a = a.reshape(-1)
        return jnp.pad(a, (0, D - a.shape[0]))

    row4 = pad_row(jnp.concatenate([small["sink"], small["loss"].reshape(1)]))
    vec = jnp.stack([small["g1"], small["gm"], small["g2"], small["gf"], row4] + [jnp.zeros((D,), F32)] * 3, axis=0)
    red = _allreduce_small(vec, comm.dep())
    comm.settle(1, red)
    last = update(comm.partials(comm.dep()))
    update(comm.partials(last))
    as_row = lambda a: a.reshape(1, -1)
    sm, loss = _adamw_small(red, [tuple(as_row(a) for a in p) for p in (
        (norm_ffn1, m_norm_ffn1, v_norm_ffn1), (norm_mix, m_norm_mix, v_norm_mix), (norm_ffn2, m_norm_ffn2, v_norm_ffn2),
        (norm_final, m_norm_final, v_norm_final), (a_sink, m_a_sink, v_a_sink))])
    sm[3] = [a.reshape(D) for a in sm[3]]

    def ordered(i):
        return [sm[0][i], upd["wg1"][i], upd["wu1"][i], upd["wd1"][i], sm[1][i], upd["w_in"][i], sm[4][i], upd["w_out"][i], sm[2][i],
                upd["wg2"][i], upd["wu2"][i], upd["wd2"][i], sm[3][i]]

    return (loss.reshape(()), grad_x[None], *ordered(0), *ordered(1), *ordered(2), *ordered(3))
```

```python
import jax
import jax.numpy as jnp
from jax import lax
from jax.experimental import pallas as pl
from jax.experimental.pallas import tpu as pltpu

F32 = jnp.float32
BF16 = jnp.bfloat16

HEAD_DIM = 64
LANES = 128
SUBLANES = 8
A_Q_W, A_KV_W, B_W = 512, 128, 512
A_HALF_WINDOW = 128
B_PATTERNS = ((128, 1), (512, 4), (2048, 16))
ROPE_THETA = 10000.0
NORM_EPS = 1e-6
FFN_RES_WEIGHT = 0.5
ADAM_LR, ADAM_B1, ADAM_B2, ADAM_EPS, ADAM_WD, ADAM_STEP = 0.001, 0.9, 0.999, 1e-08, 0.01, 10
N_CHIPS = 4
N_DEV = 8
QB = 128
SHORT_SEQ = 512
NEG = -1e30
VMEM_LIMIT = 56 * 1024 * 1024
MESH = pl.DeviceIdType.MESH
ANY = pl.BlockSpec(memory_space=pl.ANY)


def _params(sem=None):
    return pltpu.CompilerParams(dimension_semantics=sem, vmem_limit_bytes=VMEM_LIMIT)


def _sds(shape, dtype):
    return jax.ShapeDtypeStruct(tuple(shape), dtype)


def _dot(a, b):
    return jnp.dot(a, b, preferred_element_type=F32)


def _dot_nt(a, b):
    return lax.dot_general(a, b, (((1,), (1,)), ((), ())), preferred_element_type=F32)


def _dot_tn(a, b):
    return lax.dot_general(a, b, (((0,), (0,)), ((), ())), preferred_element_type=F32)


def _rms_stats(x):
    r = lax.rsqrt(jnp.mean(x * x, axis=-1, keepdims=True) + NORM_EPS)
    return x * r, r


def _rms_bwd(dh, x, g):
    xhat, r = _rms_stats(x)
    dxn = dh * g
    dx = r * (dxn - xhat * jnp.mean(dxn * xhat, axis=-1, keepdims=True))
    tm, d = x.shape
    dg = (dh * xhat).reshape(tm // SUBLANES, SUBLANES, d).sum(axis=0)
    return dx, dg


def _sigmoid(x):
    return 1.0 / (1.0 + jnp.exp(-x))


def _swap32(t):
    n = t.shape[-1]
    lane = lax.broadcasted_iota(jnp.int32, t.shape, t.ndim - 1)
    return jnp.where((lane % HEAD_DIM) < HEAD_DIM // 2, pltpu.roll(t, n - HEAD_DIM // 2, axis=t.ndim - 1),
                     pltpu.roll(t, HEAD_DIM // 2, axis=t.ndim - 1))


def _ordered(body, n_in, dep):
    if dep is None:
        return body, [], []

    def ordered(*refs):
        body(*refs[:n_in], *refs[n_in + 1:])

    return ordered, [ANY], [dep]


def _cast_place(me_arr, w, name):
    R, C = w.shape
    tr = R // 2 if (R // 2) % 16 == 0 else R

    def body(me_ref, w_ref, o_ref):
        o_ref[...] = w_ref[...].astype(BF16)

    grid_spec = pltpu.PrefetchScalarGridSpec(
        num_scalar_prefetch=1, grid=(R // tr,), in_specs=[pl.BlockSpec((tr, C), lambda t, me: (t, 0))],
        out_specs=pl.BlockSpec((None, tr, C), lambda t, me: (me[0], t, 0)))
    return pl.pallas_call(body, name=name, grid_spec=grid_spec, out_shape=_sds((N_CHIPS, R, C), BF16),
                          compiler_params=_params(("parallel",)))(me_arr, w)


HBM = pl.BlockSpec(memory_space=pltpu.HBM)
SEM = pl.BlockSpec(memory_space=pltpu.SEMAPHORE)


def _push_start(name, bufs, ncopies, plan, after):
    nb = len(bufs)

    def body(*refs):
        send, recv, token = refs[nb + 1], refs[nb + 2], refs[-1]
        for i, (src, dst, dev) in enumerate(plan(refs[:nb])):
            pltpu.make_async_remote_copy(src_ref=src, dst_ref=dst, send_sem=send.at[i], recv_sem=recv.at[i],
                                         device_id=dev, device_id_type=MESH).start()
        token[...] = jnp.zeros_like(token)

    outs = pl.pallas_call(
        body, name=name,
        out_shape=(pltpu.SemaphoreType.DMA((ncopies,)), pltpu.SemaphoreType.DMA((ncopies,)), *[pltpu.HBM(b.shape, b.dtype) for b in bufs],
                   _sds((SUBLANES, LANES), F32)),
        in_specs=[HBM] * nb + [ANY], out_specs=(SEM, SEM, *([HBM] * nb), pl.BlockSpec(memory_space=pltpu.VMEM)),
        input_output_aliases={i: 2 + i for i in range(nb)},
        compiler_params=pltpu.CompilerParams(has_side_effects=pltpu.SideEffectType.DATAFLOW_SIDE_EFFECTING),
    )(*[pltpu.with_memory_space_constraint(b, pltpu.HBM) for b in bufs], after)
    return outs[0], outs[1], list(outs[2:2 + nb]), outs[-1]


def _push_wait(name, send, recv, bufs, plan, after):
    nb = len(bufs)

    def body(*refs):
        send_ref, recv_ref = refs[nb], refs[nb + 1]
        for i, (src, dst, dev) in enumerate(plan(refs[:nb])):
            cp = pltpu.make_async_remote_copy(src_ref=src, dst_ref=dst, send_sem=send_ref.at[i], recv_sem=recv_ref.at[i],
                                              device_id=dev, device_id_type=MESH)
            cp.wait_send()
            cp.wait_recv()

    afters = list(after) if isinstance(after, (list, tuple)) else [after]
    outs = pl.pallas_call(
        body, name=name, out_shape=tuple(pltpu.HBM(b.shape, b.dtype) for b in bufs),
        in_specs=[HBM] * nb + [SEM, SEM] + [ANY] * len(afters), out_specs=tuple([HBM] * nb),
        input_output_aliases={i: i for i in range(nb)},
        compiler_params=pltpu.CompilerParams(has_side_effects=pltpu.SideEffectType.DATAFLOW_SIDE_EFFECTING),
    )(*bufs, send, recv, *afters)
    return list(outs)


def _mesh_pos():
    return lax.axis_index("x"), lax.axis_index("y"), lax.axis_index("c")


def _chip_peers(x, y, c):
    return [((1 - x, y, c), 2 * (1 - x) + y), ((x, 1 - y, c), 2 * x + (1 - y)), ((1 - x, 1 - y, c), 2 * (1 - x) + (1 - y))]


def _gather_plan(n):
    def plan(refs):
        x, y, c = _mesh_pos()
        me = 2 * x + y
        return [(refs[k].at[me], refs[k].at[me], dev) for k in range(n) for dev, _ in _chip_peers(x, y, c)]
    return plan


def _rows_of(shape, who, quarter=None):
    r2 = shape[1] // 2
    if quarter is None:
        return pl.ds(pl.multiple_of(who * r2, 16), r2)
    return pl.ds(pl.multiple_of(who * r2 + quarter * (r2 // 2), 16), r2 // 2)


def _neighbour_plan(shapes):
    def plan(refs):
        x, y, c = _mesh_pos()
        me = 2 * x + y
        return [(refs[k].at[me, _rows_of(shp, c), :], refs[k].at[me, _rows_of(shp, c), :], dev)
                for k, shp in enumerate(shapes) for dev in ((1 - x, y, c), (x, 1 - y, c))]
    return plan


def _gather_forward(fulls):
    n = len(fulls)

    def body(*refs):
        ins, outs = refs[:n], refs[n:2 * n]
        ici_send, ici_recv, d2d_send, d2d_recv = refs[2 * n:]
        x, y, c = _mesh_pos()
        cx, cy, cd = 2 * (1 - x) + y, 2 * x + (1 - y), 2 * (1 - x) + (1 - y)
        sibling, x_nbr, y_nbr = (x, y, 1 - c), (1 - x, y, c), (x, 1 - y, c)
        started = []

        def push(src, dst, send, recv, dev):
            cp = pltpu.make_async_remote_copy(src_ref=src, dst_ref=dst, send_sem=send, recv_sem=recv, device_id=dev, device_id_type=MESH)
            cp.start()
            started.append(cp)

        def arrived(blk, send, recv):
            pltpu.make_async_remote_copy(src_ref=blk, dst_ref=blk, send_sem=send, recv_sem=recv, device_id=sibling,
                                         device_id_type=MESH).wait_recv()

        for k in range(n):
            shp = fulls[k].shape
            for j, chip in enumerate((cx, cy)):
                push(ins[k].at[chip, _rows_of(shp, c), :], outs[k].at[chip, _rows_of(shp, c), :],
                     d2d_send.at[3 * k + j], d2d_recv.at[3 * k + j], sibling)
            push(ins[k].at[cx, _rows_of(shp, c, 0), :], outs[k].at[cx, _rows_of(shp, c, 0), :], ici_send.at[2 * k], ici_recv.at[2 * k], y_nbr)
            push(ins[k].at[cy, _rows_of(shp, c, 1), :], outs[k].at[cy, _rows_of(shp, c, 1), :], ici_send.at[2 * k + 1], ici_recv.at[2 * k + 1],
                 x_nbr)
        for k in range(n):
            shp = fulls[k].shape
            for q in (0, 1):
                arrived(outs[k].at[cd, _rows_of(shp, c, q), :], ici_send.at[2 * k + q], ici_recv.at[2 * k + q])
            blk = outs[k].at[cd, _rows_of(shp, c), :]
            push(blk, blk, d2d_send.at[3 * k + 2], d2d_recv.at[3 * k + 2], sibling)
        for k in range(n):
            for j, chip in enumerate((cx, cy, cd)):
                arrived(outs[k].at[chip, _rows_of(fulls[k].shape, 1 - c), :], d2d_send.at[3 * k + j], d2d_recv.at[3 * k + j])
        for cp in started:
            cp.wait_send()

    return pl.pallas_call(
        body, name="gather_forward", out_shape=[_sds(f.shape, BF16) for f in fulls],
        in_specs=[ANY] * n, out_specs=[ANY] * n, input_output_aliases={k: k for k in range(n)},
        scratch_shapes=[pltpu.SemaphoreType.DMA((n * 2,))] * 2 + [pltpu.SemaphoreType.DMA((n * 3,))] * 2,
        compiler_params=_params())(*fulls)


def _resident(shape):
    return pl.BlockSpec(shape, lambda i: (0,) * len(shape), pipeline_mode=pl.Buffered(1))


FFN_FWD_CHUNK = 256
FFN_DX_CHUNK = 512


def _chunks(n, step):
    return [(c0, min(step, n - c0)) for c0 in range(0, n, step)]


def _two_phase(chunks, first, second):
    held = {}
    for ci, ch in enumerate(chunks):
        held[ci] = first(*ch)
        if ci >= 1:
            second(*chunks[ci - 1], held.pop(ci - 1))
    last = len(chunks) - 1
    second(*chunks[last], held.pop(last))


def _loss_and_grad(x, g, target):
    D = x.shape[1]
    xhat, _ = _rms_stats(x)
    err = xhat * g - target
    loss = 0.5 * jnp.sum(jnp.sum(err * err, axis=-1, keepdims=True) * (1.0 / D), axis=0, keepdims=True)
    dx, dg = _rms_bwd(err * (1.0 / D), x, g)
    return dx, dg, loss


def _ffn_fwd(x, g, wgt, wut, wd, name, tm=512, dep=None, loss=None):
    T, D = x.shape
    F = wd.shape[0]
    n_in = 5 if loss is None else 7

    def body(*refs):
        x_ref, g_ref, wg_ref, wu_ref, wd_ref = refs[:5]
        xo_ref, h_ref, gate_ref, up_ref, act_ref = refs[n_in:n_in + 5]
        xv = x_ref[...]
        xhat, _ = _rms_stats(xv)
        h = (xhat * g_ref[...]).astype(BF16)
        h_ref[...] = h
        acc = []

        def first(c0, cw):
            return _dot_nt(h, wg_ref[c0:c0 + cw, :]), _dot_nt(h, wu_ref[c0:c0 + cw, :])

        def second(c0, cw, gate_up):
            gate, up = gate_up
            act = ((gate * _sigmoid(gate)) * up).astype(BF16)
            gate_ref[:, c0:c0 + cw] = gate.astype(BF16)
            up_ref[:, c0:c0 + cw] = up.astype(BF16)
            act_ref[:, c0:c0 + cw] = act
            d = _dot(act, wd_ref[c0:c0 + cw, :])
            acc[:] = [d if not acc else acc[0] + d]

        _two_phase(_chunks(F, FFN_FWD_CHUNK), first, second)
        xo = xv + FFN_RES_WEIGHT * acc[0]
        if loss is None:
            xo_ref[...] = xo
        else:
            gf_ref, t_ref = refs[5:7]
            dgf_ref, loss_ref = refs[n_in + 5:]

            @pl.when(pl.program_id(0) == 0)
            def _():
                dgf_ref[...] = jnp.zeros_like(dgf_ref)
                loss_ref[...] = jnp.zeros_like(loss_ref)

            xo_ref[...], dgf, part = _loss_and_grad(xo, gf_ref[...], t_ref[...])
            dgf_ref[...] += dgf
            loss_ref[...] += part

    row = pl.BlockSpec((tm, D), lambda i: (i, 0))
    gain = pl.BlockSpec((1, D), lambda i: (0, 0))
    saved = pl.BlockSpec((tm, F), lambda i: (i, 0))
    in_specs = [row, gain, _resident(wgt.shape), _resident(wut.shape), _resident(wd.shape)]
    out_specs = [row, row, saved, saved, saved]
    out_shape = [_sds((T, D), F32), _sds((T, D), BF16), _sds((T, F), BF16), _sds((T, F), BF16), _sds((T, F), BF16)]
    if loss is not None:
        in_specs += [gain, row]
        out_specs += [pl.BlockSpec((SUBLANES, D), lambda i: (0, 0)), pl.BlockSpec((SUBLANES, LANES), lambda i: (0, 0))]
        out_shape += [_sds((SUBLANES, D), F32), _sds((SUBLANES, LANES), F32)]
    body, dep_spec, dep_arg = _ordered(body, n_in, dep)
    return pl.pallas_call(
        body, name=name, grid=(T // tm,), in_specs=in_specs + dep_spec, out_specs=out_specs, out_shape=out_shape,
        compiler_params=_params(("parallel",) if loss is None else ("arbitrary",)))(x, g, wgt, wut, wd, *(loss or ()), *dep_arg)


def _ffn_dx(dxo, x, g, gate_s, up_s, wgt, wut, wd, name, tm=256, dep=None):
    T, D = x.shape
    F = wd.shape[0]

    def body(dxo_ref, x_ref, g_ref, gate_ref, up_ref, wg_ref, wu_ref, wd_ref, dx_ref, dff_ref, dgate_ref, dup_ref, dg_ref):
        @pl.when(pl.program_id(0) == 0)
        def _():
            dg_ref[...] = jnp.zeros_like(dg_ref)

        d = (FFN_RES_WEIGHT * dxo_ref[...]).astype(BF16)
        dff_ref[...] = d
        dh = []

        def first(c0, cw):
            return _dot_nt(d, wd_ref[c0:c0 + cw, :])

        def second(c0, cw, da):
            gate = gate_ref[:, c0:c0 + cw].astype(F32)
            up = up_ref[:, c0:c0 + cw].astype(F32)
            s = _sigmoid(gate)
            silu = gate * s
            dup = (da * silu).astype(BF16)
            dgate = (da * up * (s * (1.0 + gate * (1.0 - s)))).astype(BF16)
            dgate_ref[:, c0:c0 + cw] = dgate
            dup_ref[:, c0:c0 + cw] = dup
            t = _dot(dgate, wg_ref[c0:c0 + cw, :]) + _dot(dup, wu_ref[c0:c0 + cw, :])
            dh[:] = [t if not dh else dh[0] + t]

        _two_phase(_chunks(F, FFN_DX_CHUNK), first, second)
        dxn, dg = _rms_bwd(dh[0], x_ref[...], g_ref[...])
        dg_ref[...] += dg
        dx_ref[...] = dxo_ref[...] + dxn

    row = pl.BlockSpec((tm, D), lambda i: (i, 0))
    saved = pl.BlockSpec((tm, F), lambda i: (i, 0))
    body, dep_spec, dep_arg = _ordered(body, 8, dep)
    return pl.pallas_call(
        body, name=name, grid=(T // tm,),
        in_specs=[row, row, pl.BlockSpec((1, D), lambda i: (0, 0)), saved, saved, _resident(wgt.shape), _resident(wut.shape),
                  _resident(wd.shape)] + dep_spec,
        out_specs=[row, row, saved, saved, pl.BlockSpec((SUBLANES, D), lambda i: (0, 0))],
        out_shape=[_sds((T, D), F32), _sds((T, D), BF16), _sds((T, F), BF16), _sds((T, F), BF16), _sds((SUBLANES, D), F32)],
        compiler_params=_params(("arbitrary",)))(dxo, x, g, gate_s, up_s, wgt, wut, wd, *dep_arg)


def _tn(a, b, mb, name, tk=2048, dep=None):
    T, M = a.shape
    N = b.shape[1]
    nt = T // tk

    def body(a_ref, b_ref, o_ref, ob_ref):
        @pl.when(pl.program_id(1) == 0)
        def _():
            o_ref[...] = jnp.zeros_like(o_ref)

        o_ref[...] += _dot_tn(a_ref[...].astype(BF16), b_ref[...].astype(BF16))

        @pl.when(pl.program_id(1) == nt - 1)
        def _():
            ob_ref[...] = o_ref[...].astype(BF16)

    o_spec = pl.BlockSpec((mb, N), lambda g, t: (g, 0))
    body, dep_spec, dep_arg = _ordered(body, 2, dep)
    return pl.pallas_call(
        body, name=name, grid=(M // mb, nt),
        in_specs=[pl.BlockSpec((tk, mb), lambda g, t: (t, g)), pl.BlockSpec((tk, N), lambda g, t: (t, 0))] + dep_spec,
        out_specs=[o_spec, o_spec], out_shape=[_sds((M, N), F32), _sds((M, N), BF16)],
        compiler_params=_params(("parallel", "arbitrary")))(a, b, *dep_arg)


def _rope_tables(pos_col, inv_freq):
    T = pos_col.shape[0]

    def body(p_ref, f_ref, c_ref, s_ref):
        ang = p_ref[...].astype(F32) * f_ref[...]
        lane = lax.broadcasted_iota(jnp.int32, ang.shape, 1)
        c_ref[...] = jnp.cos(ang)
        sn = jnp.sin(ang)
        s_ref[...] = jnp.where((lane % HEAD_DIM) < HEAD_DIM // 2, -sn, sn)

    tm = 1024
    return pl.pallas_call(
        body, name="rope_tables", grid=(T // tm,),
        in_specs=[pl.BlockSpec((tm, 1), lambda i: (i, 0)), pl.BlockSpec((1, LANES), lambda i: (0, 0))],
        out_specs=[pl.BlockSpec((tm, LANES), lambda i: (i, 0))] * 2,
        out_shape=[_sds((T, LANES), F32)] * 2, compiler_params=_params(("parallel",)))(pos_col, inv_freq)


def _deinterleave(scr, out_ref, d, tm, nblk):
    for r in range(d):
        for cb in range(nblk):
            out_ref[r, :, cb * LANES:(cb + 1) * LANES] = scr[cb, pl.ds(r, tm // d, stride=d), :].astype(out_ref.dtype)


def _interleave(in_ref, scr, d, tm, nblk):
    for r in range(d):
        for cb in range(nblk):
            scr[cb, pl.ds(r, tm // d, stride=d), :] = in_ref[r, :, cb * LANES:(cb + 1) * LANES].astype(F32)


def _proj_rope(x, g, w_in, cos, sin, tm=512):
    T, D = x.shape
    dils = [d for _, d in B_PATTERNS if d > 1]
    nbb = B_W // LANES
    scale = HEAD_DIM ** -0.5
    cuts = [0, A_Q_W, A_Q_W + A_KV_W, A_Q_W + 2 * A_KV_W, A_Q_W + 2 * A_KV_W + B_W, A_Q_W + 2 * A_KV_W + 2 * B_W,
            A_Q_W + 2 * A_KV_W + 3 * B_W]

    def body(x_ref, g_ref, w_ref, c_ref, s_ref, h_ref, aq_ref, ak_ref, av_ref, *rest):
        b_refs, scr = rest[:-1], rest[-1]
        xhat, _ = _rms_stats(x_ref[...])
        h = (xhat * g_ref[...]).astype(BF16)
        h_ref[...] = h
        cs, sn = c_ref[...], s_ref[...]

        def project(idx, ref, rope, mult, which):
            return _dot_nt(h, w_ref[cuts[idx]:cuts[idx + 1], :])

        def finish(idx, ref, rope, mult, which, whole):
            for cb in range((cuts[idx + 1] - cuts[idx]) // LANES):
                p = whole[:, cb * LANES:(cb + 1) * LANES]
                if rope:
                    p = p * cs + _swap32(p) * sn
                if mult != 1.0:
                    p = p * mult
                ref[:, cb * LANES:(cb + 1) * LANES] = p.astype(BF16)
                if which is not None:
                    scr[which, cb] = p
            if which is not None:
                for di, d in enumerate(dils):
                    _deinterleave(scr.at[which], b_refs[3 * (di + 1) + which], d, tm, nbb)

        _two_phase([(0, aq_ref, True, scale, None), (1, ak_ref, True, 1.0, None), (2, av_ref, False, 1.0, None),
                    (3, b_refs[0], True, scale, 0), (4, b_refs[1], True, 1.0, 1), (5, b_refs[2], False, 1.0, 2)], project, finish)

    row = lambda w: pl.BlockSpec((tm, w), lambda i: (i, 0))
    out_specs = [row(D), row(A_Q_W), row(A_KV_W), row(A_KV_W)] + [row(B_W)] * 3
    out_shape = [_sds((T, D), BF16), _sds((T, A_Q_W), BF16), _sds((T, A_KV_W), BF16), _sds((T, A_KV_W), BF16)] + [_sds((T, B_W), BF16)] * 3
    for d in dils:
        out_specs += [pl.BlockSpec((d, tm // d, B_W), lambda i: (0, i, 0))] * 3
        out_shape += [_sds((d, T // d, B_W), BF16)] * 3
    return pl.pallas_call(
        body, name="proj_rope", grid=(T // tm,),
        in_specs=[row(D), pl.BlockSpec((1, D), lambda i: (0, 0)), pl.BlockSpec(w_in.shape, lambda i: (0, 0)), row(LANES), row(LANES)],
        out_specs=out_specs, out_shape=out_shape, scratch_shapes=[pltpu.VMEM((3, nbb, tm, LANES), F32)],
        compiler_params=_params(("parallel",)))(x, g, w_in, cos, sin)


def _band_bias(rel, qb, kw, hw):
    ri = lax.broadcasted_iota(jnp.int32, (2 * qb, kw), 0) & (qb - 1)
    ci = lax.broadcasted_iota(jnp.int32, (2 * qb, kw), 1)
    return jnp.where(jnp.abs(ri + rel - ci) <= hw, 0.0, NEG).astype(F32)


def _stack_heads(x, lo):
    z = jnp.zeros_like(x)
    return jnp.concatenate([jnp.where(lo, x, z), jnp.where(lo, z, x)], axis=0)


def _unstack_heads(y, lo):
    qb = y.shape[0] // 2
    return jnp.where(lo, y[:qb], y[qb:])


def _band_setup(bias_scr, qb, kw, hw):
    if bias_scr is not None:
        for i in range(3):
            bias_scr[i] = _band_bias(i * hw, qb, kw, hw)


def _band_window(bias_scr, qs, L, qb, kw, hw):
    ws = pl.multiple_of(jnp.clip(qs - hw, 0, L - kw), 64)
    if bias_scr is None:
        return ws, _band_bias(qs - ws, qb, kw, hw)
    return ws, bias_scr[lax.shift_right_logical(qs - ws, hw.bit_length() - 1)]


def _dup_kv_head(src_ref, dst_ref, head, L):
    step = min(L, 1024)
    for r0 in range(0, L, step):
        xf = src_ref[r0:r0 + step, :].astype(F32)
        lane = lax.broadcasted_iota(jnp.int32, xf.shape, 1)
        keep = jnp.logical_xor(lane < HEAD_DIM, head == 1)
        dst_ref[r0:r0 + step, :] = jnp.where(keep, xf, pltpu.roll(xf, HEAD_DIM, axis=1)).astype(dst_ref.dtype)


def _attn_fwd(q, k, v, sink, hw, gqa, out_dtype, name, qb=QB, blocks_per_step=8, out_cols=None):
    NB, L, Cq = q.shape
    Ls = min(L, 2048)
    kw = min(qb + 2 * hw, L)
    tables = L >= qb + 2 * hw
    unroll = min(blocks_per_step, Ls // qb)
    nlb = 1 if (gqa or L > SHORT_SEQ) else Cq // LANES

    def body(sink_ref, q_ref, k_ref, v_ref, o_ref, lse_ref, *scr):
        b, s_idx = pl.program_id(1), pl.program_id(2)
        bias_scr = scr[0] if tables else None
        _band_setup(bias_scr, qb, kw, hw)
        if gqa:
            kd, vd = scr[-2:]

            @pl.when(s_idx == 0)
            def _():
                _dup_kv_head(k_ref, kd, b // 2, L)
                _dup_kv_head(v_ref, vd, b // 2, L)
        else:
            kd, vd = k_ref, v_ref
        lane = lax.broadcasted_iota(jnp.int32, (qb, LANES), 1)
        lo = lane < HEAD_DIM
        if gqa:
            row = lax.broadcasted_iota(jnp.int32, (2 * qb, 1), 0)
            sk = jnp.where(row < qb, sink_ref[2 * b], sink_ref[2 * b + 1])

        def block(ql, col):
            qs = s_idx * Ls + ql
            ws, bias = _band_window(bias_scr, qs, L, qb, kw, hw)
            return ws, _dot_nt(_stack_heads(q_ref[pl.ds(ql, qb), col], lo), kd[pl.ds(ws, kw), col]) + bias

        def finish(ql, col, scores):
            ws, s = scores
            m = jnp.max(s, axis=-1, keepdims=True)
            if gqa:
                m = jnp.maximum(m, sk)
            p = jnp.exp(s - m)
            den = jnp.sum(p, axis=-1, keepdims=True)
            if gqa:
                den = den + jnp.exp(sk - m)
            o = _dot(p.astype(BF16), vd[pl.ds(ws, kw), col]) * (1.0 / den)
            o_ref[pl.ds(ql, qb), col] = _unstack_heads(o, lo).astype(o_ref.dtype)
            lse_ref[pl.ds(ql, qb), col] = _unstack_heads(m + jnp.log(den), lo)

        for lb in range(nlb):
            def step(n, carry, col=slice(lb * LANES, (lb + 1) * LANES)):
                _two_phase([(pl.multiple_of((n * unroll + u) * qb, qb), col) for u in range(unroll)], block, finish)
                return carry

            lax.fori_loop(0, Ls // (qb * unroll), step, 0)

    kv_map = (lambda r, b, s: (r, 0, 0)) if gqa else (lambda r, b, s: (r, 0, b))
    seg = pl.BlockSpec((None, Ls, nlb * LANES), lambda r, b, s: (r, s, b))
    return pl.pallas_call(
        body, name=name, grid=(NB, Cq // (nlb * LANES), L // Ls),
        in_specs=[pl.BlockSpec(memory_space=pltpu.SMEM), seg, pl.BlockSpec((None, L, nlb * LANES), kv_map),
                  pl.BlockSpec((None, L, nlb * LANES), kv_map)],
        out_specs=[seg, seg], out_shape=[_sds((NB, L, out_cols or Cq), out_dtype), _sds((NB, L, Cq), F32)],
        scratch_shapes=([pltpu.VMEM((3, 2 * qb, kw), F32)] if tables else []) + ([pltpu.VMEM((L, LANES), BF16)] * 2 if gqa else []),
        compiler_params=_params(("parallel", "parallel", "arbitrary")))(sink, q, k, v)


def _attn_bwd(q, k, v, do, lse, delta, sink, hw, gqa, name, qb=QB, blocks_per_step=8, dep=None):
    NB, L, Cq = q.shape
    Ck = k.shape[2]
    Ls = min(L, 2048)
    kw = min(qb + 2 * hw, L)
    reps = kw // LANES
    nseg = L // Ls
    scale = HEAD_DIM ** -0.5
    tables = L >= qb + 2 * hw
    unroll = min(blocks_per_step, Ls // qb)
    nlb = 1 if (gqa or L > SHORT_SEQ) else Cq // LANES

    def body(sink_ref, q_ref, do_ref, lse_ref, dl_ref, k_ref, v_ref, dq_ref, dk_ref, dv_ref, dsk_ref, *scr):
        b, s_idx = pl.program_id(1), pl.program_id(2)
        lane = lax.broadcasted_iota(jnp.int32, (qb, LANES), 1)
        lo = lane < HEAD_DIM
        bias_scr = scr[0] if tables else None
        _band_setup(bias_scr, qb, kw, hw)
        if gqa:
            kd, vd, dk_acc, dv_acc, dsk_acc = scr[-5:]

            @pl.when(s_idx == 0)
            def _():
                _dup_kv_head(k_ref, kd, b // 2, L)
                _dup_kv_head(v_ref, vd, b // 2, L)
                dk_acc[...] = jnp.zeros_like(dk_acc)
                dv_acc[...] = jnp.zeros_like(dv_acc)
                dsk_acc[...] = jnp.zeros_like(dsk_acc)

            @pl.when((s_idx == 0) & (b == 0))
            def _():
                dk_ref[...] = jnp.zeros_like(dk_ref)
                dv_ref[...] = jnp.zeros_like(dv_ref)
        else:
            kd, vd = k_ref, v_ref
            dk_acc, dv_acc = scr[-2:]

            @pl.when(s_idx == 0)
            def _():
                dk_acc[...] = jnp.zeros_like(dk_acc)
                dv_acc[...] = jnp.zeros_like(dv_acc)

        def block(ql, col):
            qs = s_idx * Ls + ql
            ws, bias = _band_window(bias_scr, qs, L, qb, kw, hw)
            qv, dov = q_ref[pl.ds(ql, qb), col], do_ref[pl.ds(ql, qb), col]
            lse, dl = lse_ref[pl.ds(ql, qb), col], dl_ref[pl.ds(ql, qb), col]
            kv_, vv = kd[pl.ds(ws, kw), col], vd[pl.ds(ws, kw), col]
            q2, do2 = _stack_heads(qv, lo), _stack_heads(dov, lo)
            return ws, q2, do2, lse, dl, _dot_nt(q2, kv_) + bias, _dot_nt(do2, vv)

        def finish(ql, col, held):
            ws, q2, do2, lse, dl, s, dp = held
            lse_sw, dl_sw = pltpu.roll(lse, HEAD_DIM, axis=1), pltpu.roll(dl, HEAD_DIM, axis=1)
            lse2 = jnp.concatenate([jnp.where(lo, lse, lse_sw), jnp.where(lo, lse_sw, lse)], axis=0)
            dl2 = jnp.concatenate([jnp.where(lo, dl, dl_sw), jnp.where(lo, dl_sw, dl)], axis=0)
            p = jnp.exp(s - jnp.tile(lse2, (1, reps)))
            ds = (p * (dp - jnp.tile(dl2, (1, reps)))).astype(BF16)
            dq_ref[pl.ds(ql, qb), col] = (_unstack_heads(_dot(ds, kd[pl.ds(ws, kw), col]), lo) * scale).astype(dq_ref.dtype)
            both = _dot_tn(jnp.concatenate([ds, p.astype(BF16)], axis=1), jnp.concatenate([q2, do2], axis=1))
            dk_acc[pl.ds(ws, kw), col] += both[:kw, :LANES]
            dv_acc[pl.ds(ws, kw), col] += both[kw:, LANES:]
            if gqa:
                sk = jnp.where(lo, sink_ref[2 * b], sink_ref[2 * b + 1])
                dsk_acc[...] += -jnp.exp(sk - lse) * dl

        for lb in range(nlb):
            def step(n, carry, col=slice(lb * LANES, (lb + 1) * LANES)):
                _two_phase([(pl.multiple_of((n * unroll + u) * qb, qb), col) for u in range(unroll)], block, finish)
                return carry

            lax.fori_loop(0, Ls // (qb * unroll), step, 0)

        if gqa:
            @pl.when(s_idx == nseg - 1)
            def _():
                step_rows = min(L, 1024)
                for r0 in range(0, L, step_rows):
                    lanek = lax.broadcasted_iota(jnp.int32, (step_rows, LANES), 1)
                    mine = jnp.logical_xor(lanek < HEAD_DIM, (b // 2) == 1)
                    for acc, ref in ((dk_acc, dk_ref), (dv_acc, dv_ref)):
                        a = acc[r0:r0 + step_rows, :]
                        ref[r0:r0 + step_rows, :] += jnp.where(mine, a + pltpu.roll(a, HEAD_DIM, axis=1), 0.0)
                dsk_ref[...] = dsk_acc[...].reshape(qb // SUBLANES, SUBLANES, LANES).sum(axis=0)
        else:
            dsk_ref[...] = jnp.zeros_like(dsk_ref)

            @pl.when(s_idx == nseg - 1)
            def _():
                dk_ref[...] = dk_acc[...].astype(dk_ref.dtype)
                dv_ref[...] = dv_acc[...].astype(dv_ref.dtype)

    kv_map = (lambda r, b, s: (r, 0, 0)) if gqa else (lambda r, b, s: (r, 0, b))
    seg = pl.BlockSpec((None, Ls, nlb * LANES), lambda r, b, s: (r, s, b))
    full = pl.BlockSpec((None, L, nlb * LANES), kv_map)
    scratch = [pltpu.VMEM((3, 2 * qb, kw), F32)] if tables else []
    if gqa:
        scratch += [pltpu.VMEM((L, LANES), BF16)] * 2 + [pltpu.VMEM((L, LANES), F32)] * 2 + [pltpu.VMEM((qb, LANES), F32)]
    else:
        scratch += [pltpu.VMEM((L, nlb * LANES), F32)] * 2
    kv_dtype = F32 if gqa else BF16
    body, dep_spec, dep_arg = _ordered(body, 7, dep)
    return pl.pallas_call(
        body, name=name, grid=(NB, Cq // (nlb * LANES), nseg),
        in_specs=[pl.BlockSpec(memory_space=pltpu.SMEM), seg, seg, seg, seg, full, full] + dep_spec,
        out_specs=[seg, full, full, pl.BlockSpec((None, None, SUBLANES, LANES), lambda r, b, s: (r, b, 0, 0))],
        out_shape=[_sds((NB, L, Cq), BF16), _sds((NB, L, Ck), kv_dtype), _sds((NB, L, Ck), kv_dtype),
                   _sds((NB, Cq // LANES, SUBLANES, LANES), F32)],
        scratch_shapes=scratch,
        compiler_params=_params(("arbitrary", "arbitrary", "arbitrary")))(sink, q, do, lse, delta, k, v, *dep_arg)


def _dilated_fwd(cat, qkv, hw, tile=2048):
    T = cat.shape[0]
    dils = sorted(qkv)
    nbb, na = B_W // LANES, A_Q_W // LANES
    qb, kw = QB, QB + 2 * hw
    rows_merge = 256
    assert T % tile == 0 and all(tile % (d * qb) == 0 and T // d >= kw for d in dils)

    def body(cat_in, *refs):
        qkv_refs = {d: refs[3 * j:3 * j + 3] for j, d in enumerate(dils)}
        cat_ref, lg_refs = refs[3 * len(dils)], refs[3 * len(dils) + 1:4 * len(dils) + 1]
        o_scr, l_scr, bias_scr = refs[4 * len(dils) + 1:]
        i = pl.program_id(1)
        _band_setup(bias_scr, qb, kw, hw)
        lane = lax.broadcasted_iota(jnp.int32, (qb, LANES), 1)
        lo = lane < HEAD_DIM
        for pi, d in enumerate(dils):
            q_ref, k_ref, v_ref = qkv_refs[d]
            L, rows = T // d, tile // d

            def place(r, n, d=d):
                return pl.ds(r + d * n * qb, qb, stride=d) if d > 1 else pl.ds(n * qb, qb)

            def scores(r, n, q_ref=q_ref, k_ref=k_ref, L=L, rows=rows):
                ws, bias = _band_window(bias_scr, i * rows + n * qb, L, qb, kw, hw)
                return ws, _dot_nt(_stack_heads(q_ref[r, n * qb:(n + 1) * qb, :], lo), k_ref[r, pl.ds(ws, kw), :]) + bias

            def finish(r, n, held, v_ref=v_ref, pi=pi, place=place):
                ws, s = held
                m = jnp.max(s, axis=-1, keepdims=True)
                p = jnp.exp(s - m)
                den = jnp.sum(p, axis=-1, keepdims=True)
                o = _dot(p.astype(BF16), v_ref[r, pl.ds(ws, kw), :]) * (1.0 / den)
                o_scr[pi, place(r, n), :] = _unstack_heads(o, lo)
                l_scr[pi, place(r, n), :] = _unstack_heads(m + jnp.log(den), lo)

            blocks = [(r, n) for r in range(d) for n in range(rows // qb)]
            for g0 in range(0, len(blocks), 8):
                _two_phase(blocks[g0:g0 + 8], scores, finish)

        for r0 in range(0, tile, rows_merge):
            rs = slice(r0, r0 + rows_merge)
            ls_ = [l_scr[pi, rs, :] for pi in range(len(dils))]
            m = ls_[0]
            for l in ls_[1:]:
                m = jnp.maximum(m, l)
            es = [jnp.exp(l - m) for l in ls_]
            den, out = es[0], es[0] * o_scr[0, rs, :]
            for pi in range(1, len(dils)):
                den = den + es[pi]
                out = out + es[pi] * o_scr[pi, rs, :]
            cat_ref[rs, :] = (out * (1.0 / den)).astype(BF16)
            l_scr[0, rs, :] = m + jnp.log(den)
        for lg_ref, d in zip(lg_refs, dils):
            for r in range(d):
                lg_ref[r] = l_scr[0, pl.ds(r, tile // d, stride=d), :] if d > 1 else l_scr[0]

    in_specs = [pl.BlockSpec(memory_space=pl.ANY)]
    operands = [cat]
    for d in dils:
        in_specs += [pl.BlockSpec((d, tile // d, LANES), lambda b, i: (0, i, b))] + [pl.BlockSpec((d, T // d, LANES), lambda b, i: (0, 0, b))] * 2
        operands += list(qkv[d])
    return pl.pallas_call(
        body, name="dilated_fwd", grid=(nbb, T // tile), in_specs=in_specs,
        out_specs=[pl.BlockSpec((tile, LANES), lambda b, i: (i, na + b))] + [pl.BlockSpec((d, tile // d, LANES), lambda b, i: (0, i, b)) for d in dils],
        out_shape=[_sds(cat.shape, BF16)] + [_sds((d, T // d, B_W), F32) for d in dils],
        input_output_aliases={0: 0},
        scratch_shapes=[pltpu.VMEM((len(dils), tile, LANES), F32)] * 2 + [pltpu.VMEM((3, 2 * qb, kw), F32)],
        compiler_params=_params(("parallel", "arbitrary")))(*operands)


def _out_proj(x, cat, w_out, tm=512):
    T, D = x.shape

    def body(x_ref, c_ref, w_ref, o_ref):
        o_ref[...] = x_ref[...] + _dot(c_ref[...], w_ref[...])

    row = lambda w: pl.BlockSpec((tm, w), lambda i: (i, 0))
    return pl.pallas_call(
        body, name="out_proj", grid=(T // tm,), in_specs=[row(D), row(cat.shape[1]), pl.BlockSpec(w_out.shape, lambda i: (0, 0))],
        out_specs=row(D), out_shape=_sds((T, D), F32), compiler_params=_params(("parallel",)))(x, cat, w_out)


def _dcat(dx, w_out, cat, tm=512):
    T, D = dx.shape
    C = cat.shape[1]
    nba, nbb = A_Q_W // LANES, B_W // LANES

    def body(dx_ref, w_ref, cat_ref, doa_ref, dla_ref, dob1_ref, dlb1_ref, dob4_ref, dlb4_ref, dob16_ref, dlb16_ref, sdo, sdl):
        dc = _dot_nt(dx_ref[...].astype(BF16), w_ref[...])
        ri = lax.broadcasted_iota(jnp.int32, (LANES, LANES), 0)
        ci = lax.broadcasted_iota(jnp.int32, (LANES, LANES), 1)
        same_head = ((ri // HEAD_DIM) == (ci // HEAD_DIM)).astype(BF16)
        for cb in range(C // LANES):
            cols = slice(cb * LANES, (cb + 1) * LANES)
            blk = dc[:, cols]
            prod = blk * cat_ref[:, cols].astype(F32)
            hi = prod.astype(BF16)
            lo_ = (prod - hi.astype(F32)).astype(BF16)
            dl = _dot(hi, same_head) + _dot(lo_, same_head)
            if cb < nba:
                doa_ref[:, cols] = blk.astype(BF16)
                dla_ref[:, cols] = dl
            else:
                bcols = slice((cb - nba) * LANES, (cb - nba + 1) * LANES)
                dob1_ref[:, bcols] = blk.astype(BF16)
                dlb1_ref[:, bcols] = dl
                sdo[cb - nba] = blk
                sdl[cb - nba] = dl
        _deinterleave(sdo, dob4_ref, 4, tm, nbb)
        _deinterleave(sdl, dlb4_ref, 4, tm, nbb)
        _deinterleave(sdo, dob16_ref, 16, tm, nbb)
        _deinterleave(sdl, dlb16_ref, 16, tm, nbb)

    row = lambda w: pl.BlockSpec((tm, w), lambda i: (i, 0))
    perm = lambda d: pl.BlockSpec((d, tm // d, B_W), lambda i: (0, i, 0))
    return pl.pallas_call(
        body, name="dcat", grid=(T // tm,), in_specs=[row(D), pl.BlockSpec(w_out.shape, lambda i: (0, 0)), row(C)],
        out_specs=[row(A_Q_W), row(A_Q_W), row(B_W), row(B_W), perm(4), perm(4), perm(16), perm(16)],
        out_shape=[_sds((T, A_Q_W), BF16), _sds((T, A_Q_W), F32), _sds((T, B_W), BF16), _sds((T, B_W), F32),
                   _sds((4, T // 4, B_W), BF16), _sds((4, T // 4, B_W), F32), _sds((16, T // 16, B_W), BF16), _sds((16, T // 16, B_W), F32)],
        scratch_shapes=[pltpu.VMEM((nbb, tm, LANES), F32)] * 2, compiler_params=_params(("parallel",)))(dx, w_out, cat)


def _mixer_in_bwd(dqa, dka, dva, b1, b4, b16, cos, sin, w_in, x, g, dres, tm=512):
    T, D = x.shape
    nbb = B_W // LANES
    width = A_Q_W + 2 * A_KV_W + 3 * B_W

    def body(dqa_ref, dka_ref, dva_ref, q1, k1, v1, q4, k4, v4, q16, k16, v16, c_ref, s_ref, w_ref, x_ref, g_ref, dr_ref,
             o_ref, dx_ref, dg_ref, scr):
        @pl.when(pl.program_id(0) == 0)
        def _():
            dg_ref[...] = jnp.zeros_like(dg_ref)

        cs, sn = c_ref[...], s_ref[...]
        dh = []

        def unrope(t):
            return t * cs + _swap32(t * sn)

        def project(c0, c1):
            t = _dot(o_ref[:, c0:c1], w_ref[c0:c1, :])
            dh[:] = [t if not dh else dh[0] + t]

        col = 0
        for ref, rope in ((dqa_ref, True), (dka_ref, True), (dva_ref, False)):
            for cb in range(ref.shape[1] // LANES):
                t = ref[:, cb * LANES:(cb + 1) * LANES].astype(F32)
                o_ref[:, col:col + LANES] = (unrope(t) if rope else t).astype(BF16)
                col += LANES
        project(0, col)
        for which, (r1, r4, r16, rope) in enumerate(((q1, q4, q16, True), (k1, k4, k16, True), (v1, v4, v16, False))):
            _interleave(r4, scr.at[0], 4, tm, nbb)
            _interleave(r16, scr.at[1], 16, tm, nbb)
            for cb in range(nbb):
                t = r1[:, cb * LANES:(cb + 1) * LANES].astype(F32) + scr[0, cb] + scr[1, cb]
                o_ref[:, col:col + LANES] = (unrope(t) if rope else t).astype(BF16)
                col += LANES
            project(col - B_W, col)
        dxn, dg = _rms_bwd(dh[0], x_ref[...], g_ref[...])
        dg_ref[...] += dg
        dx_ref[...] = dr_ref[...] + dxn

    row = lambda w: pl.BlockSpec((tm, w), lambda i: (i, 0))
    perm = lambda d: pl.BlockSpec((d, tm // d, B_W), lambda i: (0, i, 0))
    return pl.pallas_call(
        body, name="mixer_in_bwd", grid=(T // tm,),
        in_specs=[row(A_Q_W), row(A_KV_W), row(A_KV_W)] + [row(B_W)] * 3 + [perm(4)] * 3 + [perm(16)] * 3 + [row(LANES), row(LANES)]
        + [_resident(w_in.shape), row(D), pl.BlockSpec((1, D), lambda i: (0, 0)), row(D)],
        out_specs=[row(width), row(D), pl.BlockSpec((SUBLANES, D), lambda i: (0, 0))],
        out_shape=[_sds((T, width), BF16), _sds((T, D), F32), _sds((SUBLANES, D), F32)],
        scratch_shapes=[pltpu.VMEM((2, nbb, tm, LANES), F32)],
        compiler_params=_params(("arbitrary",)))(dqa, dka, dva, *b1, *b4, *b16, cos, sin, w_in, x, g, dres)


def _grad_push_plan(n):
    def plan(refs):
        x, y, c = _mesh_pos()
        return [(refs[k].at[chip], refs[n + k].at[rel], dev) for k in range(n) for rel, (dev, chip) in enumerate(_chip_peers(x, y, c))]
    return plan


def _sum_own(me_arr, g, landed, name):
    ns, R, C = g.shape
    tr = R // 2 if (R // 2) % 16 == 0 else R

    def body(me_ref, g_ref, x_ref, o_ref):
        acc = g_ref[...]
        for rel in range(ns - 1):
            acc = acc + x_ref[rel].astype(F32)
        o_ref[...] = acc

    grid_spec = pltpu.PrefetchScalarGridSpec(
        num_scalar_prefetch=1, grid=(R // tr,),
        in_specs=[pl.BlockSpec((None, tr, C), lambda t, me: (me[0], t, 0)), pl.BlockSpec((ns - 1, tr, C), lambda t, me: (0, t, 0))],
        out_specs=pl.BlockSpec((tr, C), lambda t, me: (t, 0)))
    return pl.pallas_call(body, name=name, grid_spec=grid_spec, out_shape=_sds((R, C), F32),
                          compiler_params=_params(("parallel",)))(me_arr, g, landed)


def _swap_plan(n):
    def plan(refs):
        x, y, c = _mesh_pos()
        return [(refs[k], refs[n + k], (x, y, 1 - c)) for k in range(n)]
    return plan


def _allreduce_small(v, dep):
    rows, W = v.shape

    def body(v_ref, o_ref, buf, send, recv):
        x, y, c = _mesh_pos()
        me = 4 * x + 2 * y + c
        cps = []
        for m in range(1, N_DEV):
            dev = (x ^ (m >> 2), y ^ ((m >> 1) & 1), c ^ (m & 1))
            cp = pltpu.make_async_remote_copy(src_ref=v_ref, dst_ref=buf.at[me], send_sem=send.at[m - 1], recv_sem=recv.at[m - 1],
                                              device_id=dev, device_id_type=MESH)
            cp.start()
            cps.append(cp)
        for m in range(1, N_DEV):
            pltpu.make_async_remote_copy(src_ref=v_ref, dst_ref=buf.at[me ^ m], send_sem=send.at[m - 1], recv_sem=recv.at[m - 1],
                                         device_id=(x, y, c), device_id_type=MESH).wait_recv()
        for cp in cps:
            cp.wait_send()
        buf[me] = v_ref[...]
        acc = buf[0]
        for i in range(1, N_DEV):
            acc = acc + buf[i]
        o_ref[...] = acc

    body, dep_spec, dep_arg = _ordered(body, 1, dep)
    return pl.pallas_call(
        body, name="allreduce_small", out_shape=_sds((rows, W), F32), in_specs=[pl.BlockSpec(memory_space=pltpu.VMEM)] + dep_spec,
        scratch_shapes=[pltpu.VMEM((N_DEV, rows, W), F32), pltpu.SemaphoreType.DMA((N_DEV - 1,)), pltpu.SemaphoreType.DMA((N_DEV - 1,))],
        compiler_params=_params())(v, *dep_arg)


def _adamw_math(w, g, m, v):
    c1 = 1.0 / (1.0 - ADAM_B1 ** ADAM_STEP)
    c2 = 1.0 / (1.0 - ADAM_B2 ** ADAM_STEP)
    nm = ADAM_B1 * m + (1.0 - ADAM_B1) * g
    nv = ADAM_B2 * v + (1.0 - ADAM_B2) * (g * g)
    return -ADAM_LR * ((nm * c1) / (jnp.sqrt(nv * c2) + ADAM_EPS) + ADAM_WD * w), nm, nv


def _adamw_small(rows, params):
    n = len(params)
    n_sink = params[-1][0].shape[1]

    def body(rows_ref, *refs):
        ins, outs = refs[:3 * n], refs[3 * n:]
        for j in range(n):
            g = rows_ref[j:j + 1, 0:n_sink] if j == n - 1 else rows_ref[j:j + 1, :]
            d, nm, nv = _adamw_math(ins[3 * j][...], g, ins[3 * j + 1][...], ins[3 * j + 2][...])
            for ref, val in zip(outs[4 * j:4 * j + 4], (g, d, nm, nv)):
                ref[...] = val
        outs[-1][...] = rows_ref[n - 1:n, n_sink:n_sink + 1]

    flat = [a for p in params for a in p]
    outs = pl.pallas_call(body, name="adamw_small", out_shape=[_sds(p[0].shape, F32) for p in params for _ in range(4)] + [_sds((1, 1), F32)],
                          compiler_params=_params())(rows, *flat)
    return [outs[4 * j:4 * j + 4] for j in range(n)], outs[-1]


def _adamw(w, gp, gq, m, v, name):
    R, C = w.shape
    tr = R // 2 if (R // 2) % SUBLANES == 0 else R

    def body(w_ref, gp_ref, gq_ref, m_ref, v_ref, g_ref, d_ref, nm_ref, nv_ref):
        gv = gp_ref[...] + gq_ref[...]
        g_ref[...] = gv
        d_ref[...], nm_ref[...], nv_ref[...] = _adamw_math(w_ref[...], gv, m_ref[...], v_ref[...])

    blk = pl.BlockSpec((tr, C), lambda t: (t, 0))
    return pl.pallas_call(body, name=name, grid=(R // tr,), in_specs=[blk] * 5, out_specs=[blk] * 4,
                          out_shape=[_sds((R, C), F32)] * 4, compiler_params=_params(("parallel",)))(w, gp, gq, m, v)


def _rope(positions, after):
    inv_freq = 1.0 / (ROPE_THETA ** (jnp.arange(0, HEAD_DIM, 2, dtype=F32) / HEAD_DIM))
    inv_freq = jnp.tile(inv_freq, LANES // (HEAD_DIM // 2)).reshape(1, LANES) + after[0, 0]
    return _rope_tables(positions.reshape(-1, 1), inv_freq)


def _local_step(x, rope, target, norms, a_sink, comm):
    T, D = x.shape
    g1, gm, g2, gf = norms
    cos, sin = rope
    no_sink = jnp.zeros((2 * (B_W // LANES),), F32)
    W = {k: comm.weight(k, x) for k in ("wg1", "wu1", "wd1")}

    x1, h1, gate1, up1, act1 = _ffn_fwd(x, g1, W["wg1"], W["wu1"], W["wd1"], "ffn1_fwd", dep=comm.dep())
    W["w_in"] = comm.weight("w_in", x1)
    (h2, aq, ak, av, bq1, bk1, bv1, bq4, bk4, bv4, bq16, bk16, bv16) = _proj_rope(x1, gm, W["w_in"], cos, sin)
    cat, a_lse = _attn_fwd(aq[None], ak[None], av[None], a_sink, A_HALF_WINDOW, True, BF16, "attn_a_fwd", qb=2 * QB, blocks_per_step=4,
                           out_cols=A_Q_W + B_W)
    bqs = {1: (bq1[None], bk1[None], bv1[None]), 4: (bq4, bk4, bv4), 16: (bq16, bk16, bv16)}
    (b_hw,) = {w // (2 * d) for w, d in B_PATTERNS}
    cat, lg1, lg4, lg16 = _dilated_fwd(cat[0], bqs, b_hw)
    lg1 = lg1[0]
    W["w_out"] = comm.weight("w_out", cat)
    x2 = _out_proj(x1, cat, W["w_out"])
    for k in ("wg2", "wu2", "wd2"):
        W[k] = comm.weight(k, x2)
    dx3, h3, gate2, up2, act2, dgf, loss8 = _ffn_fwd(x2, g2, W["wg2"], W["wu2"], W["wd2"], "ffn2_fwd", loss=(gf, target))

    dx2, dff2, dgate2, dup2, dg2 = _ffn_dx(dx3, x2, g2, gate2, up2, W["wg2"], W["wu2"], W["wd2"], "ffn2_dx")
    fb = gate2.shape[1] // 2
    dwg2 = _tn(dgate2, h3, fb, "ffn2_dw_gate")
    dwu2 = _tn(dup2, h3, fb, "ffn2_dw_up")
    dwd2 = _tn(act2, dff2, fb, "ffn2_dw_down")
    comm.ready(dict(wg2=dwg2, wu2=dwu2, wd2=dwd2), dwd2[0])

    doa, dla, dob1, dlb1, dob4, dlb4, dob16, dlb16 = _dcat(dx2, W["w_out"], cat)
    dw_out = _tn(cat, dx2, cat.shape[1], "w_out_dw", dep=comm.dep())
    dqa, dka, dva, dsk = _attn_bwd(aq[None], ak[None], av[None], doa[None], a_lse, dla[None], a_sink, A_HALF_WINDOW, True, "attn_a_bwd",
                                   dep=comm.dep())
    bwd_in = {1: (dob1[None], lg1[None], dlb1[None]), 4: (dob4, lg4, dlb4), 16: (dob16, lg16, dlb16)}
    bg = {}
    for w, d in B_PATTERNS:
        q_, k_, v_ = bqs[d]
        do_, l_, dl_ = bwd_in[d]
        bg[d] = _attn_bwd(q_, k_, v_, do_, l_, dl_, no_sink, w // (2 * d), False, f"attn_b{d}_bwd")[:3]
    dproj, dx1, dgm = _mixer_in_bwd(dqa[0], dka[0], dva[0], [t[0] for t in bg[1]], bg[4], bg[16], cos, sin, W["w_in"], x1, gm, dx2)
    dw_in = _tn(dproj, h2, dproj.shape[1] // 2, "w_in_dw")
    comm.ready(dict(w_in=dw_in, w_out=dw_out), dw_in[0])

    dx0, dff1, dgate1, dup1, dg1 = _ffn_dx(dx1, x, g1, gate1, up1, W["wg1"], W["wu1"], W["wd1"], "ffn1_dx", dep=comm.dep())
    comm.settle(2, dx0)
    dwd1 = _tn(act1, dff1, fb, "ffn1_dw_down", dep=comm.dep())
    comm.ready(dict(wd1=dwd1), dwd1[0])
    dwg1 = _tn(dgate1, h1, fb, "ffn1_dw_gate", dep=comm.dep())
    comm.ready(dict(wg1=dwg1), dwg1[0])
    dwu1 = _tn(dup1, h1, fb, "ffn1_dw_up", dep=comm.dep())
    comm.ready(dict(wu1=dwu1), dwu1[0])

    dsink = dsk[0, :, :, ::HEAD_DIM].sum(axis=1).reshape(-1)
    small = dict(g1=dg1.sum(axis=0), gm=dgm.sum(axis=0), g2=dg2.sum(axis=0), gf=dgf.sum(axis=0), sink=dsink, loss=loss8[0, 0])
    return dx0, small


BIG = ("wg1", "wu1", "wd1", "w_in", "w_out", "wg2", "wu2", "wd2")
GATHER_GROUPS = (("w_in",), ("w_out",), ("wg2", "wu2", "wd2"))


class _Comm:
    def __init__(self, shards, meanwhile):
        x, y, c = _mesh_pos()
        self.me = (2 * x + y).astype(jnp.int32).reshape(1)
        self.shards = shards
        self.token = None
        self.waiting = {}
        self.groups = []
        self.swaps = []
        first = ("wg1", "wu1", "wd1")
        fulls = {k: _cast_place(self.me, shards[k], f"cast_{k}") for k in first}
        plan = _neighbour_plan([fulls[k].shape for k in first])
        send, recv, bufs, tok = _push_start("gather_first_start", [fulls[k] for k in first], 2 * len(first), plan, self.me)
        self.side = meanwhile(tok)
        fulls.update({k: _cast_place(self.me, shards[k], f"cast_{k}") for k in BIG if k not in first})
        bufs = _push_wait("gather_first_wait", send, recv, bufs, plan, [fulls[k] for k in BIG if k not in first] + list(self.side))
        self.full = dict(zip(first, _gather_forward(bufs)))
        dep = self.full["wd1"]
        for gi, names in enumerate(GATHER_GROUPS):
            plan = _gather_plan(len(names))
            send, recv, bufs, self.token = _push_start(f"gather_start_{gi}", [fulls[k] for k in names], 3 * len(names), plan, dep)
            dep = self.token
            for k in names:
                self.waiting[k] = (gi, names, send, recv, bufs, plan)

    def dep(self):
        return self.token

    def weight(self, name, after):
        if name in self.waiting:
            gi, names, send, recv, bufs, plan = self.waiting[name]
            for k, buf in zip(names, _push_wait(f"gather_wait_{gi}", send, recv, bufs, plan, after)):
                self.full[k] = buf
                del self.waiting[k]
        full = self.full[name]
        return full.reshape(N_CHIPS * full.shape[1], full.shape[2])

    def ready(self, grads, after):
        names = list(grads)
        f32s, b16s = [], []
        for k in names:
            gf, gb = grads[k]
            f32s.append(gf.reshape((N_CHIPS,) + self.shards[k].shape))
            b16s.append(gb.reshape((N_CHIPS,) + self.shards[k].shape))
        n = len(names)
        lands = [lax.empty((N_CHIPS - 1,) + self.shards[k].shape, BF16) for k in names]
        plan = _grad_push_plan(n)
        send, recv, bufs, self.token = _push_start(f"grad_start_{names[0]}", b16s + lands, 3 * n, plan, after)
        self.groups.append((names, f32s, send, recv, bufs, plan))

    def settle(self, count, after):
        batch, self.groups = self.groups[:count], self.groups[count:]
        names_b, mine_b = [], []
        for names, f32s, send, recv, bufs, plan in batch:
            n = len(names)
            bufs = _push_wait(f"grad_wait_{names[0]}", send, recv, bufs, plan, mine_b[-1] if mine_b else after)
            mine_b += [_sum_own(self.me, f32s[i], bufs[n + i], f"sum_{k}") for i, k in enumerate(names)]
            names_b += names
        lands = [lax.empty(p.shape, F32) for p in mine_b]
        n = len(names_b)
        send2, recv2, both, self.token = _push_start(f"swap_start_{names_b[0]}", mine_b + lands, n, _swap_plan(n), after)
        self.swaps.append((names_b, send2, recv2, both))

    def partials(self, after):
        names_b, send2, recv2, both = self.swaps.pop(0)
        n = len(names_b)
        both = _push_wait(f"swap_wait_{names_b[0]}", send2, recv2, both, _swap_plan(n), after)
        return {k: (both[i], both[n + i]) for i, k in enumerate(names_b)}


def kernel(x, positions, norm_ffn1, w_gate1, w_up1, w_down1, norm_mix, w_in, a_sink, w_out, norm_ffn2, w_gate2, w_up2, w_down2, norm_final, loss_target, m_norm_ffn1, m_w_gate1, m_w_up1, m_w_down1, m_norm_mix, m_w_in, m_a_sink, m_w_out, m_norm_ffn2, m_w_gate2, m_w_up2, m_w_down2, m_norm_final, v_norm_ffn1, v_w_gate1, v_w_up1, v_w_down1, v_norm_mix, v_w_in, v_a_sink, v_w_out, v_norm_ffn2, v_w_gate2, v_w_up2, v_w_down2, v_norm_final):
    T, D = x.shape[1], x.shape[2]
    flip = ("wg1", "wu1", "w_in", "wg2", "wu2")

    def rows(k, a):
        return a[0].T if k in flip else a[0]

    given = dict(wg1=(w_gate1, m_w_gate1, v_w_gate1), wu1=(w_up1, m_w_up1, v_w_up1), wd1=(w_down1, m_w_down1, v_w_down1),
                 w_in=(w_in, m_w_in, v_w_in), w_out=(w_out, m_w_out, v_w_out), wg2=(w_gate2, m_w_gate2, v_w_gate2),
                 wu2=(w_up2, m_w_up2, v_w_up2), wd2=(w_down2, m_w_down2, v_w_down2))
    shards = {k: rows(k, given[k][0]) for k in BIG}

    comm = _Comm(shards, lambda tok: _rope(positions[0], tok))

    norms = (norm_ffn1, norm_mix, norm_ffn2, norm_final.reshape(1, D))
    grad_x, small = _local_step(x[0], comm.side, loss_target[0], norms, a_sink[0], comm)

    upd = {}

    def update(partial):
        for k in partial:
            outs = _adamw(shards[k], partial[k][0], partial[k][1], rows(k, given[k][1]), rows(k, given[k][2]), f"adamw_{k}")
            upd[k] = tuple((a.T if k in flip else a)[None] for a in outs)
        return outs[0]

    last = update(comm.partials(comm.dep()))
    comm.settle(2, last)

    def pad_row(a):
        a = a.reshape(-1)
        return jnp.pad(a, (0, D - a.shape[0]))

    row4 = pad_row(jnp.concatenate([small["sink"], small["loss"].reshape(1)]))
    vec = jnp.stack([small["g1"], small["gm"], small["g2"], small["gf"], row4] + [jnp.zeros((D,), F32)] * 3, axis=0)
    red = _allreduce_small(vec, comm.dep())
    comm.settle(1, red)
    last = update(comm.partials(comm.dep()))
    update(comm.partials(last))
    as_row = lambda a: a.reshape(1, -1)
    sm, loss = _adamw_small(red, [tuple(as_row(a) for a in p) for p in (
        (norm_ffn1, m_norm_ffn1, v_norm_ffn1), (norm_mix, m_norm_mix, v_norm_mix), (norm_ffn2, m_norm_ffn2, v_norm_ffn2),
        (norm_final, m_norm_final, v_norm_final), (a_sink, m_a_sink, v_a_sink))])
    sm[3] = [a.reshape(D) for a in sm[3]]

    def ordered(i):
        return [sm[0][i], upd["wg1"][i], upd["wu1"][i], upd["wd1"][i], sm[1][i], upd["w_in"][i], sm[4][i], upd["w_out"][i], sm[2][i],
                upd["wg2"][i], upd["wu2"][i], upd["wd2"][i], sm[3][i]]

    return (loss.reshape(()), grad_x[None], *ordered(0), *ordered(1), *ordered(2), *ordered(3))
```

```python
import jax
import jax.numpy as jnp
from jax import lax
from jax.experimental import pallas as pl
from jax.experimental.pallas import tpu as pltpu

F32 = jnp.float32
BF16 = jnp.bfloat16

HEAD_DIM = 64
LANES = 128
SUBLANES = 8
A_Q_W, A_KV_W, B_W = 512, 128, 512
A_HALF_WINDOW = 128
B_PATTERNS = ((128, 1), (512, 4), (2048, 16))
ROPE_THETA = 10000.0
NORM_EPS = 1e-6
FFN_RES_WEIGHT = 0.5
ADAM_LR, ADAM_B1, ADAM_B2, ADAM_EPS, ADAM_WD, ADAM_STEP = 0.001, 0.9, 0.999, 1e-08, 0.01, 10
N_CHIPS = 4
N_DEV = 8
QB = 128
SHORT_SEQ = 512
NEG = -1e30
VMEM_LIMIT = 56 * 1024 * 1024
MESH = pl.DeviceIdType.MESH
ANY = pl.BlockSpec(memory_space=pl.ANY)


def _params(sem=None):
    return pltpu.CompilerParams(dimension_semantics=sem, vmem_limit_bytes=VMEM_LIMIT)


def _sds(shape, dtype):
    return jax.ShapeDtypeStruct(tuple(shape), dtype)


def _dot(a, b):
    return jnp.dot(a, b, preferred_element_type=F32)


def _dot_nt(a, b):
    return lax.dot_general(a, b, (((1,), (1,)), ((), ())), preferred_element_type=F32)


def _dot_tn(a, b):
    return lax.dot_general(a, b, (((0,), (0,)), ((), ())), preferred_element_type=F32)


def _rms_stats(x):
    r = lax.rsqrt(jnp.mean(x * x, axis=-1, keepdims=True) + NORM_EPS)
    return x * r, r


def _rms_bwd(dh, x, g):
    xhat, r = _rms_stats(x)
    dxn = dh * g
    dx = r * (dxn - xhat * jnp.mean(dxn * xhat, axis=-1, keepdims=True))
    tm, d = x.shape
    dg = (dh * xhat).reshape(tm // SUBLANES, SUBLANES, d).sum(axis=0)
    return dx, dg


def _sigmoid(x):
    return 1.0 / (1.0 + jnp.exp(-x))


def _swap32(t):
    n = t.shape[-1]
    lane = lax.broadcasted_iota(jnp.int32, t.shape, t.ndim - 1)
    return jnp.where((lane % HEAD_DIM) < HEAD_DIM // 2, pltpu.roll(t, n - HEAD_DIM // 2, axis=t.ndim - 1),
                     pltpu.roll(t, HEAD_DIM // 2, axis=t.ndim - 1))


def _ordered(body, n_in, dep):
    if dep is None:
        return body, [], []

    def ordered(*refs):
        body(*refs[:n_in], *refs[n_in + 1:])

    return ordered, [ANY], [dep]


def _cast_place(me_arr, w, name):
    R, C = w.shape
    tr = R // 2 if (R // 2) % 16 == 0 else R

    def body(me_ref, w_ref, o_ref):
        o_ref[...] = w_ref[...].astype(BF16)

    grid_spec = pltpu.PrefetchScalarGridSpec(
        num_scalar_prefetch=1, grid=(R // tr,), in_specs=[pl.BlockSpec((tr, C), lambda t, me: (t, 0))],
        out_specs=pl.BlockSpec((None, tr, C), lambda t, me: (me[0], t, 0)))
    return pl.pallas_call(body, name=name, grid_spec=grid_spec, out_shape=_sds((N_CHIPS, R, C), BF16),
                          compiler_params=_params(("parallel",)))(me_arr, w)


HBM = pl.BlockSpec(memory_space=pltpu.HBM)
SEM = pl.BlockSpec(memory_space=pltpu.SEMAPHORE)


def _push_start(name, bufs, ncopies, plan, after):
    nb = len(bufs)

    def body(*refs):
        send, recv, token = refs[nb + 1], refs[nb + 2], refs[-1]
        for i, (src, dst, dev) in enumerate(plan(refs[:nb])):
            pltpu.make_async_remote_copy(src_ref=src, dst_ref=dst, send_sem=send.at[i], recv_sem=recv.at[i],
                                         device_id=dev, device_id_type=MESH).start()
        token[...] = jnp.zeros_like(token)

    outs = pl.pallas_call(
        body, name=name,
        out_shape=(pltpu.SemaphoreType.DMA((ncopies,)), pltpu.SemaphoreType.DMA((ncopies,)), *[pltpu.HBM(b.shape, b.dtype) for b in bufs],
                   _sds((SUBLANES, LANES), F32)),
        in_specs=[HBM] * nb + [ANY], out_specs=(SEM, SEM, *([HBM] * nb), pl.BlockSpec(memory_space=pltpu.VMEM)),
        input_output_aliases={i: 2 + i for i in range(nb)},
        compiler_params=pltpu.CompilerParams(has_side_effects=pltpu.SideEffectType.DATAFLOW_SIDE_EFFECTING),
    )(*[pltpu.with_memory_space_constraint(b, pltpu.HBM) for b in bufs], after)
    return outs[0], outs[1], list(outs[2:2 + nb]), outs[-1]


def _push_wait(name, send, recv, bufs, plan, after):
    nb = len(bufs)

    def body(*refs):
        send_ref, recv_ref = refs[nb], refs[nb + 1]
        for i, (src, dst, dev) in enumerate(plan(refs[:nb])):
            cp = pltpu.make_async_remote_copy(src_ref=src, dst_ref=dst, send_sem=send_ref.at[i], recv_sem=recv_ref.at[i],
                                              device_id=dev, device_id_type=MESH)
            cp.wait_send()
            cp.wait_recv()

    afters = list(after) if isinstance(after, (list, tuple)) else [after]
    outs = pl.pallas_call(
        body, name=name, out_shape=tuple(pltpu.HBM(b.shape, b.dtype) for b in bufs),
        in_specs=[HBM] * nb + [SEM, SEM] + [ANY] * len(afters), out_specs=tuple([HBM] * nb),
        input_output_aliases={i: i for i in range(nb)},
        compiler_params=pltpu.CompilerParams(has_side_effects=pltpu.SideEffectType.DATAFLOW_SIDE_EFFECTING),
    )(*bufs, send, recv, *afters)
    return list(outs)


def _mesh_pos():
    return lax.axis_index("x"), lax.axis_index("y"), lax.axis_index("c")


def _chip_peers(x, y, c):
    return [((1 - x, y, c), 2 * (1 - x) + y), ((x, 1 - y, c), 2 * x + (1 - y)), ((1 - x, 1 - y, c), 2 * (1 - x) + (1 - y))]


def _gather_plan(n):
    def plan(refs):
        x, y, c = _mesh_pos()
        me = 2 * x + y
        return [(refs[k].at[me], refs[k].at[me], dev) for k in range(n) for dev, _ in _chip_peers(x, y, c)]
    return plan


def _rows_of(shape, who, quarter=None):
    r2 = shape[1] // 2
    if quarter is None:
        return pl.ds(pl.multiple_of(who * r2, 16), r2)
    return pl.ds(pl.multiple_of(who * r2 + quarter * (r2 // 2), 16), r2 // 2)


def _neighbour_plan(shapes):
    def plan(refs):
        x, y, c = _mesh_pos()
        me = 2 * x + y
        return [(refs[k].at[me, _rows_of(shp, c), :], refs[k].at[me, _rows_of(shp, c), :], dev)
                for k, shp in enumerate(shapes) for dev in ((1 - x, y, c), (x, 1 - y, c))]
    return plan


def _gather_forward(fulls):
    n = len(fulls)

    def body(*refs):
        ins, outs = refs[:n], refs[n:2 * n]
        ici_send, ici_recv, d2d_send, d2d_recv = refs[2 * n:]
        x, y, c = _mesh_pos()
        cx, cy, cd = 2 * (1 - x) + y, 2 * x + (1 - y), 2 * (1 - x) + (1 - y)
        sibling, x_nbr, y_nbr = (x, y, 1 - c), (1 - x, y, c), (x, 1 - y, c)
        started = []

        def push(src, dst, send, recv, dev):
            cp = pltpu.make_async_remote_copy(src_ref=src, dst_ref=dst, send_sem=send, recv_sem=recv, device_id=dev, device_id_type=MESH)
            cp.start()
            started.append(cp)

        def arrived(blk, send, recv):
            pltpu.make_async_remote_copy(src_ref=blk, dst_ref=blk, send_sem=send, recv_sem=recv, device_id=sibling,
                                         device_id_type=MESH).wait_recv()

        for k in range(n):
            shp = fulls[k].shape
            for j, chip in enumerate((cx, cy)):
                push(ins[k].at[chip, _rows_of(shp, c), :], outs[k].at[chip, _rows_of(shp, c), :],
                     d2d_send.at[3 * k + j], d2d_recv.at[3 * k + j], sibling)
            push(ins[k].at[cx, _rows_of(shp, c, 0), :], outs[k].at[cx, _rows_of(shp, c, 0), :], ici_send.at[2 * k], ici_recv.at[2 * k], y_nbr)
            push(ins[k].at[cy, _rows_of(shp, c, 1), :], outs[k].at[cy, _rows_of(shp, c, 1), :], ici_send.at[2 * k + 1], ici_recv.at[2 * k + 1],
                 x_nbr)
        for k in range(n):
            shp = fulls[k].shape
            for q in (0, 1):
                arrived(outs[k].at[cd, _rows_of(shp, c, q), :], ici_send.at[2 * k + q], ici_recv.at[2 * k + q])
            blk = outs[k].at[cd, _rows_of(shp, c), :]
            push(blk, blk, d2d_send.at[3 * k + 2], d2d_recv.at[3 * k + 2], sibling)
        for k in range(n):
            for j, chip in enumerate((cx, cy, cd)):
                arrived(outs[k].at[chip, _rows_of(fulls[k].shape, 1 - c), :], d2d_send.at[3 * k + j], d2d_recv.at[3 * k + j])
        for cp in started:
            cp.wait_send()

    return pl.pallas_call(
        body, name="gather_forward", out_shape=[_sds(f.shape, BF16) for f in fulls],
        in_specs=[ANY] * n, out_specs=[ANY] * n, input_output_aliases={k: k for k in range(n)},
        scratch_shapes=[pltpu.SemaphoreType.DMA((n * 2,))] * 2 + [pltpu.SemaphoreType.DMA((n * 3,))] * 2,
        compiler_params=_params())(*fulls)


def _resident(shape):
    return pl.BlockSpec(shape, lambda i: (0,) * len(shape), pipeline_mode=pl.Buffered(1))


FFN_FWD_CHUNK = 256
FFN_DX_CHUNK = 512


def _chunks(n, step):
    return [(c0, min(step, n - c0)) for c0 in range(0, n, step)]


def _two_phase(chunks, first, second):
    held = {}
    for ci, ch in enumerate(chunks):
        held[ci] = first(*ch)
        if ci >= 1:
            second(*chunks[ci - 1], held.pop(ci - 1))
    last = len(chunks) - 1
    second(*chunks[last], held.pop(last))


def _loss_and_grad(x, g, target):
    D = x.shape[1]
    xhat, _ = _rms_stats(x)
    err = xhat * g - target
    loss = 0.5 * jnp.sum(jnp.sum(err * err, axis=-1, keepdims=True) * (1.0 / D), axis=0, keepdims=True)
    dx, dg = _rms_bwd(err * (1.0 / D), x, g)
    return dx, dg, loss


def _ffn_fwd(x, g, wgt, wut, wd, name, tm=512, dep=None, loss=None):
    T, D = x.shape
    F = wd.shape[0]
    n_in = 5 if loss is None else 7

    def body(*refs):
        x_ref, g_ref, wg_ref, wu_ref, wd_ref = refs[:5]
        xo_ref, h_ref, gate_ref, up_ref, act_ref = refs[n_in:n_in + 5]
        xv = x_ref[...]
        xhat, _ = _rms_stats(xv)
        h = (xhat * g_ref[...]).astype(BF16)
        h_ref[...] = h
        acc = []

        def first(c0, cw):
            return _dot_nt(h, wg_ref[c0:c0 + cw, :]), _dot_nt(h, wu_ref[c0:c0 + cw, :])

        def second(c0, cw, gate_up):
            gate, up = gate_up
            act = ((gate * _sigmoid(gate)) * up).astype(BF16)
            gate_ref[:, c0:c0 + cw] = gate.astype(BF16)
            up_ref[:, c0:c0 + cw] = up.astype(BF16)
            act_ref[:, c0:c0 + cw] = act
            d = _dot(act, wd_ref[c0:c0 + cw, :])
            acc[:] = [d if not acc else acc[0] + d]

        _two_phase(_chunks(F, FFN_FWD_CHUNK), first, second)
        xo = xv + FFN_RES_WEIGHT * acc[0]
        if loss is None:
            xo_ref[...] = xo
        else:
            gf_ref, t_ref = refs[5:7]
            dgf_ref, loss_ref = refs[n_in + 5:]

            @pl.when(pl.program_id(0) == 0)
            def _():
                dgf_ref[...] = jnp.zeros_like(dgf_ref)
                loss_ref[...] = jnp.zeros_like(loss_ref)

            xo_ref[...], dgf, part = _loss_and_grad(xo, gf_ref[...], t_ref[...])
            dgf_ref[...] += dgf
            loss_ref[...] += part

    row = pl.BlockSpec((tm, D), lambda i: (i, 0))
    gain = pl.BlockSpec((1, D), lambda i: (0, 0))
    saved = pl.BlockSpec((tm, F), lambda i: (i, 0))
    in_specs = [row, gain, _resident(wgt.shape), _resident(wut.shape), _resident(wd.shape)]
    out_specs = [row, row, saved, saved, saved]
    out_shape = [_sds((T, D), F32), _sds((T, D), BF16), _sds((T, F), BF16), _sds((T, F), BF16), _sds((T, F), BF16)]
    if loss is not None:
        in_specs += [gain, row]
        out_specs += [pl.BlockSpec((SUBLANES, D), lambda i: (0, 0)), pl.BlockSpec((SUBLANES, LANES), lambda i: (0, 0))]
        out_shape += [_sds((SUBLANES, D), F32), _sds((SUBLANES, LANES), F32)]
    body, dep_spec, dep_arg = _ordered(body, n_in, dep)
    return pl.pallas_call(
        body, name=name, grid=(T // tm,), in_specs=in_specs + dep_spec, out_specs=out_specs, out_shape=out_shape,
        compiler_params=_params(("parallel",) if loss is None else ("arbitrary",)))(x, g, wgt, wut, wd, *(loss or ()), *dep_arg)


def _ffn_dx(dxo, x, g, gate_s, up_s, wgt, wut, wd, name, tm=256, dep=None):
    T, D = x.shape
    F = wd.shape[0]

    def body(dxo_ref, x_ref, g_ref, gate_ref, up_ref, wg_ref, wu_ref, wd_ref, dx_ref, dff_ref, dgate_ref, dup_ref, dg_ref):
        @pl.when(pl.program_id(0) == 0)
        def _():
            dg_ref[...] = jnp.zeros_like(dg_ref)

        d = (FFN_RES_WEIGHT * dxo_ref[...]).astype(BF16)
        dff_ref[...] = d
        dh = []

        def first(c0, cw):
            return _dot_nt(d, wd_ref[c0:c0 + cw, :])

        def second(c0, cw, da):
            gate = gate_ref[:, c0:c0 + cw].astype(F32)
            up = up_ref[:, c0:c0 + cw].astype(F32)
            s = _sigmoid(gate)
            silu = gate * s
            dup = (da * silu).astype(BF16)
            dgate = (da * up * (s * (1.0 + gate * (1.0 - s)))).astype(BF16)
            dgate_ref[:, c0:c0 + cw] = dgate
            dup_ref[:, c0:c0 + cw] = dup
            t = _dot(dgate, wg_ref[c0:c0 + cw, :]) + _dot(dup, wu_ref[c0:c0 + cw, :])
            dh[:] = [t if not dh else dh[0] + t]

        _two_phase(_chunks(F, FFN_DX_CHUNK), first, second)
        dxn, dg = _rms_bwd(dh[0], x_ref[...], g_ref[...])
        dg_ref[...] += dg
        dx_ref[...] = dxo_ref[...] + dxn

    row = pl.BlockSpec((tm, D), lambda i: (i, 0))
    saved = pl.BlockSpec((tm, F), lambda i: (i, 0))
    body, dep_spec, dep_arg = _ordered(body, 8, dep)
    return pl.pallas_call(
        body, name=name, grid=(T // tm,),
        in_specs=[row, row, pl.BlockSpec((1, D), lambda i: (0, 0)), saved, saved, _resident(wgt.shape), _resident(wut.shape),
                  _resident(wd.shape)] + dep_spec,
        out_specs=[row, row, saved, saved, pl.BlockSpec((SUBLANES, D), lambda i: (0, 0))],
        out_shape=[_sds((T, D), F32), _sds((T, D), BF16), _sds((T, F), BF16), _sds((T, F), BF16), _sds((SUBLANES, D), F32)],
        compiler_params=_params(("arbitrary",)))(dxo, x, g, gate_s, up_s, wgt, wut, wd, *dep_arg)


def _tn(a, b, mb, name, tk=2048, dep=None):
    T, M = a.shape
    N = b.shape[1]
    nt = T // tk

    def body(a_ref, b_ref, o_ref, ob_ref):
        @pl.when(pl.program_id(1) == 0)
        def _():
            o_ref[...] = jnp.zeros_like(o_ref)

        o_ref[...] += _dot_tn(a_ref[...].astype(BF16), b_ref[...].astype(BF16))

        @pl.when(pl.program_id(1) == nt - 1)
        def _():
            ob_ref[...] = o_ref[...].astype(BF16)

    o_spec = pl.BlockSpec((mb, N), lambda g, t: (g, 0))
    body, dep_spec, dep_arg = _ordered(body, 2, dep)
    return pl.pallas_call(
        body, name=name, grid=(M // mb, nt),
        in_specs=[pl.BlockSpec((tk, mb), lambda g, t: (t, g)), pl.BlockSpec((tk, N), lambda g, t: (t, 0))] + dep_spec,
        out_specs=[o_spec, o_spec], out_shape=[_sds((M, N), F32), _sds((M, N), BF16)],
        compiler_params=_params(("parallel", "arbitrary")))(a, b, *dep_arg)


def _rope_tables(pos_col, inv_freq):
    T = pos_col.shape[0]

    def body(p_ref, f_ref, c_ref, s_ref):
        ang = p_ref[...].astype(F32) * f_ref[...]
        lane = lax.broadcasted_iota(jnp.int32, ang.shape, 1)
        c_ref[...] = jnp.cos(ang)
        sn = jnp.sin(ang)
        s_ref[...] = jnp.where((lane % HEAD_DIM) < HEAD_DIM // 2, -sn, sn)

    tm = 1024
    return pl.pallas_call(
        body, name="rope_tables", grid=(T // tm,),
        in_specs=[pl.BlockSpec((tm, 1), lambda i: (i, 0)), pl.BlockSpec((1, LANES), lambda i: (0, 0))],
        out_specs=[pl.BlockSpec((tm, LANES), lambda i: (i, 0))] * 2,
        out_shape=[_sds((T, LANES), F32)] * 2, compiler_params=_params(("parallel",)))(pos_col, inv_freq)


def _deinterleave(scr, out_ref, d, tm, nblk):
    for r in range(d):
        for cb in range(nblk):
            out_ref[r, :, cb * LANES:(cb + 1) * LANES] = scr[cb, pl.ds(r, tm // d, stride=d), :].astype(out_ref.dtype)


def _interleave(in_ref, scr, d, tm, nblk):
    for r in range(d):
        for cb in range(nblk):
            scr[cb, pl.ds(r, tm // d, stride=d), :] = in_ref[r, :, cb * LANES:(cb + 1) * LANES].astype(F32)


def _proj_rope(x, g, w_in, cos, sin, tm=512):
    T, D = x.shape
    dils = [d for _, d in B_PATTERNS if d > 1]
    nbb = B_W // LANES
    scale = HEAD_DIM ** -0.5
    cuts = [0, A_Q_W, A_Q_W + A_KV_W, A_Q_W + 2 * A_KV_W, A_Q_W + 2 * A_KV_W + B_W, A_Q_W + 2 * A_KV_W + 2 * B_W,
            A_Q_W + 2 * A_KV_W + 3 * B_W]

    def body(x_ref, g_ref, w_ref, c_ref, s_ref, h_ref, aq_ref, ak_ref, av_ref, *rest):
        b_refs, scr = rest[:-1], rest[-1]
        xhat, _ = _rms_stats(x_ref[...])
        h = (xhat * g_ref[...]).astype(BF16)
        h_ref[...] = h
        cs, sn = c_ref[...], s_ref[...]

        def project(idx, ref, rope, mult, which):
            return _dot_nt(h, w_ref[cuts[idx]:cuts[idx + 1], :])

        def finish(idx, ref, rope, mult, which, whole):
            for cb in range((cuts[idx + 1] - cuts[idx]) // LANES):
                p = whole[:, cb * LANES:(cb + 1) * LANES]
                if rope:
                    p = p * cs + _swap32(p) * sn
                if mult != 1.0:
                    p = p * mult
                ref[:, cb * LANES:(cb + 1) * LANES] = p.astype(BF16)
                if which is not None:
                    scr[which, cb] = p
            if which is not None:
                for di, d in enumerate(dils):
                    _deinterleave(scr.at[which], b_refs[3 * (di + 1) + which], d, tm, nbb)

        _two_phase([(0, aq_ref, True, scale, None), (1, ak_ref, True, 1.0, None), (2, av_ref, False, 1.0, None),
                    (3, b_refs[0], True, scale, 0), (4, b_refs[1], True, 1.0, 1), (5, b_refs[2], False, 1.0, 2)], project, finish)

    row = lambda w: pl.BlockSpec((tm, w), lambda i: (i, 0))
    out_specs = [row(D), row(A_Q_W), row(A_KV_W), row(A_KV_W)] + [row(B_W)] * 3
    out_shape = [_sds((T, D), BF16), _sds((T, A_Q_W), BF16), _sds((T, A_KV_W), BF16), _sds((T, A_KV_W), BF16)] + [_sds((T, B_W), BF16)] * 3
    for d in dils:
        out_specs += [pl.BlockSpec((d, tm // d, B_W), lambda i: (0, i, 0))] * 3
        out_shape += [_sds((d, T // d, B_W), BF16)] * 3
    return pl.pallas_call(
        body, name="proj_rope", grid=(T // tm,),
        in_specs=[row(D), pl.BlockSpec((1, D), lambda i: (0, 0)), pl.BlockSpec(w_in.shape, lambda i: (0, 0)), row(LANES), row(LANES)],
        out_specs=out_specs, out_shape=out_shape, scratch_shapes=[pltpu.VMEM((3, nbb, tm, LANES), F32)],
        compiler_params=_params(("parallel",)))(x, g, w_in, cos, sin)


def _band_bias(rel, qb, kw, hw):
    ri = lax.broadcasted_iota(jnp.int32, (2 * qb, kw), 0) & (qb - 1)
    ci = lax.broadcasted_iota(jnp.int32, (2 * qb, kw), 1)
    return jnp.where(jnp.abs(ri + rel - ci) <= hw, 0.0, NEG).astype(F32)


def _stack_heads(x, lo):
    z = jnp.zeros_like(x)
    return jnp.concatenate([jnp.where(lo, x, z), jnp.where(lo, z, x)], axis=0)


def _unstack_heads(y, lo):
    qb = y.shape[0] // 2
    return jnp.where(lo, y[:qb], y[qb:])


def _band_setup(bias_scr, qb, kw, hw):
    if bias_scr is not None:
        for i in range(3):
            bias_scr[i] = _band_bias(i * hw, qb, kw, hw)


def _band_window(bias_scr, qs, L, qb, kw, hw):
    ws = pl.multiple_of(jnp.clip(qs - hw, 0, L - kw), 64)
    if bias_scr is None:
        return ws, _band_bias(qs - ws, qb, kw, hw)
    return ws, bias_scr[lax.shift_right_logical(qs - ws, hw.bit_length() - 1)]


def _dup_kv_head(src_ref, dst_ref, head, L):
    step = min(L, 1024)
    for r0 in range(0, L, step):
        xf = src_ref[r0:r0 + step, :].astype(F32)
        lane = lax.broadcasted_iota(jnp.int32, xf.shape, 1)
        keep = jnp.logical_xor(lane < HEAD_DIM, head == 1)
        dst_ref[r0:r0 + step, :] = jnp.where(keep, xf, pltpu.roll(xf, HEAD_DIM, axis=1)).astype(dst_ref.dtype)


def _attn_fwd(q, k, v, sink, hw, gqa, out_dtype, name, qb=QB, blocks_per_step=8, out_cols=None):
    NB, L, Cq = q.shape
    Ls = min(L, 2048)
    kw = min(qb + 2 * hw, L)
    tables = L >= qb + 2 * hw
    unroll = min(blocks_per_step, Ls // qb)
    nlb = 1 if (gqa or L > SHORT_SEQ) else Cq // LANES

    def body(sink_ref, q_ref, k_ref, v_ref, o_ref, lse_ref, *scr):
        b, s_idx = pl.program_id(1), pl.program_id(2)
        bias_scr = scr[0] if tables else None
        _band_setup(bias_scr, qb, kw, hw)
        if gqa:
            kd, vd = scr[-2:]

            @pl.when(s_idx == 0)
            def _():
                _dup_kv_head(k_ref, kd, b // 2, L)
                _dup_kv_head(v_ref, vd, b // 2, L)
        else:
            kd, vd = k_ref, v_ref
        lane = lax.broadcasted_iota(jnp.int32, (qb, LANES), 1)
        lo = lane < HEAD_DIM
        if gqa:
            row = lax.broadcasted_iota(jnp.int32, (2 * qb, 1), 0)
            sk = jnp.where(row < qb, sink_ref[2 * b], sink_ref[2 * b + 1])

        def block(ql, col):
            qs = s_idx * Ls + ql
            ws, bias = _band_window(bias_scr, qs, L, qb, kw, hw)
            return ws, _dot_nt(_stack_heads(q_ref[pl.ds(ql, qb), col], lo), kd[pl.ds(ws, kw), col]) + bias

        def finish(ql, col, scores):
            ws, s = scores
            m = jnp.max(s, axis=-1, keepdims=True)
            if gqa:
                m = jnp.maximum(m, sk)
            p = jnp.exp(s - m)
            den = jnp.sum(p, axis=-1, keepdims=True)
            if gqa:
                den = den + jnp.exp(sk - m)
            o = _dot(p.astype(BF16), vd[pl.ds(ws, kw), col]) * (1.0 / den)
            o_ref[pl.ds(ql, qb), col] = _unstack_heads(o, lo).astype(o_ref.dtype)
            lse_ref[pl.ds(ql, qb), col] = _unstack_heads(m + jnp.log(den), lo)

        for lb in range(nlb):
            def step(n, carry, col=slice(lb * LANES, (lb + 1) * LANES)):
                _two_phase([(pl.multiple_of((n * unroll + u) * qb, qb), col) for u in range(unroll)], block, finish)
                return carry

            lax.fori_loop(0, Ls // (qb * unroll), step, 0)

    kv_map = (lambda r, b, s: (r, 0, 0)) if gqa else (lambda r, b, s: (r, 0, b))
    seg = pl.BlockSpec((None, Ls, nlb * LANES), lambda r, b, s: (r, s, b))
    return pl.pallas_call(
        body, name=name, grid=(NB, Cq // (nlb * LANES), L // Ls),
        in_specs=[pl.BlockSpec(memory_space=pltpu.SMEM), seg, pl.BlockSpec((None, L, nlb * LANES), kv_map),
                  pl.BlockSpec((None, L, nlb * LANES), kv_map)],
        out_specs=[seg, seg], out_shape=[_sds((NB, L, out_cols or Cq), out_dtype), _sds((NB, L, Cq), F32)],
        scratch_shapes=([pltpu.VMEM((3, 2 * qb, kw), F32)] if tables else []) + ([pltpu.VMEM((L, LANES), BF16)] * 2 if gqa else []),
        compiler_params=_params(("parallel", "parallel", "arbitrary")))(sink, q, k, v)


def _attn_bwd(q, k, v, do, lse, delta, sink, hw, gqa, name, qb=QB, blocks_per_step=8):
    NB, L, Cq = q.shape
    Ck = k.shape[2]
    Ls = min(L, 2048)
    kw = min(qb + 2 * hw, L)
    reps = kw // LANES
    nseg = L // Ls
    scale = HEAD_DIM ** -0.5
    tables = L >= qb + 2 * hw
    unroll = min(blocks_per_step, Ls // qb)
    nlb = 1 if (gqa or L > SHORT_SEQ) else Cq // LANES

    def body(sink_ref, q_ref, do_ref, lse_ref, dl_ref, k_ref, v_ref, dq_ref, dk_ref, dv_ref, dsk_ref, *scr):
        b, s_idx = pl.program_id(1), pl.program_id(2)
        lane = lax.broadcasted_iota(jnp.int32, (qb, LANES), 1)
        lo = lane < HEAD_DIM
        bias_scr = scr[0] if tables else None
        _band_setup(bias_scr, qb, kw, hw)
        if gqa:
            kd, vd, dk_acc, dv_acc, dsk_acc = scr[-5:]

            @pl.when(s_idx == 0)
            def _():
                _dup_kv_head(k_ref, kd, b // 2, L)
                _dup_kv_head(v_ref, vd, b // 2, L)
                dk_acc[...] = jnp.zeros_like(dk_acc)
                dv_acc[...] = jnp.zeros_like(dv_acc)
                dsk_acc[...] = jnp.zeros_like(dsk_acc)

            @pl.when((s_idx == 0) & (b == 0))
            def _():
                dk_ref[...] = jnp.zeros_like(dk_ref)
                dv_ref[...] = jnp.zeros_like(dv_ref)
        else:
            kd, vd = k_ref, v_ref
            dk_acc, dv_acc = scr[-2:]

            @pl.when(s_idx == 0)
            def _():
                dk_acc[...] = jnp.zeros_like(dk_acc)
                dv_acc[...] = jnp.zeros_like(dv_acc)

        def block(ql, col):
            qs = s_idx * Ls + ql
            ws, bias = _band_window(bias_scr, qs, L, qb, kw, hw)
            qv, dov = q_ref[pl.ds(ql, qb), col], do_ref[pl.ds(ql, qb), col]
            lse, dl = lse_ref[pl.ds(ql, qb), col], dl_ref[pl.ds(ql, qb), col]
            kv_, vv = kd[pl.ds(ws, kw), col], vd[pl.ds(ws, kw), col]
            q2, do2 = _stack_heads(qv, lo), _stack_heads(dov, lo)
            return ws, q2, do2, lse, dl, _dot_nt(q2, kv_) + bias, _dot_nt(do2, vv)

        def finish(ql, col, held):
            ws, q2, do2, lse, dl, s, dp = held
            lse_sw, dl_sw = pltpu.roll(lse, HEAD_DIM, axis=1), pltpu.roll(dl, HEAD_DIM, axis=1)
            lse2 = jnp.concatenate([jnp.where(lo, lse, lse_sw), jnp.where(lo, lse_sw, lse)], axis=0)
            dl2 = jnp.concatenate([jnp.where(lo, dl, dl_sw), jnp.where(lo, dl_sw, dl)], axis=0)
            p = jnp.exp(s - jnp.tile(lse2, (1, reps)))
            ds = (p * (dp - jnp.tile(dl2, (1, reps)))).astype(BF16)
            dq_ref[pl.ds(ql, qb), col] = (_unstack_heads(_dot(ds, kd[pl.ds(ws, kw), col]), lo) * scale).astype(dq_ref.dtype)
            both = _dot_tn(jnp.concatenate([ds, p.astype(BF16)], axis=1), jnp.concatenate([q2, do2], axis=1))
            dk_acc[pl.ds(ws, kw), col] += both[:kw, :LANES]
            dv_acc[pl.ds(ws, kw), col] += both[kw:, LANES:]
            if gqa:
                sk = jnp.where(lo, sink_ref[2 * b], sink_ref[2 * b + 1])
                dsk_acc[...] += -jnp.exp(sk - lse) * dl

        for lb in range(nlb):
            def step(n, carry, col=slice(lb * LANES, (lb + 1) * LANES)):
                _two_phase([(pl.multiple_of((n * unroll + u) * qb, qb), col) for u in range(unroll)], block, finish)
                return carry

            lax.fori_loop(0, Ls // (qb * unroll), step, 0)

        if gqa:
            @pl.when(s_idx == nseg - 1)
            def _():
                step_rows = min(L, 1024)
                for r0 in range(0, L, step_rows):
                    lanek = lax.broadcasted_iota(jnp.int32, (step_rows, LANES), 1)
                    mine = jnp.logical_xor(lanek < HEAD_DIM, (b // 2) == 1)
                    for acc, ref in ((dk_acc, dk_ref), (dv_acc, dv_ref)):
                        a = acc[r0:r0 + step_rows, :]
                        ref[r0:r0 + step_rows, :] += jnp.where(mine, a + pltpu.roll(a, HEAD_DIM, axis=1), 0.0)
                dsk_ref[...] = dsk_acc[...].reshape(qb // SUBLANES, SUBLANES, LANES).sum(axis=0)
        else:
            dsk_ref[...] = jnp.zeros_like(dsk_ref)

            @pl.when(s_idx == nseg - 1)
            def _():
                dk_ref[...] = dk_acc[...].astype(dk_ref.dtype)
                dv_ref[...] = dv_acc[...].astype(dv_ref.dtype)

    kv_map = (lambda r, b, s: (r, 0, 0)) if gqa else (lambda r, b, s: (r, 0, b))
    seg = pl.BlockSpec((None, Ls, nlb * LANES), lambda r, b, s: (r, s, b))
    full = pl.BlockSpec((None, L, nlb * LANES), kv_map)
    scratch = [pltpu.VMEM((3, 2 * qb, kw), F32)] if tables else []
    if gqa:
        scratch += [pltpu.VMEM((L, LANES), BF16)] * 2 + [pltpu.VMEM((L, LANES), F32)] * 2 + [pltpu.VMEM((qb, LANES), F32)]
    else:
        scratch += [pltpu.VMEM((L, nlb * LANES), F32)] * 2
    kv_dtype = F32 if gqa else BF16
    return pl.pallas_call(
        body, name=name, grid=(NB, Cq // (nlb * LANES), nseg),
        in_specs=[pl.BlockSpec(memory_space=pltpu.SMEM), seg, seg, seg, seg, full, full],
        out_specs=[seg, full, full, pl.BlockSpec((None, None, SUBLANES, LANES), lambda r, b, s: (r, b, 0, 0))],
        out_shape=[_sds((NB, L, Cq), BF16), _sds((NB, L, Ck), kv_dtype), _sds((NB, L, Ck), kv_dtype),
                   _sds((NB, Cq // LANES, SUBLANES, LANES), F32)],
        scratch_shapes=scratch,
        compiler_params=_params(("arbitrary", "arbitrary", "arbitrary")))(sink, q, do, lse, delta, k, v)


def _dilated_fwd(cat, qkv, hw, tile=2048):
    T = cat.shape[0]
    dils = sorted(qkv)
    nbb, na = B_W // LANES, A_Q_W // LANES
    qb, kw = QB, QB + 2 * hw
    rows_merge = 256
    assert T % tile == 0 and all(tile % (d * qb) == 0 and T // d >= kw for d in dils)

    def body(cat_in, *refs):
        qkv_refs = {d: refs[3 * j:3 * j + 3] for j, d in enumerate(dils)}
        cat_ref, lg_refs = refs[3 * len(dils)], refs[3 * len(dils) + 1:4 * len(dils) + 1]
        o_scr, l_scr, bias_scr = refs[4 * len(dils) + 1:]
        i = pl.program_id(1)
        _band_setup(bias_scr, qb, kw, hw)
        lane = lax.broadcasted_iota(jnp.int32, (qb, LANES), 1)
        lo = lane < HEAD_DIM
        for pi, d in enumerate(dils):
            q_ref, k_ref, v_ref = qkv_refs[d]
            L, rows = T // d, tile // d

            def place(r, n, d=d):
                return pl.ds(r + d * n * qb, qb, stride=d) if d > 1 else pl.ds(n * qb, qb)

            def scores(r, n, q_ref=q_ref, k_ref=k_ref, L=L, rows=rows):
                ws, bias = _band_window(bias_scr, i * rows + n * qb, L, qb, kw, hw)
                return ws, _dot_nt(_stack_heads(q_ref[r, n * qb:(n + 1) * qb, :], lo), k_ref[r, pl.ds(ws, kw), :]) + bias

            def finish(r, n, held, v_ref=v_ref, pi=pi, place=place):
                ws, s = held
                m = jnp.max(s, axis=-1, keepdims=True)
                p = jnp.exp(s - m)
                den = jnp.sum(p, axis=-1, keepdims=True)
                o = _dot(p.astype(BF16), v_ref[r, pl.ds(ws, kw), :]) * (1.0 / den)
                o_scr[pi, place(r, n), :] = _unstack_heads(o, lo)
                l_scr[pi, place(r, n), :] = _unstack_heads(m + jnp.log(den), lo)

            blocks = [(r, n) for r in range(d) for n in range(rows // qb)]
            for g0 in range(0, len(blocks), 8):
                _two_phase(blocks[g0:g0 + 8], scores, finish)

        for r0 in range(0, tile, rows_merge):
            rs = slice(r0, r0 + rows_merge)
            ls_ = [l_scr[pi, rs, :] for pi in range(len(dils))]
            m = ls_[0]
            for l in ls_[1:]:
                m = jnp.maximum(m, l)
            es = [jnp.exp(l - m) for l in ls_]
            den, out = es[0], es[0] * o_scr[0, rs, :]
            for pi in range(1, len(dils)):
                den = den + es[pi]
                out = out + es[pi] * o_scr[pi, rs, :]
            cat_ref[rs, :] = (out * (1.0 / den)).astype(BF16)
            l_scr[0, rs, :] = m + jnp.log(den)
        for lg_ref, d in zip(lg_refs, dils):
            for r in range(d):
                lg_ref[r] = l_scr[0, pl.ds(r, tile // d, stride=d), :] if d > 1 else l_scr[0]

    in_specs = [pl.BlockSpec(memory_space=pl.ANY)]
    operands = [cat]
    for d in dils:
        in_specs += [pl.BlockSpec((d, tile // d, LANES), lambda b, i: (0, i, b))] + [pl.BlockSpec((d, T // d, LANES), lambda b, i: (0, 0, b))] * 2
        operands += list(qkv[d])
    return pl.pallas_call(
        body, name="dilated_fwd", grid=(nbb, T // tile), in_specs=in_specs,
        out_specs=[pl.BlockSpec((tile, LANES), lambda b, i: (i, na + b))] + [pl.BlockSpec((d, tile // d, LANES), lambda b, i: (0, i, b)) for d in dils],
        out_shape=[_sds(cat.shape, BF16)] + [_sds((d, T // d, B_W), F32) for d in dils],
        input_output_aliases={0: 0},
        scratch_shapes=[pltpu.VMEM((len(dils), tile, LANES), F32)] * 2 + [pltpu.VMEM((3, 2 * qb, kw), F32)],
        compiler_params=_params(("parallel", "arbitrary")))(*operands)


def _out_proj(x, cat, w_out, tm=512):
    T, D = x.shape

    def body(x_ref, c_ref, w_ref, o_ref):
        o_ref[...] = x_ref[...] + _dot(c_ref[...], w_ref[...])

    row = lambda w: pl.BlockSpec((tm, w), lambda i: (i, 0))
    return pl.pallas_call(
        body, name="out_proj", grid=(T // tm,), in_specs=[row(D), row(cat.shape[1]), pl.BlockSpec(w_out.shape, lambda i: (0, 0))],
        out_specs=row(D), out_shape=_sds((T, D), F32), compiler_params=_params(("parallel",)))(x, cat, w_out)


def _dcat(dx, w_out, cat, tm=512, dep=None):
    T, D = dx.shape
    C = cat.shape[1]
    nba, nbb = A_Q_W // LANES, B_W // LANES
    nt = T // tm

    def body(dx_ref, w_ref, cat_ref, doa_ref, dla_ref, dob1_ref, dlb1_ref, dob4_ref, dlb4_ref, dob16_ref, dlb16_ref, dw_ref, dwb_ref,
             sdo, sdl):
        @pl.when(pl.program_id(0) == 0)
        def _():
            dw_ref[...] = jnp.zeros_like(dw_ref)

        dxb = dx_ref[...].astype(BF16)
        dc = _dot_nt(dxb, w_ref[...])
        dw_ref[...] += _dot_tn(cat_ref[...], dxb)

        @pl.when(pl.program_id(0) == nt - 1)
        def _():
            dwb_ref[...] = dw_ref[...].astype(BF16)

        ri =lax.broadcasted_iota(jnp.int32, (LANES, LANES), 0)
        ci = lax.broadcasted_iota(jnp.int32, (LANES, LANES), 1)
        same_head = ((ri // HEAD_DIM) == (ci // HEAD_DIM)).astype(BF16)
        for cb in range(C // LANES):
            cols = slice(cb * LANES, (cb + 1) * LANES)
            blk = dc[:, cols]
            prod = blk * cat_ref[:, cols].astype(F32)
            hi = prod.astype(BF16)
            lo_ = (prod - hi.astype(F32)).astype(BF16)
            dl = _dot(hi, same_head) + _dot(lo_, same_head)
            if cb < nba:
                doa_ref[:, cols] = blk.astype(BF16)
                dla_ref[:, cols] = dl
            else:
                bcols = slice((cb - nba) * LANES, (cb - nba + 1) * LANES)
                dob1_ref[:, bcols] = blk.astype(BF16)
                dlb1_ref[:, bcols] = dl
                sdo[cb - nba] = blk
                sdl[cb - nba] = dl
        _deinterleave(sdo, dob4_ref, 4, tm, nbb)
        _deinterleave(sdl, dlb4_ref, 4, tm, nbb)
        _deinterleave(sdo, dob16_ref, 16, tm, nbb)
        _deinterleave(sdl, dlb16_ref, 16, tm, nbb)

    row = lambda w: pl.BlockSpec((tm, w), lambda i: (i, 0))
    perm = lambda d: pl.BlockSpec((d, tm // d, B_W), lambda i: (0, i, 0))
    whole = pl.BlockSpec((C, D), lambda i: (0, 0))
    body, dep_spec, dep_arg = _ordered(body, 3, dep)
    outs = pl.pallas_call(
        body, name="dcat", grid=(nt,), in_specs=[row(D), whole, row(C)] + dep_spec,
        out_specs=[row(A_Q_W), row(A_Q_W), row(B_W), row(B_W), perm(4), perm(4), perm(16), perm(16), whole, whole],
        out_shape=[_sds((T, A_Q_W), BF16), _sds((T, A_Q_W), F32), _sds((T, B_W), BF16), _sds((T, B_W), F32),
                   _sds((4, T // 4, B_W), BF16), _sds((4, T // 4, B_W), F32), _sds((16, T // 16, B_W), BF16), _sds((16, T // 16, B_W), F32),
                   _sds((C, D), F32), _sds((C, D), BF16)],
        scratch_shapes=[pltpu.VMEM((nbb, tm, LANES), F32)] * 2, compiler_params=_params(("arbitrary",)))(dx, w_out, cat, *dep_arg)
    return (*outs[:8], (outs[8], outs[9]))


def _mixer_in_bwd(dqa, dka, dva, b1, b4, b16, cos, sin, w_in, x, g, dres, tm=512):
    T, D = x.shape
    nbb = B_W // LANES
    width = A_Q_W + 2 * A_KV_W + 3 * B_W

    def body(dqa_ref, dka_ref, dva_ref, q1, k1, v1, q4, k4, v4, q16, k16, v16, c_ref, s_ref, w_ref, x_ref, g_ref, dr_ref,
             o_ref, dx_ref, dg_ref, scr):
        @pl.when(pl.program_id(0) == 0)
        def _():
            dg_ref[...] = jnp.zeros_like(dg_ref)

        cs, sn = c_ref[...], s_ref[...]
        dh = []

        def unrope(t):
            return t * cs + _swap32(t * sn)

        def project(c0, c1):
            t = _dot(o_ref[:, c0:c1], w_ref[c0:c1, :])
            dh[:] = [t if not dh else dh[0] + t]

        col = 0
        for ref, rope in ((dqa_ref, True), (dka_ref, True), (dva_ref, False)):
            for cb in range(ref.shape[1] // LANES):
                t = ref[:, cb * LANES:(cb + 1) * LANES].astype(F32)
                o_ref[:, col:col + LANES] = (unrope(t) if rope else t).astype(BF16)
                col += LANES
        project(0, col)
        for which, (r1, r4, r16, rope) in enumerate(((q1, q4, q16, True), (k1, k4, k16, True), (v1, v4, v16, False))):
            _interleave(r4, scr.at[0], 4, tm, nbb)
            _interleave(r16, scr.at[1], 16, tm, nbb)
            for cb in range(nbb):
                t = r1[:, cb * LANES:(cb + 1) * LANES].astype(F32) + scr[0, cb] + scr[1, cb]
                o_ref[:, col:col + LANES] = (unrope(t) if rope else t).astype(BF16)
                col += LANES
            project(col - B_W, col)
        dxn, dg = _rms_bwd(dh[0], x_ref[...], g_ref[...])
        dg_ref[...] += dg
        dx_ref[...] = dr_ref[...] + dxn

    row = lambda w: pl.BlockSpec((tm, w), lambda i: (i, 0))
    perm = lambda d: pl.BlockSpec((d, tm // d, B_W), lambda i: (0, i, 0))
    return pl.pallas_call(
        body, name="mixer_in_bwd", grid=(T // tm,),
        in_specs=[row(A_Q_W), row(A_KV_W), row(A_KV_W)] + [row(B_W)] * 3 + [perm(4)] * 3 + [perm(16)] * 3 + [row(LANES), row(LANES)]
        + [_resident(w_in.shape), row(D), pl.BlockSpec((1, D), lambda i: (0, 0)), row(D)],
        out_specs=[row(width), row(D), pl.BlockSpec((SUBLANES, D), lambda i: (0, 0))],
        out_shape=[_sds((T, width), BF16), _sds((T, D), F32), _sds((SUBLANES, D), F32)],
        scratch_shapes=[pltpu.VMEM((2, nbb, tm, LANES), F32)],
        compiler_params=_params(("arbitrary",)))(dqa, dka, dva, *b1, *b4, *b16, cos, sin, w_in, x, g, dres)


def _grad_push_plan(n):
    def plan(refs):
        x, y, c = _mesh_pos()
        return [(refs[k].at[chip], refs[n + k].at[rel], dev) for k in range(n) for rel, (dev, chip) in enumerate(_chip_peers(x, y, c))]
    return plan


def _sum_own(me_arr, g, landed, name):
    ns, R, C = g.shape
    tr = R // 2 if (R // 2) % 16 == 0 else R

    def body(me_ref, g_ref, x_ref, o_ref):
        acc = g_ref[...]
        for rel in range(ns - 1):
            acc = acc + x_ref[rel].astype(F32)
        o_ref[...] = acc

    grid_spec = pltpu.PrefetchScalarGridSpec(
        num_scalar_prefetch=1, grid=(R // tr,),
        in_specs=[pl.BlockSpec((None, tr, C), lambda t, me: (me[0], t, 0)), pl.BlockSpec((ns - 1, tr, C), lambda t, me: (0, t, 0))],
        out_specs=pl.BlockSpec((tr, C), lambda t, me: (t, 0)))
    return pl.pallas_call(body, name=name, grid_spec=grid_spec, out_shape=_sds((R, C), F32),
                          compiler_params=_params(("parallel",)))(me_arr, g, landed)


def _swap_plan(n):
    def plan(refs):
        x, y, c = _mesh_pos()
        return [(refs[k], refs[n + k], (x, y, 1 - c)) for k in range(n)]
    return plan


def _allreduce_small(v, dep):
    rows, W = v.shape

    def body(v_ref, o_ref, buf, send, recv):
        x, y, c = _mesh_pos()
        me = 4 * x + 2 * y + c
        cps = []
        for m in range(1, N_DEV):
            dev = (x ^ (m >> 2), y ^ ((m >> 1) & 1), c ^ (m & 1))
            cp = pltpu.make_async_remote_copy(src_ref=v_ref, dst_ref=buf.at[me], send_sem=send.at[m - 1], recv_sem=recv.at[m - 1],
                                              device_id=dev, device_id_type=MESH)
            cp.start()
            cps.append(cp)
        for m in range(1, N_DEV):
            pltpu.make_async_remote_copy(src_ref=v_ref, dst_ref=buf.at[me ^ m], send_sem=send.at[m - 1], recv_sem=recv.at[m - 1],
                                         device_id=(x, y, c), device_id_type=MESH).wait_recv()
        for cp in cps:
            cp.wait_send()
        buf[me] = v_ref[...]
        acc = buf[0]
        for i in range(1, N_DEV):
            acc = acc + buf[i]
        o_ref[...] = acc

    body, dep_spec, dep_arg = _ordered(body, 1, dep)
    return pl.pallas_call(
        body, name="allreduce_small", out_shape=_sds((rows, W), F32), in_specs=[pl.BlockSpec(memory_space=pltpu.VMEM)] + dep_spec,
        scratch_shapes=[pltpu.VMEM((N_DEV, rows, W), F32), pltpu.SemaphoreType.DMA((N_DEV - 1,)), pltpu.SemaphoreType.DMA((N_DEV - 1,))],
        compiler_params=_params())(v, *dep_arg)


def _adamw_math(w, g, m, v):
    c1 = 1.0 / (1.0 - ADAM_B1 ** ADAM_STEP)
    c2 = 1.0 / (1.0 - ADAM_B2 ** ADAM_STEP)
    nm = ADAM_B1 * m + (1.0 - ADAM_B1) * g
    nv = ADAM_B2 * v + (1.0 - ADAM_B2) * (g * g)
    return -ADAM_LR * ((nm * c1) / (jnp.sqrt(nv * c2) + ADAM_EPS) + ADAM_WD * w), nm, nv


def _adamw_small(rows, params):
    n = len(params)
    n_sink = params[-1][0].shape[1]

    def body(rows_ref, *refs):
        ins, outs = refs[:3 * n], refs[3 * n:]
        for j in range(n):
            g = rows_ref[j:j + 1, 0:n_sink] if j == n - 1 else rows_ref[j:j + 1, :]
            d, nm, nv = _adamw_math(ins[3 * j][...], g, ins[3 * j + 1][...], ins[3 * j + 2][...])
            for ref, val in zip(outs[4 * j:4 * j + 4], (g, d, nm, nv)):
                ref[...] = val
        outs[-1][...] = rows_ref[n - 1:n, n_sink:n_sink + 1]

    flat = [a for p in params for a in p]
    outs = pl.pallas_call(body, name="adamw_small", out_shape=[_sds(p[0].shape, F32) for p in params for _ in range(4)] + [_sds((1, 1), F32)],
                          compiler_params=_params())(rows, *flat)
    return [outs[4 * j:4 * j + 4] for j in range(n)], outs[-1]


def _adamw(w, gp, gq, m, v, name):
    R, C = w.shape
    tr = R // 2 if (R // 2) % SUBLANES == 0 else R

    def body(w_ref, gp_ref, gq_ref, m_ref, v_ref, g_ref, d_ref, nm_ref, nv_ref):
        gv = gp_ref[...] + gq_ref[...]
        g_ref[...] = gv
        d_ref[...], nm_ref[...], nv_ref[...] = _adamw_math(w_ref[...], gv, m_ref[...], v_ref[...])

    blk = pl.BlockSpec((tr, C), lambda t: (t, 0))
    return pl.pallas_call(body, name=name, grid=(R // tr,), in_specs=[blk] * 5, out_specs=[blk] * 4,
                          out_shape=[_sds((R, C), F32)] * 4, compiler_params=_params(("parallel",)))(w, gp, gq, m, v)


def _rope(positions, after):
    inv_freq = 1.0 / (ROPE_THETA ** (jnp.arange(0, HEAD_DIM, 2, dtype=F32) / HEAD_DIM))
    inv_freq = jnp.tile(inv_freq, LANES // (HEAD_DIM // 2)).reshape(1, LANES) + after[0, 0]
    return _rope_tables(positions.reshape(-1, 1), inv_freq)


def _local_step(x, rope, target, norms, a_sink, comm):
    T, D = x.shape
    g1, gm, g2, gf = norms
    cos, sin = rope
    no_sink = jnp.zeros((2 * (B_W // LANES),), F32)
    W = {k: comm.weight(k, x) for k in ("wg1", "wu1", "wd1")}

    x1, h1, gate1, up1, act1 = _ffn_fwd(x, g1, W["wg1"], W["wu1"], W["wd1"], "ffn1_fwd", dep=comm.dep())
    W["w_in"] = comm.weight("w_in", x1)
    (h2, aq, ak, av, bq1, bk1, bv1, bq4, bk4, bv4, bq16, bk16, bv16) = _proj_rope(x1, gm, W["w_in"], cos, sin)
    cat, a_lse = _attn_fwd(aq[None], ak[None], av[None], a_sink, A_HALF_WINDOW, True, BF16, "attn_a_fwd", qb=2 * QB, blocks_per_step=4,
                           out_cols=A_Q_W + B_W)
    bqs = {1: (bq1[None], bk1[None], bv1[None]), 4: (bq4, bk4, bv4), 16: (bq16, bk16, bv16)}
    (b_hw,) = {w // (2 * d) for w, d in B_PATTERNS}
    cat, lg1, lg4, lg16 = _dilated_fwd(cat[0], bqs, b_hw)
    lg1 = lg1[0]
    W["w_out"] = comm.weight("w_out", cat)
    x2 = _out_proj(x1, cat, W["w_out"])
    for k in ("wg2", "wu2", "wd2"):
        W[k] = comm.weight(k, x2)
    dx3, h3, gate2, up2, act2, dgf, loss8 = _ffn_fwd(x2, g2, W["wg2"], W["wu2"], W["wd2"], "ffn2_fwd", loss=(gf, target))

    dx2, dff2, dgate2, dup2, dg2 = _ffn_dx(dx3, x2, g2, gate2, up2, W["wg2"], W["wu2"], W["wd2"], "ffn2_dx")
    fb = gate2.shape[1] // 2
    dwg2 = _tn(dgate2, h3, fb, "ffn2_dw_gate")
    dwu2 = _tn(dup2, h3, fb, "ffn2_dw_up")
    dwd2 = _tn(act2, dff2, fb, "ffn2_dw_down")
    comm.ready(dict(wg2=dwg2, wu2=dwu2, wd2=dwd2), dwd2[0])

    doa, dla, dob1, dlb1, dob4, dlb4, dob16, dlb16, dw_out = _dcat(dx2, W["w_out"], cat, dep=comm.dep())
    dqa, dka, dva, dsk = _attn_bwd(aq[None], ak[None], av[None], doa[None], a_lse, dla[None], a_sink, A_HALF_WINDOW, True, "attn_a_bwd")
    bwd_in = {1: (dob1[None], lg1[None], dlb1[None]), 4: (dob4, lg4, dlb4), 16: (dob16, lg16, dlb16)}
    bg = {}
    for w, d in B_PATTERNS:
        q_, k_, v_ = bqs[d]
        do_, l_, dl_ = bwd_in[d]
        bg[d] = _attn_bwd(q_, k_, v_, do_, l_, dl_, no_sink, w // (2 * d), False, f"attn_b{d}_bwd")[:3]
    dproj, dx1, dgm = _mixer_in_bwd(dqa[0], dka[0], dva[0], [t[0] for t in bg[1]], bg[4], bg[16], cos, sin, W["w_in"], x1, gm, dx2)
    dw_in = _tn(dproj, h2, dproj.shape[1] // 2, "w_in_dw")
    comm.ready(dict(w_in=dw_in, w_out=dw_out), dw_in[0])

    dx0, dff1, dgate1, dup1, dg1 = _ffn_dx(dx1, x, g1, gate1, up1, W["wg1"], W["wu1"], W["wd1"], "ffn1_dx", dep=comm.dep())
    comm.settle(2, dx0)
    dwd1 = _tn(act1, dff1, fb, "ffn1_dw_down", dep=comm.dep())
    comm.ready(dict(wd1=dwd1), dwd1[0])
    dwg1 = _tn(dgate1, h1, fb, "ffn1_dw_gate", dep=comm.dep())
    comm.ready(dict(wg1=dwg1), dwg1[0])
    dwu1 = _tn(dup1, h1, fb, "ffn1_dw_up", dep=comm.dep())
    comm.ready(dict(wu1=dwu1), dwu1[0])

    dsink = dsk[0, :, :, ::HEAD_DIM].sum(axis=1).reshape(-1)
    small = dict(g1=dg1.sum(axis=0), gm=dgm.sum(axis=0), g2=dg2.sum(axis=0), gf=dgf.sum(axis=0), sink=dsink, loss=loss8[0, 0])
    return dx0, small


BIG = ("wg1", "wu1", "wd1", "w_in", "w_out", "wg2", "wu2", "wd2")
GATHER_GROUPS = (("w_in",), ("w_out",), ("wg2", "wu2", "wd2"))


class _Comm:
    def __init__(self, shards, meanwhile):
        x, y, c = _mesh_pos()
        self.me = (2 * x + y).astype(jnp.int32).reshape(1)
        self.shards = shards
        self.token = None
        self.waiting = {}
        self.groups = []
        self.swaps = []
        first = ("wg1", "wu1", "wd1")
        fulls = {k: _cast_place(self.me, shards[k], f"cast_{k}") for k in first}
        plan = _neighbour_plan([fulls[k].shape for k in first])
        send, recv, bufs, tok = _push_start("gather_first_start", [fulls[k] for k in first], 2 * len(first), plan, self.me)
        self.side = meanwhile(tok)
        fulls.update({k: _cast_place(self.me, shards[k], f"cast_{k}") for k in BIG if k not in first})
        bufs = _push_wait("gather_first_wait", send, recv, bufs, plan, [fulls[k] for k in BIG if k not in first] + list(self.side))
        self.full = dict(zip(first, _gather_forward(bufs)))
        dep = self.full["wd1"]
        for gi, names in enumerate(GATHER_GROUPS):
            plan = _gather_plan(len(names))
            send, recv, bufs, self.token = _push_start(f"gather_start_{gi}", [fulls[k] for k in names], 3 * len(names), plan, dep)
            dep = self.token
            for k in names:
                self.waiting[k] = (gi, names, send, recv, bufs, plan)

    def dep(self):
        return self.token

    def weight(self, name, after):
        if name in self.waiting:
            gi, names, send, recv, bufs, plan = self.waiting[name]
            for k, buf in zip(names, _push_wait(f"gather_wait_{gi}", send, recv, bufs, plan, after)):
                self.full[k] = buf
                del self.waiting[k]
        full = self.full[name]
        return full.reshape(N_CHIPS * full.shape[1], full.shape[2])

    def ready(self, grads, after):
        names = list(grads)
        f32s, b16s = [], []
        for k in names:
            gf, gb = grads[k]
            f32s.append(gf.reshape((N_CHIPS,) + self.shards[k].shape))
            b16s.append(gb.reshape((N_CHIPS,) + self.shards[k].shape))
        n = len(names)
        lands = [lax.empty((N_CHIPS - 1,) + self.shards[k].shape, BF16) for k in names]
        plan = _grad_push_plan(n)
        send, recv, bufs, self.token = _push_start(f"grad_start_{names[0]}", b16s + lands, 3 * n, plan, after)
        self.groups.append((names, f32s, send, recv, bufs, plan))

    def settle(self, count, after):
        batch, self.groups = self.groups[:count], self.groups[count:]
        names_b, mine_b = [], []
        for names, f32s, send, recv, bufs, plan in batch:
            n = len(names)
            bufs = _push_wait(f"grad_wait_{names[0]}", send, recv, bufs, plan, mine_b[-1] if mine_b else after)
            mine_b += [_sum_own(self.me, f32s[i], bufs[n + i], f"sum_{k}") for i, k in enumerate(names)]
            names_b += names
        lands = [lax.empty(p.shape, F32) for p in mine_b]
        n = len(names_b)
        send2, recv2, both, self.token = _push_start(f"swap_start_{names_b[0]}", mine_b + lands, n, _swap_plan(n), after)
        self.swaps.append((names_b, send2, recv2, both))

    def partials(self, after):
        names_b, send2, recv2, both = self.swaps.pop(0)
        n = len(names_b)
        both = _push_wait(f"swap_wait_{names_b[0]}", send2, recv2, both, _swap_plan(n), after)
        return {k: (both[i], both[n + i]) for i, k in enumerate(names_b)}


def kernel(x, positions, norm_ffn1, w_gate1, w_up1, w_down1, norm_mix, w_in, a_sink, w_out, norm_ffn2, w_gate2, w_up2, w_down2, norm_final, loss_target, m_norm_ffn1, m_w_gate1, m_w_up1, m_w_down1, m_norm_mix, m_w_in, m_a_sink, m_w_out, m_norm_ffn2, m_w_gate2, m_w_up2, m_w_down2, m_norm_final, v_norm_ffn1, v_w_gate1, v_w_up1, v_w_down1, v_norm_mix, v_w_in, v_a_sink, v_w_out, v_norm_ffn2, v_w_gate2, v_w_up2, v_w_down2, v_norm_final):
    T, D = x.shape[1], x.shape[2]
    flip = ("wg1", "wu1", "w_in", "wg2", "wu2")

    def rows(k, a):
        return a[0].T if k in flip else a[0]

    given = dict(wg1=(w_gate1, m_w_gate1, v_w_gate1), wu1=(w_up1, m_w_up1, v_w_up1), wd1=(w_down1, m_w_down1, v_w_down1),
                 w_in=(w_in, m_w_in, v_w_in), w_out=(w_out, m_w_out, v_w_out), wg2=(w_gate2, m_w_gate2, v_w_gate2),
                 wu2=(w_up2, m_w_up2, v_w_up2), wd2=(w_down2, m_w_down2, v_w_down2))
    shards = {k: rows(k, given[k][0]) for k in BIG}

    comm = _Comm(shards, lambda tok: _rope(positions[0], tok))

    norms = (norm_ffn1, norm_mix, norm_ffn2, norm_final.reshape(1, D))
    grad_x, small = _local_step(x[0], comm.side, loss_target[0], norms, a_sink[0], comm)

    upd = {}

    def update(partial):
        for k in partial:
            outs = _adamw(shards[k], partial[k][0], partial[k][1], rows(k, given[k][1]), rows(k, given[k][2]), f"adamw_{k}")
            upd[k] = tuple((a.T if k in flip else a)[None] for a in outs)
        return outs[0]

    last = update(comm.partials(comm.dep()))
    comm.settle(2, last)

    def pad_row(a):
        a = a.reshape(-1)
        return jnp.pad(a, (0, D - a.shape[0]))

    row4 = pad_row(jnp.concatenate([small["sink"], small["loss"].reshape(1)]))
    vec = jnp.stack([small["g1"], small["gm"], small["g2"], small["gf"], row4] + [jnp.zeros((D,), F32)] * 3, axis=0)
    red = _allreduce_small(vec, comm.dep())
    comm.settle(1, red)
    last = update(comm.partials(comm.dep()))
    update(comm.partials(last))
    as_row = lambda a: a.reshape(1, -1)
    sm, loss = _adamw_small(red, [tuple(as_row(a) for a in p) for p in (
        (norm_ffn1, m_norm_ffn1, v_norm_ffn1), (norm_mix, m_norm_mix, v_norm_mix), (norm_ffn2, m_norm_ffn2, v_norm_ffn2),
        (norm_final, m_norm_final, v_norm_final), (a_sink, m_a_sink, v_a_sink))])
    sm[3] = [a.reshape(D) for a in sm[3]]

    def ordered(i):
        return [sm[0][i], upd["wg1"][i], upd["wu1"][i], upd["wd1"][i], sm[1][i], upd["w_in"][i], sm[4][i], upd["w_out"][i], sm[2][i],
                upd["wg2"][i], upd["wu2"][i], upd["wd2"][i], sm[3][i]]

    return (loss.reshape(()), grad_x[None], *ordered(0), *ordered(1), *ordered(2), *ordered(3))
```

```python
import jax
import jax.numpy as jnp
from jax import lax
from jax.experimental import pallas as pl
from jax.experimental.pallas import tpu as pltpu

F32 = jnp.float32
BF16 = jnp.bfloat16

HEAD_DIM = 64
LANES = 128
SUBLANES = 8
A_Q_W, A_KV_W, B_W = 512, 128, 512
A_HALF_WINDOW = 128
B_PATTERNS = ((128, 1), (512, 4), (2048, 16))
ROPE_THETA = 10000.0
NORM_EPS = 1e-6
FFN_RES_WEIGHT = 0.5
ADAM_LR, ADAM_B1, ADAM_B2, ADAM_EPS, ADAM_WD, ADAM_STEP = 0.001, 0.9, 0.999, 1e-08, 0.01, 10
N_CHIPS = 4
N_DEV = 8
QB = 128
SHORT_SEQ = 512
NEG = -1e30
VMEM_LIMIT = 56 * 1024 * 1024
MESH = pl.DeviceIdType.MESH
ANY = pl.BlockSpec(memory_space=pl.ANY)


def _params(sem=None):
    return pltpu.CompilerParams(dimension_semantics=sem, vmem_limit_bytes=VMEM_LIMIT)


def _sds(shape, dtype):
    return jax.ShapeDtypeStruct(tuple(shape), dtype)


def _dot(a, b):
    return jnp.dot(a, b, preferred_element_type=F32)


def _dot_nt(a, b):
    return lax.dot_general(a, b, (((1,), (1,)), ((), ())), preferred_element_type=F32)


def _dot_tn(a, b):
    return lax.dot_general(a, b, (((0,), (0,)), ((), ())), preferred_element_type=F32)


def _rms_stats(x):
    r = lax.rsqrt(jnp.mean(x * x, axis=-1, keepdims=True) + NORM_EPS)
    return x * r, r


def _rms_bwd(dh, x, g):
    xhat, r = _rms_stats(x)
    dxn = dh * g
    dx = r * (dxn - xhat * jnp.mean(dxn * xhat, axis=-1, keepdims=True))
    tm, d = x.shape
    dg = (dh * xhat).reshape(tm // SUBLANES, SUBLANES, d).sum(axis=0)
    return dx, dg


def _sigmoid(x):
    return 1.0 / (1.0 + jnp.exp(-x))


def _swap32(t):
    n = t.shape[-1]
    lane = lax.broadcasted_iota(jnp.int32, t.shape, t.ndim - 1)
    return jnp.where((lane % HEAD_DIM) < HEAD_DIM // 2, pltpu.roll(t, n - HEAD_DIM // 2, axis=t.ndim - 1),
                     pltpu.roll(t, HEAD_DIM // 2, axis=t.ndim - 1))


def _ordered(body, n_in, dep):
    if dep is None:
        return body, [], []

    def ordered(*refs):
        body(*refs[:n_in], *refs[n_in + 1:])

    return ordered, [ANY], [dep]


def _cast_place(me_arr, w, name):
    R, C = w.shape
    tr = R // 2 if (R // 2) % 16 == 0 else R

    def body(me_ref, w_ref, o_ref):
        o_ref[...] = w_ref[...].astype(BF16)

    grid_spec = pltpu.PrefetchScalarGridSpec(
        num_scalar_prefetch=1, grid=(R // tr,), in_specs=[pl.BlockSpec((tr, C), lambda t, me: (t, 0))],
        out_specs=pl.BlockSpec((None, tr, C), lambda t, me: (me[0], t, 0)))
    return pl.pallas_call(body, name=name, grid_spec=grid_spec, out_shape=_sds((N_CHIPS, R, C), BF16),
                          compiler_params=_params(("parallel",)))(me_arr, w)


HBM = pl.BlockSpec(memory_space=pltpu.HBM)
SEM = pl.BlockSpec(memory_space=pltpu.SEMAPHORE)


def _push_start(name, bufs, ncopies, plan, after):
    nb = len(bufs)

    def body(*refs):
        send, recv, token = refs[nb + 1], refs[nb + 2], refs[-1]
        for i, (src, dst, dev) in enumerate(plan(refs[:nb])):
            pltpu.make_async_remote_copy(src_ref=src, dst_ref=dst, send_sem=send.at[i], recv_sem=recv.at[i],
                                         device_id=dev, device_id_type=MESH).start()
        token[...] = jnp.zeros_like(token)

    outs = pl.pallas_call(
        body, name=name,
        out_shape=(pltpu.SemaphoreType.DMA((ncopies,)), pltpu.SemaphoreType.DMA((ncopies,)), *[pltpu.HBM(b.shape, b.dtype) for b in bufs],
                   _sds((SUBLANES, LANES), F32)),
        in_specs=[HBM] * nb + [ANY], out_specs=(SEM, SEM, *([HBM] * nb), pl.BlockSpec(memory_space=pltpu.VMEM)),
        input_output_aliases={i: 2 + i for i in range(nb)},
        compiler_params=pltpu.CompilerParams(has_side_effects=pltpu.SideEffectType.DATAFLOW_SIDE_EFFECTING),
    )(*[pltpu.with_memory_space_constraint(b, pltpu.HBM) for b in bufs], after)
    return outs[0], outs[1], list(outs[2:2 + nb]), outs[-1]


def _push_wait(name, send, recv, bufs, plan, after):
    nb = len(bufs)

    def body(*refs):
        send_ref, recv_ref = refs[nb], refs[nb + 1]
        for i, (src, dst, dev) in enumerate(plan(refs[:nb])):
            cp = pltpu.make_async_remote_copy(src_ref=src, dst_ref=dst, send_sem=send_ref.at[i], recv_sem=recv_ref.at[i],
                                              device_id=dev, device_id_type=MESH)
            cp.wait_send()
            cp.wait_recv()

    afters = list(after) if isinstance(after, (list, tuple)) else [after]
    outs = pl.pallas_call(
        body, name=name, out_shape=tuple(pltpu.HBM(b.shape, b.dtype) for b in bufs),
        in_specs=[HBM] * nb + [SEM, SEM] + [ANY] * len(afters), out_specs=tuple([HBM] * nb),
        input_output_aliases={i: i for i in range(nb)},
        compiler_params=pltpu.CompilerParams(has_side_effects=pltpu.SideEffectType.DATAFLOW_SIDE_EFFECTING),
    )(*bufs, send, recv, *afters)
    return list(outs)


def _mesh_pos():
    return lax.axis_index("x"), lax.axis_index("y"), lax.axis_index("c")


def _chip_peers(x, y, c):
    return [((1 - x, y, c), 2 * (1 - x) + y), ((x, 1 - y, c), 2 * x + (1 - y)), ((1 - x, 1 - y, c), 2 * (1 - x) + (1 - y))]


def _gather_plan(n):
    def plan(refs):
        x, y, c = _mesh_pos()
        me = 2 * x + y
        return [(refs[k].at[me], refs[k].at[me], dev) for k in range(n) for dev, _ in _chip_peers(x, y, c)]
    return plan


def _rows_of(shape, who, quarter=None):
    r2 = shape[1] // 2
    if quarter is None:
        return pl.ds(pl.multiple_of(who * r2, 16), r2)
    return pl.ds(pl.multiple_of(who * r2 + quarter * (r2 // 2), 16), r2 // 2)


def _neighbour_plan(shapes):
    def plan(refs):
        x, y, c = _mesh_pos()
        me = 2 * x + y
        return [(refs[k].at[me, _rows_of(shp, c), :], refs[k].at[me, _rows_of(shp, c), :], dev)
                for k, shp in enumerate(shapes) for dev in ((1 - x, y, c), (x, 1 - y, c))]
    return plan


def _gather_forward(fulls):
    n = len(fulls)

    def body(*refs):
        ins, outs = refs[:n], refs[n:2 * n]
        ici_send, ici_recv, d2d_send, d2d_recv = refs[2 * n:]
        x, y, c = _mesh_pos()
        cx, cy, cd = 2 * (1 - x) + y, 2 * x + (1 - y), 2 * (1 - x) + (1 - y)
        sibling, x_nbr, y_nbr = (x, y, 1 - c), (1 - x, y, c), (x, 1 - y, c)
        started = []

        def push(src, dst, send, recv, dev):
            cp = pltpu.make_async_remote_copy(src_ref=src, dst_ref=dst, send_sem=send, recv_sem=recv, device_id=dev, device_id_type=MESH)
            cp.start()
            started.append(cp)

        def arrived(blk, send, recv):
            pltpu.make_async_remote_copy(src_ref=blk, dst_ref=blk, send_sem=send, recv_sem=recv, device_id=sibling,
                                         device_id_type=MESH).wait_recv()

        for k in range(n):
            shp = fulls[k].shape
            for j, chip in enumerate((cx, cy)):
                push(ins[k].at[chip, _rows_of(shp, c), :], outs[k].at[chip, _rows_of(shp, c), :],
                     d2d_send.at[3 * k + j], d2d_recv.at[3 * k + j], sibling)
            push(ins[k].at[cx, _rows_of(shp, c, 0), :], outs[k].at[cx, _rows_of(shp, c, 0), :], ici_send.at[2 * k], ici_recv.at[2 * k], y_nbr)
            push(ins[k].at[cy, _rows_of(shp, c, 1), :], outs[k].at[cy, _rows_of(shp, c, 1), :], ici_send.at[2 * k + 1], ici_recv.at[2 * k + 1],
                 x_nbr)
        for k in range(n):
            shp = fulls[k].shape
            for q in (0, 1):
                arrived(outs[k].at[cd, _rows_of(shp, c, q), :], ici_send.at[2 * k + q], ici_recv.at[2 * k + q])
            blk = outs[k].at[cd, _rows_of(shp, c), :]
            push(blk, blk, d2d_send.at[3 * k + 2], d2d_recv.at[3 * k + 2], sibling)
        for k in range(n):
            for j, chip in enumerate((cx, cy, cd)):
                arrived(outs[k].at[chip, _rows_of(fulls[k].shape, 1 - c), :], d2d_send.at[3 * k + j], d2d_recv.at[3 * k + j])
        for cp in started:
            cp.wait_send()

    return pl.pallas_call(
        body, name="gather_forward", out_shape=[_sds(f.shape, BF16) for f in fulls],
        in_specs=[ANY] * n, out_specs=[ANY] * n, input_output_aliases={k: k for k in range(n)},
        scratch_shapes=[pltpu.SemaphoreType.DMA((n * 2,))] * 2 + [pltpu.SemaphoreType.DMA((n * 3,))] * 2,
        compiler_params=_params())(*fulls)


def _resident(shape):
    return pl.BlockSpec(shape, lambda i: (0,) * len(shape), pipeline_mode=pl.Buffered(1))


FFN_FWD_CHUNK = 256
FFN_DX_CHUNK = 512


def _chunks(n, step):
    return [(c0, min(step, n - c0)) for c0 in range(0, n, step)]


def _two_phase(chunks, first, second):
    held = {}
    for ci, ch in enumerate(chunks):
        held[ci] = first(*ch)
        if ci >= 1:
            second(*chunks[ci - 1], held.pop(ci - 1))
    last = len(chunks) - 1
    second(*chunks[last], held.pop(last))


def _loss_and_grad(x, g, target):
    D = x.shape[1]
    xhat, _ = _rms_stats(x)
    err = xhat * g - target
    loss = 0.5 * jnp.sum(jnp.sum(err * err, axis=-1, keepdims=True) * (1.0 / D), axis=0, keepdims=True)
    dx, dg = _rms_bwd(err * (1.0 / D), x, g)
    return dx, dg, loss


def _ffn_fwd(x, g, wgt, wut, wd, name, tm=512, dep=None, loss=None):
    T, D = x.shape
    F = wd.shape[0]
    n_in = 5 if loss is None else 7

    def body(*refs):
        x_ref, g_ref, wg_ref, wu_ref, wd_ref = refs[:5]
        xo_ref, h_ref, gate_ref, up_ref, act_ref = refs[n_in:n_in + 5]
        xv = x_ref[...]
        xhat, _ = _rms_stats(xv)
        h = (xhat * g_ref[...]).astype(BF16)
        h_ref[...] = h
        acc = []

        def first(c0, cw):
            return _dot_nt(h, wg_ref[c0:c0 + cw, :]), _dot_nt(h, wu_ref[c0:c0 + cw, :])

        def second(c0, cw, gate_up):
            gate, up = gate_up
            act = ((gate * _sigmoid(gate)) * up).astype(BF16)
            gate_ref[:, c0:c0 + cw] = gate.astype(BF16)
            up_ref[:, c0:c0 + cw] = up.astype(BF16)
            act_ref[:, c0:c0 + cw] = act
            d = _dot(act, wd_ref[c0:c0 + cw, :])
            acc[:] = [d if not acc else acc[0] + d]

        _two_phase(_chunks(F, FFN_FWD_CHUNK), first, second)
        xo = xv + FFN_RES_WEIGHT * acc[0]
        if loss is None:
            xo_ref[...] = xo
        else:
            gf_ref, t_ref = refs[5:7]
            dgf_ref, loss_ref = refs[n_in + 5:]

            @pl.when(pl.program_id(0) == 0)
            def _():
                dgf_ref[...] = jnp.zeros_like(dgf_ref)
                loss_ref[...] = jnp.zeros_like(loss_ref)

            xo_ref[...], dgf, part = _loss_and_grad(xo, gf_ref[...], t_ref[...])
            dgf_ref[...] += dgf
            loss_ref[...] += part

    row = pl.BlockSpec((tm, D), lambda i: (i, 0))
    gain = pl.BlockSpec((1, D), lambda i: (0, 0))
    saved = pl.BlockSpec((tm, F), lambda i: (i, 0))
    in_specs = [row, gain, _resident(wgt.shape), _resident(wut.shape), _resident(wd.shape)]
    out_specs = [row, row, saved, saved, saved]
    out_shape = [_sds((T, D), F32), _sds((T, D), BF16), _sds((T, F), BF16), _sds((T, F), BF16), _sds((T, F), BF16)]
    if loss is not None:
        in_specs += [gain, row]
        out_specs += [pl.BlockSpec((SUBLANES, D), lambda i: (0, 0)), pl.BlockSpec((SUBLANES, LANES), lambda i: (0, 0))]
        out_shape += [_sds((SUBLANES, D), F32), _sds((SUBLANES, LANES), F32)]
    body, dep_spec, dep_arg = _ordered(body, n_in, dep)
    return pl.pallas_call(
        body, name=name, grid=(T // tm,), in_specs=in_specs + dep_spec, out_specs=out_specs, out_shape=out_shape,
        compiler_params=_params(("parallel",) if loss is None else ("arbitrary",)))(x, g, wgt, wut, wd, *(loss or ()), *dep_arg)


def _ffn_up(x, g, wgt, wut, name, tm=512, dep=None):
    T, D = x.shape
    F = wgt.shape[0]

    def body(x_ref, g_ref, wg_ref, wu_ref, h_ref, gate_ref, up_ref, act_ref):
        xhat, _ = _rms_stats(x_ref[...])
        h = (xhat * g_ref[...]).astype(BF16)
        h_ref[...] = h

        def first(c0, cw):
            return _dot_nt(h, wg_ref[c0:c0 + cw, :]), _dot_nt(h, wu_ref[c0:c0 + cw, :])

        def second(c0, cw, gate_up):
            gate, up = gate_up
            gate_ref[:, c0:c0 + cw] = gate.astype(BF16)
            up_ref[:, c0:c0 + cw] = up.astype(BF16)
            act_ref[:, c0:c0 + cw] = ((gate * _sigmoid(gate)) * up).astype(BF16)

        _two_phase(_chunks(F, FFN_FWD_CHUNK), first, second)

    row = pl.BlockSpec((tm, D), lambda i: (i, 0))
    saved = pl.BlockSpec((tm, F), lambda i: (i, 0))
    body, dep_spec, dep_arg = _ordered(body, 4, dep)
    return pl.pallas_call(
        body, name=name, grid=(T // tm,),
        in_specs=[row, pl.BlockSpec((1, D), lambda i: (0, 0)), _resident(wgt.shape), _resident(wut.shape)] + dep_spec,
        out_specs=[row, saved, saved, saved],
        out_shape=[_sds((T, D), BF16), _sds((T, F), BF16), _sds((T, F), BF16), _sds((T, F), BF16)],
        compiler_params=_params(("parallel",)))(x, g, wgt, wut, *dep_arg)


def _ffn_down(x, act, wd, name, tm=512):
    T, D = x.shape
    F = wd.shape[0]

    def body(x_ref, a_ref, wd_ref, xo_ref):
        xo_ref[...] = x_ref[...] + FFN_RES_WEIGHT * _dot(a_ref[...], wd_ref[...])

    row = pl.BlockSpec((tm, D), lambda i: (i, 0))
    return pl.pallas_call(
        body, name=name, grid=(T // tm,), in_specs=[row, pl.BlockSpec((tm, F), lambda i: (i, 0)), _resident(wd.shape)],
        out_specs=row, out_shape=_sds((T, D), F32), compiler_params=_params(("parallel",)))(x, act, wd)


def _ffn_dx(dxo, x, g, gate_s, up_s, wgt, wut, wd, name, tm=256, dep=None):
    T, D = x.shape
    F = wd.shape[0]

    def body(dxo_ref, x_ref, g_ref, gate_ref, up_ref, wg_ref, wu_ref, wd_ref, dx_ref, dff_ref, dgate_ref, dup_ref, dg_ref):
        @pl.when(pl.program_id(0) == 0)
        def _():
            dg_ref[...] = jnp.zeros_like(dg_ref)

        d = (FFN_RES_WEIGHT * dxo_ref[...]).astype(BF16)
        dff_ref[...] = d
        dh = []

        def first(c0, cw):
            return _dot_nt(d, wd_ref[c0:c0 + cw, :])

        def second(c0, cw, da):
            gate = gate_ref[:, c0:c0 + cw].astype(F32)
            up = up_ref[:, c0:c0 + cw].astype(F32)
            s = _sigmoid(gate)
            silu = gate * s
            dup = (da * silu).astype(BF16)
            dgate = (da * up * (s * (1.0 + gate * (1.0 - s)))).astype(BF16)
            dgate_ref[:, c0:c0 + cw] = dgate
            dup_ref[:, c0:c0 + cw] = dup
            t = _dot(dgate, wg_ref[c0:c0 + cw, :]) + _dot(dup, wu_ref[c0:c0 + cw, :])
            dh[:] = [t if not dh else dh[0] + t]

        _two_phase(_chunks(F, FFN_DX_CHUNK), first, second)
        dxn, dg = _rms_bwd(dh[0], x_ref[...], g_ref[...])
        dg_ref[...] += dg
        dx_ref[...] = dxo_ref[...] + dxn

    row = pl.BlockSpec((tm, D), lambda i: (i, 0))
    saved = pl.BlockSpec((tm, F), lambda i: (i, 0))
    body, dep_spec, dep_arg = _ordered(body, 8, dep)
    return pl.pallas_call(
        body, name=name, grid=(T // tm,),
        in_specs=[row, row, pl.BlockSpec((1, D), lambda i: (0, 0)), saved, saved, _resident(wgt.shape), _resident(wut.shape),
                  _resident(wd.shape)] + dep_spec,
        out_specs=[row, row, saved, saved, pl.BlockSpec((SUBLANES, D), lambda i: (0, 0))],
        out_shape=[_sds((T, D), F32), _sds((T, D), BF16), _sds((T, F), BF16), _sds((T, F), BF16), _sds((SUBLANES, D), F32)],
        compiler_params=_params(("arbitrary",)))(dxo, x, g, gate_s, up_s, wgt, wut, wd, *dep_arg)


def _tn(a, b, mb, name, tk=2048, dep=None):
    T, M = a.shape
    N = b.shape[1]
    nt = T // tk

    def body(a_ref, b_ref, o_ref, ob_ref):
        @pl.when(pl.program_id(1) == 0)
        def _():
            o_ref[...] = jnp.zeros_like(o_ref)

        o_ref[...] += _dot_tn(a_ref[...].astype(BF16), b_ref[...].astype(BF16))

        @pl.when(pl.program_id(1) == nt - 1)
        def _():
            ob_ref[...] = o_ref[...].astype(BF16)

    o_spec = pl.BlockSpec((mb, N), lambda g, t: (g, 0))
    body, dep_spec, dep_arg = _ordered(body, 2, dep)
    return pl.pallas_call(
        body, name=name, grid=(M // mb, nt),
        in_specs=[pl.BlockSpec((tk, mb), lambda g, t: (t, g)), pl.BlockSpec((tk, N), lambda g, t: (t, 0))] + dep_spec,
        out_specs=[o_spec, o_spec], out_shape=[_sds((M, N), F32), _sds((M, N), BF16)],
        compiler_params=_params(("parallel", "arbitrary")))(a, b, *dep_arg)


def _rope_tables(pos_col, inv_freq):
    T = pos_col.shape[0]

    def body(p_ref, f_ref, c_ref, s_ref):
        ang = p_ref[...].astype(F32) * f_ref[...]
        lane = lax.broadcasted_iota(jnp.int32, ang.shape, 1)
        c_ref[...] = jnp.cos(ang)
        sn = jnp.sin(ang)
        s_ref[...] = jnp.where((lane % HEAD_DIM) < HEAD_DIM // 2, -sn, sn)

    tm = 1024
    return pl.pallas_call(
        body, name="rope_tables", grid=(T // tm,),
        in_specs=[pl.BlockSpec((tm, 1), lambda i: (i, 0)), pl.BlockSpec((1, LANES), lambda i: (0, 0))],
        out_specs=[pl.BlockSpec((tm, LANES), lambda i: (i, 0))] * 2,
        out_shape=[_sds((T, LANES), F32)] * 2, compiler_params=_params(("parallel",)))(pos_col, inv_freq)


def _deinterleave(scr, out_ref, d, tm, nblk):
    for r in range(d):
        for cb in range(nblk):
            out_ref[r, :, cb * LANES:(cb + 1) * LANES] = scr[cb, pl.ds(r, tm // d, stride=d), :].astype(out_ref.dtype)


def _interleave(in_ref, scr, d, tm, nblk):
    for r in range(d):
        for cb in range(nblk):
            scr[cb, pl.ds(r, tm // d, stride=d), :] = in_ref[r, :, cb * LANES:(cb + 1) * LANES].astype(F32)


def _proj_rope(x, g, w_in, cos, sin, tm=512):
    T, D = x.shape
    dils = [d for _, d in B_PATTERNS if d > 1]
    nbb = B_W // LANES
    scale = HEAD_DIM ** -0.5
    cuts = [0, A_Q_W, A_Q_W + A_KV_W, A_Q_W + 2 * A_KV_W, A_Q_W + 2 * A_KV_W + B_W, A_Q_W + 2 * A_KV_W + 2 * B_W,
            A_Q_W + 2 * A_KV_W + 3 * B_W]

    def body(x_ref, g_ref, w_ref, c_ref, s_ref, h_ref, aq_ref, ak_ref, av_ref, *rest):
        b_refs, scr = rest[:-1], rest[-1]
        xhat, _ = _rms_stats(x_ref[...])
        h = (xhat * g_ref[...]).astype(BF16)
        h_ref[...] = h
        cs, sn = c_ref[...], s_ref[...]

        def project(idx, ref, rope, mult, which):
            return _dot_nt(h, w_ref[cuts[idx]:cuts[idx + 1], :])

        def finish(idx, ref, rope, mult, which, whole):
            for cb in range((cuts[idx + 1] - cuts[idx]) // LANES):
                p = whole[:, cb * LANES:(cb + 1) * LANES]
                if rope:
                    p = p * cs + _swap32(p) * sn
                if mult != 1.0:
                    p = p * mult
                ref[:, cb * LANES:(cb + 1) * LANES] = p.astype(BF16)
                if which is not None:
                    scr[which, cb] = p
            if which is not None:
                for di, d in enumerate(dils):
                    _deinterleave(scr.at[which], b_refs[3 * (di + 1) + which], d, tm, nbb)

        _two_phase([(0, aq_ref, True, scale, None), (1, ak_ref, True, 1.0, None), (2, av_ref, False, 1.0, None),
                    (3, b_refs[0], True, scale, 0), (4, b_refs[1], True, 1.0, 1), (5, b_refs[2], False, 1.0, 2)], project, finish)

    row = lambda w: pl.BlockSpec((tm, w), lambda i: (i, 0))
    out_specs = [row(D), row(A_Q_W), row(A_KV_W), row(A_KV_W)] + [row(B_W)] * 3
    out_shape = [_sds((T, D), BF16), _sds((T, A_Q_W), BF16), _sds((T, A_KV_W), BF16), _sds((T, A_KV_W), BF16)] + [_sds((T, B_W), BF16)] * 3
    for d in dils:
        out_specs += [pl.BlockSpec((d, tm // d, B_W), lambda i: (0, i, 0))] * 3
        out_shape += [_sds((d, T // d, B_W), BF16)] * 3
    return pl.pallas_call(
        body, name="proj_rope", grid=(T // tm,),
        in_specs=[row(D), pl.BlockSpec((1, D), lambda i: (0, 0)), pl.BlockSpec(w_in.shape, lambda i: (0, 0)), row(LANES), row(LANES)],
        out_specs=out_specs, out_shape=out_shape, scratch_shapes=[pltpu.VMEM((3, nbb, tm, LANES), F32)],
        compiler_params=_params(("parallel",)))(x, g, w_in, cos, sin)


def _band_bias(rel, qb, kw, hw):
    ri = lax.broadcasted_iota(jnp.int32, (2 * qb, kw), 0) & (qb - 1)
    ci = lax.broadcasted_iota(jnp.int32, (2 * qb, kw), 1)
    return jnp.where(jnp.abs(ri + rel - ci) <= hw, 0.0, NEG).astype(F32)


def _stack_heads(x, lo):
    z = jnp.zeros_like(x)
    return jnp.concatenate([jnp.where(lo, x, z), jnp.where(lo, z, x)], axis=0)


def _unstack_heads(y, lo):
    qb = y.shape[0] // 2
    return jnp.where(lo, y[:qb], y[qb:])


def _band_setup(bias_scr, qb, kw, hw):
    if bias_scr is not None:
        for i in range(3):
            bias_scr[i] = _band_bias(i * hw, qb, kw, hw)


def _band_window(bias_scr, qs, L, qb, kw, hw):
    ws = pl.multiple_of(jnp.clip(qs - hw, 0, L - kw), 64)
    if bias_scr is None:
        return ws, _band_bias(qs - ws, qb, kw, hw)
    return ws, bias_scr[lax.shift_right_logical(qs - ws, hw.bit_length() - 1)]


def _dup_kv_head(src_ref, dst_ref, head, L):
    step = min(L, 1024)
    for r0 in range(0, L, step):
        xf = src_ref[r0:r0 + step, :].astype(F32)
        lane = lax.broadcasted_iota(jnp.int32, xf.shape, 1)
        keep = jnp.logical_xor(lane < HEAD_DIM, head == 1)
        dst_ref[r0:r0 + step, :] = jnp.where(keep, xf, pltpu.roll(xf, HEAD_DIM, axis=1)).astype(dst_ref.dtype)


def _attn_fwd(q, k, v, sink, hw, gqa, out_dtype, name, qb=QB, blocks_per_step=8, out_cols=None):
    NB, L, Cq = q.shape
    Ls = min(L, 2048)
    kw = min(qb + 2 * hw, L)
    tables = L >= qb + 2 * hw
    unroll = min(blocks_per_step, Ls // qb)
    nlb = 1 if (gqa or L > SHORT_SEQ) else Cq // LANES

    def body(sink_ref, q_ref, k_ref, v_ref, o_ref, lse_ref, *scr):
        b, s_idx = pl.program_id(1), pl.program_id(2)
        bias_scr = scr[0] if tables else None
        _band_setup(bias_scr, qb, kw, hw)
        if gqa:
            kd, vd = scr[-2:]

            @pl.when(s_idx == 0)
            def _():
                _dup_kv_head(k_ref, kd, b // 2, L)
                _dup_kv_head(v_ref, vd, b // 2, L)
        else:
            kd, vd = k_ref, v_ref
        lane = lax.broadcasted_iota(jnp.int32, (qb, LANES), 1)
        lo = lane < HEAD_DIM
        if gqa:
            row = lax.broadcasted_iota(jnp.int32, (2 * qb, 1), 0)
            sk = jnp.where(row < qb, sink_ref[2 * b], sink_ref[2 * b + 1])

        def block(ql, col):
            qs = s_idx * Ls + ql
            ws, bias = _band_window(bias_scr, qs, L, qb, kw, hw)
            return ws, _dot_nt(_stack_heads(q_ref[pl.ds(ql, qb), col], lo), kd[pl.ds(ws, kw), col]) + bias

        def finish(ql, col, scores):
            ws, s = scores
            m = jnp.max(s, axis=-1, keepdims=True)
            if gqa:
                m = jnp.maximum(m, sk)
            p = jnp.exp(s - m)
            den = jnp.sum(p, axis=-1, keepdims=True)
            if gqa:
                den = den + jnp.exp(sk - m)
            o = _dot(p.astype(BF16), vd[pl.ds(ws, kw), col]) * (1.0 / den)
            o_ref[pl.ds(ql, qb), col] = _unstack_heads(o, lo).astype(o_ref.dtype)
            lse_ref[pl.ds(ql, qb), col] = _unstack_heads(m + jnp.log(den), lo)

        for lb in range(nlb):
            def step(n, carry, col=slice(lb * LANES, (lb + 1) * LANES)):
                _two_phase([(pl.multiple_of((n * unroll + u) * qb, qb), col) for u in range(unroll)], block, finish)
                return carry

            lax.fori_loop(0, Ls // (qb * unroll), step, 0)

    kv_map = (lambda r, b, s: (r, 0, 0)) if gqa else (lambda r, b, s: (r, 0, b))
    seg = pl.BlockSpec((None, Ls, nlb * LANES), lambda r, b, s: (r, s, b))
    return pl.pallas_call(
        body, name=name, grid=(NB, Cq // (nlb * LANES), L // Ls),
        in_specs=[pl.BlockSpec(memory_space=pltpu.SMEM), seg, pl.BlockSpec((None, L, nlb * LANES), kv_map),
                  pl.BlockSpec((None, L, nlb * LANES), kv_map)],
        out_specs=[seg, seg], out_shape=[_sds((NB, L, out_cols or Cq), out_dtype), _sds((NB, L, Cq), F32)],
        scratch_shapes=([pltpu.VMEM((3, 2 * qb, kw), F32)] if tables else []) + ([pltpu.VMEM((L, LANES), BF16)] * 2 if gqa else []),
        compiler_params=_params(("parallel", "parallel", "arbitrary")))(sink, q, k, v)


def _attn_bwd(q, k, v, do, lse, delta, sink, hw, gqa, name, qb=QB, blocks_per_step=8):
    NB, L, Cq = q.shape
    Ck = k.shape[2]
    Ls = min(L, 2048)
    kw = min(qb + 2 * hw, L)
    reps = kw // LANES
    nseg = L // Ls
    scale = HEAD_DIM ** -0.5
    tables = L >= qb + 2 * hw
    unroll = min(blocks_per_step, Ls // qb)
    nlb = 1 if (gqa or L > SHORT_SEQ) else Cq // LANES

    def body(sink_ref, q_ref, do_ref, lse_ref, dl_ref, k_ref, v_ref, dq_ref, dk_ref, dv_ref, dsk_ref, *scr):
        b, s_idx = pl.program_id(1), pl.program_id(2)
        lane = lax.broadcasted_iota(jnp.int32, (qb, LANES), 1)
        lo = lane < HEAD_DIM
        bias_scr = scr[0] if tables else None
        _band_setup(bias_scr, qb, kw, hw)
        if gqa:
            kd, vd, dk_acc, dv_acc, dsk_acc = scr[-5:]

            @pl.when(s_idx == 0)
            def _():
                _dup_kv_head(k_ref, kd, b // 2, L)
                _dup_kv_head(v_ref, vd, b // 2, L)
                dk_acc[...] = jnp.zeros_like(dk_acc)
                dv_acc[...] = jnp.zeros_like(dv_acc)
                dsk_acc[...] = jnp.zeros_like(dsk_acc)

            @pl.when((s_idx == 0) & (b == 0))
            def _():
                dk_ref[...] = jnp.zeros_like(dk_ref)
                dv_ref[...] = jnp.zeros_like(dv_ref)
        else:
            kd, vd = k_ref, v_ref
            dk_acc, dv_acc = scr[-2:]

            @pl.when(s_idx == 0)
            def _():
                dk_acc[...] = jnp.zeros_like(dk_acc)
                dv_acc[...] = jnp.zeros_like(dv_acc)

        def block(ql, col):
            qs = s_idx * Ls + ql
            ws, bias = _band_window(bias_scr, qs, L, qb, kw, hw)
            qv, dov = q_ref[pl.ds(ql, qb), col], do_ref[pl.ds(ql, qb), col]
            lse, dl = lse_ref[pl.ds(ql, qb), col], dl_ref[pl.ds(ql, qb), col]
            kv_, vv = kd[pl.ds(ws, kw), col], vd[pl.ds(ws, kw), col]
            q2, do2 = _stack_heads(qv, lo), _stack_heads(dov, lo)
            return ws, q2, do2, lse, dl, _dot_nt(q2, kv_) + bias, _dot_nt(do2, vv)

        def finish(ql, col, held):
            ws, q2, do2, lse, dl, s, dp = held
            lse_sw, dl_sw = pltpu.roll(lse, HEAD_DIM, axis=1), pltpu.roll(dl, HEAD_DIM, axis=1)
            lse2 = jnp.concatenate([jnp.where(lo, lse, lse_sw), jnp.where(lo, lse_sw, lse)], axis=0)
            dl2 = jnp.concatenate([jnp.where(lo, dl, dl_sw), jnp.where(lo, dl_sw, dl)], axis=0)
            p = jnp.exp(s - jnp.tile(lse2, (1, reps)))
            ds = (p * (dp - jnp.tile(dl2, (1, reps)))).astype(BF16)
            dq_ref[pl.ds(ql, qb), col] = (_unstack_heads(_dot(ds, kd[pl.ds(ws, kw), col]), lo) * scale).astype(dq_ref.dtype)
            both = _dot_tn(jnp.concatenate([ds, p.astype(BF16)], axis=1), jnp.concatenate([q2, do2], axis=1))
            dk_acc[pl.ds(ws, kw), col] += both[:kw, :LANES]
            dv_acc[pl.ds(ws, kw), col] += both[kw:, LANES:]
            if gqa:
                sk = jnp.where(lo, sink_ref[2 * b], sink_ref[2 * b + 1])
                dsk_acc[...] += -jnp.exp(sk - lse) * dl

        for lb in range(nlb):
            def step(n, carry, col=slice(lb * LANES, (lb + 1) * LANES)):
                _two_phase([(pl.multiple_of((n * unroll + u) * qb, qb), col) for u in range(unroll)], block, finish)
                return carry

            lax.fori_loop(0, Ls // (qb * unroll), step, 0)

        if gqa:
            @pl.when(s_idx == nseg - 1)
            def _():
                step_rows = min(L, 1024)
                for r0 in range(0, L, step_rows):
                    lanek = lax.broadcasted_iota(jnp.int32, (step_rows, LANES), 1)
                    mine = jnp.logical_xor(lanek < HEAD_DIM, (b // 2) == 1)
                    for acc, ref in ((dk_acc, dk_ref), (dv_acc, dv_ref)):
                        a = acc[r0:r0 + step_rows, :]
                        ref[r0:r0 + step_rows, :] += jnp.where(mine, a + pltpu.roll(a, HEAD_DIM, axis=1), 0.0)
                dsk_ref[...] = dsk_acc[...].reshape(qb // SUBLANES, SUBLANES, LANES).sum(axis=0)
        else:
            dsk_ref[...] = jnp.zeros_like(dsk_ref)

            @pl.when(s_idx == nseg - 1)
            def _():
                dk_ref[...] = dk_acc[...].astype(dk_ref.dtype)
                dv_ref[...] = dv_acc[...].astype(dv_ref.dtype)

    kv_map = (lambda r, b, s: (r, 0, 0)) if gqa else (lambda r, b, s: (r, 0, b))
    seg = pl.BlockSpec((None, Ls, nlb * LANES), lambda r, b, s: (r, s, b))
    full = pl.BlockSpec((None, L, nlb * LANES), kv_map)
    scratch = [pltpu.VMEM((3, 2 * qb, kw), F32)] if tables else []
    if gqa:
        scratch += [pltpu.VMEM((L, LANES), BF16)] * 2 + [pltpu.VMEM((L, LANES), F32)] * 2 + [pltpu.VMEM((qb, LANES), F32)]
    else:
        scratch += [pltpu.VMEM((L, nlb * LANES), F32)] * 2
    kv_dtype = F32 if gqa else BF16
    return pl.pallas_call(
        body, name=name, grid=(NB, Cq // (nlb * LANES), nseg),
        in_specs=[pl.BlockSpec(memory_space=pltpu.SMEM), seg, seg, seg, seg, full, full],
        out_specs=[seg, full, full, pl.BlockSpec((None, None, SUBLANES, LANES), lambda r, b, s: (r, b, 0, 0))],
        out_shape=[_sds((NB, L, Cq), BF16), _sds((NB, L, Ck), kv_dtype), _sds((NB, L, Ck), kv_dtype),
                   _sds((NB, Cq // LANES, SUBLANES, LANES), F32)],
        scratch_shapes=scratch,
        compiler_params=_params(("arbitrary", "arbitrary", "arbitrary")))(sink, q, do, lse, delta, k, v)


def _dilated_fwd(cat, qkv, hw, tile=2048):
    T = cat.shape[0]
    dils = sorted(qkv)
    nbb, na = B_W // LANES, A_Q_W // LANES
    qb, kw = QB, QB + 2 * hw
    rows_merge = 256
    assert T % tile == 0 and all(tile % (d * qb) == 0 and T // d >= kw for d in dils)

    def body(cat_in, *refs):
        qkv_refs = {d: refs[3 * j:3 * j + 3] for j, d in enumerate(dils)}
        cat_ref, lg_refs = refs[3 * len(dils)], refs[3 * len(dils) + 1:4 * len(dils) + 1]
        o_scr, l_scr, bias_scr = refs[4 * len(dils) + 1:]
        i = pl.program_id(1)
        _band_setup(bias_scr, qb, kw, hw)
        lane = lax.broadcasted_iota(jnp.int32, (qb, LANES), 1)
        lo = lane < HEAD_DIM
        for pi, d in enumerate(dils):
            q_ref, k_ref, v_ref = qkv_refs[d]
            L, rows = T // d, tile // d

            def place(r, n, d=d):
                return pl.ds(r + d * n * qb, qb, stride=d) if d > 1 else pl.ds(n * qb, qb)

            def scores(r, n, q_ref=q_ref, k_ref=k_ref, L=L, rows=rows):
                ws, bias = _band_window(bias_scr, i * rows + n * qb, L, qb, kw, hw)
                return ws, _dot_nt(_stack_heads(q_ref[r, n * qb:(n + 1) * qb, :], lo), k_ref[r, pl.ds(ws, kw), :]) + bias

            def finish(r, n, held, v_ref=v_ref, pi=pi, place=place):
                ws, s = held
                m = jnp.max(s, axis=-1, keepdims=True)
                p = jnp.exp(s - m)
                den = jnp.sum(p, axis=-1, keepdims=True)
                o = _dot(p.astype(BF16), v_ref[r, pl.ds(ws, kw), :]) * (1.0 / den)
                o_scr[pi, place(r, n), :] = _unstack_heads(o, lo)
                l_scr[pi, place(r, n), :] = _unstack_heads(m + jnp.log(den), lo)

            blocks = [(r, n) for r in range(d) for n in range(rows // qb)]
            for g0 in range(0, len(blocks), 8):
                _two_phase(blocks[g0:g0 + 8], scores, finish)

        for r0 in range(0, tile, rows_merge):
            rs = slice(r0, r0 + rows_merge)
            ls_ = [l_scr[pi, rs, :] for pi in range(len(dils))]
            m = ls_[0]
            for l in ls_[1:]:
                m = jnp.maximum(m, l)
            es = [jnp.exp(l - m) for l in ls_]
            den, out = es[0], es[0] * o_scr[0, rs, :]
            for pi in range(1, len(dils)):
                den = den + es[pi]
                out = out + es[pi] * o_scr[pi, rs, :]
            cat_ref[rs, :] = (out * (1.0 / den)).astype(BF16)
            l_scr[0, rs, :] = m + jnp.log(den)
        for lg_ref, d in zip(lg_refs, dils):
            for r in range(d):
                lg_ref[r] = l_scr[0, pl.ds(r, tile // d, stride=d), :] if d > 1 else l_scr[0]

    in_specs = [pl.BlockSpec(memory_space=pl.ANY)]
    operands = [cat]
    for d in dils:
        in_specs += [pl.BlockSpec((d, tile // d, LANES), lambda b, i: (0, i, b))] + [pl.BlockSpec((d, T // d, LANES), lambda b, i: (0, 0, b))] * 2
        operands += list(qkv[d])
    return pl.pallas_call(
        body, name="dilated_fwd", grid=(nbb, T // tile), in_specs=in_specs,
        out_specs=[pl.BlockSpec((tile, LANES), lambda b, i: (i, na + b))] + [pl.BlockSpec((d, tile // d, LANES), lambda b, i: (0, i, b)) for d in dils],
        out_shape=[_sds(cat.shape, BF16)] + [_sds((d, T // d, B_W), F32) for d in dils],
        input_output_aliases={0: 0},
        scratch_shapes=[pltpu.VMEM((len(dils), tile, LANES), F32)] * 2 + [pltpu.VMEM((3, 2 * qb, kw), F32)],
        compiler_params=_params(("parallel", "arbitrary")))(*operands)


def _out_proj(x, cat, w_out, tm=512):
    T, D = x.shape

    def body(x_ref, c_ref, w_ref, o_ref):
        o_ref[...] = x_ref[...] + _dot(c_ref[...], w_ref[...])

    row = lambda w: pl.BlockSpec((tm, w), lambda i: (i, 0))
    return pl.pallas_call(
        body, name="out_proj", grid=(T // tm,), in_specs=[row(D), row(cat.shape[1]), pl.BlockSpec(w_out.shape, lambda i: (0, 0))],
        out_specs=row(D), out_shape=_sds((T, D), F32), compiler_params=_params(("parallel",)))(x, cat, w_out)


def _dcat(dx, w_out, cat, tm=512, dep=None):
    T, D = dx.shape
    C = cat.shape[1]
    nba, nbb = A_Q_W // LANES, B_W // LANES
    nt = T // tm

    def body(dx_ref, w_ref, cat_ref, doa_ref, dla_ref, dob1_ref, dlb1_ref, dob4_ref, dlb4_ref, dob16_ref, dlb16_ref, dw_ref, dwb_ref,
             sdo, sdl):
        @pl.when(pl.program_id(0) == 0)
        def _():
            dw_ref[...] = jnp.zeros_like(dw_ref)

        dxb = dx_ref[...].astype(BF16)
        dc = _dot_nt(dxb, w_ref[...])
        dw_ref[...] += _dot_tn(cat_ref[...], dxb)

        @pl.when(pl.program_id(0) == nt - 1)
        def _():
            dwb_ref[...] = dw_ref[...].astype(BF16)

        ri =lax.broadcasted_iota(jnp.int32, (LANES, LANES), 0)
        ci = lax.broadcasted_iota(jnp.int32, (LANES, LANES), 1)
        same_head = ((ri // HEAD_DIM) == (ci // HEAD_DIM)).astype(BF16)
        for cb in range(C // LANES):
            cols = slice(cb * LANES, (cb + 1) * LANES)
            blk = dc[:, cols]
            prod = blk * cat_ref[:, cols].astype(F32)
            hi = prod.astype(BF16)
            lo_ = (prod - hi.astype(F32)).astype(BF16)
            dl = _dot(hi, same_head) + _dot(lo_, same_head)
            if cb < nba:
                doa_ref[:, cols] = blk.astype(BF16)
                dla_ref[:, cols] = dl
            else:
                bcols = slice((cb - nba) * LANES, (cb - nba + 1) * LANES)
                dob1_ref[:, bcols] = blk.astype(BF16)
                dlb1_ref[:, bcols] = dl
                sdo[cb - nba] = blk
                sdl[cb - nba] = dl
        _deinterleave(sdo, dob4_ref, 4, tm, nbb)
        _deinterleave(sdl, dlb4_ref, 4, tm, nbb)
        _deinterleave(sdo, dob16_ref, 16, tm, nbb)
        _deinterleave(sdl, dlb16_ref, 16, tm, nbb)

    row = lambda w: pl.BlockSpec((tm, w), lambda i: (i, 0))
    perm = lambda d: pl.BlockSpec((d, tm // d, B_W), lambda i: (0, i, 0))
    whole = pl.BlockSpec((C, D), lambda i: (0, 0))
    body, dep_spec, dep_arg = _ordered(body, 3, dep)
    outs = pl.pallas_call(
        body, name="dcat", grid=(nt,), in_specs=[row(D), whole, row(C)] + dep_spec,
        out_specs=[row(A_Q_W), row(A_Q_W), row(B_W), row(B_W), perm(4), perm(4), perm(16), perm(16), whole, whole],
        out_shape=[_sds((T, A_Q_W), BF16), _sds((T, A_Q_W), F32), _sds((T, B_W), BF16), _sds((T, B_W), F32),
                   _sds((4, T // 4, B_W), BF16), _sds((4, T // 4, B_W), F32), _sds((16, T // 16, B_W), BF16), _sds((16, T // 16, B_W), F32),
                   _sds((C, D), F32), _sds((C, D), BF16)],
        scratch_shapes=[pltpu.VMEM((nbb, tm, LANES), F32)] * 2, compiler_params=_params(("arbitrary",)))(dx, w_out, cat, *dep_arg)
    return (*outs[:8], (outs[8], outs[9]))


def _mixer_in_bwd(dqa, dka, dva, b1, b4, b16, cos, sin, w_in, x, g, dres, tm=512):
    T, D = x.shape
    nbb = B_W // LANES
    width = A_Q_W + 2 * A_KV_W + 3 * B_W

    def body(dqa_ref, dka_ref, dva_ref, q1, k1, v1, q4, k4, v4, q16, k16, v16, c_ref, s_ref, w_ref, x_ref, g_ref, dr_ref,
             o_ref, dx_ref, dg_ref, scr):
        @pl.when(pl.program_id(0) == 0)
        def _():
            dg_ref[...] = jnp.zeros_like(dg_ref)

        cs, sn = c_ref[...], s_ref[...]
        dh = []

        def unrope(t):
            return t * cs + _swap32(t * sn)

        def project(c0, c1):
            t = _dot(o_ref[:, c0:c1], w_ref[c0:c1, :])
            dh[:] = [t if not dh else dh[0] + t]

        col = 0
        for ref, rope in ((dqa_ref, True), (dka_ref, True), (dva_ref, False)):
            for cb in range(ref.shape[1] // LANES):
                t = ref[:, cb * LANES:(cb + 1) * LANES].astype(F32)
                o_ref[:, col:col + LANES] = (unrope(t) if rope else t).astype(BF16)
                col += LANES
        project(0, col)
        for which, (r1, r4, r16, rope) in enumerate(((q1, q4, q16, True), (k1, k4, k16, True), (v1, v4, v16, False))):
            _interleave(r4, scr.at[0], 4, tm, nbb)
            _interleave(r16, scr.at[1], 16, tm, nbb)
            for cb in range(nbb):
                t = r1[:, cb * LANES:(cb + 1) * LANES].astype(F32) + scr[0, cb] + scr[1, cb]
                o_ref[:, col:col + LANES] = (unrope(t) if rope else t).astype(BF16)
                col += LANES
            project(col - B_W, col)
        dxn, dg = _rms_bwd(dh[0], x_ref[...], g_ref[...])
        dg_ref[...] += dg
        dx_ref[...] = dr_ref[...] + dxn

    row = lambda w: pl.BlockSpec((tm, w), lambda i: (i, 0))
    perm = lambda d: pl.BlockSpec((d, tm // d, B_W), lambda i: (0, i, 0))
    return pl.pallas_call(
        body, name="mixer_in_bwd", grid=(T // tm,),
        in_specs=[row(A_Q_W), row(A_KV_W), row(A_KV_W)] + [row(B_W)] * 3 + [perm(4)] * 3 + [perm(16)] * 3 + [row(LANES), row(LANES)]
        + [_resident(w_in.shape), row(D), pl.BlockSpec((1, D), lambda i: (0, 0)), row(D)],
        out_specs=[row(width), row(D), pl.BlockSpec((SUBLANES, D), lambda i: (0, 0))],
        out_shape=[_sds((T, width), BF16), _sds((T, D), F32), _sds((SUBLANES, D), F32)],
        scratch_shapes=[pltpu.VMEM((2, nbb, tm, LANES), F32)],
        compiler_params=_params(("arbitrary",)))(dqa, dka, dva, *b1, *b4, *b16, cos, sin, w_in, x, g, dres)


def _grad_push_plan(n):
    def plan(refs):
        x, y, c = _mesh_pos()
        return [(refs[k].at[chip], refs[n + k].at[rel], dev) for k in range(n) for rel, (dev, chip) in enumerate(_chip_peers(x, y, c))]
    return plan


def _sum_own(me_arr, g, landed, name):
    ns, R, C = g.shape
    tr = R // 2 if (R // 2) % 16 == 0 else R

    def body(me_ref, g_ref, x_ref, o_ref):
        acc = g_ref[...]
        for rel in range(ns - 1):
            acc = acc + x_ref[rel].astype(F32)
        o_ref[...] = acc

    grid_spec = pltpu.PrefetchScalarGridSpec(
        num_scalar_prefetch=1, grid=(R // tr,),
        in_specs=[pl.BlockSpec((None, tr, C), lambda t, me: (me[0], t, 0)), pl.BlockSpec((ns - 1, tr, C), lambda t, me: (0, t, 0))],
        out_specs=pl.BlockSpec((tr, C), lambda t, me: (t, 0)))
    return pl.pallas_call(body, name=name, grid_spec=grid_spec, out_shape=_sds((R, C), F32),
                          compiler_params=_params(("parallel",)))(me_arr, g, landed)


def _swap_plan(n):
    def plan(refs):
        x, y, c = _mesh_pos()
        return [(refs[k], refs[n + k], (x, y, 1 - c)) for k in range(n)]
    return plan


def _allreduce_small(v, dep):
    rows, W = v.shape

    def body(v_ref, o_ref, buf, send, recv):
        x, y, c = _mesh_pos()
        me = 4 * x + 2 * y + c
        cps = []
        for m in range(1, N_DEV):
            dev = (x ^ (m >> 2), y ^ ((m >> 1) & 1), c ^ (m & 1))
            cp = pltpu.make_async_remote_copy(src_ref=v_ref, dst_ref=buf.at[me], send_sem=send.at[m - 1], recv_sem=recv.at[m - 1],
                                              device_id=dev, device_id_type=MESH)
            cp.start()
            cps.append(cp)
        for m in range(1, N_DEV):
            pltpu.make_async_remote_copy(src_ref=v_ref, dst_ref=buf.at[me ^ m], send_sem=send.at[m - 1], recv_sem=recv.at[m - 1],
                                         device_id=(x, y, c), device_id_type=MESH).wait_recv()
        for cp in cps:
            cp.wait_send()
        buf[me] = v_ref[...]
        acc = buf[0]
        for i in range(1, N_DEV):
            acc = acc + buf[i]
        o_ref[...] = acc

    body, dep_spec, dep_arg = _ordered(body, 1, dep)
    return pl.pallas_call(
        body, name="allreduce_small", out_shape=_sds((rows, W), F32), in_specs=[pl.BlockSpec(memory_space=pltpu.VMEM)] + dep_spec,
        scratch_shapes=[pltpu.VMEM((N_DEV, rows, W), F32), pltpu.SemaphoreType.DMA((N_DEV - 1,)), pltpu.SemaphoreType.DMA((N_DEV - 1,))],
        compiler_params=_params())(v, *dep_arg)


def _adamw_math(w, g, m, v):
    c1 = 1.0 / (1.0 - ADAM_B1 ** ADAM_STEP)
    c2 = 1.0 / (1.0 - ADAM_B2 ** ADAM_STEP)
    nm = ADAM_B1 * m + (1.0 - ADAM_B1) * g
    nv = ADAM_B2 * v + (1.0 - ADAM_B2) * (g * g)
    return -ADAM_LR * ((nm * c1) / (jnp.sqrt(nv * c2) + ADAM_EPS) + ADAM_WD * w), nm, nv


def _adamw_small(rows, params):
    n = len(params)
    n_sink = params[-1][0].shape[1]

    def body(rows_ref, *refs):
        ins, outs = refs[:3 * n], refs[3 * n:]
        for j in range(n):
            g = rows_ref[j:j + 1, 0:n_sink] if j == n - 1 else rows_ref[j:j + 1, :]
            d, nm, nv = _adamw_math(ins[3 * j][...], g, ins[3 * j + 1][...], ins[3 * j + 2][...])
            for ref, val in zip(outs[4 * j:4 * j + 4], (g, d, nm, nv)):
                ref[...] = val
        outs[-1][...] = rows_ref[n - 1:n, n_sink:n_sink + 1]

    flat = [a for p in params for a in p]
    outs = pl.pallas_call(body, name="adamw_small", out_shape=[_sds(p[0].shape, F32) for p in params for _ in range(4)] + [_sds((1, 1), F32)],
                          compiler_params=_params())(rows, *flat)
    return [outs[4 * j:4 * j + 4] for j in range(n)], outs[-1]


def _adamw(w, gp, gq, m, v, name):
    R, C = w.shape
    tr = R // 2 if (R // 2) % SUBLANES == 0 else R

    def body(w_ref, gp_ref, gq_ref, m_ref, v_ref, g_ref, d_ref, nm_ref, nv_ref):
        gv = gp_ref[...] + gq_ref[...]
        g_ref[...] = gv
        d_ref[...], nm_ref[...], nv_ref[...] = _adamw_math(w_ref[...], gv, m_ref[...], v_ref[...])

    blk = pl.BlockSpec((tr, C), lambda t: (t, 0))
    return pl.pallas_call(body, name=name, grid=(R // tr,), in_specs=[blk] * 5, out_specs=[blk] * 4,
                          out_shape=[_sds((R, C), F32)] * 4, compiler_params=_params(("parallel",)))(w, gp, gq, m, v)


def _rope(positions, after):
    inv_freq = 1.0 / (ROPE_THETA ** (jnp.arange(0, HEAD_DIM, 2, dtype=F32) / HEAD_DIM))
    inv_freq = jnp.tile(inv_freq, LANES // (HEAD_DIM // 2)).reshape(1, LANES) + after[0, 0]
    return _rope_tables(positions.reshape(-1, 1), inv_freq)


def _local_step(x, rope, target, norms, a_sink, comm):
    T, D = x.shape
    g1, gm, g2, gf = norms
    cos, sin = rope
    no_sink = jnp.zeros((2 * (B_W // LANES),), F32)
    W = {k: comm.weight(k, x) for k in ("wg1", "wu1")}

    h1, gate1, up1, act1 = _ffn_up(x, g1, W["wg1"], W["wu1"], "ffn1_up", dep=comm.dep())
    W["wd1"] = comm.weight("wd1", act1)
    x1 = _ffn_down(x, act1, W["wd1"], "ffn1_down")
    W["w_in"] = comm.weight("w_in", x1)
    (h2, aq, ak, av, bq1, bk1, bv1, bq4, bk4, bv4, bq16, bk16, bv16) = _proj_rope(x1, gm, W["w_in"], cos, sin)
    cat, a_lse = _attn_fwd(aq[None], ak[None], av[None], a_sink, A_HALF_WINDOW, True, BF16, "attn_a_fwd", qb=2 * QB, blocks_per_step=4,
                           out_cols=A_Q_W + B_W)
    bqs = {1: (bq1[None], bk1[None], bv1[None]), 4: (bq4, bk4, bv4), 16: (bq16, bk16, bv16)}
    (b_hw,) = {w // (2 * d) for w, d in B_PATTERNS}
    cat, lg1, lg4, lg16 = _dilated_fwd(cat[0], bqs, b_hw)
    lg1 = lg1[0]
    W["w_out"] = comm.weight("w_out", cat)
    x2 = _out_proj(x1, cat, W["w_out"])
    for k in ("wg2", "wu2", "wd2"):
        W[k] = comm.weight(k, x2)
    dx3, h3, gate2, up2, act2, dgf, loss8 = _ffn_fwd(x2, g2, W["wg2"], W["wu2"], W["wd2"], "ffn2_fwd", loss=(gf, target))

    dx2, dff2, dgate2, dup2, dg2 = _ffn_dx(dx3, x2, g2, gate2, up2, W["wg2"], W["wu2"], W["wd2"], "ffn2_dx")
    fb = gate2.shape[1] // 2
    dwg2 = _tn(dgate2, h3, fb, "ffn2_dw_gate")
    dwu2 = _tn(dup2, h3, fb, "ffn2_dw_up")
    dwd2 = _tn(act2, dff2, fb, "ffn2_dw_down")
    comm.ready(dict(wg2=dwg2, wu2=dwu2, wd2=dwd2), dwd2[0])

    doa, dla, dob1, dlb1, dob4, dlb4, dob16, dlb16, dw_out = _dcat(dx2, W["w_out"], cat, dep=comm.dep())
    dqa, dka, dva, dsk = _attn_bwd(aq[None], ak[None], av[None], doa[None], a_lse, dla[None], a_sink, A_HALF_WINDOW, True, "attn_a_bwd")
    bwd_in = {1: (dob1[None], lg1[None], dlb1[None]), 4: (dob4, lg4, dlb4), 16: (dob16, lg16, dlb16)}
    bg = {}
    for w, d in B_PATTERNS:
        q_, k_, v_ = bqs[d]
        do_, l_, dl_ = bwd_in[d]
        bg[d] = _attn_bwd(q_, k_, v_, do_, l_, dl_, no_sink, w // (2 * d), False, f"attn_b{d}_bwd")[:3]
    dproj, dx1, dgm = _mixer_in_bwd(dqa[0], dka[0], dva[0], [t[0] for t in bg[1]], bg[4], bg[16], cos, sin, W["w_in"], x1, gm, dx2)
    dw_in = _tn(dproj, h2, dproj.shape[1] // 2, "w_in_dw")
    comm.ready(dict(w_in=dw_in, w_out=dw_out), dw_in[0])

    dx0, dff1, dgate1, dup1, dg1 = _ffn_dx(dx1, x, g1, gate1, up1, W["wg1"], W["wu1"], W["wd1"], "ffn1_dx", dep=comm.dep())
    comm.settle(2, dx0)
    dwd1 = _tn(act1, dff1, fb, "ffn1_dw_down", dep=comm.dep())
    comm.ready(dict(wd1=dwd1), dwd1[0])
    dwg1 = _tn(dgate1, h1, fb, "ffn1_dw_gate", dep=comm.dep())
    comm.ready(dict(wg1=dwg1), dwg1[0])
    dwu1 = _tn(dup1, h1, fb, "ffn1_dw_up", dep=comm.dep())
    comm.ready(dict(wu1=dwu1), dwu1[0])

    dsink = dsk[0, :, :, ::HEAD_DIM].sum(axis=1).reshape(-1)
    small = dict(g1=dg1.sum(axis=0), gm=dgm.sum(axis=0), g2=dg2.sum(axis=0), gf=dgf.sum(axis=0), sink=dsink, loss=loss8[0, 0])
    return dx0, small


BIG = ("wg1", "wu1", "wd1", "w_in", "w_out", "wg2", "wu2", "wd2")
GATHER_GROUPS = (("wd1",), ("w_in",), ("w_out",), ("wg2", "wu2", "wd2"))


class _Comm:
    def __init__(self, shards, meanwhile):
        x, y, c = _mesh_pos()
        self.me = (2 * x + y).astype(jnp.int32).reshape(1)
        self.shards = shards
        self.token = None
        self.waiting = {}
        self.groups = []
        self.swaps = []
        first = ("wg1", "wu1")
        fulls ={k: _cast_place(self.me, shards[k], f"cast_{k}") for k in first}
        plan = _neighbour_plan([fulls[k].shape for k in first])
        send, recv, bufs, tok = _push_start("gather_first_start", [fulls[k] for k in first], 2 * len(first), plan, self.me)
        self.side = meanwhile(tok)
        fulls.update({k: _cast_place(self.me, shards[k], f"cast_{k}") for k in BIG if k not in first})
        bufs = _push_wait("gather_first_wait", send, recv, bufs, plan, [fulls[k] for k in BIG if k not in first] + list(self.side))
        self.full = dict(zip(first, _gather_forward(bufs)))
        dep = self.full[first[-1]]
        for gi, names in enumerate(GATHER_GROUPS):
            plan = _gather_plan(len(names))
            send, recv, bufs, self.token = _push_start(f"gather_start_{gi}", [fulls[k] for k in names], 3 * len(names), plan, dep)
            dep = self.token
            for k in names:
                self.waiting[k] = (gi, names, send, recv, bufs, plan)

    def dep(self):
        return self.token

    def weight(self, name, after):
        if name in self.waiting:
            gi, names, send, recv, bufs, plan = self.waiting[name]
            for k, buf in zip(names, _push_wait(f"gather_wait_{gi}", send, recv, bufs, plan, after)):
                self.full[k] = buf
                del self.waiting[k]
        full = self.full[name]
        return full.reshape(N_CHIPS * full.shape[1], full.shape[2])

    def ready(self, grads, after):
        names = list(grads)
        f32s, b16s = [], []
        for k in names:
            gf, gb = grads[k]
            f32s.append(gf.reshape((N_CHIPS,) + self.shards[k].shape))
            b16s.append(gb.reshape((N_CHIPS,) + self.shards[k].shape))
        n = len(names)
        lands = [lax.empty((N_CHIPS - 1,) + self.shards[k].shape, BF16) for k in names]
        plan = _grad_push_plan(n)
        send, recv, bufs, self.token = _push_start(f"grad_start_{names[0]}", b16s + lands, 3 * n, plan, after)
        self.groups.append((names, f32s, send, recv, bufs, plan))

    def settle(self, count, after):
        batch, self.groups = self.groups[:count], self.groups[count:]
        names_b, mine_b = [], []
        for names, f32s, send, recv, bufs, plan in batch:
            n = len(names)
            bufs = _push_wait(f"grad_wait_{names[0]}", send, recv, bufs, plan, mine_b[-1] if mine_b else after)
            mine_b += [_sum_own(self.me, f32s[i], bufs[n + i], f"sum_{k}") for i, k in enumerate(names)]
            names_b += names
        lands = [lax.empty(p.shape, F32) for p in mine_b]
        n = len(names_b)
        send2, recv2, both, self.token = _push_start(f"swap_start_{names_b[0]}", mine_b + lands, n, _swap_plan(n), after)
        self.swaps.append((names_b, send2, recv2, both))

    def partials(self, after):
        names_b, send2, recv2, both = self.swaps.pop(0)
        n = len(names_b)
        both = _push_wait(f"swap_wait_{names_b[0]}", send2, recv2, both, _swap_plan(n), after)
        return {k: (both[i], both[n + i]) for i, k in enumerate(names_b)}


def kernel(x, positions, norm_ffn1, w_gate1, w_up1, w_down1, norm_mix, w_in, a_sink, w_out, norm_ffn2, w_gate2, w_up2, w_down2, norm_final, loss_target, m_norm_ffn1, m_w_gate1, m_w_up1, m_w_down1, m_norm_mix, m_w_in, m_a_sink, m_w_out, m_norm_ffn2, m_w_gate2, m_w_up2, m_w_down2, m_norm_final, v_norm_ffn1, v_w_gate1, v_w_up1, v_w_down1, v_norm_mix, v_w_in, v_a_sink, v_w_out, v_norm_ffn2, v_w_gate2, v_w_up2, v_w_down2, v_norm_final):
    T, D = x.shape[1], x.shape[2]
    flip = ("wg1", "wu1", "w_in", "wg2", "wu2")

    def rows(k, a):
        return a[0].T if k in flip else a[0]

    given = dict(wg1=(w_gate1, m_w_gate1, v_w_gate1), wu1=(w_up1, m_w_up1, v_w_up1), wd1=(w_down1, m_w_down1, v_w_down1),
                 w_in=(w_in, m_w_in, v_w_in), w_out=(w_out, m_w_out, v_w_out), wg2=(w_gate2, m_w_gate2, v_w_gate2),
                 wu2=(w_up2, m_w_up2, v_w_up2), wd2=(w_down2, m_w_down2, v_w_down2))
    shards = {k: rows(k, given[k][0]) for k in BIG}

    comm = _Comm(shards, lambda tok: _rope(positions[0], tok))

    norms = (norm_ffn1, norm_mix, norm_ffn2, norm_final.reshape(1, D))
    grad_x, small = _local_step(x[0], comm.side, loss_target[0], norms, a_sink[0], comm)

    upd = {}

    def update(partial):
        for k in partial:
            outs = _adamw(shards[k], partial[k][0], partial[k][1], rows(k, given[k][1]), rows(k, given[k][2]), f"adamw_{k}")
            upd[k] = tuple((a.T if k in flip else a)[None] for a in outs)
        return outs[0]

    last = update(comm.partials(comm.dep()))
    comm.settle(2, last)

    def pad_row(a):
        a = a.reshape(-1)
        return jnp.pad(a, (0, D - a.shape[0]))

    row4 = pad_row(jnp.concatenate([small["sink"], small["loss"].reshape(1)]))
    vec = jnp.stack([small["g1"], small["gm"], small["g2"], small["gf"], row4] + [jnp.zeros((D,), F32)] * 3, axis=0)
    red = _allreduce_small(vec, comm.dep())
    comm.settle(1, red)
    last = update(comm.partials(comm.dep()))
    update(comm.partials(last))
    as_row = lambda a: a.reshape(1, -1)
    sm, loss = _adamw_small(red, [tuple(as_row(a) for a in p) for p in (
        (norm_ffn1, m_norm_ffn1, v_norm_ffn1), (norm_mix, m_norm_mix, v_norm_mix), (norm_ffn2, m_norm_ffn2, v_norm_ffn2),
        (norm_final, m_norm_final, v_norm_final), (a_sink, m_a_sink, v_a_sink))])
    sm[3] = [a.reshape(D) for a in sm[3]]

    def ordered(i):
        return [sm[0][i], upd["wg1"][i], upd["wu1"][i], upd["wd1"][i], sm[1][i], upd["w_in"][i], sm[4][i], upd["w_out"][i], sm[2][i],
                upd["wg2"][i], upd["wu2"][i], upd["wd2"][i], sm[3][i]]

    return (loss.reshape(()), grad_x[None], *ordered(0), *ordered(1), *ordered(2), *ordered(3))
```

```python
import jax
import jax.numpy as jnp
from jax import lax
from jax.experimental import pallas as pl
from jax.experimental.pallas import tpu as pltpu

F32 = jnp.float32
BF16 = jnp.bfloat16

HEAD_DIM = 64
LANES = 128
SUBLANES = 8
A_Q_W, A_KV_W, B_W = 512, 128, 512
A_HALF_WINDOW = 128
B_PATTERNS = ((128, 1), (512, 4), (2048, 16))
ROPE_THETA = 10000.0
NORM_EPS = 1e-6
FFN_RES_WEIGHT = 0.5
ADAM_LR, ADAM_B1, ADAM_B2, ADAM_EPS, ADAM_WD, ADAM_STEP = 0.001, 0.9, 0.999, 1e-08, 0.01, 10
N_CHIPS = 4
N_DEV = 8
QB = 128
SHORT_SEQ = 512
NEG = -1e30
VMEM_LIMIT = 56 * 1024 * 1024
MESH = pl.DeviceIdType.MESH
ANY = pl.BlockSpec(memory_space=pl.ANY)


def _params(sem=None):
    return pltpu.CompilerParams(dimension_semantics=sem, vmem_limit_bytes=VMEM_LIMIT)


def _sds(shape, dtype):
    return jax.ShapeDtypeStruct(tuple(shape), dtype)


def _dot(a, b):
    return jnp.dot(a, b, preferred_element_type=F32)


def _dot_nt(a, b):
    return lax.dot_general(a, b, (((1,), (1,)), ((), ())), preferred_element_type=F32)


def _dot_tn(a, b):
    return lax.dot_general(a, b, (((0,), (0,)), ((), ())), preferred_element_type=F32)


def _rms_stats(x):
    r = lax.rsqrt(jnp.mean(x * x, axis=-1, keepdims=True) + NORM_EPS)
    return x * r, r


def _rms_bwd(dh, x, g):
    xhat, r = _rms_stats(x)
    dxn = dh * g
    dx = r * (dxn - xhat * jnp.mean(dxn * xhat, axis=-1, keepdims=True))
    tm, d = x.shape
    dg = (dh * xhat).reshape(tm // SUBLANES, SUBLANES, d).sum(axis=0)
    return dx, dg


def _sigmoid(x):
    return 1.0 / (1.0 + jnp.exp(-x))


def _swap32(t):
    n = t.shape[-1]
    lane = lax.broadcasted_iota(jnp.int32, t.shape, t.ndim - 1)
    return jnp.where((lane % HEAD_DIM) < HEAD_DIM // 2, pltpu.roll(t, n - HEAD_DIM // 2, axis=t.ndim - 1),
                     pltpu.roll(t, HEAD_DIM // 2, axis=t.ndim - 1))


def _ordered(body, n_in, dep):
    if dep is None:
        return body, [], []

    def ordered(*refs):
        body(*refs[:n_in], *refs[n_in + 1:])

    return ordered, [ANY], [dep]


def _cast_place(me_arr, w, name):
    R, C = w.shape
    tr = R // 2 if (R // 2) % 16 == 0 else R

    def body(me_ref, w_ref, o_ref):
        o_ref[...] = w_ref[...].astype(BF16)

    grid_spec = pltpu.PrefetchScalarGridSpec(
        num_scalar_prefetch=1, grid=(R // tr,), in_specs=[pl.BlockSpec((tr, C), lambda t, me: (t, 0))],
        out_specs=pl.BlockSpec((None, tr, C), lambda t, me: (me[0], t, 0)))
    return pl.pallas_call(body, name=name, grid_spec=grid_spec, out_shape=_sds((N_CHIPS, R, C), BF16),
                          compiler_params=_params(("parallel",)))(me_arr, w)


HBM = pl.BlockSpec(memory_space=pltpu.HBM)
SEM = pl.BlockSpec(memory_space=pltpu.SEMAPHORE)


def _push_start(name, bufs, ncopies, plan, after):
    nb = len(bufs)

    def body(*refs):
        send, recv, token = refs[nb + 1], refs[nb + 2], refs[-1]
        for i, (src, dst, dev) in enumerate(plan(refs[:nb])):
            pltpu.make_async_remote_copy(src_ref=src, dst_ref=dst, send_sem=send.at[i], recv_sem=recv.at[i],
                                         device_id=dev, device_id_type=MESH).start()
        token[...] = jnp.zeros_like(token)

    outs = pl.pallas_call(
        body, name=name,
        out_shape=(pltpu.SemaphoreType.DMA((ncopies,)), pltpu.SemaphoreType.DMA((ncopies,)), *[pltpu.HBM(b.shape, b.dtype) for b in bufs],
                   _sds((SUBLANES, LANES), F32)),
        in_specs=[HBM] * nb + [ANY], out_specs=(SEM, SEM, *([HBM] * nb), pl.BlockSpec(memory_space=pltpu.VMEM)),
        input_output_aliases={i: 2 + i for i in range(nb)},
        compiler_params=pltpu.CompilerParams(has_side_effects=pltpu.SideEffectType.DATAFLOW_SIDE_EFFECTING),
    )(*[pltpu.with_memory_space_constraint(b, pltpu.HBM) for b in bufs], after)
    return outs[0], outs[1], list(outs[2:2 + nb]), outs[-1]


def _push_wait(name, send, recv, bufs, plan, after):
    nb = len(bufs)

    def body(*refs):
        send_ref, recv_ref = refs[nb], refs[nb + 1]
        for i, (src, dst, dev) in enumerate(plan(refs[:nb])):
            cp = pltpu.make_async_remote_copy(src_ref=src, dst_ref=dst, send_sem=send_ref.at[i], recv_sem=recv_ref.at[i],
                                              device_id=dev, device_id_type=MESH)
            cp.wait_send()
            cp.wait_recv()

    afters = list(after) if isinstance(after, (list, tuple)) else [after]
    outs = pl.pallas_call(
        body, name=name, out_shape=tuple(pltpu.HBM(b.shape, b.dtype) for b in bufs),
        in_specs=[HBM] * nb + [SEM, SEM] + [ANY] * len(afters), out_specs=tuple([HBM] * nb),
        input_output_aliases={i: i for i in range(nb)},
        compiler_params=pltpu.CompilerParams(has_side_effects=pltpu.SideEffectType.DATAFLOW_SIDE_EFFECTING),
    )(*bufs, send, recv, *afters)
    return list(outs)


def _mesh_pos():
    return lax.axis_index("x"), lax.axis_index("y"), lax.axis_index("c")


def _chip_peers(x, y, c):
    return [((1 - x, y, c), 2 * (1 - x) + y), ((x, 1 - y, c), 2 * x + (1 - y)), ((1 - x, 1 - y, c), 2 * (1 - x) + (1 - y))]


def _gather_plan(n):
    def plan(refs):
        x, y, c = _mesh_pos()
        me = 2 * x + y
        return [(refs[k].at[me], refs[k].at[me], dev) for k in range(n) for dev, _ in _chip_peers(x, y, c)]
    return plan


def _rows_of(shape, who, quarter=None):
    r2 = shape[1] // 2
    if quarter is None:
        return pl.ds(pl.multiple_of(who * r2, 16), r2)
    return pl.ds(pl.multiple_of(who * r2 + quarter * (r2 // 2), 16), r2 // 2)


def _neighbour_plan(shapes):
    def plan(refs):
        x, y, c = _mesh_pos()
        me = 2 * x + y
        return [(refs[k].at[me, _rows_of(shp, c), :], refs[k].at[me, _rows_of(shp, c), :], dev)
                for k, shp in enumerate(shapes) for dev in ((1 - x, y, c), (x, 1 - y, c))]
    return plan


def _gather_forward(fulls):
    n = len(fulls)

    def body(*refs):
        ins, outs = refs[:n], refs[n:2 * n]
        ici_send, ici_recv, d2d_send, d2d_recv = refs[2 * n:]
        x, y, c = _mesh_pos()
        cx, cy, cd = 2 * (1 - x) + y, 2 * x + (1 - y), 2 * (1 - x) + (1 - y)
        sibling, x_nbr, y_nbr = (x, y, 1 - c), (1 - x, y, c), (x, 1 - y, c)
        started = []

        def push(src, dst, send, recv, dev):
            cp = pltpu.make_async_remote_copy(src_ref=src, dst_ref=dst, send_sem=send, recv_sem=recv, device_id=dev, device_id_type=MESH)
            cp.start()
            started.append(cp)

        def arrived(blk, send, recv):
            pltpu.make_async_remote_copy(src_ref=blk, dst_ref=blk, send_sem=send, recv_sem=recv, device_id=sibling,
                                         device_id_type=MESH).wait_recv()

        for k in range(n):
            shp = fulls[k].shape
            for j, chip in enumerate((cx, cy)):
                push(ins[k].at[chip, _rows_of(shp, c), :], outs[k].at[chip, _rows_of(shp, c), :],
                     d2d_send.at[3 * k + j], d2d_recv.at[3 * k + j], sibling)
            push(ins[k].at[cx, _rows_of(shp, c, 0), :], outs[k].at[cx, _rows_of(shp, c, 0), :], ici_send.at[2 * k], ici_recv.at[2 * k], y_nbr)
            push(ins[k].at[cy, _rows_of(shp, c, 1), :], outs[k].at[cy, _rows_of(shp, c, 1), :], ici_send.at[2 * k + 1], ici_recv.at[2 * k + 1],
                 x_nbr)
        for k in range(n):
            shp = fulls[k].shape
            for q in (0, 1):
                arrived(outs[k].at[cd, _rows_of(shp, c, q), :], ici_send.at[2 * k + q], ici_recv.at[2 * k + q])
            blk = outs[k].at[cd, _rows_of(shp, c), :]
            push(blk, blk, d2d_send.at[3 * k + 2], d2d_recv.at[3 * k + 2], sibling)
        for k in range(n):
            for j, chip in enumerate((cx, cy, cd)):
                arrived(outs[k].at[chip, _rows_of(fulls[k].shape, 1 - c), :], d2d_send.at[3 * k + j], d2d_recv.at[3 * k + j])
        for cp in started:
            cp.wait_send()

    return pl.pallas_call(
        body, name="gather_forward", out_shape=[_sds(f.shape, BF16) for f in fulls],
        in_specs=[ANY] * n, out_specs=[ANY] * n, input_output_aliases={k: k for k in range(n)},
        scratch_shapes=[pltpu.SemaphoreType.DMA((n * 2,))] * 2 + [pltpu.SemaphoreType.DMA((n * 3,))] * 2,
        compiler_params=_params())(*fulls)


def _resident(shape):
    return pl.BlockSpec(shape, lambda i: (0,) * len(shape), pipeline_mode=pl.Buffered(1))


FFN_FWD_CHUNK = 256
FFN_DX_CHUNK = 512


def _chunks(n, step):
    return [(c0, min(step, n - c0)) for c0 in range(0, n, step)]


def _two_phase(chunks, first, second):
    held = {}
    for ci, ch in enumerate(chunks):
        held[ci] = first(*ch)
        if ci >= 1:
            second(*chunks[ci - 1], held.pop(ci - 1))
    last = len(chunks) - 1
    second(*chunks[last], held.pop(last))


def _loss_and_grad(x, g, target):
    D = x.shape[1]
    xhat, _ = _rms_stats(x)
    err = xhat * g - target
    loss = 0.5 * jnp.sum(jnp.sum(err * err, axis=-1, keepdims=True) * (1.0 / D), axis=0, keepdims=True)
    dx, dg = _rms_bwd(err * (1.0 / D), x, g)
    return dx, dg, loss


def _ffn_loss(x, g, wgt, wut, wd, gf, target, name, tm=512):
    T, D = x.shape
    F = wd.shape[0]

    def body(x_ref, g_ref, wg_ref, wu_ref, wd_ref, gf_ref, t_ref, dy_ref, h_ref, gate_ref, up_ref, act_ref, dgf_ref, loss_ref):
        @pl.when(pl.program_id(0) == 0)
        def _():
            dgf_ref[...] = jnp.zeros_like(dgf_ref)
            loss_ref[...] = jnp.zeros_like(loss_ref)

        xv = x_ref[...]
        xhat, _ = _rms_stats(xv)
        h = (xhat * g_ref[...]).astype(BF16)
        h_ref[...] = h
        acc = []

        def first(c0, cw):
            return _dot_nt(h, wg_ref[c0:c0 + cw, :]), _dot_nt(h, wu_ref[c0:c0 + cw, :])

        def second(c0, cw, gate_up):
            gate, up = gate_up
            act = ((gate * _sigmoid(gate)) * up).astype(BF16)
            gate_ref[:, c0:c0 + cw] = gate.astype(BF16)
            up_ref[:, c0:c0 + cw] = up.astype(BF16)
            act_ref[:, c0:c0 + cw] = act
            d = _dot(act, wd_ref[c0:c0 + cw, :])
            acc[:] = [d if not acc else acc[0] + d]

        _two_phase(_chunks(F, FFN_FWD_CHUNK), first, second)
        dy_ref[...], dgf, part = _loss_and_grad(xv + FFN_RES_WEIGHT * acc[0], gf_ref[...], t_ref[...])
        dgf_ref[...] += dgf
        loss_ref[...] += part

    row = pl.BlockSpec((tm, D), lambda i: (i, 0))
    gain = pl.BlockSpec((1, D), lambda i: (0, 0))
    saved = pl.BlockSpec((tm, F), lambda i: (i, 0))
    return pl.pallas_call(
        body, name=name, grid=(T // tm,),
        in_specs=[row, gain, _resident(wgt.shape), _resident(wut.shape), _resident(wd.shape), gain, row],
        out_specs=[row, row, saved, saved, saved, pl.BlockSpec((SUBLANES, D), lambda i: (0, 0)), pl.BlockSpec((SUBLANES, LANES), lambda i: (0, 0))],
        out_shape=[_sds((T, D), F32), _sds((T, D), BF16), _sds((T, F), BF16), _sds((T, F), BF16), _sds((T, F), BF16),
                   _sds((SUBLANES, D), F32), _sds((SUBLANES, LANES), F32)],
        compiler_params=_params(("arbitrary",)))(x, g, wgt, wut, wd, gf, target)


def _ffn_up(x, g, wgt, wut, name, tm=512, dep=None):
    T, D = x.shape
    F = wgt.shape[0]

    def body(x_ref, g_ref, wg_ref, wu_ref, h_ref, gate_ref, up_ref, act_ref):
        xhat, _ = _rms_stats(x_ref[...])
        h = (xhat * g_ref[...]).astype(BF16)
        h_ref[...] = h

        def first(c0, cw):
            return _dot_nt(h, wg_ref[c0:c0 + cw, :]), _dot_nt(h, wu_ref[c0:c0 + cw, :])

        def second(c0, cw, gate_up):
            gate, up = gate_up
            gate_ref[:, c0:c0 + cw] = gate.astype(BF16)
            up_ref[:, c0:c0 + cw] = up.astype(BF16)
            act_ref[:, c0:c0 + cw] = ((gate * _sigmoid(gate)) * up).astype(BF16)

        _two_phase(_chunks(F, FFN_FWD_CHUNK), first, second)

    row = pl.BlockSpec((tm, D), lambda i: (i, 0))
    saved = pl.BlockSpec((tm, F), lambda i: (i, 0))
    body, dep_spec, dep_arg = _ordered(body, 4, dep)
    return pl.pallas_call(
        body, name=name, grid=(T // tm,),
        in_specs=[row, pl.BlockSpec((1, D), lambda i: (0, 0)), _resident(wgt.shape), _resident(wut.shape)] + dep_spec,
        out_specs=[row, saved, saved, saved],
        out_shape=[_sds((T, D), BF16), _sds((T, F), BF16), _sds((T, F), BF16), _sds((T, F), BF16)],
        compiler_params=_params(("parallel",)))(x, g, wgt, wut, *dep_arg)


def _ffn_down(x, act, wd, name, tm=512):
    T, D = x.shape
    F = wd.shape[0]

    def body(x_ref, a_ref, wd_ref, xo_ref):
        xo_ref[...] = x_ref[...] + FFN_RES_WEIGHT * _dot(a_ref[...], wd_ref[...])

    row = pl.BlockSpec((tm, D), lambda i: (i, 0))
    return pl.pallas_call(
        body, name=name, grid=(T // tm,), in_specs=[row, pl.BlockSpec((tm, F), lambda i: (i, 0)), _resident(wd.shape)],
        out_specs=row, out_shape=_sds((T, D), F32), compiler_params=_params(("parallel",)))(x, act, wd)


def _ffn_dx(dxo, x, g, gate_s, up_s, wgt, wut, wd, name, tm=256, dep=None):
    T, D = x.shape
    F = wd.shape[0]

    def body(dxo_ref, x_ref, g_ref, gate_ref, up_ref, wg_ref, wu_ref, wd_ref, dx_ref, dff_ref, dgate_ref, dup_ref, dg_ref):
        @pl.when(pl.program_id(0) == 0)
        def _():
            dg_ref[...] = jnp.zeros_like(dg_ref)

        d = (FFN_RES_WEIGHT * dxo_ref[...]).astype(BF16)
        dff_ref[...] = d
        dh = []

        def first(c0, cw):
            return _dot_nt(d, wd_ref[c0:c0 + cw, :])

        def second(c0, cw, da):
            gate = gate_ref[:, c0:c0 + cw].astype(F32)
            up = up_ref[:, c0:c0 + cw].astype(F32)
            s = _sigmoid(gate)
            silu = gate * s
            dup = (da * silu).astype(BF16)
            dgate = (da * up * (s * (1.0 + gate * (1.0 - s)))).astype(BF16)
            dgate_ref[:, c0:c0 + cw] = dgate
            dup_ref[:, c0:c0 + cw] = dup
            t = _dot(dgate, wg_ref[c0:c0 + cw, :]) + _dot(dup, wu_ref[c0:c0 + cw, :])
            dh[:] = [t if not dh else dh[0] + t]

        _two_phase(_chunks(F, FFN_DX_CHUNK), first, second)
        dxn, dg = _rms_bwd(dh[0], x_ref[...], g_ref[...])
        dg_ref[...] += dg
        dx_ref[...] = dxo_ref[...] + dxn

    row = pl.BlockSpec((tm, D), lambda i: (i, 0))
    saved = pl.BlockSpec((tm, F), lambda i: (i, 0))
    body, dep_spec, dep_arg = _ordered(body, 8, dep)
    return pl.pallas_call(
        body, name=name, grid=(T // tm,),
        in_specs=[row, row, pl.BlockSpec((1, D), lambda i: (0, 0)), saved, saved, _resident(wgt.shape), _resident(wut.shape),
                  _resident(wd.shape)] + dep_spec,
        out_specs=[row, row, saved, saved, pl.BlockSpec((SUBLANES, D), lambda i: (0, 0))],
        out_shape=[_sds((T, D), F32), _sds((T, D), BF16), _sds((T, F), BF16), _sds((T, F), BF16), _sds((SUBLANES, D), F32)],
        compiler_params=_params(("arbitrary",)))(dxo, x, g, gate_s, up_s, wgt, wut, wd, *dep_arg)


def _tn(a, b, mb, name, tk=2048, dep=None):
    T, M = a.shape
    N = b.shape[1]
    nt = T // tk

    def body(a_ref, b_ref, o_ref, ob_ref):
        @pl.when(pl.program_id(1) == 0)
        def _():
            o_ref[...] = jnp.zeros_like(o_ref)

        o_ref[...] += _dot_tn(a_ref[...].astype(BF16), b_ref[...].astype(BF16))

        @pl.when(pl.program_id(1) == nt - 1)
        def _():
            ob_ref[...] = o_ref[...].astype(BF16)

    o_spec = pl.BlockSpec((mb, N), lambda g, t: (g, 0))
    body, dep_spec, dep_arg = _ordered(body, 2, dep)
    return pl.pallas_call(
        body, name=name, grid=(M // mb, nt),
        in_specs=[pl.BlockSpec((tk, mb), lambda g, t: (t, g)), pl.BlockSpec((tk, N), lambda g, t: (t, 0))] + dep_spec,
        out_specs=[o_spec, o_spec], out_shape=[_sds((M, N), F32), _sds((M, N), BF16)],
        compiler_params=_params(("parallel", "arbitrary")))(a, b, *dep_arg)


def _rope_tables(pos_col, inv_freq):
    T = pos_col.shape[0]

    def body(p_ref, f_ref, c_ref, s_ref):
        ang = p_ref[...].astype(F32) * f_ref[...]
        lane = lax.broadcasted_iota(jnp.int32, ang.shape, 1)
        c_ref[...] = jnp.cos(ang)
        sn = jnp.sin(ang)
        s_ref[...] = jnp.where((lane % HEAD_DIM) < HEAD_DIM // 2, -sn, sn)

    tm = 1024
    return pl.pallas_call(
        body, name="rope_tables", grid=(T // tm,),
        in_specs=[pl.BlockSpec((tm, 1), lambda i: (i, 0)), pl.BlockSpec((1, LANES), lambda i: (0, 0))],
        out_specs=[pl.BlockSpec((tm, LANES), lambda i: (i, 0))] * 2,
        out_shape=[_sds((T, LANES), F32)] * 2, compiler_params=_params(("parallel",)))(pos_col, inv_freq)


def _deinterleave(scr, out_ref, d, tm, nblk):
    for r in range(d):
        for cb in range(nblk):
            out_ref[r, :, cb * LANES:(cb + 1) * LANES] = scr[cb, pl.ds(r, tm // d, stride=d), :].astype(out_ref.dtype)


def _interleave(in_ref, scr, d, tm, nblk):
    for r in range(d):
        for cb in range(nblk):
            scr[cb, pl.ds(r, tm // d, stride=d), :] = in_ref[r, :, cb * LANES:(cb + 1) * LANES].astype(F32)


def _proj_rope(x, g, w_in, cos, sin, tm=512):
    T, D = x.shape
    dils = [d for _, d in B_PATTERNS if d > 1]
    nbb = B_W // LANES
    scale = HEAD_DIM ** -0.5
    cuts = [0, A_Q_W, A_Q_W + A_KV_W, A_Q_W + 2 * A_KV_W, A_Q_W + 2 * A_KV_W + B_W, A_Q_W + 2 * A_KV_W + 2 * B_W,
            A_Q_W + 2 * A_KV_W + 3 * B_W]

    def body(x_ref, g_ref, w_ref, c_ref, s_ref, h_ref, aq_ref, ak_ref, av_ref, *rest):
        b_refs, scr = rest[:-1], rest[-1]
        xhat, _ = _rms_stats(x_ref[...])
        h = (xhat * g_ref[...]).astype(BF16)
        h_ref[...] = h
        cs, sn = c_ref[...], s_ref[...]

        def project(idx, ref, rope, mult, which):
            return _dot_nt(h, w_ref[cuts[idx]:cuts[idx + 1], :])

        def finish(idx, ref, rope, mult, which, whole):
            for cb in range((cuts[idx + 1] - cuts[idx]) // LANES):
                p = whole[:, cb * LANES:(cb + 1) * LANES]
                if rope:
                    p = p * cs + _swap32(p) * sn
                if mult != 1.0:
                    p = p * mult
                ref[:, cb * LANES:(cb + 1) * LANES] = p.astype(BF16)
                if which is not None:
                    scr[which, cb] = p
            if which is not None:
                for di, d in enumerate(dils):
                    _deinterleave(scr.at[which], b_refs[3 * (di + 1) + which], d, tm, nbb)

        _two_phase([(0, aq_ref, True, scale, None), (1, ak_ref, True, 1.0, None), (2, av_ref, False, 1.0, None),
                    (3, b_refs[0], True, scale, 0), (4, b_refs[1], True, 1.0, 1), (5, b_refs[2], False, 1.0, 2)], project, finish)

    row = lambda w: pl.BlockSpec((tm, w), lambda i: (i, 0))
    out_specs = [row(D), row(A_Q_W), row(A_KV_W), row(A_KV_W)] + [row(B_W)] * 3
    out_shape = [_sds((T, D), BF16), _sds((T, A_Q_W), BF16), _sds((T, A_KV_W), BF16), _sds((T, A_KV_W), BF16)] + [_sds((T, B_W), BF16)] * 3
    for d in dils:
        out_specs += [pl.BlockSpec((d, tm // d, B_W), lambda i: (0, i, 0))] * 3
        out_shape += [_sds((d, T // d, B_W), BF16)] * 3
    return pl.pallas_call(
        body, name="proj_rope", grid=(T // tm,),
        in_specs=[row(D), pl.BlockSpec((1, D), lambda i: (0, 0)), pl.BlockSpec(w_in.shape, lambda i: (0, 0)), row(LANES), row(LANES)],
        out_specs=out_specs, out_shape=out_shape, scratch_shapes=[pltpu.VMEM((3, nbb, tm, LANES), F32)],
        compiler_params=_params(("parallel",)))(x, g, w_in, cos, sin)


def _band_bias(rel, qb, kw, hw):
    ri = lax.broadcasted_iota(jnp.int32, (2 * qb, kw), 0) & (qb - 1)
    ci = lax.broadcasted_iota(jnp.int32, (2 * qb, kw), 1)
    return jnp.where(jnp.abs(ri + rel - ci) <= hw, 0.0, NEG).astype(F32)


def _stack_heads(x, lo):
    z = jnp.zeros_like(x)
    return jnp.concatenate([jnp.where(lo, x, z), jnp.where(lo, z, x)], axis=0)


def _unstack_heads(y, lo):
    qb = y.shape[0] // 2
    return jnp.where(lo, y[:qb], y[qb:])


def _band_setup(bias_scr, qb, kw, hw):
    if bias_scr is not None:
        for i in range(3):
            bias_scr[i] = _band_bias(i * hw, qb, kw, hw)


def _band_window(bias_scr, qs, L, qb, kw, hw):
    ws = pl.multiple_of(jnp.clip(qs - hw, 0, L - kw), 64)
    if bias_scr is None:
        return ws, _band_bias(qs - ws, qb, kw, hw)
    return ws, bias_scr[lax.shift_right_logical(qs - ws, hw.bit_length() - 1)]


def _dup_kv_head(src_ref, dst_ref, head, L):
    step = min(L, 1024)
    for r0 in range(0, L, step):
        xf = src_ref[r0:r0 + step, :].astype(F32)
        lane = lax.broadcasted_iota(jnp.int32, xf.shape, 1)
        keep = jnp.logical_xor(lane < HEAD_DIM, head == 1)
        dst_ref[r0:r0 + step, :] = jnp.where(keep, xf, pltpu.roll(xf, HEAD_DIM, axis=1)).astype(dst_ref.dtype)


def _attn_fwd(q, k, v, sink, hw, gqa, out_dtype, name, qb=QB, blocks_per_step=8, out_cols=None):
    NB, L, Cq = q.shape
    Ls = min(L, 2048)
    kw = min(qb + 2 * hw, L)
    tables = L >= qb + 2 * hw
    unroll = min(blocks_per_step, Ls // qb)
    nlb = 1 if (gqa or L > SHORT_SEQ) else Cq // LANES

    def body(sink_ref, q_ref, k_ref, v_ref, o_ref, lse_ref, *scr):
        b, s_idx = pl.program_id(1), pl.program_id(2)
        bias_scr = scr[0] if tables else None
        _band_setup(bias_scr, qb, kw, hw)
        if gqa:
            kd, vd = scr[-2:]

            @pl.when(s_idx == 0)
            def _():
                _dup_kv_head(k_ref, kd, b // 2, L)
                _dup_kv_head(v_ref, vd, b // 2, L)
        else:
            kd, vd = k_ref, v_ref
        lane = lax.broadcasted_iota(jnp.int32, (qb, LANES), 1)
        lo = lane < HEAD_DIM
        if gqa:
            row = lax.broadcasted_iota(jnp.int32, (2 * qb, 1), 0)
            sk = jnp.where(row < qb, sink_ref[2 * b], sink_ref[2 * b + 1])

        def block(ql, col):
            qs = s_idx * Ls + ql
            ws, bias = _band_window(bias_scr, qs, L, qb, kw, hw)
            return ws, _dot_nt(_stack_heads(q_ref[pl.ds(ql, qb), col], lo), kd[pl.ds(ws, kw), col]) + bias

        def finish(ql, col, scores):
            ws, s = scores
            m = jnp.max(s, axis=-1, keepdims=True)
            if gqa:
                m = jnp.maximum(m, sk)
            p = jnp.exp(s - m)
            den = jnp.sum(p, axis=-1, keepdims=True)
            if gqa:
                den = den + jnp.exp(sk - m)
            o = _dot(p.astype(BF16), vd[pl.ds(ws, kw), col]) * (1.0 / den)
            o_ref[pl.ds(ql, qb), col] = _unstack_heads(o, lo).astype(o_ref.dtype)
            lse_ref[pl.ds(ql, qb), col] = _unstack_heads(m + jnp.log(den), lo)

        for lb in range(nlb):
            def step(n, carry, col=slice(lb * LANES, (lb + 1) * LANES)):
                _two_phase([(pl.multiple_of((n * unroll + u) * qb, qb), col) for u in range(unroll)], block, finish)
                return carry

            lax.fori_loop(0, Ls // (qb * unroll), step, 0)

    kv_map = (lambda r, b, s: (r, 0, 0)) if gqa else (lambda r, b, s: (r, 0, b))
    seg = pl.BlockSpec((None, Ls, nlb * LANES), lambda r, b, s: (r, s, b))
    return pl.pallas_call(
        body, name=name, grid=(NB, Cq // (nlb * LANES), L // Ls),
        in_specs=[pl.BlockSpec(memory_space=pltpu.SMEM), seg, pl.BlockSpec((None, L, nlb * LANES), kv_map),
                  pl.BlockSpec((None, L, nlb * LANES), kv_map)],
        out_specs=[seg, seg], out_shape=[_sds((NB, L, out_cols or Cq), out_dtype), _sds((NB, L, Cq), F32)],
        scratch_shapes=([pltpu.VMEM((3, 2 * qb, kw), F32)] if tables else []) + ([pltpu.VMEM((L, LANES), BF16)] * 2 if gqa else []),
        compiler_params=_params(("parallel", "parallel", "arbitrary")))(sink, q, k, v)


def _attn_bwd(q, k, v, do, lse, delta, sink, hw, gqa, name, qb=QB, blocks_per_step=8):
    NB, L, Cq = q.shape
    Ck = k.shape[2]
    Ls = min(L, 2048)
    kw = min(qb + 2 * hw, L)
    reps = kw // LANES
    nseg = L // Ls
    scale = HEAD_DIM ** -0.5
    tables = L >= qb + 2 * hw
    unroll = min(blocks_per_step, Ls // qb)
    nlb = 1 if (gqa or L > SHORT_SEQ) else Cq // LANES

    def body(sink_ref, q_ref, do_ref, lse_ref, dl_ref, k_ref, v_ref, dq_ref, dk_ref, dv_ref, dsk_ref, *scr):
        b, s_idx = pl.program_id(1), pl.program_id(2)
        lane = lax.broadcasted_iota(jnp.int32, (qb, LANES), 1)
        lo = lane < HEAD_DIM
        bias_scr = scr[0] if tables else None
        _band_setup(bias_scr, qb, kw, hw)
        if gqa:
            kd, vd, dk_acc, dv_acc, dsk_acc = scr[-5:]

            @pl.when(s_idx == 0)
            def _():
                _dup_kv_head(k_ref, kd, b // 2, L)
                _dup_kv_head(v_ref, vd, b // 2, L)
                dk_acc[...] = jnp.zeros_like(dk_acc)
                dv_acc[...] = jnp.zeros_like(dv_acc)
                dsk_acc[...] = jnp.zeros_like(dsk_acc)

            @pl.when((s_idx == 0) & (b == 0))
            def _():
                dk_ref[...] = jnp.zeros_like(dk_ref)
                dv_ref[...] = jnp.zeros_like(dv_ref)
        else:
            kd, vd = k_ref, v_ref
            dk_acc, dv_acc = scr[-2:]

            @pl.when(s_idx == 0)
            def _():
                dk_acc[...] = jnp.zeros_like(dk_acc)
                dv_acc[...] = jnp.zeros_like(dv_acc)

        def block(ql, col):
            qs = s_idx * Ls + ql
            ws, bias = _band_window(bias_scr, qs, L, qb, kw, hw)
            qv, dov = q_ref[pl.ds(ql, qb), col], do_ref[pl.ds(ql, qb), col]
            lse, dl = lse_ref[pl.ds(ql, qb), col], dl_ref[pl.ds(ql, qb), col]
            kv_, vv = kd[pl.ds(ws, kw), col], vd[pl.ds(ws, kw), col]
            q2, do2 = _stack_heads(qv, lo), _stack_heads(dov, lo)
            return ws, q2, do2, lse, dl, _dot_nt(q2, kv_) + bias, _dot_nt(do2, vv)

        def finish(ql, col, held):
            ws, q2, do2, lse, dl, s, dp = held
            lse_sw, dl_sw = pltpu.roll(lse, HEAD_DIM, axis=1), pltpu.roll(dl, HEAD_DIM, axis=1)
            lse2 = jnp.concatenate([jnp.where(lo, lse, lse_sw), jnp.where(lo, lse_sw, lse)], axis=0)
            dl2 = jnp.concatenate([jnp.where(lo, dl, dl_sw), jnp.where(lo, dl_sw, dl)], axis=0)
            p = jnp.exp(s - jnp.tile(lse2, (1, reps)))
            ds = (p * (dp - jnp.tile(dl2, (1, reps)))).astype(BF16)
            dq_ref[pl.ds(ql, qb), col] = (_unstack_heads(_dot(ds, kd[pl.ds(ws, kw), col]), lo) * scale).astype(dq_ref.dtype)
            both = _dot_tn(jnp.concatenate([ds, p.astype(BF16)], axis=1), jnp.concatenate([q2, do2], axis=1))
            dk_acc[pl.ds(ws, kw), col] += both[:kw, :LANES]
            dv_acc[pl.ds(ws, kw), col] += both[kw:, LANES:]
            if gqa:
                sk = jnp.where(lo, sink_ref[2 * b], sink_ref[2 * b + 1])
                dsk_acc[...] += -jnp.exp(sk - lse) * dl

        for lb in range(nlb):
            def step(n, carry, col=slice(lb * LANES, (lb + 1) * LANES)):
                _two_phase([(pl.multiple_of((n * unroll + u) * qb, qb), col) for u in range(unroll)], block, finish)
                return carry

            lax.fori_loop(0, Ls // (qb * unroll), step, 0)

        if gqa:
            @pl.when(s_idx == nseg - 1)
            def _():
                step_rows = min(L, 1024)
                for r0 in range(0, L, step_rows):
                    lanek = lax.broadcasted_iota(jnp.int32, (step_rows, LANES), 1)
                    mine = jnp.logical_xor(lanek < HEAD_DIM, (b // 2) == 1)
                    for acc, ref in ((dk_acc, dk_ref), (dv_acc, dv_ref)):
                        a = acc[r0:r0 + step_rows, :]
                        ref[r0:r0 + step_rows, :] += jnp.where(mine, a + pltpu.roll(a, HEAD_DIM, axis=1), 0.0)
                dsk_ref[...] = dsk_acc[...].reshape(qb // SUBLANES, SUBLANES, LANES).sum(axis=0)
        else:
            dsk_ref[...] = jnp.zeros_like(dsk_ref)

            @pl.when(s_idx == nseg - 1)
            def _():
                dk_ref[...] = dk_acc[...].astype(dk_ref.dtype)
                dv_ref[...] = dv_acc[...].astype(dv_ref.dtype)

    kv_map = (lambda r, b, s: (r, 0, 0)) if gqa else (lambda r, b, s: (r, 0, b))
    seg = pl.BlockSpec((None, Ls, nlb * LANES), lambda r, b, s: (r, s, b))
    full = pl.BlockSpec((None, L, nlb * LANES), kv_map)
    scratch = [pltpu.VMEM((3, 2 * qb, kw), F32)] if tables else []
    if gqa:
        scratch += [pltpu.VMEM((L, LANES), BF16)] * 2 + [pltpu.VMEM((L, LANES), F32)] * 2 + [pltpu.VMEM((qb, LANES), F32)]
    else:
        scratch += [pltpu.VMEM((L, nlb * LANES), F32)] * 2
    kv_dtype = F32 if gqa else BF16
    return pl.pallas_call(
        body, name=name, grid=(NB, Cq // (nlb * LANES), nseg),
        in_specs=[pl.BlockSpec(memory_space=pltpu.SMEM), seg, seg, seg, seg, full, full],
        out_specs=[seg, full, full, pl.BlockSpec((None, None, SUBLANES, LANES), lambda r, b, s: (r, b, 0, 0))],
        out_shape=[_sds((NB, L, Cq), BF16), _sds((NB, L, Ck), kv_dtype), _sds((NB, L, Ck), kv_dtype),
                   _sds((NB, Cq // LANES, SUBLANES, LANES), F32)],
        scratch_shapes=scratch,
        compiler_params=_params(("arbitrary", "arbitrary", "arbitrary")))(sink, q, do, lse, delta, k, v)


def _dilated_fwd(cat, qkv, hw, tile=2048):
    T = cat.shape[0]
    dils = sorted(qkv)
    nbb, na = B_W // LANES, A_Q_W // LANES
    qb, kw = QB, QB + 2 * hw
    rows_merge = 256
    assert T % tile == 0 and all(tile % (d * qb) == 0 and T // d >= kw for d in dils)

    def body(cat_in, *refs):
        qkv_refs = {d: refs[3 * j:3 * j + 3] for j, d in enumerate(dils)}
        cat_ref, lg_refs = refs[3 * len(dils)], refs[3 * len(dils) + 1:4 * len(dils) + 1]
        o_scr, l_scr, bias_scr = refs[4 * len(dils) + 1:]
        i = pl.program_id(1)
        _band_setup(bias_scr, qb, kw, hw)
        lane = lax.broadcasted_iota(jnp.int32, (qb, LANES), 1)
        lo = lane < HEAD_DIM
        for pi, d in enumerate(dils):
            q_ref, k_ref, v_ref = qkv_refs[d]
            L, rows = T // d, tile // d

            def place(r, n, d=d):
                return pl.ds(r + d * n * qb, qb, stride=d) if d > 1 else pl.ds(n * qb, qb)

            def scores(r, n, q_ref=q_ref, k_ref=k_ref, L=L, rows=rows):
                ws, bias = _band_window(bias_scr, i * rows + n * qb, L, qb, kw, hw)
                return ws, _dot_nt(_stack_heads(q_ref[r, n * qb:(n + 1) * qb, :], lo), k_ref[r, pl.ds(ws, kw), :]) + bias

            def finish(r, n, held, v_ref=v_ref, pi=pi, place=place):
                ws, s = held
                m = jnp.max(s, axis=-1, keepdims=True)
                p = jnp.exp(s - m)
                den = jnp.sum(p, axis=-1, keepdims=True)
                o = _dot(p.astype(BF16), v_ref[r, pl.ds(ws, kw), :]) * (1.0 / den)
                o_scr[pi, place(r, n), :] = _unstack_heads(o, lo)
                l_scr[pi, place(r, n), :] = _unstack_heads(m + jnp.log(den), lo)

            blocks = [(r, n) for r in range(d) for n in range(rows // qb)]
            for g0 in range(0, len(blocks), 8):
                _two_phase(blocks[g0:g0 + 8], scores, finish)

        for r0 in range(0, tile, rows_merge):
            rs = slice(r0, r0 + rows_merge)
            ls_ = [l_scr[pi, rs, :] for pi in range(len(dils))]
            m = ls_[0]
            for l in ls_[1:]:
                m = jnp.maximum(m, l)
            es = [jnp.exp(l - m) for l in ls_]
            den, out = es[0], es[0] * o_scr[0, rs, :]
            for pi in range(1, len(dils)):
                den = den + es[pi]
                out = out + es[pi] * o_scr[pi, rs, :]
            cat_ref[rs, :] = (out * (1.0 / den)).astype(BF16)
            l_scr[0, rs, :] = m + jnp.log(den)
        for lg_ref, d in zip(lg_refs, dils):
            for r in range(d):
                lg_ref[r] = l_scr[0, pl.ds(r, tile // d, stride=d), :] if d > 1 else l_scr[0]

    in_specs = [pl.BlockSpec(memory_space=pl.ANY)]
    operands = [cat]
    for d in dils:
        in_specs += [pl.BlockSpec((d, tile // d, LANES), lambda b, i: (0, i, b))] + [pl.BlockSpec((d, T // d, LANES), lambda b, i: (0, 0, b))] * 2
        operands += list(qkv[d])
    return pl.pallas_call(
        body, name="dilated_fwd", grid=(nbb, T // tile), in_specs=in_specs,
        out_specs=[pl.BlockSpec((tile, LANES), lambda b, i: (i, na + b))] + [pl.BlockSpec((d, tile // d, LANES), lambda b, i: (0, i, b)) for d in dils],
        out_shape=[_sds(cat.shape, BF16)] + [_sds((d, T // d, B_W), F32) for d in dils],
        input_output_aliases={0: 0},
        scratch_shapes=[pltpu.VMEM((len(dils), tile, LANES), F32)] * 2 + [pltpu.VMEM((3, 2 * qb, kw), F32)],
        compiler_params=_params(("parallel", "arbitrary")))(*operands)


def _out_proj(x, cat, w_out, tm=512):
    T, D = x.shape

    def body(x_ref, c_ref, w_ref, o_ref):
        o_ref[...] = x_ref[...] + _dot(c_ref[...], w_ref[...])

    row = lambda w: pl.BlockSpec((tm, w), lambda i: (i, 0))
    return pl.pallas_call(
        body, name="out_proj", grid=(T // tm,), in_specs=[row(D), row(cat.shape[1]), pl.BlockSpec(w_out.shape, lambda i: (0, 0))],
        out_specs=row(D), out_shape=_sds((T, D), F32), compiler_params=_params(("parallel",)))(x, cat, w_out)


def _dcat(dx, w_out, cat, tm=512, dep=None):
    T, D = dx.shape
    C = cat.shape[1]
    nba, nbb = A_Q_W // LANES, B_W // LANES
    nt = T // tm

    def body(dx_ref, w_ref, cat_ref, doa_ref, dla_ref, dob1_ref, dlb1_ref, dob4_ref, dlb4_ref, dob16_ref, dlb16_ref, dw_ref, dwb_ref,
             sdo, sdl):
        @pl.when(pl.program_id(0) == 0)
        def _():
            dw_ref[...] = jnp.zeros_like(dw_ref)

        dxb = dx_ref[...].astype(BF16)
        dc = _dot_nt(dxb, w_ref[...])
        dw_ref[...] += _dot_tn(cat_ref[...], dxb)

        @pl.when(pl.program_id(0) == nt - 1)
        def _():
            dwb_ref[...] = dw_ref[...].astype(BF16)

        ri =lax.broadcasted_iota(jnp.int32, (LANES, LANES), 0)
        ci = lax.broadcasted_iota(jnp.int32, (LANES, LANES), 1)
        same_head = ((ri // HEAD_DIM) == (ci // HEAD_DIM)).astype(BF16)
        for cb in range(C // LANES):
            cols = slice(cb * LANES, (cb + 1) * LANES)
            blk = dc[:, cols]
            prod = blk * cat_ref[:, cols].astype(F32)
            hi = prod.astype(BF16)
            lo_ = (prod - hi.astype(F32)).astype(BF16)
            dl = _dot(hi, same_head) + _dot(lo_, same_head)
            if cb < nba:
                doa_ref[:, cols] = blk.astype(BF16)
                dla_ref[:, cols] = dl
            else:
                bcols = slice((cb - nba) * LANES, (cb - nba + 1) * LANES)
                dob1_ref[:, bcols] = blk.astype(BF16)
                dlb1_ref[:, bcols] = dl
                sdo[cb - nba] = blk
                sdl[cb - nba] = dl
        _deinterleave(sdo, dob4_ref, 4, tm, nbb)
        _deinterleave(sdl, dlb4_ref, 4, tm, nbb)
        _deinterleave(sdo, dob16_ref, 16, tm, nbb)
        _deinterleave(sdl, dlb16_ref, 16, tm, nbb)

    row = lambda w: pl.BlockSpec((tm, w), lambda i: (i, 0))
    perm = lambda d: pl.BlockSpec((d, tm // d, B_W), lambda i: (0, i, 0))
    whole = pl.BlockSpec((C, D), lambda i: (0, 0))
    body, dep_spec, dep_arg = _ordered(body, 3, dep)
    outs = pl.pallas_call(
        body, name="dcat", grid=(nt,), in_specs=[row(D), whole, row(C)] + dep_spec,
        out_specs=[row(A_Q_W), row(A_Q_W), row(B_W), row(B_W), perm(4), perm(4), perm(16), perm(16), whole, whole],
        out_shape=[_sds((T, A_Q_W), BF16), _sds((T, A_Q_W), F32), _sds((T, B_W), BF16), _sds((T, B_W), F32),
                   _sds((4, T // 4, B_W), BF16), _sds((4, T // 4, B_W), F32), _sds((16, T // 16, B_W), BF16), _sds((16, T // 16, B_W), F32),
                   _sds((C, D), F32), _sds((C, D), BF16)],
        scratch_shapes=[pltpu.VMEM((nbb, tm, LANES), F32)] * 2, compiler_params=_params(("arbitrary",)))(dx, w_out, cat, *dep_arg)
    return (*outs[:8], (outs[8], outs[9]))


def _mixer_in_bwd(dqa, dka, dva, b1, b4, b16, cos, sin, w_in, x, g, dres, tm=512):
    T, D = x.shape
    nbb = B_W // LANES
    width = A_Q_W + 2 * A_KV_W + 3 * B_W

    def body(dqa_ref, dka_ref, dva_ref, q1, k1, v1, q4, k4, v4, q16, k16, v16, c_ref, s_ref, w_ref, x_ref, g_ref, dr_ref,
             o_ref, dx_ref, dg_ref, scr):
        @pl.when(pl.program_id(0) == 0)
        def _():
            dg_ref[...] = jnp.zeros_like(dg_ref)

        cs, sn = c_ref[...], s_ref[...]
        dh = []

        def unrope(t):
            return t * cs + _swap32(t * sn)

        def project(c0, c1):
            t = _dot(o_ref[:, c0:c1], w_ref[c0:c1, :])
            dh[:] = [t if not dh else dh[0] + t]

        col = 0
        for ref, rope in ((dqa_ref, True), (dka_ref, True), (dva_ref, False)):
            for cb in range(ref.shape[1] // LANES):
                t = ref[:, cb * LANES:(cb + 1) * LANES].astype(F32)
                o_ref[:, col:col + LANES] = (unrope(t) if rope else t).astype(BF16)
                col += LANES
        project(0, col)
        for which, (r1, r4, r16, rope) in enumerate(((q1, q4, q16, True), (k1, k4, k16, True), (v1, v4, v16, False))):
            _interleave(r4, scr.at[0], 4, tm, nbb)
            _interleave(r16, scr.at[1], 16, tm, nbb)
            for cb in range(nbb):
                t = r1[:, cb * LANES:(cb + 1) * LANES].astype(F32) + scr[0, cb] + scr[1, cb]
                o_ref[:, col:col + LANES] = (unrope(t) if rope else t).astype(BF16)
                col += LANES
            project(col - B_W, col)
        dxn, dg = _rms_bwd(dh[0], x_ref[...], g_ref[...])
        dg_ref[...] += dg
        dx_ref[...] = dr_ref[...] + dxn

    row = lambda w: pl.BlockSpec((tm, w), lambda i: (i, 0))
    perm = lambda d: pl.BlockSpec((d, tm // d, B_W), lambda i: (0, i, 0))
    return pl.pallas_call(
        body, name="mixer_in_bwd", grid=(T // tm,),
        in_specs=[row(A_Q_W), row(A_KV_W), row(A_KV_W)] + [row(B_W)] * 3 + [perm(4)] * 3 + [perm(16)] * 3 + [row(LANES), row(LANES)]
        + [_resident(w_in.shape), row(D), pl.BlockSpec((1, D), lambda i: (0, 0)), row(D)],
        out_specs=[row(width), row(D), pl.BlockSpec((SUBLANES, D), lambda i: (0, 0))],
        out_shape=[_sds((T, width), BF16), _sds((T, D), F32), _sds((SUBLANES, D), F32)],
        scratch_shapes=[pltpu.VMEM((2, nbb, tm, LANES), F32)],
        compiler_params=_params(("arbitrary",)))(dqa, dka, dva, *b1, *b4, *b16, cos, sin, w_in, x, g, dres)


def _grad_push_plan(n):
    def plan(refs):
        x, y, c = _mesh_pos()
        return [(refs[k].at[chip], refs[n + k].at[rel], dev) for k in range(n) for rel, (dev, chip) in enumerate(_chip_peers(x, y, c))]
    return plan


def _sum_own(me_arr, g, landed, name):
    ns, R, C = g.shape
    tr = R // 2 if (R // 2) % 16 == 0 else R

    def body(me_ref, g_ref, x_ref, o_ref):
        acc = g_ref[...]
        for rel in range(ns - 1):
            acc = acc + x_ref[rel].astype(F32)
        o_ref[...] = acc

    grid_spec = pltpu.PrefetchScalarGridSpec(
        num_scalar_prefetch=1, grid=(R // tr,),
        in_specs=[pl.BlockSpec((None, tr, C), lambda t, me: (me[0], t, 0)), pl.BlockSpec((ns - 1, tr, C), lambda t, me: (0, t, 0))],
        out_specs=pl.BlockSpec((tr, C), lambda t, me: (t, 0)))
    return pl.pallas_call(body, name=name, grid_spec=grid_spec, out_shape=_sds((R, C), F32),
                          compiler_params=_params(("parallel",)))(me_arr, g, landed)


def _swap_plan(n):
    def plan(refs):
        x, y, c = _mesh_pos()
        return [(refs[k], refs[n + k], (x, y, 1 - c)) for k in range(n)]
    return plan


def _allreduce_small(v, dep):
    rows, W = v.shape

    def body(v_ref, o_ref, buf, send, recv):
        x, y, c = _mesh_pos()
        me = 4 * x + 2 * y + c
        cps = []
        for m in range(1, N_DEV):
            dev = (x ^ (m >> 2), y ^ ((m >> 1) & 1), c ^ (m & 1))
            cp = pltpu.make_async_remote_copy(src_ref=v_ref, dst_ref=buf.at[me], send_sem=send.at[m - 1], recv_sem=recv.at[m - 1],
                                              device_id=dev, device_id_type=MESH)
            cp.start()
            cps.append(cp)
        for m in range(1, N_DEV):
            pltpu.make_async_remote_copy(src_ref=v_ref, dst_ref=buf.at[me ^ m], send_sem=send.at[m - 1], recv_sem=recv.at[m - 1],
                                         device_id=(x, y, c), device_id_type=MESH).wait_recv()
        for cp in cps:
            cp.wait_send()
        buf[me] = v_ref[...]
        acc = buf[0]
        for i in range(1, N_DEV):
            acc = acc + buf[i]
        o_ref[...] = acc

    body, dep_spec, dep_arg = _ordered(body, 1, dep)
    return pl.pallas_call(
        body, name="allreduce_small", out_shape=_sds((rows, W), F32), in_specs=[pl.BlockSpec(memory_space=pltpu.VMEM)] + dep_spec,
        scratch_shapes=[pltpu.VMEM((N_DEV, rows, W), F32), pltpu.SemaphoreType.DMA((N_DEV - 1,)), pltpu.SemaphoreType.DMA((N_DEV - 1,))],
        compiler_params=_params())(v, *dep_arg)


def _adamw_math(w, g, m, v):
    c1 = 1.0 / (1.0 - ADAM_B1 ** ADAM_STEP)
    c2 = 1.0 / (1.0 - ADAM_B2 ** ADAM_STEP)
    nm = ADAM_B1 * m + (1.0 - ADAM_B1) * g
    nv = ADAM_B2 * v + (1.0 - ADAM_B2) * (g * g)
    return -ADAM_LR * ((nm * c1) / (jnp.sqrt(nv * c2) + ADAM_EPS) + ADAM_WD * w), nm, nv


def _adamw_small(rows, params):
    n = len(params)
    n_sink = params[-1][0].shape[1]

    def body(rows_ref, *refs):
        ins, outs = refs[:3 * n], refs[3 * n:]
        for j in range(n):
            g = rows_ref[j:j + 1, 0:n_sink] if j == n - 1 else rows_ref[j:j + 1, :]
            d, nm, nv = _adamw_math(ins[3 * j][...], g, ins[3 * j + 1][...], ins[3 * j + 2][...])
            for ref, val in zip(outs[4 * j:4 * j + 4], (g, d, nm, nv)):
                ref[...] = val
        outs[-1][...] = rows_ref[n - 1:n, n_sink:n_sink + 1]

    flat = [a for p in params for a in p]
    outs = pl.pallas_call(body, name="adamw_small", out_shape=[_sds(p[0].shape, F32) for p in params for _ in range(4)] + [_sds((1, 1), F32)],
                          compiler_params=_params())(rows, *flat)
    return [outs[4 * j:4 * j + 4] for j in range(n)], outs[-1]


def _adamw(w, gp, gq, m, v, name):
    R, C = w.shape
    tr = R // 2 if (R // 2) % SUBLANES == 0 else R

    def body(w_ref, gp_ref, gq_ref, m_ref, v_ref, g_ref, d_ref, nm_ref, nv_ref):
        gv = gp_ref[...] + gq_ref[...]
        g_ref[...] = gv
        d_ref[...], nm_ref[...], nv_ref[...] = _adamw_math(w_ref[...], gv, m_ref[...], v_ref[...])

    blk = pl.BlockSpec((tr, C), lambda t: (t, 0))
    return pl.pallas_call(body, name=name, grid=(R // tr,), in_specs=[blk] * 5, out_specs=[blk] * 4,
                          out_shape=[_sds((R, C), F32)] * 4, compiler_params=_params(("parallel",)))(w, gp, gq, m, v)


def _rope(positions, after):
    inv_freq = 1.0 / (ROPE_THETA ** (jnp.arange(0, HEAD_DIM, 2, dtype=F32) / HEAD_DIM))
    inv_freq = jnp.tile(inv_freq, LANES // (HEAD_DIM // 2)).reshape(1, LANES) + after[0, 0]
    return _rope_tables(positions.reshape(-1, 1), inv_freq)


def _local_step(x, rope, target, norms, a_sink, comm):
    T, D = x.shape
    g1, gm, g2, gf = norms
    cos, sin = rope
    no_sink = jnp.zeros((2 * (B_W // LANES),), F32)
    W = {k: comm.weight(k, x) for k in ("wg1", "wu1")}

    h1, gate1, up1, act1 = _ffn_up(x, g1, W["wg1"], W["wu1"], "ffn1_up", dep=comm.dep())
    W["wd1"] = comm.weight("wd1", act1)
    x1 = _ffn_down(x, act1, W["wd1"], "ffn1_down")
    W["w_in"] = comm.weight("w_in", x1)
    (h2, aq, ak, av, bq1, bk1, bv1, bq4, bk4, bv4, bq16, bk16, bv16) = _proj_rope(x1, gm, W["w_in"], cos, sin)
    cat, a_lse = _attn_fwd(aq[None], ak[None], av[None], a_sink, A_HALF_WINDOW, True, BF16, "attn_a_fwd", qb=2 * QB, blocks_per_step=4,
                           out_cols=A_Q_W + B_W)
    bqs = {1: (bq1[None], bk1[None], bv1[None]), 4: (bq4, bk4, bv4), 16: (bq16, bk16, bv16)}
    (b_hw,) = {w // (2 * d) for w, d in B_PATTERNS}
    cat, lg1, lg4, lg16 = _dilated_fwd(cat[0], bqs, b_hw)
    lg1 = lg1[0]
    W["w_out"] = comm.weight("w_out", cat)
    x2 = _out_proj(x1, cat, W["w_out"])
    for k in ("wg2", "wu2", "wd2"):
        W[k] = comm.weight(k, x2)
    dx3, h3, gate2, up2, act2, dgf, loss8 = _ffn_loss(x2, g2, W["wg2"], W["wu2"], W["wd2"], gf, target, "ffn2_fwd")

    dx2, dff2, dgate2, dup2, dg2 = _ffn_dx(dx3, x2, g2, gate2, up2, W["wg2"], W["wu2"], W["wd2"], "ffn2_dx")
    fb = gate2.shape[1] // 2
    dwg2 = _tn(dgate2, h3, fb, "ffn2_dw_gate")
    dwu2 = _tn(dup2, h3, fb, "ffn2_dw_up")
    dwd2 = _tn(act2, dff2, fb, "ffn2_dw_down")
    comm.ready(dict(wg2=dwg2, wu2=dwu2, wd2=dwd2), dwd2[0])

    doa, dla, dob1, dlb1, dob4, dlb4, dob16, dlb16, dw_out = _dcat(dx2, W["w_out"], cat, dep=comm.dep())
    dqa, dka, dva, dsk = _attn_bwd(aq[None], ak[None], av[None], doa[None], a_lse, dla[None], a_sink, A_HALF_WINDOW, True, "attn_a_bwd")
    bwd_in = {1: (dob1[None], lg1[None], dlb1[None]), 4: (dob4, lg4, dlb4), 16: (dob16, lg16, dlb16)}
    bg = {}
    for w, d in B_PATTERNS:
        q_, k_, v_ = bqs[d]
        do_, l_, dl_ = bwd_in[d]
        bg[d] = _attn_bwd(q_, k_, v_, do_, l_, dl_, no_sink, w // (2 * d), False, f"attn_b{d}_bwd")[:3]
    dproj, dx1, dgm = _mixer_in_bwd(dqa[0], dka[0], dva[0], [t[0] for t in bg[1]], bg[4], bg[16], cos, sin, W["w_in"], x1, gm, dx2)
    dw_in = _tn(dproj, h2, dproj.shape[1] // 2, "w_in_dw")
    comm.ready(dict(w_in=dw_in, w_out=dw_out), dw_in[0])

    dx0, dff1, dgate1, dup1, dg1 = _ffn_dx(dx1, x, g1, gate1, up1, W["wg1"], W["wu1"], W["wd1"], "ffn1_dx", dep=comm.dep())
    comm.settle(2, dx0)
    dwd1 = _tn(act1, dff1, fb, "ffn1_dw_down", dep=comm.dep())
    comm.ready(dict(wd1=dwd1), dwd1[0])
    dwg1 = _tn(dgate1, h1, fb, "ffn1_dw_gate", dep=comm.dep())
    comm.ready(dict(wg1=dwg1), dwg1[0])
    dwu1 = _tn(dup1, h1, fb, "ffn1_dw_up", dep=comm.dep())
    comm.ready(dict(wu1=dwu1), dwu1[0])

    dsink = dsk[0, :, :, ::HEAD_DIM].sum(axis=1).reshape(-1)
    small = dict(g1=dg1.sum(axis=0), gm=dgm.sum(axis=0), g2=dg2.sum(axis=0), gf=dgf.sum(axis=0), sink=dsink, loss=loss8[0, 0])
    return dx0, small


BIG = ("wg1", "wu1", "wd1", "w_in", "w_out", "wg2", "wu2", "wd2")
GATHER_GROUPS = (("wd1",), ("w_in",), ("w_out",), ("wg2", "wu2", "wd2"))


class _Comm:
    def __init__(self, shards, meanwhile):
        x, y, c = _mesh_pos()
        self.me = (2 * x + y).astype(jnp.int32).reshape(1)
        self.shards = shards
        self.token = None
        self.waiting = {}
        self.groups = []
        self.swaps = []
        first = ("wg1", "wu1")
        fulls ={k: _cast_place(self.me, shards[k], f"cast_{k}") for k in first}
        plan = _neighbour_plan([fulls[k].shape for k in first])
        send, recv, bufs, tok = _push_start("gather_first_start", [fulls[k] for k in first], 2 * len(first), plan, self.me)
        self.side = meanwhile(tok)
        fulls.update({k: _cast_place(self.me, shards[k], f"cast_{k}") for k in BIG if k not in first})
        bufs = _push_wait("gather_first_wait", send, recv, bufs, plan, [fulls[k] for k in BIG if k not in first] + list(self.side))
        self.full = dict(zip(first, _gather_forward(bufs)))
        dep = self.full[first[-1]]
        for gi, names in enumerate(GATHER_GROUPS):
            plan = _gather_plan(len(names))
            send, recv, bufs, self.token = _push_start(f"gather_start_{gi}", [fulls[k] for k in names], 3 * len(names), plan, dep)
            dep = self.token
            for k in names:
                self.waiting[k] = (gi, names, send, recv, bufs, plan)

    def dep(self):
        return self.token

    def weight(self, name, after):
        if name in self.waiting:
            gi, names, send, recv, bufs, plan = self.waiting[name]
            for k, buf in zip(names, _push_wait(f"gather_wait_{gi}", send, recv, bufs, plan, after)):
                self.full[k] = buf
                del self.waiting[k]
        full = self.full[name]
        return full.reshape(N_CHIPS * full.shape[1], full.shape[2])

    def ready(self, grads, after):
        names = list(grads)
        f32s, b16s = [], []
        for k in names:
            gf, gb = grads[k]
            f32s.append(gf.reshape((N_CHIPS,) + self.shards[k].shape))
            b16s.append(gb.reshape((N_CHIPS,) + self.shards[k].shape))
        n = len(names)
        lands = [lax.empty((N_CHIPS - 1,) + self.shards[k].shape, BF16) for k in names]
        plan = _grad_push_plan(n)
        send, recv, bufs, self.token = _push_start(f"grad_start_{names[0]}", b16s + lands, 3 * n, plan, after)
        self.groups.append((names, f32s, send, recv, bufs, plan))

    def settle(self, count, after):
        batch, self.groups = self.groups[:count], self.groups[count:]
        names_b, mine_b = [], []
        for names, f32s, send, recv, bufs, plan in batch:
            n = len(names)
            bufs = _push_wait(f"grad_wait_{names[0]}", send, recv, bufs, plan, mine_b[-1] if mine_b else after)
            mine_b += [_sum_own(self.me, f32s[i], bufs[n + i], f"sum_{k}") for i, k in enumerate(names)]
            names_b += names
        lands = [lax.empty(p.shape, F32) for p in mine_b]
        n = len(names_b)
        send2, recv2, both, self.token = _push_start(f"swap_start_{names_b[0]}", mine_b + lands, n, _swap_plan(n), after)
        self.swaps.append((names_b, send2, recv2, both))

    def partials(self, after):
        names_b, send2, recv2, both = self.swaps.pop(0)
        n = len(names_b)
        both = _push_wait(f"swap_wait_{names_b[0]}", send2, recv2, both, _swap_plan(n), after)
        return {k: (both[i], both[n + i]) for i, k in enumerate(names_b)}


def kernel(x, positions, norm_ffn1, w_gate1, w_up1, w_down1, norm_mix, w_in, a_sink, w_out, norm_ffn2, w_gate2, w_up2, w_down2, norm_final, loss_target, m_norm_ffn1, m_w_gate1, m_w_up1, m_w_down1, m_norm_mix, m_w_in, m_a_sink, m_w_out, m_norm_ffn2, m_w_gate2, m_w_up2, m_w_down2, m_norm_final, v_norm_ffn1, v_w_gate1, v_w_up1, v_w_down1, v_norm_mix, v_w_in, v_a_sink, v_w_out, v_norm_ffn2, v_w_gate2, v_w_up2, v_w_down2, v_norm_final):
    T, D = x.shape[1], x.shape[2]
    flip = ("wg1", "wu1", "w_in", "wg2", "wu2")

    def rows(k, a):
        return a[0].T if k in flip else a[0]

    given = dict(wg1=(w_gate1, m_w_gate1, v_w_gate1), wu1=(w_up1, m_w_up1, v_w_up1), wd1=(w_down1, m_w_down1, v_w_down1),
                 w_in=(w_in, m_w_in, v_w_in), w_out=(w_out, m_w_out, v_w_out), wg2=(w_gate2, m_w_gate2, v_w_gate2),
                 wu2=(w_up2, m_w_up2, v_w_up2), wd2=(w_down2, m_w_down2, v_w_down2))
    shards = {k: rows(k, given[k][0]) for k in BIG}

    comm = _Comm(shards, lambda tok: _rope(positions[0], tok))

    norms = (norm_ffn1, norm_mix, norm_ffn2, norm_final.reshape(1, D))
    grad_x, small = _local_step(x[0], comm.side, loss_target[0], norms, a_sink[0], comm)

    upd = {}

    def update(partial):
        for k in partial:
            outs = _adamw(shards[k], partial[k][0], partial[k][1], rows(k, given[k][1]), rows(k, given[k][2]), f"adamw_{k}")
            upd[k] = tuple((a.T if k in flip else a)[None] for a in outs)
        return outs[0]

    last = update(comm.partials(comm.dep()))
    comm.settle(2, last)

    def pad_row(a):
        a = a.reshape(-1)
        return jnp.pad(a, (0, D - a.shape[0]))

    row4 = pad_row(jnp.concatenate([small["sink"], small["loss"].reshape(1)]))
    vec = jnp.stack([small["g1"], small["gm"], small["g2"], small["gf"], row4] + [jnp.zeros((D,), F32)] * 3, axis=0)
    red = _allreduce_small(vec, comm.dep())
    comm.settle(1, red)
    last = update(comm.partials(comm.dep()))
    update(comm.partials(last))
    as_row = lambda a: a.reshape(1, -1)
    sm, loss = _adamw_small(red, [tuple(as_row(a) for a in p) for p in (
        (norm_ffn1, m_norm_ffn1, v_norm_ffn1), (norm_mix, m_norm_mix, v_norm_mix), (norm_ffn2, m_norm_ffn2, v_norm_ffn2),
        (norm_final, m_norm_final, v_norm_final), (a_sink, m_a_sink, v_a_sink))])
    sm[3] = [a.reshape(D) for a in sm[3]]

    def ordered(i):
        return [sm[0][i], upd["wg1"][i], upd["wu1"][i], upd["wd1"][i], sm[1][i], upd["w_in"][i], sm[4][i], upd["w_out"][i], sm[2][i],
                upd["wg2"][i], upd["wu2"][i], upd["wd2"][i], sm[3][i]]

    return (loss.reshape(()), grad_x[None], *ordered(0), *ordered(1), *ordered(2), *ordered(3))
```

```python
import jax
import jax.numpy as jnp
from jax import lax
from jax.experimental import pallas as pl
from jax.experimental.pallas import tpu as pltpu

F32 = jnp.float32
BF16 = jnp.bfloat16

HEAD_DIM = 64
LANES = 128
SUBLANES = 8
A_Q_W, A_KV_W, B_W = 512, 128, 512
A_HALF_WINDOW = 128
B_PATTERNS = ((128, 1), (512, 4), (2048, 16))
ROPE_THETA = 10000.0
NORM_EPS = 1e-6
FFN_RES_WEIGHT = 0.5
ADAM_LR, ADAM_B1, ADAM_B2, ADAM_EPS, ADAM_WD, ADAM_STEP = 0.001, 0.9, 0.999, 1e-08, 0.01, 10
N_CHIPS = 4
N_DEV = 8
QB = 128
SHORT_SEQ = 512
NEG = -1e30
VMEM_LIMIT = 56 * 1024 * 1024
MESH = pl.DeviceIdType.MESH
ANY = pl.BlockSpec(memory_space=pl.ANY)


def _params(sem=None):
    return pltpu.CompilerParams(dimension_semantics=sem, vmem_limit_bytes=VMEM_LIMIT)


def _sds(shape, dtype):
    return jax.ShapeDtypeStruct(tuple(shape), dtype)


def _dot(a, b):
    return jnp.dot(a, b, preferred_element_type=F32)


def _dot_nt(a, b):
    return lax.dot_general(a, b, (((1,), (1,)), ((), ())), preferred_element_type=F32)


def _dot_tn(a, b):
    return lax.dot_general(a, b, (((0,), (0,)), ((), ())), preferred_element_type=F32)


def _rms_stats(x):
    r = lax.rsqrt(jnp.mean(x * x, axis=-1, keepdims=True) + NORM_EPS)
    return x * r, r


def _rms_bwd(dh, x, g):
    xhat, r = _rms_stats(x)
    dxn = dh * g
    dx = r * (dxn - xhat * jnp.mean(dxn * xhat, axis=-1, keepdims=True))
    tm, d = x.shape
    dg = (dh * xhat).reshape(tm // SUBLANES, SUBLANES, d).sum(axis=0)
    return dx, dg


def _sigmoid(x):
    return 1.0 / (1.0 + jnp.exp(-x))


def _swap32(t):
    n = t.shape[-1]
    lane = lax.broadcasted_iota(jnp.int32, t.shape, t.ndim - 1)
    return jnp.where((lane % HEAD_DIM) < HEAD_DIM // 2, pltpu.roll(t, n - HEAD_DIM // 2, axis=t.ndim - 1),
                     pltpu.roll(t, HEAD_DIM // 2, axis=t.ndim - 1))


def _ordered(body, n_in, dep):
    if dep is None:
        return body, [], []

    def ordered(*refs):
        body(*refs[:n_in], *refs[n_in + 1:])

    return ordered, [ANY], [dep]


def _cast_place(me_arr, w, name):
    R, C = w.shape
    tr = R // 2 if (R // 2) % 16 == 0 else R

    def body(me_ref, w_ref, o_ref):
        o_ref[...] = w_ref[...].astype(BF16)

    grid_spec = pltpu.PrefetchScalarGridSpec(
        num_scalar_prefetch=1, grid=(R // tr,), in_specs=[pl.BlockSpec((tr, C), lambda t, me: (t, 0))],
        out_specs=pl.BlockSpec((None, tr, C), lambda t, me: (me[0], t, 0)))
    return pl.pallas_call(body, name=name, grid_spec=grid_spec, out_shape=_sds((N_CHIPS, R, C), BF16),
                          compiler_params=_params(("parallel",)))(me_arr, w)


HBM = pl.BlockSpec(memory_space=pltpu.HBM)
SEM = pl.BlockSpec(memory_space=pltpu.SEMAPHORE)


def _push_start(name, bufs, ncopies, plan, after):
    nb = len(bufs)

    def body(*refs):
        send, recv, token = refs[nb + 1], refs[nb + 2], refs[-1]
        for i, (src, dst, dev) in enumerate(plan(refs[:nb])):
            pltpu.make_async_remote_copy(src_ref=src, dst_ref=dst, send_sem=send.at[i], recv_sem=recv.at[i],
                                         device_id=dev, device_id_type=MESH).start()
        token[...] = jnp.zeros_like(token)

    outs = pl.pallas_call(
        body, name=name,
        out_shape=(pltpu.SemaphoreType.DMA((ncopies,)), pltpu.SemaphoreType.DMA((ncopies,)), *[pltpu.HBM(b.shape, b.dtype) for b in bufs],
                   _sds((SUBLANES, LANES), F32)),
        in_specs=[HBM] * nb + [ANY], out_specs=(SEM, SEM, *([HBM] * nb), pl.BlockSpec(memory_space=pltpu.VMEM)),
        input_output_aliases={i: 2 + i for i in range(nb)},
        compiler_params=pltpu.CompilerParams(has_side_effects=pltpu.SideEffectType.DATAFLOW_SIDE_EFFECTING),
    )(*[pltpu.with_memory_space_constraint(b, pltpu.HBM) for b in bufs], after)
    return outs[0], outs[1], list(outs[2:2 + nb]), outs[-1]


def _push_wait(name, send, recv, bufs, plan, after, first=0):
    nb = len(bufs)

    def body(*refs):
        send_ref, recv_ref = refs[nb], refs[nb + 1]
        for i, (src, dst, dev) in enumerate(plan(refs[:nb])):
            cp = pltpu.make_async_remote_copy(src_ref=src, dst_ref=dst, send_sem=send_ref.at[first + i], recv_sem=recv_ref.at[first + i],
                                              device_id=dev, device_id_type=MESH)
            cp.wait_send()
            cp.wait_recv()

    afters = list(after) if isinstance(after, (list, tuple)) else [after]
    outs = pl.pallas_call(
        body, name=name, out_shape=tuple(pltpu.HBM(b.shape, b.dtype) for b in bufs),
        in_specs=[HBM] * nb + [SEM, SEM] + [ANY] * len(afters), out_specs=tuple([HBM] * nb),
        input_output_aliases={i: i for i in range(nb)},
        compiler_params=pltpu.CompilerParams(has_side_effects=pltpu.SideEffectType.DATAFLOW_SIDE_EFFECTING),
    )(*bufs, send, recv, *afters)
    return list(outs)


def _mesh_pos():
    return lax.axis_index("x"), lax.axis_index("y"), lax.axis_index("c")


def _chip_peers(x, y, c):
    return [((1 - x, y, c), 2 * (1 - x) + y), ((x, 1 - y, c), 2 * x + (1 - y)), ((1 - x, 1 - y, c), 2 * (1 - x) + (1 - y))]


def _gather_plan(n):
    def plan(refs):
        x, y, c = _mesh_pos()
        me = 2 * x + y
        return [(refs[k].at[me], refs[k].at[me], dev) for k in range(n) for dev, _ in _chip_peers(x, y, c)]
    return plan


def _rows_of(shape, who, quarter=None):
    r2 = shape[1] // 2
    if quarter is None:
        return pl.ds(pl.multiple_of(who * r2, 16), r2)
    return pl.ds(pl.multiple_of(who * r2 + quarter * (r2 // 2), 16), r2 // 2)


def _neighbour_plan(shapes):
    def plan(refs):
        x, y, c = _mesh_pos()
        me = 2 * x + y
        return [(refs[k].at[me, _rows_of(shp, c), :], refs[k].at[me, _rows_of(shp, c), :], dev)
                for k, shp in enumerate(shapes) for dev in ((1 - x, y, c), (x, 1 - y, c))]
    return plan


def _gather_forward(fulls):
    n = len(fulls)

    def body(*refs):
        ins, outs = refs[:n], refs[n:2 * n]
        ici_send, ici_recv, d2d_send, d2d_recv = refs[2 * n:]
        x, y, c = _mesh_pos()
        cx, cy, cd = 2 * (1 - x) + y, 2 * x + (1 - y), 2 * (1 - x) + (1 - y)
        sibling, x_nbr, y_nbr = (x, y, 1 - c), (1 - x, y, c), (x, 1 - y, c)
        started = []

        def push(src, dst, send, recv, dev):
            cp = pltpu.make_async_remote_copy(src_ref=src, dst_ref=dst, send_sem=send, recv_sem=recv, device_id=dev, device_id_type=MESH)
            cp.start()
            started.append(cp)

        def arrived(blk, send, recv):
            pltpu.make_async_remote_copy(src_ref=blk, dst_ref=blk, send_sem=send, recv_sem=recv, device_id=sibling,
                                         device_id_type=MESH).wait_recv()

        for k in range(n):
            shp = fulls[k].shape
            for j, chip in enumerate((cx, cy)):
                push(ins[k].at[chip, _rows_of(shp, c), :], outs[k].at[chip, _rows_of(shp, c), :],
                     d2d_send.at[3 * k + j], d2d_recv.at[3 * k + j], sibling)
            push(ins[k].at[cx, _rows_of(shp, c, 0), :], outs[k].at[cx, _rows_of(shp, c, 0), :], ici_send.at[2 * k], ici_recv.at[2 * k], y_nbr)
            push(ins[k].at[cy, _rows_of(shp, c, 1), :], outs[k].at[cy, _rows_of(shp, c, 1), :], ici_send.at[2 * k + 1], ici_recv.at[2 * k + 1],
                 x_nbr)
        for k in range(n):
            shp = fulls[k].shape
            for q in (0, 1):
                arrived(outs[k].at[cd, _rows_of(shp, c, q), :], ici_send.at[2 * k + q], ici_recv.at[2 * k + q])
            blk = outs[k].at[cd, _rows_of(shp, c), :]
            push(blk, blk, d2d_send.at[3 * k + 2], d2d_recv.at[3 * k + 2], sibling)
        for k in range(n):
            for j, chip in enumerate((cx, cy, cd)):
                arrived(outs[k].at[chip, _rows_of(fulls[k].shape, 1 - c), :], d2d_send.at[3 * k + j], d2d_recv.at[3 * k + j])
        for cp in started:
            cp.wait_send()

    return pl.pallas_call(
        body, name="gather_forward", out_shape=[_sds(f.shape, BF16) for f in fulls],
        in_specs=[ANY] * n, out_specs=[ANY] * n, input_output_aliases={k: k for k in range(n)},
        scratch_shapes=[pltpu.SemaphoreType.DMA((n * 2,))] * 2 + [pltpu.SemaphoreType.DMA((n * 3,))] * 2,
        compiler_params=_params())(*fulls)


def _resident(shape):
    return pl.BlockSpec(shape, lambda i: (0,) * len(shape), pipeline_mode=pl.Buffered(1))


FFN_FWD_CHUNK = 256
FFN_DX_CHUNK = 512


def _chunks(n, step):
    return [(c0, min(step, n - c0)) for c0 in range(0, n, step)]


def _two_phase(chunks, first, second):
    held = {}
    for ci, ch in enumerate(chunks):
        held[ci] = first(*ch)
        if ci >= 1:
            second(*chunks[ci - 1], held.pop(ci - 1))
    last = len(chunks) - 1
    second(*chunks[last], held.pop(last))


def _loss_and_grad(x, g, target):
    D = x.shape[1]
    xhat, _ = _rms_stats(x)
    err = xhat * g - target
    loss = 0.5 * jnp.sum(jnp.sum(err * err, axis=-1, keepdims=True) * (1.0 / D), axis=0, keepdims=True)
    dx, dg = _rms_bwd(err * (1.0 / D), x, g)
    return dx, dg, loss


def _ffn_loss(x, g, wgt, wut, wd, gf, target, name, tm=512):
    T, D = x.shape
    F = wd.shape[0]

    def body(x_ref, g_ref, wg_ref, wu_ref, wd_ref, gf_ref, t_ref, dy_ref, h_ref, gate_ref, up_ref, act_ref, dgf_ref, loss_ref):
        @pl.when(pl.program_id(0) == 0)
        def _():
            dgf_ref[...] = jnp.zeros_like(dgf_ref)
            loss_ref[...] = jnp.zeros_like(loss_ref)

        xv = x_ref[...]
        xhat, _ = _rms_stats(xv)
        h = (xhat * g_ref[...]).astype(BF16)
        h_ref[...] = h
        acc = []

        def first(c0, cw):
            return _dot_nt(h, wg_ref[c0:c0 + cw, :]), _dot_nt(h, wu_ref[c0:c0 + cw, :])

        def second(c0, cw, gate_up):
            gate, up = gate_up
            act = ((gate * _sigmoid(gate)) * up).astype(BF16)
            gate_ref[:, c0:c0 + cw] = gate.astype(BF16)
            up_ref[:, c0:c0 + cw] = up.astype(BF16)
            act_ref[:, c0:c0 + cw] = act
            d = _dot(act, wd_ref[c0:c0 + cw, :])
            acc[:] = [d if not acc else acc[0] + d]

        _two_phase(_chunks(F, FFN_FWD_CHUNK), first, second)
        dy_ref[...], dgf, part = _loss_and_grad(xv + FFN_RES_WEIGHT * acc[0], gf_ref[...], t_ref[...])
        dgf_ref[...] += dgf
        loss_ref[...] += part

    row = pl.BlockSpec((tm, D), lambda i: (i, 0))
    gain = pl.BlockSpec((1, D), lambda i: (0, 0))
    saved = pl.BlockSpec((tm, F), lambda i: (i, 0))
    return pl.pallas_call(
        body, name=name, grid=(T // tm,),
        in_specs=[row, gain, _resident(wgt.shape), _resident(wut.shape), _resident(wd.shape), gain, row],
        out_specs=[row, row, saved, saved, saved, pl.BlockSpec((SUBLANES, D), lambda i: (0, 0)), pl.BlockSpec((SUBLANES, LANES), lambda i: (0, 0))],
        out_shape=[_sds((T, D), F32), _sds((T, D), BF16), _sds((T, F), BF16), _sds((T, F), BF16), _sds((T, F), BF16),
                   _sds((SUBLANES, D), F32), _sds((SUBLANES, LANES), F32)],
        compiler_params=_params(("arbitrary",)))(x, g, wgt, wut, wd, gf, target)


def _ffn_up(x, g, wgt, wut, name, tm=512, dep=None):
    T, D = x.shape
    F = wgt.shape[0]

    def body(x_ref, g_ref, wg_ref, wu_ref, h_ref, gate_ref, up_ref, act_ref):
        xhat, _ = _rms_stats(x_ref[...])
        h = (xhat * g_ref[...]).astype(BF16)
        h_ref[...] = h

        def first(c0, cw):
            return _dot_nt(h, wg_ref[c0:c0 + cw, :]), _dot_nt(h, wu_ref[c0:c0 + cw, :])

        def second(c0, cw, gate_up):
            gate, up = gate_up
            gate_ref[:, c0:c0 + cw] = gate.astype(BF16)
            up_ref[:, c0:c0 + cw] = up.astype(BF16)
            act_ref[:, c0:c0 + cw] = ((gate * _sigmoid(gate)) * up).astype(BF16)

        _two_phase(_chunks(F, FFN_FWD_CHUNK), first, second)

    row = pl.BlockSpec((tm, D), lambda i: (i, 0))
    saved = pl.BlockSpec((tm, F), lambda i: (i, 0))
    body, dep_spec, dep_arg = _ordered(body, 4, dep)
    return pl.pallas_call(
        body, name=name, grid=(T // tm,),
        in_specs=[row, pl.BlockSpec((1, D), lambda i: (0, 0)), _resident(wgt.shape), _resident(wut.shape)] + dep_spec,
        out_specs=[row, saved, saved, saved],
        out_shape=[_sds((T, D), BF16), _sds((T, F), BF16), _sds((T, F), BF16), _sds((T, F), BF16)],
        compiler_params=_params(("parallel",)))(x, g, wgt, wut, *dep_arg)


def _ffn_down(x, act, wd, name, tm=512):
    T, D = x.shape
    F = wd.shape[0]

    def body(x_ref, a_ref, wd_ref, xo_ref):
        xo_ref[...] = x_ref[...] + FFN_RES_WEIGHT * _dot(a_ref[...], wd_ref[...])

    row = pl.BlockSpec((tm, D), lambda i: (i, 0))
    return pl.pallas_call(
        body, name=name, grid=(T // tm,), in_specs=[row, pl.BlockSpec((tm, F), lambda i: (i, 0)), _resident(wd.shape)],
        out_specs=row, out_shape=_sds((T, D), F32), compiler_params=_params(("parallel",)))(x, act, wd)


def _ffn_dx(dxo, x, g, gate_s, up_s, wgt, wut, wd, name, tm=256, dep=None):
    T, D = x.shape
    F = wd.shape[0]

    def body(dxo_ref, x_ref, g_ref, gate_ref, up_ref, wg_ref, wu_ref, wd_ref, dx_ref, dff_ref, dgate_ref, dup_ref, dg_ref):
        @pl.when(pl.program_id(0) == 0)
        def _():
            dg_ref[...] = jnp.zeros_like(dg_ref)

        d = (FFN_RES_WEIGHT * dxo_ref[...]).astype(BF16)
        dff_ref[...] = d
        dh = []

        def first(c0, cw):
            return _dot_nt(d, wd_ref[c0:c0 + cw, :])

        def second(c0, cw, da):
            gate = gate_ref[:, c0:c0 + cw].astype(F32)
            up = up_ref[:, c0:c0 + cw].astype(F32)
            s = _sigmoid(gate)
            silu = gate * s
            dup = (da * silu).astype(BF16)
            dgate = (da * up * (s * (1.0 + gate * (1.0 - s)))).astype(BF16)
            dgate_ref[:, c0:c0 + cw] = dgate
            dup_ref[:, c0:c0 + cw] = dup
            t = _dot(dgate, wg_ref[c0:c0 + cw, :]) + _dot(dup, wu_ref[c0:c0 + cw, :])
            dh[:] = [t if not dh else dh[0] + t]

        _two_phase(_chunks(F, FFN_DX_CHUNK), first, second)
        dxn, dg = _rms_bwd(dh[0], x_ref[...], g_ref[...])
        dg_ref[...] += dg
        dx_ref[...] = dxo_ref[...] + dxn

    row = pl.BlockSpec((tm, D), lambda i: (i, 0))
    saved = pl.BlockSpec((tm, F), lambda i: (i, 0))
    body, dep_spec, dep_arg = _ordered(body, 8, dep)
    return pl.pallas_call(
        body, name=name, grid=(T // tm,),
        in_specs=[row, row, pl.BlockSpec((1, D), lambda i: (0, 0)), saved, saved, _resident(wgt.shape), _resident(wut.shape),
                  _resident(wd.shape)] + dep_spec,
        out_specs=[row, row, saved, saved, pl.BlockSpec((SUBLANES, D), lambda i: (0, 0))],
        out_shape=[_sds((T, D), F32), _sds((T, D), BF16), _sds((T, F), BF16), _sds((T, F), BF16), _sds((SUBLANES, D), F32)],
        compiler_params=_params(("arbitrary",)))(dxo, x, g, gate_s, up_s, wgt, wut, wd, *dep_arg)


def _tn(a, b, mb, name, tk=2048, dep=None):
    T, M = a.shape
    N = b.shape[1]
    nt = T // tk

    def body(a_ref, b_ref, o_ref, ob_ref):
        @pl.when(pl.program_id(1) == 0)
        def _():
            o_ref[...] = jnp.zeros_like(o_ref)

        o_ref[...] += _dot_tn(a_ref[...].astype(BF16), b_ref[...].astype(BF16))

        @pl.when(pl.program_id(1) == nt - 1)
        def _():
            ob_ref[...] = o_ref[...].astype(BF16)

    o_spec = pl.BlockSpec((mb, N), lambda g, t: (g, 0))
    body, dep_spec, dep_arg = _ordered(body, 2, dep)
    return pl.pallas_call(
        body, name=name, grid=(M // mb, nt),
        in_specs=[pl.BlockSpec((tk, mb), lambda g, t: (t, g)), pl.BlockSpec((tk, N), lambda g, t: (t, 0))] + dep_spec,
        out_specs=[o_spec, o_spec], out_shape=[_sds((M, N), F32), _sds((M, N), BF16)],
        compiler_params=_params(("parallel", "arbitrary")))(a, b, *dep_arg)


def _rope_tables(pos_col, inv_freq):
    T = pos_col.shape[0]

    def body(p_ref, f_ref, c_ref, s_ref):
        ang = p_ref[...].astype(F32) * f_ref[...]
        lane = lax.broadcasted_iota(jnp.int32, ang.shape, 1)
        c_ref[...] = jnp.cos(ang)
        sn = jnp.sin(ang)
        s_ref[...] = jnp.where((lane % HEAD_DIM) < HEAD_DIM // 2, -sn, sn)

    tm = 1024
    return pl.pallas_call(
        body, name="rope_tables", grid=(T // tm,),
        in_specs=[pl.BlockSpec((tm, 1), lambda i: (i, 0)), pl.BlockSpec((1, LANES), lambda i: (0, 0))],
        out_specs=[pl.BlockSpec((tm, LANES), lambda i: (i, 0))] * 2,
        out_shape=[_sds((T, LANES), F32)] * 2, compiler_params=_params(("parallel",)))(pos_col, inv_freq)


def _deinterleave(scr, out_ref, d, tm, nblk):
    for r in range(d):
        for cb in range(nblk):
            out_ref[r, :, cb * LANES:(cb + 1) * LANES] = scr[cb, pl.ds(r, tm // d, stride=d), :].astype(out_ref.dtype)


def _interleave(in_ref, scr, d, tm, nblk):
    for r in range(d):
        for cb in range(nblk):
            scr[cb, pl.ds(r, tm // d, stride=d), :] = in_ref[r, :, cb * LANES:(cb + 1) * LANES].astype(F32)


def _proj_rope(x, g, w_in, cos, sin, tm=512):
    T, D = x.shape
    dils = [d for _, d in B_PATTERNS if d > 1]
    nbb = B_W // LANES
    scale = HEAD_DIM ** -0.5
    cuts = [0, A_Q_W, A_Q_W + A_KV_W, A_Q_W + 2 * A_KV_W, A_Q_W + 2 * A_KV_W + B_W, A_Q_W + 2 * A_KV_W + 2 * B_W,
            A_Q_W + 2 * A_KV_W + 3 * B_W]

    def body(x_ref, g_ref, w_ref, c_ref, s_ref, h_ref, aq_ref, ak_ref, av_ref, *rest):
        b_refs, scr = rest[:-1], rest[-1]
        xhat, _ = _rms_stats(x_ref[...])
        h = (xhat * g_ref[...]).astype(BF16)
        h_ref[...] = h
        cs, sn = c_ref[...], s_ref[...]

        def project(idx, ref, rope, mult, which):
            return _dot_nt(h, w_ref[cuts[idx]:cuts[idx + 1], :])

        def finish(idx, ref, rope, mult, which, whole):
            for cb in range((cuts[idx + 1] - cuts[idx]) // LANES):
                p = whole[:, cb * LANES:(cb + 1) * LANES]
                if rope:
                    p = p * cs + _swap32(p) * sn
                if mult != 1.0:
                    p = p * mult
                ref[:, cb * LANES:(cb + 1) * LANES] = p.astype(BF16)
                if which is not None:
                    scr[which, cb] = p
            if which is not None:
                for di, d in enumerate(dils):
                    _deinterleave(scr.at[which], b_refs[3 * (di + 1) + which], d, tm, nbb)

        _two_phase([(0, aq_ref, True, scale, None), (1, ak_ref, True, 1.0, None), (2, av_ref, False, 1.0, None),
                    (3, b_refs[0], True, scale, 0), (4, b_refs[1], True, 1.0, 1), (5, b_refs[2], False, 1.0, 2)], project, finish)

    row = lambda w: pl.BlockSpec((tm, w), lambda i: (i, 0))
    out_specs = [row(D), row(A_Q_W), row(A_KV_W), row(A_KV_W)] + [row(B_W)] * 3
    out_shape = [_sds((T, D), BF16), _sds((T, A_Q_W), BF16), _sds((T, A_KV_W), BF16), _sds((T, A_KV_W), BF16)] + [_sds((T, B_W), BF16)] * 3
    for d in dils:
        out_specs += [pl.BlockSpec((d, tm // d, B_W), lambda i: (0, i, 0))] * 3
        out_shape += [_sds((d, T // d, B_W), BF16)] * 3
    return pl.pallas_call(
        body, name="proj_rope", grid=(T // tm,),
        in_specs=[row(D), pl.BlockSpec((1, D), lambda i: (0, 0)), pl.BlockSpec(w_in.shape, lambda i: (0, 0)), row(LANES), row(LANES)],
        out_specs=out_specs, out_shape=out_shape, scratch_shapes=[pltpu.VMEM((3, nbb, tm, LANES), F32)],
        compiler_params=_params(("parallel",)))(x, g, w_in, cos, sin)


def _band_bias(rel, qb, kw, hw):
    ri = lax.broadcasted_iota(jnp.int32, (2 * qb, kw), 0) & (qb - 1)
    ci = lax.broadcasted_iota(jnp.int32, (2 * qb, kw), 1)
    return jnp.where(jnp.abs(ri + rel - ci) <= hw, 0.0, NEG).astype(F32)


def _stack_heads(x, lo):
    z = jnp.zeros_like(x)
    return jnp.concatenate([jnp.where(lo, x, z), jnp.where(lo, z, x)], axis=0)


def _unstack_heads(y, lo):
    qb = y.shape[0] // 2
    return jnp.where(lo, y[:qb], y[qb:])


def _band_setup(bias_scr, qb, kw, hw):
    if bias_scr is not None:
        for i in range(3):
            bias_scr[i] = _band_bias(i * hw, qb, kw, hw)


def _band_window(bias_scr, qs, L, qb, kw, hw):
    ws = pl.multiple_of(jnp.clip(qs - hw, 0, L - kw), 64)
    if bias_scr is None:
        return ws, _band_bias(qs - ws, qb, kw, hw)
    return ws, bias_scr[lax.shift_right_logical(qs - ws, hw.bit_length() - 1)]


def _dup_kv_head(src_ref, dst_ref, head, L):
    step = min(L, 1024)
    for r0 in range(0, L, step):
        xf = src_ref[r0:r0 + step, :].astype(F32)
        lane = lax.broadcasted_iota(jnp.int32, xf.shape, 1)
        keep = jnp.logical_xor(lane < HEAD_DIM, head == 1)
        dst_ref[r0:r0 + step, :] = jnp.where(keep, xf, pltpu.roll(xf, HEAD_DIM, axis=1)).astype(dst_ref.dtype)


def _attn_fwd(q, k, v, sink, hw, gqa, out_dtype, name, qb=QB, blocks_per_step=8, out_cols=None):
    NB, L, Cq = q.shape
    Ls = min(L, 2048)
    kw = min(qb + 2 * hw, L)
    tables = L >= qb + 2 * hw
    unroll = min(blocks_per_step, Ls // qb)
    nlb = 1 if (gqa or L > SHORT_SEQ) else Cq // LANES

    def body(sink_ref, q_ref, k_ref, v_ref, o_ref, lse_ref, *scr):
        b, s_idx = pl.program_id(1), pl.program_id(2)
        bias_scr = scr[0] if tables else None
        _band_setup(bias_scr, qb, kw, hw)
        if gqa:
            kd, vd = scr[-2:]

            @pl.when(s_idx == 0)
            def _():
                _dup_kv_head(k_ref, kd, b // 2, L)
                _dup_kv_head(v_ref, vd, b // 2, L)
        else:
            kd, vd = k_ref, v_ref
        lane = lax.broadcasted_iota(jnp.int32, (qb, LANES), 1)
        lo = lane < HEAD_DIM
        if gqa:
            row = lax.broadcasted_iota(jnp.int32, (2 * qb, 1), 0)
            sk = jnp.where(row < qb, sink_ref[2 * b], sink_ref[2 * b + 1])

        def block(ql, col):
            qs = s_idx * Ls + ql
            ws, bias = _band_window(bias_scr, qs, L, qb, kw, hw)
            return ws, _dot_nt(_stack_heads(q_ref[pl.ds(ql, qb), col], lo), kd[pl.ds(ws, kw), col]) + bias

        def finish(ql, col, scores):
            ws, s = scores
            m = jnp.max(s, axis=-1, keepdims=True)
            if gqa:
                m = jnp.maximum(m, sk)
            p = jnp.exp(s - m)
            den = jnp.sum(p, axis=-1, keepdims=True)
            if gqa:
                den = den + jnp.exp(sk - m)
            o = _dot(p.astype(BF16), vd[pl.ds(ws, kw), col]) * (1.0 / den)
            o_ref[pl.ds(ql, qb), col] = _unstack_heads(o, lo).astype(o_ref.dtype)
            lse_ref[pl.ds(ql, qb), col] = _unstack_heads(m + jnp.log(den), lo)

        for lb in range(nlb):
            def step(n, carry, col=slice(lb * LANES, (lb + 1) * LANES)):
                _two_phase([(pl.multiple_of((n * unroll + u) * qb, qb), col) for u in range(unroll)], block, finish)
                return carry

            lax.fori_loop(0, Ls // (qb * unroll), step, 0)

    kv_map = (lambda r, b, s: (r, 0, 0)) if gqa else (lambda r, b, s: (r, 0, b))
    seg = pl.BlockSpec((None, Ls, nlb * LANES), lambda r, b, s: (r, s, b))
    return pl.pallas_call(
        body, name=name, grid=(NB, Cq // (nlb * LANES), L // Ls),
        in_specs=[pl.BlockSpec(memory_space=pltpu.SMEM), seg, pl.BlockSpec((None, L, nlb * LANES), kv_map),
                  pl.BlockSpec((None, L, nlb * LANES), kv_map)],
        out_specs=[seg, seg], out_shape=[_sds((NB, L, out_cols or Cq), out_dtype), _sds((NB, L, Cq), F32)],
        scratch_shapes=([pltpu.VMEM((3, 2 * qb, kw), F32)] if tables else []) + ([pltpu.VMEM((L, LANES), BF16)] * 2 if gqa else []),
        compiler_params=_params(("parallel", "parallel", "arbitrary")))(sink, q, k, v)


def _attn_bwd(q, k, v, do, lse, delta, sink, hw, gqa, name, qb=QB, blocks_per_step=8):
    NB, L, Cq = q.shape
    Ck = k.shape[2]
    Ls = min(L, 2048)
    kw = min(qb + 2 * hw, L)
    reps = kw // LANES
    nseg = L // Ls
    scale = HEAD_DIM ** -0.5
    tables = L >= qb + 2 * hw
    unroll = min(blocks_per_step, Ls // qb)
    nlb = 1 if (gqa or L > SHORT_SEQ) else Cq // LANES

    def body(sink_ref, q_ref, do_ref, lse_ref, dl_ref, k_ref, v_ref, dq_ref, dk_ref, dv_ref, dsk_ref, *scr):
        b, s_idx = pl.program_id(1), pl.program_id(2)
        lane = lax.broadcasted_iota(jnp.int32, (qb, LANES), 1)
        lo = lane < HEAD_DIM
        bias_scr = scr[0] if tables else None
        _band_setup(bias_scr, qb, kw, hw)
        if gqa:
            kd, vd, dk_acc, dv_acc, dsk_acc = scr[-5:]

            @pl.when(s_idx == 0)
            def _():
                _dup_kv_head(k_ref, kd, b // 2, L)
                _dup_kv_head(v_ref, vd, b // 2, L)
                dk_acc[...] = jnp.zeros_like(dk_acc)
                dv_acc[...] = jnp.zeros_like(dv_acc)
                dsk_acc[...] = jnp.zeros_like(dsk_acc)

            @pl.when((s_idx == 0) & (b == 0))
            def _():
                dk_ref[...] = jnp.zeros_like(dk_ref)
                dv_ref[...] = jnp.zeros_like(dv_ref)
        else:
            kd, vd = k_ref, v_ref
            dk_acc, dv_acc = scr[-2:]

            @pl.when(s_idx == 0)
            def _():
                dk_acc[...] = jnp.zeros_like(dk_acc)
                dv_acc[...] = jnp.zeros_like(dv_acc)

        def block(ql, col):
            qs = s_idx * Ls + ql
            ws, bias = _band_window(bias_scr, qs, L, qb, kw, hw)
            qv, dov = q_ref[pl.ds(ql, qb), col], do_ref[pl.ds(ql, qb), col]
            lse, dl = lse_ref[pl.ds(ql, qb), col], dl_ref[pl.ds(ql, qb), col]
            kv_, vv = kd[pl.ds(ws, kw), col], vd[pl.ds(ws, kw), col]
            q2, do2 = _stack_heads(qv, lo), _stack_heads(dov, lo)
            return ws, q2, do2, lse, dl, _dot_nt(q2, kv_) + bias, _dot_nt(do2, vv)

        def finish(ql, col, held):
            ws, q2, do2, lse, dl, s, dp = held
            lse_sw, dl_sw = pltpu.roll(lse, HEAD_DIM, axis=1), pltpu.roll(dl, HEAD_DIM, axis=1)
            lse2 = jnp.concatenate([jnp.where(lo, lse, lse_sw), jnp.where(lo, lse_sw, lse)], axis=0)
            dl2 = jnp.concatenate([jnp.where(lo, dl, dl_sw), jnp.where(lo, dl_sw, dl)], axis=0)
            p = jnp.exp(s - jnp.tile(lse2, (1, reps)))
            ds = (p * (dp - jnp.tile(dl2, (1, reps)))).astype(BF16)
            dq_ref[pl.ds(ql, qb), col] = (_unstack_heads(_dot(ds, kd[pl.ds(ws, kw), col]), lo) * scale).astype(dq_ref.dtype)
            both = _dot_tn(jnp.concatenate([ds, p.astype(BF16)], axis=1), jnp.concatenate([q2, do2], axis=1))
            dk_acc[pl.ds(ws, kw), col] += both[:kw, :LANES]
            dv_acc[pl.ds(ws, kw), col] += both[kw:, LANES:]
            if gqa:
                sk = jnp.where(lo, sink_ref[2 * b], sink_ref[2 * b + 1])
                dsk_acc[...] += -jnp.exp(sk - lse) * dl

        for lb in range(nlb):
            def step(n, carry, col=slice(lb * LANES, (lb + 1) * LANES)):
                _two_phase([(pl.multiple_of((n * unroll + u) * qb, qb), col) for u in range(unroll)], block, finish)
                return carry

            lax.fori_loop(0, Ls // (qb * unroll), step, 0)

        if gqa:
            @pl.when(s_idx == nseg - 1)
            def _():
                step_rows = min(L, 1024)
                for r0 in range(0, L, step_rows):
                    lanek = lax.broadcasted_iota(jnp.int32, (step_rows, LANES), 1)
                    mine = jnp.logical_xor(lanek < HEAD_DIM, (b // 2) == 1)
                    for acc, ref in ((dk_acc, dk_ref), (dv_acc, dv_ref)):
                        a = acc[r0:r0 + step_rows, :]
                        ref[r0:r0 + step_rows, :] += jnp.where(mine, a + pltpu.roll(a, HEAD_DIM, axis=1), 0.0)
                dsk_ref[...] = dsk_acc[...].reshape(qb // SUBLANES, SUBLANES, LANES).sum(axis=0)
        else:
            dsk_ref[...] = jnp.zeros_like(dsk_ref)

            @pl.when(s_idx == nseg - 1)
            def _():
                dk_ref[...] = dk_acc[...].astype(dk_ref.dtype)
                dv_ref[...] = dv_acc[...].astype(dv_ref.dtype)

    kv_map = (lambda r, b, s: (r, 0, 0)) if gqa else (lambda r, b, s: (r, 0, b))
    seg = pl.BlockSpec((None, Ls, nlb * LANES), lambda r, b, s: (r, s, b))
    full = pl.BlockSpec((None, L, nlb * LANES), kv_map)
    scratch = [pltpu.VMEM((3, 2 * qb, kw), F32)] if tables else []
    if gqa:
        scratch += [pltpu.VMEM((L, LANES), BF16)] * 2 + [pltpu.VMEM((L, LANES), F32)] * 2 + [pltpu.VMEM((qb, LANES), F32)]
    else:
        scratch += [pltpu.VMEM((L, nlb * LANES), F32)] * 2
    kv_dtype = F32 if gqa else BF16
    return pl.pallas_call(
        body, name=name, grid=(NB, Cq // (nlb * LANES), nseg),
        in_specs=[pl.BlockSpec(memory_space=pltpu.SMEM), seg, seg, seg, seg, full, full],
        out_specs=[seg, full, full, pl.BlockSpec((None, None, SUBLANES, LANES), lambda r, b, s: (r, b, 0, 0))],
        out_shape=[_sds((NB, L, Cq), BF16), _sds((NB, L, Ck), kv_dtype), _sds((NB, L, Ck), kv_dtype),
                   _sds((NB, Cq // LANES, SUBLANES, LANES), F32)],
        scratch_shapes=scratch,
        compiler_params=_params(("arbitrary", "arbitrary", "arbitrary")))(sink, q, do, lse, delta, k, v)


def _dilated_fwd(cat, qkv, hw, tile=2048):
    T = cat.shape[0]
    dils = sorted(qkv)
    nbb, na = B_W // LANES, A_Q_W // LANES
    qb, kw = QB, QB + 2 * hw
    rows_merge = 256
    assert T % tile == 0 and all(tile % (d * qb) == 0 and T // d >= kw for d in dils)

    def body(cat_in, *refs):
        qkv_refs = {d: refs[3 * j:3 * j + 3] for j, d in enumerate(dils)}
        cat_ref, lg_refs = refs[3 * len(dils)], refs[3 * len(dils) + 1:4 * len(dils) + 1]
        o_scr, l_scr, bias_scr = refs[4 * len(dils) + 1:]
        i = pl.program_id(1)
        _band_setup(bias_scr, qb, kw, hw)
        lane = lax.broadcasted_iota(jnp.int32, (qb, LANES), 1)
        lo = lane < HEAD_DIM
        for pi, d in enumerate(dils):
            q_ref, k_ref, v_ref = qkv_refs[d]
            L, rows = T // d, tile // d

            def place(r, n, d=d):
                return pl.ds(r + d * n * qb, qb, stride=d) if d > 1 else pl.ds(n * qb, qb)

            def scores(r, n, q_ref=q_ref, k_ref=k_ref, L=L, rows=rows):
                ws, bias = _band_window(bias_scr, i * rows + n * qb, L, qb, kw, hw)
                return ws, _dot_nt(_stack_heads(q_ref[r, n * qb:(n + 1) * qb, :], lo), k_ref[r, pl.ds(ws, kw), :]) + bias

            def finish(r, n, held, v_ref=v_ref, pi=pi, place=place):
                ws, s = held
                m = jnp.max(s, axis=-1, keepdims=True)
                p = jnp.exp(s - m)
                den = jnp.sum(p, axis=-1, keepdims=True)
                o = _dot(p.astype(BF16), v_ref[r, pl.ds(ws, kw), :]) * (1.0 / den)
                o_scr[pi, place(r, n), :] = _unstack_heads(o, lo)
                l_scr[pi, place(r, n), :] = _unstack_heads(m + jnp.log(den), lo)

            blocks = [(r, n) for r in range(d) for n in range(rows // qb)]
            for g0 in range(0, len(blocks), 8):
                _two_phase(blocks[g0:g0 + 8], scores, finish)

        for r0 in range(0, tile, rows_merge):
            rs = slice(r0, r0 + rows_merge)
            ls_ = [l_scr[pi, rs, :] for pi in range(len(dils))]
            m = ls_[0]
            for l in ls_[1:]:
                m = jnp.maximum(m, l)
            es = [jnp.exp(l - m) for l in ls_]
            den, out = es[0], es[0] * o_scr[0, rs, :]
            for pi in range(1, len(dils)):
                den = den + es[pi]
                out = out + es[pi] * o_scr[pi, rs, :]
            cat_ref[rs, :] = (out * (1.0 / den)).astype(BF16)
            l_scr[0, rs, :] = m + jnp.log(den)
        for lg_ref, d in zip(lg_refs, dils):
            for r in range(d):
                lg_ref[r] = l_scr[0, pl.ds(r, tile // d, stride=d), :] if d > 1 else l_scr[0]

    in_specs = [pl.BlockSpec(memory_space=pl.ANY)]
    operands = [cat]
    for d in dils:
        in_specs += [pl.BlockSpec((d, tile // d, LANES), lambda b, i: (0, i, b))] + [pl.BlockSpec((d, T // d, LANES), lambda b, i: (0, 0, b))] * 2
        operands += list(qkv[d])
    return pl.pallas_call(
        body, name="dilated_fwd", grid=(nbb, T // tile), in_specs=in_specs,
        out_specs=[pl.BlockSpec((tile, LANES), lambda b, i: (i, na + b))] + [pl.BlockSpec((d, tile // d, LANES), lambda b, i: (0, i, b)) for d in dils],
        out_shape=[_sds(cat.shape, BF16)] + [_sds((d, T // d, B_W), F32) for d in dils],
        input_output_aliases={0: 0},
        scratch_shapes=[pltpu.VMEM((len(dils), tile, LANES), F32)] * 2 + [pltpu.VMEM((3, 2 * qb, kw), F32)],
        compiler_params=_params(("parallel", "arbitrary")))(*operands)


def _out_proj(x, cat, w_out, tm=512):
    T, D = x.shape

    def body(x_ref, c_ref, w_ref, o_ref):
        o_ref[...] = x_ref[...] + _dot(c_ref[...], w_ref[...])

    row = lambda w: pl.BlockSpec((tm, w), lambda i: (i, 0))
    return pl.pallas_call(
        body, name="out_proj", grid=(T // tm,), in_specs=[row(D), row(cat.shape[1]), pl.BlockSpec(w_out.shape, lambda i: (0, 0))],
        out_specs=row(D), out_shape=_sds((T, D), F32), compiler_params=_params(("parallel",)))(x, cat, w_out)


def _dcat(dx, w_out, cat, tm=512, dep=None):
    T, D = dx.shape
    C = cat.shape[1]
    nba, nbb = A_Q_W // LANES, B_W // LANES
    nt = T // tm

    def body(dx_ref, w_ref, cat_ref, doa_ref, dla_ref, dob1_ref, dlb1_ref, dob4_ref, dlb4_ref, dob16_ref, dlb16_ref, dw_ref, dwb_ref,
             sdo, sdl):
        @pl.when(pl.program_id(0) == 0)
        def _():
            dw_ref[...] = jnp.zeros_like(dw_ref)

        dxb = dx_ref[...].astype(BF16)
        dc = _dot_nt(dxb, w_ref[...])
        dw_ref[...] += _dot_tn(cat_ref[...], dxb)

        @pl.when(pl.program_id(0) == nt - 1)
        def _():
            dwb_ref[...] = dw_ref[...].astype(BF16)

        ri =lax.broadcasted_iota(jnp.int32, (LANES, LANES), 0)
        ci = lax.broadcasted_iota(jnp.int32, (LANES, LANES), 1)
        same_head = ((ri // HEAD_DIM) == (ci // HEAD_DIM)).astype(BF16)
        for cb in range(C // LANES):
            cols = slice(cb * LANES, (cb + 1) * LANES)
            blk = dc[:, cols]
            prod = blk * cat_ref[:, cols].astype(F32)
            hi = prod.astype(BF16)
            lo_ = (prod - hi.astype(F32)).astype(BF16)
            dl = _dot(hi, same_head) + _dot(lo_, same_head)
            if cb < nba:
                doa_ref[:, cols] = blk.astype(BF16)
                dla_ref[:, cols] = dl
            else:
                bcols = slice((cb - nba) * LANES, (cb - nba + 1) * LANES)
                dob1_ref[:, bcols] = blk.astype(BF16)
                dlb1_ref[:, bcols] = dl
                sdo[cb - nba] = blk
                sdl[cb - nba] = dl
        _deinterleave(sdo, dob4_ref, 4, tm, nbb)
        _deinterleave(sdl, dlb4_ref, 4, tm, nbb)
        _deinterleave(sdo, dob16_ref, 16, tm, nbb)
        _deinterleave(sdl, dlb16_ref, 16, tm, nbb)

    row = lambda w: pl.BlockSpec((tm, w), lambda i: (i, 0))
    perm = lambda d: pl.BlockSpec((d, tm // d, B_W), lambda i: (0, i, 0))
    whole = pl.BlockSpec((C, D), lambda i: (0, 0))
    body, dep_spec, dep_arg = _ordered(body, 3, dep)
    outs = pl.pallas_call(
        body, name="dcat", grid=(nt,), in_specs=[row(D), whole, row(C)] + dep_spec,
        out_specs=[row(A_Q_W), row(A_Q_W), row(B_W), row(B_W), perm(4), perm(4), perm(16), perm(16), whole, whole],
        out_shape=[_sds((T, A_Q_W), BF16), _sds((T, A_Q_W), F32), _sds((T, B_W), BF16), _sds((T, B_W), F32),
                   _sds((4, T // 4, B_W), BF16), _sds((4, T // 4, B_W), F32), _sds((16, T // 16, B_W), BF16), _sds((16, T // 16, B_W), F32),
                   _sds((C, D), F32), _sds((C, D), BF16)],
        scratch_shapes=[pltpu.VMEM((nbb, tm, LANES), F32)] * 2, compiler_params=_params(("arbitrary",)))(dx, w_out, cat, *dep_arg)
    return (*outs[:8], (outs[8], outs[9]))


def _mixer_in_bwd(dqa, dka, dva, b1, b4, b16, cos, sin, w_in, x, g, dres, tm=512):
    T, D = x.shape
    nbb = B_W // LANES
    width = A_Q_W + 2 * A_KV_W + 3 * B_W

    def body(dqa_ref, dka_ref, dva_ref, q1, k1, v1, q4, k4, v4, q16, k16, v16, c_ref, s_ref, w_ref, x_ref, g_ref, dr_ref,
             o_ref, dx_ref, dg_ref, scr):
        @pl.when(pl.program_id(0) == 0)
        def _():
            dg_ref[...] = jnp.zeros_like(dg_ref)

        cs, sn = c_ref[...], s_ref[...]
        dh = []

        def unrope(t):
            return t * cs + _swap32(t * sn)

        def project(c0, c1):
            t = _dot(o_ref[:, c0:c1], w_ref[c0:c1, :])
            dh[:] = [t if not dh else dh[0] + t]

        col = 0
        for ref, rope in ((dqa_ref, True), (dka_ref, True), (dva_ref, False)):
            for cb in range(ref.shape[1] // LANES):
                t = ref[:, cb * LANES:(cb + 1) * LANES].astype(F32)
                o_ref[:, col:col + LANES] = (unrope(t) if rope else t).astype(BF16)
                col += LANES
        project(0, col)
        for which, (r1, r4, r16, rope) in enumerate(((q1, q4, q16, True), (k1, k4, k16, True), (v1, v4, v16, False))):
            _interleave(r4, scr.at[0], 4, tm, nbb)
            _interleave(r16, scr.at[1], 16, tm, nbb)
            for cb in range(nbb):
                t = r1[:, cb * LANES:(cb + 1) * LANES].astype(F32) + scr[0, cb] + scr[1, cb]
                o_ref[:, col:col + LANES] = (unrope(t) if rope else t).astype(BF16)
                col += LANES
            project(col - B_W, col)
        dxn, dg = _rms_bwd(dh[0], x_ref[...], g_ref[...])
        dg_ref[...] += dg
        dx_ref[...] = dr_ref[...] + dxn

    row = lambda w: pl.BlockSpec((tm, w), lambda i: (i, 0))
    perm = lambda d: pl.BlockSpec((d, tm // d, B_W), lambda i: (0, i, 0))
    return pl.pallas_call(
        body, name="mixer_in_bwd", grid=(T // tm,),
        in_specs=[row(A_Q_W), row(A_KV_W), row(A_KV_W)] + [row(B_W)] * 3 + [perm(4)] * 3 + [perm(16)] * 3 + [row(LANES), row(LANES)]
        + [_resident(w_in.shape), row(D), pl.BlockSpec((1, D), lambda i: (0, 0)), row(D)],
        out_specs=[row(width), row(D), pl.BlockSpec((SUBLANES, D), lambda i: (0, 0))],
        out_shape=[_sds((T, width), BF16), _sds((T, D), F32), _sds((SUBLANES, D), F32)],
        scratch_shapes=[pltpu.VMEM((2, nbb, tm, LANES), F32)],
        compiler_params=_params(("arbitrary",)))(dqa, dka, dva, *b1, *b4, *b16, cos, sin, w_in, x, g, dres)


def _grad_push_plan(n):
    def plan(refs):
        x, y, c = _mesh_pos()
        return [(refs[k].at[chip], refs[n + k].at[rel], dev) for k in range(n) for rel, (dev, chip) in enumerate(_chip_peers(x, y, c))]
    return plan


def _sum_own(me_arr, g, landed, name):
    ns, R, C = g.shape
    tr = R // 2 if (R // 2) % 16 == 0 else R

    def body(me_ref, g_ref, x_ref, o_ref):
        acc = g_ref[...]
        for rel in range(ns - 1):
            acc = acc + x_ref[rel].astype(F32)
        o_ref[...] = acc

    grid_spec = pltpu.PrefetchScalarGridSpec(
        num_scalar_prefetch=1, grid=(R // tr,),
        in_specs=[pl.BlockSpec((None, tr, C), lambda t, me: (me[0], t, 0)), pl.BlockSpec((ns - 1, tr, C), lambda t, me: (0, t, 0))],
        out_specs=pl.BlockSpec((tr, C), lambda t, me: (t, 0)))
    return pl.pallas_call(body, name=name, grid_spec=grid_spec, out_shape=_sds((R, C), F32),
                          compiler_params=_params(("parallel",)))(me_arr, g, landed)


def _swap_plan(n):
    def plan(refs):
        x, y, c = _mesh_pos()
        return [(refs[k], refs[n + k], (x, y, 1 - c)) for k in range(n)]
    return plan


def _allreduce_small(v, dep):
    rows, W = v.shape

    def body(v_ref, o_ref, buf, send, recv):
        x, y, c = _mesh_pos()
        me = 4 * x + 2 * y + c
        cps = []
        for m in range(1, N_DEV):
            dev = (x ^ (m >> 2), y ^ ((m >> 1) & 1), c ^ (m & 1))
            cp = pltpu.make_async_remote_copy(src_ref=v_ref, dst_ref=buf.at[me], send_sem=send.at[m - 1], recv_sem=recv.at[m - 1],
                                              device_id=dev, device_id_type=MESH)
            cp.start()
            cps.append(cp)
        for m in range(1, N_DEV):
            pltpu.make_async_remote_copy(src_ref=v_ref, dst_ref=buf.at[me ^ m], send_sem=send.at[m - 1], recv_sem=recv.at[m - 1],
                                         device_id=(x, y, c), device_id_type=MESH).wait_recv()
        for cp in cps:
            cp.wait_send()
        buf[me] = v_ref[...]
        acc = buf[0]
        for i in range(1, N_DEV):
            acc = acc + buf[i]
        o_ref[...] = acc

    body, dep_spec, dep_arg = _ordered(body, 1, dep)
    return pl.pallas_call(
        body, name="allreduce_small", out_shape=_sds((rows, W), F32), in_specs=[pl.BlockSpec(memory_space=pltpu.VMEM)] + dep_spec,
        scratch_shapes=[pltpu.VMEM((N_DEV, rows, W), F32), pltpu.SemaphoreType.DMA((N_DEV - 1,)), pltpu.SemaphoreType.DMA((N_DEV - 1,))],
        compiler_params=_params())(v, *dep_arg)


def _adamw_math(w, g, m, v):
    c1 = 1.0 / (1.0 - ADAM_B1 ** ADAM_STEP)
    c2 = 1.0 / (1.0 - ADAM_B2 ** ADAM_STEP)
    nm = ADAM_B1 * m + (1.0 - ADAM_B1) * g
    nv = ADAM_B2 * v + (1.0 - ADAM_B2) * (g * g)
    return -ADAM_LR * ((nm * c1) / (jnp.sqrt(nv * c2) + ADAM_EPS) + ADAM_WD * w), nm, nv


def _adamw_small(rows, params):
    n = len(params)
    n_sink = params[-1][0].shape[1]

    def body(rows_ref, *refs):
        ins, outs = refs[:3 * n], refs[3 * n:]
        for j in range(n):
            g = rows_ref[j:j + 1, 0:n_sink] if j == n - 1 else rows_ref[j:j + 1, :]
            d, nm, nv = _adamw_math(ins[3 * j][...], g, ins[3 * j + 1][...], ins[3 * j + 2][...])
            for ref, val in zip(outs[4 * j:4 * j + 4], (g, d, nm, nv)):
                ref[...] = val
        outs[-1][...] = rows_ref[n - 1:n, n_sink:n_sink + 1]

    flat = [a for p in params for a in p]
    outs = pl.pallas_call(body, name="adamw_small", out_shape=[_sds(p[0].shape, F32) for p in params for _ in range(4)] + [_sds((1, 1), F32)],
                          compiler_params=_params())(rows, *flat)
    return [outs[4 * j:4 * j + 4] for j in range(n)], outs[-1]


def _adamw(w, gp, gq, m, v, name):
    R, C = w.shape
    tr = R // 2 if (R // 2) % SUBLANES == 0 else R

    def body(w_ref, gp_ref, gq_ref, m_ref, v_ref, g_ref, d_ref, nm_ref, nv_ref):
        gv = gp_ref[...] + gq_ref[...]
        g_ref[...] = gv
        d_ref[...], nm_ref[...], nv_ref[...] = _adamw_math(w_ref[...], gv, m_ref[...], v_ref[...])

    blk = pl.BlockSpec((tr, C), lambda t: (t, 0))
    return pl.pallas_call(body, name=name, grid=(R // tr,), in_specs=[blk] * 5, out_specs=[blk] * 4,
                          out_shape=[_sds((R, C), F32)] * 4, compiler_params=_params(("parallel",)))(w, gp, gq, m, v)


def _rope(positions, after):
    inv_freq = 1.0 / (ROPE_THETA ** (jnp.arange(0, HEAD_DIM, 2, dtype=F32) / HEAD_DIM))
    inv_freq = jnp.tile(inv_freq, LANES // (HEAD_DIM // 2)).reshape(1, LANES) + after[0, 0]
    return _rope_tables(positions.reshape(-1, 1), inv_freq)


def _local_step(x, rope, target, norms, a_sink, comm):
    T, D = x.shape
    g1, gm, g2, gf = norms
    cos, sin = rope
    no_sink = jnp.zeros((2 * (B_W // LANES),), F32)
    W = {k: comm.weight(k, x) for k in ("wg1", "wu1")}

    h1, gate1, up1, act1 = _ffn_up(x, g1, W["wg1"], W["wu1"], "ffn1_up", dep=comm.dep())
    W["wd1"] = comm.weight("wd1", act1)
    x1 = _ffn_down(x, act1, W["wd1"], "ffn1_down")
    W["w_in"] = comm.weight("w_in", x1)
    (h2, aq, ak, av, bq1, bk1, bv1, bq4, bk4, bv4, bq16, bk16, bv16) = _proj_rope(x1, gm, W["w_in"], cos, sin)
    cat, a_lse = _attn_fwd(aq[None], ak[None], av[None], a_sink, A_HALF_WINDOW, True, BF16, "attn_a_fwd", qb=2 * QB, blocks_per_step=4,
                           out_cols=A_Q_W + B_W)
    bqs = {1: (bq1[None], bk1[None], bv1[None]), 4: (bq4, bk4, bv4), 16: (bq16, bk16, bv16)}
    (b_hw,) = {w // (2 * d) for w, d in B_PATTERNS}
    cat, lg1, lg4, lg16 = _dilated_fwd(cat[0], bqs, b_hw)
    lg1 = lg1[0]
    W["w_out"] = comm.weight("w_out", cat)
    x2 = _out_proj(x1, cat, W["w_out"])
    for k in ("wg2", "wu2", "wd2"):
        W[k] = comm.weight(k, x2)
    dx3, h3, gate2, up2, act2, dgf, loss8 = _ffn_loss(x2, g2, W["wg2"], W["wu2"], W["wd2"], gf, target, "ffn2_fwd")

    dx2, dff2, dgate2, dup2, dg2 = _ffn_dx(dx3, x2, g2, gate2, up2, W["wg2"], W["wu2"], W["wd2"], "ffn2_dx")
    fb = gate2.shape[1] // 2
    dwg2 = _tn(dgate2, h3, fb, "ffn2_dw_gate")
    dwu2 = _tn(dup2, h3, fb, "ffn2_dw_up")
    dwd2 = _tn(act2, dff2, fb, "ffn2_dw_down")
    comm.ready(dict(wg2=dwg2, wu2=dwu2, wd2=dwd2), dwd2[0])

    doa, dla, dob1, dlb1, dob4, dlb4, dob16, dlb16, dw_out = _dcat(dx2, W["w_out"], cat, dep=comm.dep())
    dqa, dka, dva, dsk = _attn_bwd(aq[None], ak[None], av[None], doa[None], a_lse, dla[None], a_sink, A_HALF_WINDOW, True, "attn_a_bwd")
    bwd_in = {1: (dob1[None], lg1[None], dlb1[None]), 4: (dob4, lg4, dlb4), 16: (dob16, lg16, dlb16)}
    bg = {}
    for w, d in B_PATTERNS:
        q_, k_, v_ = bqs[d]
        do_, l_, dl_ = bwd_in[d]
        bg[d] = _attn_bwd(q_, k_, v_, do_, l_, dl_, no_sink, w // (2 * d), False, f"attn_b{d}_bwd")[:3]
    dproj, dx1, dgm = _mixer_in_bwd(dqa[0], dka[0], dva[0], [t[0] for t in bg[1]], bg[4], bg[16], cos, sin, W["w_in"], x1, gm, dx2)
    dw_in = _tn(dproj, h2, dproj.shape[1] // 2, "w_in_dw")
    comm.ready(dict(w_in=dw_in, w_out=dw_out), dw_in[0])

    dx0, dff1, dgate1, dup1, dg1 = _ffn_dx(dx1, x, g1, gate1, up1, W["wg1"], W["wu1"], W["wd1"], "ffn1_dx", dep=comm.dep())
    comm.settle(2, dx0)
    dwd1 = _tn(act1, dff1, fb, "ffn1_dw_down", dep=comm.dep())
    comm.ready(dict(wd1=dwd1), dwd1[0])
    dwg1 = _tn(dgate1, h1, fb, "ffn1_dw_gate", dep=comm.dep())
    comm.ready(dict(wg1=dwg1), dwg1[0])
    dwu1 = _tn(dup1, h1, fb, "ffn1_dw_up", dep=comm.dep())
    comm.ready(dict(wu1=dwu1), dwu1[0])

    dsink = dsk[0, :, :, ::HEAD_DIM].sum(axis=1).reshape(-1)
    small = dict(g1=dg1.sum(axis=0), gm=dgm.sum(axis=0), g2=dg2.sum(axis=0), gf=dgf.sum(axis=0), sink=dsink, loss=loss8[0, 0])
    return dx0, small


BIG = ("wg1", "wu1", "wd1", "w_in", "w_out", "wg2", "wu2", "wd2")
GATHER_GROUPS = (("wd1",), ("w_in",), ("w_out",), ("wg2", "wu2", "wd2"))


class _Comm:
    def __init__(self, shards, meanwhile):
        x, y, c = _mesh_pos()
        self.me = (2 * x + y).astype(jnp.int32).reshape(1)
        self.shards = shards
        self.token = None
        self.waiting = {}
        self.groups = []
        self.swaps = []
        first = ("wg1", "wu1")
        fulls ={k: _cast_place(self.me, shards[k], f"cast_{k}") for k in first}
        plan = _neighbour_plan([fulls[k].shape for k in first])
        send, recv, bufs, tok = _push_start("gather_first_start", [fulls[k] for k in first], 2 * len(first), plan, self.me)
        self.side = meanwhile(tok)
        fulls.update({k: _cast_place(self.me, shards[k], f"cast_{k}") for k in BIG if k not in first})
        bufs = _push_wait("gather_first_wait", send, recv, bufs, plan, [fulls[k] for k in BIG if k not in first] + list(self.side))
        self.full = dict(zip(first, _gather_forward(bufs)))
        rest = [k for names in GATHER_GROUPS for k in names]
        send, recv, bufs, self.token = _push_start("gather_rest_start", [fulls[k] for k in rest], 3 * len(rest), _gather_plan(len(rest)),
                                                   self.full[first[-1]])
        self.rest = dict(zip(rest, bufs))
        for gi, names in enumerate(GATHER_GROUPS):
            for k in names:
                self.waiting[k] = (gi, names, send, recv, 3 * rest.index(names[0]))

    def dep(self):
        return self.token

    def weight(self, name, after):
        if name in self.waiting:
            gi, names, send, recv, first = self.waiting[name]
            bufs = [self.rest[k] for k in names]
            for k, buf in zip(names, _push_wait(f"gather_wait_{gi}", send, recv, bufs, _gather_plan(len(names)), after, first)):
                self.full[k] = buf
                del self.waiting[k]
        full = self.full[name]
        return full.reshape(N_CHIPS * full.shape[1], full.shape[2])

    def ready(self, grads, after):
        names = list(grads)
        f32s, b16s = [], []
        for k in names:
            gf, gb = grads[k]
            f32s.append(gf.reshape((N_CHIPS,) + self.shards[k].shape))
            b16s.append(gb.reshape((N_CHIPS,) + self.shards[k].shape))
        n = len(names)
        lands = [lax.empty((N_CHIPS - 1,) + self.shards[k].shape, BF16) for k in names]
        plan = _grad_push_plan(n)
        send, recv, bufs, self.token = _push_start(f"grad_start_{names[0]}", b16s + lands, 3 * n, plan, after)
        self.groups.append((names, f32s, send, recv, bufs, plan))

    def settle(self, count, after):
        batch, self.groups = self.groups[:count], self.groups[count:]
        names_b, mine_b = [], []
        for names, f32s, send, recv, bufs, plan in batch:
            n = len(names)
            bufs = _push_wait(f"grad_wait_{names[0]}", send, recv, bufs, plan, mine_b[-1] if mine_b else after)
            mine_b += [_sum_own(self.me, f32s[i], bufs[n + i], f"sum_{k}") for i, k in enumerate(names)]
            names_b += names
        lands = [lax.empty(p.shape, F32) for p in mine_b]
        n = len(names_b)
        send2, recv2, both, self.token = _push_start(f"swap_start_{names_b[0]}", mine_b + lands, n, _swap_plan(n), after)
        self.swaps.append((names_b, send2, recv2, both))

    def partials(self, after):
        names_b, send2, recv2, both = self.swaps.pop(0)
        n = len(names_b)
        both = _push_wait(f"swap_wait_{names_b[0]}", send2, recv2, both, _swap_plan(n), after)
        return {k: (both[i], both[n + i]) for i, k in enumerate(names_b)}


def kernel(x, positions, norm_ffn1, w_gate1, w_up1, w_down1, norm_mix, w_in, a_sink, w_out, norm_ffn2, w_gate2, w_up2, w_down2, norm_final, loss_target, m_norm_ffn1, m_w_gate1, m_w_up1, m_w_down1, m_norm_mix, m_w_in, m_a_sink, m_w_out, m_norm_ffn2, m_w_gate2, m_w_up2, m_w_down2, m_norm_final, v_norm_ffn1, v_w_gate1, v_w_up1, v_w_down1, v_norm_mix, v_w_in, v_a_sink, v_w_out, v_norm_ffn2, v_w_gate2, v_w_up2, v_w_down2, v_norm_final):
    T, D = x.shape[1], x.shape[2]
    flip = ("wg1", "wu1", "w_in", "wg2", "wu2")

    def rows(k, a):
        return a[0].T if k in flip else a[0]

    given = dict(wg1=(w_gate1, m_w_gate1, v_w_gate1), wu1=(w_up1, m_w_up1, v_w_up1), wd1=(w_down1, m_w_down1, v_w_down1),
                 w_in=(w_in, m_w_in, v_w_in), w_out=(w_out, m_w_out, v_w_out), wg2=(w_gate2, m_w_gate2, v_w_gate2),
                 wu2=(w_up2, m_w_up2, v_w_up2), wd2=(w_down2, m_w_down2, v_w_down2))
    shards = {k: rows(k, given[k][0]) for k in BIG}

    comm = _Comm(shards, lambda tok: _rope(positions[0], tok))

    norms = (norm_ffn1, norm_mix, norm_ffn2, norm_final.reshape(1, D))
    grad_x, small = _local_step(x[0], comm.side, loss_target[0], norms, a_sink[0], comm)

    upd = {}

    def update(partial):
        for k in partial:
            outs = _adamw(shards[k], partial[k][0], partial[k][1], rows(k, given[k][1]), rows(k, given[k][2]), f"adamw_{k}")
            upd[k] = tuple((a.T if k in flip else a)[None] for a in outs)
        return outs[0]

    last = update(comm.partials(comm.dep()))
    comm.settle(2, last)

    def pad_row(a):
        a = a.reshape(-1)
        return jnp.pad(a, (0, D - a.shape[0]))

    row4 = pad_row(jnp.concatenate([small["sink"], small["loss"].reshape(1)]))
    vec = jnp.stack([small["g1"], small["gm"], small["g2"], small["gf"], row4] + [jnp.zeros((D,), F32)] * 3, axis=0)
    red = _allreduce_small(vec, comm.dep())
    comm.settle(1, red)
    last = update(comm.partials(comm.dep()))
    update(comm.partials(last))
    as_row = lambda a: a.reshape(1, -1)
    sm, loss = _adamw_small(red, [tuple(as_row(a) for a in p) for p in (
        (norm_ffn1, m_norm_ffn1, v_norm_ffn1), (norm_mix, m_norm_mix, v_norm_mix), (norm_ffn2, m_norm_ffn2, v_norm_ffn2),
        (norm_final, m_norm_final, v_norm_final), (a_sink, m_a_sink, v_a_sink))])
    sm[3] = [a.reshape(D) for a in sm[3]]

    def ordered(i):
        return [sm[0][i], upd["wg1"][i], upd["wu1"][i], upd["wd1"][i], sm[1][i], upd["w_in"][i], sm[4][i], upd["w_out"][i], sm[2][i],
                upd["wg2"][i], upd["wu2"][i], upd["wd2"][i], sm[3][i]]

    return (loss.reshape(()), grad_x[None], *ordered(0), *ordered(1), *ordered(2), *ordered(3))
```

```python
import jax
import jax.numpy as jnp
from jax import lax
from jax.experimental import pallas as pl
from jax.experimental.pallas import tpu as pltpu

F32 = jnp.float32
BF16 = jnp.bfloat16

HEAD_DIM = 64
LANES = 128
SUBLANES = 8
A_Q_W, A_KV_W, B_W = 512, 128, 512
A_HALF_WINDOW = 128
B_PATTERNS = ((128, 1), (512, 4), (2048, 16))
ROPE_THETA = 10000.0
NORM_EPS = 1e-6
FFN_RES_WEIGHT = 0.5
ADAM_LR, ADAM_B1, ADAM_B2, ADAM_EPS, ADAM_WD, ADAM_STEP = 0.001, 0.9, 0.999, 1e-08, 0.01, 10
N_CHIPS = 4
N_DEV = 8
QB = 128
SHORT_SEQ = 512
NEG = -1e30
VMEM_LIMIT = 56 * 1024 * 1024
MESH = pl.DeviceIdType.MESH
ANY = pl.BlockSpec(memory_space=pl.ANY)


def _params(sem=None):
    return pltpu.CompilerParams(dimension_semantics=sem, vmem_limit_bytes=VMEM_LIMIT)


def _sds(shape, dtype):
    return jax.ShapeDtypeStruct(tuple(shape), dtype)


def _dot(a, b):
    return jnp.dot(a, b, preferred_element_type=F32)


def _dot_nt(a, b):
    return lax.dot_general(a, b, (((1,), (1,)), ((), ())), preferred_element_type=F32)


def _dot_tn(a, b):
    return lax.dot_general(a, b, (((0,), (0,)), ((), ())), preferred_element_type=F32)


def _rms_stats(x):
    r = lax.rsqrt(jnp.mean(x * x, axis=-1, keepdims=True) + NORM_EPS)
    return x * r, r


def _rms_bwd(dh, x, g):
    xhat, r = _rms_stats(x)
    dxn = dh * g
    dx = r * (dxn - xhat * jnp.mean(dxn * xhat, axis=-1, keepdims=True))
    tm, d = x.shape
    dg = (dh * xhat).reshape(tm // SUBLANES, SUBLANES, d).sum(axis=0)
    return dx, dg


def _sigmoid(x):
    return 1.0 / (1.0 + jnp.exp(-x))


def _swap32(t):
    n = t.shape[-1]
    lane = lax.broadcasted_iota(jnp.int32, t.shape, t.ndim - 1)
    return jnp.where((lane % HEAD_DIM) < HEAD_DIM // 2, pltpu.roll(t, n - HEAD_DIM // 2, axis=t.ndim - 1),
                     pltpu.roll(t, HEAD_DIM // 2, axis=t.ndim - 1))


def _ordered(body, n_in, dep):
    if dep is None:
        return body, [], []

    def ordered(*refs):
        body(*refs[:n_in], *refs[n_in + 1:])

    return ordered, [ANY], [dep]


def _cast_place(me_arr, w, name):
    R, C = w.shape
    tr = R // 2 if (R // 2) % 16 == 0 else R

    def body(me_ref, w_ref, o_ref):
        o_ref[...] = w_ref[...].astype(BF16)

    grid_spec = pltpu.PrefetchScalarGridSpec(
        num_scalar_prefetch=1, grid=(R // tr,), in_specs=[pl.BlockSpec((tr, C), lambda t, me: (t, 0))],
        out_specs=pl.BlockSpec((None, tr, C), lambda t, me: (me[0], t, 0)))
    return pl.pallas_call(body, name=name, grid_spec=grid_spec, out_shape=_sds((N_CHIPS, R, C), BF16),
                          compiler_params=_params(("parallel",)))(me_arr, w)


HBM = pl.BlockSpec(memory_space=pltpu.HBM)
SEM = pl.BlockSpec(memory_space=pltpu.SEMAPHORE)


def _push_start(name, bufs, ncopies, plan, after):
    nb = len(bufs)

    def body(*refs):
        send, recv, token = refs[nb + 1], refs[nb + 2], refs[-1]
        for i, (src, dst, dev) in enumerate(plan(refs[:nb])):
            pltpu.make_async_remote_copy(src_ref=src, dst_ref=dst, send_sem=send.at[i], recv_sem=recv.at[i],
                                         device_id=dev, device_id_type=MESH).start()
        token[...] = jnp.zeros_like(token)

    outs = pl.pallas_call(
        body, name=name,
        out_shape=(pltpu.SemaphoreType.DMA((ncopies,)), pltpu.SemaphoreType.DMA((ncopies,)), *[pltpu.HBM(b.shape, b.dtype) for b in bufs],
                   _sds((SUBLANES, LANES), F32)),
        in_specs=[HBM] * nb + [ANY], out_specs=(SEM, SEM, *([HBM] * nb), pl.BlockSpec(memory_space=pltpu.VMEM)),
        input_output_aliases={i: 2 + i for i in range(nb)},
        compiler_params=pltpu.CompilerParams(has_side_effects=pltpu.SideEffectType.DATAFLOW_SIDE_EFFECTING),
    )(*[pltpu.with_memory_space_constraint(b, pltpu.HBM) for b in bufs], after)
    return outs[0], outs[1], list(outs[2:2 + nb]), outs[-1]


def _push_wait(name, send, recv, bufs, plan, after, first=0):
    nb = len(bufs)

    def body(*refs):
        send_ref, recv_ref = refs[nb], refs[nb + 1]
        for i, (src, dst, dev) in enumerate(plan(refs[:nb])):
            cp = pltpu.make_async_remote_copy(src_ref=src, dst_ref=dst, send_sem=send_ref.at[first + i], recv_sem=recv_ref.at[first + i],
                                              device_id=dev, device_id_type=MESH)
            cp.wait_send()
            cp.wait_recv()

    afters = list(after) if isinstance(after, (list, tuple)) else [after]
    outs = pl.pallas_call(
        body, name=name, out_shape=tuple(pltpu.HBM(b.shape, b.dtype) for b in bufs),
        in_specs=[HBM] * nb + [SEM, SEM] + [ANY] * len(afters), out_specs=tuple([HBM] * nb),
        input_output_aliases={i: i for i in range(nb)},
        compiler_params=pltpu.CompilerParams(has_side_effects=pltpu.SideEffectType.DATAFLOW_SIDE_EFFECTING),
    )(*bufs, send, recv, *afters)
    return list(outs)


def _mesh_pos():
    return lax.axis_index("x"), lax.axis_index("y"), lax.axis_index("c")


def _chip_peers(x, y, c):
    return [((1 - x, y, c), 2 * (1 - x) + y), ((x, 1 - y, c), 2 * x + (1 - y)), ((1 - x, 1 - y, c), 2 * (1 - x) + (1 - y))]


def _gather_plan(n):
    def plan(refs):
        x, y, c = _mesh_pos()
        me = 2 * x + y
        return [(refs[k].at[me], refs[k].at[me], dev) for k in range(n) for dev, _ in _chip_peers(x, y, c)]
    return plan


def _rows_of(shape, who, quarter=None):
    r2 = shape[1] // 2
    if quarter is None:
        return pl.ds(pl.multiple_of(who * r2, 16), r2)
    return pl.ds(pl.multiple_of(who * r2 + quarter * (r2 // 2), 16), r2 // 2)


def _neighbour_plan(shapes):
    def plan(refs):
        x, y, c = _mesh_pos()
        me = 2 * x + y
        return [(refs[k].at[me, _rows_of(shp, c), :], refs[k].at[me, _rows_of(shp, c), :], dev)
                for k, shp in enumerate(shapes) for dev in ((1 - x, y, c), (x, 1 - y, c))]
    return plan


def _gather_forward(fulls):
    n = len(fulls)

    def body(*refs):
        ins, outs = refs[:n], refs[n:2 * n]
        ici_send, ici_recv, d2d_send, d2d_recv = refs[2 * n:]
        x, y, c = _mesh_pos()
        cx, cy, cd = 2 * (1 - x) + y, 2 * x + (1 - y), 2 * (1 - x) + (1 - y)
        sibling, x_nbr, y_nbr = (x, y, 1 - c), (1 - x, y, c), (x, 1 - y, c)
        started = []

        def push(src, dst, send, recv, dev):
            cp = pltpu.make_async_remote_copy(src_ref=src, dst_ref=dst, send_sem=send, recv_sem=recv, device_id=dev, device_id_type=MESH)
            cp.start()
            started.append(cp)

        def arrived(blk, send, recv):
            pltpu.make_async_remote_copy(src_ref=blk, dst_ref=blk, send_sem=send, recv_sem=recv, device_id=sibling,
                                         device_id_type=MESH).wait_recv()

        for k in range(n):
            shp = fulls[k].shape
            for j, chip in enumerate((cx, cy)):
                push(ins[k].at[chip, _rows_of(shp, c), :], outs[k].at[chip, _rows_of(shp, c), :],
                     d2d_send.at[3 * k + j], d2d_recv.at[3 * k + j], sibling)
            push(ins[k].at[cx, _rows_of(shp, c, 0), :], outs[k].at[cx, _rows_of(shp, c, 0), :], ici_send.at[2 * k], ici_recv.at[2 * k], y_nbr)
            push(ins[k].at[cy, _rows_of(shp, c, 1), :], outs[k].at[cy, _rows_of(shp, c, 1), :], ici_send.at[2 * k + 1], ici_recv.at[2 * k + 1],
                 x_nbr)
        for k in range(n):
            shp = fulls[k].shape
            for q in (0, 1):
                arrived(outs[k].at[cd, _rows_of(shp, c, q), :], ici_send.at[2 * k + q], ici_recv.at[2 * k + q])
            blk = outs[k].at[cd, _rows_of(shp, c), :]
            push(blk, blk, d2d_send.at[3 * k + 2], d2d_recv.at[3 * k + 2], sibling)
        for k in range(n):
            for j, chip in enumerate((cx, cy, cd)):
                arrived(outs[k].at[chip, _rows_of(fulls[k].shape, 1 - c), :], d2d_send.at[3 * k + j], d2d_recv.at[3 * k + j])
        for cp in started:
            cp.wait_send()

    return pl.pallas_call(
        body, name="gather_forward", out_shape=[_sds(f.shape, BF16) for f in fulls],
        in_specs=[ANY] * n, out_specs=[ANY] * n, input_output_aliases={k: k for k in range(n)},
        scratch_shapes=[pltpu.SemaphoreType.DMA((n * 2,))] * 2 + [pltpu.SemaphoreType.DMA((n * 3,))] * 2,
        compiler_params=_params())(*fulls)


def _resident(shape):
    return pl.BlockSpec(shape, lambda i: (0,) * len(shape), pipeline_mode=pl.Buffered(1))


FFN_FWD_CHUNK = 256
FFN_DX_CHUNK = 512


def _chunks(n, step):
    return [(c0, min(step, n - c0)) for c0 in range(0, n, step)]


def _two_phase(chunks, first, second):
    held = {}
    for ci, ch in enumerate(chunks):
        held[ci] = first(*ch)
        if ci >= 1:
            second(*chunks[ci - 1], held.pop(ci - 1))
    last = len(chunks) - 1
    second(*chunks[last], held.pop(last))


def _loss_and_grad(x, g, target):
    D = x.shape[1]
    xhat, _ = _rms_stats(x)
    err = xhat * g - target
    loss = 0.5 * jnp.sum(jnp.sum(err * err, axis=-1, keepdims=True) * (1.0 / D), axis=0, keepdims=True)
    dx, dg = _rms_bwd(err * (1.0 / D), x, g)
    return dx, dg, loss


def _ffn_loss(x, g, wgt, wut, wd, gf, target, name, tm=512):
    T, D = x.shape
    F = wd.shape[0]

    def body(x_ref, g_ref, wg_ref, wu_ref, wd_ref, gf_ref, t_ref, dy_ref, h_ref, gate_ref, up_ref, act_ref, dgf_ref, loss_ref):
        @pl.when(pl.program_id(0) == 0)
        def _():
            dgf_ref[...] = jnp.zeros_like(dgf_ref)
            loss_ref[...] = jnp.zeros_like(loss_ref)

        xv = x_ref[...]
        xhat, _ = _rms_stats(xv)
        h = (xhat * g_ref[...]).astype(BF16)
        h_ref[...] = h
        acc = []

        def first(c0, cw):
            return _dot_nt(h, wg_ref[c0:c0 + cw, :]), _dot_nt(h, wu_ref[c0:c0 + cw, :])

        def second(c0, cw, gate_up):
            gate, up = gate_up
            act = ((gate * _sigmoid(gate)) * up).astype(BF16)
            gate_ref[:, c0:c0 + cw] = gate.astype(BF16)
            up_ref[:, c0:c0 + cw] = up.astype(BF16)
            act_ref[:, c0:c0 + cw] = act
            d = _dot(act, wd_ref[c0:c0 + cw, :])
            acc[:] = [d if not acc else acc[0] + d]

        _two_phase(_chunks(F, FFN_FWD_CHUNK), first, second)
        dy_ref[...], dgf, part = _loss_and_grad(xv + FFN_RES_WEIGHT * acc[0], gf_ref[...], t_ref[...])
        dgf_ref[...] += dgf
        loss_ref[...] += part

    row = pl.BlockSpec((tm, D), lambda i: (i, 0))
    gain = pl.BlockSpec((1, D), lambda i: (0, 0))
    saved = pl.BlockSpec((tm, F), lambda i: (i, 0))
    return pl.pallas_call(
        body, name=name, grid=(T // tm,),
        in_specs=[row, gain, _resident(wgt.shape), _resident(wut.shape), _resident(wd.shape), gain, row],
        out_specs=[row, row, saved, saved, saved, pl.BlockSpec((SUBLANES, D), lambda i: (0, 0)), pl.BlockSpec((SUBLANES, LANES), lambda i: (0, 0))],
        out_shape=[_sds((T, D), F32), _sds((T, D), BF16), _sds((T, F), BF16), _sds((T, F), BF16), _sds((T, F), BF16),
                   _sds((SUBLANES, D), F32), _sds((SUBLANES, LANES), F32)],
        compiler_params=_params(("arbitrary",)))(x, g, wgt, wut, wd, gf, target)


def _ffn_up(x, g, wgt, wut, name, tm=512, dep=None):
    T, D = x.shape
    F = wgt.shape[0]

    def body(x_ref, g_ref, wg_ref, wu_ref, h_ref, gate_ref, up_ref, act_ref):
        xhat, _ = _rms_stats(x_ref[...])
        h = (xhat * g_ref[...]).astype(BF16)
        h_ref[...] = h

        def first(c0, cw):
            return _dot_nt(h, wg_ref[c0:c0 + cw, :]), _dot_nt(h, wu_ref[c0:c0 + cw, :])

        def second(c0, cw, gate_up):
            gate, up = gate_up
            gate_ref[:, c0:c0 + cw] = gate.astype(BF16)
            up_ref[:, c0:c0 + cw] = up.astype(BF16)
            act_ref[:, c0:c0 + cw] = ((gate * _sigmoid(gate)) * up).astype(BF16)

        _two_phase(_chunks(F, FFN_FWD_CHUNK), first, second)

    row = pl.BlockSpec((tm, D), lambda i: (i, 0))
    saved = pl.BlockSpec((tm, F), lambda i: (i, 0))
    body, dep_spec, dep_arg = _ordered(body, 4, dep)
    return pl.pallas_call(
        body, name=name, grid=(T // tm,),
        in_specs=[row, pl.BlockSpec((1, D), lambda i: (0, 0)), _resident(wgt.shape), _resident(wut.shape)] + dep_spec,
        out_specs=[row, saved, saved, saved],
        out_shape=[_sds((T, D), BF16), _sds((T, F), BF16), _sds((T, F), BF16), _sds((T, F), BF16)],
        compiler_params=_params(("parallel",)))(x, g, wgt, wut, *dep_arg)


def _ffn_down(x, act, wd, name, tm=512):
    T, D = x.shape
    F = wd.shape[0]

    nt = T // tm
    slots = 3

    def body(x_ref, a_hbm, wd_ref, xo_ref, ring, sem):
        i = pl.program_id(0)

        def fetch(step):
            slot = lax.rem(step, slots)
            return pltpu.make_async_copy(a_hbm.at[pl.ds(pl.multiple_of(step * tm, tm), tm), :], ring.at[slot], sem.at[slot])

        @pl.when(i == 0)
        def _():
            for s in range(slots - 1):
                fetch(s).start()

        @pl.when(i + slots - 1 < nt)
        def _():
            fetch(i + slots - 1).start()

        fetch(i).wait()
        xo_ref[...] = x_ref[...] + FFN_RES_WEIGHT * _dot(ring[lax.rem(i, slots)], wd_ref[...])

    assert nt >= slots
    row = pl.BlockSpec((tm, D), lambda i: (i, 0))
    return pl.pallas_call(
        body, name=name, grid=(nt,), in_specs=[row, ANY, _resident(wd.shape)],
        out_specs=row, out_shape=_sds((T, D), F32),
        scratch_shapes=[pltpu.VMEM((slots, tm, F), BF16), pltpu.SemaphoreType.DMA((slots,))],
        compiler_params=_params(("arbitrary",)))(x, act, wd)


def _ffn_dx(dxo, x, g, gate_s, up_s, wgt, wut, wd, name, tm=256, dep=None):
    T, D = x.shape
    F = wd.shape[0]

    def body(dxo_ref, x_ref, g_ref, gate_ref, up_ref, wg_ref, wu_ref, wd_ref, dx_ref, dff_ref, dgate_ref, dup_ref, dg_ref):
        @pl.when(pl.program_id(0) == 0)
        def _():
            dg_ref[...] = jnp.zeros_like(dg_ref)

        d = (FFN_RES_WEIGHT * dxo_ref[...]).astype(BF16)
        dff_ref[...] = d
        dh = []

        def first(c0, cw):
            return _dot_nt(d, wd_ref[c0:c0 + cw, :])

        def second(c0, cw, da):
            gate = gate_ref[:, c0:c0 + cw].astype(F32)
            up = up_ref[:, c0:c0 + cw].astype(F32)
            s = _sigmoid(gate)
            silu = gate * s
            dup = (da * silu).astype(BF16)
            dgate = (da * up * (s * (1.0 + gate * (1.0 - s)))).astype(BF16)
            dgate_ref[:, c0:c0 + cw] = dgate
            dup_ref[:, c0:c0 + cw] = dup
            t = _dot(dgate, wg_ref[c0:c0 + cw, :]) + _dot(dup, wu_ref[c0:c0 + cw, :])
            dh[:] = [t if not dh else dh[0] + t]

        _two_phase(_chunks(F, FFN_DX_CHUNK), first, second)
        dxn, dg = _rms_bwd(dh[0], x_ref[...], g_ref[...])
        dg_ref[...] += dg
        dx_ref[...] = dxo_ref[...] + dxn

    row = pl.BlockSpec((tm, D), lambda i: (i, 0))
    saved = pl.BlockSpec((tm, F), lambda i: (i, 0))
    body, dep_spec, dep_arg = _ordered(body, 8, dep)
    return pl.pallas_call(
        body, name=name, grid=(T // tm,),
        in_specs=[row, row, pl.BlockSpec((1, D), lambda i: (0, 0)), saved, saved, _resident(wgt.shape), _resident(wut.shape),
                  _resident(wd.shape)] + dep_spec,
        out_specs=[row, row, saved, saved, pl.BlockSpec((SUBLANES, D), lambda i: (0, 0))],
        out_shape=[_sds((T, D), F32), _sds((T, D), BF16), _sds((T, F), BF16), _sds((T, F), BF16), _sds((SUBLANES, D), F32)],
        compiler_params=_params(("arbitrary",)))(dxo, x, g, gate_s, up_s, wgt, wut, wd, *dep_arg)


def _tn(a, b, mb, name, tk=2048, dep=None):
    T, M = a.shape
    N = b.shape[1]
    nt = T // tk

    def body(a_ref, b_ref, o_ref, ob_ref):
        @pl.when(pl.program_id(1) == 0)
        def _():
            o_ref[...] = jnp.zeros_like(o_ref)

        o_ref[...] += _dot_tn(a_ref[...].astype(BF16), b_ref[...].astype(BF16))

        @pl.when(pl.program_id(1) == nt - 1)
        def _():
            ob_ref[...] = o_ref[...].astype(BF16)

    o_spec = pl.BlockSpec((mb, N), lambda g, t: (g, 0))
    body, dep_spec, dep_arg = _ordered(body, 2, dep)
    return pl.pallas_call(
        body, name=name, grid=(M // mb, nt),
        in_specs=[pl.BlockSpec((tk, mb), lambda g, t: (t, g)), pl.BlockSpec((tk, N), lambda g, t: (t, 0))] + dep_spec,
        out_specs=[o_spec, o_spec], out_shape=[_sds((M, N), F32), _sds((M, N), BF16)],
        compiler_params=_params(("parallel", "arbitrary")))(a, b, *dep_arg)


def _rope_tables(pos_col, inv_freq):
    T = pos_col.shape[0]

    def body(p_ref, f_ref, c_ref, s_ref):
        ang = p_ref[...].astype(F32) * f_ref[...]
        lane = lax.broadcasted_iota(jnp.int32, ang.shape, 1)
        c_ref[...] = jnp.cos(ang)
        sn = jnp.sin(ang)
        s_ref[...] = jnp.where((lane % HEAD_DIM) < HEAD_DIM // 2, -sn, sn)

    tm = 1024
    return pl.pallas_call(
        body, name="rope_tables", grid=(T // tm,),
        in_specs=[pl.BlockSpec((tm, 1), lambda i: (i, 0)), pl.BlockSpec((1, LANES), lambda i: (0, 0))],
        out_specs=[pl.BlockSpec((tm, LANES), lambda i: (i, 0))] * 2,
        out_shape=[_sds((T, LANES), F32)] * 2, compiler_params=_params(("parallel",)))(pos_col, inv_freq)


def _deinterleave(scr, out_ref, d, tm, nblk):
    for r in range(d):
        for cb in range(nblk):
            out_ref[r, :, cb * LANES:(cb + 1) * LANES] = scr[cb, pl.ds(r, tm // d, stride=d), :].astype(out_ref.dtype)


def _interleave(in_ref, scr, d, tm, nblk):
    for r in range(d):
        for cb in range(nblk):
            scr[cb, pl.ds(r, tm // d, stride=d), :] = in_ref[r, :, cb * LANES:(cb + 1) * LANES].astype(F32)


def _proj_rope(x, g, w_in, cos, sin, tm=512):
    T, D = x.shape
    dils = [d for _, d in B_PATTERNS if d > 1]
    nbb = B_W // LANES
    scale = HEAD_DIM ** -0.5
    cuts = [0, A_Q_W, A_Q_W + A_KV_W, A_Q_W + 2 * A_KV_W, A_Q_W + 2 * A_KV_W + B_W, A_Q_W + 2 * A_KV_W + 2 * B_W,
            A_Q_W + 2 * A_KV_W + 3 * B_W]

    def body(x_ref, g_ref, w_ref, c_ref, s_ref, h_ref, aq_ref, ak_ref, av_ref, *rest):
        b_refs, scr = rest[:-1], rest[-1]
        xhat, _ = _rms_stats(x_ref[...])
        h = (xhat * g_ref[...]).astype(BF16)
        h_ref[...] = h
        cs, sn = c_ref[...], s_ref[...]

        def project(idx, ref, rope, mult, which):
            return _dot_nt(h, w_ref[cuts[idx]:cuts[idx + 1], :])

        def finish(idx, ref, rope, mult, which, whole):
            for cb in range((cuts[idx + 1] - cuts[idx]) // LANES):
                p = whole[:, cb * LANES:(cb + 1) * LANES]
                if rope:
                    p = p * cs + _swap32(p) * sn
                if mult != 1.0:
                    p = p * mult
                ref[:, cb * LANES:(cb + 1) * LANES] = p.astype(BF16)
                if which is not None:
                    scr[which, cb] = p
            if which is not None:
                for di, d in enumerate(dils):
                    _deinterleave(scr.at[which], b_refs[3 * (di + 1) + which], d, tm, nbb)

        _two_phase([(0, aq_ref, True, scale, None), (1, ak_ref, True, 1.0, None), (2, av_ref, False, 1.0, None),
                    (3, b_refs[0], True, scale, 0), (4, b_refs[1], True, 1.0, 1), (5, b_refs[2], False, 1.0, 2)], project, finish)

    row = lambda w: pl.BlockSpec((tm, w), lambda i: (i, 0))
    out_specs = [row(D), row(A_Q_W), row(A_KV_W), row(A_KV_W)] + [row(B_W)] * 3
    out_shape = [_sds((T, D), BF16), _sds((T, A_Q_W), BF16), _sds((T, A_KV_W), BF16), _sds((T, A_KV_W), BF16)] + [_sds((T, B_W), BF16)] * 3
    for d in dils:
        out_specs += [pl.BlockSpec((d, tm // d, B_W), lambda i: (0, i, 0))] * 3
        out_shape += [_sds((d, T // d, B_W), BF16)] * 3
    return pl.pallas_call(
        body, name="proj_rope", grid=(T // tm,),
        in_specs=[row(D), pl.BlockSpec((1, D), lambda i: (0, 0)), pl.BlockSpec(w_in.shape, lambda i: (0, 0)), row(LANES), row(LANES)],
        out_specs=out_specs, out_shape=out_shape, scratch_shapes=[pltpu.VMEM((3, nbb, tm, LANES), F32)],
        compiler_params=_params(("parallel",)))(x, g, w_in, cos, sin)


def _band_bias(rel, qb, kw, hw):
    ri = lax.broadcasted_iota(jnp.int32, (2 * qb, kw), 0) & (qb - 1)
    ci = lax.broadcasted_iota(jnp.int32, (2 * qb, kw), 1)
    return jnp.where(jnp.abs(ri + rel - ci) <= hw, 0.0, NEG).astype(F32)


def _stack_heads(x, lo):
    z = jnp.zeros_like(x)
    return jnp.concatenate([jnp.where(lo, x, z), jnp.where(lo, z, x)], axis=0)


def _unstack_heads(y, lo):
    qb = y.shape[0] // 2
    return jnp.where(lo, y[:qb], y[qb:])


def _band_setup(bias_scr, qb, kw, hw):
    if bias_scr is not None:
        for i in range(3):
            bias_scr[i] = _band_bias(i * hw, qb, kw, hw)


def _band_window(bias_scr, qs, L, qb, kw, hw):
    ws = pl.multiple_of(jnp.clip(qs - hw, 0, L - kw), 64)
    if bias_scr is None:
        return ws, _band_bias(qs - ws, qb, kw, hw)
    return ws, bias_scr[lax.shift_right_logical(qs - ws, hw.bit_length() - 1)]


def _dup_kv_head(src_ref, dst_ref, head, L):
    step = min(L, 1024)
    for r0 in range(0, L, step):
        xf = src_ref[r0:r0 + step, :].astype(F32)
        lane = lax.broadcasted_iota(jnp.int32, xf.shape, 1)
        keep = jnp.logical_xor(lane < HEAD_DIM, head == 1)
        dst_ref[r0:r0 + step, :] = jnp.where(keep, xf, pltpu.roll(xf, HEAD_DIM, axis=1)).astype(dst_ref.dtype)


def _attn_fwd(q, k, v, sink, hw, gqa, out_dtype, name, qb=QB, blocks_per_step=8, out_cols=None):
    NB, L, Cq = q.shape
    Ls = min(L, 2048)
    kw = min(qb + 2 * hw, L)
    tables = L >= qb + 2 * hw
    unroll = min(blocks_per_step, Ls // qb)
    nlb = 1 if (gqa or L > SHORT_SEQ) else Cq // LANES

    def body(sink_ref, q_ref, k_ref, v_ref, o_ref, lse_ref, *scr):
        b, s_idx = pl.program_id(1), pl.program_id(2)
        bias_scr = scr[0] if tables else None
        _band_setup(bias_scr, qb, kw, hw)
        if gqa:
            kd, vd = scr[-2:]

            @pl.when(s_idx == 0)
            def _():
                _dup_kv_head(k_ref, kd, b // 2, L)
                _dup_kv_head(v_ref, vd, b // 2, L)
        else:
            kd, vd = k_ref, v_ref
        lane = lax.broadcasted_iota(jnp.int32, (qb, LANES), 1)
        lo = lane < HEAD_DIM
        if gqa:
            row = lax.broadcasted_iota(jnp.int32, (2 * qb, 1), 0)
            sk = jnp.where(row < qb, sink_ref[2 * b], sink_ref[2 * b + 1])

        def block(ql, col):
            qs = s_idx * Ls + ql
            ws, bias = _band_window(bias_scr, qs, L, qb, kw, hw)
            return ws, _dot_nt(_stack_heads(q_ref[pl.ds(ql, qb), col], lo), kd[pl.ds(ws, kw), col]) + bias

        def finish(ql, col, scores):
            ws, s = scores
            m = jnp.max(s, axis=-1, keepdims=True)
            if gqa:
                m = jnp.maximum(m, sk)
            p = jnp.exp(s - m)
            den = jnp.sum(p, axis=-1, keepdims=True)
            if gqa:
                den = den + jnp.exp(sk - m)
            o = _dot(p.astype(BF16), vd[pl.ds(ws, kw), col]) * (1.0 / den)
            o_ref[pl.ds(ql, qb), col] = _unstack_heads(o, lo).astype(o_ref.dtype)
            lse_ref[pl.ds(ql, qb), col] = _unstack_heads(m + jnp.log(den), lo)

        for lb in range(nlb):
            def step(n, carry, col=slice(lb * LANES, (lb + 1) * LANES)):
                _two_phase([(pl.multiple_of((n * unroll + u) * qb, qb), col) for u in range(unroll)], block, finish)
                return carry

            lax.fori_loop(0, Ls // (qb * unroll), step, 0)

    kv_map = (lambda r, b, s: (r, 0, 0)) if gqa else (lambda r, b, s: (r, 0, b))
    seg = pl.BlockSpec((None, Ls, nlb * LANES), lambda r, b, s: (r, s, b))
    return pl.pallas_call(
        body, name=name, grid=(NB, Cq // (nlb * LANES), L // Ls),
        in_specs=[pl.BlockSpec(memory_space=pltpu.SMEM), seg, pl.BlockSpec((None, L, nlb * LANES), kv_map),
                  pl.BlockSpec((None, L, nlb * LANES), kv_map)],
        out_specs=[seg, seg], out_shape=[_sds((NB, L, out_cols or Cq), out_dtype), _sds((NB, L, Cq), F32)],
        scratch_shapes=([pltpu.VMEM((3, 2 * qb, kw), F32)] if tables else []) + ([pltpu.VMEM((L, LANES), BF16)] * 2 if gqa else []),
        compiler_params=_params(("parallel", "parallel", "arbitrary")))(sink, q, k, v)


def _attn_bwd(q, k, v, do, lse, delta, sink, hw, gqa, name, qb=QB, blocks_per_step=8):
    NB, L, Cq = q.shape
    Ck = k.shape[2]
    Ls = min(L, 2048)
    kw = min(qb + 2 * hw, L)
    reps = kw // LANES
    nseg = L // Ls
    scale = HEAD_DIM ** -0.5
    tables = L >= qb + 2 * hw
    unroll = min(blocks_per_step, Ls // qb)
    nlb = 1 if (gqa or L > SHORT_SEQ) else Cq // LANES

    def body(sink_ref, q_ref, do_ref, lse_ref, dl_ref, k_ref, v_ref, dq_ref, dk_ref, dv_ref, dsk_ref, *scr):
        b, s_idx = pl.program_id(1), pl.program_id(2)
        lane = lax.broadcasted_iota(jnp.int32, (qb, LANES), 1)
        lo = lane < HEAD_DIM
        bias_scr = scr[0] if tables else None
        _band_setup(bias_scr, qb, kw, hw)
        if gqa:
            kd, vd, dk_acc, dv_acc, dsk_acc = scr[-5:]

            @pl.when(s_idx == 0)
            def _():
                _dup_kv_head(k_ref, kd, b // 2, L)
                _dup_kv_head(v_ref, vd, b // 2, L)
                dk_acc[...] = jnp.zeros_like(dk_acc)
                dv_acc[...] = jnp.zeros_like(dv_acc)
                dsk_acc[...] = jnp.zeros_like(dsk_acc)

            @pl.when((s_idx == 0) & (b == 0))
            def _():
                dk_ref[...] = jnp.zeros_like(dk_ref)
                dv_ref[...] = jnp.zeros_like(dv_ref)
        else:
            kd, vd = k_ref, v_ref
            dk_acc, dv_acc = scr[-2:]

            @pl.when(s_idx == 0)
            def _():
                dk_acc[...] = jnp.zeros_like(dk_acc)
                dv_acc[...] = jnp.zeros_like(dv_acc)

        def block(ql, col):
            qs = s_idx * Ls + ql
            ws, bias = _band_window(bias_scr, qs, L, qb, kw, hw)
            qv, dov = q_ref[pl.ds(ql, qb), col], do_ref[pl.ds(ql, qb), col]
            lse, dl = lse_ref[pl.ds(ql, qb), col], dl_ref[pl.ds(ql, qb), col]
            kv_, vv = kd[pl.ds(ws, kw), col], vd[pl.ds(ws, kw), col]
            q2, do2 = _stack_heads(qv, lo), _stack_heads(dov, lo)
            return ws, q2, do2, lse, dl, _dot_nt(q2, kv_) + bias, _dot_nt(do2, vv)

        def finish(ql, col, held):
            ws, q2, do2, lse, dl, s, dp = held
            lse_sw, dl_sw = pltpu.roll(lse, HEAD_DIM, axis=1), pltpu.roll(dl, HEAD_DIM, axis=1)
            lse2 = jnp.concatenate([jnp.where(lo, lse, lse_sw), jnp.where(lo, lse_sw, lse)], axis=0)
            dl2 = jnp.concatenate([jnp.where(lo, dl, dl_sw), jnp.where(lo, dl_sw, dl)], axis=0)
            p = jnp.exp(s - jnp.tile(lse2, (1, reps)))
            ds = (p * (dp - jnp.tile(dl2, (1, reps)))).astype(BF16)
            dq_ref[pl.ds(ql, qb), col] = (_unstack_heads(_dot(ds, kd[pl.ds(ws, kw), col]), lo) * scale).astype(dq_ref.dtype)
            both = _dot_tn(jnp.concatenate([ds, p.astype(BF16)], axis=1), jnp.concatenate([q2, do2], axis=1))
            dk_acc[pl.ds(ws, kw), col] += both[:kw, :LANES]
            dv_acc[pl.ds(ws, kw), col] += both[kw:, LANES:]
            if gqa:
                sk = jnp.where(lo, sink_ref[2 * b], sink_ref[2 * b + 1])
                dsk_acc[...] += -jnp.exp(sk - lse) * dl

        for lb in range(nlb):
            def step(n, carry, col=slice(lb * LANES, (lb + 1) * LANES)):
                _two_phase([(pl.multiple_of((n * unroll + u) * qb, qb), col) for u in range(unroll)], block, finish)
                return carry

            lax.fori_loop(0, Ls // (qb * unroll), step, 0)

        if gqa:
            @pl.when(s_idx == nseg - 1)
            def _():
                step_rows = min(L, 1024)
                for r0 in range(0, L, step_rows):
                    lanek = lax.broadcasted_iota(jnp.int32, (step_rows, LANES), 1)
                    mine = jnp.logical_xor(lanek < HEAD_DIM, (b // 2) == 1)
                    for acc, ref in ((dk_acc, dk_ref), (dv_acc, dv_ref)):
                        a = acc[r0:r0 + step_rows, :]
                        ref[r0:r0 + step_rows, :] += jnp.where(mine, a + pltpu.roll(a, HEAD_DIM, axis=1), 0.0)
                dsk_ref[...] = dsk_acc[...].reshape(qb // SUBLANES, SUBLANES, LANES).sum(axis=0)
        else:
            dsk_ref[...] = jnp.zeros_like(dsk_ref)

            @pl.when(s_idx == nseg - 1)
            def _():
                dk_ref[...] = dk_acc[...].astype(dk_ref.dtype)
                dv_ref[...] = dv_acc[...].astype(dv_ref.dtype)

    kv_map = (lambda r, b, s: (r, 0, 0)) if gqa else (lambda r, b, s: (r, 0, b))
    seg = pl.BlockSpec((None, Ls, nlb * LANES), lambda r, b, s: (r, s, b))
    full = pl.BlockSpec((None, L, nlb * LANES), kv_map)
    scratch = [pltpu.VMEM((3, 2 * qb, kw), F32)] if tables else []
    if gqa:
        scratch += [pltpu.VMEM((L, LANES), BF16)] * 2 + [pltpu.VMEM((L, LANES), F32)] * 2 + [pltpu.VMEM((qb, LANES), F32)]
    else:
        scratch += [pltpu.VMEM((L, nlb * LANES), F32)] * 2
    kv_dtype = F32 if gqa else BF16
    return pl.pallas_call(
        body, name=name, grid=(NB, Cq // (nlb * LANES), nseg),
        in_specs=[pl.BlockSpec(memory_space=pltpu.SMEM), seg, seg, seg, seg, full, full],
        out_specs=[seg, full, full, pl.BlockSpec((None, None, SUBLANES, LANES), lambda r, b, s: (r, b, 0, 0))],
        out_shape=[_sds((NB, L, Cq), BF16), _sds((NB, L, Ck), kv_dtype), _sds((NB, L, Ck), kv_dtype),
                   _sds((NB, Cq // LANES, SUBLANES, LANES), F32)],
        scratch_shapes=scratch,
        compiler_params=_params(("arbitrary", "arbitrary", "arbitrary")))(sink, q, do, lse, delta, k, v)


def _dilated_fwd(cat, qkv, hw, tile=2048):
    T = cat.shape[0]
    dils = sorted(qkv)
    nbb, na = B_W // LANES, A_Q_W // LANES
    qb, kw = QB, QB + 2 * hw
    rows_merge = 256
    assert T % tile == 0 and all(tile % (d * qb) == 0 and T // d >= kw for d in dils)

    def body(cat_in, *refs):
        qkv_refs = {d: refs[3 * j:3 * j + 3] for j, d in enumerate(dils)}
        cat_ref, lg_refs = refs[3 * len(dils)], refs[3 * len(dils) + 1:4 * len(dils) + 1]
        o_scr, l_scr, bias_scr = refs[4 * len(dils) + 1:]
        i = pl.program_id(1)
        _band_setup(bias_scr, qb, kw, hw)
        lane = lax.broadcasted_iota(jnp.int32, (qb, LANES), 1)
        lo = lane < HEAD_DIM
        for pi, d in enumerate(dils):
            q_ref, k_ref, v_ref = qkv_refs[d]
            L, rows = T // d, tile // d

            def place(r, n, d=d):
                return pl.ds(r + d * n * qb, qb, stride=d) if d > 1 else pl.ds(n * qb, qb)

            def scores(r, n, q_ref=q_ref, k_ref=k_ref, L=L, rows=rows):
                ws, bias = _band_window(bias_scr, i * rows + n * qb, L, qb, kw, hw)
                return ws, _dot_nt(_stack_heads(q_ref[r, n * qb:(n + 1) * qb, :], lo), k_ref[r, pl.ds(ws, kw), :]) + bias

            def finish(r, n, held, v_ref=v_ref, pi=pi, place=place):
                ws, s = held
                m = jnp.max(s, axis=-1, keepdims=True)
                p = jnp.exp(s - m)
                den = jnp.sum(p, axis=-1, keepdims=True)
                o = _dot(p.astype(BF16), v_ref[r, pl.ds(ws, kw), :]) * (1.0 / den)
                o_scr[pi, place(r, n), :] = _unstack_heads(o, lo)
                l_scr[pi, place(r, n), :] = _unstack_heads(m + jnp.log(den), lo)

            blocks = [(r, n) for r in range(d) for n in range(rows // qb)]
            for g0 in range(0, len(blocks), 8):
                _two_phase(blocks[g0:g0 + 8], scores, finish)

        for r0 in range(0, tile, rows_merge):
            rs = slice(r0, r0 + rows_merge)
            ls_ = [l_scr[pi, rs, :] for pi in range(len(dils))]
            m = ls_[0]
            for l in ls_[1:]:
                m = jnp.maximum(m, l)
            es = [jnp.exp(l - m) for l in ls_]
            den, out = es[0], es[0] * o_scr[0, rs, :]
            for pi in range(1, len(dils)):
                den = den + es[pi]
                out = out + es[pi] * o_scr[pi, rs, :]
            cat_ref[rs, :] = (out * (1.0 / den)).astype(BF16)
            l_scr[0, rs, :] = m + jnp.log(den)
        for lg_ref, d in zip(lg_refs, dils):
            for r in range(d):
                lg_ref[r] = l_scr[0, pl.ds(r, tile // d, stride=d), :] if d > 1 else l_scr[0]

    in_specs = [pl.BlockSpec(memory_space=pl.ANY)]
    operands = [cat]
    for d in dils:
        in_specs += [pl.BlockSpec((d, tile // d, LANES), lambda b, i: (0, i, b))] + [pl.BlockSpec((d, T // d, LANES), lambda b, i: (0, 0, b))] * 2
        operands += list(qkv[d])
    return pl.pallas_call(
        body, name="dilated_fwd", grid=(nbb, T // tile), in_specs=in_specs,
        out_specs=[pl.BlockSpec((tile, LANES), lambda b, i: (i, na + b))] + [pl.BlockSpec((d, tile // d, LANES), lambda b, i: (0, i, b)) for d in dils],
        out_shape=[_sds(cat.shape, BF16)] + [_sds((d, T // d, B_W), F32) for d in dils],
        input_output_aliases={0: 0},
        scratch_shapes=[pltpu.VMEM((len(dils), tile, LANES), F32)] * 2 + [pltpu.VMEM((3, 2 * qb, kw), F32)],
        compiler_params=_params(("parallel", "arbitrary")))(*operands)


def _out_proj(x, cat, w_out, tm=512):
    T, D = x.shape

    def body(x_ref, c_ref, w_ref, o_ref):
        o_ref[...] = x_ref[...] + _dot(c_ref[...], w_ref[...])

    row = lambda w: pl.BlockSpec((tm, w), lambda i: (i, 0))
    return pl.pallas_call(
        body, name="out_proj", grid=(T // tm,), in_specs=[row(D), row(cat.shape[1]), pl.BlockSpec(w_out.shape, lambda i: (0, 0))],
        out_specs=row(D), out_shape=_sds((T, D), F32), compiler_params=_params(("parallel",)))(x, cat, w_out)


def _dcat(dx, w_out, cat, tm=512, dep=None):
    T, D = dx.shape
    C = cat.shape[1]
    nba, nbb = A_Q_W // LANES, B_W // LANES
    nt = T // tm

    def body(dx_ref, w_ref, cat_ref, doa_ref, dla_ref, dob1_ref, dlb1_ref, dob4_ref, dlb4_ref, dob16_ref, dlb16_ref, dw_ref, dwb_ref,
             sdo, sdl):
        @pl.when(pl.program_id(0) == 0)
        def _():
            dw_ref[...] = jnp.zeros_like(dw_ref)

        dxb = dx_ref[...].astype(BF16)
        dc = _dot_nt(dxb, w_ref[...])
        dw_ref[...] += _dot_tn(cat_ref[...], dxb)

        @pl.when(pl.program_id(0) == nt - 1)
        def _():
            dwb_ref[...] = dw_ref[...].astype(BF16)

        ri =lax.broadcasted_iota(jnp.int32, (LANES, LANES), 0)
        ci = lax.broadcasted_iota(jnp.int32, (LANES, LANES), 1)
        same_head = ((ri // HEAD_DIM) == (ci // HEAD_DIM)).astype(BF16)
        for cb in range(C // LANES):
            cols = slice(cb * LANES, (cb + 1) * LANES)
            blk = dc[:, cols]
            prod = blk * cat_ref[:, cols].astype(F32)
            hi = prod.astype(BF16)
            lo_ = (prod - hi.astype(F32)).astype(BF16)
            dl = _dot(hi, same_head) + _dot(lo_, same_head)
            if cb < nba:
                doa_ref[:, cols] = blk.astype(BF16)
                dla_ref[:, cols] = dl
            else:
                bcols = slice((cb - nba) * LANES, (cb - nba + 1) * LANES)
                dob1_ref[:, bcols] = blk.astype(BF16)
                dlb1_ref[:, bcols] = dl
                sdo[cb - nba] = blk
                sdl[cb - nba] = dl
        _deinterleave(sdo, dob4_ref, 4, tm, nbb)
        _deinterleave(sdl, dlb4_ref, 4, tm, nbb)
        _deinterleave(sdo, dob16_ref, 16, tm, nbb)
        _deinterleave(sdl, dlb16_ref, 16, tm, nbb)

    row = lambda w: pl.BlockSpec((tm, w), lambda i: (i, 0))
    perm = lambda d: pl.BlockSpec((d, tm // d, B_W), lambda i: (0, i, 0))
    whole = pl.BlockSpec((C, D), lambda i: (0, 0))
    body, dep_spec, dep_arg = _ordered(body, 3, dep)
    outs = pl.pallas_call(
        body, name="dcat", grid=(nt,), in_specs=[row(D), whole, row(C)] + dep_spec,
        out_specs=[row(A_Q_W), row(A_Q_W), row(B_W), row(B_W), perm(4), perm(4), perm(16), perm(16), whole, whole],
        out_shape=[_sds((T, A_Q_W), BF16), _sds((T, A_Q_W), F32), _sds((T, B_W), BF16), _sds((T, B_W), F32),
                   _sds((4, T // 4, B_W), BF16), _sds((4, T // 4, B_W), F32), _sds((16, T // 16, B_W), BF16), _sds((16, T // 16, B_W), F32),
                   _sds((C, D), F32), _sds((C, D), BF16)],
        scratch_shapes=[pltpu.VMEM((nbb, tm, LANES), F32)] * 2, compiler_params=_params(("arbitrary",)))(dx, w_out, cat, *dep_arg)
    return (*outs[:8], (outs[8], outs[9]))


def _mixer_in_bwd(dqa, dka, dva, b1, b4, b16, cos, sin, w_in, x, g, dres, tm=512):
    T, D = x.shape
    nbb = B_W // LANES
    width = A_Q_W + 2 * A_KV_W + 3 * B_W

    def body(dqa_ref, dka_ref, dva_ref, q1, k1, v1, q4, k4, v4, q16, k16, v16, c_ref, s_ref, w_ref, x_ref, g_ref, dr_ref,
             o_ref, dx_ref, dg_ref, scr):
        @pl.when(pl.program_id(0) == 0)
        def _():
            dg_ref[...] = jnp.zeros_like(dg_ref)

        cs, sn = c_ref[...], s_ref[...]
        dh = []

        def unrope(t):
            return t * cs + _swap32(t * sn)

        def project(c0, c1):
            t = _dot(o_ref[:, c0:c1], w_ref[c0:c1, :])
            dh[:] = [t if not dh else dh[0] + t]

        col = 0
        for ref, rope in ((dqa_ref, True), (dka_ref, True), (dva_ref, False)):
            for cb in range(ref.shape[1] // LANES):
                t = ref[:, cb * LANES:(cb + 1) * LANES].astype(F32)
                o_ref[:, col:col + LANES] = (unrope(t) if rope else t).astype(BF16)
                col += LANES
        project(0, col)
        for which, (r1, r4, r16, rope) in enumerate(((q1, q4, q16, True), (k1, k4, k16, True), (v1, v4, v16, False))):
            _interleave(r4, scr.at[0], 4, tm, nbb)
            _interleave(r16, scr.at[1], 16, tm, nbb)
            for cb in range(nbb):
                t = r1[:, cb * LANES:(cb + 1) * LANES].astype(F32) + scr[0, cb] + scr[1, cb]
                o_ref[:, col:col + LANES] = (unrope(t) if rope else t).astype(BF16)
                col += LANES
            project(col - B_W, col)
        dxn, dg = _rms_bwd(dh[0], x_ref[...], g_ref[...])
        dg_ref[...] += dg
        dx_ref[...] = dr_ref[...] + dxn

    row = lambda w: pl.BlockSpec((tm, w), lambda i: (i, 0))
    perm = lambda d: pl.BlockSpec((d, tm // d, B_W), lambda i: (0, i, 0))
    return pl.pallas_call(
        body, name="mixer_in_bwd", grid=(T // tm,),
        in_specs=[row(A_Q_W), row(A_KV_W), row(A_KV_W)] + [row(B_W)] * 3 + [perm(4)] * 3 + [perm(16)] * 3 + [row(LANES), row(LANES)]
        + [_resident(w_in.shape), row(D), pl.BlockSpec((1, D), lambda i: (0, 0)), row(D)],
        out_specs=[row(width), row(D), pl.BlockSpec((SUBLANES, D), lambda i: (0, 0))],
        out_shape=[_sds((T, width), BF16), _sds((T, D), F32), _sds((SUBLANES, D), F32)],
        scratch_shapes=[pltpu.VMEM((2, nbb, tm, LANES), F32)],
        compiler_params=_params(("arbitrary",)))(dqa, dka, dva, *b1, *b4, *b16, cos, sin, w_in, x, g, dres)


def _grad_push_plan(n):
    def plan(refs):
        x, y, c = _mesh_pos()
        return [(refs[k].at[chip], refs[n + k].at[rel], dev) for k in range(n) for rel, (dev, chip) in enumerate(_chip_peers(x, y, c))]
    return plan


def _sum_own(me_arr, g, landed, name):
    ns, R, C = g.shape
    tr = R // 2 if (R // 2) % 16 == 0 else R

    def body(me_ref, g_ref, x_ref, o_ref):
        acc = g_ref[...]
        for rel in range(ns - 1):
            acc = acc + x_ref[rel].astype(F32)
        o_ref[...] = acc

    grid_spec = pltpu.PrefetchScalarGridSpec(
        num_scalar_prefetch=1, grid=(R // tr,),
        in_specs=[pl.BlockSpec((None, tr, C), lambda t, me: (me[0], t, 0)), pl.BlockSpec((ns - 1, tr, C), lambda t, me: (0, t, 0))],
        out_specs=pl.BlockSpec((tr, C), lambda t, me: (t, 0)))
    return pl.pallas_call(body, name=name, grid_spec=grid_spec, out_shape=_sds((R, C), F32),
                          compiler_params=_params(("parallel",)))(me_arr, g, landed)


def _swap_plan(n):
    def plan(refs):
        x, y, c = _mesh_pos()
        return [(refs[k], refs[n + k], (x, y, 1 - c)) for k in range(n)]
    return plan


def _allreduce_small(v, dep):
    rows, W = v.shape

    def body(v_ref, o_ref, buf, send, recv):
        x, y, c = _mesh_pos()
        me = 4 * x + 2 * y + c
        cps = []
        for m in range(1, N_DEV):
            dev = (x ^ (m >> 2), y ^ ((m >> 1) & 1), c ^ (m & 1))
            cp = pltpu.make_async_remote_copy(src_ref=v_ref, dst_ref=buf.at[me], send_sem=send.at[m - 1], recv_sem=recv.at[m - 1],
                                              device_id=dev, device_id_type=MESH)
            cp.start()
            cps.append(cp)
        for m in range(1, N_DEV):
            pltpu.make_async_remote_copy(src_ref=v_ref, dst_ref=buf.at[me ^ m], send_sem=send.at[m - 1], recv_sem=recv.at[m - 1],
                                         device_id=(x, y, c), device_id_type=MESH).wait_recv()
        for cp in cps:
            cp.wait_send()
        buf[me] = v_ref[...]
        acc = buf[0]
        for i in range(1, N_DEV):
            acc = acc + buf[i]
        o_ref[...] = acc

    body, dep_spec, dep_arg = _ordered(body, 1, dep)
    return pl.pallas_call(
        body, name="allreduce_small", out_shape=_sds((rows, W), F32), in_specs=[pl.BlockSpec(memory_space=pltpu.VMEM)] + dep_spec,
        scratch_shapes=[pltpu.VMEM((N_DEV, rows, W), F32), pltpu.SemaphoreType.DMA((N_DEV - 1,)), pltpu.SemaphoreType.DMA((N_DEV - 1,))],
        compiler_params=_params())(v, *dep_arg)


def _adamw_math(w, g, m, v):
    c1 = 1.0 / (1.0 - ADAM_B1 ** ADAM_STEP)
    c2 = 1.0 / (1.0 - ADAM_B2 ** ADAM_STEP)
    nm = ADAM_B1 * m + (1.0 - ADAM_B1) * g
    nv = ADAM_B2 * v + (1.0 - ADAM_B2) * (g * g)
    return -ADAM_LR * ((nm * c1) / (jnp.sqrt(nv * c2) + ADAM_EPS) + ADAM_WD * w), nm, nv


def _adamw_small(rows, params):
    n = len(params)
    n_sink = params[-1][0].shape[1]

    def body(rows_ref, *refs):
        ins, outs = refs[:3 * n], refs[3 * n:]
        for j in range(n):
            g = rows_ref[j:j + 1, 0:n_sink] if j == n - 1 else rows_ref[j:j + 1, :]
            d, nm, nv = _adamw_math(ins[3 * j][...], g, ins[3 * j + 1][...], ins[3 * j + 2][...])
            for ref, val in zip(outs[4 * j:4 * j + 4], (g, d, nm, nv)):
                ref[...] = val
        outs[-1][...] = rows_ref[n - 1:n, n_sink:n_sink + 1]

    flat = [a for p in params for a in p]
    outs = pl.pallas_call(body, name="adamw_small", out_shape=[_sds(p[0].shape, F32) for p in params for _ in range(4)] + [_sds((1, 1), F32)],
                          compiler_params=_params())(rows, *flat)
    return [outs[4 * j:4 * j + 4] for j in range(n)], outs[-1]


def _adamw(w, gp, gq, m, v, name):
    R, C = w.shape
    tr = R // 2 if (R // 2) % SUBLANES == 0 else R

    def body(w_ref, gp_ref, gq_ref, m_ref, v_ref, g_ref, d_ref, nm_ref, nv_ref):
        gv = gp_ref[...] + gq_ref[...]
        g_ref[...] = gv
        d_ref[...], nm_ref[...], nv_ref[...] = _adamw_math(w_ref[...], gv, m_ref[...], v_ref[...])

    blk = pl.BlockSpec((tr, C), lambda t: (t, 0))
    return pl.pallas_call(body, name=name, grid=(R // tr,), in_specs=[blk] * 5, out_specs=[blk] * 4,
                          out_shape=[_sds((R, C), F32)] * 4, compiler_params=_params(("parallel",)))(w, gp, gq, m, v)


def _rope(positions, after):
    inv_freq = 1.0 / (ROPE_THETA ** (jnp.arange(0, HEAD_DIM, 2, dtype=F32) / HEAD_DIM))
    inv_freq = jnp.tile(inv_freq, LANES // (HEAD_DIM // 2)).reshape(1, LANES) + after[0, 0]
    return _rope_tables(positions.reshape(-1, 1), inv_freq)


def _local_step(x, rope, target, norms, a_sink, comm):
    T, D = x.shape
    g1, gm, g2, gf = norms
    cos, sin = rope
    no_sink = jnp.zeros((2 * (B_W // LANES),), F32)
    W = {k: comm.weight(k, x) for k in ("wg1", "wu1")}

    h1, gate1, up1, act1 = _ffn_up(x, g1, W["wg1"], W["wu1"], "ffn1_up", dep=comm.dep())
    W["wd1"] = comm.weight("wd1", act1)
    x1 = _ffn_down(x, act1, W["wd1"], "ffn1_down")
    W["w_in"] = comm.weight("w_in", x1)
    (h2, aq, ak, av, bq1, bk1, bv1, bq4, bk4, bv4, bq16, bk16, bv16) = _proj_rope(x1, gm, W["w_in"], cos, sin)
    cat, a_lse = _attn_fwd(aq[None], ak[None], av[None], a_sink, A_HALF_WINDOW, True, BF16, "attn_a_fwd", qb=2 * QB, blocks_per_step=4,
                           out_cols=A_Q_W + B_W)
    bqs = {1: (bq1[None], bk1[None], bv1[None]), 4: (bq4, bk4, bv4), 16: (bq16, bk16, bv16)}
    (b_hw,) = {w // (2 * d) for w, d in B_PATTERNS}
    cat, lg1, lg4, lg16 = _dilated_fwd(cat[0], bqs, b_hw)
    lg1 = lg1[0]
    W["w_out"] = comm.weight("w_out", cat)
    x2 = _out_proj(x1, cat, W["w_out"])
    for k in ("wg2", "wu2", "wd2"):
        W[k] = comm.weight(k, x2)
    dx3, h3, gate2, up2, act2, dgf, loss8 = _ffn_loss(x2, g2, W["wg2"], W["wu2"], W["wd2"], gf, target, "ffn2_fwd")

    dx2, dff2, dgate2, dup2, dg2 = _ffn_dx(dx3, x2, g2, gate2, up2, W["wg2"], W["wu2"], W["wd2"], "ffn2_dx")
    fb = gate2.shape[1] // 2
    dwg2 = _tn(dgate2, h3, fb, "ffn2_dw_gate")
    dwu2 = _tn(dup2, h3, fb, "ffn2_dw_up")
    dwd2 = _tn(act2, dff2, fb, "ffn2_dw_down")
    comm.ready(dict(wg2=dwg2, wu2=dwu2, wd2=dwd2), dwd2[0])

    doa, dla, dob1, dlb1, dob4, dlb4, dob16, dlb16, dw_out = _dcat(dx2, W["w_out"], cat, dep=comm.dep())
    dqa, dka, dva, dsk = _attn_bwd(aq[None], ak[None], av[None], doa[None], a_lse, dla[None], a_sink, A_HALF_WINDOW, True, "attn_a_bwd")
    bwd_in = {1: (dob1[None], lg1[None], dlb1[None]), 4: (dob4, lg4, dlb4), 16: (dob16, lg16, dlb16)}
    bg = {}
    for w, d in B_PATTERNS:
        q_, k_, v_ = bqs[d]
        do_, l_, dl_ = bwd_in[d]
        bg[d] = _attn_bwd(q_, k_, v_, do_, l_, dl_, no_sink, w // (2 * d), False, f"attn_b{d}_bwd")[:3]
    dproj, dx1, dgm = _mixer_in_bwd(dqa[0], dka[0], dva[0], [t[0] for t in bg[1]], bg[4], bg[16], cos, sin, W["w_in"], x1, gm, dx2)
    dw_in = _tn(dproj, h2, dproj.shape[1] // 2, "w_in_dw")
    comm.ready(dict(w_in=dw_in, w_out=dw_out), dw_in[0])

    dx0, dff1, dgate1, dup1, dg1 = _ffn_dx(dx1, x, g1, gate1, up1, W["wg1"], W["wu1"], W["wd1"], "ffn1_dx", dep=comm.dep())
    comm.settle(2, dx0)
    dwd1 = _tn(act1, dff1, fb, "ffn1_dw_down", dep=comm.dep())
    comm.ready(dict(wd1=dwd1), dwd1[0])
    dwg1 = _tn(dgate1, h1, fb, "ffn1_dw_gate", dep=comm.dep())
    comm.ready(dict(wg1=dwg1), dwg1[0])
    dwu1 = _tn(dup1, h1, fb, "ffn1_dw_up", dep=comm.dep())
    comm.ready(dict(wu1=dwu1), dwu1[0])

    dsink = dsk[0, :, :, ::HEAD_DIM].sum(axis=1).reshape(-1)
    small = dict(g1=dg1.sum(axis=0), gm=dgm.sum(axis=0), g2=dg2.sum(axis=0), gf=dgf.sum(axis=0), sink=dsink, loss=loss8[0, 0])
    return dx0, small


BIG = ("wg1", "wu1", "wd1", "w_in", "w_out", "wg2", "wu2", "wd2")
GATHER_GROUPS = (("wd1",), ("w_in",), ("w_out",), ("wg2", "wu2", "wd2"))


class _Comm:
    def __init__(self, shards, meanwhile):
        x, y, c = _mesh_pos()
        self.me = (2 * x + y).astype(jnp.int32).reshape(1)
        self.shards = shards
        self.token = None
        self.waiting = {}
        self.groups = []
        self.swaps = []
        first = ("wg1", "wu1")
        fulls ={k: _cast_place(self.me, shards[k], f"cast_{k}") for k in first}
        plan = _neighbour_plan([fulls[k].shape for k in first])
        send, recv, bufs, tok = _push_start("gather_first_start", [fulls[k] for k in first], 2 * len(first), plan, self.me)
        self.side = meanwhile(tok)
        fulls.update({k: _cast_place(self.me, shards[k], f"cast_{k}") for k in BIG if k not in first})
        bufs = _push_wait("gather_first_wait", send, recv, bufs, plan, [fulls[k] for k in BIG if k not in first] + list(self.side))
        self.full = dict(zip(first, _gather_forward(bufs)))
        rest = [k for names in GATHER_GROUPS for k in names]
        send, recv, bufs, self.token = _push_start("gather_rest_start", [fulls[k] for k in rest], 3 * len(rest), _gather_plan(len(rest)),
                                                   self.full[first[-1]])
        self.rest = dict(zip(rest, bufs))
        for gi, names in enumerate(GATHER_GROUPS):
            for k in names:
                self.waiting[k] = (gi, names, send, recv, 3 * rest.index(names[0]))

    def dep(self):
        return self.token

    def weight(self, name, after):
        if name in self.waiting:
            gi, names, send, recv, first = self.waiting[name]
            bufs = [self.rest[k] for k in names]
            for k, buf in zip(names, _push_wait(f"gather_wait_{gi}", send, recv, bufs, _gather_plan(len(names)), after, first)):
                self.full[k] = buf
                del self.waiting[k]
        full = self.full[name]
        return full.reshape(N_CHIPS * full.shape[1], full.shape[2])

    def ready(self, grads, after):
        names = list(grads)
        f32s, b16s = [], []
        for k in names:
            gf, gb = grads[k]
            f32s.append(gf.reshape((N_CHIPS,) + self.shards[k].shape))
            b16s.append(gb.reshape((N_CHIPS,) + self.shards[k].shape))
        n = len(names)
        lands = [lax.empty((N_CHIPS - 1,) + self.shards[k].shape, BF16) for k in names]
        plan = _grad_push_plan(n)
        send, recv, bufs, self.token = _push_start(f"grad_start_{names[0]}", b16s + lands, 3 * n, plan, after)
        self.groups.append((names, f32s, send, recv, bufs, plan))

    def settle(self, count, after):
        batch, self.groups = self.groups[:count], self.groups[count:]
        names_b, mine_b = [], []
        for names, f32s, send, recv, bufs, plan in batch:
            n = len(names)
            bufs = _push_wait(f"grad_wait_{names[0]}", send, recv, bufs, plan, mine_b[-1] if mine_b else after)
            mine_b += [_sum_own(self.me, f32s[i], bufs[n + i], f"sum_{k}") for i, k in enumerate(names)]
            names_b += names
        lands = [lax.empty(p.shape, F32) for p in mine_b]
        n = len(names_b)
        send2, recv2, both, self.token = _push_start(f"swap_start_{names_b[0]}", mine_b + lands, n, _swap_plan(n), after)
        self.swaps.append((names_b, send2, recv2, both))

    def partials(self, after):
        names_b, send2, recv2, both = self.swaps.pop(0)
        n = len(names_b)
        both = _push_wait(f"swap_wait_{names_b[0]}", send2, recv2, both, _swap_plan(n), after)
        return {k: (both[i], both[n + i]) for i, k in enumerate(names_b)}


def kernel(x, positions, norm_ffn1, w_gate1, w_up1, w_down1, norm_mix, w_in, a_sink, w_out, norm_ffn2, w_gate2, w_up2, w_down2, norm_final, loss_target, m_norm_ffn1, m_w_gate1, m_w_up1, m_w_down1, m_norm_mix, m_w_in, m_a_sink, m_w_out, m_norm_ffn2, m_w_gate2, m_w_up2, m_w_down2, m_norm_final, v_norm_ffn1, v_w_gate1, v_w_up1, v_w_down1, v_norm_mix, v_w_in, v_a_sink, v_w_out, v_norm_ffn2, v_w_gate2, v_w_up2, v_w_down2, v_norm_final):
    T, D = x.shape[1], x.shape[2]
    flip = ("wg1", "wu1", "w_in", "wg2", "wu2")

    def rows(k, a):
        return a[0].T if k in flip else a[0]

    given = dict(wg1=(w_gate1, m_w_gate1, v_w_gate1), wu1=(w_up1, m_w_up1, v_w_up1), wd1=(w_down1, m_w_down1, v_w_down1),
                 w_in=(w_in, m_w_in, v_w_in), w_out=(w_out, m_w_out, v_w_out), wg2=(w_gate2, m_w_gate2, v_w_gate2),
                 wu2=(w_up2, m_w_up2, v_w_up2), wd2=(w_down2, m_w_down2, v_w_down2))
    shards = {k: rows(k, given[k][0]) for k in BIG}

    comm = _Comm(shards, lambda tok: _rope(positions[0], tok))

    norms = (norm_ffn1, norm_mix, norm_ffn2, norm_final.reshape(1, D))
    grad_x, small = _local_step(x[0], comm.side, loss_target[0], norms, a_sink[0], comm)

    upd = {}

    def update(partial):
        for k in partial:
            outs = _adamw(shards[k], partial[k][0], partial[k][1], rows(k, given[k][1]), rows(k, given[k][2]), f"adamw_{k}")
            upd[k] = tuple((a.T if k in flip else a)[None] for a in outs)
        return outs[0]

    last = update(comm.partials(comm.dep()))
    comm.settle(2, last)

    def pad_row(a):
        a = a.reshape(-1)
        return jnp.pad(a, (0, D - a.shape[0]))

    row4 = pad_row(jnp.concatenate([small["sink"], small["loss"].reshape(1)]))
    vec = jnp.stack([small["g1"], small["gm"], small["g2"], small["gf"], row4] + [jnp.zeros((D,), F32)] * 3, axis=0)
    red = _allreduce_small(vec, comm.dep())
    comm.settle(1, red)
    last = update(comm.partials(comm.dep()))
    update(comm.partials(last))
    as_row = lambda a: a.reshape(1, -1)
    sm, loss = _adamw_small(red, [tuple(as_row(a) for a in p) for p in (
        (norm_ffn1, m_norm_ffn1, v_norm_ffn1), (norm_mix, m_norm_mix, v_norm_mix), (norm_ffn2, m_norm_ffn2, v_norm_ffn2),
        (norm_final, m_norm_final, v_norm_final), (a_sink, m_a_sink, v_a_sink))])
    sm[3] = [a.reshape(D) for a in sm[3]]

    def ordered(i):
        return [sm[0][i], upd["wg1"][i], upd["wu1"][i], upd["wd1"][i], sm[1][i], upd["w_in"][i], sm[4][i], upd["w_out"][i], sm[2][i],
                upd["wg2"][i], upd["wu2"][i], upd["wd2"][i], sm[3][i]]

    return (loss.reshape(()), grad_x[None], *ordered(0), *ordered(1), *ordered(2), *ordered(3))
```

```python
import jax
import jax.numpy as jnp
from jax import lax
from jax.experimental import pallas as pl
from jax.experimental.pallas import tpu as pltpu

F32 = jnp.float32
BF16 = jnp.bfloat16

HEAD_DIM = 64
LANES = 128
SUBLANES = 8
A_Q_W, A_KV_W, B_W = 512, 128, 512
A_HALF_WINDOW = 128
B_PATTERNS = ((128, 1), (512, 4), (2048, 16))
ROPE_THETA = 10000.0
NORM_EPS = 1e-6
FFN_RES_WEIGHT = 0.5
ADAM_LR, ADAM_B1, ADAM_B2, ADAM_EPS, ADAM_WD, ADAM_STEP = 0.001, 0.9, 0.999, 1e-08, 0.01, 10
N_CHIPS = 4
N_DEV = 8
QB = 128
SHORT_SEQ = 512
NEG = -1e30
VMEM_LIMIT = 56 * 1024 * 1024
MESH = pl.DeviceIdType.MESH
ANY = pl.BlockSpec(memory_space=pl.ANY)


def _params(sem=None):
    return pltpu.CompilerParams(dimension_semantics=sem, vmem_limit_bytes=VMEM_LIMIT)


def _sds(shape, dtype):
    return jax.ShapeDtypeStruct(tuple(shape), dtype)


def _dot(a, b):
    return jnp.dot(a, b, preferred_element_type=F32)


def _dot_nt(a, b):
    return lax.dot_general(a, b, (((1,), (1,)), ((), ())), preferred_element_type=F32)


def _dot_tn(a, b):
    return lax.dot_general(a, b, (((0,), (0,)), ((), ())), preferred_element_type=F32)


def _rms_stats(x):
    r = lax.rsqrt(jnp.mean(x * x, axis=-1, keepdims=True) + NORM_EPS)
    return x * r, r


def _rms_bwd(dh, x, g):
    xhat, r = _rms_stats(x)
    dxn = dh * g
    dx = r * (dxn - xhat * jnp.mean(dxn * xhat, axis=-1, keepdims=True))
    tm, d = x.shape
    dg = (dh * xhat).reshape(tm // SUBLANES, SUBLANES, d).sum(axis=0)
    return dx, dg


def _sigmoid(x):
    return 1.0 / (1.0 + jnp.exp(-x))


def _swap32(t):
    n = t.shape[-1]
    lane = lax.broadcasted_iota(jnp.int32, t.shape, t.ndim - 1)
    return jnp.where((lane % HEAD_DIM) < HEAD_DIM // 2, pltpu.roll(t, n - HEAD_DIM // 2, axis=t.ndim - 1),
                     pltpu.roll(t, HEAD_DIM // 2, axis=t.ndim - 1))


def _ordered(body, n_in, dep):
    if dep is None:
        return body, [], []

    def ordered(*refs):
        body(*refs[:n_in], *refs[n_in + 1:])

    return ordered, [ANY], [dep]


def _cast_place(me_arr, w, name):
    R, C = w.shape
    tr = R // 2 if (R // 2) % 16 == 0 else R

    def body(me_ref, w_ref, o_ref):
        o_ref[...] = w_ref[...].astype(BF16)

    grid_spec = pltpu.PrefetchScalarGridSpec(
        num_scalar_prefetch=1, grid=(R // tr,), in_specs=[pl.BlockSpec((tr, C), lambda t, me: (t, 0))],
        out_specs=pl.BlockSpec((None, tr, C), lambda t, me: (me[0], t, 0)))
    return pl.pallas_call(body, name=name, grid_spec=grid_spec, out_shape=_sds((N_CHIPS, R, C), BF16),
                          compiler_params=_params(("parallel",)))(me_arr, w)


HBM = pl.BlockSpec(memory_space=pltpu.HBM)
SEM = pl.BlockSpec(memory_space=pltpu.SEMAPHORE)


def _push_start(name, bufs, ncopies, plan, after):
    nb = len(bufs)

    def body(*refs):
        send, recv, token = refs[nb + 1], refs[nb + 2], refs[-1]
        for i, (src, dst, dev) in enumerate(plan(refs[:nb])):
            pltpu.make_async_remote_copy(src_ref=src, dst_ref=dst, send_sem=send.at[i], recv_sem=recv.at[i],
                                         device_id=dev, device_id_type=MESH).start()
        token[...] = jnp.zeros_like(token)

    outs = pl.pallas_call(
        body, name=name,
        out_shape=(pltpu.SemaphoreType.DMA((ncopies,)), pltpu.SemaphoreType.DMA((ncopies,)), *[pltpu.HBM(b.shape, b.dtype) for b in bufs],
                   _sds((SUBLANES, LANES), F32)),
        in_specs=[HBM] * nb + [ANY], out_specs=(SEM, SEM, *([HBM] * nb), pl.BlockSpec(memory_space=pltpu.VMEM)),
        input_output_aliases={i: 2 + i for i in range(nb)},
        compiler_params=pltpu.CompilerParams(has_side_effects=pltpu.SideEffectType.DATAFLOW_SIDE_EFFECTING),
    )(*[pltpu.with_memory_space_constraint(b, pltpu.HBM) for b in bufs], after)
    return outs[0], outs[1], list(outs[2:2 + nb]), outs[-1]


def _push_wait(name, send, recv, bufs, plan, after, first=0):
    nb = len(bufs)

    def body(*refs):
        send_ref, recv_ref = refs[nb], refs[nb + 1]
        for i, (src, dst, dev) in enumerate(plan(refs[:nb])):
            cp = pltpu.make_async_remote_copy(src_ref=src, dst_ref=dst, send_sem=send_ref.at[first + i], recv_sem=recv_ref.at[first + i],
                                              device_id=dev, device_id_type=MESH)
            cp.wait_send()
            cp.wait_recv()

    afters = list(after) if isinstance(after, (list, tuple)) else [after]
    outs = pl.pallas_call(
        body, name=name, out_shape=tuple(pltpu.HBM(b.shape, b.dtype) for b in bufs),
        in_specs=[HBM] * nb + [SEM, SEM] + [ANY] * len(afters), out_specs=tuple([HBM] * nb),
        input_output_aliases={i: i for i in range(nb)},
        compiler_params=pltpu.CompilerParams(has_side_effects=pltpu.SideEffectType.DATAFLOW_SIDE_EFFECTING),
    )(*bufs, send, recv, *afters)
    return list(outs)


def _mesh_pos():
    return lax.axis_index("x"), lax.axis_index("y"), lax.axis_index("c")


def _chip_peers(x, y, c):
    return [((1 - x, y, c), 2 * (1 - x) + y), ((x, 1 - y, c), 2 * x + (1 - y)), ((1 - x, 1 - y, c), 2 * (1 - x) + (1 - y))]


def _gather_plan(n):
    def plan(refs):
        x, y, c = _mesh_pos()
        me = 2 * x + y
        return [(refs[k].at[me], refs[k].at[me], dev) for k in range(n) for dev, _ in _chip_peers(x, y, c)]
    return plan


def _rows_of(shape, who, quarter=None):
    r2 = shape[1] // 2
    if quarter is None:
        return pl.ds(pl.multiple_of(who * r2, 16), r2)
    return pl.ds(pl.multiple_of(who * r2 + quarter * (r2 // 2), 16), r2 // 2)


def _neighbour_plan(shapes):
    def plan(refs):
        x, y, c = _mesh_pos()
        me = 2 * x + y
        return [(refs[k].at[me, _rows_of(shp, c), :], refs[k].at[me, _rows_of(shp, c), :], dev)
                for k, shp in enumerate(shapes) for dev in ((1 - x, y, c), (x, 1 - y, c))]
    return plan


def _gather_forward(fulls):
    n = len(fulls)

    def body(*refs):
        ins, outs = refs[:n], refs[n:2 * n]
        ici_send, ici_recv, d2d_send, d2d_recv = refs[2 * n:]
        x, y, c = _mesh_pos()
        cx, cy, cd = 2 * (1 - x) + y, 2 * x + (1 - y), 2 * (1 - x) + (1 - y)
        sibling, x_nbr, y_nbr = (x, y, 1 - c), (1 - x, y, c), (x, 1 - y, c)
        started = []

        def push(src, dst, send, recv, dev):
            cp = pltpu.make_async_remote_copy(src_ref=src, dst_ref=dst, send_sem=send, recv_sem=recv, device_id=dev, device_id_type=MESH)
            cp.start()
            started.append(cp)

        def arrived(blk, send, recv):
            pltpu.make_async_remote_copy(src_ref=blk, dst_ref=blk, send_sem=send, recv_sem=recv, device_id=sibling,
                                         device_id_type=MESH).wait_recv()

        for k in range(n):
            shp = fulls[k].shape
            for j, chip in enumerate((cx, cy)):
                push(ins[k].at[chip, _rows_of(shp, c), :], outs[k].at[chip, _rows_of(shp, c), :],
                     d2d_send.at[3 * k + j], d2d_recv.at[3 * k + j], sibling)
            push(ins[k].at[cx, _rows_of(shp, c, 0), :], outs[k].at[cx, _rows_of(shp, c, 0), :], ici_send.at[2 * k], ici_recv.at[2 * k], y_nbr)
            push(ins[k].at[cy, _rows_of(shp, c, 1), :], outs[k].at[cy, _rows_of(shp, c, 1), :], ici_send.at[2 * k + 1], ici_recv.at[2 * k + 1],
                 x_nbr)
        for k in range(n):
            shp = fulls[k].shape
            for q in (0, 1):
                arrived(outs[k].at[cd, _rows_of(shp, c, q), :], ici_send.at[2 * k + q], ici_recv.at[2 * k + q])
            blk = outs[k].at[cd, _rows_of(shp, c), :]
            push(blk, blk, d2d_send.at[3 * k + 2], d2d_recv.at[3 * k + 2], sibling)
        for k in range(n):
            for j, chip in enumerate((cx, cy, cd)):
                arrived(outs[k].at[chip, _rows_of(fulls[k].shape, 1 - c), :], d2d_send.at[3 * k + j], d2d_recv.at[3 * k + j])
        for cp in started:
            cp.wait_send()

    return pl.pallas_call(
        body, name="gather_forward", out_shape=[_sds(f.shape, BF16) for f in fulls],
        in_specs=[ANY] * n, out_specs=[ANY] * n, input_output_aliases={k: k for k in range(n)},
        scratch_shapes=[pltpu.SemaphoreType.DMA((n * 2,))] * 2 + [pltpu.SemaphoreType.DMA((n * 3,))] * 2,
        compiler_params=_params())(*fulls)


def _resident(shape):
    return pl.BlockSpec(shape, lambda i: (0,) * len(shape), pipeline_mode=pl.Buffered(1))


FFN_FWD_CHUNK = 256
FFN_DX_CHUNK = 512


def _chunks(n, step):
    return [(c0, min(step, n - c0)) for c0 in range(0, n, step)]


def _two_phase(chunks, first, second):
    held = {}
    for ci, ch in enumerate(chunks):
        held[ci] = first(*ch)
        if ci >= 1:
            second(*chunks[ci - 1], held.pop(ci - 1))
    last = len(chunks) - 1
    second(*chunks[last], held.pop(last))


def _loss_and_grad(x, g, target):
    D = x.shape[1]
    xhat, _ = _rms_stats(x)
    err = xhat * g - target
    loss = 0.5 * jnp.sum(jnp.sum(err * err, axis=-1, keepdims=True) * (1.0 / D), axis=0, keepdims=True)
    dx, dg = _rms_bwd(err * (1.0 / D), x, g)
    return dx, dg, loss


def _ffn_loss(x, g, wgt, wut, wd, gf, target, name, tm=512):
    T, D = x.shape
    F = wd.shape[0]

    def body(x_ref, g_ref, wg_ref, wu_ref, wd_ref, gf_ref, t_ref, dy_ref, h_ref, gate_ref, up_ref, act_ref, dgf_ref, loss_ref):
        @pl.when(pl.program_id(0) == 0)
        def _():
            dgf_ref[...] = jnp.zeros_like(dgf_ref)
            loss_ref[...] = jnp.zeros_like(loss_ref)

        xv = x_ref[...]
        xhat, _ = _rms_stats(xv)
        h = (xhat * g_ref[...]).astype(BF16)
        h_ref[...] = h
        acc = []

        def first(c0, cw):
            return _dot_nt(h, wg_ref[c0:c0 + cw, :]), _dot_nt(h, wu_ref[c0:c0 + cw, :])

        def second(c0, cw, gate_up):
            gate, up = gate_up
            act = ((gate * _sigmoid(gate)) * up).astype(BF16)
            gate_ref[:, c0:c0 + cw] = gate.astype(BF16)
            up_ref[:, c0:c0 + cw] = up.astype(BF16)
            act_ref[:, c0:c0 + cw] = act
            d = _dot(act, wd_ref[c0:c0 + cw, :])
            acc[:] = [d if not acc else acc[0] + d]

        _two_phase(_chunks(F, FFN_FWD_CHUNK), first, second)
        dy_ref[...], dgf, part = _loss_and_grad(xv + FFN_RES_WEIGHT * acc[0], gf_ref[...], t_ref[...])
        dgf_ref[...] += dgf
        loss_ref[...] += part

    row = pl.BlockSpec((tm, D), lambda i: (i, 0))
    gain = pl.BlockSpec((1, D), lambda i: (0, 0))
    saved = pl.BlockSpec((tm, F), lambda i: (i, 0))
    return pl.pallas_call(
        body, name=name, grid=(T // tm,),
        in_specs=[row, gain, _resident(wgt.shape), _resident(wut.shape), _resident(wd.shape), gain, row],
        out_specs=[row, row, saved, saved, saved, pl.BlockSpec((SUBLANES, D), lambda i: (0, 0)), pl.BlockSpec((SUBLANES, LANES), lambda i: (0, 0))],
        out_shape=[_sds((T, D), F32), _sds((T, D), BF16), _sds((T, F), BF16), _sds((T, F), BF16), _sds((T, F), BF16),
                   _sds((SUBLANES, D), F32), _sds((SUBLANES, LANES), F32)],
        compiler_params=_params(("arbitrary",)))(x, g, wgt, wut, wd, gf, target)


def _ffn_up(x, g, wgt, wut, name, tm=512, dep=None):
    T, D = x.shape
    F = wgt.shape[0]

    def body(x_ref, g_ref, wg_ref, wu_ref, h_ref, gate_ref, up_ref, act_ref):
        xhat, _ = _rms_stats(x_ref[...])
        h = (xhat * g_ref[...]).astype(BF16)
        h_ref[...] = h

        def first(c0, cw):
            return _dot_nt(h, wg_ref[c0:c0 + cw, :]), _dot_nt(h, wu_ref[c0:c0 + cw, :])

        def second(c0, cw, gate_up):
            gate, up = gate_up
            gate_ref[:, c0:c0 + cw] = gate.astype(BF16)
            up_ref[:, c0:c0 + cw] = up.astype(BF16)
            act_ref[:, c0:c0 + cw] = ((gate * _sigmoid(gate)) * up).astype(BF16)

        _two_phase(_chunks(F, FFN_FWD_CHUNK), first, second)

    row = pl.BlockSpec((tm, D), lambda i: (i, 0))
    saved = pl.BlockSpec((tm, F), lambda i: (i, 0))
    body, dep_spec, dep_arg = _ordered(body, 4, dep)
    return pl.pallas_call(
        body, name=name, grid=(T // tm,),
        in_specs=[row, pl.BlockSpec((1, D), lambda i: (0, 0)), _resident(wgt.shape), _resident(wut.shape)] + dep_spec,
        out_specs=[row, saved, saved, saved],
        out_shape=[_sds((T, D), BF16), _sds((T, F), BF16), _sds((T, F), BF16), _sds((T, F), BF16)],
        compiler_params=_params(("parallel",)))(x, g, wgt, wut, *dep_arg)


def _ffn_down(x, act, wd, name, tm=512):
    T, D = x.shape
    F = wd.shape[0]

    def body(x_ref, a_ref, wd_ref, xo_ref):
        xo_ref[...] = x_ref[...] + FFN_RES_WEIGHT * _dot(a_ref[...], wd_ref[...])

    row = pl.BlockSpec((tm, D), lambda i: (i, 0))
    return pl.pallas_call(
        body, name=name, grid=(T // tm,), in_specs=[row, pl.BlockSpec((tm, F), lambda i: (i, 0)), _resident(wd.shape)],
        out_specs=row, out_shape=_sds((T, D), F32), compiler_params=_params(("parallel",)))(x, act, wd)


def _ffn_dx(dxo, x, g, gate_s, up_s, wgt, wut, wd, name, tm=256, dep=None):
    T, D = x.shape
    F = wd.shape[0]

    def body(dxo_ref, x_ref, g_ref, gate_ref, up_ref, wg_ref, wu_ref, wd_ref, dx_ref, dff_ref, dgate_ref, dup_ref, dg_ref):
        @pl.when(pl.program_id(0) == 0)
        def _():
            dg_ref[...] = jnp.zeros_like(dg_ref)

        d = (FFN_RES_WEIGHT * dxo_ref[...]).astype(BF16)
        dff_ref[...] = d
        dh = []

        def first(c0, cw):
            return _dot_nt(d, wd_ref[c0:c0 + cw, :])

        def second(c0, cw, da):
            gate = gate_ref[:, c0:c0 + cw].astype(F32)
            up = up_ref[:, c0:c0 + cw].astype(F32)
            s = _sigmoid(gate)
            silu = gate * s
            dup = (da * silu).astype(BF16)
            dgate = (da * up * (s * (1.0 + gate * (1.0 - s)))).astype(BF16)
            dgate_ref[:, c0:c0 + cw] = dgate
            dup_ref[:, c0:c0 + cw] = dup
            t = _dot(dgate, wg_ref[c0:c0 + cw, :]) + _dot(dup, wu_ref[c0:c0 + cw, :])
            dh[:] = [t if not dh else dh[0] + t]

        _two_phase(_chunks(F, FFN_DX_CHUNK), first, second)
        dxn, dg = _rms_bwd(dh[0], x_ref[...], g_ref[...])
        dg_ref[...] += dg
        dx_ref[...] = dxo_ref[...] + dxn

    row = pl.BlockSpec((tm, D), lambda i: (i, 0))
    saved = pl.BlockSpec((tm, F), lambda i: (i, 0))
    body, dep_spec, dep_arg = _ordered(body, 8, dep)
    return pl.pallas_call(
        body, name=name, grid=(T // tm,),
        in_specs=[row, row, pl.BlockSpec((1, D), lambda i: (0, 0)), saved, saved, _resident(wgt.shape), _resident(wut.shape),
                  _resident(wd.shape)] + dep_spec,
        out_specs=[row, row, saved, saved, pl.BlockSpec((SUBLANES, D), lambda i: (0, 0))],
        out_shape=[_sds((T, D), F32), _sds((T, D), BF16), _sds((T, F), BF16), _sds((T, F), BF16), _sds((SUBLANES, D), F32)],
        compiler_params=_params(("arbitrary",)))(dxo, x, g, gate_s, up_s, wgt, wut, wd, *dep_arg)


def _tn(a, b, mb, name, tk=2048, dep=None):
    T, M = a.shape
    N = b.shape[1]
    nt = T // tk

    def body(a_ref, b_ref, o_ref, ob_ref):
        @pl.when(pl.program_id(1) == 0)
        def _():
            o_ref[...] = jnp.zeros_like(o_ref)

        o_ref[...] += _dot_tn(a_ref[...].astype(BF16), b_ref[...].astype(BF16))

        @pl.when(pl.program_id(1) == nt - 1)
        def _():
            ob_ref[...] = o_ref[...].astype(BF16)

    o_spec = pl.BlockSpec((mb, N), lambda g, t: (g, 0))
    body, dep_spec, dep_arg = _ordered(body, 2, dep)
    return pl.pallas_call(
        body, name=name, grid=(M // mb, nt),
        in_specs=[pl.BlockSpec((tk, mb), lambda g, t: (t, g)), pl.BlockSpec((tk, N), lambda g, t: (t, 0))] + dep_spec,
        out_specs=[o_spec, o_spec], out_shape=[_sds((M, N), F32), _sds((M, N), BF16)],
        compiler_params=_params(("parallel", "arbitrary")))(a, b, *dep_arg)


def _rope_tables(pos_col, inv_freq):
    T = pos_col.shape[0]

    def body(p_ref, f_ref, c_ref, s_ref):
        ang = p_ref[...].astype(F32) * f_ref[...]
        lane = lax.broadcasted_iota(jnp.int32, ang.shape, 1)
        c_ref[...] = jnp.cos(ang)
        sn = jnp.sin(ang)
        s_ref[...] = jnp.where((lane % HEAD_DIM) < HEAD_DIM // 2, -sn, sn)

    tm = 1024
    return pl.pallas_call(
        body, name="rope_tables", grid=(T // tm,),
        in_specs=[pl.BlockSpec((tm, 1), lambda i: (i, 0)), pl.BlockSpec((1, LANES), lambda i: (0, 0))],
        out_specs=[pl.BlockSpec((tm, LANES), lambda i: (i, 0))] * 2,
        out_shape=[_sds((T, LANES), F32)] * 2, compiler_params=_params(("parallel",)))(pos_col, inv_freq)


def _deinterleave(scr, out_ref, d, tm, nblk):
    for r in range(d):
        for cb in range(nblk):
            out_ref[r, :, cb * LANES:(cb + 1) * LANES] = scr[cb, pl.ds(r, tm // d, stride=d), :].astype(out_ref.dtype)


def _interleave(in_ref, scr, d, tm, nblk):
    for r in range(d):
        for cb in range(nblk):
            scr[cb, pl.ds(r, tm // d, stride=d), :] = in_ref[r, :, cb * LANES:(cb + 1) * LANES].astype(F32)


def _proj_rope(x, g, w_in, cos, sin, tm=512):
    T, D = x.shape
    dils = [d for _, d in B_PATTERNS if d > 1]
    nbb = B_W // LANES
    scale = HEAD_DIM ** -0.5
    cuts = [0, A_Q_W, A_Q_W + A_KV_W, A_Q_W + 2 * A_KV_W, A_Q_W + 2 * A_KV_W + B_W, A_Q_W + 2 * A_KV_W + 2 * B_W,
            A_Q_W + 2 * A_KV_W + 3 * B_W]

    def body(x_ref, g_ref, w_ref, c_ref, s_ref, h_ref, aq_ref, ak_ref, av_ref, *rest):
        b_refs, scr = rest[:-1], rest[-1]
        xhat, _ = _rms_stats(x_ref[...])
        h = (xhat * g_ref[...]).astype(BF16)
        h_ref[...] = h
        cs, sn = c_ref[...], s_ref[...]

        def project(idx, ref, rope, mult, which):
            return _dot_nt(h, w_ref[cuts[idx]:cuts[idx + 1], :])

        def finish(idx, ref, rope, mult, which, whole):
            for cb in range((cuts[idx + 1] - cuts[idx]) // LANES):
                p = whole[:, cb * LANES:(cb + 1) * LANES]
                if rope:
                    p = p * cs + _swap32(p) * sn
                if mult != 1.0:
                    p = p * mult
                ref[:, cb * LANES:(cb + 1) * LANES] = p.astype(BF16)
                if which is not None:
                    scr[which, cb] = p
            if which is not None:
                for di, d in enumerate(dils):
                    _deinterleave(scr.at[which], b_refs[3 * (di + 1) + which], d, tm, nbb)

        _two_phase([(0, aq_ref, True, scale, None), (1, ak_ref, True, 1.0, None), (2, av_ref, False, 1.0, None),
                    (3, b_refs[0], True, scale, 0), (4, b_refs[1], True, 1.0, 1), (5, b_refs[2], False, 1.0, 2)], project, finish)

    row = lambda w: pl.BlockSpec((tm, w), lambda i: (i, 0))
    out_specs = [row(D), row(A_Q_W), row(A_KV_W), row(A_KV_W)] + [row(B_W)] * 3
    out_shape = [_sds((T, D), BF16), _sds((T, A_Q_W), BF16), _sds((T, A_KV_W), BF16), _sds((T, A_KV_W), BF16)] + [_sds((T, B_W), BF16)] * 3
    for d in dils:
        out_specs += [pl.BlockSpec((d, tm // d, B_W), lambda i: (0, i, 0))] * 3
        out_shape += [_sds((d, T // d, B_W), BF16)] * 3
    return pl.pallas_call(
        body, name="proj_rope", grid=(T // tm,),
        in_specs=[row(D), pl.BlockSpec((1, D), lambda i: (0, 0)), pl.BlockSpec(w_in.shape, lambda i: (0, 0)), row(LANES), row(LANES)],
        out_specs=out_specs, out_shape=out_shape, scratch_shapes=[pltpu.VMEM((3, nbb, tm, LANES), F32)],
        compiler_params=_params(("parallel",)))(x, g, w_in, cos, sin)


def _band_bias(rel, qb, kw, hw):
    ri = lax.broadcasted_iota(jnp.int32, (2 * qb, kw), 0) & (qb - 1)
    ci = lax.broadcasted_iota(jnp.int32, (2 * qb, kw), 1)
    return jnp.where(jnp.abs(ri + rel - ci) <= hw, 0.0, NEG).astype(F32)


def _stack_heads(x, lo):
    z = jnp.zeros_like(x)
    return jnp.concatenate([jnp.where(lo, x, z), jnp.where(lo, z, x)], axis=0)


def _unstack_heads(y, lo):
    qb = y.shape[0] // 2
    return jnp.where(lo, y[:qb], y[qb:])


def _band_setup(bias_scr, qb, kw, hw):
    if bias_scr is not None:
        for i in range(3):
            bias_scr[i] = _band_bias(i * hw, qb, kw, hw)


def _band_window(bias_scr, qs, L, qb, kw, hw):
    ws = pl.multiple_of(jnp.clip(qs - hw, 0, L - kw), 64)
    if bias_scr is None:
        return ws, _band_bias(qs - ws, qb, kw, hw)
    return ws, bias_scr[lax.shift_right_logical(qs - ws, hw.bit_length() - 1)]


def _dup_kv_head(src_ref, dst_ref, head, L):
    step = min(L, 1024)
    for r0 in range(0, L, step):
        xf = src_ref[r0:r0 + step, :].astype(F32)
        lane = lax.broadcasted_iota(jnp.int32, xf.shape, 1)
        keep = jnp.logical_xor(lane < HEAD_DIM, head == 1)
        dst_ref[r0:r0 + step, :] = jnp.where(keep, xf, pltpu.roll(xf, HEAD_DIM, axis=1)).astype(dst_ref.dtype)


def _attn_fwd(q, k, v, sink, hw, gqa, out_dtype, name, qb=QB, blocks_per_step=8, out_cols=None):
    NB, L, Cq = q.shape
    Ls = min(L, 2048)
    kw = min(qb + 2 * hw, L)
    tables = L >= qb + 2 * hw
    unroll = min(blocks_per_step, Ls // qb)
    nlb = 1 if (gqa or L > SHORT_SEQ) else Cq // LANES

    def body(sink_ref, q_ref, k_ref, v_ref, o_ref, lse_ref, *scr):
        b, s_idx = pl.program_id(1), pl.program_id(2)
        bias_scr = scr[0] if tables else None
        _band_setup(bias_scr, qb, kw, hw)
        if gqa:
            kd, vd = scr[-2:]

            @pl.when(s_idx == 0)
            def _():
                _dup_kv_head(k_ref, kd, b // 2, L)
                _dup_kv_head(v_ref, vd, b // 2, L)
        else:
            kd, vd = k_ref, v_ref
        lane = lax.broadcasted_iota(jnp.int32, (qb, LANES), 1)
        lo = lane < HEAD_DIM
        if gqa:
            row = lax.broadcasted_iota(jnp.int32, (2 * qb, 1), 0)
            sk = jnp.where(row < qb, sink_ref[2 * b], sink_ref[2 * b + 1])

        def block(ql, col):
            qs = s_idx * Ls + ql
            ws, bias = _band_window(bias_scr, qs, L, qb, kw, hw)
            return ws, _dot_nt(_stack_heads(q_ref[pl.ds(ql, qb), col], lo), kd[pl.ds(ws, kw), col]) + bias

        def finish(ql, col, scores):
            ws, s = scores
            m = jnp.max(s, axis=-1, keepdims=True)
            if gqa:
                m = jnp.maximum(m, sk)
            p = jnp.exp(s - m)
            den = jnp.sum(p, axis=-1, keepdims=True)
            if gqa:
                den = den + jnp.exp(sk - m)
            o = _dot(p.astype(BF16), vd[pl.ds(ws, kw), col]) * (1.0 / den)
            o_ref[pl.ds(ql, qb), col] = _unstack_heads(o, lo).astype(o_ref.dtype)
            lse_ref[pl.ds(ql, qb), col] = _unstack_heads(m + jnp.log(den), lo)

        for lb in range(nlb):
            def step(n, carry, col=slice(lb * LANES, (lb + 1) * LANES)):
                _two_phase([(pl.multiple_of((n * unroll + u) * qb, qb), col) for u in range(unroll)], block, finish)
                return carry

            lax.fori_loop(0, Ls // (qb * unroll), step, 0)

    kv_map = (lambda r, b, s: (r, 0, 0)) if gqa else (lambda r, b, s: (r, 0, b))
    seg = pl.BlockSpec((None, Ls, nlb * LANES), lambda r, b, s: (r, s, b))
    return pl.pallas_call(
        body, name=name, grid=(NB, Cq // (nlb * LANES), L // Ls),
        in_specs=[pl.BlockSpec(memory_space=pltpu.SMEM), seg, pl.BlockSpec((None, L, nlb * LANES), kv_map),
                  pl.BlockSpec((None, L, nlb * LANES), kv_map)],
        out_specs=[seg, seg], out_shape=[_sds((NB, L, out_cols or Cq), out_dtype), _sds((NB, L, Cq), F32)],
        scratch_shapes=([pltpu.VMEM((3, 2 * qb, kw), F32)] if tables else []) + ([pltpu.VMEM((L, LANES), BF16)] * 2 if gqa else []),
        compiler_params=_params(("parallel", "parallel", "arbitrary")))(sink, q, k, v)


def _attn_bwd(q, k, v, do, lse, delta, sink, hw, gqa, name, qb=QB, blocks_per_step=8):
    NB, L, Cq = q.shape
    Ck = k.shape[2]
    Ls = min(L, 2048)
    kw = min(qb + 2 * hw, L)
    reps = kw // LANES
    nseg = L // Ls
    scale = HEAD_DIM ** -0.5
    tables = L >= qb + 2 * hw
    unroll = min(blocks_per_step, Ls // qb)
    nlb = 1 if (gqa or L > SHORT_SEQ) else Cq // LANES

    def body(sink_ref, q_ref, do_ref, lse_ref, dl_ref, k_ref, v_ref, dq_ref, dk_ref, dv_ref, dsk_ref, *scr):
        b, s_idx = pl.program_id(1), pl.program_id(2)
        lane = lax.broadcasted_iota(jnp.int32, (qb, LANES), 1)
        lo = lane < HEAD_DIM
        bias_scr = scr[0] if tables else None
        _band_setup(bias_scr, qb, kw, hw)
        if gqa:
            kd, vd, dk_acc, dv_acc, dsk_acc = scr[-5:]

            @pl.when(s_idx == 0)
            def _():
                _dup_kv_head(k_ref, kd, b // 2, L)
                _dup_kv_head(v_ref, vd, b // 2, L)
                dk_acc[...] = jnp.zeros_like(dk_acc)
                dv_acc[...] = jnp.zeros_like(dv_acc)
                dsk_acc[...] = jnp.zeros_like(dsk_acc)

            @pl.when((s_idx == 0) & (b == 0))
            def _():
                dk_ref[...] = jnp.zeros_like(dk_ref)
                dv_ref[...] = jnp.zeros_like(dv_ref)
        else:
            kd, vd = k_ref, v_ref
            dk_acc, dv_acc = scr[-2:]

            @pl.when(s_idx == 0)
            def _():
                dk_acc[...] = jnp.zeros_like(dk_acc)
                dv_acc[...] = jnp.zeros_like(dv_acc)

        def block(ql, col):
            qs = s_idx * Ls + ql
            ws, bias = _band_window(bias_scr, qs, L, qb, kw, hw)
            qv, dov = q_ref[pl.ds(ql, qb), col], do_ref[pl.ds(ql, qb), col]
            lse, dl = lse_ref[pl.ds(ql, qb), col], dl_ref[pl.ds(ql, qb), col].astype(F32)
            kv_, vv = kd[pl.ds(ws, kw), col], vd[pl.ds(ws, kw), col]
            q2, do2 = _stack_heads(qv, lo), _stack_heads(dov, lo)
            return ws, q2, do2, lse, dl, _dot_nt(q2, kv_) + bias, _dot_nt(do2, vv)

        def finish(ql, col, held):
            ws, q2, do2, lse, dl, s, dp = held
            lse_sw, dl_sw = pltpu.roll(lse, HEAD_DIM, axis=1), pltpu.roll(dl, HEAD_DIM, axis=1)
            lse2 = jnp.concatenate([jnp.where(lo, lse, lse_sw), jnp.where(lo, lse_sw, lse)], axis=0)
            dl2 = jnp.concatenate([jnp.where(lo, dl, dl_sw), jnp.where(lo, dl_sw, dl)], axis=0)
            p = jnp.exp(s - jnp.tile(lse2, (1, reps)))
            ds = (p * (dp - jnp.tile(dl2, (1, reps)))).astype(BF16)
            dq_ref[pl.ds(ql, qb), col] = (_unstack_heads(_dot(ds, kd[pl.ds(ws, kw), col]), lo) * scale).astype(dq_ref.dtype)
            both = _dot_tn(jnp.concatenate([ds, p.astype(BF16)], axis=1), jnp.concatenate([q2, do2], axis=1))
            dk_acc[pl.ds(ws, kw), col] += both[:kw, :LANES]
            dv_acc[pl.ds(ws, kw), col] += both[kw:, LANES:]
            if gqa:
                sk = jnp.where(lo, sink_ref[2 * b], sink_ref[2 * b + 1])
                dsk_acc[...] += -jnp.exp(sk - lse) * dl

        for lb in range(nlb):
            def step(n, carry, col=slice(lb * LANES, (lb + 1) * LANES)):
                _two_phase([(pl.multiple_of((n * unroll + u) * qb, qb), col) for u in range(unroll)], block, finish)
                return carry

            lax.fori_loop(0, Ls // (qb * unroll), step, 0)

        if gqa:
            @pl.when(s_idx == nseg - 1)
            def _():
                step_rows = min(L, 1024)
                for r0 in range(0, L, step_rows):
                    lanek = lax.broadcasted_iota(jnp.int32, (step_rows, LANES), 1)
                    mine = jnp.logical_xor(lanek < HEAD_DIM, (b // 2) == 1)
                    for acc, ref in ((dk_acc, dk_ref), (dv_acc, dv_ref)):
                        a = acc[r0:r0 + step_rows, :]
                        ref[r0:r0 + step_rows, :] += jnp.where(mine, a + pltpu.roll(a, HEAD_DIM, axis=1), 0.0)
                dsk_ref[...] = dsk_acc[...].reshape(qb // SUBLANES, SUBLANES, LANES).sum(axis=0)
        else:
            dsk_ref[...] = jnp.zeros_like(dsk_ref)

            @pl.when(s_idx == nseg - 1)
            def _():
                dk_ref[...] = dk_acc[...].astype(dk_ref.dtype)
                dv_ref[...] = dv_acc[...].astype(dv_ref.dtype)

    kv_map = (lambda r, b, s: (r, 0, 0)) if gqa else (lambda r, b, s: (r, 0, b))
    seg = pl.BlockSpec((None, Ls, nlb * LANES), lambda r, b, s: (r, s, b))
    full = pl.BlockSpec((None, L, nlb * LANES), kv_map)
    scratch = [pltpu.VMEM((3, 2 * qb, kw), F32)] if tables else []
    if gqa:
        scratch += [pltpu.VMEM((L, LANES), BF16)] * 2 + [pltpu.VMEM((L, LANES), F32)] * 2 + [pltpu.VMEM((qb, LANES), F32)]
    else:
        scratch += [pltpu.VMEM((L, nlb * LANES), F32)] * 2
    kv_dtype = F32 if gqa else BF16
    return pl.pallas_call(
        body, name=name, grid=(NB, Cq // (nlb * LANES), nseg),
        in_specs=[pl.BlockSpec(memory_space=pltpu.SMEM), seg, seg, seg, seg, full, full],
        out_specs=[seg, full, full, pl.BlockSpec((None, None, SUBLANES, LANES), lambda r, b, s: (r, b, 0, 0))],
        out_shape=[_sds((NB, L, Cq), BF16), _sds((NB, L, Ck), kv_dtype), _sds((NB, L, Ck), kv_dtype),
                   _sds((NB, Cq // LANES, SUBLANES, LANES), F32)],
        scratch_shapes=scratch,
        compiler_params=_params(("arbitrary", "arbitrary", "arbitrary")))(sink, q, do, lse, delta, k, v)


def _dilated_fwd(cat, qkv, hw, tile=2048):
    T = cat.shape[0]
    dils = sorted(qkv)
    nbb, na = B_W // LANES, A_Q_W // LANES
    qb, kw = QB, QB + 2 * hw
    rows_merge = 256
    assert T % tile == 0 and all(tile % (d * qb) == 0 and T // d >= kw for d in dils)

    def body(cat_in, *refs):
        qkv_refs = {d: refs[3 * j:3 * j + 3] for j, d in enumerate(dils)}
        cat_ref, lg_refs = refs[3 * len(dils)], refs[3 * len(dils) + 1:4 * len(dils) + 1]
        o_scr, l_scr, bias_scr = refs[4 * len(dils) + 1:]
        i = pl.program_id(1)
        _band_setup(bias_scr, qb, kw, hw)
        lane = lax.broadcasted_iota(jnp.int32, (qb, LANES), 1)
        lo = lane < HEAD_DIM
        for pi, d in enumerate(dils):
            q_ref, k_ref, v_ref = qkv_refs[d]
            L, rows = T // d, tile // d

            def place(r, n, d=d):
                return pl.ds(r + d * n * qb, qb, stride=d) if d > 1 else pl.ds(n * qb, qb)

            def scores(r, n, q_ref=q_ref, k_ref=k_ref, L=L, rows=rows):
                ws, bias = _band_window(bias_scr, i * rows + n * qb, L, qb, kw, hw)
                return ws, _dot_nt(_stack_heads(q_ref[r, n * qb:(n + 1) * qb, :], lo), k_ref[r, pl.ds(ws, kw), :]) + bias

            def finish(r, n, held, v_ref=v_ref, pi=pi, place=place):
                ws, s = held
                m = jnp.max(s, axis=-1, keepdims=True)
                p = jnp.exp(s - m)
                den = jnp.sum(p, axis=-1, keepdims=True)
                o = _dot(p.astype(BF16), v_ref[r, pl.ds(ws, kw), :]) * (1.0 / den)
                o_scr[pi, place(r, n), :] = _unstack_heads(o, lo)
                l_scr[pi, place(r, n), :] = _unstack_heads(m + jnp.log(den), lo)

            blocks = [(r, n) for r in range(d) for n in range(rows // qb)]
            for g0 in range(0, len(blocks), 8):
                _two_phase(blocks[g0:g0 + 8], scores, finish)

        for r0 in range(0, tile, rows_merge):
            rs = slice(r0, r0 + rows_merge)
            ls_ = [l_scr[pi, rs, :] for pi in range(len(dils))]
            m = ls_[0]
            for l in ls_[1:]:
                m = jnp.maximum(m, l)
            es = [jnp.exp(l - m) for l in ls_]
            den, out = es[0], es[0] * o_scr[0, rs, :]
            for pi in range(1, len(dils)):
                den = den + es[pi]
                out = out + es[pi] * o_scr[pi, rs, :]
            cat_ref[rs, :] = (out * (1.0 / den)).astype(BF16)
            l_scr[0, rs, :] = m + jnp.log(den)
        for lg_ref, d in zip(lg_refs, dils):
            for r in range(d):
                lg_ref[r] = l_scr[0, pl.ds(r, tile // d, stride=d), :] if d > 1 else l_scr[0]

    in_specs = [pl.BlockSpec(memory_space=pl.ANY)]
    operands = [cat]
    for d in dils:
        in_specs += [pl.BlockSpec((d, tile // d, LANES), lambda b, i: (0, i, b))] + [pl.BlockSpec((d, T // d, LANES), lambda b, i: (0, 0, b))] * 2
        operands += list(qkv[d])
    return pl.pallas_call(
        body, name="dilated_fwd", grid=(nbb, T // tile), in_specs=in_specs,
        out_specs=[pl.BlockSpec((tile, LANES), lambda b, i: (i, na + b))] + [pl.BlockSpec((d, tile // d, LANES), lambda b, i: (0, i, b)) for d in dils],
        out_shape=[_sds(cat.shape, BF16)] + [_sds((d, T // d, B_W), F32) for d in dils],
        input_output_aliases={0: 0},
        scratch_shapes=[pltpu.VMEM((len(dils), tile, LANES), F32)] * 2 + [pltpu.VMEM((3, 2 * qb, kw), F32)],
        compiler_params=_params(("parallel", "arbitrary")))(*operands)


def _out_proj(x, cat, w_out, tm=512):
    T, D = x.shape

    def body(x_ref, c_ref, w_ref, o_ref):
        o_ref[...] = x_ref[...] + _dot(c_ref[...], w_ref[...])

    row = lambda w: pl.BlockSpec((tm, w), lambda i: (i, 0))
    return pl.pallas_call(
        body, name="out_proj", grid=(T // tm,), in_specs=[row(D), row(cat.shape[1]), pl.BlockSpec(w_out.shape, lambda i: (0, 0))],
        out_specs=row(D), out_shape=_sds((T, D), F32), compiler_params=_params(("parallel",)))(x, cat, w_out)


def _dcat(dx, w_out, cat, tm=512, dep=None):
    T, D = dx.shape
    C = cat.shape[1]
    nba, nbb = A_Q_W // LANES, B_W // LANES
    nt = T // tm

    def body(dx_ref, w_ref, cat_ref, doa_ref, dla_ref, dob1_ref, dlb1_ref, dob4_ref, dlb4_ref, dob16_ref, dlb16_ref, dw_ref, dwb_ref,
             sdo, sdl):
        @pl.when(pl.program_id(0) == 0)
        def _():
            dw_ref[...] = jnp.zeros_like(dw_ref)

        dxb = dx_ref[...].astype(BF16)
        dc = _dot_nt(dxb, w_ref[...])
        dw_ref[...] += _dot_tn(cat_ref[...], dxb)

        @pl.when(pl.program_id(0) == nt - 1)
        def _():
            dwb_ref[...] = dw_ref[...].astype(BF16)

        ri =lax.broadcasted_iota(jnp.int32, (LANES, LANES), 0)
        ci = lax.broadcasted_iota(jnp.int32, (LANES, LANES), 1)
        same_head = ((ri // HEAD_DIM) == (ci // HEAD_DIM)).astype(BF16)
        for cb in range(C // LANES):
            cols = slice(cb * LANES, (cb + 1) * LANES)
            blk = dc[:, cols]
            prod = blk * cat_ref[:, cols].astype(F32)
            hi = prod.astype(BF16)
            lo_ = (prod - hi.astype(F32)).astype(BF16)
            dl = _dot(hi, same_head) + _dot(lo_, same_head)
            if cb < nba:
                doa_ref[:, cols] = blk.astype(BF16)
                dla_ref[:, cols] = dl.astype(BF16)
            else:
                bcols = slice((cb - nba) * LANES, (cb - nba + 1) * LANES)
                dob1_ref[:, bcols] = blk.astype(BF16)
                dlb1_ref[:, bcols] = dl.astype(BF16)
                sdo[cb - nba] = blk
                sdl[cb - nba] = dl
        _deinterleave(sdo, dob4_ref, 4, tm, nbb)
        _deinterleave(sdl, dlb4_ref, 4, tm, nbb)
        _deinterleave(sdo, dob16_ref, 16, tm, nbb)
        _deinterleave(sdl, dlb16_ref, 16, tm, nbb)

    row = lambda w: pl.BlockSpec((tm, w), lambda i: (i, 0))
    perm = lambda d: pl.BlockSpec((d, tm // d, B_W), lambda i: (0, i, 0))
    whole = pl.BlockSpec((C, D), lambda i: (0, 0))
    body, dep_spec, dep_arg = _ordered(body, 3, dep)
    outs = pl.pallas_call(
        body, name="dcat", grid=(nt,), in_specs=[row(D), whole, row(C)] + dep_spec,
        out_specs=[row(A_Q_W), row(A_Q_W), row(B_W), row(B_W), perm(4), perm(4), perm(16), perm(16), whole, whole],
        out_shape=[_sds((T, A_Q_W), BF16), _sds((T, A_Q_W), BF16), _sds((T, B_W), BF16), _sds((T, B_W), BF16),
                   _sds((4, T // 4, B_W), BF16), _sds((4, T // 4, B_W), BF16), _sds((16, T // 16, B_W), BF16), _sds((16, T // 16, B_W), BF16),
                   _sds((C, D), F32), _sds((C, D), BF16)],
        scratch_shapes=[pltpu.VMEM((nbb, tm, LANES), F32)] * 2, compiler_params=_params(("arbitrary",)))(dx, w_out, cat, *dep_arg)
    return (*outs[:8], (outs[8], outs[9]))


def _mixer_in_bwd(dqa, dka, dva, b1, b4, b16, cos, sin, w_in, x, g, dres, tm=512):
    T, D = x.shape
    nbb = B_W // LANES
    width = A_Q_W + 2 * A_KV_W + 3 * B_W

    def body(dqa_ref, dka_ref, dva_ref, q1, k1, v1, q4, k4, v4, q16, k16, v16, c_ref, s_ref, w_ref, x_ref, g_ref, dr_ref,
             o_ref, dx_ref, dg_ref, scr):
        @pl.when(pl.program_id(0) == 0)
        def _():
            dg_ref[...] = jnp.zeros_like(dg_ref)

        cs, sn = c_ref[...], s_ref[...]
        dh = []

        def unrope(t):
            return t * cs + _swap32(t * sn)

        def project(c0, c1):
            t = _dot(o_ref[:, c0:c1], w_ref[c0:c1, :])
            dh[:] = [t if not dh else dh[0] + t]

        col = 0
        for ref, rope in ((dqa_ref, True), (dka_ref, True), (dva_ref, False)):
            for cb in range(ref.shape[1] // LANES):
                t = ref[:, cb * LANES:(cb + 1) * LANES].astype(F32)
                o_ref[:, col:col + LANES] = (unrope(t) if rope else t).astype(BF16)
                col += LANES
        project(0, col)
        for which, (r1, r4, r16, rope) in enumerate(((q1, q4, q16, True), (k1, k4, k16, True), (v1, v4, v16, False))):
            _interleave(r4, scr.at[0], 4, tm, nbb)
            _interleave(r16, scr.at[1], 16, tm, nbb)
            for cb in range(nbb):
                t = r1[:, cb * LANES:(cb + 1) * LANES].astype(F32) + scr[0, cb] + scr[1, cb]
                o_ref[:, col:col + LANES] = (unrope(t) if rope else t).astype(BF16)
                col += LANES
            project(col - B_W, col)
        dxn, dg = _rms_bwd(dh[0], x_ref[...], g_ref[...])
        dg_ref[...] += dg
        dx_ref[...] = dr_ref[...] + dxn

    row = lambda w: pl.BlockSpec((tm, w), lambda i: (i, 0))
    perm = lambda d: pl.BlockSpec((d, tm // d, B_W), lambda i: (0, i, 0))
    return pl.pallas_call(
        body, name="mixer_in_bwd", grid=(T // tm,),
        in_specs=[row(A_Q_W), row(A_KV_W), row(A_KV_W)] + [row(B_W)] * 3 + [perm(4)] * 3 + [perm(16)] * 3 + [row(LANES), row(LANES)]
        + [_resident(w_in.shape), row(D), pl.BlockSpec((1, D), lambda i: (0, 0)), row(D)],
        out_specs=[row(width), row(D), pl.BlockSpec((SUBLANES, D), lambda i: (0, 0))],
        out_shape=[_sds((T, width), BF16), _sds((T, D), F32), _sds((SUBLANES, D), F32)],
        scratch_shapes=[pltpu.VMEM((2, nbb, tm, LANES), F32)],
        compiler_params=_params(("arbitrary",)))(dqa, dka, dva, *b1, *b4, *b16, cos, sin, w_in, x, g, dres)


def _grad_push_plan(n):
    def plan(refs):
        x, y, c = _mesh_pos()
        return [(refs[k].at[chip], refs[n + k].at[rel], dev) for k in range(n) for rel, (dev, chip) in enumerate(_chip_peers(x, y, c))]
    return plan


def _sum_own(me_arr, g, landed, name):
    ns, R, C = g.shape
    tr = R // 2 if (R // 2) % 16 == 0 else R

    def body(me_ref, g_ref, x_ref, o_ref):
        acc = g_ref[...]
        for rel in range(ns - 1):
            acc = acc + x_ref[rel].astype(F32)
        o_ref[...] = acc

    grid_spec = pltpu.PrefetchScalarGridSpec(
        num_scalar_prefetch=1, grid=(R // tr,),
        in_specs=[pl.BlockSpec((None, tr, C), lambda t, me: (me[0], t, 0)), pl.BlockSpec((ns - 1, tr, C), lambda t, me: (0, t, 0))],
        out_specs=pl.BlockSpec((tr, C), lambda t, me: (t, 0)))
    return pl.pallas_call(body, name=name, grid_spec=grid_spec, out_shape=_sds((R, C), F32),
                          compiler_params=_params(("parallel",)))(me_arr, g, landed)


def _swap_plan(n):
    def plan(refs):
        x, y, c = _mesh_pos()
        return [(refs[k], refs[n + k], (x, y, 1 - c)) for k in range(n)]
    return plan


def _allreduce_small(v, dep):
    rows, W = v.shape

    def body(v_ref, o_ref, buf, send, recv):
        x, y, c = _mesh_pos()
        me = 4 * x + 2 * y + c
        cps = []
        for m in range(1, N_DEV):
            dev = (x ^ (m >> 2), y ^ ((m >> 1) & 1), c ^ (m & 1))
            cp = pltpu.make_async_remote_copy(src_ref=v_ref, dst_ref=buf.at[me], send_sem=send.at[m - 1], recv_sem=recv.at[m - 1],
                                              device_id=dev, device_id_type=MESH)
            cp.start()
            cps.append(cp)
        for m in range(1, N_DEV):
            pltpu.make_async_remote_copy(src_ref=v_ref, dst_ref=buf.at[me ^ m], send_sem=send.at[m - 1], recv_sem=recv.at[m - 1],
                                         device_id=(x, y, c), device_id_type=MESH).wait_recv()
        for cp in cps:
            cp.wait_send()
        buf[me] = v_ref[...]
        acc = buf[0]
        for i in range(1, N_DEV):
            acc = acc + buf[i]
        o_ref[...] = acc

    body, dep_spec, dep_arg = _ordered(body, 1, dep)
    return pl.pallas_call(
        body, name="allreduce_small", out_shape=_sds((rows, W), F32), in_specs=[pl.BlockSpec(memory_space=pltpu.VMEM)] + dep_spec,
        scratch_shapes=[pltpu.VMEM((N_DEV, rows, W), F32), pltpu.SemaphoreType.DMA((N_DEV - 1,)), pltpu.SemaphoreType.DMA((N_DEV - 1,))],
        compiler_params=_params())(v, *dep_arg)


def _adamw_math(w, g, m, v):
    c1 = 1.0 / (1.0 - ADAM_B1 ** ADAM_STEP)
    c2 = 1.0 / (1.0 - ADAM_B2 ** ADAM_STEP)
    nm = ADAM_B1 * m + (1.0 - ADAM_B1) * g
    nv = ADAM_B2 * v + (1.0 - ADAM_B2) * (g * g)
    return -ADAM_LR * ((nm * c1) / (jnp.sqrt(nv * c2) + ADAM_EPS) + ADAM_WD * w), nm, nv


def _adamw_small(rows, params):
    n = len(params)
    n_sink = params[-1][0].shape[1]

    def body(rows_ref, *refs):
        ins, outs = refs[:3 * n], refs[3 * n:]
        for j in range(n):
            g = rows_ref[j:j + 1, 0:n_sink] if j == n - 1 else rows_ref[j:j + 1, :]
            d, nm, nv = _adamw_math(ins[3 * j][...], g, ins[3 * j + 1][...], ins[3 * j + 2][...])
            for ref, val in zip(outs[4 * j:4 * j + 4], (g, d, nm, nv)):
                ref[...] = val
        outs[-1][...] = rows_ref[n - 1:n, n_sink:n_sink + 1]

    flat = [a for p in params for a in p]
    outs = pl.pallas_call(body, name="adamw_small", out_shape=[_sds(p[0].shape, F32) for p in params for _ in range(4)] + [_sds((1, 1), F32)],
                          compiler_params=_params())(rows, *flat)
    return [outs[4 * j:4 * j + 4] for j in range(n)], outs[-1]


def _adamw(w, gp, gq, m, v, name):
    R, C = w.shape
    tr = R // 2 if (R // 2) % SUBLANES == 0 else R

    def body(w_ref, gp_ref, gq_ref, m_ref, v_ref, g_ref, d_ref, nm_ref, nv_ref):
        gv = gp_ref[...] + gq_ref[...]
        g_ref[...] = gv
        d_ref[...], nm_ref[...], nv_ref[...] = _adamw_math(w_ref[...], gv, m_ref[...], v_ref[...])

    blk = pl.BlockSpec((tr, C), lambda t: (t, 0))
    return pl.pallas_call(body, name=name, grid=(R // tr,), in_specs=[blk] * 5, out_specs=[blk] * 4,
                          out_shape=[_sds((R, C), F32)] * 4, compiler_params=_params(("parallel",)))(w, gp, gq, m, v)


def _rope(positions, after):
    inv_freq = 1.0 / (ROPE_THETA ** (jnp.arange(0, HEAD_DIM, 2, dtype=F32) / HEAD_DIM))
    inv_freq = jnp.tile(inv_freq, LANES // (HEAD_DIM // 2)).reshape(1, LANES) + after[0, 0]
    return _rope_tables(positions.reshape(-1, 1), inv_freq)


def _local_step(x, rope, target, norms, a_sink, comm):
    T, D = x.shape
    g1, gm, g2, gf = norms
    cos, sin = rope
    no_sink = jnp.zeros((2 * (B_W // LANES),), F32)
    W = {k: comm.weight(k, x) for k in ("wg1", "wu1")}

    h1, gate1, up1, act1 = _ffn_up(x, g1, W["wg1"], W["wu1"], "ffn1_up", dep=comm.dep())
    W["wd1"] = comm.weight("wd1", act1)
    x1 = _ffn_down(x, act1, W["wd1"], "ffn1_down")
    W["w_in"] = comm.weight("w_in", x1)
    (h2, aq, ak, av, bq1, bk1, bv1, bq4, bk4, bv4, bq16, bk16, bv16) = _proj_rope(x1, gm, W["w_in"], cos, sin)
    cat, a_lse = _attn_fwd(aq[None], ak[None], av[None], a_sink, A_HALF_WINDOW, True, BF16, "attn_a_fwd", qb=2 * QB, blocks_per_step=4,
                           out_cols=A_Q_W + B_W)
    bqs = {1: (bq1[None], bk1[None], bv1[None]), 4: (bq4, bk4, bv4), 16: (bq16, bk16, bv16)}
    (b_hw,) = {w // (2 * d) for w, d in B_PATTERNS}
    cat, lg1, lg4, lg16 = _dilated_fwd(cat[0], bqs, b_hw)
    lg1 = lg1[0]
    W["w_out"] = comm.weight("w_out", cat)
    x2 = _out_proj(x1, cat, W["w_out"])
    for k in ("wg2", "wu2", "wd2"):
        W[k] = comm.weight(k, x2)
    dx3, h3, gate2, up2, act2, dgf, loss8 = _ffn_loss(x2, g2, W["wg2"], W["wu2"], W["wd2"], gf, target, "ffn2_fwd")

    dx2, dff2, dgate2, dup2, dg2 = _ffn_dx(dx3, x2, g2, gate2, up2, W["wg2"], W["wu2"], W["wd2"], "ffn2_dx")
    fb = gate2.shape[1] // 2
    dwg2 = _tn(dgate2, h3, fb, "ffn2_dw_gate")
    dwu2 = _tn(dup2, h3, fb, "ffn2_dw_up")
    dwd2 = _tn(act2, dff2, fb, "ffn2_dw_down")
    comm.ready(dict(wg2=dwg2, wu2=dwu2, wd2=dwd2), dwd2[0])

    doa, dla, dob1, dlb1, dob4, dlb4, dob16, dlb16, dw_out = _dcat(dx2, W["w_out"], cat, dep=comm.dep())
    dqa, dka, dva, dsk = _attn_bwd(aq[None], ak[None], av[None], doa[None], a_lse, dla[None], a_sink, A_HALF_WINDOW, True, "attn_a_bwd")
    bwd_in = {1: (dob1[None], lg1[None], dlb1[None]), 4: (dob4, lg4, dlb4), 16: (dob16, lg16, dlb16)}
    bg = {}
    for w, d in B_PATTERNS:
        q_, k_, v_ = bqs[d]
        do_, l_, dl_ = bwd_in[d]
        bg[d] = _attn_bwd(q_, k_, v_, do_, l_, dl_, no_sink, w // (2 * d), False, f"attn_b{d}_bwd")[:3]
    dproj, dx1, dgm = _mixer_in_bwd(dqa[0], dka[0], dva[0], [t[0] for t in bg[1]], bg[4], bg[16], cos, sin, W["w_in"], x1, gm, dx2)
    dw_in = _tn(dproj, h2, dproj.shape[1] // 2, "w_in_dw")
    comm.ready(dict(w_in=dw_in, w_out=dw_out), dw_in[0])

    dx0, dff1, dgate1, dup1, dg1 = _ffn_dx(dx1, x, g1, gate1, up1, W["wg1"], W["wu1"], W["wd1"], "ffn1_dx", dep=comm.dep())
    comm.settle(2, dx0)
    dwd1 = _tn(act1, dff1, fb, "ffn1_dw_down", dep=comm.dep())
    comm.ready(dict(wd1=dwd1), dwd1[0])
    dwg1 = _tn(dgate1, h1, fb, "ffn1_dw_gate", dep=comm.dep())
    comm.ready(dict(wg1=dwg1), dwg1[0])
    dwu1 = _tn(dup1, h1, fb, "ffn1_dw_up", dep=comm.dep())
    comm.ready(dict(wu1=dwu1), dwu1[0])

    dsink = dsk[0, :, :, ::HEAD_DIM].sum(axis=1).reshape(-1)
    small = dict(g1=dg1.sum(axis=0), gm=dgm.sum(axis=0), g2=dg2.sum(axis=0), gf=dgf.sum(axis=0), sink=dsink, loss=loss8[0, 0])
    return dx0, small


BIG = ("wg1", "wu1", "wd1", "w_in", "w_out", "wg2", "wu2", "wd2")
GATHER_GROUPS = (("wd1",), ("w_in",), ("w_out",), ("wg2", "wu2", "wd2"))


class _Comm:
    def __init__(self, shards, meanwhile):
        x, y, c = _mesh_pos()
        self.me = (2 * x + y).astype(jnp.int32).reshape(1)
        self.shards = shards
        self.token = None
        self.waiting = {}
        self.groups = []
        self.swaps = []
        first = ("wg1", "wu1")
        fulls ={k: _cast_place(self.me, shards[k], f"cast_{k}") for k in first}
        plan = _neighbour_plan([fulls[k].shape for k in first])
        send, recv, bufs, tok = _push_start("gather_first_start", [fulls[k] for k in first], 2 * len(first), plan, self.me)
        self.side = meanwhile(tok)
        fulls.update({k: _cast_place(self.me, shards[k], f"cast_{k}") for k in BIG if k not in first})
        bufs = _push_wait("gather_first_wait", send, recv, bufs, plan, [fulls[k] for k in BIG if k not in first] + list(self.side))
        self.full = dict(zip(first, _gather_forward(bufs)))
        rest = [k for names in GATHER_GROUPS for k in names]
        send, recv, bufs, self.token = _push_start("gather_rest_start", [fulls[k] for k in rest], 3 * len(rest), _gather_plan(len(rest)),
                                                   self.full[first[-1]])
        self.rest = dict(zip(rest, bufs))
        for gi, names in enumerate(GATHER_GROUPS):
            for k in names:
                self.waiting[k] = (gi, names, send, recv, 3 * rest.index(names[0]))

    def dep(self):
        return self.token

    def weight(self, name, after):
        if name in self.waiting:
            gi, names, send, recv, first = self.waiting[name]
            bufs = [self.rest[k] for k in names]
            for k, buf in zip(names, _push_wait(f"gather_wait_{gi}", send, recv, bufs, _gather_plan(len(names)), after, first)):
                self.full[k] = buf
                del self.waiting[k]
        full = self.full[name]
        return full.reshape(N_CHIPS * full.shape[1], full.shape[2])

    def ready(self, grads, after):
        names = list(grads)
        f32s, b16s = [], []
        for k in names:
            gf, gb = grads[k]
            f32s.append(gf.reshape((N_CHIPS,) + self.shards[k].shape))
            b16s.append(gb.reshape((N_CHIPS,) + self.shards[k].shape))
        n = len(names)
        lands = [lax.empty((N_CHIPS - 1,) + self.shards[k].shape, BF16) for k in names]
        plan = _grad_push_plan(n)
        send, recv, bufs, self.token = _push_start(f"grad_start_{names[0]}", b16s + lands, 3 * n, plan, after)
        self.groups.append((names, f32s, send, recv, bufs, plan))

    def settle(self, count, after):
        batch, self.groups = self.groups[:count], self.groups[count:]
        names_b, mine_b = [], []
        for names, f32s, send, recv, bufs, plan in batch:
            n = len(names)
            bufs = _push_wait(f"grad_wait_{names[0]}", send, recv, bufs, plan, mine_b[-1] if mine_b else after)
            mine_b += [_sum_own(self.me, f32s[i], bufs[n + i], f"sum_{k}") for i, k in enumerate(names)]
            names_b += names
        lands = [lax.empty(p.shape, F32) for p in mine_b]
        n = len(names_b)
        send2, recv2, both, self.token = _push_start(f"swap_start_{names_b[0]}", mine_b + lands, n, _swap_plan(n), after)
        self.swaps.append((names_b, send2, recv2, both))

    def partials(self, after):
        names_b, send2, recv2, both = self.swaps.pop(0)
        n = len(names_b)
        both = _push_wait(f"swap_wait_{names_b[0]}", send2, recv2, both, _swap_plan(n), after)
        return {k: (both[i], both[n + i]) for i, k in enumerate(names_b)}


def kernel(x, positions, norm_ffn1, w_gate1, w_up1, w_down1, norm_mix, w_in, a_sink, w_out, norm_ffn2, w_gate2, w_up2, w_down2, norm_final, loss_target, m_norm_ffn1, m_w_gate1, m_w_up1, m_w_down1, m_norm_mix, m_w_in, m_a_sink, m_w_out, m_norm_ffn2, m_w_gate2, m_w_up2, m_w_down2, m_norm_final, v_norm_ffn1, v_w_gate1, v_w_up1, v_w_down1, v_norm_mix, v_w_in, v_a_sink, v_w_out, v_norm_ffn2, v_w_gate2, v_w_up2, v_w_down2, v_norm_final):
    T, D = x.shape[1], x.shape[2]
    flip = ("wg1", "wu1", "w_in", "wg2", "wu2")

    def rows(k, a):
        return a[0].T if k in flip else a[0]

    given = dict(wg1=(w_gate1, m_w_gate1, v_w_gate1), wu1=(w_up1, m_w_up1, v_w_up1), wd1=(w_down1, m_w_down1, v_w_down1),
                 w_in=(w_in, m_w_in, v_w_in), w_out=(w_out, m_w_out, v_w_out), wg2=(w_gate2, m_w_gate2, v_w_gate2),
                 wu2=(w_up2, m_w_up2, v_w_up2), wd2=(w_down2, m_w_down2, v_w_down2))
    shards = {k: rows(k, given[k][0]) for k in BIG}

    comm = _Comm(shards, lambda tok: _rope(positions[0], tok))

    norms = (norm_ffn1, norm_mix, norm_ffn2, norm_final.reshape(1, D))
    grad_x, small = _local_step(x[0], comm.side, loss_target[0], norms, a_sink[0], comm)

    upd = {}

    def update(partial):
        for k in partial:
            outs = _adamw(shards[k], partial[k][0], partial[k][1], rows(k, given[k][1]), rows(k, given[k][2]), f"adamw_{k}")
            upd[k] = tuple((a.T if k in flip else a)[None] for a in outs)
        return outs[0]

    last = update(comm.partials(comm.dep()))
    comm.settle(2, last)

    def pad_row(a):
        a = a.reshape(-1)
        return jnp.pad(a, (0, D - a.shape[0]))

    row4 = pad_row(jnp.concatenate([small["sink"], small["loss"].reshape(1)]))
    vec = jnp.stack([small["g1"], small["gm"], small["g2"], small["gf"], row4] + [jnp.zeros((D,), F32)] * 3, axis=0)
    red = _allreduce_small(vec, comm.dep())
    comm.settle(1, red)
    last = update(comm.partials(comm.dep()))
    update(comm.partials(last))
    as_row = lambda a: a.reshape(1, -1)
    sm, loss = _adamw_small(red, [tuple(as_row(a) for a in p) for p in (
        (norm_ffn1, m_norm_ffn1, v_norm_ffn1), (norm_mix, m_norm_mix, v_norm_mix), (norm_ffn2, m_norm_ffn2, v_norm_ffn2),
        (norm_final, m_norm_final, v_norm_final), (a_sink, m_a_sink, v_a_sink))])
    sm[3] = [a.reshape(D) for a in sm[3]]

    def ordered(i):
        return [sm[0][i], upd["wg1"][i], upd["wu1"][i], upd["wd1"][i], sm[1][i], upd["w_in"][i], sm[4][i], upd["w_out"][i], sm[2][i],
                upd["wg2"][i], upd["wu2"][i], upd["wd2"][i], sm[3][i]]

    return (loss.reshape(()), grad_x[None], *ordered(0), *ordered(1), *ordered(2), *ordered(3))
```
